```python
import math
import jax, jax.numpy as jnp
from jax import lax
import numpy as np

D_MODEL = 2048
BATCH = 2
SEQ = 4096
DEPTH = 2
DEC_BATCH = 16
DEC_SEQ = 32
PAST_LEN = 4096

CHUNK = 64
NORM_EPS = 1e-6
A_HEADS = 4
A_DK = 128
A_DV = 256
A_GATE_RANK = 16
A_GATE_TAU = 16.0
B_HEADS = 8
B_DK = 128
B_DV = 128
B_CONV = 4
C_GROUP = 16
C_GROUPS = 64
C_STATE = 64
D_HEADS = 4
D_DK = 128
D_DV = 256

A_KW = A_HEADS * A_DK
A_VW = A_HEADS * A_DV
B_KW = B_HEADS * B_DK
B_VW = B_HEADS * B_DV
B_QKV = 2 * B_KW + B_VW
C_W = C_GROUPS * C_GROUP
D_KW = D_HEADS * D_DK
D_VW = D_HEADS * D_DV
MIX_AB = A_VW + B_VW
MIX_CD = C_W + D_VW
AB_SPLIT = (A_KW, A_KW, A_VW, A_VW, A_GATE_RANK, B_QKV, B_VW, B_HEADS, B_HEADS)
CD_SPLIT = (C_W, C_W, D_KW, D_KW, D_VW, D_VW, D_VW, D_HEADS, D_HEADS)
IN_AB = sum(AB_SPLIT)
IN_CD = sum(CD_SPLIT)

kernel_name = 'hybrid_gla_gdn_s5_mlstm_stream_step'


def _split(t, sizes):
    return jnp.split(t, [int(i) for i in np.cumsum(sizes)[:-1]], axis=-1)


def _rmsnorm(x, g):
    x32 = x.astype(jnp.float32)
    y = x32 * lax.rsqrt(jnp.mean(x32 * x32, axis=-1, keepdims=True) + NORM_EPS)
    return (y * g.astype(jnp.float32)).astype(x.dtype)


def _head_rms(o, g):
    o = o * lax.rsqrt(jnp.mean(o * o, axis=-1, keepdims=True) + NORM_EPS) * g
    return o.reshape(o.shape[0], o.shape[1], -1)


def _head_layernorm(o, g):
    mu = jnp.mean(o, axis=-1, keepdims=True)
    oc = o - mu
    o = oc * lax.rsqrt(jnp.mean(oc * oc, axis=-1, keepdims=True) + NORM_EPS) * g
    return o.reshape(o.shape[0], o.shape[1], -1)


def _l2norm(t):
    return t * lax.rsqrt(jnp.sum(t * t, axis=-1, keepdims=True) + NORM_EPS)


def _chunk(t, c):
    b, l, h, d = t.shape
    return t.reshape(b, l // c, c, h, d).transpose(0, 3, 1, 2, 4)


def _unchunk(t):
    b, h, n, c, d = t.shape
    return t.transpose(0, 2, 3, 1, 4).reshape(b, n * c, h, d)


def _gla(q, k, v, log_a, s0):
    c = min(CHUNK, q.shape[1])
    q, k, v, log_a = (_chunk(t, c) for t in (q, k, v, log_a))
    b = jnp.cumsum(log_a, axis=-2)
    b_last = b[..., -1, :]
    q_dec = q * jnp.exp(b)
    causal = jnp.tril(jnp.ones((c, c), dtype=bool))
    scores = jnp.einsum('bhnid,bhnjd->bhnij', q_dec, k * jnp.exp(-b))
    o_intra = jnp.einsum('bhnij,bhnjv->bhniv', jnp.where(causal, scores, 0.0), v)
    kv = jnp.einsum('bhnjd,bhnjv->bhndv', k * jnp.exp(b_last[..., None, :] - b), v)

    def step(s, inp):
        decay, kv_n = inp
        return decay[..., None] * s + kv_n, s

    s_fin, s_start = lax.scan(step, s0, (jnp.moveaxis(jnp.exp(b_last), 2, 0), jnp.moveaxis(kv, 2, 0)))
    o = o_intra + jnp.einsum('bhnid,bhndv->bhniv', q_dec, jnp.moveaxis(s_start, 0, 2))
    return _unchunk(o), s_fin


def _gated_delta(q, k, v, g, beta, s0):
    c = min(CHUNK, q.shape[1])
    dv = v.shape[-1]
    q, k, v = (_chunk(t, c) for t in (q, k, v))
    g, beta = (_chunk(t[..., None], c)[..., 0] for t in (g, beta))
    g = jnp.cumsum(g, axis=-1)
    causal = jnp.tril(jnp.ones((c, c), dtype=bool))
    strict = jnp.tril(jnp.ones((c, c), dtype=bool), k=-1)
    decay = jnp.exp(jnp.where(causal, g[..., :, None] - g[..., None, :], -jnp.inf))
    k_beta = k * beta[..., None]
    lower = jnp.where(strict, jnp.einsum('bhnid,bhnjd->bhnij', k_beta, k) * decay, 0.0)
    rhs = jnp.concatenate([v * beta[..., None], k_beta * jnp.exp(g)[..., None]], axis=-1)
    sol = lax.linalg.triangular_solve(lower + jnp.eye(c, dtype=lower.dtype), rhs,
                                      left_side=True, lower=True, unit_diagonal=True)
    u, w = sol[..., :dv], sol[..., dv:]
    qk = jnp.where(causal, jnp.einsum('bhnid,bhnjd->bhnij', q, k) * decay, 0.0)
    q_g = q * jnp.exp(g)[..., None]
    g_last = g[..., -1]
    k_g = k * jnp.exp(g_last[..., None] - g)[..., None]

    def step(s, inp):
        u_n, w_n, qg_n, qk_n, kg_n, gl_n = inp
        v_new = u_n - jnp.einsum('bhcd,bhdv->bhcv', w_n, s)
        o_n = jnp.einsum('bhcd,bhdv->bhcv', qg_n, s) + jnp.einsum('bhij,bhjv->bhiv', qk_n, v_new)
        s_new = jnp.exp(gl_n)[..., None, None] * s + jnp.einsum('bhcd,bhcv->bhdv', kg_n, v_new)
        return s_new, o_n

    xs = tuple(jnp.moveaxis(t, 2, 0) for t in (u, w, q_g, qk, k_g, g_last))
    s_fin, o = lax.scan(step, s0, xs)
    return _unchunk(jnp.moveaxis(o, 0, 2)), s_fin


def _s5(u, lam_re, lam_im, log_dt, b_re, b_im, c_re, c_im, d_skip, x0_re, x0_im):
    f32 = jnp.float32
    lam_re, lam_im, b_re, b_im, c_re, c_im, d_skip = (t.astype(f32) for t in (lam_re, lam_im, b_re, b_im, c_re, c_im, d_skip))
    dt = jnp.exp(log_dt.astype(f32))[:, None]
    mag = jnp.exp(lam_re * dt)
    ab_re, ab_im = mag * jnp.cos(lam_im * dt), mag * jnp.sin(lam_im * dt)
    den = lam_re * lam_re + lam_im * lam_im
    er = ab_re - 1.0
    zr = (er * lam_re + ab_im * lam_im) / den
    zi = (ab_im * lam_re - er * lam_im) / den
    bb_re = zr[..., None] * b_re - zi[..., None] * b_im
    bb_im = zr[..., None] * b_im + zi[..., None] * b_re
    bu_re = jnp.einsum('blgi,gpi->blgp', u, bb_re)
    bu_im = jnp.einsum('blgi,gpi->blgp', u, bb_im)
    bu_re = bu_re.at[:, 0].add(ab_re * x0_re - ab_im * x0_im)
    bu_im = bu_im.at[:, 0].add(ab_re * x0_im + ab_im * x0_re)
    a_re = jnp.broadcast_to(ab_re, bu_re.shape)
    a_im = jnp.broadcast_to(ab_im, bu_im.shape)

    def combine(e1, e2):
        a1r, a1i, b1r, b1i = e1
        a2r, a2i, b2r, b2i = e2
        return (a2r * a1r - a2i * a1i, a2r * a1i + a2i * a1r,
                a2r * b1r - a2i * b1i + b2r, a2r * b1i + a2i * b1r + b2i)

    _, _, xr, xi = lax.associative_scan(combine, (a_re, a_im, bu_re, bu_im), axis=1)
    y = jnp.einsum('blgp,gip->blgi', xr, c_re) - jnp.einsum('blgp,gip->blgi', xi, c_im) + d_skip * u
    return y, xr[:, -1], xi[:, -1]


def _mlstm(q, k, v, i_pre, log_f, c0, n0, m0):
    c = min(CHUNK, q.shape[1])
    q, k, v = (_chunk(t, c) for t in (q, k, v))
    i_pre, log_f = (_chunk(t[..., None], c)[..., 0] for t in (i_pre, log_f))
    b = jnp.cumsum(log_f, axis=-1)
    causal = jnp.tril(jnp.ones((c, c), dtype=bool))
    logw = jnp.where(causal, b[..., :, None] - b[..., None, :] + i_pre[..., None, :], -jnp.inf)
    m_intra = jnp.max(logw, axis=-1)
    p = jnp.exp(logw - m_intra[..., None]) * jnp.einsum('bhnid,bhnjd->bhnij', q, k)
    h_intra = jnp.einsum('bhnij,bhnjv->bhniv', p, v)
    n_intra = jnp.sum(p, axis=-1)
    m_chunk = m_intra[..., -1]
    k_w = k * jnp.exp(logw[..., -1, :] - m_chunk[..., None])[..., None]
    kv = jnp.einsum('bhnjd,bhnjv->bhndv', k_w, v)
    k_sum = jnp.sum(k_w, axis=-2)

    def step(carry, inp):
        c_s, n_s, m_s = carry
        b_l, m_c, kv_n, ks_n = inp
        m_new = jnp.maximum(b_l + m_s, m_c)
        w_old = jnp.exp(b_l + m_s - m_new)
        w_new = jnp.exp(m_c - m_new)
        c_new = w_old[..., None, None] * c_s + w_new[..., None, None] * kv_n
        n_new = w_old[..., None] * n_s + w_new[..., None] * ks_n
        return (c_new, n_new, m_new), (c_s, n_s, m_s)

    xs = tuple(jnp.moveaxis(t, 2, 0) for t in (b[..., -1], m_chunk, kv, k_sum))
    (c_f, n_f, m_f), starts = lax.scan(step, (c0, n0, m0), xs)
    c_st, n_st, m_st = (jnp.moveaxis(t, 0, 2) for t in starts)
    a = b + m_st[..., None]
    m_t = jnp.maximum(a, m_intra)
    w_a = jnp.exp(a - m_t)
    w_i = jnp.exp(m_intra - m_t)
    num = w_a[..., None] * jnp.einsum('bhnid,bhndv->bhniv', q, c_st) + w_i[..., None] * h_intra
    den = w_a * jnp.einsum('bhnid,bhnd->bhni', q, n_st) + w_i * n_intra
    h = num / jnp.maximum(jnp.abs(den), jnp.exp(-m_t))[..., None]
    return _unchunk(h), c_f, n_f, m_f


def _layer_ab(h, conv_prev, s_gla0, s_gdn0, w_in, a_gate_w, a_gate_b, a_norm_g, b_conv_w,
              b_a_log, b_dt_bias, b_norm_g, w_out):
    f32 = jnp.float32
    bsz, l, _ = h.shape
    heads = lambda t, n: t.reshape(bsz, l, n, -1)
    proj = jnp.einsum('bld,de->ble', h, w_in).astype(f32)
    qa, ka, va, za, ga, qkv, zb, beta_pre, a_pre = _split(proj, AB_SPLIT)
    log_alpha = jax.nn.log_sigmoid(ga @ a_gate_w.astype(f32) + a_gate_b) / A_GATE_TAU
    o_a, s_gla = _gla(heads(qa, A_HEADS) * A_DK ** -0.5, heads(ka, A_HEADS), heads(va, A_HEADS),
                      heads(log_alpha, A_HEADS), s_gla0.astype(f32))
    o_a = _head_rms(o_a, a_norm_g.reshape(A_HEADS, A_DV)) * jax.nn.silu(za)
    xp = jnp.concatenate([conv_prev.astype(f32), qkv], axis=1)
    conv = xp[:, 0:l] * b_conv_w[0]
    for j in range(1, B_CONV):
        conv = conv + xp[:, j:j + l] * b_conv_w[j]
    conv_new = xp[:, l:]
    qb, kb, vb = _split(jax.nn.silu(conv), (B_KW, B_KW, B_VW))
    g = -jnp.exp(b_a_log) * jax.nn.softplus(a_pre + b_dt_bias)
    beta = jax.nn.sigmoid(beta_pre)
    o_b, s_gdn = _gated_delta(_l2norm(heads(qb, B_HEADS)) * B_DK ** -0.5, _l2norm(heads(kb, B_HEADS)),
                              heads(vb, B_HEADS), g, beta, s_gdn0.astype(f32))
    o_b = _head_rms(o_b, b_norm_g) * jax.nn.silu(zb)
    out = jnp.einsum('ble,ed->bld', jnp.concatenate([o_a, o_b], axis=-1).astype(h.dtype), w_out)
    return out.astype(h.dtype), conv_new, s_gla, s_gdn


def _layer_cd(h, s5_re0, s5_im0, mc0, mn0, mm0, w_in, c_lam_re, c_lam_im, c_log_dt, c_b_re, c_b_im,
              c_c_re, c_c_im, c_d, c_glu_w, c_glu_b, d_i_bias, d_f_bias, d_norm_g, w_out):
    f32 = jnp.float32
    bsz, l, _ = h.shape
    heads = lambda t, n: t.reshape(bsz, l, n, -1)
    proj = jnp.einsum('bld,de->ble', h, w_in).astype(f32)
    uc, zc, qd, kd, vd, od, zd, i_pre, f_pre = _split(proj, CD_SPLIT)
    y, s5_re, s5_im = _s5(uc.reshape(bsz, l, C_GROUPS, C_GROUP), c_lam_re, c_lam_im, c_log_dt, c_b_re,
                          c_b_im, c_c_re, c_c_im, c_d, s5_re0.astype(f32), s5_im0.astype(f32))
    y = jax.nn.gelu(y.reshape(bsz, l, C_W))
    o_c = y * jax.nn.sigmoid(y @ c_glu_w.astype(f32) + c_glu_b) * jax.nn.silu(zc)
    h_d, mc, mn, mm = _mlstm(heads(qd, D_HEADS) * D_DK ** -0.5, heads(kd, D_HEADS), heads(vd, D_HEADS),
                             i_pre + d_i_bias, jax.nn.log_sigmoid(f_pre + d_f_bias),
                             mc0.astype(f32), mn0.astype(f32), mm0.astype(f32))
    h_d = jax.nn.sigmoid(od) * h_d.reshape(bsz, l, D_VW)
    o_d = _head_layernorm(heads(h_d, D_HEADS), d_norm_g.reshape(D_HEADS, D_DV)) * jax.nn.silu(zd)
    out = jnp.einsum('ble,ed->bld', jnp.concatenate([o_c, o_d], axis=-1).astype(h.dtype), w_out)
    return out.astype(h.dtype), s5_re, s5_im, mc, mn, mm


def _trunk(x, ab_state, cd_state, norm_g, final_norm_g, ab_w, cd_w):
    h = x
    for layer in range(DEPTH):
        hn = _rmsnorm(h, norm_g[layer])
        if layer % 2 == 0:
            out, *ab_state = _layer_ab(hn, *ab_state, *ab_w)
        else:
            out, *cd_state = _layer_cd(hn, *cd_state, *cd_w)
        h = h + out
    ab_state = [s.astype(x.dtype) for s in ab_state]
    cd_state = [s.astype(x.dtype) for s in cd_state]
    return _rmsnorm(h, final_norm_g), ab_state, cd_state


def setup_inputs(seed: int = 0) -> dict:
    key = jax.random.key(seed)
    keys = iter(jax.random.split(key, 48))

    def nrm(shape, scale):
        return jax.random.normal(next(keys), shape, jnp.float32) * scale

    def uni(shape, lo, hi):
        return jax.random.uniform(next(keys), shape, jnp.float32, lo, hi)

    x_prompt = nrm((BATCH, SEQ, D_MODEL), 1.0)
    x_sample = nrm((DEC_BATCH, DEC_SEQ, D_MODEL), 1.0)
    cache_gdn_conv = nrm((DEC_BATCH, B_CONV - 1, B_QKV), 1.0)
    state_gla = nrm((DEC_BATCH, A_HEADS, A_DK, A_DV), 0.5)
    state_gdn = nrm((DEC_BATCH, B_HEADS, B_DK, B_DV), 0.1)
    state_s5_re = nrm((DEC_BATCH, C_GROUPS, C_STATE), 0.3)
    state_s5_im = nrm((DEC_BATCH, C_GROUPS, C_STATE), 0.3)
    state_mlstm_c = nrm((DEC_BATCH, D_HEADS, D_DK, D_DV), 0.5)
    state_mlstm_n = nrm((DEC_BATCH, D_HEADS, D_DK), 0.5)
    state_mlstm_m = nrm((DEC_BATCH, D_HEADS), 1.0)
    norm_g = 1.0 + nrm((DEPTH, D_MODEL), 0.01)
    final_norm_g = 1.0 + nrm((D_MODEL,), 0.01)
    w_in_ab = nrm((D_MODEL, IN_AB), D_MODEL ** -0.5)
    a_gate_w = nrm((A_GATE_RANK, A_KW), A_GATE_RANK ** -0.5)
    a_gate_b = nrm((A_KW,), 0.01)
    a_norm_g = 1.0 + nrm((A_VW,), 0.01)
    b_conv_w = nrm((B_CONV, B_QKV), B_CONV ** -0.5)
    b_a_log = jnp.log(uni((B_HEADS,), 1.0, 16.0))
    b_dt = jnp.exp(uni((B_HEADS,), math.log(1e-3), math.log(1e-1)))
    b_dt_bias = b_dt + jnp.log(-jnp.expm1(-b_dt))
    b_norm_g = 1.0 + nrm((B_DV,), 0.01)
    w_out_ab = nrm((MIX_AB, D_MODEL), MIX_AB ** -0.5)
    w_in_cd = nrm((D_MODEL, IN_CD), D_MODEL ** -0.5)
    c_lam_re = -0.5 + nrm((C_GROUPS, C_STATE), 0.01)
    c_lam_im = math.pi * jnp.arange(C_STATE, dtype=jnp.float32) + nrm((C_GROUPS, C_STATE), 0.01)
    c_log_dt = uni((C_GROUPS,), math.log(1e-3), math.log(1e-1))
    c_b_re = nrm((C_GROUPS, C_STATE, C_GROUP), (2 * C_GROUP) ** -0.5)
    c_b_im = nrm((C_GROUPS, C_STATE, C_GROUP), (2 * C_GROUP) ** -0.5)
    c_c_re = nrm((C_GROUPS, C_GROUP, C_STATE), C_STATE ** -0.5)
    c_c_im = nrm((C_GROUPS, C_GROUP, C_STATE), C_STATE ** -0.5)
    c_d = nrm((C_GROUPS, C_GROUP), 1.0)
    c_glu_w = nrm((C_W, C_W), C_W ** -0.5)
    c_glu_b = nrm((C_W,), 0.01)
    d_i_bias = nrm((D_HEADS,), 0.1)
    d_f_bias = jnp.linspace(3.0, 6.0, D_HEADS, dtype=jnp.float32) + nrm((D_HEADS,), 0.01)
    d_norm_g = 1.0 + nrm((D_VW,), 0.01)
    w_out_cd = nrm((MIX_CD, D_MODEL), MIX_CD ** -0.5)
    return {'x_prompt': x_prompt, 'x_sample': x_sample, 'cache_gdn_conv': cache_gdn_conv,
            'state_gla': state_gla, 'state_gdn': state_gdn, 'state_s5_re': state_s5_re,
            'state_s5_im': state_s5_im, 'state_mlstm_c': state_mlstm_c, 'state_mlstm_n': state_mlstm_n,
            'state_mlstm_m': state_mlstm_m, 'norm_g': norm_g, 'final_norm_g': final_norm_g,
            'w_in_ab': w_in_ab, 'a_gate_w': a_gate_w, 'a_gate_b': a_gate_b, 'a_norm_g': a_norm_g,
            'b_conv_w': b_conv_w, 'b_a_log': b_a_log, 'b_dt_bias': b_dt_bias, 'b_norm_g': b_norm_g,
            'w_out_ab': w_out_ab, 'w_in_cd': w_in_cd, 'c_lam_re': c_lam_re, 'c_lam_im': c_lam_im,
            'c_log_dt': c_log_dt, 'c_b_re': c_b_re, 'c_b_im': c_b_im, 'c_c_re': c_c_re, 'c_c_im': c_c_im,
            'c_d': c_d, 'c_glu_w': c_glu_w, 'c_glu_b': c_glu_b, 'd_i_bias': d_i_bias, 'd_f_bias': d_f_bias,
            'd_norm_g': d_norm_g, 'w_out_cd': w_out_cd}


def reference(x_prompt, x_sample, cache_gdn_conv, state_gla, state_gdn, state_s5_re, state_s5_im,
              state_mlstm_c, state_mlstm_n, state_mlstm_m, norm_g, final_norm_g, w_in_ab, a_gate_w,
              a_gate_b, a_norm_g, b_conv_w, b_a_log, b_dt_bias, b_norm_g, w_out_ab, w_in_cd, c_lam_re,
              c_lam_im, c_log_dt, c_b_re, c_b_im, c_c_re, c_c_im, c_d, c_glu_w, c_glu_b, d_i_bias,
              d_f_bias, d_norm_g, w_out_cd):
    ab_w = (w_in_ab, a_gate_w, a_gate_b, a_norm_g, b_conv_w, b_a_log, b_dt_bias, b_norm_g, w_out_ab)
    cd_w = (w_in_cd, c_lam_re, c_lam_im, c_log_dt, c_b_re, c_b_im, c_c_re, c_c_im, c_d, c_glu_w,
            c_glu_b, d_i_bias, d_f_bias, d_norm_g, w_out_cd)
    nb = x_prompt.shape[0]
    zeros = lambda *shape: jnp.zeros(shape, jnp.float32)
    prompt_ab = (zeros(nb, B_CONV - 1, B_QKV), zeros(nb, A_HEADS, A_DK, A_DV), zeros(nb, B_HEADS, B_DK, B_DV))
    prompt_cd = (zeros(nb, C_GROUPS, C_STATE), zeros(nb, C_GROUPS, C_STATE), zeros(nb, D_HEADS, D_DK, D_DV),
                 zeros(nb, D_HEADS, D_DK), zeros(nb, D_HEADS))
    y_prompt, (p_conv, p_gla, p_gdn), (p_s5_re, p_s5_im, p_mc, p_mn, p_mm) = _trunk(
        x_prompt, prompt_ab, prompt_cd, norm_g, final_norm_g, ab_w, cd_w)
    y_sample, (s_conv, s_gla, s_gdn), (s_s5_re, s_s5_im, s_mc, s_mn, s_mm) = _trunk(
        x_sample, (cache_gdn_conv, state_gla, state_gdn),
        (state_s5_re, state_s5_im, state_mlstm_c, state_mlstm_n, state_mlstm_m),
        norm_g, final_norm_g, ab_w, cd_w)
    return (y_prompt, y_sample, p_conv, p_gla, p_gdn, p_s5_re, p_s5_im, p_mc, p_mn, p_mm,
            s_conv, s_gla, s_gdn, s_s5_re, s_s5_im, s_mc, s_mn, s_mm)
```

```python
import functools
import math

import jax
import jax.numpy as jnp
from jax import lax
from jax.experimental import pallas as pl
from jax.experimental.pallas import tpu as pltpu

F32 = jnp.float32
BF16 = jnp.bfloat16
HIGHEST = lax.Precision.HIGHEST

NORM_EPS = 1e-6
CHUNK = 64
A_HEADS, A_DK, A_DV, A_GATE_RANK, A_GATE_TAU = 4, 128, 256, 16, 16.0
B_HEADS, B_DK, B_DV, B_CONV = 8, 128, 128, 4
C_GROUP, C_GROUPS, C_STATE = 16, 64, 64
D_HEADS, D_DK, D_DV = 4, 128, 256
A_KW, A_VW = A_HEADS * A_DK, A_HEADS * A_DV
B_KW, B_VW = B_HEADS * B_DK, B_HEADS * B_DV
B_QKV = 2 * B_KW + B_VW
C_W = C_GROUPS * C_GROUP
D_KW, D_VW = D_HEADS * D_DK, D_HEADS * D_DV

LANES = 128
SUBLANES = 8
VMEM_BYTES_V7X = 64 * 1024 * 1024

AB_Q, AB_K, AB_V, AB_Z = 0, A_KW, 2 * A_KW, 2 * A_KW + A_VW
AB_QKV = AB_Z + A_VW
AB_ZB = AB_QKV + B_QKV
AB_SMALL = AB_ZB + B_VW
AB_GA_LANE, AB_BETA_LANE, AB_APRE_LANE = 0, A_GATE_RANK, A_GATE_RANK + B_HEADS
CD_U, CD_Z = 0, C_W
CD_Q = 2 * C_W
CD_K = CD_Q + D_KW
CD_V = CD_K + D_KW
CD_O = CD_V + D_VW
CD_ZD = CD_O + D_VW
CD_SMALL = CD_ZD + D_VW
CD_I_LANE, CD_F_LANE = 0, D_HEADS

PROJ_TN = 1280
S5_CHUNK = 16
S5_GROUP_BLOCK = 8


def _round_up(x, m):
    return (x + m - 1) // m * m


def _vmem_limit(block_bytes, scratch_bytes=0):
    est = 2 * block_bytes + scratch_bytes
    return int(min(max(2 * est, 32 * 1024 * 1024), VMEM_BYTES_V7X - 8 * 1024 * 1024))


def _dot(a, b):
    return jnp.dot(a.astype(BF16), b.astype(BF16), preferred_element_type=F32)


def _dot_nt(a, b):
    return lax.dot_general(a.astype(BF16), b.astype(BF16), (((1,), (1,)), ((), ())),
                           preferred_element_type=F32)


def _dot_tn(a, b):
    return lax.dot_general(a.astype(BF16), b.astype(BF16), (((0,), (0,)), ((), ())),
                           preferred_element_type=F32)


def _dot_hi(a, b):
    return jnp.dot(a, b, precision=HIGHEST, preferred_element_type=F32)


def _dot_nt_hi(a, b):
    return lax.dot_general(a, b, (((1,), (1,)), ((), ())), precision=HIGHEST,
                           preferred_element_type=F32)


def _pick_col(x, lane_idx):
    lane = lax.broadcasted_iota(jnp.int32, x.shape, 1)
    return jnp.sum(jnp.where(lane == lane_idx, x, 0.0), axis=-1, keepdims=True)


def _pick_row(x, lane_idx):
    r = lax.broadcasted_iota(jnp.int32, (SUBLANES, LANES), 0)
    l = lax.broadcasted_iota(jnp.int32, (SUBLANES, LANES), 1)
    onehot = jnp.where((r == 0) & (l == lane_idx), 1.0, 0.0).astype(F32)
    return _dot_nt_hi(onehot, x)[0:1, :]


def _causal_masks(c):
    row = lax.broadcasted_iota(jnp.int32, (c, c), 0)
    col = lax.broadcasted_iota(jnp.int32, (c, c), 1)
    return row >= col, row > col


def _norm_matmul_kernel(x_ref, g_ref, w_ref, o_ref, xn_ref):
    @pl.when(pl.program_id(1) == 0)
    def _():
        x = x_ref[...]
        y = x * lax.rsqrt(jnp.mean(x * x, axis=-1, keepdims=True) + NORM_EPS) * g_ref[...]
        xn_ref[...] = y.astype(BF16)

    o_ref[...] = jnp.dot(xn_ref[...], w_ref[...], preferred_element_type=F32)


def _norm_matmul(x2d, g_row, w_bf16):
    m, d = x2d.shape
    n = w_bf16.shape[1]
    tm = min(m, 1024)
    tn = PROJ_TN
    assert m % tm == 0 and n % tn == 0
    blocks = tm * d * 4 + d * tn * 2 + tm * tn * 4
    return pl.pallas_call(
        _norm_matmul_kernel,
        out_shape=jax.ShapeDtypeStruct((m, n), F32),
        grid=(m // tm, n // tn),
        in_specs=[pl.BlockSpec((tm, d), lambda i, j: (i, 0)),
                  pl.BlockSpec((1, d), lambda i, j: (0, 0)),
                  pl.BlockSpec((d, tn), lambda i, j: (0, j))],
        out_specs=pl.BlockSpec((tm, tn), lambda i, j: (i, j)),
        scratch_shapes=[pltpu.VMEM((tm, d), BF16)],
        compiler_params=pltpu.CompilerParams(
            dimension_semantics=("parallel", "arbitrary"),
            vmem_limit_bytes=_vmem_limit(blocks, tm * d * 2)),
        name="norm_in_proj",
    )(x2d, g_row, w_bf16)


def _out_proj_kernel(a_ref, b_ref, wa_ref, wb_ref, h_ref, o_ref):
    out = (jnp.dot(a_ref[...], wa_ref[...], preferred_element_type=F32)
           + jnp.dot(b_ref[...], wb_ref[...], preferred_element_type=F32))
    o_ref[...] = h_ref[...] + out


def _out_proj_norm_kernel(a_ref, b_ref, wa_ref, wb_ref, h_ref, g_ref, o_ref):
    out = (jnp.dot(a_ref[...], wa_ref[...], preferred_element_type=F32)
           + jnp.dot(b_ref[...], wb_ref[...], preferred_element_type=F32))
    h = h_ref[...] + out
    o_ref[...] = h * lax.rsqrt(jnp.mean(h * h, axis=-1, keepdims=True) + NORM_EPS) * g_ref[...]


def _out_proj(mix_a, mix_b, w_a, w_b, h2d, final_g_row=None):
    m, d = h2d.shape
    ka, kb = mix_a.shape[1], mix_b.shape[1]
    tm = min(m, 512)
    assert m % tm == 0
    in_specs = [pl.BlockSpec((tm, ka), lambda i: (i, 0)),
                pl.BlockSpec((tm, kb), lambda i: (i, 0)),
                pl.BlockSpec((ka, d), lambda i: (0, 0)),
                pl.BlockSpec((kb, d), lambda i: (0, 0)),
                pl.BlockSpec((tm, d), lambda i: (i, 0))]
    args = [mix_a, mix_b, w_a, w_b, h2d]
    kernel = _out_proj_kernel
    if final_g_row is not None:
        in_specs.append(pl.BlockSpec((1, d), lambda i: (0, 0)))
        args.append(final_g_row)
        kernel = _out_proj_norm_kernel
    blocks = tm * (ka + kb) * 2 + (ka + kb) * d * 2 + 2 * tm * d * 4
    return pl.pallas_call(
        kernel,
        out_shape=jax.ShapeDtypeStruct((m, d), F32),
        grid=(m // tm,),
        in_specs=in_specs,
        out_specs=pl.BlockSpec((tm, d), lambda i: (i, 0)),
        compiler_params=pltpu.CompilerParams(
            dimension_semantics=("parallel",),
            vmem_limit_bytes=_vmem_limit(blocks)),
        name="out_proj_norm" if final_g_row is not None else "out_proj",
    )(*args)


def _gla_kernel(q_ref, k_ref, v_ref, z_ref, sm_ref, gw_ref, gb_ref, ng_ref, s0_ref,
                o_ref, sout_ref, st_ref, *, c, nchunks):
    t = pl.program_id(2)
    last_t = pl.num_programs(2) - 1

    @pl.when(t == 0)
    def _():
        st_ref[...] = s0_ref[...].T

    causal, _ = _causal_masks(c)
    tri = causal.astype(F32)
    gw = gw_ref[...]
    gb = gb_ref[...]
    ng = ng_ref[...]

    def body(n, carry):
        sl = pl.ds(pl.multiple_of(n * c, c), c)
        log_alpha = jax.nn.log_sigmoid(_dot(sm_ref[sl, :], gw) + gb) * (1.0 / A_GATE_TAU)
        b = _dot_hi(tri, log_alpha)
        b_last = b[c - 1:c, :]
        q = q_ref[sl, :] * (A_DK ** -0.5)
        k = k_ref[sl, :]
        v = v_ref[sl, :].astype(BF16)
        q_dec = (q * jnp.exp(b)).astype(BF16)
        k_dec = (k * jnp.exp(-b)).astype(BF16)
        scores = jnp.where(causal, _dot_nt(q_dec, k_dec), 0.0)
        s_t = st_ref[...]
        o = _dot(scores, v) + _dot_nt(q_dec, s_t)
        k_w = k * jnp.exp(b_last - b)
        st_ref[...] = s_t * jnp.exp(b_last) + _dot_tn(v, k_w)
        o = o * lax.rsqrt(jnp.mean(o * o, axis=-1, keepdims=True) + NORM_EPS) * ng
        o_ref[sl, :] = (o * jax.nn.silu(z_ref[sl, :])).astype(o_ref.dtype)
        return carry

    lax.fori_loop(0, nchunks, body, 0)

    @pl.when(t == last_t)
    def _():
        sout_ref[...] = st_ref[...].T


def _gla(proj3, gate_w_pad, gate_b_row, norm_g_row, s0, *, c, tb):
    bsz, l, _ = proj3.shape
    nblk = l // tb
    tok = lambda col: (lambda b, h, t: (b, t, col(h)))
    in_specs = [
        pl.BlockSpec((None, tb, A_DK), tok(lambda h: AB_Q // A_DK + h)),
        pl.BlockSpec((None, tb, A_DK), tok(lambda h: AB_K // A_DK + h)),
        pl.BlockSpec((None, tb, A_DV), tok(lambda h: AB_V // A_DV + h)),
        pl.BlockSpec((None, tb, A_DV), tok(lambda h: AB_Z // A_DV + h)),
        pl.BlockSpec((None, tb, LANES), tok(lambda h: AB_SMALL // LANES)),
        pl.BlockSpec((LANES, A_DK), lambda b, h, t: (0, h)),
        pl.BlockSpec((1, A_DK), lambda b, h, t: (0, h)),
        pl.BlockSpec((1, A_DV), lambda b, h, t: (0, h)),
        pl.BlockSpec((None, None, A_DK, A_DV), lambda b, h, t: (b, h, 0, 0)),
    ]
    out_specs = [
        pl.BlockSpec((None, tb, A_DV), lambda b, h, t: (b, t, h)),
        pl.BlockSpec((None, None, A_DK, A_DV), lambda b, h, t: (b, h, 0, 0)),
    ]
    blocks = tb * (2 * A_DK + 2 * A_DV + LANES) * 4 + tb * A_DV * 2 + 2 * A_DK * A_DV * 4
    return pl.pallas_call(
        functools.partial(_gla_kernel, c=c, nchunks=tb // c),
        out_shape=[jax.ShapeDtypeStruct((bsz, l, A_VW), BF16),
                   jax.ShapeDtypeStruct((bsz, A_HEADS, A_DK, A_DV), F32)],
        grid=(bsz, A_HEADS, nblk),
        in_specs=in_specs,
        out_specs=out_specs,
        scratch_shapes=[pltpu.VMEM((A_DV, A_DK), F32)],
        compiler_params=pltpu.CompilerParams(
            dimension_semantics=("parallel", "parallel", "arbitrary"),
            vmem_limit_bytes=_vmem_limit(blocks, A_DK * A_DV * 4)),
        name="gla_mixer",
    )(proj3, proj3, proj3, proj3, proj3, gate_w_pad, gate_b_row, norm_g_row, s0)


def _gdn_kernel(xq_ref, xk_ref, xv_ref, z_ref, sm_ref, wq_ref, wk_ref, wv_ref,
                cq_ref, ck_ref, cv_ref, alog_ref, dtb_ref, ng_ref, s0_ref,
                o_ref, sout_ref, s_ref, tail_ref, conv_ref, *, c, nchunks):
    h = pl.program_id(1)
    t = pl.program_id(2)
    last_t = pl.num_programs(2) - 1
    tb = c * nchunks

    @pl.when(t == 0)
    def _():
        s_ref[...] = s0_ref[...]
        for i, cp in enumerate((cq_ref, ck_ref, cv_ref)):
            tail_ref[i, 0:SUBLANES - (B_CONV - 1), :] = jnp.zeros((SUBLANES - (B_CONV - 1), B_DK), F32)
            tail_ref[i, SUBLANES - (B_CONV - 1):SUBLANES, :] = cp[...]

    for i, (x_ref, w_ref) in enumerate(((xq_ref, wq_ref), (xk_ref, wk_ref), (xv_ref, wv_ref))):
        x = x_ref[...]
        w = w_ref[...]
        ext = jnp.concatenate([tail_ref[i], x], axis=0)
        conv = ext[SUBLANES - 3:SUBLANES - 3 + tb, :] * w[0:1, :]
        conv = conv + ext[SUBLANES - 2:SUBLANES - 2 + tb, :] * w[1:2, :]
        conv = conv + ext[SUBLANES - 1:SUBLANES - 1 + tb, :] * w[2:3, :]
        conv = conv + x * w[3:4, :]
        conv_ref[i] = jax.nn.silu(conv)
        tail_ref[i] = x[tb - SUBLANES:tb, :]

    causal, strict = _causal_masks(c)
    tri = causal.astype(F32)
    eye = jnp.where(causal & jnp.logical_not(strict), 1.0, 0.0).astype(F32)
    neg_a_exp = -jnp.exp(alog_ref[...])
    dtb = dtb_ref[...]
    ng = ng_ref[...]
    n_double = int(math.log2(c)) - 1

    def body(n, carry):
        sl = pl.ds(pl.multiple_of(n * c, c), c)
        q = conv_ref[0, sl, :]
        k = conv_ref[1, sl, :]
        v = conv_ref[2, sl, :]
        q = q * lax.rsqrt(jnp.sum(q * q, axis=-1, keepdims=True) + NORM_EPS) * (B_DK ** -0.5)
        k = k * lax.rsqrt(jnp.sum(k * k, axis=-1, keepdims=True) + NORM_EPS)
        sm = sm_ref[sl, :]
        g_full = neg_a_exp * jax.nn.softplus(sm + dtb)
        g_cum = _dot_hi(tri, g_full)
        g_col = _pick_col(g_cum, AB_APRE_LANE + h)
        g_row = _pick_row(g_cum, AB_APRE_LANE + h)
        beta = _pick_col(jax.nn.sigmoid(sm), AB_BETA_LANE + h)
        decay = jnp.exp(jnp.where(causal, g_col - g_row, -jnp.inf))
        k_beta = k * beta
        lower = jnp.where(strict, _dot_nt(k_beta, k) * decay, 0.0)
        pw = -lower
        inv = eye + pw
        for _ in range(n_double):
            pw = _dot_hi(pw, pw)
            inv = inv + _dot_hi(inv, pw)
        e_g = jnp.exp(g_col)
        u = _dot_hi(inv, v * beta)
        w = _dot_hi(inv, k_beta * e_g)
        qk = jnp.where(causal, _dot_nt(q, k) * decay, 0.0)
        s = s_ref[...]
        v_new = u - _dot(w, s)
        o = _dot(q * e_g, s) + _dot(qk, v_new)
        g_last = g_col[c - 1:c, :]
        k_g = k * jnp.exp(g_last - g_col)
        s_ref[...] = jnp.exp(g_last) * s + _dot_tn(k_g, v_new)
        o = o * lax.rsqrt(jnp.mean(o * o, axis=-1, keepdims=True) + NORM_EPS) * ng
        o_ref[sl, :] = (o * jax.nn.silu(z_ref[sl, :])).astype(o_ref.dtype)
        return carry

    lax.fori_loop(0, nchunks, body, 0)

    @pl.when(t == last_t)
    def _():
        sout_ref[...] = s_ref[...]


def _gdn(proj3, conv_prev, conv_w, alog_row, dtb_row, norm_g_row, s0, *, c, tb):
    bsz, l, _ = proj3.shape
    nblk = l // tb
    qb = AB_QKV // B_DK
    tok = lambda col: (lambda b, h, t: (b, t, col(h)))
    par = lambda col: (lambda b, h, t: (0, col(h)))
    prv = lambda col: (lambda b, h, t: (b, 0, col(h)))
    in_specs = [
        pl.BlockSpec((None, tb, B_DK), tok(lambda h: qb + h)),
        pl.BlockSpec((None, tb, B_DK), tok(lambda h: qb + B_HEADS + h)),
        pl.BlockSpec((None, tb, B_DV), tok(lambda h: qb + 2 * B_HEADS + h)),
        pl.BlockSpec((None, tb, B_DV), tok(lambda h: AB_ZB // B_DV + h)),
        pl.BlockSpec((None, tb, LANES), tok(lambda h: AB_SMALL // LANES)),
        pl.BlockSpec((B_CONV, B_DK), par(lambda h: h)),
        pl.BlockSpec((B_CONV, B_DK), par(lambda h: B_HEADS + h)),
        pl.BlockSpec((B_CONV, B_DV), par(lambda h: 2 * B_HEADS + h)),
        pl.BlockSpec((None, B_CONV - 1, B_DK), prv(lambda h: h)),
        pl.BlockSpec((None, B_CONV - 1, B_DK), prv(lambda h: B_HEADS + h)),
        pl.BlockSpec((None, B_CONV - 1, B_DV), prv(lambda h: 2 * B_HEADS + h)),
        pl.BlockSpec((1, LANES), lambda b, h, t: (0, 0)),
        pl.BlockSpec((1, LANES), lambda b, h, t: (0, 0)),
        pl.BlockSpec((1, B_DV), lambda b, h, t: (0, 0)),
        pl.BlockSpec((None, None, B_DK, B_DV), lambda b, h, t: (b, h, 0, 0)),
    ]
    out_specs = [
        pl.BlockSpec((None, tb, B_DV), lambda b, h, t: (b, t, h)),
        pl.BlockSpec((None, None, B_DK, B_DV), lambda b, h, t: (b, h, 0, 0)),
    ]
    blocks = tb * 5 * LANES * 4 + tb * B_DV * 2 + 2 * B_DK * B_DV * 4
    scratch = B_DK * B_DV * 4 + 3 * SUBLANES * B_DK * 4 + 3 * tb * B_DK * 4
    return pl.pallas_call(
        functools.partial(_gdn_kernel, c=c, nchunks=tb // c),
        out_shape=[jax.ShapeDtypeStruct((bsz, l, B_VW), BF16),
                   jax.ShapeDtypeStruct((bsz, B_HEADS, B_DK, B_DV), F32)],
        grid=(bsz, B_HEADS, nblk),
        in_specs=in_specs,
        out_specs=out_specs,
        scratch_shapes=[pltpu.VMEM((B_DK, B_DV), F32),
                        pltpu.VMEM((3, SUBLANES, B_DK), F32),
                        pltpu.VMEM((3, tb, B_DK), F32)],
        compiler_params=pltpu.CompilerParams(
            dimension_semantics=("parallel", "parallel", "arbitrary"),
            vmem_limit_bytes=_vmem_limit(blocks, scratch)),
        name="gdn_mixer",
    )(proj3, proj3, proj3, proj3, proj3, conv_w, conv_w, conv_w,
      conv_prev, conv_prev, conv_prev, alog_row, dtb_row, norm_g_row, s0)


def _mlstm_kernel(q_ref, k_ref, v_ref, og_ref, z_ref, sm_ref, ib_ref, fb_ref, ng_ref,
                  c0_ref, n0_ref, m0_ref, o_ref, cout_ref, nout_ref, mout_ref,
                  c_ref, n_ref, m_ref, *, c, nchunks):
    h = pl.program_id(1)
    t = pl.program_id(2)
    last_t = pl.num_programs(2) - 1

    @pl.when(t == 0)
    def _():
        c_ref[...] = c0_ref[...]
        n_ref[...] = n0_ref[...]
        m_ref[...] = m0_ref[...]

    causal, _ = _causal_masks(c)
    tri = causal.astype(F32)
    ib = ib_ref[...]
    fb = fb_ref[...]
    ng = ng_ref[...]

    def body(n, carry):
        sl = pl.ds(pl.multiple_of(n * c, c), c)
        sm = sm_ref[sl, :]
        i_full = sm + ib
        b_full = _dot_hi(tri, jax.nn.log_sigmoid(sm + fb))
        b_col = _pick_col(b_full, CD_F_LANE + h)
        b_row = _pick_row(b_full, CD_F_LANE + h)
        i_col = _pick_col(i_full, CD_I_LANE + h)
        i_row = _pick_row(i_full, CD_I_LANE + h)
        logw = jnp.where(causal, b_col - b_row + i_row, -jnp.inf)
        m_intra = jnp.max(logw, axis=-1, keepdims=True)
        q = q_ref[sl, :] * (D_DK ** -0.5)
        k = k_ref[sl, :]
        v = v_ref[sl, :].astype(BF16)
        p = jnp.exp(logw - m_intra) * _dot_nt(q, k)
        h_intra = _dot(p, v)
        n_intra = jnp.sum(p, axis=-1, keepdims=True)
        b_last = b_col[c - 1:c, :]
        m_chunk = m_intra[c - 1:c, :]
        k_w = k * jnp.exp(b_last - b_col + i_col - m_chunk)
        kv = _dot_tn(k_w, v)
        k_sum = jnp.sum(k_w, axis=0, keepdims=True)
        c_s = c_ref[...]
        n_s = n_ref[...]
        m_s = m_ref[...]
        a = b_col + m_s
        m_t = jnp.maximum(a, m_intra)
        w_a = jnp.exp(a - m_t)
        w_i = jnp.exp(m_intra - m_t)
        num = w_a * _dot(q, c_s) + w_i * h_intra
        den = w_a * jnp.sum(q * n_s, axis=-1, keepdims=True) + w_i * n_intra
        hh = num / jnp.maximum(jnp.abs(den), jnp.exp(-m_t))
        m_new = jnp.maximum(b_last + m_s, m_chunk)
        w_old = jnp.exp(b_last + m_s - m_new)
        w_new = jnp.exp(m_chunk - m_new)
        c_ref[...] = w_old * c_s + w_new * kv
        n_ref[...] = w_old * n_s + w_new * k_sum
        m_ref[...] = m_new
        hd = jax.nn.sigmoid(og_ref[sl, :]) * hh
        oc = hd - jnp.mean(hd, axis=-1, keepdims=True)
        o = oc * lax.rsqrt(jnp.mean(oc * oc, axis=-1, keepdims=True) + NORM_EPS) * ng
        o_ref[sl, :] = (o * jax.nn.silu(z_ref[sl, :])).astype(o_ref.dtype)
        return carry

    lax.fori_loop(0, nchunks, body, 0)

    @pl.when(t == last_t)
    def _():
        cout_ref[...] = c_ref[...]
        nout_ref[...] = n_ref[...]
        mout_ref[...] = m_ref[...]


def _mlstm(proj3, ib_row, fb_row, norm_g_row, c0, n0, m0, *, c, tb):
    bsz, l, _ = proj3.shape
    nblk = l // tb
    tok = lambda col: (lambda b, h, t: (b, t, col(h)))
    st4 = lambda b, h, t: (b, h, 0, 0)
    in_specs = [
        pl.BlockSpec((None, tb, D_DK), tok(lambda h: CD_Q // D_DK + h)),
        pl.BlockSpec((None, tb, D_DK), tok(lambda h: CD_K // D_DK + h)),
        pl.BlockSpec((None, tb, D_DV), tok(lambda h: CD_V // D_DV + h)),
        pl.BlockSpec((None, tb, D_DV), tok(lambda h: CD_O // D_DV + h)),
        pl.BlockSpec((None, tb, D_DV), tok(lambda h: CD_ZD // D_DV + h)),
        pl.BlockSpec((None, tb, LANES), tok(lambda h: CD_SMALL // LANES)),
        pl.BlockSpec((1, LANES), lambda b, h, t: (0, 0)),
        pl.BlockSpec((1, LANES), lambda b, h, t: (0, 0)),
        pl.BlockSpec((1, D_DV), lambda b, h, t: (0, h)),
        pl.BlockSpec((None, None, D_DK, D_DV), st4),
        pl.BlockSpec((None, None, 1, D_DK), st4),
        pl.BlockSpec((None, None, 1, 1), st4),
    ]
    out_specs = [
        pl.BlockSpec((None, tb, D_DV), lambda b, h, t: (b, t, h)),
        pl.BlockSpec((None, None, D_DK, D_DV), st4),
        pl.BlockSpec((None, None, 1, D_DK), st4),
        pl.BlockSpec((None, None, 1, 1), st4),
    ]
    blocks = tb * (2 * D_DK + 3 * D_DV + LANES) * 4 + tb * D_DV * 2 + 2 * D_DK * D_DV * 4
    return pl.pallas_call(
        functools.partial(_mlstm_kernel, c=c, nchunks=tb // c),
        out_shape=[jax.ShapeDtypeStruct((bsz, l, D_VW), BF16),
                   jax.ShapeDtypeStruct((bsz, D_HEADS, D_DK, D_DV), F32),
                   jax.ShapeDtypeStruct((bsz, D_HEADS, 1, D_DK), F32),
                   jax.ShapeDtypeStruct((bsz, D_HEADS, 1, 1), F32)],
        grid=(bsz, D_HEADS, nblk),
        in_specs=in_specs,
        out_specs=out_specs,
        scratch_shapes=[pltpu.VMEM((D_DK, D_DV), F32),
                        pltpu.VMEM((1, D_DK), F32),
                        pltpu.VMEM((1, 1), F32)],
        compiler_params=pltpu.CompilerParams(
            dimension_semantics=("parallel", "parallel", "arbitrary"),
            vmem_limit_bytes=_vmem_limit(blocks, D_DK * D_DV * 4)),
        name="mlstm_mixer",
    )(proj3, proj3, proj3, proj3, proj3, proj3, ib_row, fb_row, norm_g_row, c0, n0, m0)


def _s5_kernel(u_ref, km_ref, bst_ref, cst_ref, apow_ref, x0_ref, y_ref, xf_ref, *, nc, bb):
    m = nc * bb
    gb = u_ref.shape[0]
    row = lax.broadcasted_iota(jnp.int32, (m, 2 * C_STATE), 0)
    n_idx = row & (nc - 1)
    n_log = int(math.log2(nc))

    def cmul(a1, a2, x):
        return a1 * x + a2 * pltpu.roll(x, C_STATE, 1)

    for g in range(gb):
        u = u_ref[g]
        x0_rows = jnp.zeros((m, 2 * C_STATE), F32)
        for b in range(bb):
            x0_rows = jnp.where(row == b * nc, x0_ref[g, b:b + 1, :], x0_rows)
        x = (jnp.dot(u, bst_ref[g], preferred_element_type=F32)
             + cmul(apow_ref[g, 0:1, :], apow_ref[g, 1:2, :], x0_rows))
        for j in range(n_log):
            sh = 1 << j
            shifted = jnp.where(n_idx >= sh, pltpu.roll(x, sh, 0), 0.0)
            x = x + cmul(apow_ref[g, 2 * j:2 * j + 1, :], apow_ref[g, 2 * j + 1:2 * j + 2, :], shifted)
        x_start = jnp.where(n_idx >= 1, pltpu.roll(x, 1, 0), x0_rows)
        y_ref[g] = (jnp.dot(u, km_ref[g], preferred_element_type=F32)
                    + jnp.dot(x_start.astype(BF16), cst_ref[g], preferred_element_type=F32))
        for b in range(bb):
            xf_ref[g, b:b + 1, :] = x[b * nc + nc - 1:b * nc + nc, :]


def _s5_chunks(u_t, kmat, bst, cst, apow, x0, *, nc, bb):
    g, m_total, w = u_t.shape
    m = nc * bb
    nb = m_total // m
    gb = S5_GROUP_BLOCK
    sw = 2 * C_STATE
    blocks = gb * (m * w * 2 + w * w * 2 + 2 * w * sw * 2 + apow.shape[1] * sw * 4
                   + 2 * bb * sw * 4 + m * w * 4)
    return pl.pallas_call(
        functools.partial(_s5_kernel, nc=nc, bb=bb),
        out_shape=[jax.ShapeDtypeStruct((g, m_total, w), F32),
                   jax.ShapeDtypeStruct((g, nb * bb, sw), F32)],
        grid=(g // gb, nb),
        in_specs=[pl.BlockSpec((gb, m, w), lambda i, j: (i, j, 0)),
                  pl.BlockSpec((gb, w, w), lambda i, j: (i, 0, 0)),
                  pl.BlockSpec((gb, w, sw), lambda i, j: (i, 0, 0)),
                  pl.BlockSpec((gb, sw, w), lambda i, j: (i, 0, 0)),
                  pl.BlockSpec((gb, apow.shape[1], sw), lambda i, j: (i, 0, 0)),
                  pl.BlockSpec((gb, bb, sw), lambda i, j: (i, j, 0))],
        out_specs=[pl.BlockSpec((gb, m, w), lambda i, j: (i, j, 0)),
                   pl.BlockSpec((gb, bb, sw), lambda i, j: (i, j, 0))],
        compiler_params=pltpu.CompilerParams(
            dimension_semantics=("parallel", "parallel"),
            vmem_limit_bytes=_vmem_limit(blocks)),
        name="s5_chunks",
    )(u_t, kmat, bst, cst, apow, x0)


def _s5_glu_kernel(y_ref, u_ref, z_ref, d_ref, w_ref, b_ref, o_ref):
    y = jax.nn.gelu(y_ref[...] + d_ref[...] * u_ref[...])
    gate = jax.nn.sigmoid(jnp.dot(y.astype(BF16), w_ref[...], preferred_element_type=F32) + b_ref[...])
    o_ref[...] = (y * gate * jax.nn.silu(z_ref[...])).astype(o_ref.dtype)


def _s5_glu(y2d, proj2d, d_row, glu_w, glu_b_row):
    m = y2d.shape[0]
    tm = min(m, 512)
    blocks = 3 * tm * C_W * 4 + C_W * C_W * 2 + tm * C_W * 2
    return pl.pallas_call(
        _s5_glu_kernel,
        out_shape=jax.ShapeDtypeStruct((m, C_W), BF16),
        grid=(m // tm,),
        in_specs=[pl.BlockSpec((tm, C_W), lambda i: (i, 0)),
                  pl.BlockSpec((tm, C_W), lambda i: (i, CD_U // C_W)),
                  pl.BlockSpec((tm, C_W), lambda i: (i, CD_Z // C_W)),
                  pl.BlockSpec((1, C_W), lambda i: (0, 0)),
                  pl.BlockSpec((C_W, C_W), lambda i: (0, 0)),
                  pl.BlockSpec((1, C_W), lambda i: (0, 0))],
        out_specs=pl.BlockSpec((tm, C_W), lambda i: (i, 0)),
        compiler_params=pltpu.CompilerParams(
            dimension_semantics=("parallel",),
            vmem_limit_bytes=_vmem_limit(blocks)),
        name="s5_glu",
    )(y2d, proj2d, proj2d, d_row, glu_w, glu_b_row)


def _s5_operators(lam_re, lam_im, log_dt, b_re, b_im, c_re, c_im, n_log):
    g, p = lam_re.shape
    tc = S5_CHUNK
    dt = jnp.exp(log_dt.astype(F32))[:, None]
    mag = jnp.exp(lam_re * dt)
    ab_re, ab_im = mag * jnp.cos(lam_im * dt), mag * jnp.sin(lam_im * dt)
    den = lam_re * lam_re + lam_im * lam_im
    er = ab_re - 1.0
    zr = (er * lam_re + ab_im * lam_im) / den
    zi = (ab_im * lam_re - er * lam_im) / den
    bb_re = zr[..., None] * b_re - zi[..., None] * b_im
    bb_im = zr[..., None] * b_im + zi[..., None] * b_re
    pr, pi = [jnp.ones_like(ab_re)], [jnp.zeros_like(ab_re)]
    for _ in range(tc):
        pr, pi = pr + [pr[-1] * ab_re - pi[-1] * ab_im], pi + [pr[-1] * ab_im + pi[-1] * ab_re]
    pw_re, pw_im = jnp.stack(pr), jnp.stack(pi)
    abr = pw_re[:tc, :, :, None] * bb_re - pw_im[:tc, :, :, None] * bb_im
    abi = pw_re[:tc, :, :, None] * bb_im + pw_im[:tc, :, :, None] * bb_re
    kern = (jnp.einsum('gjp,dgpi->dgji', c_re, abr, precision=HIGHEST)
            - jnp.einsum('gjp,dgpi->dgji', c_im, abi, precision=HIGHEST))
    tau = jnp.arange(tc)
    delta = tau[None, :] - tau[:, None]
    kt = jnp.where((delta >= 0)[:, :, None, None, None], kern[jnp.clip(delta, 0, tc - 1)], 0.0)
    kmat = kt.transpose(2, 0, 4, 1, 3).reshape(g, tc * C_GROUP, tc * C_GROUP)
    to_rows = lambda a: a[::-1].transpose(1, 0, 3, 2).reshape(g, tc * C_GROUP, p)
    bst = jnp.concatenate([to_rows(abr), to_rows(abi)], axis=-1)
    cr = c_re[None] * pw_re[1:, :, None, :] - c_im[None] * pw_im[1:, :, None, :]
    ci = -(c_re[None] * pw_im[1:, :, None, :] + c_im[None] * pw_re[1:, :, None, :])
    to_cols = lambda a: a.transpose(1, 3, 0, 2).reshape(g, p, tc * C_GROUP)
    cst = jnp.concatenate([to_cols(cr), to_cols(ci)], axis=1)
    r, i = pw_re[tc], pw_im[tc]
    rows = []
    for _ in range(max(n_log, 1)):
        rows += [jnp.concatenate([r, r], -1), jnp.concatenate([-i, i], -1)]
        r, i = r * r - i * i, 2.0 * r * i
    apow = jnp.stack(rows, axis=1)
    return kmat.astype(BF16), bst.astype(BF16), cst.astype(BF16), apow


def _s5(proj3, ops, x0_re, x0_im):
    kmat, bst, cst, apow = ops
    bsz, l, _ = proj3.shape
    tc = S5_CHUNK
    nc = l // tc
    u = proj3[:, :, CD_U:CD_U + C_W].astype(BF16)
    u_t = u.reshape(bsz, nc, tc, C_GROUPS, C_GROUP).transpose(3, 0, 1, 2, 4).reshape(
        C_GROUPS, bsz * nc, tc * C_GROUP)
    x0 = jnp.concatenate([x0_re, x0_im], axis=-1).transpose(1, 0, 2)
    bb = bsz if bsz * nc <= 512 else max(1, 512 // nc)
    y_t, xf = _s5_chunks(u_t, kmat, bst, cst, apow, x0, nc=nc, bb=bb)
    y = y_t.reshape(C_GROUPS, bsz, nc, tc, C_GROUP).transpose(1, 2, 3, 0, 4).reshape(bsz * l, C_W)
    xf = xf.transpose(1, 0, 2)
    return y, xf[..., :C_STATE], xf[..., C_STATE:]


def _pad_cols(w, n):
    return jnp.pad(w, ((0, 0), (0, n - w.shape[1])))


def _lane_row(vals, lane0):
    return jnp.zeros((1, LANES), F32).at[0, lane0:lane0 + vals.shape[0]].set(vals.astype(F32))


def _prepare_weights(norm_g, final_norm_g, w_in_ab, a_gate_w, a_gate_b, a_norm_g, b_conv_w, b_a_log,
                     b_dt_bias, b_norm_g, w_out_ab, w_in_cd, c_lam_re, c_lam_im, c_log_dt, c_b_re,
                     c_b_im, c_c_re, c_c_im, c_d, c_glu_w, c_glu_b, d_i_bias, d_f_bias, d_norm_g,
                     w_out_cd, n_log):
    ab_split = (A_KW, A_KW, A_VW, A_VW, A_GATE_RANK, B_QKV, B_VW, B_HEADS, B_HEADS)
    offs = [0]
    for s in ab_split:
        offs.append(offs[-1] + s)
    cols = lambda i: w_in_ab[:, offs[i]:offs[i + 1]]
    w_ab = jnp.concatenate([cols(0), cols(1), cols(2), cols(3), cols(5), cols(6),
                            cols(4), cols(7), cols(8)], axis=1)
    w_ab = _pad_cols(w_ab, _round_up(AB_SMALL + LANES, PROJ_TN)).astype(BF16)
    w_cd = _pad_cols(w_in_cd, _round_up(CD_SMALL + LANES, PROJ_TN)).astype(BF16)
    gate_w = jnp.zeros((LANES, A_KW), F32).at[AB_GA_LANE:AB_GA_LANE + A_GATE_RANK].set(
        a_gate_w.astype(F32)).astype(BF16)
    return dict(
        norm_g=norm_g.astype(F32), final_g=final_norm_g.astype(F32)[None, :],
        w_ab=w_ab, w_cd=w_cd, gate_w=gate_w, gate_b=a_gate_b.astype(F32)[None, :],
        a_norm_g=a_norm_g.astype(F32)[None, :], conv_w=b_conv_w.astype(F32),
        alog=_lane_row(b_a_log, AB_APRE_LANE), dtb=_lane_row(b_dt_bias, AB_APRE_LANE),
        b_norm_g=b_norm_g.astype(F32)[None, :],
        w_out_a=w_out_ab[:A_VW].astype(BF16), w_out_b=w_out_ab[A_VW:].astype(BF16),
        s5_ops=_s5_operators(c_lam_re.astype(F32), c_lam_im.astype(F32), c_log_dt, c_b_re.astype(F32),
                             c_b_im.astype(F32), c_c_re.astype(F32), c_c_im.astype(F32), n_log),
        c_d=c_d.astype(F32).reshape(1, C_W), glu_w=c_glu_w.astype(BF16),
        glu_b=c_glu_b.astype(F32)[None, :],
        ib=_lane_row(d_i_bias, CD_I_LANE), fb=_lane_row(d_f_bias, CD_F_LANE),
        d_norm_g=d_norm_g.astype(F32)[None, :],
        w_out_c=w_out_cd[:C_W].astype(BF16), w_out_d=w_out_cd[C_W:].astype(BF16),
    )


def _trunk(x, conv_prev, s_gla0, s_gdn0, s5_re0, s5_im0, mc0, mn0, mm0, w):
    bsz, l, d = x.shape
    c = min(CHUNK, l)
    tb = min(l, 8 * c)
    assert l % tb == 0 and l % S5_CHUNK == 0
    x2d = x.reshape(bsz * l, d)

    proj = _norm_matmul(x2d, w['norm_g'][0:1], w['w_ab'])
    proj3 = proj.reshape(bsz, l, proj.shape[1])
    o_a, s_gla = _gla(proj3, w['gate_w'], w['gate_b'], w['a_norm_g'], s_gla0.astype(F32), c=c, tb=tb)
    o_b, s_gdn = _gdn(proj3, conv_prev.astype(F32), w['conv_w'], w['alog'], w['dtb'], w['b_norm_g'],
                      s_gdn0.astype(F32), c=c, tb=tb)
    conv_new = proj3[:, l - (B_CONV - 1):, AB_QKV:AB_QKV + B_QKV]
    h1 = _out_proj(o_a.reshape(bsz * l, A_VW), o_b.reshape(bsz * l, B_VW), w['w_out_a'], w['w_out_b'], x2d)

    proj = _norm_matmul(h1, w['norm_g'][1:2], w['w_cd'])
    proj3 = proj.reshape(bsz, l, proj.shape[1])
    y, s5_re, s5_im = _s5(proj3, w['s5_ops'], s5_re0.astype(F32), s5_im0.astype(F32))
    o_c = _s5_glu(y, proj, w['c_d'], w['glu_w'], w['glu_b'])
    o_d, mc, mn, mm = _mlstm(proj3, w['ib'], w['fb'], w['d_norm_g'], mc0.astype(F32),
                             mn0.astype(F32)[:, :, None, :], mm0.astype(F32)[:, :, None, None], c=c, tb=tb)
    y_out = _out_proj(o_c, o_d.reshape(bsz * l, D_VW), w['w_out_c'], w['w_out_d'], h1, w['final_g'])
    dt = x.dtype
    return (y_out.reshape(bsz, l, d).astype(dt), conv_new.astype(dt), s_gla.astype(dt), s_gdn.astype(dt),
            s5_re.astype(dt), s5_im.astype(dt), mc.astype(dt), mn[:, :, 0, :].astype(dt),
            mm[:, :, 0, 0].astype(dt))


def kernel(x_prompt, x_sample, cache_gdn_conv, state_gla, state_gdn, state_s5_re, state_s5_im,
           state_mlstm_c, state_mlstm_n, state_mlstm_m, norm_g, final_norm_g, w_in_ab, a_gate_w,
           a_gate_b, a_norm_g, b_conv_w, b_a_log, b_dt_bias, b_norm_g, w_out_ab, w_in_cd, c_lam_re,
           c_lam_im, c_log_dt, c_b_re, c_b_im, c_c_re, c_c_im, c_d, c_glu_w, c_glu_b, d_i_bias,
           d_f_bias, d_norm_g, w_out_cd):
    n_log = int(math.log2(max(x_prompt.shape[1], x_sample.shape[1]) // S5_CHUNK))
    w = _prepare_weights(norm_g, final_norm_g, w_in_ab, a_gate_w, a_gate_b, a_norm_g, b_conv_w, b_a_log,
                         b_dt_bias, b_norm_g, w_out_ab, w_in_cd, c_lam_re, c_lam_im, c_log_dt, c_b_re,
                         c_b_im, c_c_re, c_c_im, c_d, c_glu_w, c_glu_b, d_i_bias, d_f_bias, d_norm_g,
                         w_out_cd, n_log)
    nb = x_prompt.shape[0]
    zeros = lambda *shape: jnp.zeros(shape, F32)
    p_out = _trunk(x_prompt, zeros(nb, B_CONV - 1, B_QKV), zeros(nb, A_HEADS, A_DK, A_DV),
                   zeros(nb, B_HEADS, B_DK, B_DV), zeros(nb, C_GROUPS, C_STATE), zeros(nb, C_GROUPS, C_STATE),
                   zeros(nb, D_HEADS, D_DK, D_DV), zeros(nb, D_HEADS, D_DK), zeros(nb, D_HEADS), w)
    s_out = _trunk(x_sample, cache_gdn_conv, state_gla, state_gdn, state_s5_re, state_s5_im,
                   state_mlstm_c, state_mlstm_n, state_mlstm_m, w)
    return (p_out[0], s_out[0]) + tuple(p_out[1:]) + tuple(s_out[1:])
```

```python
import functools
import math

import jax
import jax.numpy as jnp
from jax import lax
from jax.experimental import pallas as pl
from jax.experimental.pallas import tpu as pltpu

F32 = jnp.float32
BF16 = jnp.bfloat16
HIGHEST = lax.Precision.HIGHEST

NORM_EPS = 1e-6
CHUNK = 64
A_HEADS, A_DK, A_DV, A_GATE_RANK, A_GATE_TAU = 4, 128, 256, 16, 16.0
B_HEADS, B_DK, B_DV, B_CONV = 8, 128, 128, 4
C_GROUP, C_GROUPS, C_STATE = 16, 64, 64
D_HEADS, D_DK, D_DV = 4, 128, 256
A_KW, A_VW = A_HEADS * A_DK, A_HEADS * A_DV
B_KW, B_VW = B_HEADS * B_DK, B_HEADS * B_DV
B_QKV = 2 * B_KW + B_VW
C_W = C_GROUPS * C_GROUP
D_KW, D_VW = D_HEADS * D_DK, D_HEADS * D_DV

LANES = 128
SUBLANES = 8
VMEM_BYTES_V7X = 64 * 1024 * 1024

AB_Q, AB_K, AB_V, AB_Z = 0, A_KW, 2 * A_KW, 2 * A_KW + A_VW
AB_QKV = AB_Z + A_VW
AB_ZB = AB_QKV + B_QKV
AB_SMALL = AB_ZB + B_VW
AB_GA_LANE, AB_BETA_LANE, AB_APRE_LANE = 0, A_GATE_RANK, A_GATE_RANK + B_HEADS
CD_U, CD_Z = 0, C_W
CD_Q = 2 * C_W
CD_K = CD_Q + D_KW
CD_V = CD_K + D_KW
CD_O = CD_V + D_VW
CD_ZD = CD_O + D_VW
CD_SMALL = CD_ZD + D_VW
CD_I_LANE, CD_F_LANE = 0, D_HEADS

PROJ_TN = 1280
S5_CHUNK = 16
S5_GROUP_BLOCK = 8
CONV_SLAB = 512


def _round_up(x, m):
    return (x + m - 1) // m * m


def _vmem_limit(block_bytes, scratch_bytes=0):
    est = 2 * block_bytes + scratch_bytes
    return int(min(max(2 * est, 32 * 1024 * 1024), VMEM_BYTES_V7X - 8 * 1024 * 1024))


def _mm(a, b):
    return jnp.dot(a, b, preferred_element_type=F32)


def _dot(a, b):
    return _mm(a.astype(BF16), b.astype(BF16))


def _dot_nt(a, b):
    return lax.dot_general(a.astype(BF16), b.astype(BF16), (((1,), (1,)), ((), ())),
                           preferred_element_type=F32)


def _dot_tn(a, b):
    return lax.dot_general(a.astype(BF16), b.astype(BF16), (((0,), (0,)), ((), ())),
                           preferred_element_type=F32)


def _split2(x):
    hi = x.astype(BF16)
    return hi, (x - hi.astype(F32)).astype(BF16)


def _split3(x):
    hi = x.astype(BF16)
    r = x - hi.astype(F32)
    mid = r.astype(BF16)
    return hi, mid, (r - mid.astype(F32)).astype(BF16)


def _dot3(a, b):
    ah, al = _split2(a)
    bh, bl = _split2(b)
    return _mm(ah, bh) + _mm(ah, bl) + _mm(al, bh)


def _cumsum_rows(tri_bf16, x):
    hi, mid, lo = _split3(x)
    return _mm(tri_bf16, hi) + _mm(tri_bf16, mid) + _mm(tri_bf16, lo)


def _select_rows(sel_bf16, x):
    nt = lambda b: lax.dot_general(sel_bf16, b, (((1,), (1,)), ((), ())), preferred_element_type=F32)
    hi, mid, lo = _split3(x)
    return nt(hi) + nt(mid) + nt(lo)


def _lane_selector(lane0):
    r = lax.broadcasted_iota(jnp.int32, (SUBLANES, LANES), 0)
    l = lax.broadcasted_iota(jnp.int32, (SUBLANES, LANES), 1)
    return jnp.where(l == r + lane0, 1.0, 0.0).astype(BF16)


def _causal_masks(c):
    row = lax.broadcasted_iota(jnp.int32, (c, c), 0)
    col = lax.broadcasted_iota(jnp.int32, (c, c), 1)
    return row >= col, row > col


def _norm_matmul_kernel(x_ref, g_ref, w_ref, o_ref, xn_ref):
    @pl.when(pl.program_id(1) == 0)
    def _():
        x = x_ref[...]
        y = x * lax.rsqrt(jnp.mean(x * x, axis=-1, keepdims=True) + NORM_EPS) * g_ref[...]
        xn_ref[...] = y.astype(BF16)

    o_ref[...] = jnp.dot(xn_ref[...], w_ref[...], preferred_element_type=F32)


def _norm_matmul(x2d, g_row, w_bf16):
    m, d = x2d.shape
    n = w_bf16.shape[1]
    tm = min(m, 1024)
    tn = PROJ_TN
    assert m % tm == 0 and n % tn == 0
    blocks = tm * d * 4 + d * tn * 2 + tm * tn * 4
    return pl.pallas_call(
        _norm_matmul_kernel,
        out_shape=jax.ShapeDtypeStruct((m, n), F32),
        grid=(m // tm, n // tn),
        in_specs=[pl.BlockSpec((tm, d), lambda i, j: (i, 0)),
                  pl.BlockSpec((1, d), lambda i, j: (0, 0)),
                  pl.BlockSpec((d, tn), lambda i, j: (0, j))],
        out_specs=pl.BlockSpec((tm, tn), lambda i, j: (i, j)),
        scratch_shapes=[pltpu.VMEM((tm, d), BF16)],
        compiler_params=pltpu.CompilerParams(
            dimension_semantics=("parallel", "arbitrary"),
            vmem_limit_bytes=_vmem_limit(blocks, tm * d * 2)),
        name="norm_in_proj",
    )(x2d, g_row, w_bf16)


def _out_proj_kernel(a_ref, b_ref, wa_ref, wb_ref, h_ref, o_ref):
    out = (jnp.dot(a_ref[...], wa_ref[...], preferred_element_type=F32)
           + jnp.dot(b_ref[...], wb_ref[...], preferred_element_type=F32))
    o_ref[...] = h_ref[...] + out


def _out_proj_norm_kernel(a_ref, b_ref, wa_ref, wb_ref, h_ref, g_ref, o_ref):
    out = (jnp.dot(a_ref[...], wa_ref[...], preferred_element_type=F32)
           + jnp.dot(b_ref[...], wb_ref[...], preferred_element_type=F32))
    h = h_ref[...] + out
    o_ref[...] = h * lax.rsqrt(jnp.mean(h * h, axis=-1, keepdims=True) + NORM_EPS) * g_ref[...]


def _out_proj(mix_a, mix_b, w_a, w_b, h2d, final_g_row=None):
    m, d = h2d.shape
    ka, kb = mix_a.shape[1], mix_b.shape[1]
    tm = min(m, 512)
    assert m % tm == 0
    in_specs = [pl.BlockSpec((tm, ka), lambda i: (i, 0)),
                pl.BlockSpec((tm, kb), lambda i: (i, 0)),
                pl.BlockSpec((ka, d), lambda i: (0, 0)),
                pl.BlockSpec((kb, d), lambda i: (0, 0)),
                pl.BlockSpec((tm, d), lambda i: (i, 0))]
    args = [mix_a, mix_b, w_a, w_b, h2d]
    kernel = _out_proj_kernel
    if final_g_row is not None:
        in_specs.append(pl.BlockSpec((1, d), lambda i: (0, 0)))
        args.append(final_g_row)
        kernel = _out_proj_norm_kernel
    blocks = tm * (ka + kb) * 2 + (ka + kb) * d * 2 + 2 * tm * d * 4
    return pl.pallas_call(
        kernel,
        out_shape=jax.ShapeDtypeStruct((m, d), F32),
        grid=(m // tm,),
        in_specs=in_specs,
        out_specs=pl.BlockSpec((tm, d), lambda i: (i, 0)),
        compiler_params=pltpu.CompilerParams(
            dimension_semantics=("parallel",),
            vmem_limit_bytes=_vmem_limit(blocks)),
        name="out_proj_norm" if final_g_row is not None else "out_proj",
    )(*args)


def _gla_kernel(q_ref, k_ref, v_ref, z_ref, sm_ref, gw_ref, gb_ref, ng_ref, s0_ref,
                o_ref, sout_ref, st_ref, *, c, nchunks):
    t = pl.program_id(1)
    last_t = pl.num_programs(1) - 1

    @pl.when(t == 0)
    def _():
        for h in range(A_HEADS):
            st_ref[h] = s0_ref[h].T

    causal, _ = _causal_masks(c)
    tri = causal.astype(BF16)
    gw = gw_ref[...]
    gb = gb_ref[...]

    def body(n, carry):
        sl = pl.ds(pl.multiple_of(n * c, c), c)
        log_alpha = jax.nn.log_sigmoid(_dot(sm_ref[sl, :], gw) + gb) * (1.0 / A_GATE_TAU)
        b_all = _cumsum_rows(tri, log_alpha)
        heads = range(A_HEADS)
        ksl = [slice(h * A_DK, (h + 1) * A_DK) for h in heads]
        vsl = [slice(h * A_DV, (h + 1) * A_DV) for h in heads]
        b = [b_all[:, ksl[h]] for h in heads]
        b_last = [b[h][c - 1:c, :] for h in heads]
        k = [k_ref[sl, ksl[h]] for h in heads]
        v = [v_ref[sl, vsl[h]].astype(BF16) for h in heads]
        q_dec = [(q_ref[sl, ksl[h]] * (A_DK ** -0.5) * jnp.exp(b[h])).astype(BF16) for h in heads]
        k_dec = [(k[h] * jnp.exp(-b[h])).astype(BF16) for h in heads]
        k_w = [(k[h] * jnp.exp(b_last[h] - b[h])).astype(BF16) for h in heads]
        scores = [jnp.where(causal, _dot_nt(q_dec[h], k_dec[h]), 0.0).astype(BF16) for h in heads]
        s_t = [st_ref[h] for h in heads]
        outs = [_mm(scores[h], v[h]) + _dot_nt(q_dec[h], s_t[h]) for h in heads]
        for h in heads:
            st_ref[h] = s_t[h] * jnp.exp(b_last[h]) + _dot_tn(v[h], k_w[h])
        for h in heads:
            o = outs[h]
            o = o * lax.rsqrt(jnp.mean(o * o, axis=-1, keepdims=True) + NORM_EPS) * ng_ref[:, vsl[h]]
            o_ref[sl, vsl[h]] = (o * jax.nn.silu(z_ref[sl, vsl[h]])).astype(o_ref.dtype)
        return carry

    lax.fori_loop(0, nchunks, body, 0)

    @pl.when(t == last_t)
    def _():
        for h in range(A_HEADS):
            sout_ref[h] = st_ref[h].T


def _gla(proj3, gate_w_pad, gate_b_row, norm_g_row, s0, *, c, tb):
    bsz, l, _ = proj3.shape
    nblk = l // tb
    tok = lambda col: (lambda b, t: (b, t, col))
    in_specs = [
        pl.BlockSpec((None, tb, A_KW), tok(AB_Q // A_KW)),
        pl.BlockSpec((None, tb, A_KW), tok(AB_K // A_KW)),
        pl.BlockSpec((None, tb, A_VW), tok(AB_V // A_VW)),
        pl.BlockSpec((None, tb, A_VW), tok(AB_Z // A_VW)),
        pl.BlockSpec((None, tb, LANES), tok(AB_SMALL // LANES)),
        pl.BlockSpec((LANES, A_KW), lambda b, t: (0, 0)),
        pl.BlockSpec((1, A_KW), lambda b, t: (0, 0)),
        pl.BlockSpec((1, A_VW), lambda b, t: (0, 0)),
        pl.BlockSpec((None, A_HEADS, A_DK, A_DV), lambda b, t: (b, 0, 0, 0)),
    ]
    out_specs = [
        pl.BlockSpec((None, tb, A_VW), lambda b, t: (b, t, 0)),
        pl.BlockSpec((None, A_HEADS, A_DK, A_DV), lambda b, t: (b, 0, 0, 0)),
    ]
    blocks = tb * (2 * A_KW + 2 * A_VW + LANES) * 4 + tb * A_VW * 2 + 2 * A_HEADS * A_DK * A_DV * 4
    return pl.pallas_call(
        functools.partial(_gla_kernel, c=c, nchunks=tb // c),
        out_shape=[jax.ShapeDtypeStruct((bsz, l, A_VW), BF16),
                   jax.ShapeDtypeStruct((bsz, A_HEADS, A_DK, A_DV), F32)],
        grid=(bsz, nblk),
        in_specs=in_specs,
        out_specs=out_specs,
        scratch_shapes=[pltpu.VMEM((A_HEADS, A_DV, A_DK), F32)],
        compiler_params=pltpu.CompilerParams(
            dimension_semantics=("parallel", "arbitrary"),
            vmem_limit_bytes=_vmem_limit(blocks, A_HEADS * A_DK * A_DV * 4)),
        name="gla_mixer",
    )(proj3, proj3, proj3, proj3, proj3, gate_w_pad, gate_b_row, norm_g_row, s0)


def _gdn_kernel(x_ref, z_ref, sm_ref, w_ref, cp_ref, alog_ref, dtb_ref, ng_ref, s0_ref,
                o_ref, sout_ref, s_ref, tail_ref, conv_ref, u_ref, wm_ref, qg_ref, kg_ref, qk_ref, gl_ref,
                *, c, nchunks):
    t = pl.program_id(1)
    last_t = pl.num_programs(1) - 1
    tb = c * nchunks
    keep = SUBLANES - (B_CONV - 1)

    @pl.when(t == 0)
    def _():
        s_ref[...] = s0_ref[...]
        tail_ref[0:keep, :] = jnp.zeros((keep, B_QKV), F32)
        tail_ref[keep:SUBLANES, :] = cp_ref[...]

    for j in range(0, B_QKV, CONV_SLAB):
        cs = slice(j, j + CONV_SLAB)
        x = x_ref[:, cs]
        w = w_ref[:, cs]
        ext = jnp.concatenate([tail_ref[:, cs], x], axis=0)
        conv = ext[keep:keep + tb, :] * w[0:1, :]
        conv = conv + ext[keep + 1:keep + 1 + tb, :] * w[1:2, :]
        conv = conv + ext[keep + 2:keep + 2 + tb, :] * w[2:3, :]
        conv = conv + x * w[3:4, :]
        conv_ref[:, cs] = jax.nn.silu(conv)
        tail_ref[:, cs] = x[tb - SUBLANES:tb, :]

    causal, strict = _causal_masks(c)
    tri = causal.astype(BF16)
    eye = jnp.where(causal & jnp.logical_not(strict), 1.0, 0.0).astype(F32)
    sel = _lane_selector(AB_APRE_LANE)
    neg_a_exp = -jnp.exp(alog_ref[...])
    dtb = dtb_ref[...]
    ng = ng_ref[...]
    n_double = int(math.log2(c)) - 1
    heads = range(B_HEADS)

    def wy_factors(n, carry):
        sl = pl.ds(pl.multiple_of(n * c, c), c)
        sm = sm_ref[sl, :]
        g_cum = _cumsum_rows(tri, neg_a_exp * jax.nn.softplus(sm + dtb))
        g_rows = _select_rows(sel, g_cum)
        beta_all = jax.nn.sigmoid(sm)
        gl_ref[pl.ds(n, 1), :] = g_cum[c - 1:c, :]
        pws, rhss = [], []
        for h in heads:
            hs = slice(h * B_DK, (h + 1) * B_DK)
            q = conv_ref[sl, hs]
            k = conv_ref[sl, B_KW + h * B_DK:B_KW + (h + 1) * B_DK]
            v = conv_ref[sl, 2 * B_KW + h * B_DV:2 * B_KW + (h + 1) * B_DV]
            q = q * lax.rsqrt(jnp.sum(q * q, axis=-1, keepdims=True) + NORM_EPS) * (B_DK ** -0.5)
            k = k * lax.rsqrt(jnp.sum(k * k, axis=-1, keepdims=True) + NORM_EPS)
            g_col = g_cum[:, AB_APRE_LANE + h:AB_APRE_LANE + h + 1]
            beta = beta_all[:, AB_BETA_LANE + h:AB_BETA_LANE + h + 1]
            decay = jnp.exp(jnp.where(causal, g_col - g_rows[h:h + 1, :], -jnp.inf))
            e_g = jnp.exp(g_col)
            k_beta = k * beta
            pws.append(-jnp.where(strict, _dot_nt(k_beta, k) * decay, 0.0))
            rhss.append(jnp.concatenate([v * beta, k_beta * e_g], axis=1))
            qk_ref[sl, h * LANES:h * LANES + c] = jnp.where(causal, _dot_nt(q, k) * decay, 0.0).astype(BF16)
            qg_ref[sl, hs] = (q * e_g).astype(BF16)
            kg_ref[sl, hs] = (k * jnp.exp(g_col[c - 1:c, :] - g_col)).astype(BF16)
        invs = [eye + p for p in pws]
        for _ in range(n_double):
            pws = [_dot3(p, p) for p in pws]
            invs = [i + _dot3(i, p) for i, p in zip(invs, pws)]
        for h in heads:
            hs = slice(h * B_DK, (h + 1) * B_DK)
            uw = _dot3(invs[h], rhss[h])
            u_ref[sl, hs] = uw[:, :B_DV]
            wm_ref[sl, hs] = uw[:, B_DV:].astype(BF16)
        return carry

    lax.fori_loop(0, nchunks, wy_factors, 0)

    def recurrence(n, carry):
        sl = pl.ds(pl.multiple_of(n * c, c), c)
        e_last = jnp.exp(gl_ref[pl.ds(n, 1), :])
        hsl = [slice(h * B_DK, (h + 1) * B_DK) for h in heads]
        s_old = [s_ref[h] for h in heads]
        s_bf = [s.astype(BF16) for s in s_old]
        v_new = [(u_ref[sl, hsl[h]] - _mm(wm_ref[sl, hsl[h]], s_bf[h])).astype(BF16) for h in heads]
        outs = [_mm(qg_ref[sl, hsl[h]], s_bf[h]) + _mm(qk_ref[sl, h * LANES:h * LANES + c], v_new[h])
                for h in heads]
        for h in heads:
            s_ref[h] = (e_last[:, AB_APRE_LANE + h:AB_APRE_LANE + h + 1] * s_old[h]
                        + lax.dot_general(kg_ref[sl, hsl[h]], v_new[h], (((0,), (0,)), ((), ())),
                                          preferred_element_type=F32))
        for h in heads:
            o = outs[h]
            o = o * lax.rsqrt(jnp.mean(o * o, axis=-1, keepdims=True) + NORM_EPS) * ng
            o_ref[sl, hsl[h]] = (o * jax.nn.silu(z_ref[sl, hsl[h]])).astype(o_ref.dtype)
        return carry

    lax.fori_loop(0, nchunks, recurrence, 0)

    @pl.when(t == last_t)
    def _():
        sout_ref[...] = s_ref[...]


def _gdn(proj3, conv_prev, conv_w, alog_row, dtb_row, norm_g_row, s0, *, c, tb):
    bsz, l, _ = proj3.shape
    nblk = l // tb
    assert AB_QKV % B_QKV == 0 and AB_ZB % B_VW == 0
    in_specs = [
        pl.BlockSpec((None, tb, B_QKV), lambda b, t: (b, t, AB_QKV // B_QKV)),
        pl.BlockSpec((None, tb, B_VW), lambda b, t: (b, t, AB_ZB // B_VW)),
        pl.BlockSpec((None, tb, LANES), lambda b, t: (b, t, AB_SMALL // LANES)),
        pl.BlockSpec((B_CONV, B_QKV), lambda b, t: (0, 0)),
        pl.BlockSpec((None, B_CONV - 1, B_QKV), lambda b, t: (b, 0, 0)),
        pl.BlockSpec((1, LANES), lambda b, t: (0, 0)),
        pl.BlockSpec((1, LANES), lambda b, t: (0, 0)),
        pl.BlockSpec((1, B_DV), lambda b, t: (0, 0)),
        pl.BlockSpec((None, B_HEADS, B_DK, B_DV), lambda b, t: (b, 0, 0, 0)),
    ]
    out_specs = [
        pl.BlockSpec((None, tb, B_VW), lambda b, t: (b, t, 0)),
        pl.BlockSpec((None, B_HEADS, B_DK, B_DV), lambda b, t: (b, 0, 0, 0)),
    ]
    blocks = tb * (B_QKV + B_VW + LANES) * 4 + tb * B_VW * 2 + 2 * B_HEADS * B_DK * B_DV * 4
    scratch = (B_HEADS * B_DK * B_DV * 4 + SUBLANES * B_QKV * 4 + tb * B_QKV * 4 + tb * B_VW * 4
               + 4 * tb * B_KW * 2 + SUBLANES * LANES * 4)
    return pl.pallas_call(
        functools.partial(_gdn_kernel, c=c, nchunks=tb // c),
        out_shape=[jax.ShapeDtypeStruct((bsz, l, B_VW), BF16),
                   jax.ShapeDtypeStruct((bsz, B_HEADS, B_DK, B_DV), F32)],
        grid=(bsz, nblk),
        in_specs=in_specs,
        out_specs=out_specs,
        scratch_shapes=[pltpu.VMEM((B_HEADS, B_DK, B_DV), F32),
                        pltpu.VMEM((SUBLANES, B_QKV), F32),
                        pltpu.VMEM((tb, B_QKV), F32),
                        pltpu.VMEM((tb, B_VW), F32),
                        pltpu.VMEM((tb, B_KW), BF16),
                        pltpu.VMEM((tb, B_KW), BF16),
                        pltpu.VMEM((tb, B_KW), BF16),
                        pltpu.VMEM((tb, B_HEADS * LANES), BF16),
                        pltpu.VMEM((SUBLANES, LANES), F32)],
        compiler_params=pltpu.CompilerParams(
            dimension_semantics=("parallel", "arbitrary"),
            vmem_limit_bytes=_vmem_limit(blocks, scratch)),
        name="gdn_mixer",
    )(proj3, proj3, proj3, conv_w, conv_prev, alog_row, dtb_row, norm_g_row, s0)


def _mlstm_kernel(q_ref, k_ref, v_ref, og_ref, z_ref, sm_ref, ib_ref, fb_ref, ng_ref,
                  c0_ref, n0_ref, m0_ref, o_ref, cout_ref, nout_ref, mout_ref,
                  c_ref, n_ref, m_ref, *, c, nchunks):
    t = pl.program_id(1)
    last_t = pl.num_programs(1) - 1

    @pl.when(t == 0)
    def _():
        c_ref[...] = c0_ref[...]
        n_ref[...] = n0_ref[...]
        m_ref[...] = m0_ref[...]

    causal, _ = _causal_masks(c)
    tri = causal.astype(BF16)
    sel = _lane_selector(0)
    lane = lax.broadcasted_iota(jnp.int32, (c, LANES), 1)
    ib = ib_ref[...]
    fb = fb_ref[...]

    def body(n, carry):
        sl = pl.ds(pl.multiple_of(n * c, c), c)
        sm = sm_ref[sl, :]
        i_full = sm + ib
        b_full = _cumsum_rows(tri, jax.nn.log_sigmoid(sm + fb))
        rows = _select_rows(sel, jnp.where(lane < CD_F_LANE, i_full, b_full))
        heads = range(D_HEADS)
        ksl = [slice(h * D_DK, (h + 1) * D_DK) for h in heads]
        vsl = [slice(h * D_DV, (h + 1) * D_DV) for h in heads]
        b_col = [b_full[:, CD_F_LANE + h:CD_F_LANE + h + 1] for h in heads]
        i_col = [i_full[:, CD_I_LANE + h:CD_I_LANE + h + 1] for h in heads]
        logw = [jnp.where(causal, b_col[h] - rows[CD_F_LANE + h:CD_F_LANE + h + 1, :]
                          + rows[CD_I_LANE + h:CD_I_LANE + h + 1, :], -jnp.inf) for h in heads]
        m_intra = [jnp.max(logw[h], axis=-1, keepdims=True) for h in heads]
        q = [q_ref[sl, ksl[h]] * (D_DK ** -0.5) for h in heads]
        q_bf = [x.astype(BF16) for x in q]
        k = [k_ref[sl, ksl[h]] for h in heads]
        v = [v_ref[sl, vsl[h]].astype(BF16) for h in heads]
        p = [jnp.exp(logw[h] - m_intra[h]) * _dot_nt(q_bf[h], k[h]) for h in heads]
        h_intra = [_dot(p[h], v[h]) for h in heads]
        n_intra = [jnp.sum(p[h], axis=-1, keepdims=True) for h in heads]
        b_last = [b_col[h][c - 1:c, :] for h in heads]
        m_chunk = [m_intra[h][c - 1:c, :] for h in heads]
        k_w = [k[h] * jnp.exp(b_last[h] - b_col[h] + i_col[h] - m_chunk[h]) for h in heads]
        kv = [_dot_tn(k_w[h], v[h]) for h in heads]
        k_sum = [jnp.sum(k_w[h], axis=0, keepdims=True) for h in heads]
        c_s = [c_ref[h] for h in heads]
        n_s = [n_ref[h:h + 1, :] for h in heads]
        m_s = [m_ref[:, h:h + 1] for h in heads]
        qc = [_dot(q_bf[h], c_s[h]) for h in heads]
        for h in heads:
            m_new = jnp.maximum(b_last[h] + m_s[h], m_chunk[h])
            w_old = jnp.exp(b_last[h] + m_s[h] - m_new)
            w_new = jnp.exp(m_chunk[h] - m_new)
            c_ref[h] = w_old * c_s[h] + w_new * kv[h]
            n_ref[h:h + 1, :] = w_old * n_s[h] + w_new * k_sum[h]
            m_ref[:, h:h + 1] = m_new
        for h in heads:
            a = b_col[h] + m_s[h]
            m_t = jnp.maximum(a, m_intra[h])
            w_a = jnp.exp(a - m_t)
            w_i = jnp.exp(m_intra[h] - m_t)
            num = w_a * qc[h] + w_i * h_intra[h]
            den = w_a * jnp.sum(q[h] * n_s[h], axis=-1, keepdims=True) + w_i * n_intra[h]
            hh = num / jnp.maximum(jnp.abs(den), jnp.exp(-m_t))
            hd = jax.nn.sigmoid(og_ref[sl, vsl[h]]) * hh
            oc = hd - jnp.mean(hd, axis=-1, keepdims=True)
            o = oc * lax.rsqrt(jnp.mean(oc * oc, axis=-1, keepdims=True) + NORM_EPS) * ng_ref[:, vsl[h]]
            o_ref[sl, vsl[h]] = (o * jax.nn.silu(z_ref[sl, vsl[h]])).astype(o_ref.dtype)
        return carry

    lax.fori_loop(0, nchunks, body, 0)

    @pl.when(t == last_t)
    def _():
        cout_ref[...] = c_ref[...]
        nout_ref[...] = n_ref[...]
        mout_ref[...] = m_ref[...]


def _mlstm(proj3, ib_row, fb_row, norm_g_row, c0, n0, m0, *, c, tb):
    bsz, l, _ = proj3.shape
    nblk = l // tb
    tok = lambda col: (lambda b, t: (b, t, col))
    st4 = lambda b, t: (b, 0, 0, 0)
    st3 = lambda b, t: (b, 0, 0)
    in_specs = [
        pl.BlockSpec((None, tb, D_KW), tok(CD_Q // D_KW)),
        pl.BlockSpec((None, tb, D_KW), tok(CD_K // D_KW)),
        pl.BlockSpec((None, tb, D_VW), tok(CD_V // D_VW)),
        pl.BlockSpec((None, tb, D_VW), tok(CD_O // D_VW)),
        pl.BlockSpec((None, tb, D_VW), tok(CD_ZD // D_VW)),
        pl.BlockSpec((None, tb, LANES), tok(CD_SMALL // LANES)),
        pl.BlockSpec((1, LANES), lambda b, t: (0, 0)),
        pl.BlockSpec((1, LANES), lambda b, t: (0, 0)),
        pl.BlockSpec((1, D_VW), lambda b, t: (0, 0)),
        pl.BlockSpec((None, D_HEADS, D_DK, D_DV), st4),
        pl.BlockSpec((None, D_HEADS, D_DK), st3),
        pl.BlockSpec((None, 1, D_HEADS), st3),
    ]
    out_specs = [
        pl.BlockSpec((None, tb, D_VW), lambda b, t: (b, t, 0)),
        pl.BlockSpec((None, D_HEADS, D_DK, D_DV), st4),
        pl.BlockSpec((None, D_HEADS, D_DK), st3),
        pl.BlockSpec((None, 1, D_HEADS), st3),
    ]
    blocks = tb * (2 * D_KW + 3 * D_VW + LANES) * 4 + tb * D_VW * 2 + 2 * D_HEADS * D_DK * D_DV * 4
    return pl.pallas_call(
        functools.partial(_mlstm_kernel, c=c, nchunks=tb // c),
        out_shape=[jax.ShapeDtypeStruct((bsz, l, D_VW), BF16),
                   jax.ShapeDtypeStruct((bsz, D_HEADS, D_DK, D_DV), F32),
                   jax.ShapeDtypeStruct((bsz, D_HEADS, D_DK), F32),
                   jax.ShapeDtypeStruct((bsz, 1, D_HEADS), F32)],
        grid=(bsz, nblk),
        in_specs=in_specs,
        out_specs=out_specs,
        scratch_shapes=[pltpu.VMEM((D_HEADS, D_DK, D_DV), F32),
                        pltpu.VMEM((D_HEADS, D_DK), F32),
                        pltpu.VMEM((1, D_HEADS), F32)],
        compiler_params=pltpu.CompilerParams(
            dimension_semantics=("parallel", "arbitrary"),
            vmem_limit_bytes=_vmem_limit(blocks, D_HEADS * D_DK * D_DV * 4)),
        name="mlstm_mixer",
    )(proj3, proj3, proj3, proj3, proj3, proj3, ib_row, fb_row, norm_g_row, c0, n0, m0)


def _s5_kernel(u_ref, km_ref, bst_ref, cst_ref, apow_ref, x0_ref, y_ref, xf_ref, *, nc, bb):
    m = nc * bb
    gb = u_ref.shape[0]
    row = lax.broadcasted_iota(jnp.int32, (m, 2 * C_STATE), 0)
    n_idx = row & (nc - 1)
    n_log = int(math.log2(nc))

    def cmul(a1, a2, x):
        return a1 * x + a2 * pltpu.roll(x, C_STATE, 1)

    for g in range(gb):
        u = u_ref[g]
        x0_rows = jnp.zeros((m, 2 * C_STATE), F32)
        for b in range(bb):
            x0_rows = jnp.where(row == b * nc, x0_ref[g, b:b + 1, :], x0_rows)
        x = (jnp.dot(u, bst_ref[g], preferred_element_type=F32)
             + cmul(apow_ref[g, 0:1, :], apow_ref[g, 1:2, :], x0_rows))
        for j in range(n_log):
            sh = 1 << j
            shifted = jnp.where(n_idx >= sh, pltpu.roll(x, sh, 0), 0.0)
            x = x + cmul(apow_ref[g, 2 * j:2 * j + 1, :], apow_ref[g, 2 * j + 1:2 * j + 2, :], shifted)
        x_start = jnp.where(n_idx >= 1, pltpu.roll(x, 1, 0), x0_rows)
        y_ref[g] = (jnp.dot(u, km_ref[g], preferred_element_type=F32)
                    + jnp.dot(x_start.astype(BF16), cst_ref[g], preferred_element_type=F32))
        for b in range(bb):
            xf_ref[g, b:b + 1, :] = x[b * nc + nc - 1:b * nc + nc, :]


def _s5_chunks(u_t, kmat, bst, cst, apow, x0, *, nc, bb):
    g, m_total, w = u_t.shape
    m = nc * bb
    nb = m_total // m
    gb = S5_GROUP_BLOCK
    sw = 2 * C_STATE
    blocks = gb * (m * w * 2 + w * w * 2 + 2 * w * sw * 2 + apow.shape[1] * sw * 4
                   + 2 * bb * sw * 4 + m * w * 4)
    return pl.pallas_call(
        functools.partial(_s5_kernel, nc=nc, bb=bb),
        out_shape=[jax.ShapeDtypeStruct((g, m_total, w), F32),
                   jax.ShapeDtypeStruct((g, nb * bb, sw), F32)],
        grid=(g // gb, nb),
        in_specs=[pl.BlockSpec((gb, m, w), lambda i, j: (i, j, 0)),
                  pl.BlockSpec((gb, w, w), lambda i, j: (i, 0, 0)),
                  pl.BlockSpec((gb, w, sw), lambda i, j: (i, 0, 0)),
                  pl.BlockSpec((gb, sw, w), lambda i, j: (i, 0, 0)),
                  pl.BlockSpec((gb, apow.shape[1], sw), lambda i, j: (i, 0, 0)),
                  pl.BlockSpec((gb, bb, sw), lambda i, j: (i, j, 0))],
        out_specs=[pl.BlockSpec((gb, m, w), lambda i, j: (i, j, 0)),
                   pl.BlockSpec((gb, bb, sw), lambda i, j: (i, j, 0))],
        compiler_params=pltpu.CompilerParams(
            dimension_semantics=("parallel", "parallel"),
            vmem_limit_bytes=_vmem_limit(blocks)),
        name="s5_chunks",
    )(u_t, kmat, bst, cst, apow, x0)


def _s5_glu_kernel(y_ref, u_ref, z_ref, d_ref, w_ref, b_ref, o_ref):
    y = jax.nn.gelu(y_ref[...] + d_ref[...] * u_ref[...])
    gate = jax.nn.sigmoid(jnp.dot(y.astype(BF16), w_ref[...], preferred_element_type=F32) + b_ref[...])
    o_ref[...] = (y * gate * jax.nn.silu(z_ref[...])).astype(o_ref.dtype)


def _s5_glu(y2d, proj2d, d_row, glu_w, glu_b_row):
    m = y2d.shape[0]
    tm = min(m, 512)
    blocks = 3 * tm * C_W * 4 + C_W * C_W * 2 + tm * C_W * 2
    return pl.pallas_call(
        _s5_glu_kernel,
        out_shape=jax.ShapeDtypeStruct((m, C_W), BF16),
        grid=(m // tm,),
        in_specs=[pl.BlockSpec((tm, C_W), lambda i: (i, 0)),
                  pl.BlockSpec((tm, C_W), lambda i: (i, CD_U // C_W)),
                  pl.BlockSpec((tm, C_W), lambda i: (i, CD_Z // C_W)),
                  pl.BlockSpec((1, C_W), lambda i: (0, 0)),
                  pl.BlockSpec((C_W, C_W), lambda i: (0, 0)),
                  pl.BlockSpec((1, C_W), lambda i: (0, 0))],
        out_specs=pl.BlockSpec((tm, C_W), lambda i: (i, 0)),
        compiler_params=pltpu.CompilerParams(
            dimension_semantics=("parallel",),
            vmem_limit_bytes=_vmem_limit(blocks)),
        name="s5_glu",
    )(y2d, proj2d, proj2d, d_row, glu_w, glu_b_row)


def _s5_operators(lam_re, lam_im, log_dt, b_re, b_im, c_re, c_im, n_log):
    g, p = lam_re.shape
    tc = S5_CHUNK
    dt = jnp.exp(log_dt.astype(F32))[:, None]
    mag = jnp.exp(lam_re * dt)
    ab_re, ab_im = mag * jnp.cos(lam_im * dt), mag * jnp.sin(lam_im * dt)
    den = lam_re * lam_re + lam_im * lam_im
    er = ab_re - 1.0
    zr = (er * lam_re + ab_im * lam_im) / den
    zi = (ab_im * lam_re - er * lam_im) / den
    bb_re = zr[..., None] * b_re - zi[..., None] * b_im
    bb_im = zr[..., None] * b_im + zi[..., None] * b_re
    pr, pi = [jnp.ones_like(ab_re)], [jnp.zeros_like(ab_re)]
    for _ in range(tc):
        pr, pi = pr + [pr[-1] * ab_re - pi[-1] * ab_im], pi + [pr[-1] * ab_im + pi[-1] * ab_re]
    pw_re, pw_im = jnp.stack(pr), jnp.stack(pi)
    abr = pw_re[:tc, :, :, None] * bb_re - pw_im[:tc, :, :, None] * bb_im
    abi = pw_re[:tc, :, :, None] * bb_im + pw_im[:tc, :, :, None] * bb_re
    kern = (jnp.einsum('gjp,dgpi->dgji', c_re, abr, precision=HIGHEST)
            - jnp.einsum('gjp,dgpi->dgji', c_im, abi, precision=HIGHEST))
    tau = jnp.arange(tc)
    delta = tau[None, :] - tau[:, None]
    kt = jnp.where((delta >= 0)[:, :, None, None, None], kern[jnp.clip(delta, 0, tc - 1)], 0.0)
    kmat = kt.transpose(2, 0, 4, 1, 3).reshape(g, tc * C_GROUP, tc * C_GROUP)
    to_rows = lambda a: a[::-1].transpose(1, 0, 3, 2).reshape(g, tc * C_GROUP, p)
    bst = jnp.concatenate([to_rows(abr), to_rows(abi)], axis=-1)
    cr = c_re[None] * pw_re[1:, :, None, :] - c_im[None] * pw_im[1:, :, None, :]
    ci = -(c_re[None] * pw_im[1:, :, None, :] + c_im[None] * pw_re[1:, :, None, :])
    to_cols = lambda a: a.transpose(1, 3, 0, 2).reshape(g, p, tc * C_GROUP)
    cst = jnp.concatenate([to_cols(cr), to_cols(ci)], axis=1)
    r, i = pw_re[tc], pw_im[tc]
    rows = []
    for _ in range(max(n_log, 1)):
        rows += [jnp.concatenate([r, r], -1), jnp.concatenate([-i, i], -1)]
        r, i = r * r - i * i, 2.0 * r * i
    apow = jnp.stack(rows, axis=1)
    return kmat.astype(BF16), bst.astype(BF16), cst.astype(BF16), apow


def _s5(proj3, ops, x0_re, x0_im):
    kmat, bst, cst, apow = ops
    bsz, l, _ = proj3.shape
    tc = S5_CHUNK
    nc = l // tc
    u = proj3[:, :, CD_U:CD_U + C_W].astype(BF16)
    u_t = u.reshape(bsz, nc, tc, C_GROUPS, C_GROUP).transpose(3, 0, 1, 2, 4).reshape(
        C_GROUPS, bsz * nc, tc * C_GROUP)
    x0 = jnp.concatenate([x0_re, x0_im], axis=-1).transpose(1, 0, 2)
    bb = bsz if bsz * nc <= 512 else max(1, 512 // nc)
    y_t, xf = _s5_chunks(u_t, kmat, bst, cst, apow, x0, nc=nc, bb=bb)
    y = y_t.reshape(C_GROUPS, bsz, nc, tc, C_GROUP).transpose(1, 2, 3, 0, 4).reshape(bsz * l, C_W)
    xf = xf.transpose(1, 0, 2)
    return y, xf[..., :C_STATE], xf[..., C_STATE:]


def _lane_row(vals, lane0):
    return jnp.zeros((1, LANES), F32).at[0, lane0:lane0 + vals.shape[0]].set(vals.astype(F32))


def _prepare_weights(norm_g, final_norm_g, w_in_ab, a_gate_w, a_gate_b, a_norm_g, b_conv_w, b_a_log,
                     b_dt_bias, b_norm_g, w_out_ab, w_in_cd, c_lam_re, c_lam_im, c_log_dt, c_b_re,
                     c_b_im, c_c_re, c_c_im, c_d, c_glu_w, c_glu_b, d_i_bias, d_f_bias, d_norm_g,
                     w_out_cd, n_log):
    ab_split = (A_KW, A_KW, A_VW, A_VW, A_GATE_RANK, B_QKV, B_VW, B_HEADS, B_HEADS)
    offs = [0]
    for s in ab_split:
        offs.append(offs[-1] + s)
    d_model = w_in_ab.shape[0]
    cols = lambda i: w_in_ab[:, offs[i]:offs[i + 1]].astype(BF16)
    n_ab = _round_up(AB_SMALL + LANES, PROJ_TN)
    w_ab = jnp.concatenate([cols(0), cols(1), cols(2), cols(3), cols(5), cols(6), cols(4), cols(7), cols(8),
                            jnp.zeros((d_model, n_ab - w_in_ab.shape[1]), BF16)], axis=1)
    n_cd = _round_up(CD_SMALL + LANES, PROJ_TN)
    w_cd = jnp.concatenate([w_in_cd.astype(BF16), jnp.zeros((d_model, n_cd - w_in_cd.shape[1]), BF16)], axis=1)
    gate_w = jnp.zeros((LANES, A_KW), F32).at[AB_GA_LANE:AB_GA_LANE + A_GATE_RANK].set(
        a_gate_w.astype(F32)).astype(BF16)
    return dict(
        norm_g=norm_g.astype(F32), final_g=final_norm_g.astype(F32)[None, :],
        w_ab=w_ab, w_cd=w_cd, gate_w=gate_w, gate_b=a_gate_b.astype(F32)[None, :],
        a_norm_g=a_norm_g.astype(F32)[None, :], conv_w=b_conv_w.astype(F32),
        alog=_lane_row(b_a_log, AB_APRE_LANE), dtb=_lane_row(b_dt_bias, AB_APRE_LANE),
        b_norm_g=b_norm_g.astype(F32)[None, :],
        w_out_a=w_out_ab[:A_VW].astype(BF16), w_out_b=w_out_ab[A_VW:].astype(BF16),
        s5_ops=_s5_operators(c_lam_re.astype(F32), c_lam_im.astype(F32), c_log_dt, c_b_re.astype(F32),
                             c_b_im.astype(F32), c_c_re.astype(F32), c_c_im.astype(F32), n_log),
        c_d=c_d.astype(F32).reshape(1, C_W), glu_w=c_glu_w.astype(BF16),
        glu_b=c_glu_b.astype(F32)[None, :],
        ib=_lane_row(d_i_bias, CD_I_LANE), fb=_lane_row(d_f_bias, CD_F_LANE),
        d_norm_g=d_norm_g.astype(F32)[None, :],
        w_out_c=w_out_cd[:C_W].astype(BF16), w_out_d=w_out_cd[C_W:].astype(BF16),
    )


def _trunk(x, conv_prev, s_gla0, s_gdn0, s5_re0, s5_im0, mc0, mn0, mm0, w):
    bsz, l, d = x.shape
    c = min(CHUNK, l)
    tb = min(l, 8 * c)
    assert l % tb == 0 and l % S5_CHUNK == 0
    x2d = x.reshape(bsz * l, d)

    proj = _norm_matmul(x2d, w['norm_g'][0:1], w['w_ab'])
    proj3 = proj.reshape(bsz, l, proj.shape[1])
    o_a, s_gla = _gla(proj3, w['gate_w'], w['gate_b'], w['a_norm_g'], s_gla0.astype(F32), c=c, tb=tb)
    o_b, s_gdn = _gdn(proj3, conv_prev.astype(F32), w['conv_w'], w['alog'], w['dtb'], w['b_norm_g'],
                      s_gdn0.astype(F32), c=c, tb=tb)
    conv_new = proj3[:, l - (B_CONV - 1):, AB_QKV:AB_QKV + B_QKV]
    h1 = _out_proj(o_a.reshape(bsz * l, A_VW), o_b.reshape(bsz * l, B_VW), w['w_out_a'], w['w_out_b'], x2d)

    proj = _norm_matmul(h1, w['norm_g'][1:2], w['w_cd'])
    proj3 = proj.reshape(bsz, l, proj.shape[1])
    y, s5_re, s5_im = _s5(proj3, w['s5_ops'], s5_re0.astype(F32), s5_im0.astype(F32))
    o_c = _s5_glu(y, proj, w['c_d'], w['glu_w'], w['glu_b'])
    o_d, mc, mn, mm = _mlstm(proj3, w['ib'], w['fb'], w['d_norm_g'], mc0.astype(F32),
                             mn0.astype(F32), mm0.astype(F32)[:, None, :], c=c, tb=tb)
    y_out = _out_proj(o_c, o_d.reshape(bsz * l, D_VW), w['w_out_c'], w['w_out_d'], h1, w['final_g'])
    dt = x.dtype
    return (y_out.reshape(bsz, l, d).astype(dt), conv_new.astype(dt), s_gla.astype(dt), s_gdn.astype(dt),
            s5_re.astype(dt), s5_im.astype(dt), mc.astype(dt), mn.astype(dt), mm[:, 0, :].astype(dt))


def kernel(x_prompt, x_sample, cache_gdn_conv, state_gla, state_gdn, state_s5_re, state_s5_im,
           state_mlstm_c, state_mlstm_n, state_mlstm_m, norm_g, final_norm_g, w_in_ab, a_gate_w,
           a_gate_b, a_norm_g, b_conv_w, b_a_log, b_dt_bias, b_norm_g, w_out_ab, w_in_cd, c_lam_re,
           c_lam_im, c_log_dt, c_b_re, c_b_im, c_c_re, c_c_im, c_d, c_glu_w, c_glu_b, d_i_bias,
           d_f_bias, d_norm_g, w_out_cd):
    n_log = int(math.log2(max(x_prompt.shape[1], x_sample.shape[1]) // S5_CHUNK))
    w = _prepare_weights(norm_g, final_norm_g, w_in_ab, a_gate_w, a_gate_b, a_norm_g, b_conv_w, b_a_log,
                         b_dt_bias, b_norm_g, w_out_ab, w_in_cd, c_lam_re, c_lam_im, c_log_dt, c_b_re,
                         c_b_im, c_c_re, c_c_im, c_d, c_glu_w, c_glu_b, d_i_bias, d_f_bias, d_norm_g,
                         w_out_cd, n_log)
    nb = x_prompt.shape[0]
    zeros = lambda *shape: jnp.zeros(shape, F32)
    p_out = _trunk(x_prompt, zeros(nb, B_CONV - 1, B_QKV), zeros(nb, A_HEADS, A_DK, A_DV),
                   zeros(nb, B_HEADS, B_DK, B_DV), zeros(nb, C_GROUPS, C_STATE), zeros(nb, C_GROUPS, C_STATE),
                   zeros(nb, D_HEADS, D_DK, D_DV), zeros(nb, D_HEADS, D_DK), zeros(nb, D_HEADS), w)
    s_out = _trunk(x_sample, cache_gdn_conv, state_gla, state_gdn, state_s5_re, state_s5_im,
                   state_mlstm_c, state_mlstm_n, state_mlstm_m, w)
    return (p_out[0], s_out[0]) + tuple(p_out[1:]) + tuple(s_out[1:])
```

```python
import functools
import math

import jax
import jax.numpy as jnp
from jax import lax
from jax.experimental import pallas as pl
from jax.experimental.pallas import tpu as pltpu

F32 = jnp.float32
BF16 = jnp.bfloat16
HIGHEST = lax.Precision.HIGHEST

NORM_EPS = 1e-6
CHUNK = 64
A_HEADS, A_DK, A_DV, A_GATE_RANK, A_GATE_TAU = 4, 128, 256, 16, 16.0
B_HEADS, B_DK, B_DV, B_CONV = 8, 128, 128, 4
C_GROUP, C_GROUPS, C_STATE = 16, 64, 64
D_HEADS, D_DK, D_DV = 4, 128, 256
A_KW, A_VW = A_HEADS * A_DK, A_HEADS * A_DV
B_KW, B_VW = B_HEADS * B_DK, B_HEADS * B_DV
B_QKV = 2 * B_KW + B_VW
C_W = C_GROUPS * C_GROUP
D_KW, D_VW = D_HEADS * D_DK, D_HEADS * D_DV

LANES = 128
SUBLANES = 8
VMEM_BYTES_V7X = 64 * 1024 * 1024

AB_Q, AB_K, AB_V, AB_Z = 0, A_KW, 2 * A_KW, 2 * A_KW + A_VW
AB_QKV = AB_Z + A_VW
AB_ZB = AB_QKV + B_QKV
AB_SMALL = AB_ZB + B_VW
AB_GA_LANE, AB_BETA_LANE, AB_APRE_LANE = 0, A_GATE_RANK, A_GATE_RANK + B_HEADS
CD_U, CD_Z = 0, C_W
CD_Q = 2 * C_W
CD_K = CD_Q + D_KW
CD_V = CD_K + D_KW
CD_O = CD_V + D_VW
CD_ZD = CD_O + D_VW
CD_SMALL = CD_ZD + D_VW
CD_I_LANE, CD_F_LANE = 0, D_HEADS

PROJ_TN = 1280
S5_CHUNK = 16
S5_GROUP_BLOCK = 8
CONV_SLAB = 512


def _round_up(x, m):
    return (x + m - 1) // m * m


def _vmem_limit(block_bytes, scratch_bytes=0):
    est = 2 * block_bytes + scratch_bytes
    return int(min(max(2 * est, 32 * 1024 * 1024), VMEM_BYTES_V7X - 8 * 1024 * 1024))


def _mm(a, b):
    return jnp.dot(a, b, preferred_element_type=F32)


def _dot(a, b):
    return _mm(a.astype(BF16), b.astype(BF16))


def _dot_nt(a, b):
    return lax.dot_general(a.astype(BF16), b.astype(BF16), (((1,), (1,)), ((), ())),
                           preferred_element_type=F32)


def _dot_tn(a, b):
    return lax.dot_general(a.astype(BF16), b.astype(BF16), (((0,), (0,)), ((), ())),
                           preferred_element_type=F32)


def _split2(x):
    hi = x.astype(BF16)
    return hi, (x - hi.astype(F32)).astype(BF16)


def _split3(x):
    hi = x.astype(BF16)
    r = x - hi.astype(F32)
    mid = r.astype(BF16)
    return hi, mid, (r - mid.astype(F32)).astype(BF16)


def _dot3(a, b):
    ah, al = _split2(a)
    bh, bl = _split2(b)
    return _mm(ah, bh) + _mm(ah, bl) + _mm(al, bh)


def _cumsum_rows(tri_bf16, x):
    hi, mid, lo = _split3(x)
    return _mm(tri_bf16, hi) + _mm(tri_bf16, mid) + _mm(tri_bf16, lo)


def _select_rows(sel_bf16, x):
    nt = lambda b: lax.dot_general(sel_bf16, b, (((1,), (1,)), ((), ())), preferred_element_type=F32)
    hi, mid, lo = _split3(x)
    return nt(hi) + nt(mid) + nt(lo)


def _lane_selector(lane0):
    r = lax.broadcasted_iota(jnp.int32, (SUBLANES, LANES), 0)
    l = lax.broadcasted_iota(jnp.int32, (SUBLANES, LANES), 1)
    return jnp.where(l == r + lane0, 1.0, 0.0).astype(BF16)


def _causal_masks(c):
    row = lax.broadcasted_iota(jnp.int32, (c, c), 0)
    col = lax.broadcasted_iota(jnp.int32, (c, c), 1)
    return row >= col, row > col


def _norm_matmul_kernel(x_ref, g_ref, w_ref, o_ref, xn_ref):
    @pl.when(pl.program_id(1) == 0)
    def _():
        x = x_ref[...]
        y = x * lax.rsqrt(jnp.mean(x * x, axis=-1, keepdims=True) + NORM_EPS) * g_ref[...]
        xn_ref[...] = y.astype(BF16)

    o_ref[...] = jnp.dot(xn_ref[...], w_ref[...], preferred_element_type=F32)


def _norm_matmul(x2d, g_row, w_bf16):
    m, d = x2d.shape
    n = w_bf16.shape[1]
    tm = min(m, 1024)
    tn = PROJ_TN
    assert m % tm == 0 and n % tn == 0
    blocks = tm * d * 4 + d * tn * 2 + tm * tn * 4
    return pl.pallas_call(
        _norm_matmul_kernel,
        out_shape=jax.ShapeDtypeStruct((m, n), F32),
        grid=(m // tm, n // tn),
        in_specs=[pl.BlockSpec((tm, d), lambda i, j: (i, 0)),
                  pl.BlockSpec((1, d), lambda i, j: (0, 0)),
                  pl.BlockSpec((d, tn), lambda i, j: (0, j))],
        out_specs=pl.BlockSpec((tm, tn), lambda i, j: (i, j)),
        scratch_shapes=[pltpu.VMEM((tm, d), BF16)],
        compiler_params=pltpu.CompilerParams(
            dimension_semantics=("parallel", "arbitrary"),
            vmem_limit_bytes=_vmem_limit(blocks, tm * d * 2)),
        name="norm_in_proj",
    )(x2d, g_row, w_bf16)


def _out_proj_kernel(a_ref, b_ref, wa_ref, wb_ref, h_ref, o_ref):
    out = (jnp.dot(a_ref[...], wa_ref[...], preferred_element_type=F32)
           + jnp.dot(b_ref[...], wb_ref[...], preferred_element_type=F32))
    o_ref[...] = h_ref[...] + out


def _out_proj_norm_kernel(a_ref, b_ref, wa_ref, wb_ref, h_ref, g_ref, o_ref):
    out = (jnp.dot(a_ref[...], wa_ref[...], preferred_element_type=F32)
           + jnp.dot(b_ref[...], wb_ref[...], preferred_element_type=F32))
    h = h_ref[...] + out
    o_ref[...] = h * lax.rsqrt(jnp.mean(h * h, axis=-1, keepdims=True) + NORM_EPS) * g_ref[...]


def _out_proj(mix_a, mix_b, w_a, w_b, h2d, final_g_row=None):
    m, d = h2d.shape
    ka, kb = mix_a.shape[1], mix_b.shape[1]
    tm = min(m, 512)
    assert m % tm == 0
    in_specs = [pl.BlockSpec((tm, ka), lambda i: (i, 0)),
                pl.BlockSpec((tm, kb), lambda i: (i, 0)),
                pl.BlockSpec((ka, d), lambda i: (0, 0)),
                pl.BlockSpec((kb, d), lambda i: (0, 0)),
                pl.BlockSpec((tm, d), lambda i: (i, 0))]
    args = [mix_a, mix_b, w_a, w_b, h2d]
    kernel = _out_proj_kernel
    if final_g_row is not None:
        in_specs.append(pl.BlockSpec((1, d), lambda i: (0, 0)))
        args.append(final_g_row)
        kernel = _out_proj_norm_kernel
    blocks = tm * (ka + kb) * 2 + (ka + kb) * d * 2 + 2 * tm * d * 4
    return pl.pallas_call(
        kernel,
        out_shape=jax.ShapeDtypeStruct((m, d), F32),
        grid=(m // tm,),
        in_specs=in_specs,
        out_specs=pl.BlockSpec((tm, d), lambda i: (i, 0)),
        compiler_params=pltpu.CompilerParams(
            dimension_semantics=("parallel",),
            vmem_limit_bytes=_vmem_limit(blocks)),
        name="out_proj_norm" if final_g_row is not None else "out_proj",
    )(*args)


def _gla_kernel(q_ref, k_ref, v_ref, z_ref, sm_ref, gw_ref, gb_ref, ng_ref, s0_ref,
                o_ref, sout_ref, st_ref, *, c, nchunks):
    t = pl.program_id(1)
    last_t = pl.num_programs(1) - 1

    @pl.when(t == 0)
    def _():
        for h in range(A_HEADS):
            st_ref[h] = s0_ref[h].T

    causal, _ = _causal_masks(c)
    tri = causal.astype(BF16)
    gw = gw_ref[...]
    gb = gb_ref[...]

    def body(n, carry):
        sl = pl.ds(pl.multiple_of(n * c, c), c)
        log_alpha = jax.nn.log_sigmoid(_dot(sm_ref[sl, :], gw) + gb) * (1.0 / A_GATE_TAU)
        b_all = _cumsum_rows(tri, log_alpha)
        heads = range(A_HEADS)
        ksl = [slice(h * A_DK, (h + 1) * A_DK) for h in heads]
        vsl = [slice(h * A_DV, (h + 1) * A_DV) for h in heads]
        b = [b_all[:, ksl[h]] for h in heads]
        b_last = [b[h][c - 1:c, :] for h in heads]
        k = [k_ref[sl, ksl[h]] for h in heads]
        v = [v_ref[sl, vsl[h]].astype(BF16) for h in heads]
        q_dec = [(q_ref[sl, ksl[h]] * (A_DK ** -0.5) * jnp.exp(b[h])).astype(BF16) for h in heads]
        k_dec = [(k[h] * jnp.exp(-b[h])).astype(BF16) for h in heads]
        k_w = [(k[h] * jnp.exp(b_last[h] - b[h])).astype(BF16) for h in heads]
        scores = [jnp.where(causal, _dot_nt(q_dec[h], k_dec[h]), 0.0).astype(BF16) for h in heads]
        s_t = [st_ref[h] for h in heads]
        outs = [_mm(scores[h], v[h]) + _dot_nt(q_dec[h], s_t[h]) for h in heads]
        for h in heads:
            st_ref[h] = s_t[h] * jnp.exp(b_last[h]) + _dot_tn(v[h], k_w[h])
        for h in heads:
            o = outs[h]
            o = o * lax.rsqrt(jnp.mean(o * o, axis=-1, keepdims=True) + NORM_EPS) * ng_ref[:, vsl[h]]
            o_ref[sl, vsl[h]] = (o * jax.nn.silu(z_ref[sl, vsl[h]])).astype(o_ref.dtype)
        return carry

    lax.fori_loop(0, nchunks, body, 0)

    @pl.when(t == last_t)
    def _():
        for h in range(A_HEADS):
            sout_ref[h] = st_ref[h].T


def _gla(proj3, gate_w_pad, gate_b_row, norm_g_row, s0, *, c, tb):
    bsz, l, _ = proj3.shape
    nblk = l // tb
    tok = lambda col: (lambda b, t: (b, t, col))
    in_specs = [
        pl.BlockSpec((None, tb, A_KW), tok(AB_Q // A_KW)),
        pl.BlockSpec((None, tb, A_KW), tok(AB_K // A_KW)),
        pl.BlockSpec((None, tb, A_VW), tok(AB_V // A_VW)),
        pl.BlockSpec((None, tb, A_VW), tok(AB_Z // A_VW)),
        pl.BlockSpec((None, tb, LANES), tok(AB_SMALL // LANES)),
        pl.BlockSpec((LANES, A_KW), lambda b, t: (0, 0)),
        pl.BlockSpec((1, A_KW), lambda b, t: (0, 0)),
        pl.BlockSpec((1, A_VW), lambda b, t: (0, 0)),
        pl.BlockSpec((None, A_HEADS, A_DK, A_DV), lambda b, t: (b, 0, 0, 0)),
    ]
    out_specs = [
        pl.BlockSpec((None, tb, A_VW), lambda b, t: (b, t, 0)),
        pl.BlockSpec((None, A_HEADS, A_DK, A_DV), lambda b, t: (b, 0, 0, 0)),
    ]
    blocks = tb * (2 * A_KW + 2 * A_VW + LANES) * 4 + tb * A_VW * 2 + 2 * A_HEADS * A_DK * A_DV * 4
    return pl.pallas_call(
        functools.partial(_gla_kernel, c=c, nchunks=tb // c),
        out_shape=[jax.ShapeDtypeStruct((bsz, l, A_VW), BF16),
                   jax.ShapeDtypeStruct((bsz, A_HEADS, A_DK, A_DV), F32)],
        grid=(bsz, nblk),
        in_specs=in_specs,
        out_specs=out_specs,
        scratch_shapes=[pltpu.VMEM((A_HEADS, A_DV, A_DK), F32)],
        compiler_params=pltpu.CompilerParams(
            dimension_semantics=("parallel", "arbitrary"),
            vmem_limit_bytes=_vmem_limit(blocks, A_HEADS * A_DK * A_DV * 4)),
        name="gla_mixer",
    )(proj3, proj3, proj3, proj3, proj3, gate_w_pad, gate_b_row, norm_g_row, s0)


def _gdn_kernel(x_ref, z_ref, sm_ref, w_ref, cp_ref, alog_ref, dtb_ref, ng_ref, s0_ref,
                o_ref, sout_ref, s_ref, tail_ref, conv_ref, u_ref, wm_ref, qg_ref, kg_ref, qk_ref, gl_ref,
                *, c, nchunks):
    t = pl.program_id(1)
    last_t = pl.num_programs(1) - 1
    tb = c * nchunks
    keep = SUBLANES - (B_CONV - 1)

    @pl.when(t == 0)
    def _():
        s_ref[...] = s0_ref[...]
        tail_ref[0:keep, :] = jnp.zeros((keep, B_QKV), F32)
        tail_ref[keep:SUBLANES, :] = cp_ref[...]

    for j in range(0, B_QKV, CONV_SLAB):
        cs = slice(j, j + CONV_SLAB)
        x = x_ref[:, cs]
        w = w_ref[:, cs]
        ext = jnp.concatenate([tail_ref[:, cs], x], axis=0)
        conv = ext[keep:keep + tb, :] * w[0:1, :]
        conv = conv + ext[keep + 1:keep + 1 + tb, :] * w[1:2, :]
        conv = conv + ext[keep + 2:keep + 2 + tb, :] * w[2:3, :]
        conv = conv + x * w[3:4, :]
        conv_ref[:, cs] = jax.nn.silu(conv)
        tail_ref[:, cs] = x[tb - SUBLANES:tb, :]

    causal, strict = _causal_masks(c)
    tri = causal.astype(BF16)
    eye = jnp.where(causal & jnp.logical_not(strict), 1.0, 0.0).astype(F32)
    sel = _lane_selector(AB_APRE_LANE)
    neg_a_exp = -jnp.exp(alog_ref[...])
    dtb = dtb_ref[...]
    ng = ng_ref[...]
    n_double = int(math.log2(c)) - 1
    heads = range(B_HEADS)

    def wy_factors(n, carry):
        sl = pl.ds(pl.multiple_of(n * c, c), c)
        sm = sm_ref[sl, :]
        g_cum = _cumsum_rows(tri, neg_a_exp * jax.nn.softplus(sm + dtb))
        g_rows = _select_rows(sel, g_cum)
        beta_all = jax.nn.sigmoid(sm)
        gl_ref[pl.ds(n, 1), :] = g_cum[c - 1:c, :]
        pws, rhss = [], []
        for h in heads:
            hs = slice(h * B_DK, (h + 1) * B_DK)
            q = conv_ref[sl, hs]
            k = conv_ref[sl, B_KW + h * B_DK:B_KW + (h + 1) * B_DK]
            v = conv_ref[sl, 2 * B_KW + h * B_DV:2 * B_KW + (h + 1) * B_DV]
            q = q * lax.rsqrt(jnp.sum(q * q, axis=-1, keepdims=True) + NORM_EPS) * (B_DK ** -0.5)
            k = k * lax.rsqrt(jnp.sum(k * k, axis=-1, keepdims=True) + NORM_EPS)
            g_col = g_cum[:, AB_APRE_LANE + h:AB_APRE_LANE + h + 1]
            beta = beta_all[:, AB_BETA_LANE + h:AB_BETA_LANE + h + 1]
            decay = jnp.exp(jnp.where(causal, g_col - g_rows[h:h + 1, :], -jnp.inf))
            e_g = jnp.exp(g_col)
            k_beta = k * beta
            pws.append(-jnp.where(strict, _dot_nt(k_beta, k) * decay, 0.0))
            rhss.append(jnp.concatenate([v * beta, k_beta * e_g], axis=1))
            qk_ref[sl, h * LANES:h * LANES + c] = jnp.where(causal, _dot_nt(q, k) * decay, 0.0).astype(BF16)
            qg_ref[sl, hs] = (q * e_g).astype(BF16)
            kg_ref[sl, hs] = (k * jnp.exp(g_col[c - 1:c, :] - g_col)).astype(BF16)
        invs = [eye + p for p in pws]
        for _ in range(n_double):
            pws = [_dot3(p, p) for p in pws]
            invs = [i + _dot3(i, p) for i, p in zip(invs, pws)]
        for h in heads:
            hs = slice(h * B_DK, (h + 1) * B_DK)
            uw = _dot3(invs[h], rhss[h])
            u_ref[sl, hs] = uw[:, :B_DV]
            wm_ref[sl, hs] = uw[:, B_DV:].astype(BF16)
        return carry

    lax.fori_loop(0, nchunks, wy_factors, 0)

    def recurrence(n, carry):
        sl = pl.ds(pl.multiple_of(n * c, c), c)
        e_last = jnp.exp(gl_ref[pl.ds(n, 1), :])
        hsl = [slice(h * B_DK, (h + 1) * B_DK) for h in heads]
        s_old = [s_ref[h] for h in heads]
        s_bf = [s.astype(BF16) for s in s_old]
        v_new = [(u_ref[sl, hsl[h]] - _mm(wm_ref[sl, hsl[h]], s_bf[h])).astype(BF16) for h in heads]
        outs = [_mm(qg_ref[sl, hsl[h]], s_bf[h]) + _mm(qk_ref[sl, h * LANES:h * LANES + c], v_new[h])
                for h in heads]
        for h in heads:
            s_ref[h] = (e_last[:, AB_APRE_LANE + h:AB_APRE_LANE + h + 1] * s_old[h]
                        + lax.dot_general(kg_ref[sl, hsl[h]], v_new[h], (((0,), (0,)), ((), ())),
                                          preferred_element_type=F32))
        for h in heads:
            o = outs[h]
            o = o * lax.rsqrt(jnp.mean(o * o, axis=-1, keepdims=True) + NORM_EPS) * ng
            o_ref[sl, hsl[h]] = (o * jax.nn.silu(z_ref[sl, hsl[h]])).astype(o_ref.dtype)
        return carry

    lax.fori_loop(0, nchunks, recurrence, 0)

    @pl.when(t == last_t)
    def _():
        sout_ref[...] = s_ref[...]


def _gdn(proj3, conv_prev, conv_w, alog_row, dtb_row, norm_g_row, s0, *, c, tb):
    bsz, l, _ = proj3.shape
    nblk = l // tb
    assert AB_QKV % B_QKV == 0 and AB_ZB % B_VW == 0
    in_specs = [
        pl.BlockSpec((None, tb, B_QKV), lambda b, t: (b, t, AB_QKV // B_QKV)),
        pl.BlockSpec((None, tb, B_VW), lambda b, t: (b, t, AB_ZB // B_VW)),
        pl.BlockSpec((None, tb, LANES), lambda b, t: (b, t, AB_SMALL // LANES)),
        pl.BlockSpec((B_CONV, B_QKV), lambda b, t: (0, 0)),
        pl.BlockSpec((None, B_CONV - 1, B_QKV), lambda b, t: (b, 0, 0)),
        pl.BlockSpec((1, LANES), lambda b, t: (0, 0)),
        pl.BlockSpec((1, LANES), lambda b, t: (0, 0)),
        pl.BlockSpec((1, B_DV), lambda b, t: (0, 0)),
        pl.BlockSpec((None, B_HEADS, B_DK, B_DV), lambda b, t: (b, 0, 0, 0)),
    ]
    out_specs = [
        pl.BlockSpec((None, tb, B_VW), lambda b, t: (b, t, 0)),
        pl.BlockSpec((None, B_HEADS, B_DK, B_DV), lambda b, t: (b, 0, 0, 0)),
    ]
    blocks = tb * (B_QKV + B_VW + LANES) * 4 + tb * B_VW * 2 + 2 * B_HEADS * B_DK * B_DV * 4
    scratch = (B_HEADS * B_DK * B_DV * 4 + SUBLANES * B_QKV * 4 + tb * B_QKV * 4 + tb * B_VW * 4
               + 4 * tb * B_KW * 2 + SUBLANES * LANES * 4)
    return pl.pallas_call(
        functools.partial(_gdn_kernel, c=c, nchunks=tb // c),
        out_shape=[jax.ShapeDtypeStruct((bsz, l, B_VW), BF16),
                   jax.ShapeDtypeStruct((bsz, B_HEADS, B_DK, B_DV), F32)],
        grid=(bsz, nblk),
        in_specs=in_specs,
        out_specs=out_specs,
        scratch_shapes=[pltpu.VMEM((B_HEADS, B_DK, B_DV), F32),
                        pltpu.VMEM((SUBLANES, B_QKV), F32),
                        pltpu.VMEM((tb, B_QKV), F32),
                        pltpu.VMEM((tb, B_VW), F32),
                        pltpu.VMEM((tb, B_KW), BF16),
                        pltpu.VMEM((tb, B_KW), BF16),
                        pltpu.VMEM((tb, B_KW), BF16),
                        pltpu.VMEM((tb, B_HEADS * LANES), BF16),
                        pltpu.VMEM((SUBLANES, LANES), F32)],
        compiler_params=pltpu.CompilerParams(
            dimension_semantics=("parallel", "arbitrary"),
            vmem_limit_bytes=_vmem_limit(blocks, scratch)),
        name="gdn_mixer",
    )(proj3, proj3, proj3, conv_w, conv_prev, alog_row, dtb_row, norm_g_row, s0)


def _mlstm_kernel(q_ref, k_ref, v_ref, og_ref, z_ref, sm_ref, ib_ref, fb_ref, ng_ref,
                  c0_ref, n0_ref, m0_ref, o_ref, cout_ref, nout_ref, mout_ref,
                  c_ref, n_ref, m_ref, *, c, nchunks):
    t = pl.program_id(1)
    last_t = pl.num_programs(1) - 1

    @pl.when(t == 0)
    def _():
        c_ref[...] = c0_ref[...]
        n_ref[...] = n0_ref[...]
        m_ref[...] = m0_ref[...]

    causal, _ = _causal_masks(c)
    tri = causal.astype(BF16)
    sel = _lane_selector(0)
    lane = lax.broadcasted_iota(jnp.int32, (c, LANES), 1)
    ib = ib_ref[...]
    fb = fb_ref[...]

    def body(n, carry):
        sl = pl.ds(pl.multiple_of(n * c, c), c)
        sm = sm_ref[sl, :]
        i_full = sm + ib
        b_full = _cumsum_rows(tri, jax.nn.log_sigmoid(sm + fb))
        rows = _select_rows(sel, jnp.where(lane < CD_F_LANE, i_full, b_full))
        heads = range(D_HEADS)
        ksl = [slice(h * D_DK, (h + 1) * D_DK) for h in heads]
        vsl = [slice(h * D_DV, (h + 1) * D_DV) for h in heads]
        b_col = [b_full[:, CD_F_LANE + h:CD_F_LANE + h + 1] for h in heads]
        i_col = [i_full[:, CD_I_LANE + h:CD_I_LANE + h + 1] for h in heads]
        logw = [jnp.where(causal, b_col[h] - rows[CD_F_LANE + h:CD_F_LANE + h + 1, :]
                          + rows[CD_I_LANE + h:CD_I_LANE + h + 1, :], -jnp.inf) for h in heads]
        m_intra = [jnp.max(logw[h], axis=-1, keepdims=True) for h in heads]
        q = [q_ref[sl, ksl[h]] * (D_DK ** -0.5) for h in heads]
        q_bf = [x.astype(BF16) for x in q]
        k = [k_ref[sl, ksl[h]] for h in heads]
        v = [v_ref[sl, vsl[h]].astype(BF16) for h in heads]
        p = [jnp.exp(logw[h] - m_intra[h]) * _dot_nt(q_bf[h], k[h]) for h in heads]
        h_intra = [_dot(p[h], v[h]) for h in heads]
        n_intra = [jnp.sum(p[h], axis=-1, keepdims=True) for h in heads]
        b_last = [b_col[h][c - 1:c, :] for h in heads]
        m_chunk = [m_intra[h][c - 1:c, :] for h in heads]
        k_w = [k[h] * jnp.exp(b_last[h] - b_col[h] + i_col[h] - m_chunk[h]) for h in heads]
        kv = [_dot_tn(k_w[h], v[h]) for h in heads]
        k_sum = [jnp.sum(k_w[h], axis=0, keepdims=True) for h in heads]
        c_s = [c_ref[h] for h in heads]
        n_s = [n_ref[h:h + 1, :] for h in heads]
        m_s = [m_ref[:, h:h + 1] for h in heads]
        qc = [_dot(q_bf[h], c_s[h]) for h in heads]
        for h in heads:
            m_new = jnp.maximum(b_last[h] + m_s[h], m_chunk[h])
            w_old = jnp.exp(b_last[h] + m_s[h] - m_new)
            w_new = jnp.exp(m_chunk[h] - m_new)
            c_ref[h] = w_old * c_s[h] + w_new * kv[h]
            n_ref[h:h + 1, :] = w_old * n_s[h] + w_new * k_sum[h]
            m_ref[:, h:h + 1] = m_new
        for h in heads:
            a = b_col[h] + m_s[h]
            m_t = jnp.maximum(a, m_intra[h])
            w_a = jnp.exp(a - m_t)
            w_i = jnp.exp(m_intra[h] - m_t)
            num = w_a * qc[h] + w_i * h_intra[h]
            den = w_a * jnp.sum(q[h] * n_s[h], axis=-1, keepdims=True) + w_i * n_intra[h]
            hh = num / jnp.maximum(jnp.abs(den), jnp.exp(-m_t))
            hd = jax.nn.sigmoid(og_ref[sl, vsl[h]]) * hh
            oc = hd - jnp.mean(hd, axis=-1, keepdims=True)
            o = oc * lax.rsqrt(jnp.mean(oc * oc, axis=-1, keepdims=True) + NORM_EPS) * ng_ref[:, vsl[h]]
            o_ref[sl, vsl[h]] = (o * jax.nn.silu(z_ref[sl, vsl[h]])).astype(o_ref.dtype)
        return carry

    lax.fori_loop(0, nchunks, body, 0)

    @pl.when(t == last_t)
    def _():
        cout_ref[...] = c_ref[...]
        nout_ref[...] = n_ref[...]
        mout_ref[...] = m_ref[...]


def _mlstm(proj3, ib_row, fb_row, norm_g_row, c0, n0, m0, *, c, tb):
    bsz, l, _ = proj3.shape
    nblk = l // tb
    tok = lambda col: (lambda b, t: (b, t, col))
    st4 = lambda b, t: (b, 0, 0, 0)
    st3 = lambda b, t: (b, 0, 0)
    in_specs = [
        pl.BlockSpec((None, tb, D_KW), tok(CD_Q // D_KW)),
        pl.BlockSpec((None, tb, D_KW), tok(CD_K // D_KW)),
        pl.BlockSpec((None, tb, D_VW), tok(CD_V // D_VW)),
        pl.BlockSpec((None, tb, D_VW), tok(CD_O // D_VW)),
        pl.BlockSpec((None, tb, D_VW), tok(CD_ZD // D_VW)),
        pl.BlockSpec((None, tb, LANES), tok(CD_SMALL // LANES)),
        pl.BlockSpec((1, LANES), lambda b, t: (0, 0)),
        pl.BlockSpec((1, LANES), lambda b, t: (0, 0)),
        pl.BlockSpec((1, D_VW), lambda b, t: (0, 0)),
        pl.BlockSpec((None, D_HEADS, D_DK, D_DV), st4),
        pl.BlockSpec((None, D_HEADS, D_DK), st3),
        pl.BlockSpec((None, 1, D_HEADS), st3),
    ]
    out_specs = [
        pl.BlockSpec((None, tb, D_VW), lambda b, t: (b, t, 0)),
        pl.BlockSpec((None, D_HEADS, D_DK, D_DV), st4),
        pl.BlockSpec((None, D_HEADS, D_DK), st3),
        pl.BlockSpec((None, 1, D_HEADS), st3),
    ]
    blocks = tb * (2 * D_KW + 3 * D_VW + LANES) * 4 + tb * D_VW * 2 + 2 * D_HEADS * D_DK * D_DV * 4
    return pl.pallas_call(
        functools.partial(_mlstm_kernel, c=c, nchunks=tb // c),
        out_shape=[jax.ShapeDtypeStruct((bsz, l, D_VW), BF16),
                   jax.ShapeDtypeStruct((bsz, D_HEADS, D_DK, D_DV), F32),
                   jax.ShapeDtypeStruct((bsz, D_HEADS, D_DK), F32),
                   jax.ShapeDtypeStruct((bsz, 1, D_HEADS), F32)],
        grid=(bsz, nblk),
        in_specs=in_specs,
        out_specs=out_specs,
        scratch_shapes=[pltpu.VMEM((D_HEADS, D_DK, D_DV), F32),
                        pltpu.VMEM((D_HEADS, D_DK), F32),
                        pltpu.VMEM((1, D_HEADS), F32)],
        compiler_params=pltpu.CompilerParams(
            dimension_semantics=("parallel", "arbitrary"),
            vmem_limit_bytes=_vmem_limit(blocks, D_HEADS * D_DK * D_DV * 4)),
        name="mlstm_mixer",
    )(proj3, proj3, proj3, proj3, proj3, proj3, ib_row, fb_row, norm_g_row, c0, n0, m0)


def _s5_kernel(u_ref, bd_ref, bst_ref, cst_ref, apow_ref, x0_ref, y_ref, xf_ref, *, nc, bb):
    m = nc * bb
    tc = S5_CHUNK
    sw = 2 * C_STATE
    row = lax.broadcasted_iota(jnp.int32, (m, sw), 0)
    n_idx = row & (nc - 1)
    n_log = int(math.log2(nc))

    def cmul(a1, a2, x):
        return a1 * x + a2 * pltpu.roll(x, C_STATE, 1)

    lhs = jnp.concatenate([u_ref[pl.ds(tau, m, stride=tc), :].astype(BF16) for tau in range(tc)], axis=1)
    e_all = _mm(lhs, bst_ref[...])
    x_start = []
    for g in range(S5_GROUP_BLOCK):
        gs = slice(g * sw, (g + 1) * sw)
        x0_rows = jnp.zeros((m, sw), F32)
        for b in range(bb):
            x0_rows = jnp.where(row == b * nc, x0_ref[b, :, gs], x0_rows)
        x = e_all[:, gs] + cmul(apow_ref[0:1, gs], apow_ref[1:2, gs], x0_rows)
        for j in range(n_log):
            sh = 1 << j
            shifted = jnp.where(n_idx >= sh, pltpu.roll(x, sh, 0), 0.0)
            x = x + cmul(apow_ref[2 * j:2 * j + 1, gs], apow_ref[2 * j + 1:2 * j + 2, gs], shifted)
        x_start.append(jnp.where(n_idx >= 1, pltpu.roll(x, 1, 0), x0_rows).astype(BF16))
        for b in range(bb):
            xf_ref[b, :, gs] = x[b * nc + nc - 1:b * nc + nc, :]
    y_state = _mm(jnp.concatenate(x_start, axis=1), cst_ref[...])
    for tau in range(tc):
        y = (_mm(lhs[:, :(tau + 1) * LANES], bd_ref[(tc - 1 - tau) * LANES:, :])
             + y_state[:, tau * LANES:(tau + 1) * LANES])
        y_ref[pl.ds(tau, m, stride=tc), :] = y


def _s5_chunks(proj2d, bd, bst, cst, apow, x0, *, l, bb):
    tc = S5_CHUNK
    nc = l // tc
    bsz = x0.shape[1]
    nt = C_GROUPS // S5_GROUP_BLOCK
    rows = bb * l
    sw_t = S5_GROUP_BLOCK * 2 * C_STATE
    w_t = tc * LANES
    assert CD_U % LANES == 0 and bsz % bb == 0
    blocks = (2 * rows * LANES * 4 + w_t * LANES * 2 + 2 * w_t * sw_t * 2 + apow.shape[1] * sw_t * 4
              + 2 * bb * SUBLANES * sw_t * 4)
    temps = bb * nc * (w_t * 2 + w_t * 4 + 3 * sw_t * 4)
    return pl.pallas_call(
        functools.partial(_s5_kernel, nc=nc, bb=bb),
        out_shape=[jax.ShapeDtypeStruct((bsz * l, C_W), F32),
                   jax.ShapeDtypeStruct((nt, bsz, 1, sw_t), F32)],
        grid=(nt, bsz // bb),
        in_specs=[pl.BlockSpec((rows, LANES), lambda i, j: (j, CD_U // LANES + i)),
                  pl.BlockSpec((None, w_t, LANES), lambda i, j: (i, 0, 0)),
                  pl.BlockSpec((None, w_t, sw_t), lambda i, j: (i, 0, 0)),
                  pl.BlockSpec((None, sw_t, w_t), lambda i, j: (i, 0, 0)),
                  pl.BlockSpec((None, apow.shape[1], sw_t), lambda i, j: (i, 0, 0)),
                  pl.BlockSpec((None, bb, 1, sw_t), lambda i, j: (i, j, 0, 0))],
        out_specs=[pl.BlockSpec((rows, LANES), lambda i, j: (j, i)),
                   pl.BlockSpec((None, bb, 1, sw_t), lambda i, j: (i, j, 0, 0))],
        compiler_params=pltpu.CompilerParams(
            dimension_semantics=("parallel", "parallel"),
            vmem_limit_bytes=_vmem_limit(blocks, temps)),
        name="s5_chunks",
    )(proj2d, bd, bst, cst, apow, x0)


def _s5_glu_kernel(y_ref, u_ref, z_ref, d_ref, w_ref, b_ref, o_ref):
    y = jax.nn.gelu(y_ref[...] + d_ref[...] * u_ref[...])
    gate = jax.nn.sigmoid(jnp.dot(y.astype(BF16), w_ref[...], preferred_element_type=F32) + b_ref[...])
    o_ref[...] = (y * gate * jax.nn.silu(z_ref[...])).astype(o_ref.dtype)


def _s5_glu(y2d, proj2d, d_row, glu_w, glu_b_row):
    m = y2d.shape[0]
    tm = min(m, 512)
    blocks = 3 * tm * C_W * 4 + C_W * C_W * 2 + tm * C_W * 2
    return pl.pallas_call(
        _s5_glu_kernel,
        out_shape=jax.ShapeDtypeStruct((m, C_W), BF16),
        grid=(m // tm,),
        in_specs=[pl.BlockSpec((tm, C_W), lambda i: (i, 0)),
                  pl.BlockSpec((tm, C_W), lambda i: (i, CD_U // C_W)),
                  pl.BlockSpec((tm, C_W), lambda i: (i, CD_Z // C_W)),
                  pl.BlockSpec((1, C_W), lambda i: (0, 0)),
                  pl.BlockSpec((C_W, C_W), lambda i: (0, 0)),
                  pl.BlockSpec((1, C_W), lambda i: (0, 0))],
        out_specs=pl.BlockSpec((tm, C_W), lambda i: (i, 0)),
        compiler_params=pltpu.CompilerParams(
            dimension_semantics=("parallel",),
            vmem_limit_bytes=_vmem_limit(blocks)),
        name="s5_glu",
    )(y2d, proj2d, proj2d, d_row, glu_w, glu_b_row)


def _s5_operators(lam_re, lam_im, log_dt, b_re, b_im, c_re, c_im, n_log):
    g, p = lam_re.shape
    tc = S5_CHUNK
    dt = jnp.exp(log_dt.astype(F32))[:, None]
    mag = jnp.exp(lam_re * dt)
    ab_re, ab_im = mag * jnp.cos(lam_im * dt), mag * jnp.sin(lam_im * dt)
    den = lam_re * lam_re + lam_im * lam_im
    er = ab_re - 1.0
    zr = (er * lam_re + ab_im * lam_im) / den
    zi = (ab_im * lam_re - er * lam_im) / den
    bb_re = zr[..., None] * b_re - zi[..., None] * b_im
    bb_im = zr[..., None] * b_im + zi[..., None] * b_re
    pr, pi = [jnp.ones_like(ab_re)], [jnp.zeros_like(ab_re)]
    for _ in range(tc):
        pr, pi = pr + [pr[-1] * ab_re - pi[-1] * ab_im], pi + [pr[-1] * ab_im + pi[-1] * ab_re]
    pw_re, pw_im = jnp.stack(pr), jnp.stack(pi)
    abr = pw_re[:tc, :, :, None] * bb_re - pw_im[:tc, :, :, None] * bb_im
    abi = pw_re[:tc, :, :, None] * bb_im + pw_im[:tc, :, :, None] * bb_re
    kern = (jnp.einsum('gjp,dgpi->dgji', c_re, abr, precision=HIGHEST)
            - jnp.einsum('gjp,dgpi->dgji', c_im, abi, precision=HIGHEST))
    gt = S5_GROUP_BLOCK
    nt = g // gt
    eye = jnp.eye(gt, dtype=F32)
    sw = 2 * p
    bd = jnp.einsum('dtgji,gh->tdgihj', kern.reshape(tc, nt, gt, C_GROUP, C_GROUP), eye)
    bd = bd[:, ::-1].reshape(nt, tc * gt * C_GROUP, gt * C_GROUP)
    ab = jnp.concatenate([abr, abi], axis=2)[::-1]
    bst = jnp.einsum('atgpi,gh->tagihp', ab.reshape(tc, nt, gt, sw, C_GROUP), eye)
    bst = bst.reshape(nt, tc * gt * C_GROUP, gt * sw)
    cr = c_re[None] * pw_re[1:, :, None, :] - c_im[None] * pw_im[1:, :, None, :]
    ci = -(c_re[None] * pw_im[1:, :, None, :] + c_im[None] * pw_re[1:, :, None, :])
    cc = jnp.concatenate([cr, ci], axis=3)
    cst = jnp.einsum('atgjp,gh->tgpahj', cc.reshape(tc, nt, gt, C_GROUP, sw), eye)
    cst = cst.reshape(nt, gt * sw, tc * gt * C_GROUP)
    r, i = pw_re[tc], pw_im[tc]
    rows = []
    for _ in range(max(n_log, 1)):
        rows += [jnp.concatenate([r, r], -1), jnp.concatenate([-i, i], -1)]
        r, i = r * r - i * i, 2.0 * r * i
    apow = jnp.stack(rows, axis=1)
    apow = apow.reshape(nt, gt, -1, sw).transpose(0, 2, 1, 3).reshape(nt, -1, gt * sw)
    return bd.astype(BF16), bst.astype(BF16), cst.astype(BF16), apow


def _s5(proj2d, ops, x0_re, x0_im, *, l):
    bd, bst, cst, apow = ops
    bsz = x0_re.shape[0]
    nt = C_GROUPS // S5_GROUP_BLOCK
    x0 = jnp.concatenate([x0_re, x0_im], axis=-1).reshape(bsz, nt, 1, -1).transpose(1, 0, 2, 3)
    bb = bsz if bsz * l <= 4096 else 1
    y, xf = _s5_chunks(proj2d, bd, bst, cst, apow, x0, l=l, bb=bb)
    xf = xf.transpose(1, 0, 2, 3).reshape(bsz, C_GROUPS, 2 * C_STATE)
    return y, xf[..., :C_STATE], xf[..., C_STATE:]


def _lane_row(vals, lane0):
    return jnp.zeros((1, LANES), F32).at[0, lane0:lane0 + vals.shape[0]].set(vals.astype(F32))


def _prepare_weights(norm_g, final_norm_g, w_in_ab, a_gate_w, a_gate_b, a_norm_g, b_conv_w, b_a_log,
                     b_dt_bias, b_norm_g, w_out_ab, w_in_cd, c_lam_re, c_lam_im, c_log_dt, c_b_re,
                     c_b_im, c_c_re, c_c_im, c_d, c_glu_w, c_glu_b, d_i_bias, d_f_bias, d_norm_g,
                     w_out_cd, n_log):
    ab_split = (A_KW, A_KW, A_VW, A_VW, A_GATE_RANK, B_QKV, B_VW, B_HEADS, B_HEADS)
    offs = [0]
    for s in ab_split:
        offs.append(offs[-1] + s)
    d_model = w_in_ab.shape[0]
    cols = lambda i: w_in_ab[:, offs[i]:offs[i + 1]].astype(BF16)
    n_ab = _round_up(AB_SMALL + LANES, PROJ_TN)
    w_ab = jnp.concatenate([cols(0), cols(1), cols(2), cols(3), cols(5), cols(6), cols(4), cols(7), cols(8),
                            jnp.zeros((d_model, n_ab - w_in_ab.shape[1]), BF16)], axis=1)
    n_cd = _round_up(CD_SMALL + LANES, PROJ_TN)
    w_cd = jnp.concatenate([w_in_cd.astype(BF16), jnp.zeros((d_model, n_cd - w_in_cd.shape[1]), BF16)], axis=1)
    gate_w = jnp.zeros((LANES, A_KW), F32).at[AB_GA_LANE:AB_GA_LANE + A_GATE_RANK].set(
        a_gate_w.astype(F32)).astype(BF16)
    return dict(
        norm_g=norm_g.astype(F32), final_g=final_norm_g.astype(F32)[None, :],
        w_ab=w_ab, w_cd=w_cd, gate_w=gate_w, gate_b=a_gate_b.astype(F32)[None, :],
        a_norm_g=a_norm_g.astype(F32)[None, :], conv_w=b_conv_w.astype(F32),
        alog=_lane_row(b_a_log, AB_APRE_LANE), dtb=_lane_row(b_dt_bias, AB_APRE_LANE),
        b_norm_g=b_norm_g.astype(F32)[None, :],
        w_out_a=w_out_ab[:A_VW].astype(BF16), w_out_b=w_out_ab[A_VW:].astype(BF16),
        s5_ops=_s5_operators(c_lam_re.astype(F32), c_lam_im.astype(F32), c_log_dt, c_b_re.astype(F32),
                             c_b_im.astype(F32), c_c_re.astype(F32), c_c_im.astype(F32), n_log),
        c_d=c_d.astype(F32).reshape(1, C_W), glu_w=c_glu_w.astype(BF16),
        glu_b=c_glu_b.astype(F32)[None, :],
        ib=_lane_row(d_i_bias, CD_I_LANE), fb=_lane_row(d_f_bias, CD_F_LANE),
        d_norm_g=d_norm_g.astype(F32)[None, :],
        w_out_c=w_out_cd[:C_W].astype(BF16), w_out_d=w_out_cd[C_W:].astype(BF16),
    )


def _trunk(x, conv_prev, s_gla0, s_gdn0, s5_re0, s5_im0, mc0, mn0, mm0, w):
    bsz, l, d = x.shape
    c = min(CHUNK, l)
    tb = min(l, 8 * c)
    assert l % tb == 0 and l % S5_CHUNK == 0
    x2d = x.reshape(bsz * l, d)

    proj = _norm_matmul(x2d, w['norm_g'][0:1], w['w_ab'])
    proj3 = proj.reshape(bsz, l, proj.shape[1])
    o_a, s_gla = _gla(proj3, w['gate_w'], w['gate_b'], w['a_norm_g'], s_gla0.astype(F32), c=c, tb=tb)
    o_b, s_gdn = _gdn(proj3, conv_prev.astype(F32), w['conv_w'], w['alog'], w['dtb'], w['b_norm_g'],
                      s_gdn0.astype(F32), c=c, tb=tb)
    conv_new = proj3[:, l - (B_CONV - 1):, AB_QKV:AB_QKV + B_QKV]
    h1 = _out_proj(o_a.reshape(bsz * l, A_VW), o_b.reshape(bsz * l, B_VW), w['w_out_a'], w['w_out_b'], x2d)

    proj = _norm_matmul(h1, w['norm_g'][1:2], w['w_cd'])
    proj3 = proj.reshape(bsz, l, proj.shape[1])
    y, s5_re, s5_im = _s5(proj, w['s5_ops'], s5_re0.astype(F32), s5_im0.astype(F32), l=l)
    o_c = _s5_glu(y, proj, w['c_d'], w['glu_w'], w['glu_b'])
    o_d, mc, mn, mm = _mlstm(proj3, w['ib'], w['fb'], w['d_norm_g'], mc0.astype(F32),
                             mn0.astype(F32), mm0.astype(F32)[:, None, :], c=c, tb=tb)
    y_out = _out_proj(o_c, o_d.reshape(bsz * l, D_VW), w['w_out_c'], w['w_out_d'], h1, w['final_g'])
    dt = x.dtype
    return (y_out.reshape(bsz, l, d).astype(dt), conv_new.astype(dt), s_gla.astype(dt), s_gdn.astype(dt),
            s5_re.astype(dt), s5_im.astype(dt), mc.astype(dt), mn.astype(dt), mm[:, 0, :].astype(dt))


def kernel(x_prompt, x_sample, cache_gdn_conv, state_gla, state_gdn, state_s5_re, state_s5_im,
           state_mlstm_c, state_mlstm_n, state_mlstm_m, norm_g, final_norm_g, w_in_ab, a_gate_w,
           a_gate_b, a_norm_g, b_conv_w, b_a_log, b_dt_bias, b_norm_g, w_out_ab, w_in_cd, c_lam_re,
           c_lam_im, c_log_dt, c_b_re, c_b_im, c_c_re, c_c_im, c_d, c_glu_w, c_glu_b, d_i_bias,
           d_f_bias, d_norm_g, w_out_cd):
    n_log = int(math.log2(max(x_prompt.shape[1], x_sample.shape[1]) // S5_CHUNK))
    w = _prepare_weights(norm_g, final_norm_g, w_in_ab, a_gate_w, a_gate_b, a_norm_g, b_conv_w, b_a_log,
                         b_dt_bias, b_norm_g, w_out_ab, w_in_cd, c_lam_re, c_lam_im, c_log_dt, c_b_re,
                         c_b_im, c_c_re, c_c_im, c_d, c_glu_w, c_glu_b, d_i_bias, d_f_bias, d_norm_g,
                         w_out_cd, n_log)
    nb = x_prompt.shape[0]
    zeros = lambda *shape: jnp.zeros(shape, F32)
    p_out = _trunk(x_prompt, zeros(nb, B_CONV - 1, B_QKV), zeros(nb, A_HEADS, A_DK, A_DV),
                   zeros(nb, B_HEADS, B_DK, B_DV), zeros(nb, C_GROUPS, C_STATE), zeros(nb, C_GROUPS, C_STATE),
                   zeros(nb, D_HEADS, D_DK, D_DV), zeros(nb, D_HEADS, D_DK), zeros(nb, D_HEADS), w)
    s_out = _trunk(x_sample, cache_gdn_conv, state_gla, state_gdn, state_s5_re, state_s5_im,
                   state_mlstm_c, state_mlstm_n, state_mlstm_m, w)
    return (p_out[0], s_out[0]) + tuple(p_out[1:]) + tuple(s_out[1:])
```

```python
import functools
import math

import jax
import jax.numpy as jnp
from jax import lax
from jax.experimental import pallas as pl
from jax.experimental.pallas import tpu as pltpu

F32 = jnp.float32
BF16 = jnp.bfloat16
HIGHEST = lax.Precision.HIGHEST

NORM_EPS = 1e-6
CHUNK = 64
A_HEADS, A_DK, A_DV, A_GATE_RANK, A_GATE_TAU = 4, 128, 256, 16, 16.0
B_HEADS, B_DK, B_DV, B_CONV = 8, 128, 128, 4
C_GROUP, C_GROUPS, C_STATE = 16, 64, 64
D_HEADS, D_DK, D_DV = 4, 128, 256
A_KW, A_VW = A_HEADS * A_DK, A_HEADS * A_DV
B_KW, B_VW = B_HEADS * B_DK, B_HEADS * B_DV
B_QKV = 2 * B_KW + B_VW
C_W = C_GROUPS * C_GROUP
D_KW, D_VW = D_HEADS * D_DK, D_HEADS * D_DV

LANES = 128
SUBLANES = 8
VMEM_BYTES_V7X = 64 * 1024 * 1024

AB_Q, AB_K, AB_V, AB_Z = 0, A_KW, 2 * A_KW, 2 * A_KW + A_VW
AB_QKV = AB_Z + A_VW
AB_ZB = AB_QKV + B_QKV
AB_SMALL = AB_ZB + B_VW
AB_GA_LANE, AB_BETA_LANE, AB_APRE_LANE = 0, A_GATE_RANK, A_GATE_RANK + B_HEADS
CD_U, CD_Z = 0, C_W
CD_Q = 2 * C_W
CD_K = CD_Q + D_KW
CD_V = CD_K + D_KW
CD_O = CD_V + D_VW
CD_ZD = CD_O + D_VW
CD_SMALL = CD_ZD + D_VW
CD_I_LANE, CD_F_LANE = 0, D_HEADS

PROJ_TN = 1280
S5_CHUNK = 16
S5_GROUP_BLOCK = 8
CONV_SLAB = 512


def _round_up(x, m):
    return (x + m - 1) // m * m


def _vmem_limit(block_bytes, scratch_bytes=0):
    est = 2 * block_bytes + scratch_bytes
    return int(min(max(2 * est, 32 * 1024 * 1024), VMEM_BYTES_V7X - 8 * 1024 * 1024))


def _mm(a, b):
    return jnp.dot(a, b, preferred_element_type=F32)


def _dot(a, b):
    return _mm(a.astype(BF16), b.astype(BF16))


def _dot_nt(a, b):
    return lax.dot_general(a.astype(BF16), b.astype(BF16), (((1,), (1,)), ((), ())),
                           preferred_element_type=F32)


def _dot_tn(a, b):
    return lax.dot_general(a.astype(BF16), b.astype(BF16), (((0,), (0,)), ((), ())),
                           preferred_element_type=F32)


def _split2(x):
    hi = x.astype(BF16)
    return hi, (x - hi.astype(F32)).astype(BF16)


def _split3(x):
    hi = x.astype(BF16)
    r = x - hi.astype(F32)
    mid = r.astype(BF16)
    return hi, mid, (r - mid.astype(F32)).astype(BF16)


def _dot3(a, b):
    ah, al = _split2(a)
    bh, bl = _split2(b)
    return _mm(ah, bh) + _mm(ah, bl) + _mm(al, bh)


def _cumsum_rows(tri_bf16, x):
    hi, mid, lo = _split3(x)
    return _mm(tri_bf16, hi) + _mm(tri_bf16, mid) + _mm(tri_bf16, lo)


def _select_rows(sel_bf16, x):
    nt = lambda b: lax.dot_general(sel_bf16, b, (((1,), (1,)), ((), ())), preferred_element_type=F32)
    hi, mid, lo = _split3(x)
    return nt(hi) + nt(mid) + nt(lo)


def _lane_selector(lane0):
    r = lax.broadcasted_iota(jnp.int32, (SUBLANES, LANES), 0)
    l = lax.broadcasted_iota(jnp.int32, (SUBLANES, LANES), 1)
    return jnp.where(l == r + lane0, 1.0, 0.0).astype(BF16)


def _causal_masks(c):
    row = lax.broadcasted_iota(jnp.int32, (c, c), 0)
    col = lax.broadcasted_iota(jnp.int32, (c, c), 1)
    return row >= col, row > col


def _norm_matmul_kernel(x_ref, g_ref, w_ref, o_ref, xn_ref):
    @pl.when(pl.program_id(1) == 0)
    def _():
        x = x_ref[...]
        y = x * lax.rsqrt(jnp.mean(x * x, axis=-1, keepdims=True) + NORM_EPS) * g_ref[...]
        xn_ref[...] = y.astype(BF16)

    o_ref[...] = jnp.dot(xn_ref[...], w_ref[...], preferred_element_type=F32)


def _norm_matmul(x2d, g_row, w_bf16):
    m, d = x2d.shape
    n = w_bf16.shape[1]
    tm = min(m, 1024)
    tn = PROJ_TN
    assert m % tm == 0 and n % tn == 0
    blocks = tm * d * 4 + d * tn * 2 + tm * tn * 4
    return pl.pallas_call(
        _norm_matmul_kernel,
        out_shape=jax.ShapeDtypeStruct((m, n), F32),
        grid=(m // tm, n // tn),
        in_specs=[pl.BlockSpec((tm, d), lambda i, j: (i, 0)),
                  pl.BlockSpec((1, d), lambda i, j: (0, 0)),
                  pl.BlockSpec((d, tn), lambda i, j: (0, j))],
        out_specs=pl.BlockSpec((tm, tn), lambda i, j: (i, j)),
        scratch_shapes=[pltpu.VMEM((tm, d), BF16)],
        compiler_params=pltpu.CompilerParams(
            dimension_semantics=("parallel", "arbitrary"),
            vmem_limit_bytes=_vmem_limit(blocks, tm * d * 2)),
        name="norm_in_proj",
    )(x2d, g_row, w_bf16)


def _out_proj_kernel(a_ref, b_ref, wa_ref, wb_ref, h_ref, o_ref):
    out = (jnp.dot(a_ref[...], wa_ref[...], preferred_element_type=F32)
           + jnp.dot(b_ref[...], wb_ref[...], preferred_element_type=F32))
    o_ref[...] = h_ref[...] + out


def _out_proj_norm_kernel(a_ref, b_ref, wa_ref, wb_ref, h_ref, g_ref, o_ref):
    out = (jnp.dot(a_ref[...], wa_ref[...], preferred_element_type=F32)
           + jnp.dot(b_ref[...], wb_ref[...], preferred_element_type=F32))
    h = h_ref[...] + out
    o_ref[...] = h * lax.rsqrt(jnp.mean(h * h, axis=-1, keepdims=True) + NORM_EPS) * g_ref[...]


def _out_proj(mix_a, mix_b, w_a, w_b, h2d, final_g_row=None):
    m, d = h2d.shape
    ka, kb = mix_a.shape[1], mix_b.shape[1]
    tm = min(m, 512)
    assert m % tm == 0
    in_specs = [pl.BlockSpec((tm, ka), lambda i: (i, 0)),
                pl.BlockSpec((tm, kb), lambda i: (i, 0)),
                pl.BlockSpec((ka, d), lambda i: (0, 0)),
                pl.BlockSpec((kb, d), lambda i: (0, 0)),
                pl.BlockSpec((tm, d), lambda i: (i, 0))]
    args = [mix_a, mix_b, w_a, w_b, h2d]
    kernel = _out_proj_kernel
    if final_g_row is not None:
        in_specs.append(pl.BlockSpec((1, d), lambda i: (0, 0)))
        args.append(final_g_row)
        kernel = _out_proj_norm_kernel
    blocks = tm * (ka + kb) * 2 + (ka + kb) * d * 2 + 2 * tm * d * 4
    return pl.pallas_call(
        kernel,
        out_shape=jax.ShapeDtypeStruct((m, d), F32),
        grid=(m // tm,),
        in_specs=in_specs,
        out_specs=pl.BlockSpec((tm, d), lambda i: (i, 0)),
        compiler_params=pltpu.CompilerParams(
            dimension_semantics=("parallel",),
            vmem_limit_bytes=_vmem_limit(blocks)),
        name="out_proj_norm" if final_g_row is not None else "out_proj",
    )(*args)


def _gla_kernel(q_ref, k_ref, v_ref, z_ref, sm_ref, gw_ref, gb_ref, ng_ref, s0_ref,
                o_ref, sout_ref, st_ref, *, c, nchunks):
    t = pl.program_id(1)
    last_t = pl.num_programs(1) - 1

    @pl.when(t == 0)
    def _():
        for h in range(A_HEADS):
            st_ref[h] = s0_ref[h].T

    causal, _ = _causal_masks(c)
    tri = causal.astype(BF16)
    gw = gw_ref[...]
    gb = gb_ref[...]

    def body(n, carry):
        sl = pl.ds(pl.multiple_of(n * c, c), c)
        log_alpha = jax.nn.log_sigmoid(_dot(sm_ref[sl, :], gw) + gb) * (1.0 / A_GATE_TAU)
        b_all = _cumsum_rows(tri, log_alpha)
        heads = range(A_HEADS)
        ksl = [slice(h * A_DK, (h + 1) * A_DK) for h in heads]
        vsl = [slice(h * A_DV, (h + 1) * A_DV) for h in heads]
        b = [b_all[:, ksl[h]] for h in heads]
        b_last = [b[h][c - 1:c, :] for h in heads]
        k = [k_ref[sl, ksl[h]] for h in heads]
        v = [v_ref[sl, vsl[h]].astype(BF16) for h in heads]
        q_dec = [(q_ref[sl, ksl[h]] * (A_DK ** -0.5) * jnp.exp(b[h])).astype(BF16) for h in heads]
        k_dec = [(k[h] * jnp.exp(-b[h])).astype(BF16) for h in heads]
        k_w = [(k[h] * jnp.exp(b_last[h] - b[h])).astype(BF16) for h in heads]
        scores = [jnp.where(causal, _dot_nt(q_dec[h], k_dec[h]), 0.0).astype(BF16) for h in heads]
        s_t = [st_ref[h] for h in heads]
        outs = [_mm(scores[h], v[h]) + _dot_nt(q_dec[h], s_t[h]) for h in heads]
        for h in heads:
            st_ref[h] = s_t[h] * jnp.exp(b_last[h]) + _dot_tn(v[h], k_w[h])
        for h in heads:
            o = outs[h]
            o = o * lax.rsqrt(jnp.mean(o * o, axis=-1, keepdims=True) + NORM_EPS) * ng_ref[:, vsl[h]]
            o_ref[sl, vsl[h]] = (o * jax.nn.silu(z_ref[sl, vsl[h]])).astype(o_ref.dtype)
        return carry

    lax.fori_loop(0, nchunks, body, 0)

    @pl.when(t == last_t)
    def _():
        for h in range(A_HEADS):
            sout_ref[h] = st_ref[h].T


def _gla(proj3, gate_w_pad, gate_b_row, norm_g_row, s0, *, c, tb):
    bsz, l, _ = proj3.shape
    nblk = l // tb
    tok = lambda col: (lambda b, t: (b, t, col))
    in_specs = [
        pl.BlockSpec((None, tb, A_KW), tok(AB_Q // A_KW)),
        pl.BlockSpec((None, tb, A_KW), tok(AB_K // A_KW)),
        pl.BlockSpec((None, tb, A_VW), tok(AB_V // A_VW)),
        pl.BlockSpec((None, tb, A_VW), tok(AB_Z // A_VW)),
        pl.BlockSpec((None, tb, LANES), tok(AB_SMALL // LANES)),
        pl.BlockSpec((LANES, A_KW), lambda b, t: (0, 0)),
        pl.BlockSpec((1, A_KW), lambda b, t: (0, 0)),
        pl.BlockSpec((1, A_VW), lambda b, t: (0, 0)),
        pl.BlockSpec((None, A_HEADS, A_DK, A_DV), lambda b, t: (b, 0, 0, 0)),
    ]
    out_specs = [
        pl.BlockSpec((None, tb, A_VW), lambda b, t: (b, t, 0)),
        pl.BlockSpec((None, A_HEADS, A_DK, A_DV), lambda b, t: (b, 0, 0, 0)),
    ]
    blocks = tb * (2 * A_KW + 2 * A_VW + LANES) * 4 + tb * A_VW * 2 + 2 * A_HEADS * A_DK * A_DV * 4
    return pl.pallas_call(
        functools.partial(_gla_kernel, c=c, nchunks=tb // c),
        out_shape=[jax.ShapeDtypeStruct((bsz, l, A_VW), BF16),
                   jax.ShapeDtypeStruct((bsz, A_HEADS, A_DK, A_DV), F32)],
        grid=(bsz, nblk),
        in_specs=in_specs,
        out_specs=out_specs,
        scratch_shapes=[pltpu.VMEM((A_HEADS, A_DV, A_DK), F32)],
        compiler_params=pltpu.CompilerParams(
            dimension_semantics=("parallel", "arbitrary"),
            vmem_limit_bytes=_vmem_limit(blocks, A_HEADS * A_DK * A_DV * 4)),
        name="gla_mixer",
    )(proj3, proj3, proj3, proj3, proj3, gate_w_pad, gate_b_row, norm_g_row, s0)


def _gdn_kernel(x_ref, z_ref, sm_ref, w_ref, cp_ref, alog_ref, dtb_ref, ng_ref, s0_ref,
                o_ref, sout_ref, s_ref, tail_ref, conv_ref, u_ref, wm_ref, qg_ref, kg_ref, qk_ref, gl_ref,
                *, c, nchunks):
    t = pl.program_id(1)
    last_t = pl.num_programs(1) - 1
    tb = c * nchunks
    keep = SUBLANES - (B_CONV - 1)

    @pl.when(t == 0)
    def _():
        s_ref[...] = s0_ref[...]
        tail_ref[0:keep, :] = jnp.zeros((keep, B_QKV), F32)
        tail_ref[keep:SUBLANES, :] = cp_ref[...]

    for j in range(0, B_QKV, CONV_SLAB):
        cs = slice(j, j + CONV_SLAB)
        x = x_ref[:, cs]
        w = w_ref[:, cs]
        ext = jnp.concatenate([tail_ref[:, cs], x], axis=0)
        conv = ext[keep:keep + tb, :] * w[0:1, :]
        conv = conv + ext[keep + 1:keep + 1 + tb, :] * w[1:2, :]
        conv = conv + ext[keep + 2:keep + 2 + tb, :] * w[2:3, :]
        conv = conv + x * w[3:4, :]
        conv_ref[:, cs] = jax.nn.silu(conv)
        tail_ref[:, cs] = x[tb - SUBLANES:tb, :]

    causal, strict = _causal_masks(c)
    tri = causal.astype(BF16)
    eye = jnp.where(causal & jnp.logical_not(strict), 1.0, 0.0).astype(F32)
    sel = _lane_selector(AB_APRE_LANE)
    neg_a_exp = -jnp.exp(alog_ref[...])
    dtb = dtb_ref[...]
    ng = ng_ref[...]
    n_double = int(math.log2(c)) - 1
    heads = range(B_HEADS)

    def wy_factors(n, carry):
        sl = pl.ds(pl.multiple_of(n * c, c), c)
        sm = sm_ref[sl, :]
        g_cum = _cumsum_rows(tri, neg_a_exp * jax.nn.softplus(sm + dtb))
        g_rows = _select_rows(sel, g_cum)
        beta_all = jax.nn.sigmoid(sm)
        gl_ref[pl.ds(n, 1), :] = g_cum[c - 1:c, :]
        pws, rhss = [], []
        for h in heads:
            hs = slice(h * B_DK, (h + 1) * B_DK)
            q = conv_ref[sl, hs]
            k = conv_ref[sl, B_KW + h * B_DK:B_KW + (h + 1) * B_DK]
            v = conv_ref[sl, 2 * B_KW + h * B_DV:2 * B_KW + (h + 1) * B_DV]
            q = q * lax.rsqrt(jnp.sum(q * q, axis=-1, keepdims=True) + NORM_EPS) * (B_DK ** -0.5)
            k = k * lax.rsqrt(jnp.sum(k * k, axis=-1, keepdims=True) + NORM_EPS)
            g_col = g_cum[:, AB_APRE_LANE + h:AB_APRE_LANE + h + 1]
            beta = beta_all[:, AB_BETA_LANE + h:AB_BETA_LANE + h + 1]
            decay = jnp.exp(jnp.where(causal, g_col - g_rows[h:h + 1, :], -jnp.inf))
            e_g = jnp.exp(g_col)
            k_beta = k * beta
            pws.append(-jnp.where(strict, _dot_nt(k_beta, k) * decay, 0.0))
            rhss.append(jnp.concatenate([v * beta, k_beta * e_g], axis=1))
            qk_ref[sl, h * LANES:h * LANES + c] = jnp.where(causal, _dot_nt(q, k) * decay, 0.0).astype(BF16)
            qg_ref[sl, hs] = (q * e_g).astype(BF16)
            kg_ref[sl, hs] = (k * jnp.exp(g_col[c - 1:c, :] - g_col)).astype(BF16)
        invs = [eye + p for p in pws]
        for _ in range(n_double):
            pws = [_dot3(p, p) for p in pws]
            invs = [i + _dot3(i, p) for i, p in zip(invs, pws)]
        for h in heads:
            hs = slice(h * B_DK, (h + 1) * B_DK)
            uw = _dot3(invs[h], rhss[h])
            u_ref[sl, hs] = uw[:, :B_DV]
            wm_ref[sl, hs] = uw[:, B_DV:].astype(BF16)
        return carry

    lax.fori_loop(0, nchunks, wy_factors, 0)

    def recurrence(n, carry):
        sl = pl.ds(pl.multiple_of(n * c, c), c)
        e_last = jnp.exp(gl_ref[pl.ds(n, 1), :])
        hsl = [slice(h * B_DK, (h + 1) * B_DK) for h in heads]
        s_old = [s_ref[h] for h in heads]
        s_bf = [s.astype(BF16) for s in s_old]
        v_new = [(u_ref[sl, hsl[h]] - _mm(wm_ref[sl, hsl[h]], s_bf[h])).astype(BF16) for h in heads]
        outs = [_mm(qg_ref[sl, hsl[h]], s_bf[h]) + _mm(qk_ref[sl, h * LANES:h * LANES + c], v_new[h])
                for h in heads]
        for h in heads:
            s_ref[h] = (e_last[:, AB_APRE_LANE + h:AB_APRE_LANE + h + 1] * s_old[h]
                        + lax.dot_general(kg_ref[sl, hsl[h]], v_new[h], (((0,), (0,)), ((), ())),
                                          preferred_element_type=F32))
        for h in heads:
            o = outs[h]
            o = o * lax.rsqrt(jnp.mean(o * o, axis=-1, keepdims=True) + NORM_EPS) * ng
            o_ref[sl, hsl[h]] = (o * jax.nn.silu(z_ref[sl, hsl[h]])).astype(o_ref.dtype)
        return carry

    lax.fori_loop(0, nchunks, recurrence, 0)

    @pl.when(t == last_t)
    def _():
        sout_ref[...] = s_ref[...]


def _gdn(proj3, conv_prev, conv_w, alog_row, dtb_row, norm_g_row, s0, *, c, tb):
    bsz, l, _ = proj3.shape
    nblk = l // tb
    assert AB_QKV % B_QKV == 0 and AB_ZB % B_VW == 0
    in_specs = [
        pl.BlockSpec((None, tb, B_QKV), lambda b, t: (b, t, AB_QKV // B_QKV)),
        pl.BlockSpec((None, tb, B_VW), lambda b, t: (b, t, AB_ZB // B_VW)),
        pl.BlockSpec((None, tb, LANES), lambda b, t: (b, t, AB_SMALL // LANES)),
        pl.BlockSpec((B_CONV, B_QKV), lambda b, t: (0, 0)),
        pl.BlockSpec((None, B_CONV - 1, B_QKV), lambda b, t: (b, 0, 0)),
        pl.BlockSpec((1, LANES), lambda b, t: (0, 0)),
        pl.BlockSpec((1, LANES), lambda b, t: (0, 0)),
        pl.BlockSpec((1, B_DV), lambda b, t: (0, 0)),
        pl.BlockSpec((None, B_HEADS, B_DK, B_DV), lambda b, t: (b, 0, 0, 0)),
    ]
    out_specs = [
        pl.BlockSpec((None, tb, B_VW), lambda b, t: (b, t, 0)),
        pl.BlockSpec((None, B_HEADS, B_DK, B_DV), lambda b, t: (b, 0, 0, 0)),
    ]
    blocks = tb * (B_QKV + B_VW + LANES) * 4 + tb * B_VW * 2 + 2 * B_HEADS * B_DK * B_DV * 4
    scratch = (B_HEADS * B_DK * B_DV * 4 + SUBLANES * B_QKV * 4 + tb * B_QKV * 4 + tb * B_VW * 4
               + 4 * tb * B_KW * 2 + SUBLANES * LANES * 4)
    return pl.pallas_call(
        functools.partial(_gdn_kernel, c=c, nchunks=tb // c),
        out_shape=[jax.ShapeDtypeStruct((bsz, l, B_VW), BF16),
                   jax.ShapeDtypeStruct((bsz, B_HEADS, B_DK, B_DV), F32)],
        grid=(bsz, nblk),
        in_specs=in_specs,
        out_specs=out_specs,
        scratch_shapes=[pltpu.VMEM((B_HEADS, B_DK, B_DV), F32),
                        pltpu.VMEM((SUBLANES, B_QKV), F32),
                        pltpu.VMEM((tb, B_QKV), F32),
                        pltpu.VMEM((tb, B_VW), F32),
                        pltpu.VMEM((tb, B_KW), BF16),
                        pltpu.VMEM((tb, B_KW), BF16),
                        pltpu.VMEM((tb, B_KW), BF16),
                        pltpu.VMEM((tb, B_HEADS * LANES), BF16),
                        pltpu.VMEM((SUBLANES, LANES), F32)],
        compiler_params=pltpu.CompilerParams(
            dimension_semantics=("parallel", "arbitrary"),
            vmem_limit_bytes=_vmem_limit(blocks, scratch)),
        name="gdn_mixer",
    )(proj3, proj3, proj3, conv_w, conv_prev, alog_row, dtb_row, norm_g_row, s0)


def _mlstm_kernel(q_ref, k_ref, v_ref, og_ref, z_ref, sm_ref, ib_ref, fb_ref, ng_ref,
                  c0_ref, n0_ref, m0_ref, o_ref, cout_ref, nout_ref, mout_ref,
                  c_ref, n_ref, m_ref, *, c, nchunks):
    t = pl.program_id(1)
    last_t = pl.num_programs(1) - 1

    @pl.when(t == 0)
    def _():
        c_ref[...] = c0_ref[...]
        n_ref[...] = n0_ref[...]
        m_ref[...] = m0_ref[...]

    causal, _ = _causal_masks(c)
    tri = causal.astype(BF16)
    sel = _lane_selector(0)
    lane = lax.broadcasted_iota(jnp.int32, (c, LANES), 1)
    ib = ib_ref[...]
    fb = fb_ref[...]

    def body(n, carry):
        sl = pl.ds(pl.multiple_of(n * c, c), c)
        sm = sm_ref[sl, :]
        i_full = sm + ib
        b_full = _cumsum_rows(tri, jax.nn.log_sigmoid(sm + fb))
        rows = _select_rows(sel, jnp.where(lane < CD_F_LANE, i_full, b_full))
        heads = range(D_HEADS)
        ksl = [slice(h * D_DK, (h + 1) * D_DK) for h in heads]
        vsl = [slice(h * D_DV, (h + 1) * D_DV) for h in heads]
        b_col = [b_full[:, CD_F_LANE + h:CD_F_LANE + h + 1] for h in heads]
        i_col = [i_full[:, CD_I_LANE + h:CD_I_LANE + h + 1] for h in heads]
        logw = [jnp.where(causal, b_col[h] - rows[CD_F_LANE + h:CD_F_LANE + h + 1, :]
                          + rows[CD_I_LANE + h:CD_I_LANE + h + 1, :], -jnp.inf) for h in heads]
        m_intra = [jnp.max(logw[h], axis=-1, keepdims=True) for h in heads]
        q = [q_ref[sl, ksl[h]] * (D_DK ** -0.5) for h in heads]
        q_bf = [x.astype(BF16) for x in q]
        k = [k_ref[sl, ksl[h]] for h in heads]
        v = [v_ref[sl, vsl[h]].astype(BF16) for h in heads]
        p = [jnp.exp(logw[h] - m_intra[h]) * _dot_nt(q_bf[h], k[h]) for h in heads]
        h_intra = [_dot(p[h], v[h]) for h in heads]
        n_intra = [jnp.sum(p[h], axis=-1, keepdims=True) for h in heads]
        b_last = [b_col[h][c - 1:c, :] for h in heads]
        m_chunk = [m_intra[h][c - 1:c, :] for h in heads]
        k_w = [k[h] * jnp.exp(b_last[h] - b_col[h] + i_col[h] - m_chunk[h]) for h in heads]
        kv = [_dot_tn(k_w[h], v[h]) for h in heads]
        k_sum = [jnp.sum(k_w[h], axis=0, keepdims=True) for h in heads]
        c_s = [c_ref[h] for h in heads]
        n_s = [n_ref[h:h + 1, :] for h in heads]
        m_s = [m_ref[:, h:h + 1] for h in heads]
        qc = [_dot(q_bf[h], c_s[h]) for h in heads]
        for h in heads:
            m_new = jnp.maximum(b_last[h] + m_s[h], m_chunk[h])
            w_old = jnp.exp(b_last[h] + m_s[h] - m_new)
            w_new = jnp.exp(m_chunk[h] - m_new)
            c_ref[h] = w_old * c_s[h] + w_new * kv[h]
            n_ref[h:h + 1, :] = w_old * n_s[h] + w_new * k_sum[h]
            m_ref[:, h:h + 1] = m_new
        for h in heads:
            a = b_col[h] + m_s[h]
            m_t = jnp.maximum(a, m_intra[h])
            w_a = jnp.exp(a - m_t)
            w_i = jnp.exp(m_intra[h] - m_t)
            num = w_a * qc[h] + w_i * h_intra[h]
            den = w_a * jnp.sum(q[h] * n_s[h], axis=-1, keepdims=True) + w_i * n_intra[h]
            hh = num / jnp.maximum(jnp.abs(den), jnp.exp(-m_t))
            hd = jax.nn.sigmoid(og_ref[sl, vsl[h]]) * hh
            oc = hd - jnp.mean(hd, axis=-1, keepdims=True)
            o = oc * lax.rsqrt(jnp.mean(oc * oc, axis=-1, keepdims=True) + NORM_EPS) * ng_ref[:, vsl[h]]
            o_ref[sl, vsl[h]] = (o * jax.nn.silu(z_ref[sl, vsl[h]])).astype(o_ref.dtype)
        return carry

    lax.fori_loop(0, nchunks, body, 0)

    @pl.when(t == last_t)
    def _():
        cout_ref[...] = c_ref[...]
        nout_ref[...] = n_ref[...]
        mout_ref[...] = m_ref[...]


def _mlstm(proj3, ib_row, fb_row, norm_g_row, c0, n0, m0, *, c, tb):
    bsz, l, _ = proj3.shape
    nblk = l // tb
    tok = lambda col: (lambda b, t: (b, t, col))
    st4 = lambda b, t: (b, 0, 0, 0)
    st3 = lambda b, t: (b, 0, 0)
    in_specs = [
        pl.BlockSpec((None, tb, D_KW), tok(CD_Q // D_KW)),
        pl.BlockSpec((None, tb, D_KW), tok(CD_K // D_KW)),
        pl.BlockSpec((None, tb, D_VW), tok(CD_V // D_VW)),
        pl.BlockSpec((None, tb, D_VW), tok(CD_O // D_VW)),
        pl.BlockSpec((None, tb, D_VW), tok(CD_ZD // D_VW)),
        pl.BlockSpec((None, tb, LANES), tok(CD_SMALL // LANES)),
        pl.BlockSpec((1, LANES), lambda b, t: (0, 0)),
        pl.BlockSpec((1, LANES), lambda b, t: (0, 0)),
        pl.BlockSpec((1, D_VW), lambda b, t: (0, 0)),
        pl.BlockSpec((None, D_HEADS, D_DK, D_DV), st4),
        pl.BlockSpec((None, D_HEADS, D_DK), st3),
        pl.BlockSpec((None, 1, D_HEADS), st3),
    ]
    out_specs = [
        pl.BlockSpec((None, tb, D_VW), lambda b, t: (b, t, 0)),
        pl.BlockSpec((None, D_HEADS, D_DK, D_DV), st4),
        pl.BlockSpec((None, D_HEADS, D_DK), st3),
        pl.BlockSpec((None, 1, D_HEADS), st3),
    ]
    blocks = tb * (2 * D_KW + 3 * D_VW + LANES) * 4 + tb * D_VW * 2 + 2 * D_HEADS * D_DK * D_DV * 4
    return pl.pallas_call(
        functools.partial(_mlstm_kernel, c=c, nchunks=tb // c),
        out_shape=[jax.ShapeDtypeStruct((bsz, l, D_VW), BF16),
                   jax.ShapeDtypeStruct((bsz, D_HEADS, D_DK, D_DV), F32),
                   jax.ShapeDtypeStruct((bsz, D_HEADS, D_DK), F32),
                   jax.ShapeDtypeStruct((bsz, 1, D_HEADS), F32)],
        grid=(bsz, nblk),
        in_specs=in_specs,
        out_specs=out_specs,
        scratch_shapes=[pltpu.VMEM((D_HEADS, D_DK, D_DV), F32),
                        pltpu.VMEM((D_HEADS, D_DK), F32),
                        pltpu.VMEM((1, D_HEADS), F32)],
        compiler_params=pltpu.CompilerParams(
            dimension_semantics=("parallel", "arbitrary"),
            vmem_limit_bytes=_vmem_limit(blocks, D_HEADS * D_DK * D_DV * 4)),
        name="mlstm_mixer",
    )(proj3, proj3, proj3, proj3, proj3, proj3, ib_row, fb_row, norm_g_row, c0, n0, m0)


def _s5_expand_operators(kc_ref, bc_ref, cc_ref, bd_ref, bst_ref, cst_ref):
    tc, gt, sw, cg = S5_CHUNK, S5_GROUP_BLOCK, 2 * C_STATE, C_GROUP
    w_t = tc * LANES
    iota = lambda shape, d: lax.broadcasted_iota(jnp.int32, shape, d)
    row_g = (iota((w_t, LANES), 0) // cg) % gt
    tile16 = jnp.where(iota((cg, LANES), 1) % cg == iota((cg, LANES), 0), 1.0, 0.0).astype(BF16)
    bd_ref[...] = jnp.where(row_g == iota((w_t, LANES), 1) // cg, _mm(kc_ref[...], tile16), 0.0).astype(BF16)
    bc = bc_ref[...].astype(F32)
    for g in range(gt):
        bst_ref[:, g * sw:(g + 1) * sw] = jnp.where(row_g == g, bc, 0.0).astype(BF16)
    src, dst = iota((tc * cg, w_t), 0), iota((tc * cg, w_t), 1)
    spread = jnp.where((src // cg == dst // LANES) & (src % cg == dst % cg), 1.0, 0.0).astype(BF16)
    lane_g = (iota((sw, w_t), 1) // cg) % gt
    for g in range(gt):
        full = _mm(cc_ref[g * sw:(g + 1) * sw, :], spread)
        cst_ref[g * sw:(g + 1) * sw, :] = jnp.where(lane_g == g, full, 0.0).astype(BF16)


def _s5_kernel(u_ref, kc_ref, bc_ref, cc_ref, apow_ref, x0_ref, y_ref, xf_ref, bd_ref, bst_ref, cst_ref,
               *, nc, bb):
    m = nc * bb
    tc = S5_CHUNK
    sw = 2 * C_STATE

    @pl.when(pl.program_id(1) == 0)
    def _():
        _s5_expand_operators(kc_ref, bc_ref, cc_ref, bd_ref, bst_ref, cst_ref)

    row = lax.broadcasted_iota(jnp.int32, (m, sw), 0)
    n_idx = row & (nc - 1)
    n_log = int(math.log2(nc))

    def cmul(a1, a2, x):
        return a1 * x + a2 * pltpu.roll(x, C_STATE, 1)

    lhs = jnp.concatenate([u_ref[pl.ds(tau, m, stride=tc), :].astype(BF16) for tau in range(tc)], axis=1)
    e_all = _mm(lhs, bst_ref[...])
    x_start = []
    for g in range(S5_GROUP_BLOCK):
        gs = slice(g * sw, (g + 1) * sw)
        x0_rows = jnp.zeros((m, sw), F32)
        for b in range(bb):
            x0_rows = jnp.where(row == b * nc, x0_ref[b, :, gs], x0_rows)
        x = e_all[:, gs] + cmul(apow_ref[0:1, gs], apow_ref[1:2, gs], x0_rows)
        for j in range(n_log):
            sh = 1 << j
            shifted = jnp.where(n_idx >= sh, pltpu.roll(x, sh, 0), 0.0)
            x = x + cmul(apow_ref[2 * j:2 * j + 1, gs], apow_ref[2 * j + 1:2 * j + 2, gs], shifted)
        x_start.append(jnp.where(n_idx >= 1, pltpu.roll(x, 1, 0), x0_rows).astype(BF16))
        for b in range(bb):
            xf_ref[b, :, gs] = x[b * nc + nc - 1:b * nc + nc, :]
    y_state = _mm(jnp.concatenate(x_start, axis=1), cst_ref[...])
    for tau in range(tc):
        y = (_mm(lhs[:, :(tau + 1) * LANES], bd_ref[(tc - 1 - tau) * LANES:, :])
             + y_state[:, tau * LANES:(tau + 1) * LANES])
        y_ref[pl.ds(tau, m, stride=tc), :] = y


def _s5_chunks(proj2d, kc, bc, cc, apow, x0, *, l, bb):
    tc = S5_CHUNK
    nc = l // tc
    bsz = x0.shape[1]
    nt = C_GROUPS // S5_GROUP_BLOCK
    rows = bb * l
    sw = 2 * C_STATE
    sw_t = S5_GROUP_BLOCK * sw
    w_t = tc * LANES
    assert CD_U % LANES == 0 and bsz % bb == 0
    blocks = (2 * rows * LANES * 4 + 2 * w_t * LANES * 2 + sw_t * tc * C_GROUP * 2 + apow.shape[1] * sw_t * 4
              + 2 * bb * SUBLANES * sw_t * 4)
    scratch = w_t * LANES * 2 + 2 * w_t * sw_t * 2
    temps = bb * nc * (w_t * 2 + w_t * 4 + 3 * sw_t * 4) + 4 * sw * w_t * 4
    return pl.pallas_call(
        functools.partial(_s5_kernel, nc=nc, bb=bb),
        out_shape=[jax.ShapeDtypeStruct((bsz * l, C_W), F32),
                   jax.ShapeDtypeStruct((nt, bsz, 1, sw_t), F32)],
        grid=(nt, bsz // bb),
        in_specs=[pl.BlockSpec((rows, LANES), lambda i, j: (j, CD_U // LANES + i)),
                  pl.BlockSpec((None, w_t, C_GROUP), lambda i, j: (i, 0, 0)),
                  pl.BlockSpec((None, w_t, sw), lambda i, j: (i, 0, 0)),
                  pl.BlockSpec((None, sw_t, tc * C_GROUP), lambda i, j: (i, 0, 0)),
                  pl.BlockSpec((None, apow.shape[1], sw_t), lambda i, j: (i, 0, 0)),
                  pl.BlockSpec((None, bb, 1, sw_t), lambda i, j: (i, j, 0, 0))],
        out_specs=[pl.BlockSpec((rows, LANES), lambda i, j: (j, i)),
                   pl.BlockSpec((None, bb, 1, sw_t), lambda i, j: (i, j, 0, 0))],
        scratch_shapes=[pltpu.VMEM((w_t, LANES), BF16),
                        pltpu.VMEM((w_t, sw_t), BF16),
                        pltpu.VMEM((sw_t, w_t), BF16)],
        compiler_params=pltpu.CompilerParams(
            dimension_semantics=("parallel", "arbitrary"),
            vmem_limit_bytes=_vmem_limit(blocks, scratch + temps)),
        name="s5_chunks",
    )(proj2d, kc, bc, cc, apow, x0)


def _s5_glu_kernel(y_ref, u_ref, z_ref, d_ref, w_ref, b_ref, o_ref):
    y = jax.nn.gelu(y_ref[...] + d_ref[...] * u_ref[...])
    gate = jax.nn.sigmoid(jnp.dot(y.astype(BF16), w_ref[...], preferred_element_type=F32) + b_ref[...])
    o_ref[...] = (y * gate * jax.nn.silu(z_ref[...])).astype(o_ref.dtype)


def _s5_glu(y2d, proj2d, d_row, glu_w, glu_b_row):
    m = y2d.shape[0]
    tm = min(m, 512)
    blocks = 3 * tm * C_W * 4 + C_W * C_W * 2 + tm * C_W * 2
    return pl.pallas_call(
        _s5_glu_kernel,
        out_shape=jax.ShapeDtypeStruct((m, C_W), BF16),
        grid=(m // tm,),
        in_specs=[pl.BlockSpec((tm, C_W), lambda i: (i, 0)),
                  pl.BlockSpec((tm, C_W), lambda i: (i, CD_U // C_W)),
                  pl.BlockSpec((tm, C_W), lambda i: (i, CD_Z // C_W)),
                  pl.BlockSpec((1, C_W), lambda i: (0, 0)),
                  pl.BlockSpec((C_W, C_W), lambda i: (0, 0)),
                  pl.BlockSpec((1, C_W), lambda i: (0, 0))],
        out_specs=pl.BlockSpec((tm, C_W), lambda i: (i, 0)),
        compiler_params=pltpu.CompilerParams(
            dimension_semantics=("parallel",),
            vmem_limit_bytes=_vmem_limit(blocks)),
        name="s5_glu",
    )(y2d, proj2d, proj2d, d_row, glu_w, glu_b_row)


def _s5_operators(lam_re, lam_im, log_dt, b_re, b_im, c_re, c_im, n_log):
    g, p = lam_re.shape
    tc = S5_CHUNK
    dt = jnp.exp(log_dt.astype(F32))[:, None]
    mag = jnp.exp(lam_re * dt)
    ab_re, ab_im = mag * jnp.cos(lam_im * dt), mag * jnp.sin(lam_im * dt)
    den = lam_re * lam_re + lam_im * lam_im
    er = ab_re - 1.0
    zr = (er * lam_re + ab_im * lam_im) / den
    zi = (ab_im * lam_re - er * lam_im) / den
    bb_re = zr[..., None] * b_re - zi[..., None] * b_im
    bb_im = zr[..., None] * b_im + zi[..., None] * b_re
    pr, pi = [jnp.ones_like(ab_re)], [jnp.zeros_like(ab_re)]
    for _ in range(tc):
        pr, pi = pr + [pr[-1] * ab_re - pi[-1] * ab_im], pi + [pr[-1] * ab_im + pi[-1] * ab_re]
    pw_re, pw_im = jnp.stack(pr), jnp.stack(pi)
    abr = pw_re[:tc, :, :, None] * bb_re - pw_im[:tc, :, :, None] * bb_im
    abi = pw_re[:tc, :, :, None] * bb_im + pw_im[:tc, :, :, None] * bb_re
    kern = (jnp.einsum('gjp,dgpi->dgji', c_re, abr, precision=HIGHEST)
            - jnp.einsum('gjp,dgpi->dgji', c_im, abi, precision=HIGHEST))
    gt = S5_GROUP_BLOCK
    nt = g // gt
    sw = 2 * p
    kc = kern[::-1].reshape(tc, nt, gt, C_GROUP, C_GROUP).transpose(1, 0, 2, 4, 3)
    kc = kc.reshape(nt, tc * gt * C_GROUP, C_GROUP)
    ab = jnp.concatenate([abr, abi], axis=2)[::-1]
    bc = ab.reshape(tc, nt, gt, sw, C_GROUP).transpose(1, 0, 2, 4, 3).reshape(nt, tc * gt * C_GROUP, sw)
    cr = c_re[None] * pw_re[1:, :, None, :] - c_im[None] * pw_im[1:, :, None, :]
    ci = -(c_re[None] * pw_im[1:, :, None, :] + c_im[None] * pw_re[1:, :, None, :])
    cc = jnp.concatenate([cr, ci], axis=3)
    cc = cc.reshape(tc, nt, gt, C_GROUP, sw).transpose(1, 2, 4, 0, 3).reshape(nt, gt * sw, tc * C_GROUP)
    r, i = pw_re[tc], pw_im[tc]
    rows = []
    for _ in range(max(n_log, 1)):
        rows += [jnp.concatenate([r, r], -1), jnp.concatenate([-i, i], -1)]
        r, i = r * r - i * i, 2.0 * r * i
    apow = jnp.stack(rows, axis=1)
    apow = apow.reshape(nt, gt, -1, sw).transpose(0, 2, 1, 3).reshape(nt, -1, gt * sw)
    return kc.astype(BF16), bc.astype(BF16), cc.astype(BF16), apow


def _s5(proj2d, ops, x0_re, x0_im, *, l):
    kc, bc, cc, apow = ops
    bsz = x0_re.shape[0]
    nt = C_GROUPS // S5_GROUP_BLOCK
    x0 = jnp.concatenate([x0_re, x0_im], axis=-1).reshape(bsz, nt, 1, -1).transpose(1, 0, 2, 3)
    bb = bsz if bsz * l <= 4096 else 1
    y, xf = _s5_chunks(proj2d, kc, bc, cc, apow, x0, l=l, bb=bb)
    xf = xf.transpose(1, 0, 2, 3).reshape(bsz, C_GROUPS, 2 * C_STATE)
    return y, xf[..., :C_STATE], xf[..., C_STATE:]


AB_SRC_GA = AB_QKV
AB_SRC_QKV = AB_SRC_GA + A_GATE_RANK
AB_SRC_TAIL = AB_SRC_QKV + B_QKV + B_VW
IN_AB = AB_SRC_TAIL + 2 * B_HEADS
IN_CD = CD_SMALL + 2 * D_HEADS
WPREP_ROWS = 256


def _prep_w_ab_kernel(w_ref, o_ref):
    rows = w_ref.shape[0]
    n_out = o_ref.shape[1]
    o_ref[:, 0:AB_QKV] = w_ref[:, 0:AB_QKV].astype(BF16)
    o_ref[:, AB_QKV:AB_SMALL] = w_ref[:, AB_SRC_QKV:AB_SRC_TAIL].astype(BF16)
    small = jnp.concatenate([w_ref[:, AB_SRC_GA:AB_SRC_QKV], w_ref[:, AB_SRC_TAIL:IN_AB],
                             jnp.zeros((rows, LANES - A_GATE_RANK - 2 * B_HEADS), F32)], axis=1)
    o_ref[:, AB_SMALL:AB_SMALL + LANES] = small.astype(BF16)
    o_ref[:, AB_SMALL + LANES:n_out] = jnp.zeros((rows, n_out - AB_SMALL - LANES), BF16)


def _prep_w_cd_kernel(w_ref, o_ref):
    rows = w_ref.shape[0]
    n_out = o_ref.shape[1]
    o_ref[:, 0:CD_SMALL] = w_ref[:, 0:CD_SMALL].astype(BF16)
    small = jnp.concatenate([w_ref[:, CD_SMALL:IN_CD], jnp.zeros((rows, LANES - 2 * D_HEADS), F32)], axis=1)
    o_ref[:, CD_SMALL:CD_SMALL + LANES] = small.astype(BF16)
    o_ref[:, CD_SMALL + LANES:n_out] = jnp.zeros((rows, n_out - CD_SMALL - LANES), BF16)


def _prep_w(kernel, w, n_out, name):
    d, n_in = w.shape
    rows = WPREP_ROWS
    assert d % rows == 0
    return pl.pallas_call(
        kernel,
        out_shape=jax.ShapeDtypeStruct((d, n_out), BF16),
        grid=(d // rows,),
        in_specs=[pl.BlockSpec((rows, n_in), lambda i: (i, 0))],
        out_specs=pl.BlockSpec((rows, n_out), lambda i: (i, 0)),
        compiler_params=pltpu.CompilerParams(
            dimension_semantics=("parallel",),
            vmem_limit_bytes=_vmem_limit(rows * n_in * 4 + rows * n_out * 2, rows * n_out * 4)),
        name=name,
    )(w)


def _lane_row(vals, lane0):
    return jnp.zeros((1, LANES), F32).at[0, lane0:lane0 + vals.shape[0]].set(vals.astype(F32))


def _prepare_weights(norm_g, final_norm_g, w_in_ab, a_gate_w, a_gate_b, a_norm_g, b_conv_w, b_a_log,
                     b_dt_bias, b_norm_g, w_out_ab, w_in_cd, c_lam_re, c_lam_im, c_log_dt, c_b_re,
                     c_b_im, c_c_re, c_c_im, c_d, c_glu_w, c_glu_b, d_i_bias, d_f_bias, d_norm_g,
                     w_out_cd, n_log):
    assert w_in_ab.shape[1] == IN_AB and w_in_cd.shape[1] == IN_CD
    w_ab = _prep_w(_prep_w_ab_kernel, w_in_ab.astype(F32), _round_up(AB_SMALL + LANES, PROJ_TN), "prep_w_in_ab")
    w_cd = _prep_w(_prep_w_cd_kernel, w_in_cd.astype(F32), _round_up(CD_SMALL + LANES, PROJ_TN), "prep_w_in_cd")
    gate_w = jnp.zeros((LANES, A_KW), F32).at[AB_GA_LANE:AB_GA_LANE + A_GATE_RANK].set(
        a_gate_w.astype(F32)).astype(BF16)
    return dict(
        norm_g=norm_g.astype(F32), final_g=final_norm_g.astype(F32)[None, :],
        w_ab=w_ab, w_cd=w_cd, gate_w=gate_w, gate_b=a_gate_b.astype(F32)[None, :],
        a_norm_g=a_norm_g.astype(F32)[None, :], conv_w=b_conv_w.astype(F32),
        alog=_lane_row(b_a_log, AB_APRE_LANE), dtb=_lane_row(b_dt_bias, AB_APRE_LANE),
        b_norm_g=b_norm_g.astype(F32)[None, :],
        w_out_a=w_out_ab[:A_VW].astype(BF16), w_out_b=w_out_ab[A_VW:].astype(BF16),
        s5_ops=_s5_operators(c_lam_re.astype(F32), c_lam_im.astype(F32), c_log_dt, c_b_re.astype(F32),
                             c_b_im.astype(F32), c_c_re.astype(F32), c_c_im.astype(F32), n_log),
        c_d=c_d.astype(F32).reshape(1, C_W), glu_w=c_glu_w.astype(BF16),
        glu_b=c_glu_b.astype(F32)[None, :],
        ib=_lane_row(d_i_bias, CD_I_LANE), fb=_lane_row(d_f_bias, CD_F_LANE),
        d_norm_g=d_norm_g.astype(F32)[None, :],
        w_out_c=w_out_cd[:C_W].astype(BF16), w_out_d=w_out_cd[C_W:].astype(BF16),
    )


def _trunk(x, conv_prev, s_gla0, s_gdn0, s5_re0, s5_im0, mc0, mn0, mm0, w):
    bsz, l, d = x.shape
    c = min(CHUNK, l)
    tb = min(l, 8 * c)
    assert l % tb == 0 and l % S5_CHUNK == 0
    x2d = x.reshape(bsz * l, d)

    proj = _norm_matmul(x2d, w['norm_g'][0:1], w['w_ab'])
    proj3 = proj.reshape(bsz, l, proj.shape[1])
    o_a, s_gla = _gla(proj3, w['gate_w'], w['gate_b'], w['a_norm_g'], s_gla0.astype(F32), c=c, tb=tb)
    o_b, s_gdn = _gdn(proj3, conv_prev.astype(F32), w['conv_w'], w['alog'], w['dtb'], w['b_norm_g'],
                      s_gdn0.astype(F32), c=c, tb=tb)
    conv_new = proj3[:, l - (B_CONV - 1):, AB_QKV:AB_QKV + B_QKV]
    h1 = _out_proj(o_a.reshape(bsz * l, A_VW), o_b.reshape(bsz * l, B_VW), w['w_out_a'], w['w_out_b'], x2d)

    proj = _norm_matmul(h1, w['norm_g'][1:2], w['w_cd'])
    proj3 = proj.reshape(bsz, l, proj.shape[1])
    y, s5_re, s5_im = _s5(proj, w['s5_ops'], s5_re0.astype(F32), s5_im0.astype(F32), l=l)
    o_c = _s5_glu(y, proj, w['c_d'], w['glu_w'], w['glu_b'])
    o_d, mc, mn, mm = _mlstm(proj3, w['ib'], w['fb'], w['d_norm_g'], mc0.astype(F32),
                             mn0.astype(F32), mm0.astype(F32)[:, None, :], c=c, tb=tb)
    y_out = _out_proj(o_c, o_d.reshape(bsz * l, D_VW), w['w_out_c'], w['w_out_d'], h1, w['final_g'])
    dt = x.dtype
    return (y_out.reshape(bsz, l, d).astype(dt), conv_new.astype(dt), s_gla.astype(dt), s_gdn.astype(dt),
            s5_re.astype(dt), s5_im.astype(dt), mc.astype(dt), mn.astype(dt), mm[:, 0, :].astype(dt))


def kernel(x_prompt, x_sample, cache_gdn_conv, state_gla, state_gdn, state_s5_re, state_s5_im,
           state_mlstm_c, state_mlstm_n, state_mlstm_m, norm_g, final_norm_g, w_in_ab, a_gate_w,
           a_gate_b, a_norm_g, b_conv_w, b_a_log, b_dt_bias, b_norm_g, w_out_ab, w_in_cd, c_lam_re,
           c_lam_im, c_log_dt, c_b_re, c_b_im, c_c_re, c_c_im, c_d, c_glu_w, c_glu_b, d_i_bias,
           d_f_bias, d_norm_g, w_out_cd):
    n_log = int(math.log2(max(x_prompt.shape[1], x_sample.shape[1]) // S5_CHUNK))
    w = _prepare_weights(norm_g, final_norm_g, w_in_ab, a_gate_w, a_gate_b, a_norm_g, b_conv_w, b_a_log,
                         b_dt_bias, b_norm_g, w_out_ab, w_in_cd, c_lam_re, c_lam_im, c_log_dt, c_b_re,
                         c_b_im, c_c_re, c_c_im, c_d, c_glu_w, c_glu_b, d_i_bias, d_f_bias, d_norm_g,
                         w_out_cd, n_log)
    nb = x_prompt.shape[0]
    zeros = lambda *shape: jnp.zeros(shape, F32)
    p_out = _trunk(x_prompt, zeros(nb, B_CONV - 1, B_QKV), zeros(nb, A_HEADS, A_DK, A_DV),
                   zeros(nb, B_HEADS, B_DK, B_DV), zeros(nb, C_GROUPS, C_STATE), zeros(nb, C_GROUPS, C_STATE),
                   zeros(nb, D_HEADS, D_DK, D_DV), zeros(nb, D_HEADS, D_DK), zeros(nb, D_HEADS), w)
    s_out = _trunk(x_sample, cache_gdn_conv, state_gla, state_gdn, state_s5_re, state_s5_im,
                   state_mlstm_c, state_mlstm_n, state_mlstm_m, w)
    return (p_out[0], s_out[0]) + tuple(p_out[1:]) + tuple(s_out[1:])
```

```python
import functools
import math

import jax
import jax.numpy as jnp
from jax import lax
from jax.experimental import pallas as pl
from jax.experimental.pallas import tpu as pltpu

F32 = jnp.float32
BF16 = jnp.bfloat16
HIGHEST = lax.Precision.HIGHEST

NORM_EPS = 1e-6
CHUNK = 64
A_HEADS, A_DK, A_DV, A_GATE_RANK, A_GATE_TAU = 4, 128, 256, 16, 16.0
B_HEADS, B_DK, B_DV, B_CONV = 8, 128, 128, 4
C_GROUP, C_GROUPS, C_STATE = 16, 64, 64
D_HEADS, D_DK, D_DV = 4, 128, 256
A_KW, A_VW = A_HEADS * A_DK, A_HEADS * A_DV
B_KW, B_VW = B_HEADS * B_DK, B_HEADS * B_DV
B_QKV = 2 * B_KW + B_VW
C_W = C_GROUPS * C_GROUP
D_KW, D_VW = D_HEADS * D_DK, D_HEADS * D_DV

LANES = 128
SUBLANES = 8
VMEM_BYTES_V7X = 64 * 1024 * 1024

AB_Q, AB_K, AB_V, AB_Z = 0, A_KW, 2 * A_KW, 2 * A_KW + A_VW
AB_QKV = AB_Z + A_VW
AB_ZB = AB_QKV + B_QKV
AB_SMALL = AB_ZB + B_VW
AB_GA_LANE, AB_BETA_LANE, AB_APRE_LANE = 0, A_GATE_RANK, A_GATE_RANK + B_HEADS
CD_U, CD_Z = 0, C_W
CD_Q = 2 * C_W
CD_K = CD_Q + D_KW
CD_V = CD_K + D_KW
CD_O = CD_V + D_VW
CD_ZD = CD_O + D_VW
CD_SMALL = CD_ZD + D_VW
CD_I_LANE, CD_F_LANE = 0, D_HEADS

PROJ_TN = 1280
S5_CHUNK = 16
S5_GROUP_BLOCK = 8
CONV_SLAB = 512


def _round_up(x, m):
    return (x + m - 1) // m * m


def _vmem_limit(block_bytes, scratch_bytes=0):
    est = 2 * block_bytes + scratch_bytes
    return int(min(max(2 * est, 32 * 1024 * 1024), VMEM_BYTES_V7X - 8 * 1024 * 1024))


def _mm(a, b):
    return jnp.dot(a, b, preferred_element_type=F32)


def _dot(a, b):
    return _mm(a.astype(BF16), b.astype(BF16))


def _dot_nt(a, b):
    return lax.dot_general(a.astype(BF16), b.astype(BF16), (((1,), (1,)), ((), ())),
                           preferred_element_type=F32)


def _dot_tn(a, b):
    return lax.dot_general(a.astype(BF16), b.astype(BF16), (((0,), (0,)), ((), ())),
                           preferred_element_type=F32)


def _split2(x):
    hi = x.astype(BF16)
    return hi, (x - hi.astype(F32)).astype(BF16)


def _split3(x):
    hi = x.astype(BF16)
    r = x - hi.astype(F32)
    mid = r.astype(BF16)
    return hi, mid, (r - mid.astype(F32)).astype(BF16)


def _dot3(a, b):
    ah, al = _split2(a)
    bh, bl = _split2(b)
    return _mm(ah, bh) + _mm(ah, bl) + _mm(al, bh)


def _cumsum_rows(tri_bf16, x):
    hi, mid, lo = _split3(x)
    return _mm(tri_bf16, hi) + _mm(tri_bf16, mid) + _mm(tri_bf16, lo)


def _select_rows(sel_bf16, x):
    nt = lambda b: lax.dot_general(sel_bf16, b, (((1,), (1,)), ((), ())), preferred_element_type=F32)
    hi, mid, lo = _split3(x)
    return nt(hi) + nt(mid) + nt(lo)


def _lane_selector(lane0):
    r = lax.broadcasted_iota(jnp.int32, (SUBLANES, LANES), 0)
    l = lax.broadcasted_iota(jnp.int32, (SUBLANES, LANES), 1)
    return jnp.where(l == r + lane0, 1.0, 0.0).astype(BF16)


def _causal_masks(c):
    row = lax.broadcasted_iota(jnp.int32, (c, c), 0)
    col = lax.broadcasted_iota(jnp.int32, (c, c), 1)
    return row >= col, row > col


def _norm_matmul_kernel(x_ref, g_ref, w_ref, o_ref, xn_ref):
    @pl.when(pl.program_id(1) == 0)
    def _():
        x = x_ref[...]
        y = x * lax.rsqrt(jnp.mean(x * x, axis=-1, keepdims=True) + NORM_EPS) * g_ref[...]
        xn_ref[...] = y.astype(BF16)

    o_ref[...] = jnp.dot(xn_ref[...], w_ref[...], preferred_element_type=F32)


def _norm_matmul(x2d, g_row, w_bf16):
    m, d = x2d.shape
    n = w_bf16.shape[1]
    tm = min(m, 1024)
    tn = PROJ_TN
    assert m % tm == 0 and n % tn == 0
    blocks = tm * d * 4 + d * tn * 2 + tm * tn * 4
    return pl.pallas_call(
        _norm_matmul_kernel,
        out_shape=jax.ShapeDtypeStruct((m, n), F32),
        grid=(m // tm, n // tn),
        in_specs=[pl.BlockSpec((tm, d), lambda i, j: (i, 0)),
                  pl.BlockSpec((1, d), lambda i, j: (0, 0)),
                  pl.BlockSpec((d, tn), lambda i, j: (0, j))],
        out_specs=pl.BlockSpec((tm, tn), lambda i, j: (i, j)),
        scratch_shapes=[pltpu.VMEM((tm, d), BF16)],
        compiler_params=pltpu.CompilerParams(
            dimension_semantics=("parallel", "arbitrary"),
            vmem_limit_bytes=_vmem_limit(blocks, tm * d * 2)),
        name="norm_in_proj",
    )(x2d, g_row, w_bf16)


def _out_proj_kernel(a_ref, b_ref, wa_ref, wb_ref, h_ref, o_ref):
    out = (jnp.dot(a_ref[...], wa_ref[...], preferred_element_type=F32)
           + jnp.dot(b_ref[...], wb_ref[...], preferred_element_type=F32))
    o_ref[...] = h_ref[...] + out


def _out_proj_norm_kernel(a_ref, b_ref, wa_ref, wb_ref, h_ref, g_ref, o_ref):
    out = (jnp.dot(a_ref[...], wa_ref[...], preferred_element_type=F32)
           + jnp.dot(b_ref[...], wb_ref[...], preferred_element_type=F32))
    h = h_ref[...] + out
    o_ref[...] = h * lax.rsqrt(jnp.mean(h * h, axis=-1, keepdims=True) + NORM_EPS) * g_ref[...]


def _out_proj(mix_a, mix_b, w_a, w_b, h2d, final_g_row=None):
    m, d = h2d.shape
    ka, kb = mix_a.shape[1], mix_b.shape[1]
    tm = min(m, 512)
    assert m % tm == 0
    in_specs = [pl.BlockSpec((tm, ka), lambda i: (i, 0)),
                pl.BlockSpec((tm, kb), lambda i: (i, 0)),
                pl.BlockSpec((ka, d), lambda i: (0, 0)),
                pl.BlockSpec((kb, d), lambda i: (0, 0)),
                pl.BlockSpec((tm, d), lambda i: (i, 0))]
    args = [mix_a, mix_b, w_a, w_b, h2d]
    kernel = _out_proj_kernel
    if final_g_row is not None:
        in_specs.append(pl.BlockSpec((1, d), lambda i: (0, 0)))
        args.append(final_g_row)
        kernel = _out_proj_norm_kernel
    blocks = tm * (ka + kb) * 2 + (ka + kb) * d * 2 + 2 * tm * d * 4
    return pl.pallas_call(
        kernel,
        out_shape=jax.ShapeDtypeStruct((m, d), F32),
        grid=(m // tm,),
        in_specs=in_specs,
        out_specs=pl.BlockSpec((tm, d), lambda i: (i, 0)),
        compiler_params=pltpu.CompilerParams(
            dimension_semantics=("parallel",),
            vmem_limit_bytes=_vmem_limit(blocks)),
        name="out_proj_norm" if final_g_row is not None else "out_proj",
    )(*args)


def _gla_kernel(q_ref, k_ref, v_ref, z_ref, sm_ref, gw_ref, gb_ref, ng_ref, s0_ref,
                o_ref, sout_ref, st_ref, *, c, nchunks):
    t = pl.program_id(1)
    last_t = pl.num_programs(1) - 1

    @pl.when(t == 0)
    def _():
        for h in range(A_HEADS):
            st_ref[h] = s0_ref[h].T

    causal, _ = _causal_masks(c)
    tri = causal.astype(BF16)
    gw = gw_ref[...]
    gb = gb_ref[...]

    def body(n, carry):
        sl = pl.ds(pl.multiple_of(n * c, c), c)
        log_alpha = jax.nn.log_sigmoid(_dot(sm_ref[sl, :], gw) + gb) * (1.0 / A_GATE_TAU)
        b_all = _cumsum_rows(tri, log_alpha)
        heads = range(A_HEADS)
        ksl = [slice(h * A_DK, (h + 1) * A_DK) for h in heads]
        vsl = [slice(h * A_DV, (h + 1) * A_DV) for h in heads]
        b = [b_all[:, ksl[h]] for h in heads]
        b_last = [b[h][c - 1:c, :] for h in heads]
        k = [k_ref[sl, ksl[h]] for h in heads]
        v = [v_ref[sl, vsl[h]].astype(BF16) for h in heads]
        q_dec = [(q_ref[sl, ksl[h]] * (A_DK ** -0.5) * jnp.exp(b[h])).astype(BF16) for h in heads]
        k_dec = [(k[h] * jnp.exp(-b[h])).astype(BF16) for h in heads]
        k_w = [(k[h] * jnp.exp(b_last[h] - b[h])).astype(BF16) for h in heads]
        scores = [jnp.where(causal, _dot_nt(q_dec[h], k_dec[h]), 0.0).astype(BF16) for h in heads]
        s_t = [st_ref[h] for h in heads]
        outs = [_mm(scores[h], v[h]) + _dot_nt(q_dec[h], s_t[h]) for h in heads]
        for h in heads:
            st_ref[h] = s_t[h] * jnp.exp(b_last[h]) + _dot_tn(v[h], k_w[h])
        for h in heads:
            o = outs[h]
            o = o * lax.rsqrt(jnp.mean(o * o, axis=-1, keepdims=True) + NORM_EPS) * ng_ref[:, vsl[h]]
            o_ref[sl, vsl[h]] = (o * jax.nn.silu(z_ref[sl, vsl[h]])).astype(o_ref.dtype)
        return carry

    lax.fori_loop(0, nchunks, body, 0)

    @pl.when(t == last_t)
    def _():
        for h in range(A_HEADS):
            sout_ref[h] = st_ref[h].T


def _gla(proj3, gate_w_pad, gate_b_row, norm_g_row, s0, *, c, tb):
    bsz, l, _ = proj3.shape
    nblk = l // tb
    tok = lambda col: (lambda b, t: (b, t, col))
    in_specs = [
        pl.BlockSpec((None, tb, A_KW), tok(AB_Q // A_KW)),
        pl.BlockSpec((None, tb, A_KW), tok(AB_K // A_KW)),
        pl.BlockSpec((None, tb, A_VW), tok(AB_V // A_VW)),
        pl.BlockSpec((None, tb, A_VW), tok(AB_Z // A_VW)),
        pl.BlockSpec((None, tb, LANES), tok(AB_SMALL // LANES)),
        pl.BlockSpec((LANES, A_KW), lambda b, t: (0, 0)),
        pl.BlockSpec((1, A_KW), lambda b, t: (0, 0)),
        pl.BlockSpec((1, A_VW), lambda b, t: (0, 0)),
        pl.BlockSpec((None, A_HEADS, A_DK, A_DV), lambda b, t: (b, 0, 0, 0)),
    ]
    out_specs = [
        pl.BlockSpec((None, tb, A_VW), lambda b, t: (b, t, 0)),
        pl.BlockSpec((None, A_HEADS, A_DK, A_DV), lambda b, t: (b, 0, 0, 0)),
    ]
    blocks = tb * (2 * A_KW + 2 * A_VW + LANES) * 4 + tb * A_VW * 2 + 2 * A_HEADS * A_DK * A_DV * 4
    return pl.pallas_call(
        functools.partial(_gla_kernel, c=c, nchunks=tb // c),
        out_shape=[jax.ShapeDtypeStruct((bsz, l, A_VW), BF16),
                   jax.ShapeDtypeStruct((bsz, A_HEADS, A_DK, A_DV), F32)],
        grid=(bsz, nblk),
        in_specs=in_specs,
        out_specs=out_specs,
        scratch_shapes=[pltpu.VMEM((A_HEADS, A_DV, A_DK), F32)],
        compiler_params=pltpu.CompilerParams(
            dimension_semantics=("parallel", "arbitrary"),
            vmem_limit_bytes=_vmem_limit(blocks, A_HEADS * A_DK * A_DV * 4)),
        name="gla_mixer",
    )(proj3, proj3, proj3, proj3, proj3, gate_w_pad, gate_b_row, norm_g_row, s0)


def _gdn_kernel(x_ref, z_ref, sm_ref, w_ref, cp_ref, alog_ref, dtb_ref, ng_ref, s0_ref,
                o_ref, sout_ref, s_ref, tail_ref, conv_ref, u_ref, wm_ref, qg_ref, kg_ref, qk_ref, gl_ref,
                *, c, nchunks):
    t = pl.program_id(1)
    last_t = pl.num_programs(1) - 1
    tb = c * nchunks
    keep = SUBLANES - (B_CONV - 1)

    @pl.when(t == 0)
    def _():
        s_ref[...] = s0_ref[...]
        tail_ref[0:keep, :] = jnp.zeros((keep, B_QKV), F32)
        tail_ref[keep:SUBLANES, :] = cp_ref[...]

    for j in range(0, B_QKV, CONV_SLAB):
        cs = slice(j, j + CONV_SLAB)
        x = x_ref[:, cs]
        w = w_ref[:, cs]
        ext = jnp.concatenate([tail_ref[:, cs], x], axis=0)
        conv = ext[keep:keep + tb, :] * w[0:1, :]
        conv = conv + ext[keep + 1:keep + 1 + tb, :] * w[1:2, :]
        conv = conv + ext[keep + 2:keep + 2 + tb, :] * w[2:3, :]
        conv = conv + x * w[3:4, :]
        conv_ref[:, cs] = jax.nn.silu(conv)
        tail_ref[:, cs] = x[tb - SUBLANES:tb, :]

    causal, strict = _causal_masks(c)
    tri = causal.astype(BF16)
    eye = jnp.where(causal & jnp.logical_not(strict), 1.0, 0.0).astype(F32)
    sel = _lane_selector(AB_APRE_LANE)
    neg_a_exp = -jnp.exp(alog_ref[...])
    dtb = dtb_ref[...]
    ng = ng_ref[...]
    n_double = int(math.log2(c)) - 1
    heads = range(B_HEADS)

    def wy_factors(n, carry):
        sl = pl.ds(pl.multiple_of(n * c, c), c)
        sm = sm_ref[sl, :]
        g_cum = _cumsum_rows(tri, neg_a_exp * jax.nn.softplus(sm + dtb))
        g_rows = _select_rows(sel, g_cum)
        beta_all = jax.nn.sigmoid(sm)
        gl_ref[pl.ds(n, 1), :] = g_cum[c - 1:c, :]
        pws, rhss = [], []
        for h in heads:
            hs = slice(h * B_DK, (h + 1) * B_DK)
            q = conv_ref[sl, hs]
            k = conv_ref[sl, B_KW + h * B_DK:B_KW + (h + 1) * B_DK]
            v = conv_ref[sl, 2 * B_KW + h * B_DV:2 * B_KW + (h + 1) * B_DV]
            q = q * lax.rsqrt(jnp.sum(q * q, axis=-1, keepdims=True) + NORM_EPS) * (B_DK ** -0.5)
            k = k * lax.rsqrt(jnp.sum(k * k, axis=-1, keepdims=True) + NORM_EPS)
            g_col = g_cum[:, AB_APRE_LANE + h:AB_APRE_LANE + h + 1]
            beta = beta_all[:, AB_BETA_LANE + h:AB_BETA_LANE + h + 1]
            decay = jnp.exp(jnp.where(causal, g_col - g_rows[h:h + 1, :], -jnp.inf))
            e_g = jnp.exp(g_col)
            k_beta = k * beta
            pws.append(-jnp.where(strict, _dot_nt(k_beta, k) * decay, 0.0))
            rhss.append(jnp.concatenate([v * beta, k_beta * e_g], axis=1))
            qk_ref[sl, h * LANES:h * LANES + c] = jnp.where(causal, _dot_nt(q, k) * decay, 0.0).astype(BF16)
            qg_ref[sl, hs] = (q * e_g).astype(BF16)
            kg_ref[sl, hs] = (k * jnp.exp(g_col[c - 1:c, :] - g_col)).astype(BF16)
        invs = [eye + p for p in pws]
        for _ in range(n_double):
            pws = [_dot3(p, p) for p in pws]
            invs = [i + _dot3(i, p) for i, p in zip(invs, pws)]
        for h in heads:
            hs = slice(h * B_DK, (h + 1) * B_DK)
            uw = _dot3(invs[h], rhss[h])
            u_ref[sl, hs] = uw[:, :B_DV]
            wm_ref[sl, hs] = uw[:, B_DV:].astype(BF16)
        return carry

    lax.fori_loop(0, nchunks, wy_factors, 0)

    def recurrence(n, carry):
        sl = pl.ds(pl.multiple_of(n * c, c), c)
        e_last = jnp.exp(gl_ref[pl.ds(n, 1), :])
        hsl = [slice(h * B_DK, (h + 1) * B_DK) for h in heads]
        s_old = [s_ref[h] for h in heads]
        s_bf = [s.astype(BF16) for s in s_old]
        v_new = [(u_ref[sl, hsl[h]] - _mm(wm_ref[sl, hsl[h]], s_bf[h])).astype(BF16) for h in heads]
        outs = [_mm(qg_ref[sl, hsl[h]], s_bf[h]) + _mm(qk_ref[sl, h * LANES:h * LANES + c], v_new[h])
                for h in heads]
        for h in heads:
            s_ref[h] = (e_last[:, AB_APRE_LANE + h:AB_APRE_LANE + h + 1] * s_old[h]
                        + lax.dot_general(kg_ref[sl, hsl[h]], v_new[h], (((0,), (0,)), ((), ())),
                                          preferred_element_type=F32))
        for h in heads:
            o = outs[h]
            o = o * lax.rsqrt(jnp.mean(o * o, axis=-1, keepdims=True) + NORM_EPS) * ng
            o_ref[sl, hsl[h]] = (o * jax.nn.silu(z_ref[sl, hsl[h]])).astype(o_ref.dtype)
        return carry

    lax.fori_loop(0, nchunks, recurrence, 0)

    @pl.when(t == last_t)
    def _():
        sout_ref[...] = s_ref[...]


def _gdn(proj3, conv_prev, conv_w, alog_row, dtb_row, norm_g_row, s0, *, c, tb):
    bsz, l, _ = proj3.shape
    nblk = l // tb
    assert AB_QKV % B_QKV == 0 and AB_ZB % B_VW == 0
    in_specs = [
        pl.BlockSpec((None, tb, B_QKV), lambda b, t: (b, t, AB_QKV // B_QKV)),
        pl.BlockSpec((None, tb, B_VW), lambda b, t: (b, t, AB_ZB // B_VW)),
        pl.BlockSpec((None, tb, LANES), lambda b, t: (b, t, AB_SMALL // LANES)),
        pl.BlockSpec((B_CONV, B_QKV), lambda b, t: (0, 0)),
        pl.BlockSpec((None, B_CONV - 1, B_QKV), lambda b, t: (b, 0, 0)),
        pl.BlockSpec((1, LANES), lambda b, t: (0, 0)),
        pl.BlockSpec((1, LANES), lambda b, t: (0, 0)),
        pl.BlockSpec((1, B_DV), lambda b, t: (0, 0)),
        pl.BlockSpec((None, B_HEADS, B_DK, B_DV), lambda b, t: (b, 0, 0, 0)),
    ]
    out_specs = [
        pl.BlockSpec((None, tb, B_VW), lambda b, t: (b, t, 0)),
        pl.BlockSpec((None, B_HEADS, B_DK, B_DV), lambda b, t: (b, 0, 0, 0)),
    ]
    blocks = tb * (B_QKV + B_VW + LANES) * 4 + tb * B_VW * 2 + 2 * B_HEADS * B_DK * B_DV * 4
    scratch = (B_HEADS * B_DK * B_DV * 4 + SUBLANES * B_QKV * 4 + tb * B_QKV * 4 + tb * B_VW * 4
               + 4 * tb * B_KW * 2 + SUBLANES * LANES * 4)
    return pl.pallas_call(
        functools.partial(_gdn_kernel, c=c, nchunks=tb // c),
        out_shape=[jax.ShapeDtypeStruct((bsz, l, B_VW), BF16),
                   jax.ShapeDtypeStruct((bsz, B_HEADS, B_DK, B_DV), F32)],
        grid=(bsz, nblk),
        in_specs=in_specs,
        out_specs=out_specs,
        scratch_shapes=[pltpu.VMEM((B_HEADS, B_DK, B_DV), F32),
                        pltpu.VMEM((SUBLANES, B_QKV), F32),
                        pltpu.VMEM((tb, B_QKV), F32),
                        pltpu.VMEM((tb, B_VW), F32),
                        pltpu.VMEM((tb, B_KW), BF16),
                        pltpu.VMEM((tb, B_KW), BF16),
                        pltpu.VMEM((tb, B_KW), BF16),
                        pltpu.VMEM((tb, B_HEADS * LANES), BF16),
                        pltpu.VMEM((SUBLANES, LANES), F32)],
        compiler_params=pltpu.CompilerParams(
            dimension_semantics=("parallel", "arbitrary"),
            vmem_limit_bytes=_vmem_limit(blocks, scratch)),
        name="gdn_mixer",
    )(proj3, proj3, proj3, conv_w, conv_prev, alog_row, dtb_row, norm_g_row, s0)


def _mlstm_kernel(q_ref, k_ref, v_ref, og_ref, z_ref, sm_ref, ib_ref, fb_ref, ng_ref,
                  c0_ref, n0_ref, m0_ref, o_ref, cout_ref, nout_ref, mout_ref,
                  c_ref, n_ref, m_ref, *, c, nchunks):
    t = pl.program_id(1)
    last_t = pl.num_programs(1) - 1

    @pl.when(t == 0)
    def _():
        c_ref[...] = c0_ref[...]
        n_ref[...] = n0_ref[...]
        m_ref[...] = m0_ref[...]

    causal, _ = _causal_masks(c)
    tri = causal.astype(BF16)
    sel = _lane_selector(0)
    lane = lax.broadcasted_iota(jnp.int32, (c, LANES), 1)
    ib = ib_ref[...]
    fb = fb_ref[...]

    def body(n, carry):
        sl = pl.ds(pl.multiple_of(n * c, c), c)
        sm = sm_ref[sl, :]
        i_full = sm + ib
        b_full = _cumsum_rows(tri, jax.nn.log_sigmoid(sm + fb))
        rows = _select_rows(sel, jnp.where(lane < CD_F_LANE, i_full, b_full))
        heads = range(D_HEADS)
        ksl = [slice(h * D_DK, (h + 1) * D_DK) for h in heads]
        vsl = [slice(h * D_DV, (h + 1) * D_DV) for h in heads]
        b_col = [b_full[:, CD_F_LANE + h:CD_F_LANE + h + 1] for h in heads]
        i_col = [i_full[:, CD_I_LANE + h:CD_I_LANE + h + 1] for h in heads]
        logw = [jnp.where(causal, b_col[h] - rows[CD_F_LANE + h:CD_F_LANE + h + 1, :]
                          + rows[CD_I_LANE + h:CD_I_LANE + h + 1, :], -jnp.inf) for h in heads]
        m_intra = [jnp.max(logw[h], axis=-1, keepdims=True) for h in heads]
        q = [q_ref[sl, ksl[h]] * (D_DK ** -0.5) for h in heads]
        q_bf = [x.astype(BF16) for x in q]
        k = [k_ref[sl, ksl[h]] for h in heads]
        v = [v_ref[sl, vsl[h]].astype(BF16) for h in heads]
        p = [jnp.exp(logw[h] - m_intra[h]) * _dot_nt(q_bf[h], k[h]) for h in heads]
        h_intra = [_dot(p[h], v[h]) for h in heads]
        n_intra = [jnp.sum(p[h], axis=-1, keepdims=True) for h in heads]
        b_last = [b_col[h][c - 1:c, :] for h in heads]
        m_chunk = [m_intra[h][c - 1:c, :] for h in heads]
        k_w = [k[h] * jnp.exp(b_last[h] - b_col[h] + i_col[h] - m_chunk[h]) for h in heads]
        kv = [_dot_tn(k_w[h], v[h]) for h in heads]
        k_sum = [jnp.sum(k_w[h], axis=0, keepdims=True) for h in heads]
        c_s = [c_ref[h] for h in heads]
        n_s = [n_ref[h:h + 1, :] for h in heads]
        m_s = [m_ref[:, h:h + 1] for h in heads]
        qc = [_dot(q_bf[h], c_s[h]) for h in heads]
        for h in heads:
            m_new = jnp.maximum(b_last[h] + m_s[h], m_chunk[h])
            w_old = jnp.exp(b_last[h] + m_s[h] - m_new)
            w_new = jnp.exp(m_chunk[h] - m_new)
            c_ref[h] = w_old * c_s[h] + w_new * kv[h]
            n_ref[h:h + 1, :] = w_old * n_s[h] + w_new * k_sum[h]
            m_ref[:, h:h + 1] = m_new
        for h in heads:
            a = b_col[h] + m_s[h]
            m_t = jnp.maximum(a, m_intra[h])
            w_a = jnp.exp(a - m_t)
            w_i = jnp.exp(m_intra[h] - m_t)
            num = w_a * qc[h] + w_i * h_intra[h]
            den = w_a * jnp.sum(q[h] * n_s[h], axis=-1, keepdims=True) + w_i * n_intra[h]
            hh = num / jnp.maximum(jnp.abs(den), jnp.exp(-m_t))
            hd = jax.nn.sigmoid(og_ref[sl, vsl[h]]) * hh
            oc = hd - jnp.mean(hd, axis=-1, keepdims=True)
            o = oc * lax.rsqrt(jnp.mean(oc * oc, axis=-1, keepdims=True) + NORM_EPS) * ng_ref[:, vsl[h]]
            o_ref[sl, vsl[h]] = (o * jax.nn.silu(z_ref[sl, vsl[h]])).astype(o_ref.dtype)
        return carry

    lax.fori_loop(0, nchunks, body, 0)

    @pl.when(t == last_t)
    def _():
        cout_ref[...] = c_ref[...]
        nout_ref[...] = n_ref[...]
        mout_ref[...] = m_ref[...]


def _mlstm(proj3, ib_row, fb_row, norm_g_row, c0, n0, m0, *, c, tb):
    bsz, l, _ = proj3.shape
    nblk = l // tb
    tok = lambda col: (lambda b, t: (b, t, col))
    st4 = lambda b, t: (b, 0, 0, 0)
    st3 = lambda b, t: (b, 0, 0)
    in_specs = [
        pl.BlockSpec((None, tb, D_KW), tok(CD_Q // D_KW)),
        pl.BlockSpec((None, tb, D_KW), tok(CD_K // D_KW)),
        pl.BlockSpec((None, tb, D_VW), tok(CD_V // D_VW)),
        pl.BlockSpec((None, tb, D_VW), tok(CD_O // D_VW)),
        pl.BlockSpec((None, tb, D_VW), tok(CD_ZD // D_VW)),
        pl.BlockSpec((None, tb, LANES), tok(CD_SMALL // LANES)),
        pl.BlockSpec((1, LANES), lambda b, t: (0, 0)),
        pl.BlockSpec((1, LANES), lambda b, t: (0, 0)),
        pl.BlockSpec((1, D_VW), lambda b, t: (0, 0)),
        pl.BlockSpec((None, D_HEADS, D_DK, D_DV), st4),
        pl.BlockSpec((None, D_HEADS, D_DK), st3),
        pl.BlockSpec((None, 1, D_HEADS), st3),
    ]
    out_specs = [
        pl.BlockSpec((None, tb, D_VW), lambda b, t: (b, t, 0)),
        pl.BlockSpec((None, D_HEADS, D_DK, D_DV), st4),
        pl.BlockSpec((None, D_HEADS, D_DK), st3),
        pl.BlockSpec((None, 1, D_HEADS), st3),
    ]
    blocks = tb * (2 * D_KW + 3 * D_VW + LANES) * 4 + tb * D_VW * 2 + 2 * D_HEADS * D_DK * D_DV * 4
    return pl.pallas_call(
        functools.partial(_mlstm_kernel, c=c, nchunks=tb // c),
        out_shape=[jax.ShapeDtypeStruct((bsz, l, D_VW), BF16),
                   jax.ShapeDtypeStruct((bsz, D_HEADS, D_DK, D_DV), F32),
                   jax.ShapeDtypeStruct((bsz, D_HEADS, D_DK), F32),
                   jax.ShapeDtypeStruct((bsz, 1, D_HEADS), F32)],
        grid=(bsz, nblk),
        in_specs=in_specs,
        out_specs=out_specs,
        scratch_shapes=[pltpu.VMEM((D_HEADS, D_DK, D_DV), F32),
                        pltpu.VMEM((D_HEADS, D_DK), F32),
                        pltpu.VMEM((1, D_HEADS), F32)],
        compiler_params=pltpu.CompilerParams(
            dimension_semantics=("parallel", "arbitrary"),
            vmem_limit_bytes=_vmem_limit(blocks, D_HEADS * D_DK * D_DV * 4)),
        name="mlstm_mixer",
    )(proj3, proj3, proj3, proj3, proj3, proj3, ib_row, fb_row, norm_g_row, c0, n0, m0)


def _s5_expand_operators(kc_ref, bc_ref, cc_ref, bd_ref, bst_ref, cst_ref):
    tc, gt, sw, cg = S5_CHUNK, S5_GROUP_BLOCK, 2 * C_STATE, C_GROUP
    w_t = tc * LANES
    iota = lambda shape, d: lax.broadcasted_iota(jnp.int32, shape, d)
    row_g = (iota((w_t, LANES), 0) // cg) % gt
    tile16 = jnp.where(iota((cg, LANES), 1) % cg == iota((cg, LANES), 0), 1.0, 0.0).astype(BF16)
    bd_ref[...] = jnp.where(row_g == iota((w_t, LANES), 1) // cg, _mm(kc_ref[...], tile16), 0.0).astype(BF16)
    bc = bc_ref[...].astype(F32)
    for g in range(gt):
        bst_ref[:, g * sw:(g + 1) * sw] = jnp.where(row_g == g, bc, 0.0).astype(BF16)
    src, dst = iota((tc * cg, w_t), 0), iota((tc * cg, w_t), 1)
    spread = jnp.where((src // cg == dst // LANES) & (src % cg == dst % cg), 1.0, 0.0).astype(BF16)
    lane_g = (iota((sw, w_t), 1) // cg) % gt
    for g in range(gt):
        full = _mm(cc_ref[g * sw:(g + 1) * sw, :], spread)
        cst_ref[g * sw:(g + 1) * sw, :] = jnp.where(lane_g == g, full, 0.0).astype(BF16)


def _s5_kernel(u_ref, kc_ref, bc_ref, cc_ref, apow_ref, x0_ref, y_ref, xf_ref, bd_ref, bst_ref, cst_ref,
               *, nc, bb):
    m = nc * bb
    tc = S5_CHUNK
    sw = 2 * C_STATE

    @pl.when(pl.program_id(1) == 0)
    def _():
        _s5_expand_operators(kc_ref, bc_ref, cc_ref, bd_ref, bst_ref, cst_ref)

    row = lax.broadcasted_iota(jnp.int32, (m, sw), 0)
    n_idx = row & (nc - 1)
    n_log = int(math.log2(nc))

    def cmul(a1, a2, x):
        return a1 * x + a2 * pltpu.roll(x, C_STATE, 1)

    lhs = jnp.concatenate([u_ref[pl.ds(tau, m, stride=tc), :].astype(BF16) for tau in range(tc)], axis=1)
    e_all = _mm(lhs, bst_ref[...])
    x_start = []
    for g in range(S5_GROUP_BLOCK):
        gs = slice(g * sw, (g + 1) * sw)
        x0_rows = jnp.zeros((m, sw), F32)
        for b in range(bb):
            x0_rows = jnp.where(row == b * nc, x0_ref[b, :, gs], x0_rows)
        x = e_all[:, gs] + cmul(apow_ref[0:1, gs], apow_ref[1:2, gs], x0_rows)
        for j in range(n_log):
            sh = 1 << j
            shifted = jnp.where(n_idx >= sh, pltpu.roll(x, sh, 0), 0.0)
            x = x + cmul(apow_ref[2 * j:2 * j + 1, gs], apow_ref[2 * j + 1:2 * j + 2, gs], shifted)
        x_start.append(jnp.where(n_idx >= 1, pltpu.roll(x, 1, 0), x0_rows).astype(BF16))
        for b in range(bb):
            xf_ref[b, :, gs] = x[b * nc + nc - 1:b * nc + nc, :]
    y_state = _mm(jnp.concatenate(x_start, axis=1), cst_ref[...])
    for tau in range(tc):
        y = (_mm(lhs[:, :(tau + 1) * LANES], bd_ref[(tc - 1 - tau) * LANES:, :])
             + y_state[:, tau * LANES:(tau + 1) * LANES])
        y_ref[pl.ds(tau, m, stride=tc), :] = y


def _s5_chunks(proj2d, kc, bc, cc, apow, x0, *, l, bb):
    tc = S5_CHUNK
    nc = l // tc
    bsz = x0.shape[1]
    nt = C_GROUPS // S5_GROUP_BLOCK
    rows = bb * l
    sw = 2 * C_STATE
    sw_t = S5_GROUP_BLOCK * sw
    w_t = tc * LANES
    assert CD_U % LANES == 0 and bsz % bb == 0
    blocks = (2 * rows * LANES * 4 + 2 * w_t * LANES * 2 + sw_t * tc * C_GROUP * 2 + apow.shape[1] * sw_t * 4
              + 2 * bb * SUBLANES * sw_t * 4)
    scratch = w_t * LANES * 2 + 2 * w_t * sw_t * 2
    temps = bb * nc * (w_t * 2 + w_t * 4 + 3 * sw_t * 4) + 4 * sw * w_t * 4
    return pl.pallas_call(
        functools.partial(_s5_kernel, nc=nc, bb=bb),
        out_shape=[jax.ShapeDtypeStruct((bsz * l, C_W), F32),
                   jax.ShapeDtypeStruct((nt, bsz, 1, sw_t), F32)],
        grid=(nt, bsz // bb),
        in_specs=[pl.BlockSpec((rows, LANES), lambda i, j: (j, CD_U // LANES + i)),
                  pl.BlockSpec((None, w_t, C_GROUP), lambda i, j: (i, 0, 0)),
                  pl.BlockSpec((None, w_t, sw), lambda i, j: (i, 0, 0)),
                  pl.BlockSpec((None, sw_t, tc * C_GROUP), lambda i, j: (i, 0, 0)),
                  pl.BlockSpec((None, apow.shape[1], sw_t), lambda i, j: (i, 0, 0)),
                  pl.BlockSpec((None, bb, 1, sw_t), lambda i, j: (i, j, 0, 0))],
        out_specs=[pl.BlockSpec((rows, LANES), lambda i, j: (j, i)),
                   pl.BlockSpec((None, bb, 1, sw_t), lambda i, j: (i, j, 0, 0))],
        scratch_shapes=[pltpu.VMEM((w_t, LANES), BF16),
                        pltpu.VMEM((w_t, sw_t), BF16),
                        pltpu.VMEM((sw_t, w_t), BF16)],
        compiler_params=pltpu.CompilerParams(
            dimension_semantics=("parallel", "arbitrary"),
            vmem_limit_bytes=_vmem_limit(blocks, scratch + temps)),
        name="s5_chunks",
    )(proj2d, kc, bc, cc, apow, x0)


def _s5_glu_kernel(y_ref, u_ref, z_ref, d_ref, w_ref, b_ref, o_ref):
    y = jax.nn.gelu(y_ref[...] + d_ref[...] * u_ref[...])
    gate = jax.nn.sigmoid(jnp.dot(y.astype(BF16), w_ref[...], preferred_element_type=F32) + b_ref[...])
    o_ref[...] = (y * gate * jax.nn.silu(z_ref[...])).astype(o_ref.dtype)


def _s5_glu(y2d, proj2d, d_row, glu_w, glu_b_row):
    m = y2d.shape[0]
    tm = min(m, 512)
    blocks = 3 * tm * C_W * 4 + C_W * C_W * 2 + tm * C_W * 2
    return pl.pallas_call(
        _s5_glu_kernel,
        out_shape=jax.ShapeDtypeStruct((m, C_W), BF16),
        grid=(m // tm,),
        in_specs=[pl.BlockSpec((tm, C_W), lambda i: (i, 0)),
                  pl.BlockSpec((tm, C_W), lambda i: (i, CD_U // C_W)),
                  pl.BlockSpec((tm, C_W), lambda i: (i, CD_Z // C_W)),
                  pl.BlockSpec((1, C_W), lambda i: (0, 0)),
                  pl.BlockSpec((C_W, C_W), lambda i: (0, 0)),
                  pl.BlockSpec((1, C_W), lambda i: (0, 0))],
        out_specs=pl.BlockSpec((tm, C_W), lambda i: (i, 0)),
        compiler_params=pltpu.CompilerParams(
            dimension_semantics=("parallel",),
            vmem_limit_bytes=_vmem_limit(blocks)),
        name="s5_glu",
    )(y2d, proj2d, proj2d, d_row, glu_w, glu_b_row)


def _s5_operators(lam_re, lam_im, log_dt, b_re, b_im, c_re, c_im, n_log):
    g, p = lam_re.shape
    tc = S5_CHUNK
    dt = jnp.exp(log_dt.astype(F32))[:, None]
    mag = jnp.exp(lam_re * dt)
    ab_re, ab_im = mag * jnp.cos(lam_im * dt), mag * jnp.sin(lam_im * dt)
    den = lam_re * lam_re + lam_im * lam_im
    er = ab_re - 1.0
    zr = (er * lam_re + ab_im * lam_im) / den
    zi = (ab_im * lam_re - er * lam_im) / den
    bb_re = zr[..., None] * b_re - zi[..., None] * b_im
    bb_im = zr[..., None] * b_im + zi[..., None] * b_re
    pr, pi = [jnp.ones_like(ab_re)], [jnp.zeros_like(ab_re)]
    for _ in range(tc):
        pr, pi = pr + [pr[-1] * ab_re - pi[-1] * ab_im], pi + [pr[-1] * ab_im + pi[-1] * ab_re]
    pw_re, pw_im = jnp.stack(pr), jnp.stack(pi)
    abr = pw_re[:tc, :, :, None] * bb_re - pw_im[:tc, :, :, None] * bb_im
    abi = pw_re[:tc, :, :, None] * bb_im + pw_im[:tc, :, :, None] * bb_re
    kern = (jnp.einsum('gjp,dgpi->dgji', c_re, abr, precision=HIGHEST)
            - jnp.einsum('gjp,dgpi->dgji', c_im, abi, precision=HIGHEST))
    gt = S5_GROUP_BLOCK
    nt = g // gt
    sw = 2 * p
    kc = kern[::-1].reshape(tc, nt, gt, C_GROUP, C_GROUP).transpose(1, 0, 2, 4, 3)
    kc = kc.reshape(nt, tc * gt * C_GROUP, C_GROUP)
    ab = jnp.concatenate([abr, abi], axis=2)[::-1]
    bc = ab.reshape(tc, nt, gt, sw, C_GROUP).transpose(1, 0, 2, 4, 3).reshape(nt, tc * gt * C_GROUP, sw)
    cr = c_re[None] * pw_re[1:, :, None, :] - c_im[None] * pw_im[1:, :, None, :]
    ci = -(c_re[None] * pw_im[1:, :, None, :] + c_im[None] * pw_re[1:, :, None, :])
    cc = jnp.concatenate([cr, ci], axis=3)
    cc = cc.reshape(tc, nt, gt, C_GROUP, sw).transpose(1, 2, 4, 0, 3).reshape(nt, gt * sw, tc * C_GROUP)
    r, i = pw_re[tc], pw_im[tc]
    rows = []
    for _ in range(max(n_log, 1)):
        rows += [jnp.concatenate([r, r], -1), jnp.concatenate([-i, i], -1)]
        r, i = r * r - i * i, 2.0 * r * i
    apow = jnp.stack(rows, axis=1)
    apow = apow.reshape(nt, gt, -1, sw).transpose(0, 2, 1, 3).reshape(nt, -1, gt * sw)
    return kc.astype(BF16), bc.astype(BF16), cc.astype(BF16), apow


def _s5(proj2d, ops, x0_re, x0_im, *, l):
    kc, bc, cc, apow = ops
    bsz = x0_re.shape[0]
    nt = C_GROUPS // S5_GROUP_BLOCK
    x0 = jnp.concatenate([x0_re, x0_im], axis=-1).reshape(bsz, nt, 1, -1).transpose(1, 0, 2, 3)
    bb = bsz if bsz * l <= 4096 else 1
    y, xf = _s5_chunks(proj2d, kc, bc, cc, apow, x0, l=l, bb=bb)
    xf = xf.transpose(1, 0, 2, 3).reshape(bsz, C_GROUPS, 2 * C_STATE)
    return y, xf[..., :C_STATE], xf[..., C_STATE:]


AB_SRC_GA = AB_QKV
AB_SRC_QKV = AB_SRC_GA + A_GATE_RANK
AB_SRC_TAIL = AB_SRC_QKV + B_QKV + B_VW
IN_AB = AB_SRC_TAIL + 2 * B_HEADS
IN_CD = CD_SMALL + 2 * D_HEADS
WPREP_AB_TILE = 512
WPREP_CD_TILE = 640


def _prep_w_ab_kernel(wt_ref, ga_ref, o_ref):
    j = pl.program_id(0)
    tile = o_ref.shape[1]
    n_small = A_GATE_RANK + 2 * B_HEADS

    @pl.when(j < AB_SMALL // tile)
    def _():
        o_ref[...] = wt_ref[...].T.astype(BF16)

    @pl.when(j == AB_SMALL // tile)
    def _():
        rows = jnp.concatenate([ga_ref[...], wt_ref[tile - 2 * B_HEADS:tile, :],
                                jnp.zeros((tile - n_small, wt_ref.shape[1]), F32)], axis=0)
        o_ref[...] = rows.T.astype(BF16)

    @pl.when(j > AB_SMALL // tile)
    def _():
        o_ref[...] = jnp.zeros(o_ref.shape, BF16)


def _prep_w_ab(w_t, n_out):
    n_in, d = w_t.shape
    tile = WPREP_AB_TILE
    assert AB_QKV % tile == 0 and AB_SMALL % tile == 0 and n_out % tile == 0 and n_in >= tile

    unit = A_GATE_RANK
    assert tile % unit == 0 and (AB_SRC_QKV - AB_QKV) % unit == 0 and (n_in - tile) % unit == 0

    def src_row(j):
        k = j * (tile // unit)
        k = jnp.where(j < AB_QKV // tile, k,
                      jnp.where(j < AB_SMALL // tile, k + (AB_SRC_QKV - AB_QKV) // unit, (n_in - tile) // unit))
        return unit * k

    return pl.pallas_call(
        _prep_w_ab_kernel,
        out_shape=jax.ShapeDtypeStruct((d, n_out), BF16),
        grid=(n_out // tile,),
        in_specs=[pl.BlockSpec((pl.Element(tile), pl.Element(d)), lambda j: (src_row(j), 0)),
                  pl.BlockSpec((pl.Element(A_GATE_RANK), pl.Element(d)), lambda j: (AB_SRC_GA, 0))],
        out_specs=pl.BlockSpec((d, tile), lambda j: (0, j)),
        compiler_params=pltpu.CompilerParams(
            dimension_semantics=("parallel",),
            vmem_limit_bytes=_vmem_limit(tile * d * 6, 2 * tile * d * 4)),
        name="prep_w_in_ab",
    )(w_t, w_t)


def _prep_w_cd_kernel(wt_ref, o_ref):
    tile = o_ref.shape[1]
    row = pl.program_id(0) * tile + lax.broadcasted_iota(jnp.int32, wt_ref.shape, 0)
    o_ref[...] = jnp.where(row < IN_CD, wt_ref[...], 0.0).T.astype(BF16)


def _prep_w_cd(w_t, n_out):
    n_in, d = w_t.shape
    tile = WPREP_CD_TILE
    assert n_out % tile == 0
    return pl.pallas_call(
        _prep_w_cd_kernel,
        out_shape=jax.ShapeDtypeStruct((d, n_out), BF16),
        grid=(n_out // tile,),
        in_specs=[pl.BlockSpec((tile, d), lambda j: (j, 0))],
        out_specs=pl.BlockSpec((d, tile), lambda j: (0, j)),
        compiler_params=pltpu.CompilerParams(
            dimension_semantics=("parallel",),
            vmem_limit_bytes=_vmem_limit(tile * d * 6, 2 * tile * d * 4)),
        name="prep_w_in_cd",
    )(w_t)


def _lane_row(vals, lane0):
    return jnp.zeros((1, LANES), F32).at[0, lane0:lane0 + vals.shape[0]].set(vals.astype(F32))


def _prepare_weights(norm_g, final_norm_g, w_in_ab, a_gate_w, a_gate_b, a_norm_g, b_conv_w, b_a_log,
                     b_dt_bias, b_norm_g, w_out_ab, w_in_cd, c_lam_re, c_lam_im, c_log_dt, c_b_re,
                     c_b_im, c_c_re, c_c_im, c_d, c_glu_w, c_glu_b, d_i_bias, d_f_bias, d_norm_g,
                     w_out_cd, n_log):
    assert w_in_ab.shape[1] == IN_AB and w_in_cd.shape[1] == IN_CD
    w_ab = _prep_w_ab(w_in_ab.astype(F32).T, _round_up(AB_SMALL + LANES, PROJ_TN))
    w_cd = _prep_w_cd(w_in_cd.astype(F32).T, _round_up(CD_SMALL + LANES, PROJ_TN))
    gate_w = jnp.zeros((LANES, A_KW), F32).at[AB_GA_LANE:AB_GA_LANE + A_GATE_RANK].set(
        a_gate_w.astype(F32)).astype(BF16)
    return dict(
        norm_g=norm_g.astype(F32), final_g=final_norm_g.astype(F32)[None, :],
        w_ab=w_ab, w_cd=w_cd, gate_w=gate_w, gate_b=a_gate_b.astype(F32)[None, :],
        a_norm_g=a_norm_g.astype(F32)[None, :], conv_w=b_conv_w.astype(F32),
        alog=_lane_row(b_a_log, AB_APRE_LANE), dtb=_lane_row(b_dt_bias, AB_APRE_LANE),
        b_norm_g=b_norm_g.astype(F32)[None, :],
        w_out_a=w_out_ab[:A_VW].astype(BF16), w_out_b=w_out_ab[A_VW:].astype(BF16),
        s5_ops=_s5_operators(c_lam_re.astype(F32), c_lam_im.astype(F32), c_log_dt, c_b_re.astype(F32),
                             c_b_im.astype(F32), c_c_re.astype(F32), c_c_im.astype(F32), n_log),
        c_d=c_d.astype(F32).reshape(1, C_W), glu_w=c_glu_w.astype(BF16),
        glu_b=c_glu_b.astype(F32)[None, :],
        ib=_lane_row(d_i_bias, CD_I_LANE), fb=_lane_row(d_f_bias, CD_F_LANE),
        d_norm_g=d_norm_g.astype(F32)[None, :],
        w_out_c=w_out_cd[:C_W].astype(BF16), w_out_d=w_out_cd[C_W:].astype(BF16),
    )


def _trunk(x, conv_prev, s_gla0, s_gdn0, s5_re0, s5_im0, mc0, mn0, mm0, w):
    bsz, l, d = x.shape
    c = min(CHUNK, l)
    tb = min(l, 8 * c)
    assert l % tb == 0 and l % S5_CHUNK == 0
    x2d = x.reshape(bsz * l, d)

    proj = _norm_matmul(x2d, w['norm_g'][0:1], w['w_ab'])
    proj3 = proj.reshape(bsz, l, proj.shape[1])
    o_a, s_gla = _gla(proj3, w['gate_w'], w['gate_b'], w['a_norm_g'], s_gla0.astype(F32), c=c, tb=tb)
    o_b, s_gdn = _gdn(proj3, conv_prev.astype(F32), w['conv_w'], w['alog'], w['dtb'], w['b_norm_g'],
                      s_gdn0.astype(F32), c=c, tb=tb)
    conv_new = proj3[:, l - (B_CONV - 1):, AB_QKV:AB_QKV + B_QKV]
    h1 = _out_proj(o_a.reshape(bsz * l, A_VW), o_b.reshape(bsz * l, B_VW), w['w_out_a'], w['w_out_b'], x2d)

    proj = _norm_matmul(h1, w['norm_g'][1:2], w['w_cd'])
    proj3 = proj.reshape(bsz, l, proj.shape[1])
    y, s5_re, s5_im = _s5(proj, w['s5_ops'], s5_re0.astype(F32), s5_im0.astype(F32), l=l)
    o_c = _s5_glu(y, proj, w['c_d'], w['glu_w'], w['glu_b'])
    o_d, mc, mn, mm = _mlstm(proj3, w['ib'], w['fb'], w['d_norm_g'], mc0.astype(F32),
                             mn0.astype(F32), mm0.astype(F32)[:, None, :], c=c, tb=tb)
    y_out = _out_proj(o_c, o_d.reshape(bsz * l, D_VW), w['w_out_c'], w['w_out_d'], h1, w['final_g'])
    dt = x.dtype
    return (y_out.reshape(bsz, l, d).astype(dt), conv_new.astype(dt), s_gla.astype(dt), s_gdn.astype(dt),
            s5_re.astype(dt), s5_im.astype(dt), mc.astype(dt), mn.astype(dt), mm[:, 0, :].astype(dt))


def kernel(x_prompt, x_sample, cache_gdn_conv, state_gla, state_gdn, state_s5_re, state_s5_im,
           state_mlstm_c, state_mlstm_n, state_mlstm_m, norm_g, final_norm_g, w_in_ab, a_gate_w,
           a_gate_b, a_norm_g, b_conv_w, b_a_log, b_dt_bias, b_norm_g, w_out_ab, w_in_cd, c_lam_re,
           c_lam_im, c_log_dt, c_b_re, c_b_im, c_c_re, c_c_im, c_d, c_glu_w, c_glu_b, d_i_bias,
           d_f_bias, d_norm_g, w_out_cd):
    n_log = int(math.log2(max(x_prompt.shape[1], x_sample.shape[1]) // S5_CHUNK))
    w = _prepare_weights(norm_g, final_norm_g, w_in_ab, a_gate_w, a_gate_b, a_norm_g, b_conv_w, b_a_log,
                         b_dt_bias, b_norm_g, w_out_ab, w_in_cd, c_lam_re, c_lam_im, c_log_dt, c_b_re,
                         c_b_im, c_c_re, c_c_im, c_d, c_glu_w, c_glu_b, d_i_bias, d_f_bias, d_norm_g,
                         w_out_cd, n_log)
    nb = x_prompt.shape[0]
    zeros = lambda *shape: jnp.zeros(shape, F32)
    p_out = _trunk(x_prompt, zeros(nb, B_CONV - 1, B_QKV), zeros(nb, A_HEADS, A_DK, A_DV),
                   zeros(nb, B_HEADS, B_DK, B_DV), zeros(nb, C_GROUPS, C_STATE), zeros(nb, C_GROUPS, C_STATE),
                   zeros(nb, D_HEADS, D_DK, D_DV), zeros(nb, D_HEADS, D_DK), zeros(nb, D_HEADS), w)
    s_out = _trunk(x_sample, cache_gdn_conv, state_gla, state_gdn, state_s5_re, state_s5_im,
                   state_mlstm_c, state_mlstm_n, state_mlstm_m, w)
    return (p_out[0], s_out[0]) + tuple(p_out[1:]) + tuple(s_out[1:])
```

```python
import functools
import math

import jax
import jax.numpy as jnp
from jax import lax
from jax.experimental import pallas as pl
from jax.experimental.pallas import tpu as pltpu

F32 = jnp.float32
BF16 = jnp.bfloat16
HIGHEST = lax.Precision.HIGHEST

NORM_EPS = 1e-6
CHUNK = 64
A_HEADS, A_DK, A_DV, A_GATE_RANK, A_GATE_TAU = 4, 128, 256, 16, 16.0
B_HEADS, B_DK, B_DV, B_CONV = 8, 128, 128, 4
C_GROUP, C_GROUPS, C_STATE = 16, 64, 64
D_HEADS, D_DK, D_DV = 4, 128, 256
A_KW, A_VW = A_HEADS * A_DK, A_HEADS * A_DV
B_KW, B_VW = B_HEADS * B_DK, B_HEADS * B_DV
B_QKV = 2 * B_KW + B_VW
C_W = C_GROUPS * C_GROUP
D_KW, D_VW = D_HEADS * D_DK, D_HEADS * D_DV

LANES = 128
SUBLANES = 8
VMEM_BYTES_V7X = 64 * 1024 * 1024

AB_Q, AB_K, AB_V, AB_Z = 0, A_KW, 2 * A_KW, 2 * A_KW + A_VW
AB_QKV = AB_Z + A_VW
AB_ZB = AB_QKV + B_QKV
AB_SMALL = AB_ZB + B_VW
AB_GA_LANE, AB_BETA_LANE, AB_APRE_LANE = 0, A_GATE_RANK, A_GATE_RANK + B_HEADS
CD_U, CD_Z = 0, C_W
CD_Q = 2 * C_W
CD_K = CD_Q + D_KW
CD_V = CD_K + D_KW
CD_O = CD_V + D_VW
CD_ZD = CD_O + D_VW
CD_SMALL = CD_ZD + D_VW
CD_I_LANE, CD_F_LANE = 0, D_HEADS

PROJ_TN = 1280
S5_CHUNK = 16
S5_GROUP_BLOCK = 8
MIXER_STREAMS = 2
GDN_WY_CHUNKS = 4


def _round_up(x, m):
    return (x + m - 1) // m * m


def _vmem_limit(block_bytes, scratch_bytes=0):
    est = 2 * block_bytes + scratch_bytes
    return int(min(max(2 * est, 32 * 1024 * 1024), VMEM_BYTES_V7X - 8 * 1024 * 1024))


def _mm(a, b):
    return jnp.dot(a, b, preferred_element_type=F32)


def _dot(a, b):
    return _mm(a.astype(BF16), b.astype(BF16))


def _dot_nt(a, b):
    return lax.dot_general(a.astype(BF16), b.astype(BF16), (((1,), (1,)), ((), ())),
                           preferred_element_type=F32)


def _dot_tn(a, b):
    return lax.dot_general(a.astype(BF16), b.astype(BF16), (((0,), (0,)), ((), ())),
                           preferred_element_type=F32)


def _split2(x):
    hi = x.astype(BF16)
    return hi, (x - hi.astype(F32)).astype(BF16)


def _split3(x):
    hi = x.astype(BF16)
    r = x - hi.astype(F32)
    mid = r.astype(BF16)
    return hi, mid, (r - mid.astype(F32)).astype(BF16)


def _dot3(a, b):
    ah, al = _split2(a)
    bh, bl = _split2(b)
    return _mm(ah, bh) + _mm(ah, bl) + _mm(al, bh)


def _cumsum_rows(tri_bf16, x):
    hi, mid, lo = _split3(x)
    return _mm(tri_bf16, hi) + _mm(tri_bf16, mid) + _mm(tri_bf16, lo)


def _select_rows(sel_bf16, x):
    nt = lambda b: lax.dot_general(sel_bf16, b, (((1,), (1,)), ((), ())), preferred_element_type=F32)
    hi, mid, lo = _split3(x)
    return nt(hi) + nt(mid) + nt(lo)


def _lane_selector(lane0):
    r = lax.broadcasted_iota(jnp.int32, (SUBLANES, LANES), 0)
    l = lax.broadcasted_iota(jnp.int32, (SUBLANES, LANES), 1)
    return jnp.where(l == r + lane0, 1.0, 0.0).astype(BF16)


def _causal_masks(c):
    row = lax.broadcasted_iota(jnp.int32, (c, c), 0)
    col = lax.broadcasted_iota(jnp.int32, (c, c), 1)
    return row >= col, row > col


def _norm_matmul_kernel(x_ref, g_ref, w_ref, o_ref, xn_ref):
    @pl.when(pl.program_id(1) == 0)
    def _():
        x = x_ref[...]
        y = x * lax.rsqrt(jnp.mean(x * x, axis=-1, keepdims=True) + NORM_EPS) * g_ref[...]
        xn_ref[...] = y.astype(BF16)

    o_ref[...] = jnp.dot(xn_ref[...], w_ref[...], preferred_element_type=F32)


def _norm_matmul(x2d, g_row, w_bf16):
    m, d = x2d.shape
    n = w_bf16.shape[1]
    tm = min(m, 1024)
    tn = PROJ_TN
    assert m % tm == 0 and n % tn == 0
    blocks = tm * d * 4 + d * tn * 2 + tm * tn * 4
    return pl.pallas_call(
        _norm_matmul_kernel,
        out_shape=jax.ShapeDtypeStruct((m, n), F32),
        grid=(m // tm, n // tn),
        in_specs=[pl.BlockSpec((tm, d), lambda i, j: (i, 0)),
                  pl.BlockSpec((1, d), lambda i, j: (0, 0)),
                  pl.BlockSpec((d, tn), lambda i, j: (0, j))],
        out_specs=pl.BlockSpec((tm, tn), lambda i, j: (i, j)),
        scratch_shapes=[pltpu.VMEM((tm, d), BF16)],
        compiler_params=pltpu.CompilerParams(
            dimension_semantics=("parallel", "arbitrary"),
            vmem_limit_bytes=_vmem_limit(blocks, tm * d * 2)),
        name="norm_in_proj",
    )(x2d, g_row, w_bf16)


def _out_proj_kernel(a_ref, b_ref, wa_ref, wb_ref, h_ref, o_ref):
    out = (jnp.dot(a_ref[...], wa_ref[...], preferred_element_type=F32)
           + jnp.dot(b_ref[...], wb_ref[...], preferred_element_type=F32))
    o_ref[...] = h_ref[...] + out


def _out_proj_norm_kernel(a_ref, b_ref, wa_ref, wb_ref, h_ref, g_ref, o_ref):
    out = (jnp.dot(a_ref[...], wa_ref[...], preferred_element_type=F32)
           + jnp.dot(b_ref[...], wb_ref[...], preferred_element_type=F32))
    h = h_ref[...] + out
    o_ref[...] = h * lax.rsqrt(jnp.mean(h * h, axis=-1, keepdims=True) + NORM_EPS) * g_ref[...]


def _out_proj(mix_a, mix_b, w_a, w_b, h2d, final_g_row=None):
    m, d = h2d.shape
    ka, kb = mix_a.shape[1], mix_b.shape[1]
    tm = min(m, 512)
    assert m % tm == 0
    in_specs = [pl.BlockSpec((tm, ka), lambda i: (i, 0)),
                pl.BlockSpec((tm, kb), lambda i: (i, 0)),
                pl.BlockSpec((ka, d), lambda i: (0, 0)),
                pl.BlockSpec((kb, d), lambda i: (0, 0)),
                pl.BlockSpec((tm, d), lambda i: (i, 0))]
    args = [mix_a, mix_b, w_a, w_b, h2d]
    kernel = _out_proj_kernel
    if final_g_row is not None:
        in_specs.append(pl.BlockSpec((1, d), lambda i: (0, 0)))
        args.append(final_g_row)
        kernel = _out_proj_norm_kernel
    blocks = tm * (ka + kb) * 2 + (ka + kb) * d * 2 + 2 * tm * d * 4
    return pl.pallas_call(
        kernel,
        out_shape=jax.ShapeDtypeStruct((m, d), F32),
        grid=(m // tm,),
        in_specs=in_specs,
        out_specs=pl.BlockSpec((tm, d), lambda i: (i, 0)),
        compiler_params=pltpu.CompilerParams(
            dimension_semantics=("parallel",),
            vmem_limit_bytes=_vmem_limit(blocks)),
        name="out_proj_norm" if final_g_row is not None else "out_proj",
    )(*args)


def _gla_kernel(q_ref, k_ref, v_ref, z_ref, sm_ref, gw_ref, gb_ref, ng_ref, s0_ref,
                o_ref, sout_ref, st_ref, *, c, nchunks):
    t = pl.program_id(1)
    last_t = pl.num_programs(1) - 1
    bb = q_ref.shape[0]
    chains = [(bi, h) for bi in range(bb) for h in range(A_HEADS)]
    ksl = [slice(h * A_DK, (h + 1) * A_DK) for _, h in chains]
    vsl = [slice(h * A_DV, (h + 1) * A_DV) for _, h in chains]
    idx = range(len(chains))

    @pl.when(t == 0)
    def _():
        for bi, h in chains:
            st_ref[bi, h] = s0_ref[bi, h].T

    causal, _ = _causal_masks(c)
    tri = causal.astype(BF16)
    gw = gw_ref[...]
    gb = gb_ref[...]

    def body(n, carry):
        sl = pl.ds(pl.multiple_of(n * c, c), c)
        b_all = [_cumsum_rows(tri, jax.nn.log_sigmoid(_dot(sm_ref[bi, sl, :], gw) + gb) * (1.0 / A_GATE_TAU))
                 for bi in range(bb)]
        b = [b_all[chains[i][0]][:, ksl[i]] for i in idx]
        b_last = [b[i][c - 1:c, :] for i in idx]
        k = [k_ref[chains[i][0], sl, ksl[i]] for i in idx]
        v = [v_ref[chains[i][0], sl, vsl[i]].astype(BF16) for i in idx]
        q_dec = [(q_ref[chains[i][0], sl, ksl[i]] * (A_DK ** -0.5) * jnp.exp(b[i])).astype(BF16) for i in idx]
        k_dec = [(k[i] * jnp.exp(-b[i])).astype(BF16) for i in idx]
        k_w = [(k[i] * jnp.exp(b_last[i] - b[i])).astype(BF16) for i in idx]
        scores = [jnp.where(causal, _dot_nt(q_dec[i], k_dec[i]), 0.0).astype(BF16) for i in idx]
        s_t = [st_ref[bi, h] for bi, h in chains]
        outs = [_mm(scores[i], v[i]) + _dot_nt(q_dec[i], s_t[i]) for i in idx]
        for i, (bi, h) in enumerate(chains):
            st_ref[bi, h] = s_t[i] * jnp.exp(b_last[i]) + _dot_tn(v[i], k_w[i])
        for i, (bi, h) in enumerate(chains):
            o = outs[i]
            o = o * lax.rsqrt(jnp.mean(o * o, axis=-1, keepdims=True) + NORM_EPS) * ng_ref[:, vsl[i]]
            o_ref[bi, sl, vsl[i]] = (o * jax.nn.silu(z_ref[bi, sl, vsl[i]])).astype(o_ref.dtype)
        return carry

    lax.fori_loop(0, nchunks, body, 0)

    @pl.when(t == last_t)
    def _():
        for bi, h in chains:
            sout_ref[bi, h] = st_ref[bi, h].T


def _gla(proj3, gate_w_pad, gate_b_row, norm_g_row, s0, *, c, tb, bb):
    bsz, l, _ = proj3.shape
    nblk = l // tb
    assert bsz % bb == 0
    tok = lambda col: (lambda b, t: (b, t, col))
    in_specs = [
        pl.BlockSpec((bb, tb, A_KW), tok(AB_Q // A_KW)),
        pl.BlockSpec((bb, tb, A_KW), tok(AB_K // A_KW)),
        pl.BlockSpec((bb, tb, A_VW), tok(AB_V // A_VW)),
        pl.BlockSpec((bb, tb, A_VW), tok(AB_Z // A_VW)),
        pl.BlockSpec((bb, tb, LANES), tok(AB_SMALL // LANES)),
        pl.BlockSpec((LANES, A_KW), lambda b, t: (0, 0)),
        pl.BlockSpec((1, A_KW), lambda b, t: (0, 0)),
        pl.BlockSpec((1, A_VW), lambda b, t: (0, 0)),
        pl.BlockSpec((bb, A_HEADS, A_DK, A_DV), lambda b, t: (b, 0, 0, 0)),
    ]
    out_specs = [
        pl.BlockSpec((bb, tb, A_VW), lambda b, t: (b, t, 0)),
        pl.BlockSpec((bb, A_HEADS, A_DK, A_DV), lambda b, t: (b, 0, 0, 0)),
    ]
    state = bb * A_HEADS * A_DK * A_DV * 4
    blocks = bb * tb * (2 * A_KW + 2 * A_VW + LANES) * 4 + bb * tb * A_VW * 2 + 2 * state
    return pl.pallas_call(
        functools.partial(_gla_kernel, c=c, nchunks=tb // c),
        out_shape=[jax.ShapeDtypeStruct((bsz, l, A_VW), BF16),
                   jax.ShapeDtypeStruct((bsz, A_HEADS, A_DK, A_DV), F32)],
        grid=(bsz // bb, nblk),
        in_specs=in_specs,
        out_specs=out_specs,
        scratch_shapes=[pltpu.VMEM((bb, A_HEADS, A_DV, A_DK), F32)],
        compiler_params=pltpu.CompilerParams(
            dimension_semantics=("parallel", "arbitrary"),
            vmem_limit_bytes=_vmem_limit(blocks, state)),
        name="gla_mixer",
    )(proj3, proj3, proj3, proj3, proj3, gate_w_pad, gate_b_row, norm_g_row, s0)


def _gdn_kernel(x_ref, z_ref, sm_ref, w_ref, cp_ref, alog_ref, dtb_ref, ng_ref, s0_ref,
                o_ref, sout_ref, s_ref, tail_ref, u_ref, wm_ref, qg_ref, kg_ref, qk_ref, gl_ref,
                *, c, nchunks):
    t = pl.program_id(1)
    last_t = pl.num_programs(1) - 1
    tb = c * nchunks
    keep = SUBLANES - (B_CONV - 1)

    @pl.when(t == 0)
    def _():
        s_ref[...] = s0_ref[...]
        tail_ref[0:keep, :] = jnp.zeros((keep, B_QKV), F32)
        tail_ref[keep:SUBLANES, :] = cp_ref[...]

    def conv_silu(row0, first, cols):
        x = x_ref[pl.ds(row0, c), cols]
        if first is None:
            prev = x_ref[pl.ds(pl.multiple_of(row0 - SUBLANES, SUBLANES), SUBLANES), cols]
        else:
            before = pl.multiple_of(jnp.maximum(row0 - SUBLANES, 0), SUBLANES)
            prev = jnp.where(first, tail_ref[:, cols], x_ref[pl.ds(before, SUBLANES), cols])
        w = w_ref[:, cols]
        ext = jnp.concatenate([prev, x], axis=0)
        ext1 = pltpu.roll(ext, 1, 0)
        newer = ext * w[3:4, :] + ext1 * w[2:3, :]
        older = ext * w[1:2, :] + ext1 * w[0:1, :]
        conv = newer + pltpu.roll(older, 2, 0)
        return jax.nn.silu(conv[SUBLANES:SUBLANES + c, :])

    causal, strict = _causal_masks(c)
    tri = causal.astype(BF16)
    eye = jnp.where(causal & jnp.logical_not(strict), 1.0, 0.0).astype(F32)
    sel = _lane_selector(AB_APRE_LANE)
    neg_a_exp = -jnp.exp(alog_ref[...])
    dtb = dtb_ref[...]
    ng = ng_ref[...]
    n_double = int(math.log2(c)) - 1
    heads = range(B_HEADS)

    group = GDN_WY_CHUNKS if nchunks % GDN_WY_CHUNKS == 0 else 1

    def wy_factors(n, carry):
        pws, rhss, where = [], [], []
        for ci in range(group):
            row0 = pl.multiple_of((n * group + ci) * c, c)
            first = (n == 0) if ci == 0 else None
            sl = pl.ds(row0, c)
            sm = sm_ref[sl, :]
            g_cum = _cumsum_rows(tri, neg_a_exp * jax.nn.softplus(sm + dtb))
            g_rows = _select_rows(sel, g_cum)
            beta_all = jax.nn.sigmoid(sm)
            gl_ref[pl.ds(n * group + ci, 1), :] = g_cum[c - 1:c, :]
            for h in heads:
                hs = slice(h * B_DK, (h + 1) * B_DK)
                q = conv_silu(row0, first, hs)
                k = conv_silu(row0, first, slice(B_KW + h * B_DK, B_KW + (h + 1) * B_DK))
                v = conv_silu(row0, first, slice(2 * B_KW + h * B_DV, 2 * B_KW + (h + 1) * B_DV))
                q = q * lax.rsqrt(jnp.sum(q * q, axis=-1, keepdims=True) + NORM_EPS) * (B_DK ** -0.5)
                k = k * lax.rsqrt(jnp.sum(k * k, axis=-1, keepdims=True) + NORM_EPS)
                g_col = g_cum[:, AB_APRE_LANE + h:AB_APRE_LANE + h + 1]
                beta = beta_all[:, AB_BETA_LANE + h:AB_BETA_LANE + h + 1]
                decay = jnp.exp(jnp.where(causal, g_col - g_rows[h:h + 1, :], -jnp.inf))
                e_g = jnp.exp(g_col)
                k_beta = k * beta
                pws.append(-jnp.where(strict, _dot_nt(k_beta, k) * decay, 0.0))
                rhss.append(jnp.concatenate([v * beta, k_beta * e_g], axis=1))
                where.append((sl, hs))
                qk_ref[sl, h * LANES:h * LANES + c] = jnp.where(causal, _dot_nt(q, k) * decay, 0.0).astype(BF16)
                qg_ref[sl, hs] = (q * e_g).astype(BF16)
                kg_ref[sl, hs] = (k * jnp.exp(g_col[c - 1:c, :] - g_col)).astype(BF16)
        items = range(len(pws))
        neg_lower = pws
        invs = [eye + p for p in pws]
        for _ in range(n_double):
            pws = [_dot(p, p) for p in pws]
            invs = [i + _dot(i, p) for i, p in zip(invs, pws)]
        invs = [i.astype(BF16) for i in invs]
        sol = [_mm(invs[i], rhss[i].astype(BF16)) for i in items]
        resid = [rhss[i] - sol[i] + _dot3(neg_lower[i], sol[i]) for i in items]
        for i in items:
            sl, hs = where[i]
            uw = sol[i] + _mm(invs[i], resid[i].astype(BF16))
            u_ref[sl, hs] = uw[:, :B_DV]
            wm_ref[sl, hs] = uw[:, B_DV:].astype(BF16)
        return carry

    lax.fori_loop(0, nchunks // group, wy_factors, 0)
    tail_ref[...] = x_ref[tb - SUBLANES:tb, :]

    def recurrence(n, carry):
        sl = pl.ds(pl.multiple_of(n * c, c), c)
        e_last = jnp.exp(gl_ref[pl.ds(n, 1), :])
        hsl = [slice(h * B_DK, (h + 1) * B_DK) for h in heads]
        s_old = [s_ref[h] for h in heads]
        s_bf = [s.astype(BF16) for s in s_old]
        v_new = [(u_ref[sl, hsl[h]] - _mm(wm_ref[sl, hsl[h]], s_bf[h])).astype(BF16) for h in heads]
        outs = [_mm(qg_ref[sl, hsl[h]], s_bf[h]) + _mm(qk_ref[sl, h * LANES:h * LANES + c], v_new[h])
                for h in heads]
        for h in heads:
            s_ref[h] = (e_last[:, AB_APRE_LANE + h:AB_APRE_LANE + h + 1] * s_old[h]
                        + lax.dot_general(kg_ref[sl, hsl[h]], v_new[h], (((0,), (0,)), ((), ())),
                                          preferred_element_type=F32))
        for h in heads:
            o = outs[h]
            o = o * lax.rsqrt(jnp.mean(o * o, axis=-1, keepdims=True) + NORM_EPS) * ng
            o_ref[sl, hsl[h]] = (o * jax.nn.silu(z_ref[sl, hsl[h]])).astype(o_ref.dtype)
        return carry

    lax.fori_loop(0, nchunks, recurrence, 0)

    @pl.when(t == last_t)
    def _():
        sout_ref[...] = s_ref[...]


def _gdn(proj3, conv_prev, conv_w, alog_row, dtb_row, norm_g_row, s0, *, c, tb):
    bsz, l, _ = proj3.shape
    nblk = l // tb
    assert AB_QKV % B_QKV == 0 and AB_ZB % B_VW == 0
    in_specs = [
        pl.BlockSpec((None, tb, B_QKV), lambda b, t: (b, t, AB_QKV // B_QKV)),
        pl.BlockSpec((None, tb, B_VW), lambda b, t: (b, t, AB_ZB // B_VW)),
        pl.BlockSpec((None, tb, LANES), lambda b, t: (b, t, AB_SMALL // LANES)),
        pl.BlockSpec((B_CONV, B_QKV), lambda b, t: (0, 0)),
        pl.BlockSpec((None, B_CONV - 1, B_QKV), lambda b, t: (b, 0, 0)),
        pl.BlockSpec((1, LANES), lambda b, t: (0, 0)),
        pl.BlockSpec((1, LANES), lambda b, t: (0, 0)),
        pl.BlockSpec((1, B_DV), lambda b, t: (0, 0)),
        pl.BlockSpec((None, B_HEADS, B_DK, B_DV), lambda b, t: (b, 0, 0, 0)),
    ]
    out_specs = [
        pl.BlockSpec((None, tb, B_VW), lambda b, t: (b, t, 0)),
        pl.BlockSpec((None, B_HEADS, B_DK, B_DV), lambda b, t: (b, 0, 0, 0)),
    ]
    blocks = tb * (B_QKV + B_VW + LANES) * 4 + tb * B_VW * 2 + 2 * B_HEADS * B_DK * B_DV * 4
    scratch = (B_HEADS * B_DK * B_DV * 4 + SUBLANES * B_QKV * 4 + tb * B_VW * 4
               + 4 * tb * B_KW * 2 + SUBLANES * LANES * 4)
    return pl.pallas_call(
        functools.partial(_gdn_kernel, c=c, nchunks=tb // c),
        out_shape=[jax.ShapeDtypeStruct((bsz, l, B_VW), BF16),
                   jax.ShapeDtypeStruct((bsz, B_HEADS, B_DK, B_DV), F32)],
        grid=(bsz, nblk),
        in_specs=in_specs,
        out_specs=out_specs,
        scratch_shapes=[pltpu.VMEM((B_HEADS, B_DK, B_DV), F32),
                        pltpu.VMEM((SUBLANES, B_QKV), F32),
                        pltpu.VMEM((tb, B_VW), F32),
                        pltpu.VMEM((tb, B_KW), BF16),
                        pltpu.VMEM((tb, B_KW), BF16),
                        pltpu.VMEM((tb, B_KW), BF16),
                        pltpu.VMEM((tb, B_HEADS * LANES), BF16),
                        pltpu.VMEM((SUBLANES, LANES), F32)],
        compiler_params=pltpu.CompilerParams(
            dimension_semantics=("parallel", "arbitrary"),
            vmem_limit_bytes=_vmem_limit(blocks, scratch)),
        name="gdn_mixer",
    )(proj3, proj3, proj3, conv_w, conv_prev, alog_row, dtb_row, norm_g_row, s0)


def _mlstm_kernel(q_ref, k_ref, v_ref, og_ref, z_ref, sm_ref, ib_ref, fb_ref, ng_ref,
                  c0_ref, n0_ref, m0_ref, o_ref, cout_ref, nout_ref, mout_ref,
                  c_ref, n_ref, m_ref, *, c, nchunks):
    t = pl.program_id(1)
    last_t = pl.num_programs(1) - 1
    bb = q_ref.shape[0]
    chains = [(bi, h) for bi in range(bb) for h in range(D_HEADS)]
    ksl = [slice(h * D_DK, (h + 1) * D_DK) for _, h in chains]
    vsl = [slice(h * D_DV, (h + 1) * D_DV) for _, h in chains]
    idx = range(len(chains))

    @pl.when(t == 0)
    def _():
        c_ref[...] = c0_ref[...]
        n_ref[...] = n0_ref[...]
        m_ref[...] = m0_ref[...]

    causal, _ = _causal_masks(c)
    tri = causal.astype(BF16)
    sel = _lane_selector(0)
    lane = lax.broadcasted_iota(jnp.int32, (c, LANES), 1)
    ib = ib_ref[...]
    fb = fb_ref[...]

    def body(n, carry):
        sl = pl.ds(pl.multiple_of(n * c, c), c)
        sm = [sm_ref[bi, sl, :] for bi in range(bb)]
        i_full = [x + ib for x in sm]
        b_full = [_cumsum_rows(tri, jax.nn.log_sigmoid(x + fb)) for x in sm]
        rows = [_select_rows(sel, jnp.where(lane < CD_F_LANE, i_full[bi], b_full[bi]))
                for bi in range(bb)]
        b_col = [b_full[bi][:, CD_F_LANE + h:CD_F_LANE + h + 1] for bi, h in chains]
        i_col = [i_full[bi][:, CD_I_LANE + h:CD_I_LANE + h + 1] for bi, h in chains]
        logw = [jnp.where(causal, b_col[i] - rows[bi][CD_F_LANE + h:CD_F_LANE + h + 1, :]
                          + rows[bi][CD_I_LANE + h:CD_I_LANE + h + 1, :], -jnp.inf)
                for i, (bi, h) in enumerate(chains)]
        m_intra = [jnp.max(logw[i], axis=-1, keepdims=True) for i in idx]
        q = [q_ref[chains[i][0], sl, ksl[i]] * (D_DK ** -0.5) for i in idx]
        q_bf = [x.astype(BF16) for x in q]
        k = [k_ref[chains[i][0], sl, ksl[i]] for i in idx]
        v = [v_ref[chains[i][0], sl, vsl[i]].astype(BF16) for i in idx]
        p = [jnp.exp(logw[i] - m_intra[i]) * _dot_nt(q_bf[i], k[i]) for i in idx]
        h_intra = [_dot(p[i], v[i]) for i in idx]
        n_intra = [jnp.sum(p[i], axis=-1, keepdims=True) for i in idx]
        b_last = [b_col[i][c - 1:c, :] for i in idx]
        m_chunk = [m_intra[i][c - 1:c, :] for i in idx]
        k_w = [k[i] * jnp.exp(b_last[i] - b_col[i] + i_col[i] - m_chunk[i]) for i in idx]
        kv = [_dot_tn(k_w[i], v[i]) for i in idx]
        k_sum = [jnp.sum(k_w[i], axis=0, keepdims=True) for i in idx]
        c_s = [c_ref[bi, h] for bi, h in chains]
        n_s = [n_ref[bi, h:h + 1, :] for bi, h in chains]
        m_s = [m_ref[bi, :, h:h + 1] for bi, h in chains]
        qc = [_dot(q_bf[i], c_s[i]) for i in idx]
        for i, (bi, h) in enumerate(chains):
            m_new = jnp.maximum(b_last[i] + m_s[i], m_chunk[i])
            w_old = jnp.exp(b_last[i] + m_s[i] - m_new)
            w_new = jnp.exp(m_chunk[i] - m_new)
            c_ref[bi, h] = w_old * c_s[i] + w_new * kv[i]
            n_ref[bi, h:h + 1, :] = w_old * n_s[i] + w_new * k_sum[i]
            m_ref[bi, :, h:h + 1] = m_new
        for i, (bi, h) in enumerate(chains):
            a = b_col[i] + m_s[i]
            m_t = jnp.maximum(a, m_intra[i])
            w_a = jnp.exp(a - m_t)
            w_i = jnp.exp(m_intra[i] - m_t)
            num = w_a * qc[i] + w_i * h_intra[i]
            den = w_a * jnp.sum(q[i] * n_s[i], axis=-1, keepdims=True) + w_i * n_intra[i]
            hh = num / jnp.maximum(jnp.abs(den), jnp.exp(-m_t))
            hd = jax.nn.sigmoid(og_ref[bi, sl, vsl[i]]) * hh
            oc = hd - jnp.mean(hd, axis=-1, keepdims=True)
            o = oc * lax.rsqrt(jnp.mean(oc * oc, axis=-1, keepdims=True) + NORM_EPS) * ng_ref[:, vsl[i]]
            o_ref[bi, sl, vsl[i]] = (o * jax.nn.silu(z_ref[bi, sl, vsl[i]])).astype(o_ref.dtype)
        return carry

    lax.fori_loop(0, nchunks, body, 0)

    @pl.when(t == last_t)
    def _():
        cout_ref[...] = c_ref[...]
        nout_ref[...] = n_ref[...]
        mout_ref[...] = m_ref[...]


def _mlstm(proj3, ib_row, fb_row, norm_g_row, c0, n0, m0, *, c, tb, bb):
    bsz, l, _ = proj3.shape
    nblk = l // tb
    assert bsz % bb == 0
    tok = lambda col: (lambda b, t: (b, t, col))
    st4 = lambda b, t: (b, 0, 0, 0)
    st3 = lambda b, t: (b, 0, 0)
    in_specs = [
        pl.BlockSpec((bb, tb, D_KW), tok(CD_Q // D_KW)),
        pl.BlockSpec((bb, tb, D_KW), tok(CD_K // D_KW)),
        pl.BlockSpec((bb, tb, D_VW), tok(CD_V // D_VW)),
        pl.BlockSpec((bb, tb, D_VW), tok(CD_O // D_VW)),
        pl.BlockSpec((bb, tb, D_VW), tok(CD_ZD // D_VW)),
        pl.BlockSpec((bb, tb, LANES), tok(CD_SMALL // LANES)),
        pl.BlockSpec((1, LANES), lambda b, t: (0, 0)),
        pl.BlockSpec((1, LANES), lambda b, t: (0, 0)),
        pl.BlockSpec((1, D_VW), lambda b, t: (0, 0)),
        pl.BlockSpec((bb, D_HEADS, D_DK, D_DV), st4),
        pl.BlockSpec((bb, D_HEADS, D_DK), st3),
        pl.BlockSpec((bb, 1, D_HEADS), st3),
    ]
    out_specs = [
        pl.BlockSpec((bb, tb, D_VW), lambda b, t: (b, t, 0)),
        pl.BlockSpec((bb, D_HEADS, D_DK, D_DV), st4),
        pl.BlockSpec((bb, D_HEADS, D_DK), st3),
        pl.BlockSpec((bb, 1, D_HEADS), st3),
    ]
    state = bb * D_HEADS * D_DK * D_DV * 4
    blocks = bb * tb * (2 * D_KW + 3 * D_VW + LANES) * 4 + bb * tb * D_VW * 2 + 2 * state
    return pl.pallas_call(
        functools.partial(_mlstm_kernel, c=c, nchunks=tb // c),
        out_shape=[jax.ShapeDtypeStruct((bsz, l, D_VW), BF16),
                   jax.ShapeDtypeStruct((bsz, D_HEADS, D_DK, D_DV), F32),
                   jax.ShapeDtypeStruct((bsz, D_HEADS, D_DK), F32),
                   jax.ShapeDtypeStruct((bsz, 1, D_HEADS), F32)],
        grid=(bsz // bb, nblk),
        in_specs=in_specs,
        out_specs=out_specs,
        scratch_shapes=[pltpu.VMEM((bb, D_HEADS, D_DK, D_DV), F32),
                        pltpu.VMEM((bb, D_HEADS, D_DK), F32),
                        pltpu.VMEM((bb, 1, D_HEADS), F32)],
        compiler_params=pltpu.CompilerParams(
            dimension_semantics=("parallel", "arbitrary"),
            vmem_limit_bytes=_vmem_limit(blocks, state)),
        name="mlstm_mixer",
    )(proj3, proj3, proj3, proj3, proj3, proj3, ib_row, fb_row, norm_g_row, c0, n0, m0)


def _s5_expand_operators(kc_ref, bc_ref, cc_ref, bd_ref, bst_ref, cst_ref):
    tc, gt, sw, cg = S5_CHUNK, S5_GROUP_BLOCK, 2 * C_STATE, C_GROUP
    w_t = tc * LANES
    iota = lambda shape, d: lax.broadcasted_iota(jnp.int32, shape, d)
    row_g = (iota((w_t, LANES), 0) // cg) % gt
    tile16 = jnp.where(iota((cg, LANES), 1) % cg == iota((cg, LANES), 0), 1.0, 0.0).astype(BF16)
    bd = jnp.where(row_g == iota((w_t, LANES), 1) // cg, _mm(kc_ref[...], tile16), 0.0).astype(BF16)
    bd_ref[:, LANES:2 * LANES] = bd
    bd_ref[0:w_t - LANES, 0:LANES] = bd[LANES:, :]
    bd_ref[w_t - LANES:w_t, 0:LANES] = jnp.zeros((LANES, LANES), BF16)
    bc = bc_ref[...].astype(F32)
    for g in range(gt):
        bst_ref[:, g * sw:(g + 1) * sw] = jnp.where(row_g == g, bc, 0.0).astype(BF16)
    src, dst = iota((tc * cg, w_t), 0), iota((tc * cg, w_t), 1)
    spread = jnp.where((src // cg == dst // LANES) & (src % cg == dst % cg), 1.0, 0.0).astype(BF16)
    lane_g = (iota((sw, w_t), 1) // cg) % gt
    for g in range(gt):
        full = _mm(cc_ref[g * sw:(g + 1) * sw, :], spread)
        cst_ref[g * sw:(g + 1) * sw, :] = jnp.where(lane_g == g, full, 0.0).astype(BF16)


def _s5_kernel(u_ref, kc_ref, bc_ref, cc_ref, apow_ref, x0_ref, y_ref, xf_ref, bd_ref, bst_ref, cst_ref,
               *, nc, bb):
    m = nc * bb
    tc = S5_CHUNK
    sw = 2 * C_STATE

    @pl.when(pl.program_id(1) == 0)
    def _():
        _s5_expand_operators(kc_ref, bc_ref, cc_ref, bd_ref, bst_ref, cst_ref)

    row = lax.broadcasted_iota(jnp.int32, (m, sw), 0)
    n_idx = row & (nc - 1)
    n_log = int(math.log2(nc))

    def cmul(a1, a2, x):
        return a1 * x + a2 * pltpu.roll(x, C_STATE, 1)

    lhs = jnp.concatenate([u_ref[pl.ds(tau, m, stride=tc), :].astype(BF16) for tau in range(tc)], axis=1)
    e_all = _mm(lhs, bst_ref[...])
    x_start = []
    for g in range(S5_GROUP_BLOCK):
        gs = slice(g * sw, (g + 1) * sw)
        x0_rows = jnp.zeros((m, sw), F32)
        for b in range(bb):
            x0_rows = jnp.where(row == b * nc, x0_ref[b, :, gs], x0_rows)
        x = e_all[:, gs] + cmul(apow_ref[0:1, gs], apow_ref[1:2, gs], x0_rows)
        for j in range(n_log):
            sh = 1 << j
            shifted = jnp.where(n_idx >= sh, pltpu.roll(x, sh, 0), 0.0)
            x = x + cmul(apow_ref[2 * j:2 * j + 1, gs], apow_ref[2 * j + 1:2 * j + 2, gs], shifted)
        x_start.append(jnp.where(n_idx >= 1, pltpu.roll(x, 1, 0), x0_rows).astype(BF16))
        for b in range(bb):
            xf_ref[b, :, gs] = x[b * nc + nc - 1:b * nc + nc, :]
    y_state = _mm(jnp.concatenate(x_start, axis=1), cst_ref[...])
    for tau in range(0, tc, 2):
        width = (tau + 2) * LANES
        start = (tc - 2 - tau) * LANES
        y = _mm(lhs[:, :width], bd_ref[start:start + width, :]) + y_state[:, tau * LANES:(tau + 2) * LANES]
        y_ref[pl.ds(tau, m, stride=tc), :] = y[:, :LANES]
        y_ref[pl.ds(tau + 1, m, stride=tc), :] = y[:, LANES:]


def _s5_chunks(proj2d, kc, bc, cc, apow, x0, *, l, bb):
    tc = S5_CHUNK
    nc = l // tc
    bsz = x0.shape[1]
    nt = C_GROUPS // S5_GROUP_BLOCK
    rows = bb * l
    sw = 2 * C_STATE
    sw_t = S5_GROUP_BLOCK * sw
    w_t = tc * LANES
    assert CD_U % LANES == 0 and bsz % bb == 0
    blocks = (2 * rows * LANES * 4 + 2 * w_t * LANES * 2 + sw_t * tc * C_GROUP * 2 + apow.shape[1] * sw_t * 4
              + 2 * bb * SUBLANES * sw_t * 4)
    scratch = w_t * 2 * LANES * 2 + 2 * w_t * sw_t * 2
    temps = bb * nc * (w_t * 2 + w_t * 4 + 3 * sw_t * 4) + 4 * sw * w_t * 4
    return pl.pallas_call(
        functools.partial(_s5_kernel, nc=nc, bb=bb),
        out_shape=[jax.ShapeDtypeStruct((bsz * l, C_W), F32),
                   jax.ShapeDtypeStruct((nt, bsz, 1, sw_t), F32)],
        grid=(nt, bsz // bb),
        in_specs=[pl.BlockSpec((rows, LANES), lambda i, j: (j, CD_U // LANES + i)),
                  pl.BlockSpec((None, w_t, C_GROUP), lambda i, j: (i, 0, 0)),
                  pl.BlockSpec((None, w_t, sw), lambda i, j: (i, 0, 0)),
                  pl.BlockSpec((None, sw_t, tc * C_GROUP), lambda i, j: (i, 0, 0)),
                  pl.BlockSpec((None, apow.shape[1], sw_t), lambda i, j: (i, 0, 0)),
                  pl.BlockSpec((None, bb, 1, sw_t), lambda i, j: (i, j, 0, 0))],
        out_specs=[pl.BlockSpec((rows, LANES), lambda i, j: (j, i)),
                   pl.BlockSpec((None, bb, 1, sw_t), lambda i, j: (i, j, 0, 0))],
        scratch_shapes=[pltpu.VMEM((w_t, 2 * LANES), BF16),
                        pltpu.VMEM((w_t, sw_t), BF16),
                        pltpu.VMEM((sw_t, w_t), BF16)],
        compiler_params=pltpu.CompilerParams(
            dimension_semantics=("parallel", "arbitrary"),
            vmem_limit_bytes=_vmem_limit(blocks, scratch + temps)),
        name="s5_chunks",
    )(proj2d, kc, bc, cc, apow, x0)


def _s5_glu_kernel(y_ref, u_ref, z_ref, d_ref, w_ref, b_ref, o_ref):
    y = jax.nn.gelu(y_ref[...] + d_ref[...] * u_ref[...])
    gate = jax.nn.sigmoid(jnp.dot(y.astype(BF16), w_ref[...], preferred_element_type=F32) + b_ref[...])
    o_ref[...] = (y * gate * jax.nn.silu(z_ref[...])).astype(o_ref.dtype)


def _s5_glu(y2d, proj2d, d_row, glu_w, glu_b_row):
    m = y2d.shape[0]
    tm = min(m, 512)
    blocks = 3 * tm * C_W * 4 + C_W * C_W * 2 + tm * C_W * 2
    return pl.pallas_call(
        _s5_glu_kernel,
        out_shape=jax.ShapeDtypeStruct((m, C_W), BF16),
        grid=(m // tm,),
        in_specs=[pl.BlockSpec((tm, C_W), lambda i: (i, 0)),
                  pl.BlockSpec((tm, C_W), lambda i: (i, CD_U // C_W)),
                  pl.BlockSpec((tm, C_W), lambda i: (i, CD_Z // C_W)),
                  pl.BlockSpec((1, C_W), lambda i: (0, 0)),
                  pl.BlockSpec((C_W, C_W), lambda i: (0, 0)),
                  pl.BlockSpec((1, C_W), lambda i: (0, 0))],
        out_specs=pl.BlockSpec((tm, C_W), lambda i: (i, 0)),
        compiler_params=pltpu.CompilerParams(
            dimension_semantics=("parallel",),
            vmem_limit_bytes=_vmem_limit(blocks)),
        name="s5_glu",
    )(y2d, proj2d, proj2d, d_row, glu_w, glu_b_row)


def _s5_operators(lam_re, lam_im, log_dt, b_re, b_im, c_re, c_im, n_log):
    g, p = lam_re.shape
    tc = S5_CHUNK
    dt = jnp.exp(log_dt.astype(F32))[:, None]
    mag = jnp.exp(lam_re * dt)
    ab_re, ab_im = mag * jnp.cos(lam_im * dt), mag * jnp.sin(lam_im * dt)
    den = lam_re * lam_re + lam_im * lam_im
    er = ab_re - 1.0
    zr = (er * lam_re + ab_im * lam_im) / den
    zi = (ab_im * lam_re - er * lam_im) / den
    bb_re = zr[..., None] * b_re - zi[..., None] * b_im
    bb_im = zr[..., None] * b_im + zi[..., None] * b_re
    pr, pi = [jnp.ones_like(ab_re)], [jnp.zeros_like(ab_re)]
    for _ in range(tc):
        pr, pi = pr + [pr[-1] * ab_re - pi[-1] * ab_im], pi + [pr[-1] * ab_im + pi[-1] * ab_re]
    pw_re, pw_im = jnp.stack(pr), jnp.stack(pi)
    abr = pw_re[:tc, :, :, None] * bb_re - pw_im[:tc, :, :, None] * bb_im
    abi = pw_re[:tc, :, :, None] * bb_im + pw_im[:tc, :, :, None] * bb_re
    kern = (jnp.einsum('gjp,dgpi->dgji', c_re, abr, precision=HIGHEST)
            - jnp.einsum('gjp,dgpi->dgji', c_im, abi, precision=HIGHEST))
    gt = S5_GROUP_BLOCK
    nt = g // gt
    sw = 2 * p
    kc = kern[::-1].reshape(tc, nt, gt, C_GROUP, C_GROUP).transpose(1, 0, 2, 4, 3)
    kc = kc.reshape(nt, tc * gt * C_GROUP, C_GROUP)
    ab = jnp.concatenate([abr, abi], axis=2)[::-1]
    bc = ab.reshape(tc, nt, gt, sw, C_GROUP).transpose(1, 0, 2, 4, 3).reshape(nt, tc * gt * C_GROUP, sw)
    cr = c_re[None] * pw_re[1:, :, None, :] - c_im[None] * pw_im[1:, :, None, :]
    ci = -(c_re[None] * pw_im[1:, :, None, :] + c_im[None] * pw_re[1:, :, None, :])
    cc = jnp.concatenate([cr, ci], axis=3)
    cc = cc.reshape(tc, nt, gt, C_GROUP, sw).transpose(1, 2, 4, 0, 3).reshape(nt, gt * sw, tc * C_GROUP)
    r, i = pw_re[tc], pw_im[tc]
    rows = []
    for _ in range(max(n_log, 1)):
        rows += [jnp.concatenate([r, r], -1), jnp.concatenate([-i, i], -1)]
        r, i = r * r - i * i, 2.0 * r * i
    apow = jnp.stack(rows, axis=1)
    apow = apow.reshape(nt, gt, -1, sw).transpose(0, 2, 1, 3).reshape(nt, -1, gt * sw)
    return kc.astype(BF16), bc.astype(BF16), cc.astype(BF16), apow


def _s5(proj2d, ops, x0_re, x0_im, *, l):
    kc, bc, cc, apow = ops
    bsz = x0_re.shape[0]
    nt = C_GROUPS // S5_GROUP_BLOCK
    x0 = jnp.concatenate([x0_re, x0_im], axis=-1).reshape(bsz, nt, 1, -1).transpose(1, 0, 2, 3)
    bb = bsz if bsz * l <= 4096 else 1
    y, xf = _s5_chunks(proj2d, kc, bc, cc, apow, x0, l=l, bb=bb)
    xf = xf.transpose(1, 0, 2, 3).reshape(bsz, C_GROUPS, 2 * C_STATE)
    return y, xf[..., :C_STATE], xf[..., C_STATE:]


AB_SRC_GA = AB_QKV
AB_SRC_QKV = AB_SRC_GA + A_GATE_RANK
AB_SRC_TAIL = AB_SRC_QKV + B_QKV + B_VW
IN_AB = AB_SRC_TAIL + 2 * B_HEADS
IN_CD = CD_SMALL + 2 * D_HEADS
WPREP_AB_TILE = 512
WPREP_CD_TILE = 640


def _prep_w_ab_kernel(wt_ref, ga_ref, o_ref):
    j = pl.program_id(0)
    tile = o_ref.shape[1]
    n_small = A_GATE_RANK + 2 * B_HEADS

    @pl.when(j < AB_SMALL // tile)
    def _():
        o_ref[...] = wt_ref[...].T.astype(BF16)

    @pl.when(j == AB_SMALL // tile)
    def _():
        rows = jnp.concatenate([ga_ref[...], wt_ref[tile - 2 * B_HEADS:tile, :],
                                jnp.zeros((tile - n_small, wt_ref.shape[1]), F32)], axis=0)
        o_ref[...] = rows.T.astype(BF16)

    @pl.when(j > AB_SMALL // tile)
    def _():
        o_ref[...] = jnp.zeros(o_ref.shape, BF16)


def _prep_w_ab(w_t, n_out):
    n_in, d = w_t.shape
    tile = WPREP_AB_TILE
    assert AB_QKV % tile == 0 and AB_SMALL % tile == 0 and n_out % tile == 0 and n_in >= tile

    unit = A_GATE_RANK
    assert tile % unit == 0 and (AB_SRC_QKV - AB_QKV) % unit == 0 and (n_in - tile) % unit == 0

    def src_row(j):
        k = j * (tile // unit)
        k = jnp.where(j < AB_QKV // tile, k,
                      jnp.where(j < AB_SMALL // tile, k + (AB_SRC_QKV - AB_QKV) // unit, (n_in - tile) // unit))
        return unit * k

    return pl.pallas_call(
        _prep_w_ab_kernel,
        out_shape=jax.ShapeDtypeStruct((d, n_out), BF16),
        grid=(n_out // tile,),
        in_specs=[pl.BlockSpec((pl.Element(tile), pl.Element(d)), lambda j: (src_row(j), 0)),
                  pl.BlockSpec((pl.Element(A_GATE_RANK), pl.Element(d)), lambda j: (AB_SRC_GA, 0))],
        out_specs=pl.BlockSpec((d, tile), lambda j: (0, j)),
        compiler_params=pltpu.CompilerParams(
            dimension_semantics=("parallel",),
            vmem_limit_bytes=_vmem_limit(tile * d * 6, 2 * tile * d * 4)),
        name="prep_w_in_ab",
    )(w_t, w_t)


def _prep_w_cd_kernel(wt_ref, o_ref):
    tile = o_ref.shape[1]
    row = pl.program_id(0) * tile + lax.broadcasted_iota(jnp.int32, wt_ref.shape, 0)
    o_ref[...] = jnp.where(row < IN_CD, wt_ref[...], 0.0).T.astype(BF16)


def _prep_w_cd(w_t, n_out):
    n_in, d = w_t.shape
    tile = WPREP_CD_TILE
    assert n_out % tile == 0
    return pl.pallas_call(
        _prep_w_cd_kernel,
        out_shape=jax.ShapeDtypeStruct((d, n_out), BF16),
        grid=(n_out // tile,),
        in_specs=[pl.BlockSpec((tile, d), lambda j: (j, 0))],
        out_specs=pl.BlockSpec((d, tile), lambda j: (0, j)),
        compiler_params=pltpu.CompilerParams(
            dimension_semantics=("parallel",),
            vmem_limit_bytes=_vmem_limit(tile * d * 6, 2 * tile * d * 4)),
        name="prep_w_in_cd",
    )(w_t)


def _lane_row(vals, lane0):
    return jnp.zeros((1, LANES), F32).at[0, lane0:lane0 + vals.shape[0]].set(vals.astype(F32))


def _prepare_weights(norm_g, final_norm_g, w_in_ab, a_gate_w, a_gate_b, a_norm_g, b_conv_w, b_a_log,
                     b_dt_bias, b_norm_g, w_out_ab, w_in_cd, c_lam_re, c_lam_im, c_log_dt, c_b_re,
                     c_b_im, c_c_re, c_c_im, c_d, c_glu_w, c_glu_b, d_i_bias, d_f_bias, d_norm_g,
                     w_out_cd, n_log):
    assert w_in_ab.shape[1] == IN_AB and w_in_cd.shape[1] == IN_CD
    w_ab = _prep_w_ab(w_in_ab.astype(F32).T, _round_up(AB_SMALL + LANES, PROJ_TN))
    w_cd = _prep_w_cd(w_in_cd.astype(F32).T, _round_up(CD_SMALL + LANES, PROJ_TN))
    gate_w = jnp.zeros((LANES, A_KW), F32).at[AB_GA_LANE:AB_GA_LANE + A_GATE_RANK].set(
        a_gate_w.astype(F32)).astype(BF16)
    return dict(
        norm_g=norm_g.astype(F32), final_g=final_norm_g.astype(F32)[None, :],
        w_ab=w_ab, w_cd=w_cd, gate_w=gate_w, gate_b=a_gate_b.astype(F32)[None, :],
        a_norm_g=a_norm_g.astype(F32)[None, :], conv_w=b_conv_w.astype(F32),
        alog=_lane_row(b_a_log, AB_APRE_LANE), dtb=_lane_row(b_dt_bias, AB_APRE_LANE),
        b_norm_g=b_norm_g.astype(F32)[None, :],
        w_out_a=w_out_ab[:A_VW].astype(BF16), w_out_b=w_out_ab[A_VW:].astype(BF16),
        s5_ops=_s5_operators(c_lam_re.astype(F32), c_lam_im.astype(F32), c_log_dt, c_b_re.astype(F32),
                             c_b_im.astype(F32), c_c_re.astype(F32), c_c_im.astype(F32), n_log),
        c_d=c_d.astype(F32).reshape(1, C_W), glu_w=c_glu_w.astype(BF16),
        glu_b=c_glu_b.astype(F32)[None, :],
        ib=_lane_row(d_i_bias, CD_I_LANE), fb=_lane_row(d_f_bias, CD_F_LANE),
        d_norm_g=d_norm_g.astype(F32)[None, :],
        w_out_c=w_out_cd[:C_W].astype(BF16), w_out_d=w_out_cd[C_W:].astype(BF16),
    )


def _trunk(x, conv_prev, s_gla0, s_gdn0, s5_re0, s5_im0, mc0, mn0, mm0, w):
    bsz, l, d = x.shape
    c = min(CHUNK, l)
    tb = min(l, 8 * c)
    assert l % tb == 0 and l % S5_CHUNK == 0
    x2d = x.reshape(bsz * l, d)

    proj = _norm_matmul(x2d, w['norm_g'][0:1], w['w_ab'])
    proj3 = proj.reshape(bsz, l, proj.shape[1])
    bb = MIXER_STREAMS if bsz % MIXER_STREAMS == 0 else 1
    tb_s = min(l, (8 // bb) * c)
    o_a, s_gla = _gla(proj3, w['gate_w'], w['gate_b'], w['a_norm_g'], s_gla0.astype(F32), c=c, tb=tb_s, bb=bb)
    o_b, s_gdn = _gdn(proj3, conv_prev.astype(F32), w['conv_w'], w['alog'], w['dtb'], w['b_norm_g'],
                      s_gdn0.astype(F32), c=c, tb=tb)
    conv_new = proj3[:, l - (B_CONV - 1):, AB_QKV:AB_QKV + B_QKV]
    h1 = _out_proj(o_a.reshape(bsz * l, A_VW), o_b.reshape(bsz * l, B_VW), w['w_out_a'], w['w_out_b'], x2d)

    proj = _norm_matmul(h1, w['norm_g'][1:2], w['w_cd'])
    proj3 = proj.reshape(bsz, l, proj.shape[1])
    y, s5_re, s5_im = _s5(proj, w['s5_ops'], s5_re0.astype(F32), s5_im0.astype(F32), l=l)
    o_c = _s5_glu(y, proj, w['c_d'], w['glu_w'], w['glu_b'])
    o_d, mc, mn, mm = _mlstm(proj3, w['ib'], w['fb'], w['d_norm_g'], mc0.astype(F32),
                             mn0.astype(F32), mm0.astype(F32)[:, None, :], c=c, tb=tb_s, bb=bb)
    y_out = _out_proj(o_c, o_d.reshape(bsz * l, D_VW), w['w_out_c'], w['w_out_d'], h1, w['final_g'])
    dt = x.dtype
    return (y_out.reshape(bsz, l, d).astype(dt), conv_new.astype(dt), s_gla.astype(dt), s_gdn.astype(dt),
            s5_re.astype(dt), s5_im.astype(dt), mc.astype(dt), mn.astype(dt), mm[:, 0, :].astype(dt))


def kernel(x_prompt, x_sample, cache_gdn_conv, state_gla, state_gdn, state_s5_re, state_s5_im,
           state_mlstm_c, state_mlstm_n, state_mlstm_m, norm_g, final_norm_g, w_in_ab, a_gate_w,
           a_gate_b, a_norm_g, b_conv_w, b_a_log, b_dt_bias, b_norm_g, w_out_ab, w_in_cd, c_lam_re,
           c_lam_im, c_log_dt, c_b_re, c_b_im, c_c_re, c_c_im, c_d, c_glu_w, c_glu_b, d_i_bias,
           d_f_bias, d_norm_g, w_out_cd):
    n_log = int(math.log2(max(x_prompt.shape[1], x_sample.shape[1]) // S5_CHUNK))
    w = _prepare_weights(norm_g, final_norm_g, w_in_ab, a_gate_w, a_gate_b, a_norm_g, b_conv_w, b_a_log,
                         b_dt_bias, b_norm_g, w_out_ab, w_in_cd, c_lam_re, c_lam_im, c_log_dt, c_b_re,
                         c_b_im, c_c_re, c_c_im, c_d, c_glu_w, c_glu_b, d_i_bias, d_f_bias, d_norm_g,
                         w_out_cd, n_log)
    nb = x_prompt.shape[0]
    zeros = lambda *shape: jnp.zeros(shape, F32)
    p_out = _trunk(x_prompt, zeros(nb, B_CONV - 1, B_QKV), zeros(nb, A_HEADS, A_DK, A_DV),
                   zeros(nb, B_HEADS, B_DK, B_DV), zeros(nb, C_GROUPS, C_STATE), zeros(nb, C_GROUPS, C_STATE),
                   zeros(nb, D_HEADS, D_DK, D_DV), zeros(nb, D_HEADS, D_DK), zeros(nb, D_HEADS), w)
    s_out = _trunk(x_sample, cache_gdn_conv, state_gla, state_gdn, state_s5_re, state_s5_im,
                   state_mlstm_c, state_mlstm_n, state_mlstm_m, w)
    return (p_out[0], s_out[0]) + tuple(p_out[1:]) + tuple(s_out[1:])
```

```python
import functools
import math

import jax
import jax.numpy as jnp
from jax import lax
from jax.experimental import pallas as pl
from jax.experimental.pallas import tpu as pltpu

F32 = jnp.float32
BF16 = jnp.bfloat16
HIGHEST = lax.Precision.HIGHEST

NORM_EPS = 1e-6
CHUNK = 64
A_HEADS, A_DK, A_DV, A_GATE_RANK, A_GATE_TAU = 4, 128, 256, 16, 16.0
B_HEADS, B_DK, B_DV, B_CONV = 8, 128, 128, 4
C_GROUP, C_GROUPS, C_STATE = 16, 64, 64
D_HEADS, D_DK, D_DV = 4, 128, 256
A_KW, A_VW = A_HEADS * A_DK, A_HEADS * A_DV
B_KW, B_VW = B_HEADS * B_DK, B_HEADS * B_DV
B_QKV = 2 * B_KW + B_VW
C_W = C_GROUPS * C_GROUP
D_KW, D_VW = D_HEADS * D_DK, D_HEADS * D_DV

LANES = 128
SUBLANES = 8
VMEM_BYTES_V7X = 64 * 1024 * 1024

AB_Q, AB_K, AB_V, AB_Z = 0, A_KW, 2 * A_KW, 2 * A_KW + A_VW
AB_QKV = AB_Z + A_VW
AB_ZB = AB_QKV + B_QKV
AB_SMALL = AB_ZB + B_VW
AB_GA_LANE, AB_BETA_LANE, AB_APRE_LANE = 0, A_GATE_RANK, A_GATE_RANK + B_HEADS
CD_U, CD_Z = 0, C_W
CD_Q = 2 * C_W
CD_K = CD_Q + D_KW
CD_V = CD_K + D_KW
CD_O = CD_V + D_VW
CD_ZD = CD_O + D_VW
CD_SMALL = CD_ZD + D_VW
CD_I_LANE, CD_F_LANE = 0, D_HEADS

PROJ_TILE = 1024
S5_CHUNK = 16
S5_GROUP_BLOCK = 8
MIXER_STREAMS = 2
GDN_WY_CHUNKS = 4


def _round_up(x, m):
    return (x + m - 1) // m * m


def _vmem_limit(block_bytes, scratch_bytes=0):
    est = 2 * block_bytes + scratch_bytes
    return int(min(max(2 * est, 32 * 1024 * 1024), VMEM_BYTES_V7X - 8 * 1024 * 1024))


def _mm(a, b):
    return jnp.dot(a, b, preferred_element_type=F32)


def _dot(a, b):
    return _mm(a.astype(BF16), b.astype(BF16))


def _dot_nt(a, b):
    return lax.dot_general(a.astype(BF16), b.astype(BF16), (((1,), (1,)), ((), ())),
                           preferred_element_type=F32)


def _dot_tn(a, b):
    return lax.dot_general(a.astype(BF16), b.astype(BF16), (((0,), (0,)), ((), ())),
                           preferred_element_type=F32)


def _split2(x):
    hi = x.astype(BF16)
    return hi, (x - hi.astype(F32)).astype(BF16)


def _split3(x):
    hi = x.astype(BF16)
    r = x - hi.astype(F32)
    mid = r.astype(BF16)
    return hi, mid, (r - mid.astype(F32)).astype(BF16)


def _dot3(a, b):
    ah, al = _split2(a)
    bh, bl = _split2(b)
    return _mm(ah, bh) + _mm(ah, bl) + _mm(al, bh)


def _cumsum_rows(tri_bf16, x):
    hi, mid, lo = _split3(x)
    return _mm(tri_bf16, hi) + _mm(tri_bf16, mid) + _mm(tri_bf16, lo)


def _select_rows(sel_bf16, x):
    nt = lambda b: lax.dot_general(sel_bf16, b, (((1,), (1,)), ((), ())), preferred_element_type=F32)
    hi, mid, lo = _split3(x)
    return nt(hi) + nt(mid) + nt(lo)


def _lane_selector(lane0):
    r = lax.broadcasted_iota(jnp.int32, (SUBLANES, LANES), 0)
    l = lax.broadcasted_iota(jnp.int32, (SUBLANES, LANES), 1)
    return jnp.where(l == r + lane0, 1.0, 0.0).astype(BF16)


def _causal_masks(c):
    row = lax.broadcasted_iota(jnp.int32, (c, c), 0)
    col = lax.broadcasted_iota(jnp.int32, (c, c), 1)
    return row >= col, row > col


def _dot_rows(xn, w_rows):
    return lax.dot_general(xn, w_rows.astype(BF16), (((1,), (1,)), ((), ())), preferred_element_type=F32)


def _norm_matmul_kernel(x_ref, g_ref, wt_ref, head_ref, o_ref, xn_ref, *, n_full, head_rows, tail_rows):
    j = pl.program_id(1)

    @pl.when(j == 0)
    def _():
        x = x_ref[...]
        y = x * lax.rsqrt(jnp.mean(x * x, axis=-1, keepdims=True) + NORM_EPS) * g_ref[...]
        xn_ref[...] = y.astype(BF16)

    @pl.when(j < n_full)
    def _():
        o_ref[...] = _dot_rows(xn_ref[...], wt_ref[...])

    @pl.when(j == n_full)
    def _():
        tile, k = wt_ref.shape
        parts = [head_ref[0:head_rows, :]] if head_rows else []
        parts += [wt_ref[tile - tail_rows:tile, :], jnp.zeros((LANES - head_rows - tail_rows, k), F32)]
        o_ref[:, 0:LANES] = _dot_rows(xn_ref[...], jnp.concatenate(parts, axis=0))
        o_ref[:, LANES:] = jnp.zeros((o_ref.shape[0], tile - LANES), F32)


def _norm_matmul(x2d, g_row, w_t, *, n_aligned, shift_from, shift, head_row, head_rows, tail_rows):
    m, d = x2d.shape
    n_in = w_t.shape[0]
    tm = min(m, 1024)
    tile = PROJ_TILE
    unit = SUBLANES
    n_full = n_aligned // tile
    assert m % tm == 0 and n_aligned % tile == 0 and shift_from % tile == 0
    assert shift % unit == 0 and (n_in - tile) % unit == 0 and n_in >= tile

    def src_row(i, j):
        k = j * (tile // unit)
        k = jnp.where(j < shift_from // tile, k, jnp.where(j < n_full, k + shift // unit, (n_in - tile) // unit))
        return unit * k, 0

    blocks = tm * d * 4 + tile * d * 4 + tm * tile * 4 + SUBLANES * 2 * d * 4
    return pl.pallas_call(
        functools.partial(_norm_matmul_kernel, n_full=n_full, head_rows=head_rows, tail_rows=tail_rows),
        out_shape=jax.ShapeDtypeStruct((m, n_aligned + LANES), F32),
        grid=(m // tm, n_full + 1),
        in_specs=[pl.BlockSpec((tm, d), lambda i, j: (i, 0)),
                  pl.BlockSpec((1, d), lambda i, j: (0, 0)),
                  pl.BlockSpec((pl.Element(tile), pl.Element(d)), src_row),
                  pl.BlockSpec((pl.Element(2 * SUBLANES), pl.Element(d)), lambda i, j: (head_row, 0))],
        out_specs=pl.BlockSpec((tm, tile), lambda i, j: (i, j)),
        scratch_shapes=[pltpu.VMEM((tm, d), BF16)],
        compiler_params=pltpu.CompilerParams(
            dimension_semantics=("parallel", "arbitrary"),
            vmem_limit_bytes=_vmem_limit(blocks, tm * d * 2 + tile * d * 2)),
        name="norm_in_proj",
    )(x2d, g_row, w_t, w_t)


def _out_proj_kernel(a_ref, b_ref, wa_ref, wb_ref, h_ref, o_ref):
    out = (jnp.dot(a_ref[...], wa_ref[...], preferred_element_type=F32)
           + jnp.dot(b_ref[...], wb_ref[...], preferred_element_type=F32))
    o_ref[...] = h_ref[...] + out


def _out_proj_norm_kernel(a_ref, b_ref, wa_ref, wb_ref, h_ref, g_ref, o_ref):
    out = (jnp.dot(a_ref[...], wa_ref[...], preferred_element_type=F32)
           + jnp.dot(b_ref[...], wb_ref[...], preferred_element_type=F32))
    h = h_ref[...] + out
    o_ref[...] = h * lax.rsqrt(jnp.mean(h * h, axis=-1, keepdims=True) + NORM_EPS) * g_ref[...]


def _out_proj(mix_a, mix_b, w_a, w_b, h2d, final_g_row=None):
    m, d = h2d.shape
    ka, kb = mix_a.shape[1], mix_b.shape[1]
    tm = min(m, 512)
    assert m % tm == 0
    in_specs = [pl.BlockSpec((tm, ka), lambda i: (i, 0)),
                pl.BlockSpec((tm, kb), lambda i: (i, 0)),
                pl.BlockSpec((ka, d), lambda i: (0, 0)),
                pl.BlockSpec((kb, d), lambda i: (0, 0)),
                pl.BlockSpec((tm, d), lambda i: (i, 0))]
    args = [mix_a, mix_b, w_a, w_b, h2d]
    kernel = _out_proj_kernel
    if final_g_row is not None:
        in_specs.append(pl.BlockSpec((1, d), lambda i: (0, 0)))
        args.append(final_g_row)
        kernel = _out_proj_norm_kernel
    blocks = tm * (ka + kb) * 2 + (ka + kb) * d * 2 + 2 * tm * d * 4
    return pl.pallas_call(
        kernel,
        out_shape=jax.ShapeDtypeStruct((m, d), F32),
        grid=(m // tm,),
        in_specs=in_specs,
        out_specs=pl.BlockSpec((tm, d), lambda i: (i, 0)),
        compiler_params=pltpu.CompilerParams(
            dimension_semantics=("parallel",),
            vmem_limit_bytes=_vmem_limit(blocks)),
        name="out_proj_norm" if final_g_row is not None else "out_proj",
    )(*args)


def _gla_kernel(q_ref, k_ref, v_ref, z_ref, sm_ref, gw_ref, gb_ref, ng_ref, s0_ref,
                o_ref, sout_ref, st_ref, *, c, nchunks):
    t = pl.program_id(1)
    last_t = pl.num_programs(1) - 1
    bb = q_ref.shape[0]
    chains = [(bi, h) for bi in range(bb) for h in range(A_HEADS)]
    ksl = [slice(h * A_DK, (h + 1) * A_DK) for _, h in chains]
    vsl = [slice(h * A_DV, (h + 1) * A_DV) for _, h in chains]
    idx = range(len(chains))

    @pl.when(t == 0)
    def _():
        for bi, h in chains:
            st_ref[bi, h] = s0_ref[bi, h].T

    causal, _ = _causal_masks(c)
    tri = causal.astype(BF16)
    gw = gw_ref[...]
    gb = gb_ref[...]

    def body(n, carry):
        sl = pl.ds(pl.multiple_of(n * c, c), c)
        b_all = [_cumsum_rows(tri, jax.nn.log_sigmoid(_dot(sm_ref[bi, sl, :], gw) + gb) * (1.0 / A_GATE_TAU))
                 for bi in range(bb)]
        b = [b_all[chains[i][0]][:, ksl[i]] for i in idx]
        b_last = [b[i][c - 1:c, :] for i in idx]
        k = [k_ref[chains[i][0], sl, ksl[i]] for i in idx]
        v = [v_ref[chains[i][0], sl, vsl[i]].astype(BF16) for i in idx]
        q_dec = [(q_ref[chains[i][0], sl, ksl[i]] * (A_DK ** -0.5) * jnp.exp(b[i])).astype(BF16) for i in idx]
        k_dec = [(k[i] * jnp.exp(-b[i])).astype(BF16) for i in idx]
        k_w = [(k[i] * jnp.exp(b_last[i] - b[i])).astype(BF16) for i in idx]
        scores = [jnp.where(causal, _dot_nt(q_dec[i], k_dec[i]), 0.0).astype(BF16) for i in idx]
        s_t = [st_ref[bi, h] for bi, h in chains]
        outs = [_mm(scores[i], v[i]) + _dot_nt(q_dec[i], s_t[i]) for i in idx]
        for i, (bi, h) in enumerate(chains):
            st_ref[bi, h] = s_t[i] * jnp.exp(b_last[i]) + _dot_tn(v[i], k_w[i])
        for i, (bi, h) in enumerate(chains):
            o = outs[i]
            o = o * lax.rsqrt(jnp.mean(o * o, axis=-1, keepdims=True) + NORM_EPS) * ng_ref[:, vsl[i]]
            o_ref[bi, sl, vsl[i]] = (o * jax.nn.silu(z_ref[bi, sl, vsl[i]])).astype(o_ref.dtype)
        return carry

    lax.fori_loop(0, nchunks, body, 0)

    @pl.when(t == last_t)
    def _():
        for bi, h in chains:
            sout_ref[bi, h] = st_ref[bi, h].T


def _gla(proj3, gate_w_pad, gate_b_row, norm_g_row, s0, *, c, tb, bb):
    bsz, l, _ = proj3.shape
    nblk = l // tb
    assert bsz % bb == 0
    tok = lambda col: (lambda b, t: (b, t, col))
    in_specs = [
        pl.BlockSpec((bb, tb, A_KW), tok(AB_Q // A_KW)),
        pl.BlockSpec((bb, tb, A_KW), tok(AB_K // A_KW)),
        pl.BlockSpec((bb, tb, A_VW), tok(AB_V // A_VW)),
        pl.BlockSpec((bb, tb, A_VW), tok(AB_Z // A_VW)),
        pl.BlockSpec((bb, tb, LANES), tok(AB_SMALL // LANES)),
        pl.BlockSpec((LANES, A_KW), lambda b, t: (0, 0)),
        pl.BlockSpec((1, A_KW), lambda b, t: (0, 0)),
        pl.BlockSpec((1, A_VW), lambda b, t: (0, 0)),
        pl.BlockSpec((bb, A_HEADS, A_DK, A_DV), lambda b, t: (b, 0, 0, 0)),
    ]
    out_specs = [
        pl.BlockSpec((bb, tb, A_VW), lambda b, t: (b, t, 0)),
        pl.BlockSpec((bb, A_HEADS, A_DK, A_DV), lambda b, t: (b, 0, 0, 0)),
    ]
    state = bb * A_HEADS * A_DK * A_DV * 4
    blocks = bb * tb * (2 * A_KW + 2 * A_VW + LANES) * 4 + bb * tb * A_VW * 2 + 2 * state
    return pl.pallas_call(
        functools.partial(_gla_kernel, c=c, nchunks=tb // c),
        out_shape=[jax.ShapeDtypeStruct((bsz, l, A_VW), BF16),
                   jax.ShapeDtypeStruct((bsz, A_HEADS, A_DK, A_DV), F32)],
        grid=(bsz // bb, nblk),
        in_specs=in_specs,
        out_specs=out_specs,
        scratch_shapes=[pltpu.VMEM((bb, A_HEADS, A_DV, A_DK), F32)],
        compiler_params=pltpu.CompilerParams(
            dimension_semantics=("parallel", "arbitrary"),
            vmem_limit_bytes=_vmem_limit(blocks, state)),
        name="gla_mixer",
    )(proj3, proj3, proj3, proj3, proj3, gate_w_pad, gate_b_row, norm_g_row, s0)


def _gdn_kernel(x_ref, z_ref, sm_ref, w_ref, cp_ref, alog_ref, dtb_ref, ng_ref, s0_ref,
                o_ref, sout_ref, s_ref, tail_ref, u_ref, wm_ref, qg_ref, kg_ref, qk_ref, gl_ref,
                *, c, nchunks):
    t = pl.program_id(1)
    last_t = pl.num_programs(1) - 1
    tb = c * nchunks
    keep = SUBLANES - (B_CONV - 1)

    @pl.when(t == 0)
    def _():
        s_ref[...] = s0_ref[...]
        tail_ref[0:keep, :] = jnp.zeros((keep, B_QKV), F32)
        tail_ref[keep:SUBLANES, :] = cp_ref[...]

    def conv_silu(row0, first, cols):
        x = x_ref[pl.ds(row0, c), cols]
        if first is None:
            prev = x_ref[pl.ds(pl.multiple_of(row0 - SUBLANES, SUBLANES), SUBLANES), cols]
        else:
            before = pl.multiple_of(jnp.maximum(row0 - SUBLANES, 0), SUBLANES)
            prev = jnp.where(first, tail_ref[:, cols], x_ref[pl.ds(before, SUBLANES), cols])
        w = w_ref[:, cols]
        ext = jnp.concatenate([prev, x], axis=0)
        ext1 = pltpu.roll(ext, 1, 0)
        newer = ext * w[3:4, :] + ext1 * w[2:3, :]
        older = ext * w[1:2, :] + ext1 * w[0:1, :]
        conv = newer + pltpu.roll(older, 2, 0)
        return jax.nn.silu(conv[SUBLANES:SUBLANES + c, :])

    causal, strict = _causal_masks(c)
    tri = causal.astype(BF16)
    eye = jnp.where(causal & jnp.logical_not(strict), 1.0, 0.0).astype(F32)
    sel = _lane_selector(AB_APRE_LANE)
    neg_a_exp = -jnp.exp(alog_ref[...])
    dtb = dtb_ref[...]
    ng = ng_ref[...]
    n_double = int(math.log2(c)) - 1
    heads = range(B_HEADS)

    group = GDN_WY_CHUNKS if nchunks % GDN_WY_CHUNKS == 0 else 1

    def wy_factors(n, carry):
        pws, rhss, where = [], [], []
        for ci in range(group):
            row0 = pl.multiple_of((n * group + ci) * c, c)
            first = (n == 0) if ci == 0 else None
            sl = pl.ds(row0, c)
            sm = sm_ref[sl, :]
            g_cum = _cumsum_rows(tri, neg_a_exp * jax.nn.softplus(sm + dtb))
            g_rows = _select_rows(sel, g_cum)
            beta_all = jax.nn.sigmoid(sm)
            gl_ref[pl.ds(n * group + ci, 1), :] = g_cum[c - 1:c, :]
            for h in heads:
                hs = slice(h * B_DK, (h + 1) * B_DK)
                q = conv_silu(row0, first, hs)
                k = conv_silu(row0, first, slice(B_KW + h * B_DK, B_KW + (h + 1) * B_DK))
                v = conv_silu(row0, first, slice(2 * B_KW + h * B_DV, 2 * B_KW + (h + 1) * B_DV))
                q = q * lax.rsqrt(jnp.sum(q * q, axis=-1, keepdims=True) + NORM_EPS) * (B_DK ** -0.5)
                k = k * lax.rsqrt(jnp.sum(k * k, axis=-1, keepdims=True) + NORM_EPS)
                g_col = g_cum[:, AB_APRE_LANE + h:AB_APRE_LANE + h + 1]
                beta = beta_all[:, AB_BETA_LANE + h:AB_BETA_LANE + h + 1]
                decay = jnp.exp(jnp.where(causal, g_col - g_rows[h:h + 1, :], -jnp.inf))
                e_g = jnp.exp(g_col)
                k_beta = k * beta
                pws.append(-jnp.where(strict, _dot_nt(k_beta, k) * decay, 0.0))
                rhss.append(jnp.concatenate([v * beta, k_beta * e_g], axis=1))
                where.append((sl, hs))
                qk_ref[sl, h * LANES:h * LANES + c] = jnp.where(causal, _dot_nt(q, k) * decay, 0.0).astype(BF16)
                qg_ref[sl, hs] = (q * e_g).astype(BF16)
                kg_ref[sl, hs] = (k * jnp.exp(g_col[c - 1:c, :] - g_col)).astype(BF16)
        items = range(len(pws))
        neg_lower = pws
        invs = [eye + p for p in pws]
        for _ in range(n_double):
            pws = [_dot(p, p) for p in pws]
            invs = [i + _dot(i, p) for i, p in zip(invs, pws)]
        invs = [i.astype(BF16) for i in invs]
        sol = [_mm(invs[i], rhss[i].astype(BF16)) for i in items]
        resid = [rhss[i] - sol[i] + _dot3(neg_lower[i], sol[i]) for i in items]
        for i in items:
            sl, hs = where[i]
            uw = sol[i] + _mm(invs[i], resid[i].astype(BF16))
            u_ref[sl, hs] = uw[:, :B_DV]
            wm_ref[sl, hs] = uw[:, B_DV:].astype(BF16)
        return carry

    lax.fori_loop(0, nchunks // group, wy_factors, 0)
    tail_ref[...] = x_ref[tb - SUBLANES:tb, :]

    def recurrence(n, carry):
        sl = pl.ds(pl.multiple_of(n * c, c), c)
        e_last = jnp.exp(gl_ref[pl.ds(n, 1), :])
        hsl = [slice(h * B_DK, (h + 1) * B_DK) for h in heads]
        s_old = [s_ref[h] for h in heads]
        s_bf = [s.astype(BF16) for s in s_old]
        v_new = [(u_ref[sl, hsl[h]] - _mm(wm_ref[sl, hsl[h]], s_bf[h])).astype(BF16) for h in heads]
        outs = [_mm(qg_ref[sl, hsl[h]], s_bf[h]) + _mm(qk_ref[sl, h * LANES:h * LANES + c], v_new[h])
                for h in heads]
        for h in heads:
            s_ref[h] = (e_last[:, AB_APRE_LANE + h:AB_APRE_LANE + h + 1] * s_old[h]
                        + lax.dot_general(kg_ref[sl, hsl[h]], v_new[h], (((0,), (0,)), ((), ())),
                                          preferred_element_type=F32))
        for h in heads:
            o = outs[h]
            o = o * lax.rsqrt(jnp.mean(o * o, axis=-1, keepdims=True) + NORM_EPS) * ng
            o_ref[sl, hsl[h]] = (o * jax.nn.silu(z_ref[sl, hsl[h]])).astype(o_ref.dtype)
        return carry

    lax.fori_loop(0, nchunks, recurrence, 0)

    @pl.when(t == last_t)
    def _():
        sout_ref[...] = s_ref[...]


def _gdn(proj3, conv_prev, conv_w, alog_row, dtb_row, norm_g_row, s0, *, c, tb):
    bsz, l, _ = proj3.shape
    nblk = l // tb
    assert AB_QKV % B_QKV == 0 and AB_ZB % B_VW == 0
    in_specs = [
        pl.BlockSpec((None, tb, B_QKV), lambda b, t: (b, t, AB_QKV // B_QKV)),
        pl.BlockSpec((None, tb, B_VW), lambda b, t: (b, t, AB_ZB // B_VW)),
        pl.BlockSpec((None, tb, LANES), lambda b, t: (b, t, AB_SMALL // LANES)),
        pl.BlockSpec((B_CONV, B_QKV), lambda b, t: (0, 0)),
        pl.BlockSpec((None, B_CONV - 1, B_QKV), lambda b, t: (b, 0, 0)),
        pl.BlockSpec((1, LANES), lambda b, t: (0, 0)),
        pl.BlockSpec((1, LANES), lambda b, t: (0, 0)),
        pl.BlockSpec((1, B_DV), lambda b, t: (0, 0)),
        pl.BlockSpec((None, B_HEADS, B_DK, B_DV), lambda b, t: (b, 0, 0, 0)),
    ]
    out_specs = [
        pl.BlockSpec((None, tb, B_VW), lambda b, t: (b, t, 0)),
        pl.BlockSpec((None, B_HEADS, B_DK, B_DV), lambda b, t: (b, 0, 0, 0)),
    ]
    blocks = tb * (B_QKV + B_VW + LANES) * 4 + tb * B_VW * 2 + 2 * B_HEADS * B_DK * B_DV * 4
    scratch = (B_HEADS * B_DK * B_DV * 4 + SUBLANES * B_QKV * 4 + tb * B_VW * 4
               + 4 * tb * B_KW * 2 + SUBLANES * LANES * 4)
    return pl.pallas_call(
        functools.partial(_gdn_kernel, c=c, nchunks=tb // c),
        out_shape=[jax.ShapeDtypeStruct((bsz, l, B_VW), BF16),
                   jax.ShapeDtypeStruct((bsz, B_HEADS, B_DK, B_DV), F32)],
        grid=(bsz, nblk),
        in_specs=in_specs,
        out_specs=out_specs,
        scratch_shapes=[pltpu.VMEM((B_HEADS, B_DK, B_DV), F32),
                        pltpu.VMEM((SUBLANES, B_QKV), F32),
                        pltpu.VMEM((tb, B_VW), F32),
                        pltpu.VMEM((tb, B_KW), BF16),
                        pltpu.VMEM((tb, B_KW), BF16),
                        pltpu.VMEM((tb, B_KW), BF16),
                        pltpu.VMEM((tb, B_HEADS * LANES), BF16),
                        pltpu.VMEM((SUBLANES, LANES), F32)],
        compiler_params=pltpu.CompilerParams(
            dimension_semantics=("parallel", "arbitrary"),
            vmem_limit_bytes=_vmem_limit(blocks, scratch)),
        name="gdn_mixer",
    )(proj3, proj3, proj3, conv_w, conv_prev, alog_row, dtb_row, norm_g_row, s0)


def _mlstm_kernel(q_ref, k_ref, v_ref, og_ref, z_ref, sm_ref, ib_ref, fb_ref, ng_ref,
                  c0_ref, n0_ref, m0_ref, o_ref, cout_ref, nout_ref, mout_ref,
                  c_ref, n_ref, m_ref, *, c, nchunks):
    t = pl.program_id(1)
    last_t = pl.num_programs(1) - 1
    bb = q_ref.shape[0]
    chains = [(bi, h) for bi in range(bb) for h in range(D_HEADS)]
    ksl = [slice(h * D_DK, (h + 1) * D_DK) for _, h in chains]
    vsl = [slice(h * D_DV, (h + 1) * D_DV) for _, h in chains]
    idx = range(len(chains))

    @pl.when(t == 0)
    def _():
        c_ref[...] = c0_ref[...]
        n_ref[...] = n0_ref[...]
        m_ref[...] = m0_ref[...]

    causal, _ = _causal_masks(c)
    tri = causal.astype(BF16)
    sel = _lane_selector(0)
    lane = lax.broadcasted_iota(jnp.int32, (c, LANES), 1)
    ib = ib_ref[...]
    fb = fb_ref[...]

    def body(n, carry):
        sl = pl.ds(pl.multiple_of(n * c, c), c)
        sm = [sm_ref[bi, sl, :] for bi in range(bb)]
        i_full = [x + ib for x in sm]
        b_full = [_cumsum_rows(tri, jax.nn.log_sigmoid(x + fb)) for x in sm]
        rows = [_select_rows(sel, jnp.where(lane < CD_F_LANE, i_full[bi], b_full[bi]))
                for bi in range(bb)]
        b_col = [b_full[bi][:, CD_F_LANE + h:CD_F_LANE + h + 1] for bi, h in chains]
        i_col = [i_full[bi][:, CD_I_LANE + h:CD_I_LANE + h + 1] for bi, h in chains]
        logw = [jnp.where(causal, b_col[i] - rows[bi][CD_F_LANE + h:CD_F_LANE + h + 1, :]
                          + rows[bi][CD_I_LANE + h:CD_I_LANE + h + 1, :], -jnp.inf)
                for i, (bi, h) in enumerate(chains)]
        m_intra = [jnp.max(logw[i], axis=-1, keepdims=True) for i in idx]
        q = [q_ref[chains[i][0], sl, ksl[i]] * (D_DK ** -0.5) for i in idx]
        q_bf = [x.astype(BF16) for x in q]
        k = [k_ref[chains[i][0], sl, ksl[i]] for i in idx]
        v = [v_ref[chains[i][0], sl, vsl[i]].astype(BF16) for i in idx]
        p = [jnp.exp(logw[i] - m_intra[i]) * _dot_nt(q_bf[i], k[i]) for i in idx]
        h_intra = [_dot(p[i], v[i]) for i in idx]
        n_intra = [jnp.sum(p[i], axis=-1, keepdims=True) for i in idx]
        b_last = [b_col[i][c - 1:c, :] for i in idx]
        m_chunk = [m_intra[i][c - 1:c, :] for i in idx]
        k_w = [k[i] * jnp.exp(b_last[i] - b_col[i] + i_col[i] - m_chunk[i]) for i in idx]
        kv = [_dot_tn(k_w[i], v[i]) for i in idx]
        k_sum = [jnp.sum(k_w[i], axis=0, keepdims=True) for i in idx]
        c_s = [c_ref[bi, h] for bi, h in chains]
        n_s = [n_ref[bi, h:h + 1, :] for bi, h in chains]
        m_s = [m_ref[bi, :, h:h + 1] for bi, h in chains]
        qc = [_dot(q_bf[i], c_s[i]) for i in idx]
        for i, (bi, h) in enumerate(chains):
            m_new = jnp.maximum(b_last[i] + m_s[i], m_chunk[i])
            w_old = jnp.exp(b_last[i] + m_s[i] - m_new)
            w_new = jnp.exp(m_chunk[i] - m_new)
            c_ref[bi, h] = w_old * c_s[i] + w_new * kv[i]
            n_ref[bi, h:h + 1, :] = w_old * n_s[i] + w_new * k_sum[i]
            m_ref[bi, :, h:h + 1] = m_new
        for i, (bi, h) in enumerate(chains):
            a = b_col[i] + m_s[i]
            m_t = jnp.maximum(a, m_intra[i])
            w_a = jnp.exp(a - m_t)
            w_i = jnp.exp(m_intra[i] - m_t)
            num = w_a * qc[i] + w_i * h_intra[i]
            den = w_a * jnp.sum(q[i] * n_s[i], axis=-1, keepdims=True) + w_i * n_intra[i]
            hh = num / jnp.maximum(jnp.abs(den), jnp.exp(-m_t))
            hd = jax.nn.sigmoid(og_ref[bi, sl, vsl[i]]) * hh
            oc = hd - jnp.mean(hd, axis=-1, keepdims=True)
            o = oc * lax.rsqrt(jnp.mean(oc * oc, axis=-1, keepdims=True) + NORM_EPS) * ng_ref[:, vsl[i]]
            o_ref[bi, sl, vsl[i]] = (o * jax.nn.silu(z_ref[bi, sl, vsl[i]])).astype(o_ref.dtype)
        return carry

    lax.fori_loop(0, nchunks, body, 0)

    @pl.when(t == last_t)
    def _():
        cout_ref[...] = c_ref[...]
        nout_ref[...] = n_ref[...]
        mout_ref[...] = m_ref[...]


def _mlstm(proj3, ib_row, fb_row, norm_g_row, c0, n0, m0, *, c, tb, bb):
    bsz, l, _ = proj3.shape
    nblk = l // tb
    assert bsz % bb == 0
    tok = lambda col: (lambda b, t: (b, t, col))
    st4 = lambda b, t: (b, 0, 0, 0)
    st3 = lambda b, t: (b, 0, 0)
    in_specs = [
        pl.BlockSpec((bb, tb, D_KW), tok(CD_Q // D_KW)),
        pl.BlockSpec((bb, tb, D_KW), tok(CD_K // D_KW)),
        pl.BlockSpec((bb, tb, D_VW), tok(CD_V // D_VW)),
        pl.BlockSpec((bb, tb, D_VW), tok(CD_O // D_VW)),
        pl.BlockSpec((bb, tb, D_VW), tok(CD_ZD // D_VW)),
        pl.BlockSpec((bb, tb, LANES), tok(CD_SMALL // LANES)),
        pl.BlockSpec((1, LANES), lambda b, t: (0, 0)),
        pl.BlockSpec((1, LANES), lambda b, t: (0, 0)),
        pl.BlockSpec((1, D_VW), lambda b, t: (0, 0)),
        pl.BlockSpec((bb, D_HEADS, D_DK, D_DV), st4),
        pl.BlockSpec((bb, D_HEADS, D_DK), st3),
        pl.BlockSpec((bb, 1, D_HEADS), st3),
    ]
    out_specs = [
        pl.BlockSpec((bb, tb, D_VW), lambda b, t: (b, t, 0)),
        pl.BlockSpec((bb, D_HEADS, D_DK, D_DV), st4),
        pl.BlockSpec((bb, D_HEADS, D_DK), st3),
        pl.BlockSpec((bb, 1, D_HEADS), st3),
    ]
    state = bb * D_HEADS * D_DK * D_DV * 4
    blocks = bb * tb * (2 * D_KW + 3 * D_VW + LANES) * 4 + bb * tb * D_VW * 2 + 2 * state
    return pl.pallas_call(
        functools.partial(_mlstm_kernel, c=c, nchunks=tb // c),
        out_shape=[jax.ShapeDtypeStruct((bsz, l, D_VW), BF16),
                   jax.ShapeDtypeStruct((bsz, D_HEADS, D_DK, D_DV), F32),
                   jax.ShapeDtypeStruct((bsz, D_HEADS, D_DK), F32),
                   jax.ShapeDtypeStruct((bsz, 1, D_HEADS), F32)],
        grid=(bsz // bb, nblk),
        in_specs=in_specs,
        out_specs=out_specs,
        scratch_shapes=[pltpu.VMEM((bb, D_HEADS, D_DK, D_DV), F32),
                        pltpu.VMEM((bb, D_HEADS, D_DK), F32),
                        pltpu.VMEM((bb, 1, D_HEADS), F32)],
        compiler_params=pltpu.CompilerParams(
            dimension_semantics=("parallel", "arbitrary"),
            vmem_limit_bytes=_vmem_limit(blocks, state)),
        name="mlstm_mixer",
    )(proj3, proj3, proj3, proj3, proj3, proj3, ib_row, fb_row, norm_g_row, c0, n0, m0)


def _s5_expand_operators(kc_ref, bc_ref, cc_ref, bd_ref, bst_ref, cst_ref):
    tc, gt, sw, cg = S5_CHUNK, S5_GROUP_BLOCK, 2 * C_STATE, C_GROUP
    w_t = tc * LANES
    iota = lambda shape, d: lax.broadcasted_iota(jnp.int32, shape, d)
    row_g = (iota((w_t, LANES), 0) // cg) % gt
    tile16 = jnp.where(iota((cg, LANES), 1) % cg == iota((cg, LANES), 0), 1.0, 0.0).astype(BF16)
    bd = jnp.where(row_g == iota((w_t, LANES), 1) // cg, _mm(kc_ref[...], tile16), 0.0).astype(BF16)
    bd_ref[:, LANES:2 * LANES] = bd
    bd_ref[0:w_t - LANES, 0:LANES] = bd[LANES:, :]
    bd_ref[w_t - LANES:w_t, 0:LANES] = jnp.zeros((LANES, LANES), BF16)
    bc = bc_ref[...].astype(F32)
    for g in range(gt):
        bst_ref[:, g * sw:(g + 1) * sw] = jnp.where(row_g == g, bc, 0.0).astype(BF16)
    src, dst = iota((tc * cg, w_t), 0), iota((tc * cg, w_t), 1)
    spread = jnp.where((src // cg == dst // LANES) & (src % cg == dst % cg), 1.0, 0.0).astype(BF16)
    lane_g = (iota((sw, w_t), 1) // cg) % gt
    for g in range(gt):
        full = _mm(cc_ref[g * sw:(g + 1) * sw, :], spread)
        cst_ref[g * sw:(g + 1) * sw, :] = jnp.where(lane_g == g, full, 0.0).astype(BF16)


def _s5_kernel(u_ref, kc_ref, bc_ref, cc_ref, apow_ref, x0_ref, y_ref, xf_ref, bd_ref, bst_ref, cst_ref,
               *, nc, bb):
    m = nc * bb
    tc = S5_CHUNK
    sw = 2 * C_STATE

    @pl.when(pl.program_id(1) == 0)
    def _():
        _s5_expand_operators(kc_ref, bc_ref, cc_ref, bd_ref, bst_ref, cst_ref)

    row = lax.broadcasted_iota(jnp.int32, (m, sw), 0)
    n_idx = row & (nc - 1)
    n_log = int(math.log2(nc))

    def cmul(a1, a2, x):
        return a1 * x + a2 * pltpu.roll(x, C_STATE, 1)

    lhs = jnp.concatenate([u_ref[pl.ds(tau, m, stride=tc), :].astype(BF16) for tau in range(tc)], axis=1)
    e_all = _mm(lhs, bst_ref[...])
    x_start = []
    for g in range(S5_GROUP_BLOCK):
        gs = slice(g * sw, (g + 1) * sw)
        x0_rows = jnp.zeros((m, sw), F32)
        for b in range(bb):
            x0_rows = jnp.where(row == b * nc, x0_ref[b, :, gs], x0_rows)
        x = e_all[:, gs] + cmul(apow_ref[0:1, gs], apow_ref[1:2, gs], x0_rows)
        for j in range(n_log):
            sh = 1 << j
            shifted = jnp.where(n_idx >= sh, pltpu.roll(x, sh, 0), 0.0)
            x = x + cmul(apow_ref[2 * j:2 * j + 1, gs], apow_ref[2 * j + 1:2 * j + 2, gs], shifted)
        x_start.append(jnp.where(n_idx >= 1, pltpu.roll(x, 1, 0), x0_rows).astype(BF16))
        for b in range(bb):
            xf_ref[b, :, gs] = x[b * nc + nc - 1:b * nc + nc, :]
    y_state = _mm(jnp.concatenate(x_start, axis=1), cst_ref[...])
    for tau in range(0, tc, 2):
        width = (tau + 2) * LANES
        start = (tc - 2 - tau) * LANES
        y = _mm(lhs[:, :width], bd_ref[start:start + width, :]) + y_state[:, tau * LANES:(tau + 2) * LANES]
        y_ref[pl.ds(tau, m, stride=tc), :] = y[:, :LANES]
        y_ref[pl.ds(tau + 1, m, stride=tc), :] = y[:, LANES:]


def _s5_chunks(proj2d, kc, bc, cc, apow, x0, *, l, bb):
    tc = S5_CHUNK
    nc = l // tc
    bsz = x0.shape[1]
    nt = C_GROUPS // S5_GROUP_BLOCK
    rows = bb * l
    sw = 2 * C_STATE
    sw_t = S5_GROUP_BLOCK * sw
    w_t = tc * LANES
    assert CD_U % LANES == 0 and bsz % bb == 0
    blocks = (2 * rows * LANES * 4 + 2 * w_t * LANES * 2 + sw_t * tc * C_GROUP * 2 + apow.shape[1] * sw_t * 4
              + 2 * bb * SUBLANES * sw_t * 4)
    scratch = w_t * 2 * LANES * 2 + 2 * w_t * sw_t * 2
    temps = bb * nc * (w_t * 2 + w_t * 4 + 3 * sw_t * 4) + 4 * sw * w_t * 4
    return pl.pallas_call(
        functools.partial(_s5_kernel, nc=nc, bb=bb),
        out_shape=[jax.ShapeDtypeStruct((bsz * l, C_W), F32),
                   jax.ShapeDtypeStruct((nt, bsz, 1, sw_t), F32)],
        grid=(nt, bsz // bb),
        in_specs=[pl.BlockSpec((rows, LANES), lambda i, j: (j, CD_U // LANES + i)),
                  pl.BlockSpec((None, w_t, C_GROUP), lambda i, j: (i, 0, 0)),
                  pl.BlockSpec((None, w_t, sw), lambda i, j: (i, 0, 0)),
                  pl.BlockSpec((None, sw_t, tc * C_GROUP), lambda i, j: (i, 0, 0)),
                  pl.BlockSpec((None, apow.shape[1], sw_t), lambda i, j: (i, 0, 0)),
                  pl.BlockSpec((None, bb, 1, sw_t), lambda i, j: (i, j, 0, 0))],
        out_specs=[pl.BlockSpec((rows, LANES), lambda i, j: (j, i)),
                   pl.BlockSpec((None, bb, 1, sw_t), lambda i, j: (i, j, 0, 0))],
        scratch_shapes=[pltpu.VMEM((w_t, 2 * LANES), BF16),
                        pltpu.VMEM((w_t, sw_t), BF16),
                        pltpu.VMEM((sw_t, w_t), BF16)],
        compiler_params=pltpu.CompilerParams(
            dimension_semantics=("parallel", "arbitrary"),
            vmem_limit_bytes=_vmem_limit(blocks, scratch + temps)),
        name="s5_chunks",
    )(proj2d, kc, bc, cc, apow, x0)


def _s5_glu_kernel(y_ref, u_ref, z_ref, d_ref, w_ref, b_ref, o_ref):
    y = jax.nn.gelu(y_ref[...] + d_ref[...] * u_ref[...])
    gate = jax.nn.sigmoid(jnp.dot(y.astype(BF16), w_ref[...], preferred_element_type=F32) + b_ref[...])
    o_ref[...] = (y * gate * jax.nn.silu(z_ref[...])).astype(o_ref.dtype)


def _s5_glu(y2d, proj2d, d_row, glu_w, glu_b_row):
    m = y2d.shape[0]
    tm = min(m, 512)
    blocks = 3 * tm * C_W * 4 + C_W * C_W * 2 + tm * C_W * 2
    return pl.pallas_call(
        _s5_glu_kernel,
        out_shape=jax.ShapeDtypeStruct((m, C_W), BF16),
        grid=(m // tm,),
        in_specs=[pl.BlockSpec((tm, C_W), lambda i: (i, 0)),
                  pl.BlockSpec((tm, C_W), lambda i: (i, CD_U // C_W)),
                  pl.BlockSpec((tm, C_W), lambda i: (i, CD_Z // C_W)),
                  pl.BlockSpec((1, C_W), lambda i: (0, 0)),
                  pl.BlockSpec((C_W, C_W), lambda i: (0, 0)),
                  pl.BlockSpec((1, C_W), lambda i: (0, 0))],
        out_specs=pl.BlockSpec((tm, C_W), lambda i: (i, 0)),
        compiler_params=pltpu.CompilerParams(
            dimension_semantics=("parallel",),
            vmem_limit_bytes=_vmem_limit(blocks)),
        name="s5_glu",
    )(y2d, proj2d, proj2d, d_row, glu_w, glu_b_row)


def _s5_operators(lam_re, lam_im, log_dt, b_re, b_im, c_re, c_im, n_log):
    g, p = lam_re.shape
    tc = S5_CHUNK
    dt = jnp.exp(log_dt.astype(F32))[:, None]
    mag = jnp.exp(lam_re * dt)
    ab_re, ab_im = mag * jnp.cos(lam_im * dt), mag * jnp.sin(lam_im * dt)
    den = lam_re * lam_re + lam_im * lam_im
    er = ab_re - 1.0
    zr = (er * lam_re + ab_im * lam_im) / den
    zi = (ab_im * lam_re - er * lam_im) / den
    bb_re = zr[..., None] * b_re - zi[..., None] * b_im
    bb_im = zr[..., None] * b_im + zi[..., None] * b_re
    pr, pi = [jnp.ones_like(ab_re)], [jnp.zeros_like(ab_re)]
    for _ in range(tc):
        pr, pi = pr + [pr[-1] * ab_re - pi[-1] * ab_im], pi + [pr[-1] * ab_im + pi[-1] * ab_re]
    pw_re, pw_im = jnp.stack(pr), jnp.stack(pi)
    abr = pw_re[:tc, :, :, None] * bb_re - pw_im[:tc, :, :, None] * bb_im
    abi = pw_re[:tc, :, :, None] * bb_im + pw_im[:tc, :, :, None] * bb_re
    kern = (jnp.einsum('gjp,dgpi->dgji', c_re, abr, precision=HIGHEST)
            - jnp.einsum('gjp,dgpi->dgji', c_im, abi, precision=HIGHEST))
    gt = S5_GROUP_BLOCK
    nt = g // gt
    sw = 2 * p
    kc = kern[::-1].reshape(tc, nt, gt, C_GROUP, C_GROUP).transpose(1, 0, 2, 4, 3)
    kc = kc.reshape(nt, tc * gt * C_GROUP, C_GROUP)
    ab = jnp.concatenate([abr, abi], axis=2)[::-1]
    bc = ab.reshape(tc, nt, gt, sw, C_GROUP).transpose(1, 0, 2, 4, 3).reshape(nt, tc * gt * C_GROUP, sw)
    cr = c_re[None] * pw_re[1:, :, None, :] - c_im[None] * pw_im[1:, :, None, :]
    ci = -(c_re[None] * pw_im[1:, :, None, :] + c_im[None] * pw_re[1:, :, None, :])
    cc = jnp.concatenate([cr, ci], axis=3)
    cc = cc.reshape(tc, nt, gt, C_GROUP, sw).transpose(1, 2, 4, 0, 3).reshape(nt, gt * sw, tc * C_GROUP)
    r, i = pw_re[tc], pw_im[tc]
    rows = []
    for _ in range(max(n_log, 1)):
        rows += [jnp.concatenate([r, r], -1), jnp.concatenate([-i, i], -1)]
        r, i = r * r - i * i, 2.0 * r * i
    apow = jnp.stack(rows, axis=1)
    apow = apow.reshape(nt, gt, -1, sw).transpose(0, 2, 1, 3).reshape(nt, -1, gt * sw)
    return kc.astype(BF16), bc.astype(BF16), cc.astype(BF16), apow


def _s5(proj2d, ops, x0_re, x0_im, *, l):
    kc, bc, cc, apow = ops
    bsz = x0_re.shape[0]
    nt = C_GROUPS // S5_GROUP_BLOCK
    x0 = jnp.concatenate([x0_re, x0_im], axis=-1).reshape(bsz, nt, 1, -1).transpose(1, 0, 2, 3)
    bb = bsz if bsz * l <= 4096 else 1
    y, xf = _s5_chunks(proj2d, kc, bc, cc, apow, x0, l=l, bb=bb)
    xf = xf.transpose(1, 0, 2, 3).reshape(bsz, C_GROUPS, 2 * C_STATE)
    return y, xf[..., :C_STATE], xf[..., C_STATE:]


AB_SRC_GA = AB_QKV
AB_SRC_QKV = AB_SRC_GA + A_GATE_RANK
AB_SRC_TAIL = AB_SRC_QKV + B_QKV + B_VW
IN_AB = AB_SRC_TAIL + 2 * B_HEADS
IN_CD = CD_SMALL + 2 * D_HEADS
WPREP_AB_TILE = 512
WPREP_CD_TILE = 640


def _prep_w_ab_kernel(wt_ref, ga_ref, o_ref):
    j = pl.program_id(0)
    tile = o_ref.shape[1]
    n_small = A_GATE_RANK + 2 * B_HEADS

    @pl.when(j < AB_SMALL // tile)
    def _():
        o_ref[...] = wt_ref[...].T.astype(BF16)

    @pl.when(j == AB_SMALL // tile)
    def _():
        rows = jnp.concatenate([ga_ref[...], wt_ref[tile - 2 * B_HEADS:tile, :],
                                jnp.zeros((tile - n_small, wt_ref.shape[1]), F32)], axis=0)
        o_ref[...] = rows.T.astype(BF16)

    @pl.when(j > AB_SMALL // tile)
    def _():
        o_ref[...] = jnp.zeros(o_ref.shape, BF16)


def _prep_w_ab(w_t, n_out):
    n_in, d = w_t.shape
    tile = WPREP_AB_TILE
    assert AB_QKV % tile == 0 and AB_SMALL % tile == 0 and n_out % tile == 0 and n_in >= tile

    unit = A_GATE_RANK
    assert tile % unit == 0 and (AB_SRC_QKV - AB_QKV) % unit == 0 and (n_in - tile) % unit == 0

    def src_row(j):
        k = j * (tile // unit)
        k = jnp.where(j < AB_QKV // tile, k,
                      jnp.where(j < AB_SMALL // tile, k + (AB_SRC_QKV - AB_QKV) // unit, (n_in - tile) // unit))
        return unit * k

    return pl.pallas_call(
        _prep_w_ab_kernel,
        out_shape=jax.ShapeDtypeStruct((d, n_out), BF16),
        grid=(n_out // tile,),
        in_specs=[pl.BlockSpec((pl.Element(tile), pl.Element(d)), lambda j: (src_row(j), 0)),
                  pl.BlockSpec((pl.Element(A_GATE_RANK), pl.Element(d)), lambda j: (AB_SRC_GA, 0))],
        out_specs=pl.BlockSpec((d, tile), lambda j: (0, j)),
        compiler_params=pltpu.CompilerParams(
            dimension_semantics=("parallel",),
            vmem_limit_bytes=_vmem_limit(tile * d * 6, 2 * tile * d * 4)),
        name="prep_w_in_ab",
    )(w_t, w_t)


def _prep_w_cd_kernel(wt_ref, o_ref):
    tile = o_ref.shape[1]
    row = pl.program_id(0) * tile + lax.broadcasted_iota(jnp.int32, wt_ref.shape, 0)
    o_ref[...] = jnp.where(row < IN_CD, wt_ref[...], 0.0).T.astype(BF16)


def _prep_w_cd(w_t, n_out):
    n_in, d = w_t.shape
    tile = WPREP_CD_TILE
    assert n_out % tile == 0
    return pl.pallas_call(
        _prep_w_cd_kernel,
        out_shape=jax.ShapeDtypeStruct((d, n_out), BF16),
        grid=(n_out // tile,),
        in_specs=[pl.BlockSpec((tile, d), lambda j: (j, 0))],
        out_specs=pl.BlockSpec((d, tile), lambda j: (0, j)),
        compiler_params=pltpu.CompilerParams(
            dimension_semantics=("parallel",),
            vmem_limit_bytes=_vmem_limit(tile * d * 6, 2 * tile * d * 4)),
        name="prep_w_in_cd",
    )(w_t)


def _lane_row(vals, lane0):
    return jnp.zeros((1, LANES), F32).at[0, lane0:lane0 + vals.shape[0]].set(vals.astype(F32))


def _prepare_weights(norm_g, final_norm_g, w_in_ab, a_gate_w, a_gate_b, a_norm_g, b_conv_w, b_a_log,
                     b_dt_bias, b_norm_g, w_out_ab, w_in_cd, c_lam_re, c_lam_im, c_log_dt, c_b_re,
                     c_b_im, c_c_re, c_c_im, c_d, c_glu_w, c_glu_b, d_i_bias, d_f_bias, d_norm_g,
                     w_out_cd, n_log):
    assert w_in_ab.shape[1] == IN_AB and w_in_cd.shape[1] == IN_CD
    w_ab = w_in_ab.astype(F32).T
    w_cd = w_in_cd.astype(F32).T
    gate_w = jnp.zeros((LANES, A_KW), F32).at[AB_GA_LANE:AB_GA_LANE + A_GATE_RANK].set(
        a_gate_w.astype(F32)).astype(BF16)
    return dict(
        norm_g=norm_g.astype(F32), final_g=final_norm_g.astype(F32)[None, :],
        w_ab=w_ab, w_cd=w_cd, gate_w=gate_w, gate_b=a_gate_b.astype(F32)[None, :],
        a_norm_g=a_norm_g.astype(F32)[None, :], conv_w=b_conv_w.astype(F32),
        alog=_lane_row(b_a_log, AB_APRE_LANE), dtb=_lane_row(b_dt_bias, AB_APRE_LANE),
        b_norm_g=b_norm_g.astype(F32)[None, :],
        w_out_a=w_out_ab[:A_VW].astype(BF16), w_out_b=w_out_ab[A_VW:].astype(BF16),
        s5_ops=_s5_operators(c_lam_re.astype(F32), c_lam_im.astype(F32), c_log_dt, c_b_re.astype(F32),
                             c_b_im.astype(F32), c_c_re.astype(F32), c_c_im.astype(F32), n_log),
        c_d=c_d.astype(F32).reshape(1, C_W), glu_w=c_glu_w.astype(BF16),
        glu_b=c_glu_b.astype(F32)[None, :],
        ib=_lane_row(d_i_bias, CD_I_LANE), fb=_lane_row(d_f_bias, CD_F_LANE),
        d_norm_g=d_norm_g.astype(F32)[None, :],
        w_out_c=w_out_cd[:C_W].astype(BF16), w_out_d=w_out_cd[C_W:].astype(BF16),
    )


def _trunk(x, conv_prev, s_gla0, s_gdn0, s5_re0, s5_im0, mc0, mn0, mm0, w):
    bsz, l, d = x.shape
    c = min(CHUNK, l)
    tb = min(l, 8 * c)
    assert l % tb == 0 and l % S5_CHUNK == 0
    x2d = x.reshape(bsz * l, d)

    proj = _norm_matmul(x2d, w['norm_g'][0:1], w['w_ab'], n_aligned=AB_SMALL, shift_from=AB_QKV,
                        shift=AB_SRC_QKV - AB_QKV, head_row=AB_SRC_GA, head_rows=A_GATE_RANK,
                        tail_rows=2 * B_HEADS)
    proj3 = proj.reshape(bsz, l, proj.shape[1])
    bb = MIXER_STREAMS if bsz % MIXER_STREAMS == 0 else 1
    tb_s = min(l, (8 // bb) * c)
    o_a, s_gla = _gla(proj3, w['gate_w'], w['gate_b'], w['a_norm_g'], s_gla0.astype(F32), c=c, tb=tb_s, bb=bb)
    o_b, s_gdn = _gdn(proj3, conv_prev.astype(F32), w['conv_w'], w['alog'], w['dtb'], w['b_norm_g'],
                      s_gdn0.astype(F32), c=c, tb=tb)
    conv_new = proj3[:, l - (B_CONV - 1):, AB_QKV:AB_QKV + B_QKV]
    h1 = _out_proj(o_a.reshape(bsz * l, A_VW), o_b.reshape(bsz * l, B_VW), w['w_out_a'], w['w_out_b'], x2d)

    proj = _norm_matmul(h1, w['norm_g'][1:2], w['w_cd'], n_aligned=CD_SMALL, shift_from=CD_SMALL, shift=0,
                        head_row=0, head_rows=0, tail_rows=2 * D_HEADS)
    proj3 = proj.reshape(bsz, l, proj.shape[1])
    y, s5_re, s5_im = _s5(proj, w['s5_ops'], s5_re0.astype(F32), s5_im0.astype(F32), l=l)
    o_c = _s5_glu(y, proj, w['c_d'], w['glu_w'], w['glu_b'])
    o_d, mc, mn, mm = _mlstm(proj3, w['ib'], w['fb'], w['d_norm_g'], mc0.astype(F32),
                             mn0.astype(F32), mm0.astype(F32)[:, None, :], c=c, tb=tb_s, bb=bb)
    y_out = _out_proj(o_c, o_d.reshape(bsz * l, D_VW), w['w_out_c'], w['w_out_d'], h1, w['final_g'])
    dt = x.dtype
    return (y_out.reshape(bsz, l, d).astype(dt), conv_new.astype(dt), s_gla.astype(dt), s_gdn.astype(dt),
            s5_re.astype(dt), s5_im.astype(dt), mc.astype(dt), mn.astype(dt), mm[:, 0, :].astype(dt))


def kernel(x_prompt, x_sample, cache_gdn_conv, state_gla, state_gdn, state_s5_re, state_s5_im,
           state_mlstm_c, state_mlstm_n, state_mlstm_m, norm_g, final_norm_g, w_in_ab, a_gate_w,
           a_gate_b, a_norm_g, b_conv_w, b_a_log, b_dt_bias, b_norm_g, w_out_ab, w_in_cd, c_lam_re,
           c_lam_im, c_log_dt, c_b_re, c_b_im, c_c_re, c_c_im, c_d, c_glu_w, c_glu_b, d_i_bias,
           d_f_bias, d_norm_g, w_out_cd):
    n_log = int(math.log2(max(x_prompt.shape[1], x_sample.shape[1]) // S5_CHUNK))
    w = _prepare_weights(norm_g, final_norm_g, w_in_ab, a_gate_w, a_gate_b, a_norm_g, b_conv_w, b_a_log,
                         b_dt_bias, b_norm_g, w_out_ab, w_in_cd, c_lam_re, c_lam_im, c_log_dt, c_b_re,
                         c_b_im, c_c_re, c_c_im, c_d, c_glu_w, c_glu_b, d_i_bias, d_f_bias, d_norm_g,
                         w_out_cd, n_log)
    nb = x_prompt.shape[0]
    zeros = lambda *shape: jnp.zeros(shape, F32)
    p_out = _trunk(x_prompt, zeros(nb, B_CONV - 1, B_QKV), zeros(nb, A_HEADS, A_DK, A_DV),
                   zeros(nb, B_HEADS, B_DK, B_DV), zeros(nb, C_GROUPS, C_STATE), zeros(nb, C_GROUPS, C_STATE),
                   zeros(nb, D_HEADS, D_DK, D_DV), zeros(nb, D_HEADS, D_DK), zeros(nb, D_HEADS), w)
    s_out = _trunk(x_sample, cache_gdn_conv, state_gla, state_gdn, state_s5_re, state_s5_im,
                   state_mlstm_c, state_mlstm_n, state_mlstm_m, w)
    return (p_out[0], s_out[0]) + tuple(p_out[1:]) + tuple(s_out[1:])
```

```python
import functools
import math

import jax
import jax.numpy as jnp
from jax import lax
from jax.experimental import pallas as pl
from jax.experimental.pallas import tpu as pltpu

F32 = jnp.float32
BF16 = jnp.bfloat16
HIGHEST = lax.Precision.HIGHEST

NORM_EPS = 1e-6
CHUNK = 64
A_HEADS, A_DK, A_DV, A_GATE_RANK, A_GATE_TAU = 4, 128, 256, 16, 16.0
B_HEADS, B_DK, B_DV, B_CONV = 8, 128, 128, 4
C_GROUP, C_GROUPS, C_STATE = 16, 64, 64
D_HEADS, D_DK, D_DV = 4, 128, 256
A_KW, A_VW = A_HEADS * A_DK, A_HEADS * A_DV
B_KW, B_VW = B_HEADS * B_DK, B_HEADS * B_DV
B_QKV = 2 * B_KW + B_VW
C_W = C_GROUPS * C_GROUP
D_KW, D_VW = D_HEADS * D_DK, D_HEADS * D_DV

LANES = 128
SUBLANES = 8
VMEM_BYTES_V7X = 64 * 1024 * 1024

AB_Q, AB_K, AB_V, AB_Z = 0, A_KW, 2 * A_KW, 2 * A_KW + A_VW
AB_QKV = AB_Z + A_VW
AB_ZB = AB_QKV + B_QKV
AB_SMALL = AB_ZB + B_VW
AB_GA_LANE, AB_BETA_LANE, AB_APRE_LANE = 0, A_GATE_RANK, A_GATE_RANK + B_HEADS
CD_U, CD_Z = 0, C_W
CD_Q = 2 * C_W
CD_K = CD_Q + D_KW
CD_V = CD_K + D_KW
CD_O = CD_V + D_VW
CD_ZD = CD_O + D_VW
CD_SMALL = CD_ZD + D_VW
CD_I_LANE, CD_F_LANE = 0, D_HEADS

PROJ_TN = 1280
S5_CHUNK = 16
S5_GROUP_BLOCK = 8
MIXER_STREAMS = 2
MLSTM_INTRA_CHUNKS = 4
GDN_WY_CHUNKS = 4


def _round_up(x, m):
    return (x + m - 1) // m * m


def _vmem_limit(block_bytes, scratch_bytes=0):
    est = 2 * block_bytes + scratch_bytes
    return int(min(max(2 * est, 32 * 1024 * 1024), VMEM_BYTES_V7X - 8 * 1024 * 1024))


def _mm(a, b):
    return jnp.dot(a, b, preferred_element_type=F32)


def _dot(a, b):
    return _mm(a.astype(BF16), b.astype(BF16))


def _dot_nt(a, b):
    return lax.dot_general(a.astype(BF16), b.astype(BF16), (((1,), (1,)), ((), ())),
                           preferred_element_type=F32)


def _dot_tn(a, b):
    return lax.dot_general(a.astype(BF16), b.astype(BF16), (((0,), (0,)), ((), ())),
                           preferred_element_type=F32)


def _split2(x):
    hi = x.astype(BF16)
    return hi, (x - hi.astype(F32)).astype(BF16)


def _split3(x):
    hi = x.astype(BF16)
    r = x - hi.astype(F32)
    mid = r.astype(BF16)
    return hi, mid, (r - mid.astype(F32)).astype(BF16)


def _dot3(a, b):
    ah, al = _split2(a)
    bh, bl = _split2(b)
    return _mm(ah, bh) + _mm(ah, bl) + _mm(al, bh)


def _cumsum_rows(tri_bf16, x):
    hi, mid, lo = _split3(x)
    return _mm(tri_bf16, hi) + _mm(tri_bf16, mid) + _mm(tri_bf16, lo)


def _select_rows(sel_bf16, x):
    nt = lambda b: lax.dot_general(sel_bf16, b, (((1,), (1,)), ((), ())), preferred_element_type=F32)
    hi, mid, lo = _split3(x)
    return nt(hi) + nt(mid) + nt(lo)


def _lane_selector(lane0):
    r = lax.broadcasted_iota(jnp.int32, (SUBLANES, LANES), 0)
    l = lax.broadcasted_iota(jnp.int32, (SUBLANES, LANES), 1)
    return jnp.where(l == r + lane0, 1.0, 0.0).astype(BF16)


def _causal_masks(c):
    row = lax.broadcasted_iota(jnp.int32, (c, c), 0)
    col = lax.broadcasted_iota(jnp.int32, (c, c), 1)
    return row >= col, row > col


def _norm_matmul_kernel(x_ref, g_ref, w_ref, o_ref, xn_ref):
    @pl.when(pl.program_id(1) == 0)
    def _():
        x = x_ref[...]
        y = x * lax.rsqrt(jnp.mean(x * x, axis=-1, keepdims=True) + NORM_EPS) * g_ref[...]
        xn_ref[...] = y.astype(BF16)

    o_ref[...] = jnp.dot(xn_ref[...], w_ref[...], preferred_element_type=F32)


def _norm_matmul(x2d, g_row, w_bf16):
    m, d = x2d.shape
    n = w_bf16.shape[1]
    tm = min(m, 1024)
    tn = PROJ_TN
    assert m % tm == 0 and n % tn == 0
    blocks = tm * d * 4 + d * tn * 2 + tm * tn * 4
    return pl.pallas_call(
        _norm_matmul_kernel,
        out_shape=jax.ShapeDtypeStruct((m, n), F32),
        grid=(m // tm, n // tn),
        in_specs=[pl.BlockSpec((tm, d), lambda i, j: (i, 0)),
                  pl.BlockSpec((1, d), lambda i, j: (0, 0)),
                  pl.BlockSpec((d, tn), lambda i, j: (0, j))],
        out_specs=pl.BlockSpec((tm, tn), lambda i, j: (i, j)),
        scratch_shapes=[pltpu.VMEM((tm, d), BF16)],
        compiler_params=pltpu.CompilerParams(
            dimension_semantics=("parallel", "arbitrary"),
            vmem_limit_bytes=_vmem_limit(blocks, tm * d * 2)),
        name="norm_in_proj",
    )(x2d, g_row, w_bf16)


def _out_proj_kernel(a_ref, b_ref, wa_ref, wb_ref, h_ref, o_ref):
    out = (jnp.dot(a_ref[...], wa_ref[...], preferred_element_type=F32)
           + jnp.dot(b_ref[...], wb_ref[...], preferred_element_type=F32))
    o_ref[...] = h_ref[...] + out


def _out_proj_norm_kernel(a_ref, b_ref, wa_ref, wb_ref, h_ref, g_ref, o_ref):
    out = (jnp.dot(a_ref[...], wa_ref[...], preferred_element_type=F32)
           + jnp.dot(b_ref[...], wb_ref[...], preferred_element_type=F32))
    h = h_ref[...] + out
    o_ref[...] = h * lax.rsqrt(jnp.mean(h * h, axis=-1, keepdims=True) + NORM_EPS) * g_ref[...]


def _out_proj(mix_a, mix_b, w_a, w_b, h2d, final_g_row=None):
    m, d = h2d.shape
    ka, kb = mix_a.shape[1], mix_b.shape[1]
    tm = min(m, 512)
    assert m % tm == 0
    in_specs = [pl.BlockSpec((tm, ka), lambda i: (i, 0)),
                pl.BlockSpec((tm, kb), lambda i: (i, 0)),
                pl.BlockSpec((ka, d), lambda i: (0, 0)),
                pl.BlockSpec((kb, d), lambda i: (0, 0)),
                pl.BlockSpec((tm, d), lambda i: (i, 0))]
    args = [mix_a, mix_b, w_a, w_b, h2d]
    kernel = _out_proj_kernel
    if final_g_row is not None:
        in_specs.append(pl.BlockSpec((1, d), lambda i: (0, 0)))
        args.append(final_g_row)
        kernel = _out_proj_norm_kernel
    blocks = tm * (ka + kb) * 2 + (ka + kb) * d * 2 + 2 * tm * d * 4
    return pl.pallas_call(
        kernel,
        out_shape=jax.ShapeDtypeStruct((m, d), F32),
        grid=(m // tm,),
        in_specs=in_specs,
        out_specs=pl.BlockSpec((tm, d), lambda i: (i, 0)),
        compiler_params=pltpu.CompilerParams(
            dimension_semantics=("parallel",),
            vmem_limit_bytes=_vmem_limit(blocks)),
        name="out_proj_norm" if final_g_row is not None else "out_proj",
    )(*args)


def _gla_kernel(q_ref, k_ref, v_ref, z_ref, sm_ref, gw_ref, gb_ref, ng_ref, s0_ref,
                o_ref, sout_ref, st_ref, *, c, nchunks):
    t = pl.program_id(1)
    last_t = pl.num_programs(1) - 1
    bb = q_ref.shape[0]
    chains = [(bi, h) for bi in range(bb) for h in range(A_HEADS)]
    ksl = [slice(h * A_DK, (h + 1) * A_DK) for _, h in chains]
    vsl = [slice(h * A_DV, (h + 1) * A_DV) for _, h in chains]
    idx = range(len(chains))

    @pl.when(t == 0)
    def _():
        for bi, h in chains:
            st_ref[bi, h] = s0_ref[bi, h].T

    causal, _ = _causal_masks(c)
    tri = causal.astype(BF16)
    gw = gw_ref[...]
    gb = gb_ref[...]

    def body(n, carry):
        sl = pl.ds(pl.multiple_of(n * c, c), c)
        b_all = [_cumsum_rows(tri, jax.nn.log_sigmoid(_dot(sm_ref[bi, sl, :], gw) + gb) * (1.0 / A_GATE_TAU))
                 for bi in range(bb)]
        b = [b_all[chains[i][0]][:, ksl[i]] for i in idx]
        b_last = [b[i][c - 1:c, :] for i in idx]
        k = [k_ref[chains[i][0], sl, ksl[i]] for i in idx]
        v = [v_ref[chains[i][0], sl, vsl[i]].astype(BF16) for i in idx]
        q_dec = [(q_ref[chains[i][0], sl, ksl[i]] * (A_DK ** -0.5) * jnp.exp(b[i])).astype(BF16) for i in idx]
        k_dec = [(k[i] * jnp.exp(-b[i])).astype(BF16) for i in idx]
        k_w = [(k[i] * jnp.exp(b_last[i] - b[i])).astype(BF16) for i in idx]
        scores = [jnp.where(causal, _dot_nt(q_dec[i], k_dec[i]), 0.0).astype(BF16) for i in idx]
        s_t = [st_ref[bi, h] for bi, h in chains]
        outs = [_mm(scores[i], v[i]) + _dot_nt(q_dec[i], s_t[i]) for i in idx]
        for i, (bi, h) in enumerate(chains):
            st_ref[bi, h] = s_t[i] * jnp.exp(b_last[i]) + _dot_tn(v[i], k_w[i])
        for i, (bi, h) in enumerate(chains):
            o = outs[i]
            o = o * lax.rsqrt(jnp.mean(o * o, axis=-1, keepdims=True) + NORM_EPS) * ng_ref[:, vsl[i]]
            o_ref[bi, sl, vsl[i]] = (o * jax.nn.silu(z_ref[bi, sl, vsl[i]])).astype(o_ref.dtype)
        return carry

    lax.fori_loop(0, nchunks, body, 0)

    @pl.when(t == last_t)
    def _():
        for bi, h in chains:
            sout_ref[bi, h] = st_ref[bi, h].T


def _gla(proj3, gate_w_pad, gate_b_row, norm_g_row, s0, *, c, tb, bb):
    bsz, l, _ = proj3.shape
    nblk = l // tb
    assert bsz % bb == 0
    tok = lambda col: (lambda b, t: (b, t, col))
    in_specs = [
        pl.BlockSpec((bb, tb, A_KW), tok(AB_Q // A_KW)),
        pl.BlockSpec((bb, tb, A_KW), tok(AB_K // A_KW)),
        pl.BlockSpec((bb, tb, A_VW), tok(AB_V // A_VW)),
        pl.BlockSpec((bb, tb, A_VW), tok(AB_Z // A_VW)),
        pl.BlockSpec((bb, tb, LANES), tok(AB_SMALL // LANES)),
        pl.BlockSpec((LANES, A_KW), lambda b, t: (0, 0)),
        pl.BlockSpec((1, A_KW), lambda b, t: (0, 0)),
        pl.BlockSpec((1, A_VW), lambda b, t: (0, 0)),
        pl.BlockSpec((bb, A_HEADS, A_DK, A_DV), lambda b, t: (b, 0, 0, 0)),
    ]
    out_specs = [
        pl.BlockSpec((bb, tb, A_VW), lambda b, t: (b, t, 0)),
        pl.BlockSpec((bb, A_HEADS, A_DK, A_DV), lambda b, t: (b, 0, 0, 0)),
    ]
    state = bb * A_HEADS * A_DK * A_DV * 4
    blocks = bb * tb * (2 * A_KW + 2 * A_VW + LANES) * 4 + bb * tb * A_VW * 2 + 2 * state
    return pl.pallas_call(
        functools.partial(_gla_kernel, c=c, nchunks=tb // c),
        out_shape=[jax.ShapeDtypeStruct((bsz, l, A_VW), BF16),
                   jax.ShapeDtypeStruct((bsz, A_HEADS, A_DK, A_DV), F32)],
        grid=(bsz // bb, nblk),
        in_specs=in_specs,
        out_specs=out_specs,
        scratch_shapes=[pltpu.VMEM((bb, A_HEADS, A_DV, A_DK), F32)],
        compiler_params=pltpu.CompilerParams(
            dimension_semantics=("parallel", "arbitrary"),
            vmem_limit_bytes=_vmem_limit(blocks, state)),
        name="gla_mixer",
    )(proj3, proj3, proj3, proj3, proj3, gate_w_pad, gate_b_row, norm_g_row, s0)


def _gdn_kernel(x_ref, z_ref, sm_ref, w_ref, cp_ref, alog_ref, dtb_ref, ng_ref, s0_ref,
                o_ref, sout_ref, s_ref, tail_ref, u_ref, wm_ref, qg_ref, kg_ref, qk_ref, gl_ref,
                *, c, nchunks):
    t = pl.program_id(1)
    last_t = pl.num_programs(1) - 1
    tb = c * nchunks
    keep = SUBLANES - (B_CONV - 1)

    @pl.when(t == 0)
    def _():
        s_ref[...] = s0_ref[...]
        tail_ref[0:keep, :] = jnp.zeros((keep, B_QKV), F32)
        tail_ref[keep:SUBLANES, :] = cp_ref[...]

    def conv_silu(row0, first, cols):
        x = x_ref[pl.ds(row0, c), cols]
        if first is None:
            prev = x_ref[pl.ds(pl.multiple_of(row0 - SUBLANES, SUBLANES), SUBLANES), cols]
        else:
            before = pl.multiple_of(jnp.maximum(row0 - SUBLANES, 0), SUBLANES)
            prev = jnp.where(first, tail_ref[:, cols], x_ref[pl.ds(before, SUBLANES), cols])
        w = w_ref[:, cols]
        ext = jnp.concatenate([prev, x], axis=0)
        ext1 = pltpu.roll(ext, 1, 0)
        newer = ext * w[3:4, :] + ext1 * w[2:3, :]
        older = ext * w[1:2, :] + ext1 * w[0:1, :]
        conv = newer + pltpu.roll(older, 2, 0)
        return jax.nn.silu(conv[SUBLANES:SUBLANES + c, :])

    causal, strict = _causal_masks(c)
    tri = causal.astype(BF16)
    eye = jnp.where(causal & jnp.logical_not(strict), 1.0, 0.0).astype(F32)
    sel = _lane_selector(AB_APRE_LANE)
    neg_a_exp = -jnp.exp(alog_ref[...])
    dtb = dtb_ref[...]
    ng = ng_ref[...]
    n_double = int(math.log2(c)) - 1
    heads = range(B_HEADS)

    group = GDN_WY_CHUNKS if nchunks % GDN_WY_CHUNKS == 0 else 1

    def wy_factors(n, carry):
        pws, rhss, where = [], [], []
        for ci in range(group):
            row0 = pl.multiple_of((n * group + ci) * c, c)
            first = (n == 0) if ci == 0 else None
            sl = pl.ds(row0, c)
            sm = sm_ref[sl, :]
            g_cum = _cumsum_rows(tri, neg_a_exp * jax.nn.softplus(sm + dtb))
            g_rows = _select_rows(sel, g_cum)
            beta_all = jax.nn.sigmoid(sm)
            gl_ref[pl.ds(n * group + ci, 1), :] = g_cum[c - 1:c, :]
            for h in heads:
                hs = slice(h * B_DK, (h + 1) * B_DK)
                q = conv_silu(row0, first, hs)
                k = conv_silu(row0, first, slice(B_KW + h * B_DK, B_KW + (h + 1) * B_DK))
                v = conv_silu(row0, first, slice(2 * B_KW + h * B_DV, 2 * B_KW + (h + 1) * B_DV))
                q = q * lax.rsqrt(jnp.sum(q * q, axis=-1, keepdims=True) + NORM_EPS) * (B_DK ** -0.5)
                k = k * lax.rsqrt(jnp.sum(k * k, axis=-1, keepdims=True) + NORM_EPS)
                g_col = g_cum[:, AB_APRE_LANE + h:AB_APRE_LANE + h + 1]
                beta = beta_all[:, AB_BETA_LANE + h:AB_BETA_LANE + h + 1]
                decay = jnp.exp(jnp.where(causal, g_col - g_rows[h:h + 1, :], -jnp.inf))
                e_g = jnp.exp(g_col)
                k_beta = k * beta
                pws.append(-jnp.where(strict, _dot_nt(k_beta, k) * decay, 0.0))
                rhss.append(jnp.concatenate([v * beta, k_beta * e_g], axis=1))
                where.append((sl, hs))
                qk_ref[sl, h * LANES:h * LANES + c] = jnp.where(causal, _dot_nt(q, k) * decay, 0.0).astype(BF16)
                qg_ref[sl, hs] = (q * e_g).astype(BF16)
                kg_ref[sl, hs] = (k * jnp.exp(g_col[c - 1:c, :] - g_col)).astype(BF16)
        items = range(len(pws))
        neg_lower = pws
        invs = [eye + p for p in pws]
        for _ in range(n_double):
            pws = [_dot(p, p) for p in pws]
            invs = [i + _dot(i, p) for i, p in zip(invs, pws)]
        invs = [i.astype(BF16) for i in invs]
        sol = [_mm(invs[i], rhss[i].astype(BF16)) for i in items]
        resid = [rhss[i] - sol[i] + _dot3(neg_lower[i], sol[i]) for i in items]
        for i in items:
            sl, hs = where[i]
            uw = sol[i] + _mm(invs[i], resid[i].astype(BF16))
            u_ref[sl, hs] = uw[:, :B_DV]
            wm_ref[sl, hs] = uw[:, B_DV:].astype(BF16)
        return carry

    lax.fori_loop(0, nchunks // group, wy_factors, 0)
    tail_ref[...] = x_ref[tb - SUBLANES:tb, :]

    def recurrence(n, carry):
        sl = pl.ds(pl.multiple_of(n * c, c), c)
        e_last = jnp.exp(gl_ref[pl.ds(n, 1), :])
        hsl = [slice(h * B_DK, (h + 1) * B_DK) for h in heads]
        s_old = [s_ref[h] for h in heads]
        s_bf = [s.astype(BF16) for s in s_old]
        v_new = [(u_ref[sl, hsl[h]] - _mm(wm_ref[sl, hsl[h]], s_bf[h])).astype(BF16) for h in heads]
        outs = [_mm(qg_ref[sl, hsl[h]], s_bf[h]) + _mm(qk_ref[sl, h * LANES:h * LANES + c], v_new[h])
                for h in heads]
        for h in heads:
            s_ref[h] = (e_last[:, AB_APRE_LANE + h:AB_APRE_LANE + h + 1] * s_old[h]
                        + lax.dot_general(kg_ref[sl, hsl[h]], v_new[h], (((0,), (0,)), ((), ())),
                                          preferred_element_type=F32))
        for h in heads:
            o = outs[h]
            o = o * lax.rsqrt(jnp.mean(o * o, axis=-1, keepdims=True) + NORM_EPS) * ng
            o_ref[sl, hsl[h]] = (o * jax.nn.silu(z_ref[sl, hsl[h]])).astype(o_ref.dtype)
        return carry

    lax.fori_loop(0, nchunks, recurrence, 0)

    @pl.when(t == last_t)
    def _():
        sout_ref[...] = s_ref[...]


def _gdn(proj3, conv_prev, conv_w, alog_row, dtb_row, norm_g_row, s0, *, c, tb):
    bsz, l, _ = proj3.shape
    nblk = l // tb
    assert AB_QKV % B_QKV == 0 and AB_ZB % B_VW == 0
    in_specs = [
        pl.BlockSpec((None, tb, B_QKV), lambda b, t: (b, t, AB_QKV // B_QKV)),
        pl.BlockSpec((None, tb, B_VW), lambda b, t: (b, t, AB_ZB // B_VW)),
        pl.BlockSpec((None, tb, LANES), lambda b, t: (b, t, AB_SMALL // LANES)),
        pl.BlockSpec((B_CONV, B_QKV), lambda b, t: (0, 0)),
        pl.BlockSpec((None, B_CONV - 1, B_QKV), lambda b, t: (b, 0, 0)),
        pl.BlockSpec((1, LANES), lambda b, t: (0, 0)),
        pl.BlockSpec((1, LANES), lambda b, t: (0, 0)),
        pl.BlockSpec((1, B_DV), lambda b, t: (0, 0)),
        pl.BlockSpec((None, B_HEADS, B_DK, B_DV), lambda b, t: (b, 0, 0, 0)),
    ]
    out_specs = [
        pl.BlockSpec((None, tb, B_VW), lambda b, t: (b, t, 0)),
        pl.BlockSpec((None, B_HEADS, B_DK, B_DV), lambda b, t: (b, 0, 0, 0)),
    ]
    blocks = tb * (B_QKV + B_VW + LANES) * 4 + tb * B_VW * 2 + 2 * B_HEADS * B_DK * B_DV * 4
    scratch = (B_HEADS * B_DK * B_DV * 4 + SUBLANES * B_QKV * 4 + tb * B_VW * 4
               + 4 * tb * B_KW * 2 + SUBLANES * LANES * 4)
    return pl.pallas_call(
        functools.partial(_gdn_kernel, c=c, nchunks=tb // c),
        out_shape=[jax.ShapeDtypeStruct((bsz, l, B_VW), BF16),
                   jax.ShapeDtypeStruct((bsz, B_HEADS, B_DK, B_DV), F32)],
        grid=(bsz, nblk),
        in_specs=in_specs,
        out_specs=out_specs,
        scratch_shapes=[pltpu.VMEM((B_HEADS, B_DK, B_DV), F32),
                        pltpu.VMEM((SUBLANES, B_QKV), F32),
                        pltpu.VMEM((tb, B_VW), F32),
                        pltpu.VMEM((tb, B_KW), BF16),
                        pltpu.VMEM((tb, B_KW), BF16),
                        pltpu.VMEM((tb, B_KW), BF16),
                        pltpu.VMEM((tb, B_HEADS * LANES), BF16),
                        pltpu.VMEM((SUBLANES, LANES), F32)],
        compiler_params=pltpu.CompilerParams(
            dimension_semantics=("parallel", "arbitrary"),
            vmem_limit_bytes=_vmem_limit(blocks, scratch)),
        name="gdn_mixer",
    )(proj3, proj3, proj3, conv_w, conv_prev, alog_row, dtb_row, norm_g_row, s0)


def _mlstm_kernel(q_ref, k_ref, v_ref, og_ref, z_ref, sm_ref, ib_ref, fb_ref, ng_ref,
                  c0_ref, n0_ref, m0_ref, o_ref, cout_ref, nout_ref, mout_ref,
                  c_ref, n_ref, m_ref, hi_ref, mi_ref, ni_ref, bc_ref, kv_ref, ks_ref, mc_ref, bl_ref,
                  *, c, nchunks):
    t = pl.program_id(1)
    last_t = pl.num_programs(1) - 1
    bb = q_ref.shape[0]
    chains = [(bi, h) for bi in range(bb) for h in range(D_HEADS)]
    ksl = [slice(h * D_DK, (h + 1) * D_DK) for _, h in chains]
    vsl = [slice(h * D_DV, (h + 1) * D_DV) for _, h in chains]
    idx = range(len(chains))

    @pl.when(t == 0)
    def _():
        c_ref[...] = c0_ref[...]
        n_ref[...] = n0_ref[...]
        for bi, h in chains:
            m_ref[bi, h] = jnp.broadcast_to(m0_ref[bi, :, h:h + 1], (1, LANES))

    causal, _ = _causal_masks(c)
    tri = causal.astype(BF16)
    sel = _lane_selector(0)
    lane = lax.broadcasted_iota(jnp.int32, (c, LANES), 1)
    ib = ib_ref[...]
    fb = fb_ref[...]

    group = MLSTM_INTRA_CHUNKS if nchunks % MLSTM_INTRA_CHUNKS == 0 else 1

    def intra(n, carry):
        items = [(bi, ci, h) for ci in range(group) for bi, h in chains]
        sls = [pl.ds(pl.multiple_of((n * group + ci) * c, c), c) for ci in range(group)]
        sm = {(bi, ci): sm_ref[bi, sls[ci], :] for ci in range(group) for bi in range(bb)}
        i_full = {key: x + ib for key, x in sm.items()}
        b_full = {key: _cumsum_rows(tri, jax.nn.log_sigmoid(x + fb)) for key, x in sm.items()}
        rows = {key: _select_rows(sel, jnp.where(lane < CD_F_LANE, i_full[key], b_full[key]))
                for key in sm}
        b_col = [b_full[bi, ci][:, CD_F_LANE + h:CD_F_LANE + h + 1] for bi, ci, h in items]
        i_col = [i_full[bi, ci][:, CD_I_LANE + h:CD_I_LANE + h + 1] for bi, ci, h in items]
        logw = [jnp.where(causal, b_col[i] - rows[bi, ci][CD_F_LANE + h:CD_F_LANE + h + 1, :]
                          + rows[bi, ci][CD_I_LANE + h:CD_I_LANE + h + 1, :], -jnp.inf)
                for i, (bi, ci, h) in enumerate(items)]
        ids = range(len(items))
        m_intra = [jnp.max(logw[i], axis=-1, keepdims=True) for i in ids]
        ks_ = [slice(h * D_DK, (h + 1) * D_DK) for _, _, h in items]
        vs_ = [slice(h * D_DV, (h + 1) * D_DV) for _, _, h in items]
        q_bf = [(q_ref[bi, sls[ci], ks_[i]] * (D_DK ** -0.5)).astype(BF16) for i, (bi, ci, h) in enumerate(items)]
        k = [k_ref[bi, sls[ci], ks_[i]] for i, (bi, ci, h) in enumerate(items)]
        v = [v_ref[bi, sls[ci], vs_[i]].astype(BF16) for i, (bi, ci, h) in enumerate(items)]
        p = [jnp.exp(logw[i] - m_intra[i]) * _dot_nt(q_bf[i], k[i]) for i in ids]
        n_intra = [jnp.sum(p[i], axis=-1, keepdims=True) for i in ids]
        m_chunk = [m_intra[i][c - 1:c, :] for i in ids]
        k_w = [k[i] * jnp.exp(b_col[i][c - 1:c, :] - b_col[i] + i_col[i] - m_chunk[i]) for i in ids]
        rep = lambda x: jnp.broadcast_to(x, (x.shape[0], LANES))
        for i, (bi, ci, h) in enumerate(items):
            row = pl.ds(n * group + ci, 1)
            hi_ref[bi, sls[ci], vs_[i]] = _dot(p[i], v[i])
            kv_ref[bi, n * group + ci, h] = _dot_tn(k_w[i], v[i])
            ks_ref[bi, h, row, :] = jnp.sum(k_w[i], axis=0, keepdims=True)
            mi_ref[bi, h, sls[ci], :] = rep(m_intra[i])
            ni_ref[bi, h, sls[ci], :] = rep(n_intra[i])
            bc_ref[bi, h, sls[ci], :] = rep(b_col[i])
            mc_ref[bi, h, row, :] = rep(m_chunk[i])
            bl_ref[bi, h, row, :] = rep(b_col[i][c - 1:c, :])
        return carry

    lax.fori_loop(0, nchunks // group, intra, 0)

    twice = lambda x: jnp.concatenate([x, x], axis=1)

    def body(n, carry):
        sl = pl.ds(pl.multiple_of(n * c, c), c)
        row = pl.ds(n, 1)
        q = [q_ref[chains[i][0], sl, ksl[i]] * (D_DK ** -0.5) for i in idx]
        c_s = [c_ref[bi, h] for bi, h in chains]
        n_s = [n_ref[bi, h:h + 1, :] for bi, h in chains]
        m_s = [m_ref[bi, h] for bi, h in chains]
        qc = [_dot(q[i], c_s[i]) for i in idx]
        b_last = [bl_ref[bi, h, row, :] for bi, h in chains]
        m_chunk = [mc_ref[bi, h, row, :] for bi, h in chains]
        for i, (bi, h) in enumerate(chains):
            m_new = jnp.maximum(b_last[i] + m_s[i], m_chunk[i])
            w_old = jnp.exp(b_last[i] + m_s[i] - m_new)
            w_new = jnp.exp(m_chunk[i] - m_new)
            c_ref[bi, h] = twice(w_old) * c_s[i] + twice(w_new) * kv_ref[bi, n, h]
            n_ref[bi, h:h + 1, :] = w_old * n_s[i] + w_new * ks_ref[bi, h, row, :]
            m_ref[bi, h] = m_new
        for i, (bi, h) in enumerate(chains):
            m_intra = mi_ref[bi, h, sl, :]
            a = bc_ref[bi, h, sl, :] + m_s[i]
            m_t = jnp.maximum(a, m_intra)
            w_a = jnp.exp(a - m_t)
            w_i = jnp.exp(m_intra - m_t)
            num = twice(w_a) * qc[i] + twice(w_i) * hi_ref[bi, sl, vsl[i]]
            den = w_a * jnp.sum(q[i] * n_s[i], axis=-1, keepdims=True) + w_i * ni_ref[bi, h, sl, :]
            hh = num / twice(jnp.maximum(jnp.abs(den), jnp.exp(-m_t)))
            hd = jax.nn.sigmoid(og_ref[bi, sl, vsl[i]]) * hh
            oc = hd - jnp.mean(hd, axis=-1, keepdims=True)
            o = oc * lax.rsqrt(jnp.mean(oc * oc, axis=-1, keepdims=True) + NORM_EPS) * ng_ref[:, vsl[i]]
            o_ref[bi, sl, vsl[i]] = (o * jax.nn.silu(z_ref[bi, sl, vsl[i]])).astype(o_ref.dtype)
        return carry

    lax.fori_loop(0, nchunks, body, 0)

    @pl.when(t == last_t)
    def _():
        cout_ref[...] = c_ref[...]
        nout_ref[...] = n_ref[...]
        for bi, h in chains:
            mout_ref[bi, :, h:h + 1] = m_ref[bi, h][:, 0:1]


def _mlstm(proj3, ib_row, fb_row, norm_g_row, c0, n0, m0, *, c, tb, bb):
    bsz, l, _ = proj3.shape
    nblk = l // tb
    assert bsz % bb == 0
    tok = lambda col: (lambda b, t: (b, t, col))
    st4 = lambda b, t: (b, 0, 0, 0)
    st3 = lambda b, t: (b, 0, 0)
    in_specs = [
        pl.BlockSpec((bb, tb, D_KW), tok(CD_Q // D_KW)),
        pl.BlockSpec((bb, tb, D_KW), tok(CD_K // D_KW)),
        pl.BlockSpec((bb, tb, D_VW), tok(CD_V // D_VW)),
        pl.BlockSpec((bb, tb, D_VW), tok(CD_O // D_VW)),
        pl.BlockSpec((bb, tb, D_VW), tok(CD_ZD // D_VW)),
        pl.BlockSpec((bb, tb, LANES), tok(CD_SMALL // LANES)),
        pl.BlockSpec((1, LANES), lambda b, t: (0, 0)),
        pl.BlockSpec((1, LANES), lambda b, t: (0, 0)),
        pl.BlockSpec((1, D_VW), lambda b, t: (0, 0)),
        pl.BlockSpec((bb, D_HEADS, D_DK, D_DV), st4),
        pl.BlockSpec((bb, D_HEADS, D_DK), st3),
        pl.BlockSpec((bb, 1, D_HEADS), st3),
    ]
    out_specs = [
        pl.BlockSpec((bb, tb, D_VW), lambda b, t: (b, t, 0)),
        pl.BlockSpec((bb, D_HEADS, D_DK, D_DV), st4),
        pl.BlockSpec((bb, D_HEADS, D_DK), st3),
        pl.BlockSpec((bb, 1, D_HEADS), st3),
    ]
    state = bb * D_HEADS * D_DK * D_DV * 4
    blocks = bb * tb * (2 * D_KW + 3 * D_VW + LANES) * 4 + bb * tb * D_VW * 2 + 2 * state
    nchunks = tb // c
    return pl.pallas_call(
        functools.partial(_mlstm_kernel, c=c, nchunks=nchunks),
        out_shape=[jax.ShapeDtypeStruct((bsz, l, D_VW), BF16),
                   jax.ShapeDtypeStruct((bsz, D_HEADS, D_DK, D_DV), F32),
                   jax.ShapeDtypeStruct((bsz, D_HEADS, D_DK), F32),
                   jax.ShapeDtypeStruct((bsz, 1, D_HEADS), F32)],
        grid=(bsz // bb, nblk),
        in_specs=in_specs,
        out_specs=out_specs,
        scratch_shapes=[pltpu.VMEM((bb, D_HEADS, D_DK, D_DV), F32),
                        pltpu.VMEM((bb, D_HEADS, D_DK), F32),
                        pltpu.VMEM((bb, D_HEADS, 1, LANES), F32),
                        pltpu.VMEM((bb, tb, D_VW), F32),
                        pltpu.VMEM((bb, D_HEADS, tb, LANES), F32),
                        pltpu.VMEM((bb, D_HEADS, tb, LANES), F32),
                        pltpu.VMEM((bb, D_HEADS, tb, LANES), F32),
                        pltpu.VMEM((bb, nchunks, D_HEADS, D_DK, D_DV), F32),
                        pltpu.VMEM((bb, D_HEADS, _round_up(nchunks, SUBLANES), D_DK), F32),
                        pltpu.VMEM((bb, D_HEADS, _round_up(nchunks, SUBLANES), LANES), F32),
                        pltpu.VMEM((bb, D_HEADS, _round_up(nchunks, SUBLANES), LANES), F32)],
        compiler_params=pltpu.CompilerParams(
            dimension_semantics=("parallel", "arbitrary"),
            vmem_limit_bytes=_vmem_limit(blocks, state * (1 + nchunks) + bb * tb * (D_VW + 2 * LANES) * 4)),
        name="mlstm_mixer",
    )(proj3, proj3, proj3, proj3, proj3, proj3, ib_row, fb_row, norm_g_row, c0, n0, m0)


def _s5_expand_operators(kc_ref, bc_ref, cc_ref, bd_ref, bst_ref, cst_ref):
    tc, gt, sw, cg = S5_CHUNK, S5_GROUP_BLOCK, 2 * C_STATE, C_GROUP
    w_t = tc * LANES
    iota = lambda shape, d: lax.broadcasted_iota(jnp.int32, shape, d)
    row_g = (iota((w_t, LANES), 0) // cg) % gt
    tile16 = jnp.where(iota((cg, LANES), 1) % cg == iota((cg, LANES), 0), 1.0, 0.0).astype(BF16)
    bd = jnp.where(row_g == iota((w_t, LANES), 1) // cg, _mm(kc_ref[...], tile16), 0.0).astype(BF16)
    bd_ref[:, LANES:2 * LANES] = bd
    bd_ref[0:w_t - LANES, 0:LANES] = bd[LANES:, :]
    bd_ref[w_t - LANES:w_t, 0:LANES] = jnp.zeros((LANES, LANES), BF16)
    bc = bc_ref[...].astype(F32)
    for g in range(gt):
        bst_ref[:, g * sw:(g + 1) * sw] = jnp.where(row_g == g, bc, 0.0).astype(BF16)
    src, dst = iota((tc * cg, w_t), 0), iota((tc * cg, w_t), 1)
    spread = jnp.where((src // cg == dst // LANES) & (src % cg == dst % cg), 1.0, 0.0).astype(BF16)
    lane_g = (iota((sw, w_t), 1) // cg) % gt
    for g in range(gt):
        full = _mm(cc_ref[g * sw:(g + 1) * sw, :], spread)
        cst_ref[g * sw:(g + 1) * sw, :] = jnp.where(lane_g == g, full, 0.0).astype(BF16)


def _s5_kernel(u_ref, kc_ref, bc_ref, cc_ref, apow_ref, x0_ref, y_ref, xf_ref, bd_ref, bst_ref, cst_ref,
               *, nc, bb):
    m = nc * bb
    tc = S5_CHUNK
    sw = 2 * C_STATE

    @pl.when(pl.program_id(1) == 0)
    def _():
        _s5_expand_operators(kc_ref, bc_ref, cc_ref, bd_ref, bst_ref, cst_ref)

    row = lax.broadcasted_iota(jnp.int32, (m, sw), 0)
    n_idx = row & (nc - 1)
    n_log = int(math.log2(nc))

    def cmul(a1, a2, x):
        return a1 * x + a2 * pltpu.roll(x, C_STATE, 1)

    lhs = jnp.concatenate([u_ref[pl.ds(tau, m, stride=tc), :].astype(BF16) for tau in range(tc)], axis=1)
    e_all = _mm(lhs, bst_ref[...])
    x_start = []
    for g in range(S5_GROUP_BLOCK):
        gs = slice(g * sw, (g + 1) * sw)
        x0_rows = jnp.zeros((m, sw), F32)
        for b in range(bb):
            x0_rows = jnp.where(row == b * nc, x0_ref[b, :, gs], x0_rows)
        x = e_all[:, gs] + cmul(apow_ref[0:1, gs], apow_ref[1:2, gs], x0_rows)
        for j in range(n_log):
            sh = 1 << j
            shifted = jnp.where(n_idx >= sh, pltpu.roll(x, sh, 0), 0.0)
            x = x + cmul(apow_ref[2 * j:2 * j + 1, gs], apow_ref[2 * j + 1:2 * j + 2, gs], shifted)
        x_start.append(jnp.where(n_idx >= 1, pltpu.roll(x, 1, 0), x0_rows).astype(BF16))
        for b in range(bb):
            xf_ref[b, :, gs] = x[b * nc + nc - 1:b * nc + nc, :]
    y_state = _mm(jnp.concatenate(x_start, axis=1), cst_ref[...])
    for tau in range(0, tc, 2):
        width = (tau + 2) * LANES
        start = (tc - 2 - tau) * LANES
        y = _mm(lhs[:, :width], bd_ref[start:start + width, :]) + y_state[:, tau * LANES:(tau + 2) * LANES]
        y_ref[pl.ds(tau, m, stride=tc), :] = y[:, :LANES]
        y_ref[pl.ds(tau + 1, m, stride=tc), :] = y[:, LANES:]


def _s5_chunks(proj2d, kc, bc, cc, apow, x0, *, l, bb):
    tc = S5_CHUNK
    nc = l // tc
    bsz = x0.shape[1]
    nt = C_GROUPS // S5_GROUP_BLOCK
    rows = bb * l
    sw = 2 * C_STATE
    sw_t = S5_GROUP_BLOCK * sw
    w_t = tc * LANES
    assert CD_U % LANES == 0 and bsz % bb == 0
    blocks = (2 * rows * LANES * 4 + 2 * w_t * LANES * 2 + sw_t * tc * C_GROUP * 2 + apow.shape[1] * sw_t * 4
              + 2 * bb * SUBLANES * sw_t * 4)
    scratch = w_t * 2 * LANES * 2 + 2 * w_t * sw_t * 2
    temps = bb * nc * (w_t * 2 + w_t * 4 + 3 * sw_t * 4) + 4 * sw * w_t * 4
    return pl.pallas_call(
        functools.partial(_s5_kernel, nc=nc, bb=bb),
        out_shape=[jax.ShapeDtypeStruct((bsz * l, C_W), F32),
                   jax.ShapeDtypeStruct((nt, bsz, 1, sw_t), F32)],
        grid=(nt, bsz // bb),
        in_specs=[pl.BlockSpec((rows, LANES), lambda i, j: (j, CD_U // LANES + i)),
                  pl.BlockSpec((None, w_t, C_GROUP), lambda i, j: (i, 0, 0)),
                  pl.BlockSpec((None, w_t, sw), lambda i, j: (i, 0, 0)),
                  pl.BlockSpec((None, sw_t, tc * C_GROUP), lambda i, j: (i, 0, 0)),
                  pl.BlockSpec((None, apow.shape[1], sw_t), lambda i, j: (i, 0, 0)),
                  pl.BlockSpec((None, bb, 1, sw_t), lambda i, j: (i, j, 0, 0))],
        out_specs=[pl.BlockSpec((rows, LANES), lambda i, j: (j, i)),
                   pl.BlockSpec((None, bb, 1, sw_t), lambda i, j: (i, j, 0, 0))],
        scratch_shapes=[pltpu.VMEM((w_t, 2 * LANES), BF16),
                        pltpu.VMEM((w_t, sw_t), BF16),
                        pltpu.VMEM((sw_t, w_t), BF16)],
        compiler_params=pltpu.CompilerParams(
            dimension_semantics=("parallel", "arbitrary"),
            vmem_limit_bytes=_vmem_limit(blocks, scratch + temps)),
        name="s5_chunks",
    )(proj2d, kc, bc, cc, apow, x0)


def _s5_glu_kernel(y_ref, u_ref, z_ref, d_ref, w_ref, b_ref, o_ref):
    y = jax.nn.gelu(y_ref[...] + d_ref[...] * u_ref[...])
    gate = jax.nn.sigmoid(jnp.dot(y.astype(BF16), w_ref[...], preferred_element_type=F32) + b_ref[...])
    o_ref[...] = (y * gate * jax.nn.silu(z_ref[...])).astype(o_ref.dtype)


def _s5_glu(y2d, proj2d, d_row, glu_w, glu_b_row):
    m = y2d.shape[0]
    tm = min(m, 512)
    blocks = 3 * tm * C_W * 4 + C_W * C_W * 2 + tm * C_W * 2
    return pl.pallas_call(
        _s5_glu_kernel,
        out_shape=jax.ShapeDtypeStruct((m, C_W), BF16),
        grid=(m // tm,),
        in_specs=[pl.BlockSpec((tm, C_W), lambda i: (i, 0)),
                  pl.BlockSpec((tm, C_W), lambda i: (i, CD_U // C_W)),
                  pl.BlockSpec((tm, C_W), lambda i: (i, CD_Z // C_W)),
                  pl.BlockSpec((1, C_W), lambda i: (0, 0)),
                  pl.BlockSpec((C_W, C_W), lambda i: (0, 0)),
                  pl.BlockSpec((1, C_W), lambda i: (0, 0))],
        out_specs=pl.BlockSpec((tm, C_W), lambda i: (i, 0)),
        compiler_params=pltpu.CompilerParams(
            dimension_semantics=("parallel",),
            vmem_limit_bytes=_vmem_limit(blocks)),
        name="s5_glu",
    )(y2d, proj2d, proj2d, d_row, glu_w, glu_b_row)


def _s5_operators(lam_re, lam_im, log_dt, b_re, b_im, c_re, c_im, n_log):
    g, p = lam_re.shape
    tc = S5_CHUNK
    dt = jnp.exp(log_dt.astype(F32))[:, None]
    mag = jnp.exp(lam_re * dt)
    ab_re, ab_im = mag * jnp.cos(lam_im * dt), mag * jnp.sin(lam_im * dt)
    den = lam_re * lam_re + lam_im * lam_im
    er = ab_re - 1.0
    zr = (er * lam_re + ab_im * lam_im) / den
    zi = (ab_im * lam_re - er * lam_im) / den
    bb_re = zr[..., None] * b_re - zi[..., None] * b_im
    bb_im = zr[..., None] * b_im + zi[..., None] * b_re
    pr, pi = [jnp.ones_like(ab_re)], [jnp.zeros_like(ab_re)]
    for _ in range(tc):
        pr, pi = pr + [pr[-1] * ab_re - pi[-1] * ab_im], pi + [pr[-1] * ab_im + pi[-1] * ab_re]
    pw_re, pw_im = jnp.stack(pr), jnp.stack(pi)
    abr = pw_re[:tc, :, :, None] * bb_re - pw_im[:tc, :, :, None] * bb_im
    abi = pw_re[:tc, :, :, None] * bb_im + pw_im[:tc, :, :, None] * bb_re
    kern = (jnp.einsum('gjp,dgpi->dgji', c_re, abr, precision=HIGHEST)
            - jnp.einsum('gjp,dgpi->dgji', c_im, abi, precision=HIGHEST))
    gt = S5_GROUP_BLOCK
    nt = g // gt
    sw = 2 * p
    kc = kern[::-1].reshape(tc, nt, gt, C_GROUP, C_GROUP).transpose(1, 0, 2, 4, 3)
    kc = kc.reshape(nt, tc * gt * C_GROUP, C_GROUP)
    ab = jnp.concatenate([abr, abi], axis=2)[::-1]
    bc = ab.reshape(tc, nt, gt, sw, C_GROUP).transpose(1, 0, 2, 4, 3).reshape(nt, tc * gt * C_GROUP, sw)
    cr = c_re[None] * pw_re[1:, :, None, :] - c_im[None] * pw_im[1:, :, None, :]
    ci = -(c_re[None] * pw_im[1:, :, None, :] + c_im[None] * pw_re[1:, :, None, :])
    cc = jnp.concatenate([cr, ci], axis=3)
    cc = cc.reshape(tc, nt, gt, C_GROUP, sw).transpose(1, 2, 4, 0, 3).reshape(nt, gt * sw, tc * C_GROUP)
    r, i = pw_re[tc], pw_im[tc]
    rows = []
    for _ in range(max(n_log, 1)):
        rows += [jnp.concatenate([r, r], -1), jnp.concatenate([-i, i], -1)]
        r, i = r * r - i * i, 2.0 * r * i
    apow = jnp.stack(rows, axis=1)
    apow = apow.reshape(nt, gt, -1, sw).transpose(0, 2, 1, 3).reshape(nt, -1, gt * sw)
    return kc.astype(BF16), bc.astype(BF16), cc.astype(BF16), apow


def _s5(proj2d, ops, x0_re, x0_im, *, l):
    kc, bc, cc, apow = ops
    bsz = x0_re.shape[0]
    nt = C_GROUPS // S5_GROUP_BLOCK
    x0 = jnp.concatenate([x0_re, x0_im], axis=-1).reshape(bsz, nt, 1, -1).transpose(1, 0, 2, 3)
    bb = bsz if bsz * l <= 4096 else 1
    y, xf = _s5_chunks(proj2d, kc, bc, cc, apow, x0, l=l, bb=bb)
    xf = xf.transpose(1, 0, 2, 3).reshape(bsz, C_GROUPS, 2 * C_STATE)
    return y, xf[..., :C_STATE], xf[..., C_STATE:]


AB_SRC_GA = AB_QKV
AB_SRC_QKV = AB_SRC_GA + A_GATE_RANK
AB_SRC_TAIL = AB_SRC_QKV + B_QKV + B_VW
IN_AB = AB_SRC_TAIL + 2 * B_HEADS
IN_CD = CD_SMALL + 2 * D_HEADS
WPREP_AB_TILE = 512
WPREP_CD_TILE = 640


def _prep_w_ab_kernel(wt_ref, ga_ref, o_ref):
    j = pl.program_id(0)
    tile = o_ref.shape[1]
    n_small = A_GATE_RANK + 2 * B_HEADS

    @pl.when(j < AB_SMALL // tile)
    def _():
        o_ref[...] = wt_ref[...].T.astype(BF16)

    @pl.when(j == AB_SMALL // tile)
    def _():
        rows = jnp.concatenate([ga_ref[...], wt_ref[tile - 2 * B_HEADS:tile, :],
                                jnp.zeros((tile - n_small, wt_ref.shape[1]), F32)], axis=0)
        o_ref[...] = rows.T.astype(BF16)

    @pl.when(j > AB_SMALL // tile)
    def _():
        o_ref[...] = jnp.zeros(o_ref.shape, BF16)


def _prep_w_ab(w_t, n_out):
    n_in, d = w_t.shape
    tile = WPREP_AB_TILE
    assert AB_QKV % tile == 0 and AB_SMALL % tile == 0 and n_out % tile == 0 and n_in >= tile

    unit = A_GATE_RANK
    assert tile % unit == 0 and (AB_SRC_QKV - AB_QKV) % unit == 0 and (n_in - tile) % unit == 0

    def src_row(j):
        k = j * (tile // unit)
        k = jnp.where(j < AB_QKV // tile, k,
                      jnp.where(j < AB_SMALL // tile, k + (AB_SRC_QKV - AB_QKV) // unit, (n_in - tile) // unit))
        return unit * k

    return pl.pallas_call(
        _prep_w_ab_kernel,
        out_shape=jax.ShapeDtypeStruct((d, n_out), BF16),
        grid=(n_out // tile,),
        in_specs=[pl.BlockSpec((pl.Element(tile), pl.Element(d)), lambda j: (src_row(j), 0)),
                  pl.BlockSpec((pl.Element(A_GATE_RANK), pl.Element(d)), lambda j: (AB_SRC_GA, 0))],
        out_specs=pl.BlockSpec((d, tile), lambda j: (0, j)),
        compiler_params=pltpu.CompilerParams(
            dimension_semantics=("parallel",),
            vmem_limit_bytes=_vmem_limit(tile * d * 6, 2 * tile * d * 4)),
        name="prep_w_in_ab",
    )(w_t, w_t)


def _prep_w_cd_kernel(wt_ref, o_ref):
    tile = o_ref.shape[1]
    row = pl.program_id(0) * tile + lax.broadcasted_iota(jnp.int32, wt_ref.shape, 0)
    o_ref[...] = jnp.where(row < IN_CD, wt_ref[...], 0.0).T.astype(BF16)


def _prep_w_cd(w_t, n_out):
    n_in, d = w_t.shape
    tile = WPREP_CD_TILE
    assert n_out % tile == 0
    return pl.pallas_call(
        _prep_w_cd_kernel,
        out_shape=jax.ShapeDtypeStruct((d, n_out), BF16),
        grid=(n_out // tile,),
        in_specs=[pl.BlockSpec((tile, d), lambda j: (j, 0))],
        out_specs=pl.BlockSpec((d, tile), lambda j: (0, j)),
        compiler_params=pltpu.CompilerParams(
            dimension_semantics=("parallel",),
            vmem_limit_bytes=_vmem_limit(tile * d * 6, 2 * tile * d * 4)),
        name="prep_w_in_cd",
    )(w_t)


def _lane_row(vals, lane0):
    return jnp.zeros((1, LANES), F32).at[0, lane0:lane0 + vals.shape[0]].set(vals.astype(F32))


def _prepare_weights(norm_g, final_norm_g, w_in_ab, a_gate_w, a_gate_b, a_norm_g, b_conv_w, b_a_log,
                     b_dt_bias, b_norm_g, w_out_ab, w_in_cd, c_lam_re, c_lam_im, c_log_dt, c_b_re,
                     c_b_im, c_c_re, c_c_im, c_d, c_glu_w, c_glu_b, d_i_bias, d_f_bias, d_norm_g,
                     w_out_cd, n_log):
    assert w_in_ab.shape[1] == IN_AB and w_in_cd.shape[1] == IN_CD
    w_ab = _prep_w_ab(w_in_ab.astype(F32).T, _round_up(AB_SMALL + LANES, PROJ_TN))
    w_cd = _prep_w_cd(w_in_cd.astype(F32).T, _round_up(CD_SMALL + LANES, PROJ_TN))
    gate_w = jnp.zeros((LANES, A_KW), F32).at[AB_GA_LANE:AB_GA_LANE + A_GATE_RANK].set(
        a_gate_w.astype(F32)).astype(BF16)
    return dict(
        norm_g=norm_g.astype(F32), final_g=final_norm_g.astype(F32)[None, :],
        w_ab=w_ab, w_cd=w_cd, gate_w=gate_w, gate_b=a_gate_b.astype(F32)[None, :],
        a_norm_g=a_norm_g.astype(F32)[None, :], conv_w=b_conv_w.astype(F32),
        alog=_lane_row(b_a_log, AB_APRE_LANE), dtb=_lane_row(b_dt_bias, AB_APRE_LANE),
        b_norm_g=b_norm_g.astype(F32)[None, :],
        w_out_a=w_out_ab[:A_VW].astype(BF16), w_out_b=w_out_ab[A_VW:].astype(BF16),
        s5_ops=_s5_operators(c_lam_re.astype(F32), c_lam_im.astype(F32), c_log_dt, c_b_re.astype(F32),
                             c_b_im.astype(F32), c_c_re.astype(F32), c_c_im.astype(F32), n_log),
        c_d=c_d.astype(F32).reshape(1, C_W), glu_w=c_glu_w.astype(BF16),
        glu_b=c_glu_b.astype(F32)[None, :],
        ib=_lane_row(d_i_bias, CD_I_LANE), fb=_lane_row(d_f_bias, CD_F_LANE),
        d_norm_g=d_norm_g.astype(F32)[None, :],
        w_out_c=w_out_cd[:C_W].astype(BF16), w_out_d=w_out_cd[C_W:].astype(BF16),
    )


def _trunk(x, conv_prev, s_gla0, s_gdn0, s5_re0, s5_im0, mc0, mn0, mm0, w):
    bsz, l, d = x.shape
    c = min(CHUNK, l)
    tb = min(l, 8 * c)
    assert l % tb == 0 and l % S5_CHUNK == 0
    x2d = x.reshape(bsz * l, d)

    proj = _norm_matmul(x2d, w['norm_g'][0:1], w['w_ab'])
    proj3 = proj.reshape(bsz, l, proj.shape[1])
    bb = MIXER_STREAMS if bsz % MIXER_STREAMS == 0 else 1
    tb_s = min(l, (8 // bb) * c)
    o_a, s_gla = _gla(proj3, w['gate_w'], w['gate_b'], w['a_norm_g'], s_gla0.astype(F32), c=c, tb=tb_s, bb=bb)
    o_b, s_gdn = _gdn(proj3, conv_prev.astype(F32), w['conv_w'], w['alog'], w['dtb'], w['b_norm_g'],
                      s_gdn0.astype(F32), c=c, tb=tb)
    conv_new = proj3[:, l - (B_CONV - 1):, AB_QKV:AB_QKV + B_QKV]
    h1 = _out_proj(o_a.reshape(bsz * l, A_VW), o_b.reshape(bsz * l, B_VW), w['w_out_a'], w['w_out_b'], x2d)

    proj = _norm_matmul(h1, w['norm_g'][1:2], w['w_cd'])
    proj3 = proj.reshape(bsz, l, proj.shape[1])
    y, s5_re, s5_im = _s5(proj, w['s5_ops'], s5_re0.astype(F32), s5_im0.astype(F32), l=l)
    o_c = _s5_glu(y, proj, w['c_d'], w['glu_w'], w['glu_b'])
    o_d, mc, mn, mm = _mlstm(proj3, w['ib'], w['fb'], w['d_norm_g'], mc0.astype(F32),
                             mn0.astype(F32), mm0.astype(F32)[:, None, :], c=c, tb=tb_s, bb=bb)
    y_out = _out_proj(o_c, o_d.reshape(bsz * l, D_VW), w['w_out_c'], w['w_out_d'], h1, w['final_g'])
    dt = x.dtype
    return (y_out.reshape(bsz, l, d).astype(dt), conv_new.astype(dt), s_gla.astype(dt), s_gdn.astype(dt),
            s5_re.astype(dt), s5_im.astype(dt), mc.astype(dt), mn.astype(dt), mm[:, 0, :].astype(dt))


def kernel(x_prompt, x_sample, cache_gdn_conv, state_gla, state_gdn, state_s5_re, state_s5_im,
           state_mlstm_c, state_mlstm_n, state_mlstm_m, norm_g, final_norm_g, w_in_ab, a_gate_w,
           a_gate_b, a_norm_g, b_conv_w, b_a_log, b_dt_bias, b_norm_g, w_out_ab, w_in_cd, c_lam_re,
           c_lam_im, c_log_dt, c_b_re, c_b_im, c_c_re, c_c_im, c_d, c_glu_w, c_glu_b, d_i_bias,
           d_f_bias, d_norm_g, w_out_cd):
    n_log = int(math.log2(max(x_prompt.shape[1], x_sample.shape[1]) // S5_CHUNK))
    w = _prepare_weights(norm_g, final_norm_g, w_in_ab, a_gate_w, a_gate_b, a_norm_g, b_conv_w, b_a_log,
                         b_dt_bias, b_norm_g, w_out_ab, w_in_cd, c_lam_re, c_lam_im, c_log_dt, c_b_re,
                         c_b_im, c_c_re, c_c_im, c_d, c_glu_w, c_glu_b, d_i_bias, d_f_bias, d_norm_g,
                         w_out_cd, n_log)
    nb = x_prompt.shape[0]
    zeros = lambda *shape: jnp.zeros(shape, F32)
    p_out = _trunk(x_prompt, zeros(nb, B_CONV - 1, B_QKV), zeros(nb, A_HEADS, A_DK, A_DV),
                   zeros(nb, B_HEADS, B_DK, B_DV), zeros(nb, C_GROUPS, C_STATE), zeros(nb, C_GROUPS, C_STATE),
                   zeros(nb, D_HEADS, D_DK, D_DV), zeros(nb, D_HEADS, D_DK), zeros(nb, D_HEADS), w)
    s_out = _trunk(x_sample, cache_gdn_conv, state_gla, state_gdn, state_s5_re, state_s5_im,
                   state_mlstm_c, state_mlstm_n, state_mlstm_m, w)
    return (p_out[0], s_out[0]) + tuple(p_out[1:]) + tuple(s_out[1:])
```

```python
import functools
import math

import jax
import jax.numpy as jnp
from jax import lax
from jax.experimental import pallas as pl
from jax.experimental.pallas import tpu as pltpu

F32 = jnp.float32
BF16 = jnp.bfloat16
HIGHEST = lax.Precision.HIGHEST

NORM_EPS = 1e-6
CHUNK = 64
A_HEADS, A_DK, A_DV, A_GATE_RANK, A_GATE_TAU = 4, 128, 256, 16, 16.0
B_HEADS, B_DK, B_DV, B_CONV = 8, 128, 128, 4
C_GROUP, C_GROUPS, C_STATE = 16, 64, 64
D_HEADS, D_DK, D_DV = 4, 128, 256
A_KW, A_VW = A_HEADS * A_DK, A_HEADS * A_DV
B_KW, B_VW = B_HEADS * B_DK, B_HEADS * B_DV
B_QKV = 2 * B_KW + B_VW
C_W = C_GROUPS * C_GROUP
D_KW, D_VW = D_HEADS * D_DK, D_HEADS * D_DV

LANES = 128
SUBLANES = 8
VMEM_BYTES_V7X = 64 * 1024 * 1024

AB_Q, AB_K, AB_V, AB_Z = 0, A_KW, 2 * A_KW, 2 * A_KW + A_VW
AB_QKV = AB_Z + A_VW
AB_ZB = AB_QKV + B_QKV
AB_SMALL = AB_ZB + B_VW
AB_GA_LANE, AB_BETA_LANE, AB_APRE_LANE = 0, A_GATE_RANK, A_GATE_RANK + B_HEADS
CD_U, CD_Z = 0, C_W
CD_Q = 2 * C_W
CD_K = CD_Q + D_KW
CD_V = CD_K + D_KW
CD_O = CD_V + D_VW
CD_ZD = CD_O + D_VW
CD_SMALL = CD_ZD + D_VW
CD_I_LANE, CD_F_LANE = 0, D_HEADS

PROJ_TN = 1280
S5_CHUNK = 16
S5_GROUP_BLOCK = 8
MIXER_STREAMS = 2
MLSTM_INTRA_CHUNKS = 4
GDN_WY_CHAINS = 32


def _round_up(x, m):
    return (x + m - 1) // m * m


def _vmem_limit(block_bytes, scratch_bytes=0):
    est = 2 * block_bytes + scratch_bytes
    return int(min(max(2 * est, 32 * 1024 * 1024), VMEM_BYTES_V7X - 8 * 1024 * 1024))


def _mm(a, b):
    return jnp.dot(a, b, preferred_element_type=F32)


def _dot(a, b):
    return _mm(a.astype(BF16), b.astype(BF16))


def _dot_nt(a, b):
    return lax.dot_general(a.astype(BF16), b.astype(BF16), (((1,), (1,)), ((), ())),
                           preferred_element_type=F32)


def _dot_tn(a, b):
    return lax.dot_general(a.astype(BF16), b.astype(BF16), (((0,), (0,)), ((), ())),
                           preferred_element_type=F32)


def _split2(x):
    hi = x.astype(BF16)
    return hi, (x - hi.astype(F32)).astype(BF16)


def _split3(x):
    hi = x.astype(BF16)
    r = x - hi.astype(F32)
    mid = r.astype(BF16)
    return hi, mid, (r - mid.astype(F32)).astype(BF16)


def _dot3(a, b):
    ah, al = _split2(a)
    bh, bl = _split2(b)
    return _mm(ah, bh) + _mm(ah, bl) + _mm(al, bh)


def _cumsum_rows(tri_bf16, x):
    hi, mid, lo = _split3(x)
    return _mm(tri_bf16, hi) + _mm(tri_bf16, mid) + _mm(tri_bf16, lo)


def _select_rows(sel_bf16, x):
    nt = lambda b: lax.dot_general(sel_bf16, b, (((1,), (1,)), ((), ())), preferred_element_type=F32)
    hi, mid, lo = _split3(x)
    return nt(hi) + nt(mid) + nt(lo)


def _lane_selector(lane0):
    r = lax.broadcasted_iota(jnp.int32, (SUBLANES, LANES), 0)
    l = lax.broadcasted_iota(jnp.int32, (SUBLANES, LANES), 1)
    return jnp.where(l == r + lane0, 1.0, 0.0).astype(BF16)


def _causal_masks(c):
    row = lax.broadcasted_iota(jnp.int32, (c, c), 0)
    col = lax.broadcasted_iota(jnp.int32, (c, c), 1)
    return row >= col, row > col


def _norm_matmul_kernel(x_ref, g_ref, w_ref, o_ref, xn_ref):
    @pl.when(pl.program_id(1) == 0)
    def _():
        x = x_ref[...]
        y = x * lax.rsqrt(jnp.mean(x * x, axis=-1, keepdims=True) + NORM_EPS) * g_ref[...]
        xn_ref[...] = y.astype(BF16)

    o_ref[...] = jnp.dot(xn_ref[...], w_ref[...], preferred_element_type=F32)


def _norm_matmul(x2d, g_row, w_bf16):
    m, d = x2d.shape
    n = w_bf16.shape[1]
    tm = min(m, 1024)
    tn = PROJ_TN
    assert m % tm == 0 and n % tn == 0
    blocks = tm * d * 4 + d * tn * 2 + tm * tn * 4
    return pl.pallas_call(
        _norm_matmul_kernel,
        out_shape=jax.ShapeDtypeStruct((m, n), F32),
        grid=(m // tm, n // tn),
        in_specs=[pl.BlockSpec((tm, d), lambda i, j: (i, 0)),
                  pl.BlockSpec((1, d), lambda i, j: (0, 0)),
                  pl.BlockSpec((d, tn), lambda i, j: (0, j))],
        out_specs=pl.BlockSpec((tm, tn), lambda i, j: (i, j)),
        scratch_shapes=[pltpu.VMEM((tm, d), BF16)],
        compiler_params=pltpu.CompilerParams(
            dimension_semantics=("parallel", "arbitrary"),
            vmem_limit_bytes=_vmem_limit(blocks, tm * d * 2)),
        name="norm_in_proj",
    )(x2d, g_row, w_bf16)


def _out_proj_kernel(a_ref, b_ref, wa_ref, wb_ref, h_ref, o_ref):
    out = (jnp.dot(a_ref[...], wa_ref[...], preferred_element_type=F32)
           + jnp.dot(b_ref[...], wb_ref[...], preferred_element_type=F32))
    o_ref[...] = h_ref[...] + out


def _glu_out_proj_norm_kernel(y_ref, u_ref, z_ref, od_ref, d_ref, gw_ref, gb_ref, wc_ref, wd_ref, h_ref, g_ref,
                              o_ref):
    y = jax.nn.gelu(y_ref[...] + d_ref[...] * u_ref[...])
    gate = jax.nn.sigmoid(jnp.dot(y.astype(BF16), gw_ref[...], preferred_element_type=F32) + gb_ref[...])
    o_c = (y * gate * jax.nn.silu(z_ref[...])).astype(BF16)
    out = (jnp.dot(o_c, wc_ref[...], preferred_element_type=F32)
           + jnp.dot(od_ref[...], wd_ref[...], preferred_element_type=F32))
    h = h_ref[...] + out
    o_ref[...] = h * lax.rsqrt(jnp.mean(h * h, axis=-1, keepdims=True) + NORM_EPS) * g_ref[...]


def _glu_out_proj_norm(y2d, proj2d, o_d, d_row, glu_w, glu_b_row, w_c, w_d, h2d, final_g_row):
    m, d = h2d.shape
    tm = min(m, 512)
    assert m % tm == 0
    row = lambda col: (lambda i: (i, col))
    const = lambda i: (0, 0)
    resident = pl.Buffered(1)
    blocks = 3 * tm * C_W * 4 + tm * D_VW * 2 + 2 * tm * d * 4
    weights = C_W * C_W * 2 + (C_W + D_VW) * d * 2
    return pl.pallas_call(
        _glu_out_proj_norm_kernel,
        out_shape=jax.ShapeDtypeStruct((m, d), F32),
        grid=(m // tm,),
        in_specs=[pl.BlockSpec((tm, C_W), row(0)),
                  pl.BlockSpec((tm, C_W), row(CD_U // C_W)),
                  pl.BlockSpec((tm, C_W), row(CD_Z // C_W)),
                  pl.BlockSpec((tm, D_VW), row(0)),
                  pl.BlockSpec((1, C_W), const),
                  pl.BlockSpec((C_W, C_W), const, pipeline_mode=resident),
                  pl.BlockSpec((1, C_W), const),
                  pl.BlockSpec((C_W, d), const, pipeline_mode=resident),
                  pl.BlockSpec((D_VW, d), const, pipeline_mode=resident),
                  pl.BlockSpec((tm, d), row(0)),
                  pl.BlockSpec((1, d), const)],
        out_specs=pl.BlockSpec((tm, d), row(0)),
        compiler_params=pltpu.CompilerParams(
            dimension_semantics=("parallel",),
            vmem_limit_bytes=_vmem_limit(blocks, weights + 3 * tm * C_W * 4)),
        name="glu_out_proj_norm",
    )(y2d, proj2d, proj2d, o_d, d_row, glu_w, glu_b_row, w_c, w_d, h2d, final_g_row)


def _out_proj(mix_a, mix_b, w_a, w_b, h2d):
    m, d = h2d.shape
    ka, kb = mix_a.shape[1], mix_b.shape[1]
    tm = min(m, 512)
    assert m % tm == 0
    blocks = tm * (ka + kb) * 2 + (ka + kb) * d * 2 + 2 * tm * d * 4
    return pl.pallas_call(
        _out_proj_kernel,
        out_shape=jax.ShapeDtypeStruct((m, d), F32),
        grid=(m // tm,),
        in_specs=[pl.BlockSpec((tm, ka), lambda i: (i, 0)),
                  pl.BlockSpec((tm, kb), lambda i: (i, 0)),
                  pl.BlockSpec((ka, d), lambda i: (0, 0)),
                  pl.BlockSpec((kb, d), lambda i: (0, 0)),
                  pl.BlockSpec((tm, d), lambda i: (i, 0))],
        out_specs=pl.BlockSpec((tm, d), lambda i: (i, 0)),
        compiler_params=pltpu.CompilerParams(
            dimension_semantics=("parallel",),
            vmem_limit_bytes=_vmem_limit(blocks)),
        name="out_proj",
    )(mix_a, mix_b, w_a, w_b, h2d)


def _gla_kernel(q_ref, k_ref, v_ref, z_ref, sm_ref, gw_ref, gb_ref, ng_ref, s0_ref,
                o_ref, sout_ref, st_ref, *, c, nchunks):
    t = pl.program_id(1)
    last_t = pl.num_programs(1) - 1
    bb = q_ref.shape[0]
    chains = [(bi, h) for bi in range(bb) for h in range(A_HEADS)]
    ksl = [slice(h * A_DK, (h + 1) * A_DK) for _, h in chains]
    vsl = [slice(h * A_DV, (h + 1) * A_DV) for _, h in chains]
    idx = range(len(chains))

    @pl.when(t == 0)
    def _():
        for bi, h in chains:
            st_ref[bi, h] = s0_ref[bi, h].T

    causal, _ = _causal_masks(c)
    tri = causal.astype(BF16)
    gw = gw_ref[...]
    gb = gb_ref[...]

    def body(n, carry):
        sl = pl.ds(pl.multiple_of(n * c, c), c)
        b_all = [_cumsum_rows(tri, jax.nn.log_sigmoid(_dot(sm_ref[bi, sl, :], gw) + gb) * (1.0 / A_GATE_TAU))
                 for bi in range(bb)]
        b = [b_all[chains[i][0]][:, ksl[i]] for i in idx]
        b_last = [b[i][c - 1:c, :] for i in idx]
        k = [k_ref[chains[i][0], sl, ksl[i]] for i in idx]
        v = [v_ref[chains[i][0], sl, vsl[i]].astype(BF16) for i in idx]
        q_dec = [(q_ref[chains[i][0], sl, ksl[i]] * (A_DK ** -0.5) * jnp.exp(b[i])).astype(BF16) for i in idx]
        k_dec = [(k[i] * jnp.exp(-b[i])).astype(BF16) for i in idx]
        k_w = [(k[i] * jnp.exp(b_last[i] - b[i])).astype(BF16) for i in idx]
        scores = [jnp.where(causal, _dot_nt(q_dec[i], k_dec[i]), 0.0).astype(BF16) for i in idx]
        s_t = [st_ref[bi, h] for bi, h in chains]
        outs = [_mm(scores[i], v[i]) + _dot_nt(q_dec[i], s_t[i]) for i in idx]
        for i, (bi, h) in enumerate(chains):
            st_ref[bi, h] = s_t[i] * jnp.exp(b_last[i]) + _dot_tn(v[i], k_w[i])
        for i, (bi, h) in enumerate(chains):
            o = outs[i]
            o = o * lax.rsqrt(jnp.mean(o * o, axis=-1, keepdims=True) + NORM_EPS) * ng_ref[:, vsl[i]]
            o_ref[bi, sl, vsl[i]] = (o * jax.nn.silu(z_ref[bi, sl, vsl[i]])).astype(o_ref.dtype)
        return carry

    lax.fori_loop(0, nchunks, body, 0)

    @pl.when(t == last_t)
    def _():
        for bi, h in chains:
            sout_ref[bi, h] = st_ref[bi, h].T


def _gla(proj3, gate_w_pad, gate_b_row, norm_g_row, s0, *, c, tb, bb):
    bsz, l, _ = proj3.shape
    nblk = l // tb
    assert bsz % bb == 0
    tok = lambda col: (lambda b, t: (b, t, col))
    in_specs = [
        pl.BlockSpec((bb, tb, A_KW), tok(AB_Q // A_KW)),
        pl.BlockSpec((bb, tb, A_KW), tok(AB_K // A_KW)),
        pl.BlockSpec((bb, tb, A_VW), tok(AB_V // A_VW)),
        pl.BlockSpec((bb, tb, A_VW), tok(AB_Z // A_VW)),
        pl.BlockSpec((bb, tb, LANES), tok(AB_SMALL // LANES)),
        pl.BlockSpec((LANES, A_KW), lambda b, t: (0, 0)),
        pl.BlockSpec((1, A_KW), lambda b, t: (0, 0)),
        pl.BlockSpec((1, A_VW), lambda b, t: (0, 0)),
        pl.BlockSpec((bb, A_HEADS, A_DK, A_DV), lambda b, t: (b, 0, 0, 0)),
    ]
    out_specs = [
        pl.BlockSpec((bb, tb, A_VW), lambda b, t: (b, t, 0)),
        pl.BlockSpec((bb, A_HEADS, A_DK, A_DV), lambda b, t: (b, 0, 0, 0)),
    ]
    state = bb * A_HEADS * A_DK * A_DV * 4
    blocks = bb * tb * (2 * A_KW + 2 * A_VW + LANES) * 4 + bb * tb * A_VW * 2 + 2 * state
    return pl.pallas_call(
        functools.partial(_gla_kernel, c=c, nchunks=tb // c),
        out_shape=[jax.ShapeDtypeStruct((bsz, l, A_VW), BF16),
                   jax.ShapeDtypeStruct((bsz, A_HEADS, A_DK, A_DV), F32)],
        grid=(bsz // bb, nblk),
        in_specs=in_specs,
        out_specs=out_specs,
        scratch_shapes=[pltpu.VMEM((bb, A_HEADS, A_DV, A_DK), F32)],
        compiler_params=pltpu.CompilerParams(
            dimension_semantics=("parallel", "arbitrary"),
            vmem_limit_bytes=_vmem_limit(blocks, state)),
        name="gla_mixer",
    )(proj3, proj3, proj3, proj3, proj3, gate_w_pad, gate_b_row, norm_g_row, s0)


def _gdn_kernel(x_ref, z_ref, sm_ref, w_ref, cp_ref, alog_ref, dtb_ref, ng_ref, s0_ref,
                o_ref, sout_ref, s_ref, tail_ref, u_ref, wm_ref, qg_ref, kg_ref, qk_ref, gl_ref,
                *, c, nchunks):
    t = pl.program_id(1)
    last_t = pl.num_programs(1) - 1
    tb = c * nchunks
    keep = SUBLANES - (B_CONV - 1)
    bb = x_ref.shape[0]

    @pl.when(t == 0)
    def _():
        s_ref[...] = s0_ref[...]
        tail_ref[:, 0:keep, :] = jnp.zeros((bb, keep, B_QKV), F32)
        tail_ref[:, keep:SUBLANES, :] = cp_ref[...]

    def conv_silu(bi, row0, first, cols):
        x = x_ref[bi, pl.ds(row0, c), cols]
        if first is None:
            prev = x_ref[bi, pl.ds(pl.multiple_of(row0 - SUBLANES, SUBLANES), SUBLANES), cols]
        else:
            before = pl.multiple_of(jnp.maximum(row0 - SUBLANES, 0), SUBLANES)
            prev = jnp.where(first, tail_ref[bi, :, cols], x_ref[bi, pl.ds(before, SUBLANES), cols])
        w = w_ref[:, cols]
        ext = jnp.concatenate([prev, x], axis=0)
        ext1 = pltpu.roll(ext, 1, 0)
        newer = ext * w[3:4, :] + ext1 * w[2:3, :]
        older = ext * w[1:2, :] + ext1 * w[0:1, :]
        conv = newer + pltpu.roll(older, 2, 0)
        return jax.nn.silu(conv[SUBLANES:SUBLANES + c, :])

    causal, strict = _causal_masks(c)
    tri = causal.astype(BF16)
    eye = jnp.where(causal & jnp.logical_not(strict), 1.0, 0.0).astype(F32)
    sel = _lane_selector(AB_APRE_LANE)
    neg_a_exp = -jnp.exp(alog_ref[...])
    dtb = dtb_ref[...]
    ng = ng_ref[...]
    n_double = int(math.log2(c)) - 1
    heads = range(B_HEADS)

    group = max(1, GDN_WY_CHAINS // (bb * B_HEADS))
    group = group if nchunks % group == 0 else 1

    def wy_factors(n, carry):
        pws, rhss, where = [], [], []
        for bi, ci in [(bi, ci) for ci in range(group) for bi in range(bb)]:
            row0 = pl.multiple_of((n * group + ci) * c, c)
            first = (n == 0) if ci == 0 else None
            sl = pl.ds(row0, c)
            sm = sm_ref[bi, sl, :]
            g_cum = _cumsum_rows(tri, neg_a_exp * jax.nn.softplus(sm + dtb))
            g_rows = _select_rows(sel, g_cum)
            beta_all = jax.nn.sigmoid(sm)
            gl_ref[bi, pl.ds(n * group + ci, 1), :] = g_cum[c - 1:c, :]
            for h in heads:
                hs = slice(h * B_DK, (h + 1) * B_DK)
                q = conv_silu(bi, row0, first, hs)
                k = conv_silu(bi, row0, first, slice(B_KW + h * B_DK, B_KW + (h + 1) * B_DK))
                v = conv_silu(bi, row0, first, slice(2 * B_KW + h * B_DV, 2 * B_KW + (h + 1) * B_DV))
                q = q * lax.rsqrt(jnp.sum(q * q, axis=-1, keepdims=True) + NORM_EPS) * (B_DK ** -0.5)
                k = k * lax.rsqrt(jnp.sum(k * k, axis=-1, keepdims=True) + NORM_EPS)
                g_col = g_cum[:, AB_APRE_LANE + h:AB_APRE_LANE + h + 1]
                beta = beta_all[:, AB_BETA_LANE + h:AB_BETA_LANE + h + 1]
                decay = jnp.exp(jnp.where(causal, g_col - g_rows[h:h + 1, :], -jnp.inf))
                e_g = jnp.exp(g_col)
                k_beta = k * beta
                pws.append(-jnp.where(strict, _dot_nt(k_beta, k) * decay, 0.0))
                rhss.append(jnp.concatenate([v * beta, k_beta * e_g], axis=1))
                where.append((bi, sl, hs))
                qk_ref[bi, sl, h * LANES:h * LANES + c] = jnp.where(causal, _dot_nt(q, k) * decay,
                                                                    0.0).astype(BF16)
                qg_ref[bi, sl, hs] = (q * e_g).astype(BF16)
                kg_ref[bi, sl, hs] = (k * jnp.exp(g_col[c - 1:c, :] - g_col)).astype(BF16)
        items = range(len(pws))
        neg_lower = pws
        invs = [eye + p for p in pws]
        for _ in range(n_double):
            pws = [_dot(p, p) for p in pws]
            invs = [i + _dot(i, p) for i, p in zip(invs, pws)]
        invs = [i.astype(BF16) for i in invs]
        sol = [_mm(invs[i], rhss[i].astype(BF16)) for i in items]
        resid = [rhss[i] - sol[i] + _dot3(neg_lower[i], sol[i]) for i in items]
        for i in items:
            bi, sl, hs = where[i]
            uw = sol[i] + _mm(invs[i], resid[i].astype(BF16))
            u_ref[bi, sl, hs] = uw[:, :B_DV]
            wm_ref[bi, sl, hs] = uw[:, B_DV:].astype(BF16)
        return carry

    lax.fori_loop(0, nchunks // group, wy_factors, 0)
    tail_ref[...] = x_ref[:, tb - SUBLANES:tb, :]

    chains = [(bi, h) for bi in range(bb) for h in heads]
    hsl = [slice(h * B_DK, (h + 1) * B_DK) for _, h in chains]

    def recurrence(n, carry):
        sl = pl.ds(pl.multiple_of(n * c, c), c)
        e_last = [jnp.exp(gl_ref[bi, pl.ds(n, 1), :]) for bi in range(bb)]
        s_old = [s_ref[bi, h] for bi, h in chains]
        s_bf = [s.astype(BF16) for s in s_old]
        v_new = [(u_ref[bi, sl, hsl[i]] - _mm(wm_ref[bi, sl, hsl[i]], s_bf[i])).astype(BF16)
                 for i, (bi, h) in enumerate(chains)]
        outs = [_mm(qg_ref[bi, sl, hsl[i]], s_bf[i]) + _mm(qk_ref[bi, sl, h * LANES:h * LANES + c], v_new[i])
                for i, (bi, h) in enumerate(chains)]
        for i, (bi, h) in enumerate(chains):
            s_ref[bi, h] = (e_last[bi][:, AB_APRE_LANE + h:AB_APRE_LANE + h + 1] * s_old[i]
                            + lax.dot_general(kg_ref[bi, sl, hsl[i]], v_new[i], (((0,), (0,)), ((), ())),
                                              preferred_element_type=F32))
        for i, (bi, h) in enumerate(chains):
            o = outs[i]
            o = o * lax.rsqrt(jnp.mean(o * o, axis=-1, keepdims=True) + NORM_EPS) * ng
            o_ref[bi, sl, hsl[i]] = (o * jax.nn.silu(z_ref[bi, sl, hsl[i]])).astype(o_ref.dtype)
        return carry

    lax.fori_loop(0, nchunks, recurrence, 0)

    @pl.when(t == last_t)
    def _():
        sout_ref[...] = s_ref[...]


def _gdn(proj3, conv_prev, conv_w, alog_row, dtb_row, norm_g_row, s0, *, c, tb, bb):
    bsz, l, _ = proj3.shape
    nblk = l // tb
    assert AB_QKV % B_QKV == 0 and AB_ZB % B_VW == 0 and bsz % bb == 0
    in_specs = [
        pl.BlockSpec((bb, tb, B_QKV), lambda b, t: (b, t, AB_QKV // B_QKV)),
        pl.BlockSpec((bb, tb, B_VW), lambda b, t: (b, t, AB_ZB // B_VW)),
        pl.BlockSpec((bb, tb, LANES), lambda b, t: (b, t, AB_SMALL // LANES)),
        pl.BlockSpec((B_CONV, B_QKV), lambda b, t: (0, 0)),
        pl.BlockSpec((bb, B_CONV - 1, B_QKV), lambda b, t: (b, 0, 0)),
        pl.BlockSpec((1, LANES), lambda b, t: (0, 0)),
        pl.BlockSpec((1, LANES), lambda b, t: (0, 0)),
        pl.BlockSpec((1, B_DV), lambda b, t: (0, 0)),
        pl.BlockSpec((bb, B_HEADS, B_DK, B_DV), lambda b, t: (b, 0, 0, 0)),
    ]
    out_specs = [
        pl.BlockSpec((bb, tb, B_VW), lambda b, t: (b, t, 0)),
        pl.BlockSpec((bb, B_HEADS, B_DK, B_DV), lambda b, t: (b, 0, 0, 0)),
    ]
    blocks = bb * (tb * (B_QKV + B_VW + LANES) * 4 + tb * B_VW * 2 + 2 * B_HEADS * B_DK * B_DV * 4)
    scratch = bb * (B_HEADS * B_DK * B_DV * 4 + SUBLANES * B_QKV * 4 + tb * B_VW * 4
                    + 4 * tb * B_KW * 2 + SUBLANES * LANES * 4)
    return pl.pallas_call(
        functools.partial(_gdn_kernel, c=c, nchunks=tb // c),
        out_shape=[jax.ShapeDtypeStruct((bsz, l, B_VW), BF16),
                   jax.ShapeDtypeStruct((bsz, B_HEADS, B_DK, B_DV), F32)],
        grid=(bsz // bb, nblk),
        in_specs=in_specs,
        out_specs=out_specs,
        scratch_shapes=[pltpu.VMEM((bb, B_HEADS, B_DK, B_DV), F32),
                        pltpu.VMEM((bb, SUBLANES, B_QKV), F32),
                        pltpu.VMEM((bb, tb, B_VW), F32),
                        pltpu.VMEM((bb, tb, B_KW), BF16),
                        pltpu.VMEM((bb, tb, B_KW), BF16),
                        pltpu.VMEM((bb, tb, B_KW), BF16),
                        pltpu.VMEM((bb, tb, B_HEADS * LANES), BF16),
                        pltpu.VMEM((bb, SUBLANES, LANES), F32)],
        compiler_params=pltpu.CompilerParams(
            dimension_semantics=("parallel", "arbitrary"),
            vmem_limit_bytes=_vmem_limit(blocks, scratch)),
        name="gdn_mixer",
    )(proj3, proj3, proj3, conv_w, conv_prev, alog_row, dtb_row, norm_g_row, s0)


def _mlstm_kernel(q_ref, k_ref, v_ref, og_ref, z_ref, sm_ref, ib_ref, fb_ref, ng_ref,
                  c0_ref, n0_ref, m0_ref, o_ref, cout_ref, nout_ref, mout_ref,
                  c_ref, n_ref, m_ref, hi_ref, mi_ref, ni_ref, bc_ref, kv_ref, ks_ref, mc_ref, bl_ref,
                  *, c, nchunks):
    t = pl.program_id(1)
    last_t = pl.num_programs(1) - 1
    bb = q_ref.shape[0]
    chains = [(bi, h) for bi in range(bb) for h in range(D_HEADS)]
    ksl = [slice(h * D_DK, (h + 1) * D_DK) for _, h in chains]
    vsl = [slice(h * D_DV, (h + 1) * D_DV) for _, h in chains]
    idx = range(len(chains))

    @pl.when(t == 0)
    def _():
        c_ref[...] = c0_ref[...]
        n_ref[...] = n0_ref[...]
        for bi, h in chains:
            m_ref[bi, h] = jnp.broadcast_to(m0_ref[bi, :, h:h + 1], (1, LANES))

    causal, _ = _causal_masks(c)
    tri = causal.astype(BF16)
    sel = _lane_selector(0)
    lane = lax.broadcasted_iota(jnp.int32, (c, LANES), 1)
    ib = ib_ref[...]
    fb = fb_ref[...]

    group = MLSTM_INTRA_CHUNKS if nchunks % MLSTM_INTRA_CHUNKS == 0 else 1

    def intra(n, carry):
        items = [(bi, ci, h) for ci in range(group) for bi, h in chains]
        sls = [pl.ds(pl.multiple_of((n * group + ci) * c, c), c) for ci in range(group)]
        sm = {(bi, ci): sm_ref[bi, sls[ci], :] for ci in range(group) for bi in range(bb)}
        i_full = {key: x + ib for key, x in sm.items()}
        b_full = {key: _cumsum_rows(tri, jax.nn.log_sigmoid(x + fb)) for key, x in sm.items()}
        rows = {key: _select_rows(sel, jnp.where(lane < CD_F_LANE, i_full[key], b_full[key]))
                for key in sm}
        b_col = [b_full[bi, ci][:, CD_F_LANE + h:CD_F_LANE + h + 1] for bi, ci, h in items]
        i_col = [i_full[bi, ci][:, CD_I_LANE + h:CD_I_LANE + h + 1] for bi, ci, h in items]
        logw = [jnp.where(causal, b_col[i] - rows[bi, ci][CD_F_LANE + h:CD_F_LANE + h + 1, :]
                          + rows[bi, ci][CD_I_LANE + h:CD_I_LANE + h + 1, :], -jnp.inf)
                for i, (bi, ci, h) in enumerate(items)]
        ids = range(len(items))
        m_intra = [jnp.max(logw[i], axis=-1, keepdims=True) for i in ids]
        ks_ = [slice(h * D_DK, (h + 1) * D_DK) for _, _, h in items]
        vs_ = [slice(h * D_DV, (h + 1) * D_DV) for _, _, h in items]
        q_bf = [(q_ref[bi, sls[ci], ks_[i]] * (D_DK ** -0.5)).astype(BF16) for i, (bi, ci, h) in enumerate(items)]
        k = [k_ref[bi, sls[ci], ks_[i]] for i, (bi, ci, h) in enumerate(items)]
        v = [v_ref[bi, sls[ci], vs_[i]].astype(BF16) for i, (bi, ci, h) in enumerate(items)]
        p = [jnp.exp(logw[i] - m_intra[i]) * _dot_nt(q_bf[i], k[i]) for i in ids]
        n_intra = [jnp.sum(p[i], axis=-1, keepdims=True) for i in ids]
        m_chunk = [m_intra[i][c - 1:c, :] for i in ids]
        k_w = [k[i] * jnp.exp(b_col[i][c - 1:c, :] - b_col[i] + i_col[i] - m_chunk[i]) for i in ids]
        rep = lambda x: jnp.broadcast_to(x, (x.shape[0], LANES))
        for i, (bi, ci, h) in enumerate(items):
            row = pl.ds(n * group + ci, 1)
            hi_ref[bi, sls[ci], vs_[i]] = _dot(p[i], v[i])
            kv_ref[bi, n * group + ci, h] = _dot_tn(k_w[i], v[i])
            ks_ref[bi, h, row, :] = jnp.sum(k_w[i], axis=0, keepdims=True)
            mi_ref[bi, h, sls[ci], :] = rep(m_intra[i])
            ni_ref[bi, h, sls[ci], :] = rep(n_intra[i])
            bc_ref[bi, h, sls[ci], :] = rep(b_col[i])
            mc_ref[bi, h, row, :] = rep(m_chunk[i])
            bl_ref[bi, h, row, :] = rep(b_col[i][c - 1:c, :])
        return carry

    lax.fori_loop(0, nchunks // group, intra, 0)

    twice = lambda x: jnp.concatenate([x, x], axis=1)

    def body(n, carry):
        sl = pl.ds(pl.multiple_of(n * c, c), c)
        row = pl.ds(n, 1)
        q = [q_ref[chains[i][0], sl, ksl[i]] * (D_DK ** -0.5) for i in idx]
        c_s = [c_ref[bi, h] for bi, h in chains]
        n_s = [n_ref[bi, h:h + 1, :] for bi, h in chains]
        m_s = [m_ref[bi, h] for bi, h in chains]
        qc = [_dot(q[i], c_s[i]) for i in idx]
        b_last = [bl_ref[bi, h, row, :] for bi, h in chains]
        m_chunk = [mc_ref[bi, h, row, :] for bi, h in chains]
        for i, (bi, h) in enumerate(chains):
            m_new = jnp.maximum(b_last[i] + m_s[i], m_chunk[i])
            w_old = jnp.exp(b_last[i] + m_s[i] - m_new)
            w_new = jnp.exp(m_chunk[i] - m_new)
            c_ref[bi, h] = twice(w_old) * c_s[i] + twice(w_new) * kv_ref[bi, n, h]
            n_ref[bi, h:h + 1, :] = w_old * n_s[i] + w_new * ks_ref[bi, h, row, :]
            m_ref[bi, h] = m_new
        for i, (bi, h) in enumerate(chains):
            m_intra = mi_ref[bi, h, sl, :]
            a = bc_ref[bi, h, sl, :] + m_s[i]
            m_t = jnp.maximum(a, m_intra)
            w_a = jnp.exp(a - m_t)
            w_i = jnp.exp(m_intra - m_t)
            num = twice(w_a) * qc[i] + twice(w_i) * hi_ref[bi, sl, vsl[i]]
            den = w_a * jnp.sum(q[i] * n_s[i], axis=-1, keepdims=True) + w_i * ni_ref[bi, h, sl, :]
            hh = num / twice(jnp.maximum(jnp.abs(den), jnp.exp(-m_t)))
            hd = jax.nn.sigmoid(og_ref[bi, sl, vsl[i]]) * hh
            oc = hd - jnp.mean(hd, axis=-1, keepdims=True)
            o = oc * lax.rsqrt(jnp.mean(oc * oc, axis=-1, keepdims=True) + NORM_EPS) * ng_ref[:, vsl[i]]
            o_ref[bi, sl, vsl[i]] = (o * jax.nn.silu(z_ref[bi, sl, vsl[i]])).astype(o_ref.dtype)
        return carry

    lax.fori_loop(0, nchunks, body, 0)

    @pl.when(t == last_t)
    def _():
        cout_ref[...] = c_ref[...]
        nout_ref[...] = n_ref[...]
        for bi, h in chains:
            mout_ref[bi, :, h:h + 1] = m_ref[bi, h][:, 0:1]


def _mlstm(proj3, ib_row, fb_row, norm_g_row, c0, n0, m0, *, c, tb, bb):
    bsz, l, _ = proj3.shape
    nblk = l // tb
    assert bsz % bb == 0
    tok = lambda col: (lambda b, t: (b, t, col))
    st4 = lambda b, t: (b, 0, 0, 0)
    st3 = lambda b, t: (b, 0, 0)
    in_specs = [
        pl.BlockSpec((bb, tb, D_KW), tok(CD_Q // D_KW)),
        pl.BlockSpec((bb, tb, D_KW), tok(CD_K // D_KW)),
        pl.BlockSpec((bb, tb, D_VW), tok(CD_V // D_VW)),
        pl.BlockSpec((bb, tb, D_VW), tok(CD_O // D_VW)),
        pl.BlockSpec((bb, tb, D_VW), tok(CD_ZD // D_VW)),
        pl.BlockSpec((bb, tb, LANES), tok(CD_SMALL // LANES)),
        pl.BlockSpec((1, LANES), lambda b, t: (0, 0)),
        pl.BlockSpec((1, LANES), lambda b, t: (0, 0)),
        pl.BlockSpec((1, D_VW), lambda b, t: (0, 0)),
        pl.BlockSpec((bb, D_HEADS, D_DK, D_DV), st4),
        pl.BlockSpec((bb, D_HEADS, D_DK), st3),
        pl.BlockSpec((bb, 1, D_HEADS), st3),
    ]
    out_specs = [
        pl.BlockSpec((bb, tb, D_VW), lambda b, t: (b, t, 0)),
        pl.BlockSpec((bb, D_HEADS, D_DK, D_DV), st4),
        pl.BlockSpec((bb, D_HEADS, D_DK), st3),
        pl.BlockSpec((bb, 1, D_HEADS), st3),
    ]
    state = bb * D_HEADS * D_DK * D_DV * 4
    blocks = bb * tb * (2 * D_KW + 3 * D_VW + LANES) * 4 + bb * tb * D_VW * 2 + 2 * state
    nchunks = tb // c
    return pl.pallas_call(
        functools.partial(_mlstm_kernel, c=c, nchunks=nchunks),
        out_shape=[jax.ShapeDtypeStruct((bsz, l, D_VW), BF16),
                   jax.ShapeDtypeStruct((bsz, D_HEADS, D_DK, D_DV), F32),
                   jax.ShapeDtypeStruct((bsz, D_HEADS, D_DK), F32),
                   jax.ShapeDtypeStruct((bsz, 1, D_HEADS), F32)],
        grid=(bsz // bb, nblk),
        in_specs=in_specs,
        out_specs=out_specs,
        scratch_shapes=[pltpu.VMEM((bb, D_HEADS, D_DK, D_DV), F32),
                        pltpu.VMEM((bb, D_HEADS, D_DK), F32),
                        pltpu.VMEM((bb, D_HEADS, 1, LANES), F32),
                        pltpu.VMEM((bb, tb, D_VW), F32),
                        pltpu.VMEM((bb, D_HEADS, tb, LANES), F32),
                        pltpu.VMEM((bb, D_HEADS, tb, LANES), F32),
                        pltpu.VMEM((bb, D_HEADS, tb, LANES), F32),
                        pltpu.VMEM((bb, nchunks, D_HEADS, D_DK, D_DV), F32),
                        pltpu.VMEM((bb, D_HEADS, _round_up(nchunks, SUBLANES), D_DK), F32),
                        pltpu.VMEM((bb, D_HEADS, _round_up(nchunks, SUBLANES), LANES), F32),
                        pltpu.VMEM((bb, D_HEADS, _round_up(nchunks, SUBLANES), LANES), F32)],
        compiler_params=pltpu.CompilerParams(
            dimension_semantics=("parallel", "arbitrary"),
            vmem_limit_bytes=_vmem_limit(blocks, state * (1 + nchunks) + bb * tb * (D_VW + 2 * LANES) * 4)),
        name="mlstm_mixer",
    )(proj3, proj3, proj3, proj3, proj3, proj3, ib_row, fb_row, norm_g_row, c0, n0, m0)


def _s5_expand_operators(kc_ref, bc_ref, cc_ref, bd_ref, bst_ref, cst_ref):
    tc, gt, sw, cg = S5_CHUNK, S5_GROUP_BLOCK, 2 * C_STATE, C_GROUP
    w_t = tc * LANES
    iota = lambda shape, d: lax.broadcasted_iota(jnp.int32, shape, d)
    row_g = (iota((w_t, LANES), 0) // cg) % gt
    tile16 = jnp.where(iota((cg, LANES), 1) % cg == iota((cg, LANES), 0), 1.0, 0.0).astype(BF16)
    bd = jnp.where(row_g == iota((w_t, LANES), 1) // cg, _mm(kc_ref[...], tile16), 0.0).astype(BF16)
    bd_ref[:, LANES:2 * LANES] = bd
    bd_ref[0:w_t - LANES, 0:LANES] = bd[LANES:, :]
    bd_ref[w_t - LANES:w_t, 0:LANES] = jnp.zeros((LANES, LANES), BF16)
    bc = bc_ref[...].astype(F32)
    for g in range(gt):
        bst_ref[:, g * sw:(g + 1) * sw] = jnp.where(row_g == g, bc, 0.0).astype(BF16)
    src, dst = iota((tc * cg, w_t), 0), iota((tc * cg, w_t), 1)
    spread = jnp.where((src // cg == dst // LANES) & (src % cg == dst % cg), 1.0, 0.0).astype(BF16)
    lane_g = (iota((sw, w_t), 1) // cg) % gt
    for g in range(gt):
        full = _mm(cc_ref[g * sw:(g + 1) * sw, :], spread)
        cst_ref[g * sw:(g + 1) * sw, :] = jnp.where(lane_g == g, full, 0.0).astype(BF16)


def _s5_kernel(u_ref, kc_ref, bc_ref, cc_ref, apow_ref, x0_ref, y_ref, xf_ref, bd_ref, bst_ref, cst_ref,
               *, nc, bb):
    m = nc * bb
    tc = S5_CHUNK
    sw = 2 * C_STATE

    @pl.when(pl.program_id(1) == 0)
    def _():
        _s5_expand_operators(kc_ref, bc_ref, cc_ref, bd_ref, bst_ref, cst_ref)

    row = lax.broadcasted_iota(jnp.int32, (m, sw), 0)
    n_idx = row & (nc - 1)
    n_log = int(math.log2(nc))

    def cmul(a1, a2, x):
        return a1 * x + a2 * pltpu.roll(x, C_STATE, 1)

    lhs = jnp.concatenate([u_ref[pl.ds(tau, m, stride=tc), :].astype(BF16) for tau in range(tc)], axis=1)
    e_all = _mm(lhs, bst_ref[...])
    x_start = []
    for g in range(S5_GROUP_BLOCK):
        gs = slice(g * sw, (g + 1) * sw)
        x0_rows = jnp.zeros((m, sw), F32)
        for b in range(bb):
            x0_rows = jnp.where(row == b * nc, x0_ref[b, :, gs], x0_rows)
        x = e_all[:, gs] + cmul(apow_ref[0:1, gs], apow_ref[1:2, gs], x0_rows)
        for j in range(n_log):
            sh = 1 << j
            shifted = jnp.where(n_idx >= sh, pltpu.roll(x, sh, 0), 0.0)
            x = x + cmul(apow_ref[2 * j:2 * j + 1, gs], apow_ref[2 * j + 1:2 * j + 2, gs], shifted)
        x_start.append(jnp.where(n_idx >= 1, pltpu.roll(x, 1, 0), x0_rows).astype(BF16))
        for b in range(bb):
            xf_ref[b, :, gs] = x[b * nc + nc - 1:b * nc + nc, :]
    y_state = _mm(jnp.concatenate(x_start, axis=1), cst_ref[...])
    for tau in range(0, tc, 2):
        width = (tau + 2) * LANES
        start = (tc - 2 - tau) * LANES
        y = _mm(lhs[:, :width], bd_ref[start:start + width, :]) + y_state[:, tau * LANES:(tau + 2) * LANES]
        y_ref[pl.ds(tau, m, stride=tc), :] = y[:, :LANES]
        y_ref[pl.ds(tau + 1, m, stride=tc), :] = y[:, LANES:]


def _s5_chunks(proj2d, kc, bc, cc, apow, x0, *, l, bb):
    tc = S5_CHUNK
    nc = l // tc
    bsz = x0.shape[1]
    nt = C_GROUPS // S5_GROUP_BLOCK
    rows = bb * l
    sw = 2 * C_STATE
    sw_t = S5_GROUP_BLOCK * sw
    w_t = tc * LANES
    assert CD_U % LANES == 0 and bsz % bb == 0
    blocks = (2 * rows * LANES * 4 + 2 * w_t * LANES * 2 + sw_t * tc * C_GROUP * 2 + apow.shape[1] * sw_t * 4
              + 2 * bb * SUBLANES * sw_t * 4)
    scratch = w_t * 2 * LANES * 2 + 2 * w_t * sw_t * 2
    temps = bb * nc * (w_t * 2 + w_t * 4 + 3 * sw_t * 4) + 4 * sw * w_t * 4
    return pl.pallas_call(
        functools.partial(_s5_kernel, nc=nc, bb=bb),
        out_shape=[jax.ShapeDtypeStruct((bsz * l, C_W), F32),
                   jax.ShapeDtypeStruct((nt, bsz, 1, sw_t), F32)],
        grid=(nt, bsz // bb),
        in_specs=[pl.BlockSpec((rows, LANES), lambda i, j: (j, CD_U // LANES + i)),
                  pl.BlockSpec((None, w_t, C_GROUP), lambda i, j: (i, 0, 0)),
                  pl.BlockSpec((None, w_t, sw), lambda i, j: (i, 0, 0)),
                  pl.BlockSpec((None, sw_t, tc * C_GROUP), lambda i, j: (i, 0, 0)),
                  pl.BlockSpec((None, apow.shape[1], sw_t), lambda i, j: (i, 0, 0)),
                  pl.BlockSpec((None, bb, 1, sw_t), lambda i, j: (i, j, 0, 0))],
        out_specs=[pl.BlockSpec((rows, LANES), lambda i, j: (j, i)),
                   pl.BlockSpec((None, bb, 1, sw_t), lambda i, j: (i, j, 0, 0))],
        scratch_shapes=[pltpu.VMEM((w_t, 2 * LANES), BF16),
                        pltpu.VMEM((w_t, sw_t), BF16),
                        pltpu.VMEM((sw_t, w_t), BF16)],
        compiler_params=pltpu.CompilerParams(
            dimension_semantics=("parallel", "arbitrary"),
            vmem_limit_bytes=_vmem_limit(blocks, scratch + temps)),
        name="s5_chunks",
    )(proj2d, kc, bc, cc, apow, x0)


def _s5_operators(lam_re, lam_im, log_dt, b_re, b_im, c_re, c_im, n_log):
    g, p = lam_re.shape
    tc = S5_CHUNK
    dt = jnp.exp(log_dt.astype(F32))[:, None]
    mag = jnp.exp(lam_re * dt)
    ab_re, ab_im = mag * jnp.cos(lam_im * dt), mag * jnp.sin(lam_im * dt)
    den = lam_re * lam_re + lam_im * lam_im
    er = ab_re - 1.0
    zr = (er * lam_re + ab_im * lam_im) / den
    zi = (ab_im * lam_re - er * lam_im) / den
    bb_re = zr[..., None] * b_re - zi[..., None] * b_im
    bb_im = zr[..., None] * b_im + zi[..., None] * b_re
    pw_re, pw_im = ab_re[None], ab_im[None]
    while pw_re.shape[0] < tc:
        top_re, top_im = pw_re[-1], pw_im[-1]
        pw_re, pw_im = (jnp.concatenate([pw_re, top_re * pw_re - top_im * pw_im]),
                        jnp.concatenate([pw_im, top_re * pw_im + top_im * pw_re]))
    pw_re = jnp.concatenate([jnp.ones_like(ab_re)[None], pw_re])
    pw_im = jnp.concatenate([jnp.zeros_like(ab_im)[None], pw_im])
    abr = pw_re[:tc, :, :, None] * bb_re - pw_im[:tc, :, :, None] * bb_im
    abi = pw_re[:tc, :, :, None] * bb_im + pw_im[:tc, :, :, None] * bb_re
    kern = (jnp.einsum('gjp,dgpi->dgji', c_re, abr, precision=HIGHEST)
            - jnp.einsum('gjp,dgpi->dgji', c_im, abi, precision=HIGHEST))
    gt = S5_GROUP_BLOCK
    nt = g // gt
    sw = 2 * p
    kc = kern[::-1].reshape(tc, nt, gt, C_GROUP, C_GROUP).transpose(1, 0, 2, 4, 3)
    kc = kc.reshape(nt, tc * gt * C_GROUP, C_GROUP)
    ab = jnp.concatenate([abr, abi], axis=2)[::-1]
    bc = ab.reshape(tc, nt, gt, sw, C_GROUP).transpose(1, 0, 2, 4, 3).reshape(nt, tc * gt * C_GROUP, sw)
    cr = c_re[None] * pw_re[1:, :, None, :] - c_im[None] * pw_im[1:, :, None, :]
    ci = -(c_re[None] * pw_im[1:, :, None, :] + c_im[None] * pw_re[1:, :, None, :])
    cc = jnp.concatenate([cr, ci], axis=3)
    cc = cc.reshape(tc, nt, gt, C_GROUP, sw).transpose(1, 2, 4, 0, 3).reshape(nt, gt * sw, tc * C_GROUP)
    r, i = pw_re[tc], pw_im[tc]
    rows = []
    for _ in range(max(n_log, 1)):
        rows += [jnp.concatenate([r, r], -1), jnp.concatenate([-i, i], -1)]
        r, i = r * r - i * i, 2.0 * r * i
    apow = jnp.stack(rows, axis=1)
    apow = apow.reshape(nt, gt, -1, sw).transpose(0, 2, 1, 3).reshape(nt, -1, gt * sw)
    return kc.astype(BF16), bc.astype(BF16), cc.astype(BF16), apow


def _s5(proj2d, ops, x0_re, x0_im, *, l):
    kc, bc, cc, apow = ops
    bsz = x0_re.shape[0]
    nt = C_GROUPS // S5_GROUP_BLOCK
    x0 = jnp.concatenate([x0_re, x0_im], axis=-1).reshape(bsz, nt, 1, -1).transpose(1, 0, 2, 3)
    bb = bsz if bsz * l <= 4096 else 1
    y, xf = _s5_chunks(proj2d, kc, bc, cc, apow, x0, l=l, bb=bb)
    xf = xf.transpose(1, 0, 2, 3).reshape(bsz, C_GROUPS, 2 * C_STATE)
    return y, xf[..., :C_STATE], xf[..., C_STATE:]


AB_SRC_GA = AB_QKV
AB_SRC_QKV = AB_SRC_GA + A_GATE_RANK
AB_SRC_TAIL = AB_SRC_QKV + B_QKV + B_VW
IN_AB = AB_SRC_TAIL + 2 * B_HEADS
IN_CD = CD_SMALL + 2 * D_HEADS
WPREP_AB_TILE = 512
WPREP_CD_TILE = 640


def _prep_w_ab_kernel(wt_ref, ga_ref, o_ref):
    j = pl.program_id(0)
    tile = o_ref.shape[1]
    n_small = A_GATE_RANK + 2 * B_HEADS

    @pl.when(j < AB_SMALL // tile)
    def _():
        o_ref[...] = wt_ref[...].T.astype(BF16)

    @pl.when(j == AB_SMALL // tile)
    def _():
        rows = jnp.concatenate([ga_ref[...], wt_ref[tile - 2 * B_HEADS:tile, :],
                                jnp.zeros((tile - n_small, wt_ref.shape[1]), F32)], axis=0)
        o_ref[...] = rows.T.astype(BF16)

    @pl.when(j > AB_SMALL // tile)
    def _():
        o_ref[...] = jnp.zeros(o_ref.shape, BF16)


def _prep_w_ab(w_t, n_out):
    n_in, d = w_t.shape
    tile = WPREP_AB_TILE
    assert AB_QKV % tile == 0 and AB_SMALL % tile == 0 and n_out % tile == 0 and n_in >= tile

    unit = A_GATE_RANK
    assert tile % unit == 0 and (AB_SRC_QKV - AB_QKV) % unit == 0 and (n_in - tile) % unit == 0

    def src_row(j):
        k = j * (tile // unit)
        k = jnp.where(j < AB_QKV // tile, k,
                      jnp.where(j < AB_SMALL // tile, k + (AB_SRC_QKV - AB_QKV) // unit, (n_in - tile) // unit))
        return unit * k

    return pl.pallas_call(
        _prep_w_ab_kernel,
        out_shape=jax.ShapeDtypeStruct((d, n_out), BF16),
        grid=(n_out // tile,),
        in_specs=[pl.BlockSpec((pl.Element(tile), pl.Element(d)), lambda j: (src_row(j), 0)),
                  pl.BlockSpec((pl.Element(A_GATE_RANK), pl.Element(d)), lambda j: (AB_SRC_GA, 0))],
        out_specs=pl.BlockSpec((d, tile), lambda j: (0, j)),
        compiler_params=pltpu.CompilerParams(
            dimension_semantics=("parallel",),
            vmem_limit_bytes=_vmem_limit(tile * d * 6, 2 * tile * d * 4)),
        name="prep_w_in_ab",
    )(w_t, w_t)


def _prep_w_cd_kernel(wt_ref, o_ref):
    tile = o_ref.shape[1]
    row = pl.program_id(0) * tile + lax.broadcasted_iota(jnp.int32, wt_ref.shape, 0)
    o_ref[...] = jnp.where(row < IN_CD, wt_ref[...], 0.0).T.astype(BF16)


def _prep_w_cd(w_t, n_out):
    n_in, d = w_t.shape
    tile = WPREP_CD_TILE
    assert n_out % tile == 0
    return pl.pallas_call(
        _prep_w_cd_kernel,
        out_shape=jax.ShapeDtypeStruct((d, n_out), BF16),
        grid=(n_out // tile,),
        in_specs=[pl.BlockSpec((tile, d), lambda j: (j, 0))],
        out_specs=pl.BlockSpec((d, tile), lambda j: (0, j)),
        compiler_params=pltpu.CompilerParams(
            dimension_semantics=("parallel",),
            vmem_limit_bytes=_vmem_limit(tile * d * 6, 2 * tile * d * 4)),
        name="prep_w_in_cd",
    )(w_t)


def _lane_row(vals, lane0):
    return jnp.zeros((1, LANES), F32).at[0, lane0:lane0 + vals.shape[0]].set(vals.astype(F32))


def _prepare_weights(norm_g, final_norm_g, w_in_ab, a_gate_w, a_gate_b, a_norm_g, b_conv_w, b_a_log,
                     b_dt_bias, b_norm_g, w_out_ab, w_in_cd, c_lam_re, c_lam_im, c_log_dt, c_b_re,
                     c_b_im, c_c_re, c_c_im, c_d, c_glu_w, c_glu_b, d_i_bias, d_f_bias, d_norm_g,
                     w_out_cd, n_log):
    assert w_in_ab.shape[1] == IN_AB and w_in_cd.shape[1] == IN_CD
    w_ab = _prep_w_ab(w_in_ab.astype(F32).T, _round_up(AB_SMALL + LANES, PROJ_TN))
    w_cd = _prep_w_cd(w_in_cd.astype(F32).T, _round_up(CD_SMALL + LANES, PROJ_TN))
    gate_w = jnp.zeros((LANES, A_KW), F32).at[AB_GA_LANE:AB_GA_LANE + A_GATE_RANK].set(
        a_gate_w.astype(F32)).astype(BF16)
    return dict(
        norm_g=norm_g.astype(F32), final_g=final_norm_g.astype(F32)[None, :],
        w_ab=w_ab, w_cd=w_cd, gate_w=gate_w, gate_b=a_gate_b.astype(F32)[None, :],
        a_norm_g=a_norm_g.astype(F32)[None, :], conv_w=b_conv_w.astype(F32),
        alog=_lane_row(b_a_log, AB_APRE_LANE), dtb=_lane_row(b_dt_bias, AB_APRE_LANE),
        b_norm_g=b_norm_g.astype(F32)[None, :],
        w_out_a=w_out_ab[:A_VW].astype(BF16), w_out_b=w_out_ab[A_VW:].astype(BF16),
        s5_ops=_s5_operators(c_lam_re.astype(F32), c_lam_im.astype(F32), c_log_dt, c_b_re.astype(F32),
                             c_b_im.astype(F32), c_c_re.astype(F32), c_c_im.astype(F32), n_log),
        c_d=c_d.astype(F32).reshape(1, C_W), glu_w=c_glu_w.astype(BF16),
        glu_b=c_glu_b.astype(F32)[None, :],
        ib=_lane_row(d_i_bias, CD_I_LANE), fb=_lane_row(d_f_bias, CD_F_LANE),
        d_norm_g=d_norm_g.astype(F32)[None, :],
        w_out_c=w_out_cd[:C_W].astype(BF16), w_out_d=w_out_cd[C_W:].astype(BF16),
    )


def _trunk(x, conv_prev, s_gla0, s_gdn0, s5_re0, s5_im0, mc0, mn0, mm0, w):
    bsz, l, d = x.shape
    c = min(CHUNK, l)
    tb = min(l, 8 * c)
    assert l % tb == 0 and l % S5_CHUNK == 0
    x2d = x.reshape(bsz * l, d)

    proj = _norm_matmul(x2d, w['norm_g'][0:1], w['w_ab'])
    proj3 = proj.reshape(bsz, l, proj.shape[1])
    bb = MIXER_STREAMS if bsz % MIXER_STREAMS == 0 else 1
    tb_s = min(l, (8 // bb) * c)
    o_a, s_gla = _gla(proj3, w['gate_w'], w['gate_b'], w['a_norm_g'], s_gla0.astype(F32), c=c, tb=tb_s, bb=bb)
    o_b, s_gdn = _gdn(proj3, conv_prev.astype(F32), w['conv_w'], w['alog'], w['dtb'], w['b_norm_g'],
                      s_gdn0.astype(F32), c=c, tb=tb_s, bb=bb)
    conv_new = proj3[:, l - (B_CONV - 1):, AB_QKV:AB_QKV + B_QKV]
    h1 = _out_proj(o_a.reshape(bsz * l, A_VW), o_b.reshape(bsz * l, B_VW), w['w_out_a'], w['w_out_b'], x2d)

    proj = _norm_matmul(h1, w['norm_g'][1:2], w['w_cd'])
    proj3 = proj.reshape(bsz, l, proj.shape[1])
    y, s5_re, s5_im = _s5(proj, w['s5_ops'], s5_re0.astype(F32), s5_im0.astype(F32), l=l)
    o_d, mc, mn, mm = _mlstm(proj3, w['ib'], w['fb'], w['d_norm_g'], mc0.astype(F32),
                             mn0.astype(F32), mm0.astype(F32)[:, None, :], c=c, tb=tb_s, bb=bb)
    y_out = _glu_out_proj_norm(y, proj, o_d.reshape(bsz * l, D_VW), w['c_d'], w['glu_w'], w['glu_b'],
                               w['w_out_c'], w['w_out_d'], h1, w['final_g'])
    dt = x.dtype
    return (y_out.reshape(bsz, l, d).astype(dt), conv_new.astype(dt), s_gla.astype(dt), s_gdn.astype(dt),
            s5_re.astype(dt), s5_im.astype(dt), mc.astype(dt), mn.astype(dt), mm[:, 0, :].astype(dt))


def kernel(x_prompt, x_sample, cache_gdn_conv, state_gla, state_gdn, state_s5_re, state_s5_im,
           state_mlstm_c, state_mlstm_n, state_mlstm_m, norm_g, final_norm_g, w_in_ab, a_gate_w,
           a_gate_b, a_norm_g, b_conv_w, b_a_log, b_dt_bias, b_norm_g, w_out_ab, w_in_cd, c_lam_re,
           c_lam_im, c_log_dt, c_b_re, c_b_im, c_c_re, c_c_im, c_d, c_glu_w, c_glu_b, d_i_bias,
           d_f_bias, d_norm_g, w_out_cd):
    n_log = int(math.log2(max(x_prompt.shape[1], x_sample.shape[1]) // S5_CHUNK))
    w = _prepare_weights(norm_g, final_norm_g, w_in_ab, a_gate_w, a_gate_b, a_norm_g, b_conv_w, b_a_log,
                         b_dt_bias, b_norm_g, w_out_ab, w_in_cd, c_lam_re, c_lam_im, c_log_dt, c_b_re,
                         c_b_im, c_c_re, c_c_im, c_d, c_glu_w, c_glu_b, d_i_bias, d_f_bias, d_norm_g,
                         w_out_cd, n_log)
    nb = x_prompt.shape[0]
    zeros = lambda *shape: jnp.zeros(shape, F32)
    p_out = _trunk(x_prompt, zeros(nb, B_CONV - 1, B_QKV), zeros(nb, A_HEADS, A_DK, A_DV),
                   zeros(nb, B_HEADS, B_DK, B_DV), zeros(nb, C_GROUPS, C_STATE), zeros(nb, C_GROUPS, C_STATE),
                   zeros(nb, D_HEADS, D_DK, D_DV), zeros(nb, D_HEADS, D_DK), zeros(nb, D_HEADS), w)
    s_out = _trunk(x_sample, cache_gdn_conv, state_gla, state_gdn, state_s5_re, state_s5_im,
                   state_mlstm_c, state_mlstm_n, state_mlstm_m, w)
    return (p_out[0], s_out[0]) + tuple(p_out[1:]) + tuple(s_out[1:])
```

```python
import functools
import math

import jax
import jax.numpy as jnp
from jax import lax
from jax.experimental import pallas as pl
from jax.experimental.pallas import tpu as pltpu

F32 = jnp.float32
BF16 = jnp.bfloat16
HIGHEST = lax.Precision.HIGHEST

NORM_EPS = 1e-6
CHUNK = 64
A_HEADS, A_DK, A_DV, A_GATE_RANK, A_GATE_TAU = 4, 128, 256, 16, 16.0
B_HEADS, B_DK, B_DV, B_CONV = 8, 128, 128, 4
C_GROUP, C_GROUPS, C_STATE = 16, 64, 64
D_HEADS, D_DK, D_DV = 4, 128, 256
A_KW, A_VW = A_HEADS * A_DK, A_HEADS * A_DV
B_KW, B_VW = B_HEADS * B_DK, B_HEADS * B_DV
B_QKV = 2 * B_KW + B_VW
C_W = C_GROUPS * C_GROUP
D_KW, D_VW = D_HEADS * D_DK, D_HEADS * D_DV

LANES = 128
SUBLANES = 8
VMEM_BYTES_V7X = 64 * 1024 * 1024

AB_Q, AB_K, AB_V, AB_Z = 0, A_KW, 2 * A_KW, 2 * A_KW + A_VW
AB_QKV = AB_Z + A_VW
AB_ZB = AB_QKV + B_QKV
AB_SMALL = AB_ZB + B_VW
AB_GA_LANE, AB_BETA_LANE, AB_APRE_LANE = 0, A_GATE_RANK, A_GATE_RANK + B_HEADS
CD_U, CD_Z = 0, C_W
CD_Q = 2 * C_W
CD_K = CD_Q + D_KW
CD_V = CD_K + D_KW
CD_O = CD_V + D_VW
CD_ZD = CD_O + D_VW
CD_SMALL = CD_ZD + D_VW
CD_I_LANE, CD_F_LANE = 0, D_HEADS

PROJ_TN = 1280
PROJ_TILE = 1024
S5_CHUNK = 16
S5_GROUP_BLOCK = 8
MIXER_STREAMS = 2
MLSTM_INTRA_CHUNKS = 4
GDN_WY_CHAINS = 32


def _round_up(x, m):
    return (x + m - 1) // m * m


def _vmem_limit(block_bytes, scratch_bytes=0):
    est = 2 * block_bytes + scratch_bytes
    return int(min(max(2 * est, 32 * 1024 * 1024), VMEM_BYTES_V7X - 8 * 1024 * 1024))


def _mm(a, b):
    return jnp.dot(a, b, preferred_element_type=F32)


def _dot(a, b):
    return _mm(a.astype(BF16), b.astype(BF16))


def _dot_nt(a, b):
    return lax.dot_general(a.astype(BF16), b.astype(BF16), (((1,), (1,)), ((), ())),
                           preferred_element_type=F32)


def _dot_tn(a, b):
    return lax.dot_general(a.astype(BF16), b.astype(BF16), (((0,), (0,)), ((), ())),
                           preferred_element_type=F32)


def _split2(x):
    hi = x.astype(BF16)
    return hi, (x - hi.astype(F32)).astype(BF16)


def _split3(x):
    hi = x.astype(BF16)
    r = x - hi.astype(F32)
    mid = r.astype(BF16)
    return hi, mid, (r - mid.astype(F32)).astype(BF16)


def _dot3(a, b):
    ah, al = _split2(a)
    bh, bl = _split2(b)
    return _mm(ah, bh) + _mm(ah, bl) + _mm(al, bh)


def _cumsum_rows(tri_bf16, x):
    hi, mid, lo = _split3(x)
    return _mm(tri_bf16, hi) + _mm(tri_bf16, mid) + _mm(tri_bf16, lo)


def _select_rows(sel_bf16, x):
    nt = lambda b: lax.dot_general(sel_bf16, b, (((1,), (1,)), ((), ())), preferred_element_type=F32)
    hi, mid, lo = _split3(x)
    return nt(hi) + nt(mid) + nt(lo)


def _lane_selector(lane0):
    r = lax.broadcasted_iota(jnp.int32, (SUBLANES, LANES), 0)
    l = lax.broadcasted_iota(jnp.int32, (SUBLANES, LANES), 1)
    return jnp.where(l == r + lane0, 1.0, 0.0).astype(BF16)


def _causal_masks(c):
    row = lax.broadcasted_iota(jnp.int32, (c, c), 0)
    col = lax.broadcasted_iota(jnp.int32, (c, c), 1)
    return row >= col, row > col


def _norm_matmul_kernel(x_ref, g_ref, w_ref, o_ref, xn_ref):
    @pl.when(pl.program_id(1) == 0)
    def _():
        x = x_ref[...]
        y = x * lax.rsqrt(jnp.mean(x * x, axis=-1, keepdims=True) + NORM_EPS) * g_ref[...]
        xn_ref[...] = y.astype(BF16)

    o_ref[...] = jnp.dot(xn_ref[...], w_ref[...], preferred_element_type=F32)


def _norm_matmul(x2d, g_row, w_bf16):
    m, d = x2d.shape
    n = w_bf16.shape[1]
    tm = min(m, 1024)
    tn = PROJ_TN
    assert m % tm == 0 and n % tn == 0
    blocks = tm * d * 4 + d * tn * 2 + tm * tn * 4
    return pl.pallas_call(
        _norm_matmul_kernel,
        out_shape=jax.ShapeDtypeStruct((m, n), F32),
        grid=(m // tm, n // tn),
        in_specs=[pl.BlockSpec((tm, d), lambda i, j: (i, 0)),
                  pl.BlockSpec((1, d), lambda i, j: (0, 0)),
                  pl.BlockSpec((d, tn), lambda i, j: (0, j))],
        out_specs=pl.BlockSpec((tm, tn), lambda i, j: (i, j)),
        scratch_shapes=[pltpu.VMEM((tm, d), BF16)],
        compiler_params=pltpu.CompilerParams(
            dimension_semantics=("parallel", "arbitrary"),
            vmem_limit_bytes=_vmem_limit(blocks, tm * d * 2)),
        name="norm_in_proj",
    )(x2d, g_row, w_bf16)


def _rmsnorm_cast_kernel(x_ref, g_ref, o_ref):
    x = x_ref[...]
    y = x * lax.rsqrt(jnp.mean(x * x, axis=-1, keepdims=True) + NORM_EPS) * g_ref[...]
    o_ref[...] = y.astype(BF16)


def _rmsnorm_cast(x2d, g_row):
    m, d = x2d.shape
    tm = min(m, 1024)
    assert m % tm == 0
    return pl.pallas_call(
        _rmsnorm_cast_kernel,
        out_shape=jax.ShapeDtypeStruct((m, d), BF16),
        grid=(m // tm,),
        in_specs=[pl.BlockSpec((tm, d), lambda i: (i, 0)), pl.BlockSpec((1, d), lambda i: (0, 0))],
        out_specs=pl.BlockSpec((tm, d), lambda i: (i, 0)),
        compiler_params=pltpu.CompilerParams(
            dimension_semantics=("parallel",),
            vmem_limit_bytes=_vmem_limit(tm * d * 6, tm * d * 4)),
        name="rmsnorm_cast",
    )(x2d, g_row)


def _proj_kernel(xn_ref, wt_ref, head_ref, o_ref, w_ref, *, n_full, head_rows, tail_rows):
    j = pl.program_id(0)
    i = pl.program_id(1)
    tile, k = wt_ref.shape

    @pl.when((i == 0) & (j < n_full))
    def _():
        w_ref[...] = wt_ref[...].T.astype(BF16)

    @pl.when((i == 0) & (j == n_full))
    def _():
        parts = [head_ref[0:head_rows, :]] if head_rows else []
        parts += [wt_ref[tile - tail_rows:tile, :], jnp.zeros((LANES - head_rows - tail_rows, k), F32)]
        w_ref[:, 0:LANES] = jnp.concatenate(parts, axis=0).T.astype(BF16)

    @pl.when(j < n_full)
    def _():
        o_ref[...] = jnp.dot(xn_ref[...], w_ref[...], preferred_element_type=F32)

    @pl.when(j == n_full)
    def _():
        o_ref[:, 0:LANES] = jnp.dot(xn_ref[...], w_ref[:, 0:LANES], preferred_element_type=F32)
        o_ref[:, LANES:] = jnp.zeros((o_ref.shape[0], tile - LANES), F32)


def _proj(xn, w_t, *, n_aligned, shift_from, shift, head_row, head_rows, tail_rows):
    m, d = xn.shape
    n_in = w_t.shape[0]
    tm = min(m, 1024)
    tile = PROJ_TILE
    unit = SUBLANES
    n_full = n_aligned // tile
    assert m % tm == 0 and n_aligned % tile == 0 and shift_from % tile == 0
    assert shift % unit == 0 and (n_in - tile) % unit == 0 and n_in >= tile and head_row % unit == 0

    def src_row(j, i):
        k = j * (tile // unit)
        k = jnp.where(j < shift_from // tile, k, jnp.where(j < n_full, k + shift // unit, (n_in - tile) // unit))
        return unit * k, 0

    blocks = tm * d * 2 + tile * d * 4 + tm * tile * 4 + 2 * SUBLANES * d * 4
    return pl.pallas_call(
        functools.partial(_proj_kernel, n_full=n_full, head_rows=head_rows, tail_rows=tail_rows),
        out_shape=jax.ShapeDtypeStruct((m, n_aligned + LANES), F32),
        grid=(n_full + 1, m // tm),
        in_specs=[pl.BlockSpec((tm, d), lambda j, i: (i, 0)),
                  pl.BlockSpec((pl.Element(tile), pl.Element(d)), src_row),
                  pl.BlockSpec((pl.Element(2 * SUBLANES), pl.Element(d)), lambda j, i: (head_row, 0))],
        out_specs=pl.BlockSpec((tm, tile), lambda j, i: (i, j)),
        scratch_shapes=[pltpu.VMEM((d, tile), BF16)],
        compiler_params=pltpu.CompilerParams(
            dimension_semantics=("arbitrary", "arbitrary"),
            vmem_limit_bytes=_vmem_limit(blocks, d * tile * 2 + tile * d * 4)),
        name="in_proj",
    )(xn, w_t, w_t)


def _out_proj_kernel(a_ref, b_ref, wa_ref, wb_ref, h_ref, g_ref, o_ref, xn_ref):
    out = (jnp.dot(a_ref[...], wa_ref[...], preferred_element_type=F32)
           + jnp.dot(b_ref[...], wb_ref[...], preferred_element_type=F32))
    h = h_ref[...] + out
    o_ref[...] = h
    xn_ref[...] = (h * lax.rsqrt(jnp.mean(h * h, axis=-1, keepdims=True) + NORM_EPS) * g_ref[...]).astype(BF16)


def _glu_out_proj_norm_kernel(y_ref, u_ref, z_ref, od_ref, d_ref, gw_ref, gb_ref, wc_ref, wd_ref, h_ref, g_ref,
                              o_ref):
    y = jax.nn.gelu(y_ref[...] + d_ref[...] * u_ref[...])
    gate = jax.nn.sigmoid(jnp.dot(y.astype(BF16), gw_ref[...], preferred_element_type=F32) + gb_ref[...])
    o_c = (y * gate * jax.nn.silu(z_ref[...])).astype(BF16)
    out = (jnp.dot(o_c, wc_ref[...], preferred_element_type=F32)
           + jnp.dot(od_ref[...], wd_ref[...], preferred_element_type=F32))
    h = h_ref[...] + out
    o_ref[...] = h * lax.rsqrt(jnp.mean(h * h, axis=-1, keepdims=True) + NORM_EPS) * g_ref[...]


def _glu_out_proj_norm(y2d, proj2d, o_d, d_row, glu_w, glu_b_row, w_c, w_d, h2d, final_g_row):
    m, d = h2d.shape
    tm = min(m, 512)
    assert m % tm == 0
    row = lambda col: (lambda i: (i, col))
    const = lambda i: (0, 0)
    resident = pl.Buffered(1)
    blocks = 3 * tm * C_W * 4 + tm * D_VW * 2 + 2 * tm * d * 4
    weights = C_W * C_W * 2 + (C_W + D_VW) * d * 2
    return pl.pallas_call(
        _glu_out_proj_norm_kernel,
        out_shape=jax.ShapeDtypeStruct((m, d), F32),
        grid=(m // tm,),
        in_specs=[pl.BlockSpec((tm, C_W), row(0)),
                  pl.BlockSpec((tm, C_W), row(CD_U // C_W)),
                  pl.BlockSpec((tm, C_W), row(CD_Z // C_W)),
                  pl.BlockSpec((tm, D_VW), row(0)),
                  pl.BlockSpec((1, C_W), const),
                  pl.BlockSpec((C_W, C_W), const, pipeline_mode=resident),
                  pl.BlockSpec((1, C_W), const),
                  pl.BlockSpec((C_W, d), const, pipeline_mode=resident),
                  pl.BlockSpec((D_VW, d), const, pipeline_mode=resident),
                  pl.BlockSpec((tm, d), row(0)),
                  pl.BlockSpec((1, d), const)],
        out_specs=pl.BlockSpec((tm, d), row(0)),
        compiler_params=pltpu.CompilerParams(
            dimension_semantics=("parallel",),
            vmem_limit_bytes=_vmem_limit(blocks, weights + 3 * tm * C_W * 4)),
        name="glu_out_proj_norm",
    )(y2d, proj2d, proj2d, o_d, d_row, glu_w, glu_b_row, w_c, w_d, h2d, final_g_row)


def _out_proj(mix_a, mix_b, w_a, w_b, h2d, next_g_row):
    m, d = h2d.shape
    ka, kb = mix_a.shape[1], mix_b.shape[1]
    tm = min(m, 512)
    assert m % tm == 0
    blocks = tm * (ka + kb) * 2 + (ka + kb) * d * 2 + 2 * tm * d * 4 + tm * d * 2
    return pl.pallas_call(
        _out_proj_kernel,
        out_shape=[jax.ShapeDtypeStruct((m, d), F32), jax.ShapeDtypeStruct((m, d), BF16)],
        grid=(m // tm,),
        in_specs=[pl.BlockSpec((tm, ka), lambda i: (i, 0)),
                  pl.BlockSpec((tm, kb), lambda i: (i, 0)),
                  pl.BlockSpec((ka, d), lambda i: (0, 0)),
                  pl.BlockSpec((kb, d), lambda i: (0, 0)),
                  pl.BlockSpec((tm, d), lambda i: (i, 0)),
                  pl.BlockSpec((1, d), lambda i: (0, 0))],
        out_specs=[pl.BlockSpec((tm, d), lambda i: (i, 0)), pl.BlockSpec((tm, d), lambda i: (i, 0))],
        compiler_params=pltpu.CompilerParams(
            dimension_semantics=("parallel",),
            vmem_limit_bytes=_vmem_limit(blocks)),
        name="out_proj",
    )(mix_a, mix_b, w_a, w_b, h2d, next_g_row)


def _gla_kernel(q_ref, k_ref, v_ref, z_ref, sm_ref, gw_ref, gb_ref, ng_ref, s0_ref,
                o_ref, sout_ref, st_ref, *, c, nchunks):
    t = pl.program_id(1)
    last_t = pl.num_programs(1) - 1
    bb = q_ref.shape[0]
    chains = [(bi, h) for bi in range(bb) for h in range(A_HEADS)]
    ksl = [slice(h * A_DK, (h + 1) * A_DK) for _, h in chains]
    vsl = [slice(h * A_DV, (h + 1) * A_DV) for _, h in chains]
    idx = range(len(chains))

    @pl.when(t == 0)
    def _():
        for bi, h in chains:
            st_ref[bi, h] = s0_ref[bi, h].T

    causal, _ = _causal_masks(c)
    tri = causal.astype(BF16)
    gw = gw_ref[...]
    gb = gb_ref[...]

    def body(n, carry):
        sl = pl.ds(pl.multiple_of(n * c, c), c)
        b_all = [_cumsum_rows(tri, jax.nn.log_sigmoid(_dot(sm_ref[bi, sl, :], gw) + gb) * (1.0 / A_GATE_TAU))
                 for bi in range(bb)]
        b = [b_all[chains[i][0]][:, ksl[i]] for i in idx]
        b_last = [b[i][c - 1:c, :] for i in idx]
        k = [k_ref[chains[i][0], sl, ksl[i]] for i in idx]
        v = [v_ref[chains[i][0], sl, vsl[i]].astype(BF16) for i in idx]
        q_dec = [(q_ref[chains[i][0], sl, ksl[i]] * (A_DK ** -0.5) * jnp.exp(b[i])).astype(BF16) for i in idx]
        k_dec = [(k[i] * jnp.exp(-b[i])).astype(BF16) for i in idx]
        k_w = [(k[i] * jnp.exp(b_last[i] - b[i])).astype(BF16) for i in idx]
        scores = [jnp.where(causal, _dot_nt(q_dec[i], k_dec[i]), 0.0).astype(BF16) for i in idx]
        s_t = [st_ref[bi, h] for bi, h in chains]
        outs = [_mm(scores[i], v[i]) + _dot_nt(q_dec[i], s_t[i]) for i in idx]
        for i, (bi, h) in enumerate(chains):
            st_ref[bi, h] = s_t[i] * jnp.exp(b_last[i]) + _dot_tn(v[i], k_w[i])
        for i, (bi, h) in enumerate(chains):
            o = outs[i]
            o = o * lax.rsqrt(jnp.mean(o * o, axis=-1, keepdims=True) + NORM_EPS) * ng_ref[:, vsl[i]]
            o_ref[bi, sl, vsl[i]] = (o * jax.nn.silu(z_ref[bi, sl, vsl[i]])).astype(o_ref.dtype)
        return carry

    lax.fori_loop(0, nchunks, body, 0)

    @pl.when(t == last_t)
    def _():
        for bi, h in chains:
            sout_ref[bi, h] = st_ref[bi, h].T


def _gla(proj3, gate_w_pad, gate_b_row, norm_g_row, s0, *, c, tb, bb):
    bsz, l, _ = proj3.shape
    nblk = l // tb
    assert bsz % bb == 0
    tok = lambda col: (lambda b, t: (b, t, col))
    in_specs = [
        pl.BlockSpec((bb, tb, A_KW), tok(AB_Q // A_KW)),
        pl.BlockSpec((bb, tb, A_KW), tok(AB_K // A_KW)),
        pl.BlockSpec((bb, tb, A_VW), tok(AB_V // A_VW)),
        pl.BlockSpec((bb, tb, A_VW), tok(AB_Z // A_VW)),
        pl.BlockSpec((bb, tb, LANES), tok(AB_SMALL // LANES)),
        pl.BlockSpec((LANES, A_KW), lambda b, t: (0, 0)),
        pl.BlockSpec((1, A_KW), lambda b, t: (0, 0)),
        pl.BlockSpec((1, A_VW), lambda b, t: (0, 0)),
        pl.BlockSpec((bb, A_HEADS, A_DK, A_DV), lambda b, t: (b, 0, 0, 0)),
    ]
    out_specs = [
        pl.BlockSpec((bb, tb, A_VW), lambda b, t: (b, t, 0)),
        pl.BlockSpec((bb, A_HEADS, A_DK, A_DV), lambda b, t: (b, 0, 0, 0)),
    ]
    state = bb * A_HEADS * A_DK * A_DV * 4
    blocks = bb * tb * (2 * A_KW + 2 * A_VW + LANES) * 4 + bb * tb * A_VW * 2 + 2 * state
    return pl.pallas_call(
        functools.partial(_gla_kernel, c=c, nchunks=tb // c),
        out_shape=[jax.ShapeDtypeStruct((bsz, l, A_VW), BF16),
                   jax.ShapeDtypeStruct((bsz, A_HEADS, A_DK, A_DV), F32)],
        grid=(bsz // bb, nblk),
        in_specs=in_specs,
        out_specs=out_specs,
        scratch_shapes=[pltpu.VMEM((bb, A_HEADS, A_DV, A_DK), F32)],
        compiler_params=pltpu.CompilerParams(
            dimension_semantics=("parallel", "arbitrary"),
            vmem_limit_bytes=_vmem_limit(blocks, state)),
        name="gla_mixer",
    )(proj3, proj3, proj3, proj3, proj3, gate_w_pad, gate_b_row, norm_g_row, s0)


def _gdn_kernel(x_ref, z_ref, sm_ref, w_ref, cp_ref, alog_ref, dtb_ref, ng_ref, s0_ref,
                o_ref, sout_ref, s_ref, tail_ref, u_ref, wm_ref, qg_ref, kg_ref, qk_ref, gl_ref,
                *, c, nchunks):
    t = pl.program_id(1)
    last_t = pl.num_programs(1) - 1
    tb = c * nchunks
    keep = SUBLANES - (B_CONV - 1)
    bb = x_ref.shape[0]

    @pl.when(t == 0)
    def _():
        s_ref[...] = s0_ref[...]
        tail_ref[:, 0:keep, :] = jnp.zeros((bb, keep, B_QKV), F32)
        tail_ref[:, keep:SUBLANES, :] = cp_ref[...]

    def conv_silu(bi, row0, first, cols):
        x = x_ref[bi, pl.ds(row0, c), cols]
        if first is None:
            prev = x_ref[bi, pl.ds(pl.multiple_of(row0 - SUBLANES, SUBLANES), SUBLANES), cols]
        else:
            before = pl.multiple_of(jnp.maximum(row0 - SUBLANES, 0), SUBLANES)
            prev = jnp.where(first, tail_ref[bi, :, cols], x_ref[bi, pl.ds(before, SUBLANES), cols])
        w = w_ref[:, cols]
        ext = jnp.concatenate([prev, x], axis=0)
        ext1 = pltpu.roll(ext, 1, 0)
        newer = ext * w[3:4, :] + ext1 * w[2:3, :]
        older = ext * w[1:2, :] + ext1 * w[0:1, :]
        conv = newer + pltpu.roll(older, 2, 0)
        return jax.nn.silu(conv[SUBLANES:SUBLANES + c, :])

    causal, strict = _causal_masks(c)
    tri = causal.astype(BF16)
    eye = jnp.where(causal & jnp.logical_not(strict), 1.0, 0.0).astype(F32)
    sel = _lane_selector(AB_APRE_LANE)
    neg_a_exp = -jnp.exp(alog_ref[...])
    dtb = dtb_ref[...]
    ng = ng_ref[...]
    n_double = int(math.log2(c)) - 1
    heads = range(B_HEADS)

    group = max(1, GDN_WY_CHAINS // (bb * B_HEADS))
    group = group if nchunks % group == 0 else 1

    def wy_factors(n, carry):
        pws, rhss, where = [], [], []
        for bi, ci in [(bi, ci) for ci in range(group) for bi in range(bb)]:
            row0 = pl.multiple_of((n * group + ci) * c, c)
            first = (n == 0) if ci == 0 else None
            sl = pl.ds(row0, c)
            sm = sm_ref[bi, sl, :]
            g_cum = _cumsum_rows(tri, neg_a_exp * jax.nn.softplus(sm + dtb))
            g_rows = _select_rows(sel, g_cum)
            beta_all = jax.nn.sigmoid(sm)
            gl_ref[bi, pl.ds(n * group + ci, 1), :] = g_cum[c - 1:c, :]
            for h in heads:
                hs = slice(h * B_DK, (h + 1) * B_DK)
                q = conv_silu(bi, row0, first, hs)
                k = conv_silu(bi, row0, first, slice(B_KW + h * B_DK, B_KW + (h + 1) * B_DK))
                v = conv_silu(bi, row0, first, slice(2 * B_KW + h * B_DV, 2 * B_KW + (h + 1) * B_DV))
                q = q * lax.rsqrt(jnp.sum(q * q, axis=-1, keepdims=True) + NORM_EPS) * (B_DK ** -0.5)
                k = k * lax.rsqrt(jnp.sum(k * k, axis=-1, keepdims=True) + NORM_EPS)
                g_col = g_cum[:, AB_APRE_LANE + h:AB_APRE_LANE + h + 1]
                beta = beta_all[:, AB_BETA_LANE + h:AB_BETA_LANE + h + 1]
                decay = jnp.exp(jnp.where(causal, g_col - g_rows[h:h + 1, :], -jnp.inf))
                e_g = jnp.exp(g_col)
                k_beta = k * beta
                pws.append(-jnp.where(strict, _dot_nt(k_beta, k) * decay, 0.0))
                rhss.append(jnp.concatenate([v * beta, k_beta * e_g], axis=1))
                where.append((bi, sl, hs))
                qk_ref[bi, sl, h * LANES:h * LANES + c] = jnp.where(causal, _dot_nt(q, k) * decay,
                                                                    0.0).astype(BF16)
                qg_ref[bi, sl, hs] = (q * e_g).astype(BF16)
                kg_ref[bi, sl, hs] = (k * jnp.exp(g_col[c - 1:c, :] - g_col)).astype(BF16)
        items = range(len(pws))
        neg_lower = pws
        invs = [eye + p for p in pws]
        for _ in range(n_double):
            pws = [_dot(p, p) for p in pws]
            invs = [i + _dot(i, p) for i, p in zip(invs, pws)]
        invs = [i.astype(BF16) for i in invs]
        sol = [_mm(invs[i], rhss[i].astype(BF16)) for i in items]
        resid = [rhss[i] - sol[i] + _dot3(neg_lower[i], sol[i]) for i in items]
        for i in items:
            bi, sl, hs = where[i]
            uw = sol[i] + _mm(invs[i], resid[i].astype(BF16))
            u_ref[bi, sl, hs] = uw[:, :B_DV]
            wm_ref[bi, sl, hs] = uw[:, B_DV:].astype(BF16)
        return carry

    lax.fori_loop(0, nchunks // group, wy_factors, 0)
    tail_ref[...] = x_ref[:, tb - SUBLANES:tb, :]

    chains = [(bi, h) for bi in range(bb) for h in heads]
    hsl = [slice(h * B_DK, (h + 1) * B_DK) for _, h in chains]

    def recurrence(n, carry):
        sl = pl.ds(pl.multiple_of(n * c, c), c)
        e_last = [jnp.exp(gl_ref[bi, pl.ds(n, 1), :]) for bi in range(bb)]
        s_old = [s_ref[bi, h] for bi, h in chains]
        s_bf = [s.astype(BF16) for s in s_old]
        v_new = [(u_ref[bi, sl, hsl[i]] - _mm(wm_ref[bi, sl, hsl[i]], s_bf[i])).astype(BF16)
                 for i, (bi, h) in enumerate(chains)]
        outs = [_mm(qg_ref[bi, sl, hsl[i]], s_bf[i]) + _mm(qk_ref[bi, sl, h * LANES:h * LANES + c], v_new[i])
                for i, (bi, h) in enumerate(chains)]
        for i, (bi, h) in enumerate(chains):
            s_ref[bi, h] = (e_last[bi][:, AB_APRE_LANE + h:AB_APRE_LANE + h + 1] * s_old[i]
                            + lax.dot_general(kg_ref[bi, sl, hsl[i]], v_new[i], (((0,), (0,)), ((), ())),
                                              preferred_element_type=F32))
        for i, (bi, h) in enumerate(chains):
            o = outs[i]
            o = o * lax.rsqrt(jnp.mean(o * o, axis=-1, keepdims=True) + NORM_EPS) * ng
            o_ref[bi, sl, hsl[i]] = (o * jax.nn.silu(z_ref[bi, sl, hsl[i]])).astype(o_ref.dtype)
        return carry

    lax.fori_loop(0, nchunks, recurrence, 0)

    @pl.when(t == last_t)
    def _():
        sout_ref[...] = s_ref[...]


def _gdn(proj3, conv_prev, conv_w, alog_row, dtb_row, norm_g_row, s0, *, c, tb, bb):
    bsz, l, _ = proj3.shape
    nblk = l // tb
    assert AB_QKV % B_QKV == 0 and AB_ZB % B_VW == 0 and bsz % bb == 0
    in_specs = [
        pl.BlockSpec((bb, tb, B_QKV), lambda b, t: (b, t, AB_QKV // B_QKV)),
        pl.BlockSpec((bb, tb, B_VW), lambda b, t: (b, t, AB_ZB // B_VW)),
        pl.BlockSpec((bb, tb, LANES), lambda b, t: (b, t, AB_SMALL // LANES)),
        pl.BlockSpec((B_CONV, B_QKV), lambda b, t: (0, 0)),
        pl.BlockSpec((bb, B_CONV - 1, B_QKV), lambda b, t: (b, 0, 0)),
        pl.BlockSpec((1, LANES), lambda b, t: (0, 0)),
        pl.BlockSpec((1, LANES), lambda b, t: (0, 0)),
        pl.BlockSpec((1, B_DV), lambda b, t: (0, 0)),
        pl.BlockSpec((bb, B_HEADS, B_DK, B_DV), lambda b, t: (b, 0, 0, 0)),
    ]
    out_specs = [
        pl.BlockSpec((bb, tb, B_VW), lambda b, t: (b, t, 0)),
        pl.BlockSpec((bb, B_HEADS, B_DK, B_DV), lambda b, t: (b, 0, 0, 0)),
    ]
    blocks = bb * (tb * (B_QKV + B_VW + LANES) * 4 + tb * B_VW * 2 + 2 * B_HEADS * B_DK * B_DV * 4)
    scratch = bb * (B_HEADS * B_DK * B_DV * 4 + SUBLANES * B_QKV * 4 + tb * B_VW * 4
                    + 4 * tb * B_KW * 2 + SUBLANES * LANES * 4)
    return pl.pallas_call(
        functools.partial(_gdn_kernel, c=c, nchunks=tb // c),
        out_shape=[jax.ShapeDtypeStruct((bsz, l, B_VW), BF16),
                   jax.ShapeDtypeStruct((bsz, B_HEADS, B_DK, B_DV), F32)],
        grid=(bsz // bb, nblk),
        in_specs=in_specs,
        out_specs=out_specs,
        scratch_shapes=[pltpu.VMEM((bb, B_HEADS, B_DK, B_DV), F32),
                        pltpu.VMEM((bb, SUBLANES, B_QKV), F32),
                        pltpu.VMEM((bb, tb, B_VW), F32),
                        pltpu.VMEM((bb, tb, B_KW), BF16),
                        pltpu.VMEM((bb, tb, B_KW), BF16),
                        pltpu.VMEM((bb, tb, B_KW), BF16),
                        pltpu.VMEM((bb, tb, B_HEADS * LANES), BF16),
                        pltpu.VMEM((bb, SUBLANES, LANES), F32)],
        compiler_params=pltpu.CompilerParams(
            dimension_semantics=("parallel", "arbitrary"),
            vmem_limit_bytes=_vmem_limit(blocks, scratch)),
        name="gdn_mixer",
    )(proj3, proj3, proj3, conv_w, conv_prev, alog_row, dtb_row, norm_g_row, s0)


def _mlstm_kernel(q_ref, k_ref, v_ref, og_ref, z_ref, sm_ref, ib_ref, fb_ref, ng_ref,
                  c0_ref, n0_ref, m0_ref, o_ref, cout_ref, nout_ref, mout_ref,
                  c_ref, n_ref, m_ref, hi_ref, mi_ref, ni_ref, bc_ref, kv_ref, ks_ref, mc_ref, bl_ref,
                  *, c, nchunks):
    t = pl.program_id(1)
    last_t = pl.num_programs(1) - 1
    bb = q_ref.shape[0]
    chains = [(bi, h) for bi in range(bb) for h in range(D_HEADS)]
    ksl = [slice(h * D_DK, (h + 1) * D_DK) for _, h in chains]
    vsl = [slice(h * D_DV, (h + 1) * D_DV) for _, h in chains]
    idx = range(len(chains))

    @pl.when(t == 0)
    def _():
        c_ref[...] = c0_ref[...]
        n_ref[...] = n0_ref[...]
        for bi, h in chains:
            m_ref[bi, h] = jnp.broadcast_to(m0_ref[bi, :, h:h + 1], (1, LANES))

    causal, _ = _causal_masks(c)
    tri = causal.astype(BF16)
    sel = _lane_selector(0)
    lane = lax.broadcasted_iota(jnp.int32, (c, LANES), 1)
    ib = ib_ref[...]
    fb = fb_ref[...]

    group = MLSTM_INTRA_CHUNKS if nchunks % MLSTM_INTRA_CHUNKS == 0 else 1

    def intra(n, carry):
        items = [(bi, ci, h) for ci in range(group) for bi, h in chains]
        sls = [pl.ds(pl.multiple_of((n * group + ci) * c, c), c) for ci in range(group)]
        sm = {(bi, ci): sm_ref[bi, sls[ci], :] for ci in range(group) for bi in range(bb)}
        i_full = {key: x + ib for key, x in sm.items()}
        b_full = {key: _cumsum_rows(tri, jax.nn.log_sigmoid(x + fb)) for key, x in sm.items()}
        rows = {key: _select_rows(sel, jnp.where(lane < CD_F_LANE, i_full[key], b_full[key]))
                for key in sm}
        b_col = [b_full[bi, ci][:, CD_F_LANE + h:CD_F_LANE + h + 1] for bi, ci, h in items]
        i_col = [i_full[bi, ci][:, CD_I_LANE + h:CD_I_LANE + h + 1] for bi, ci, h in items]
        logw = [jnp.where(causal, b_col[i] - rows[bi, ci][CD_F_LANE + h:CD_F_LANE + h + 1, :]
                          + rows[bi, ci][CD_I_LANE + h:CD_I_LANE + h + 1, :], -jnp.inf)
                for i, (bi, ci, h) in enumerate(items)]
        ids = range(len(items))
        m_intra = [jnp.max(logw[i], axis=-1, keepdims=True) for i in ids]
        ks_ = [slice(h * D_DK, (h + 1) * D_DK) for _, _, h in items]
        vs_ = [slice(h * D_DV, (h + 1) * D_DV) for _, _, h in items]
        q_bf = [(q_ref[bi, sls[ci], ks_[i]] * (D_DK ** -0.5)).astype(BF16) for i, (bi, ci, h) in enumerate(items)]
        k = [k_ref[bi, sls[ci], ks_[i]] for i, (bi, ci, h) in enumerate(items)]
        v = [v_ref[bi, sls[ci], vs_[i]].astype(BF16) for i, (bi, ci, h) in enumerate(items)]
        p = [jnp.exp(logw[i] - m_intra[i]) * _dot_nt(q_bf[i], k[i]) for i in ids]
        n_intra = [jnp.sum(p[i], axis=-1, keepdims=True) for i in ids]
        m_chunk = [m_intra[i][c - 1:c, :] for i in ids]
        k_w = [k[i] * jnp.exp(b_col[i][c - 1:c, :] - b_col[i] + i_col[i] - m_chunk[i]) for i in ids]
        rep = lambda x: jnp.broadcast_to(x, (x.shape[0], LANES))
        for i, (bi, ci, h) in enumerate(items):
            row = pl.ds(n * group + ci, 1)
            hi_ref[bi, sls[ci], vs_[i]] = _dot(p[i], v[i])
            kv_ref[bi, n * group + ci, h] = _dot_tn(k_w[i], v[i])
            ks_ref[bi, h, row, :] = jnp.sum(k_w[i], axis=0, keepdims=True)
            mi_ref[bi, h, sls[ci], :] = rep(m_intra[i])
            ni_ref[bi, h, sls[ci], :] = rep(n_intra[i])
            bc_ref[bi, h, sls[ci], :] = rep(b_col[i])
            mc_ref[bi, h, row, :] = rep(m_chunk[i])
            bl_ref[bi, h, row, :] = rep(b_col[i][c - 1:c, :])
        return carry

    lax.fori_loop(0, nchunks // group, intra, 0)

    twice = lambda x: jnp.concatenate([x, x], axis=1)

    def body(n, carry):
        sl = pl.ds(pl.multiple_of(n * c, c), c)
        row = pl.ds(n, 1)
        q = [q_ref[chains[i][0], sl, ksl[i]] * (D_DK ** -0.5) for i in idx]
        c_s = [c_ref[bi, h] for bi, h in chains]
        n_s = [n_ref[bi, h:h + 1, :] for bi, h in chains]
        m_s = [m_ref[bi, h] for bi, h in chains]
        qc = [_dot(q[i], c_s[i]) for i in idx]
        b_last = [bl_ref[bi, h, row, :] for bi, h in chains]
        m_chunk = [mc_ref[bi, h, row, :] for bi, h in chains]
        for i, (bi, h) in enumerate(chains):
            m_new = jnp.maximum(b_last[i] + m_s[i], m_chunk[i])
            w_old = jnp.exp(b_last[i] + m_s[i] - m_new)
            w_new = jnp.exp(m_chunk[i] - m_new)
            c_ref[bi, h] = twice(w_old) * c_s[i] + twice(w_new) * kv_ref[bi, n, h]
            n_ref[bi, h:h + 1, :] = w_old * n_s[i] + w_new * ks_ref[bi, h, row, :]
            m_ref[bi, h] = m_new
        for i, (bi, h) in enumerate(chains):
            m_intra = mi_ref[bi, h, sl, :]
            a = bc_ref[bi, h, sl, :] + m_s[i]
            m_t = jnp.maximum(a, m_intra)
            w_a = jnp.exp(a - m_t)
            w_i = jnp.exp(m_intra - m_t)
            num = twice(w_a) * qc[i] + twice(w_i) * hi_ref[bi, sl, vsl[i]]
            den = w_a * jnp.sum(q[i] * n_s[i], axis=-1, keepdims=True) + w_i * ni_ref[bi, h, sl, :]
            hh = num / twice(jnp.maximum(jnp.abs(den), jnp.exp(-m_t)))
            hd = jax.nn.sigmoid(og_ref[bi, sl, vsl[i]]) * hh
            oc = hd - jnp.mean(hd, axis=-1, keepdims=True)
            o = oc * lax.rsqrt(jnp.mean(oc * oc, axis=-1, keepdims=True) + NORM_EPS) * ng_ref[:, vsl[i]]
            o_ref[bi, sl, vsl[i]] = (o * jax.nn.silu(z_ref[bi, sl, vsl[i]])).astype(o_ref.dtype)
        return carry

    lax.fori_loop(0, nchunks, body, 0)

    @pl.when(t == last_t)
    def _():
        cout_ref[...] = c_ref[...]
        nout_ref[...] = n_ref[...]
        for bi, h in chains:
            mout_ref[bi, :, h:h + 1] = m_ref[bi, h][:, 0:1]


def _mlstm(proj3, ib_row, fb_row, norm_g_row, c0, n0, m0, *, c, tb, bb):
    bsz, l, _ = proj3.shape
    nblk = l // tb
    assert bsz % bb == 0
    tok = lambda col: (lambda b, t: (b, t, col))
    st4 = lambda b, t: (b, 0, 0, 0)
    st3 = lambda b, t: (b, 0, 0)
    in_specs = [
        pl.BlockSpec((bb, tb, D_KW), tok(CD_Q // D_KW)),
        pl.BlockSpec((bb, tb, D_KW), tok(CD_K // D_KW)),
        pl.BlockSpec((bb, tb, D_VW), tok(CD_V // D_VW)),
        pl.BlockSpec((bb, tb, D_VW), tok(CD_O // D_VW)),
        pl.BlockSpec((bb, tb, D_VW), tok(CD_ZD // D_VW)),
        pl.BlockSpec((bb, tb, LANES), tok(CD_SMALL // LANES)),
        pl.BlockSpec((1, LANES), lambda b, t: (0, 0)),
        pl.BlockSpec((1, LANES), lambda b, t: (0, 0)),
        pl.BlockSpec((1, D_VW), lambda b, t: (0, 0)),
        pl.BlockSpec((bb, D_HEADS, D_DK, D_DV), st4),
        pl.BlockSpec((bb, D_HEADS, D_DK), st3),
        pl.BlockSpec((bb, 1, D_HEADS), st3),
    ]
    out_specs = [
        pl.BlockSpec((bb, tb, D_VW), lambda b, t: (b, t, 0)),
        pl.BlockSpec((bb, D_HEADS, D_DK, D_DV), st4),
        pl.BlockSpec((bb, D_HEADS, D_DK), st3),
        pl.BlockSpec((bb, 1, D_HEADS), st3),
    ]
    state = bb * D_HEADS * D_DK * D_DV * 4
    blocks = bb * tb * (2 * D_KW + 3 * D_VW + LANES) * 4 + bb * tb * D_VW * 2 + 2 * state
    nchunks = tb // c
    return pl.pallas_call(
        functools.partial(_mlstm_kernel, c=c, nchunks=nchunks),
        out_shape=[jax.ShapeDtypeStruct((bsz, l, D_VW), BF16),
                   jax.ShapeDtypeStruct((bsz, D_HEADS, D_DK, D_DV), F32),
                   jax.ShapeDtypeStruct((bsz, D_HEADS, D_DK), F32),
                   jax.ShapeDtypeStruct((bsz, 1, D_HEADS), F32)],
        grid=(bsz // bb, nblk),
        in_specs=in_specs,
        out_specs=out_specs,
        scratch_shapes=[pltpu.VMEM((bb, D_HEADS, D_DK, D_DV), F32),
                        pltpu.VMEM((bb, D_HEADS, D_DK), F32),
                        pltpu.VMEM((bb, D_HEADS, 1, LANES), F32),
                        pltpu.VMEM((bb, tb, D_VW), F32),
                        pltpu.VMEM((bb, D_HEADS, tb, LANES), F32),
                        pltpu.VMEM((bb, D_HEADS, tb, LANES), F32),
                        pltpu.VMEM((bb, D_HEADS, tb, LANES), F32),
                        pltpu.VMEM((bb, nchunks, D_HEADS, D_DK, D_DV), F32),
                        pltpu.VMEM((bb, D_HEADS, _round_up(nchunks, SUBLANES), D_DK), F32),
                        pltpu.VMEM((bb, D_HEADS, _round_up(nchunks, SUBLANES), LANES), F32),
                        pltpu.VMEM((bb, D_HEADS, _round_up(nchunks, SUBLANES), LANES), F32)],
        compiler_params=pltpu.CompilerParams(
            dimension_semantics=("parallel", "arbitrary"),
            vmem_limit_bytes=_vmem_limit(blocks, state * (1 + nchunks) + bb * tb * (D_VW + 2 * LANES) * 4)),
        name="mlstm_mixer",
    )(proj3, proj3, proj3, proj3, proj3, proj3, ib_row, fb_row, norm_g_row, c0, n0, m0)


def _s5_expand_operators(kc_ref, bc_ref, cc_ref, bd_ref, bst_ref, cst_ref):
    tc, gt, sw, cg = S5_CHUNK, S5_GROUP_BLOCK, 2 * C_STATE, C_GROUP
    w_t = tc * LANES
    iota = lambda shape, d: lax.broadcasted_iota(jnp.int32, shape, d)
    row_g = (iota((w_t, LANES), 0) // cg) % gt
    tile16 = jnp.where(iota((cg, LANES), 1) % cg == iota((cg, LANES), 0), 1.0, 0.0).astype(BF16)
    bd = jnp.where(row_g == iota((w_t, LANES), 1) // cg, _mm(kc_ref[...], tile16), 0.0).astype(BF16)
    bd_ref[:, LANES:2 * LANES] = bd
    bd_ref[0:w_t - LANES, 0:LANES] = bd[LANES:, :]
    bd_ref[w_t - LANES:w_t, 0:LANES] = jnp.zeros((LANES, LANES), BF16)
    bc = bc_ref[...].astype(F32)
    for g in range(gt):
        bst_ref[:, g * sw:(g + 1) * sw] = jnp.where(row_g == g, bc, 0.0).astype(BF16)
    src, dst = iota((tc * cg, w_t), 0), iota((tc * cg, w_t), 1)
    spread = jnp.where((src // cg == dst // LANES) & (src % cg == dst % cg), 1.0, 0.0).astype(BF16)
    lane_g = (iota((sw, w_t), 1) // cg) % gt
    for g in range(gt):
        full = _mm(cc_ref[g * sw:(g + 1) * sw, :], spread)
        cst_ref[g * sw:(g + 1) * sw, :] = jnp.where(lane_g == g, full, 0.0).astype(BF16)


def _s5_kernel(u_ref, kc_ref, bc_ref, cc_ref, apow_ref, x0_ref, y_ref, xf_ref, bd_ref, bst_ref, cst_ref,
               *, nc, bb):
    m = nc * bb
    tc = S5_CHUNK
    sw = 2 * C_STATE

    @pl.when(pl.program_id(1) == 0)
    def _():
        _s5_expand_operators(kc_ref, bc_ref, cc_ref, bd_ref, bst_ref, cst_ref)

    row = lax.broadcasted_iota(jnp.int32, (m, sw), 0)
    n_idx = row & (nc - 1)
    n_log = int(math.log2(nc))

    def cmul(a1, a2, x):
        return a1 * x + a2 * pltpu.roll(x, C_STATE, 1)

    lhs = jnp.concatenate([u_ref[pl.ds(tau, m, stride=tc), :].astype(BF16) for tau in range(tc)], axis=1)
    e_all = _mm(lhs, bst_ref[...])
    x_start = []
    for g in range(S5_GROUP_BLOCK):
        gs = slice(g * sw, (g + 1) * sw)
        x0_rows = jnp.zeros((m, sw), F32)
        for b in range(bb):
            x0_rows = jnp.where(row == b * nc, x0_ref[b, :, gs], x0_rows)
        x = e_all[:, gs] + cmul(apow_ref[0:1, gs], apow_ref[1:2, gs], x0_rows)
        for j in range(n_log):
            sh = 1 << j
            shifted = jnp.where(n_idx >= sh, pltpu.roll(x, sh, 0), 0.0)
            x = x + cmul(apow_ref[2 * j:2 * j + 1, gs], apow_ref[2 * j + 1:2 * j + 2, gs], shifted)
        x_start.append(jnp.where(n_idx >= 1, pltpu.roll(x, 1, 0), x0_rows).astype(BF16))
        for b in range(bb):
            xf_ref[b, :, gs] = x[b * nc + nc - 1:b * nc + nc, :]
    y_state = _mm(jnp.concatenate(x_start, axis=1), cst_ref[...])
    for tau in range(0, tc, 2):
        width = (tau + 2) * LANES
        start = (tc - 2 - tau) * LANES
        y = _mm(lhs[:, :width], bd_ref[start:start + width, :]) + y_state[:, tau * LANES:(tau + 2) * LANES]
        y_ref[pl.ds(tau, m, stride=tc), :] = y[:, :LANES]
        y_ref[pl.ds(tau + 1, m, stride=tc), :] = y[:, LANES:]


def _s5_chunks(proj2d, kc, bc, cc, apow, x0, *, l, bb):
    tc = S5_CHUNK
    nc = l // tc
    bsz = x0.shape[1]
    nt = C_GROUPS // S5_GROUP_BLOCK
    rows = bb * l
    sw = 2 * C_STATE
    sw_t = S5_GROUP_BLOCK * sw
    w_t = tc * LANES
    assert CD_U % LANES == 0 and bsz % bb == 0
    blocks = (2 * rows * LANES * 4 + 2 * w_t * LANES * 2 + sw_t * tc * C_GROUP * 2 + apow.shape[1] * sw_t * 4
              + 2 * bb * SUBLANES * sw_t * 4)
    scratch = w_t * 2 * LANES * 2 + 2 * w_t * sw_t * 2
    temps = bb * nc * (w_t * 2 + w_t * 4 + 3 * sw_t * 4) + 4 * sw * w_t * 4
    return pl.pallas_call(
        functools.partial(_s5_kernel, nc=nc, bb=bb),
        out_shape=[jax.ShapeDtypeStruct((bsz * l, C_W), F32),
                   jax.ShapeDtypeStruct((nt, bsz, 1, sw_t), F32)],
        grid=(nt, bsz // bb),
        in_specs=[pl.BlockSpec((rows, LANES), lambda i, j: (j, CD_U // LANES + i)),
                  pl.BlockSpec((None, w_t, C_GROUP), lambda i, j: (i, 0, 0)),
                  pl.BlockSpec((None, w_t, sw), lambda i, j: (i, 0, 0)),
                  pl.BlockSpec((None, sw_t, tc * C_GROUP), lambda i, j: (i, 0, 0)),
                  pl.BlockSpec((None, apow.shape[1], sw_t), lambda i, j: (i, 0, 0)),
                  pl.BlockSpec((None, bb, 1, sw_t), lambda i, j: (i, j, 0, 0))],
        out_specs=[pl.BlockSpec((rows, LANES), lambda i, j: (j, i)),
                   pl.BlockSpec((None, bb, 1, sw_t), lambda i, j: (i, j, 0, 0))],
        scratch_shapes=[pltpu.VMEM((w_t, 2 * LANES), BF16),
                        pltpu.VMEM((w_t, sw_t), BF16),
                        pltpu.VMEM((sw_t, w_t), BF16)],
        compiler_params=pltpu.CompilerParams(
            dimension_semantics=("parallel", "arbitrary"),
            vmem_limit_bytes=_vmem_limit(blocks, scratch + temps)),
        name="s5_chunks",
    )(proj2d, kc, bc, cc, apow, x0)


def _s5_operators(lam_re, lam_im, log_dt, b_re, b_im, c_re, c_im, n_log):
    g, p = lam_re.shape
    tc = S5_CHUNK
    dt = jnp.exp(log_dt.astype(F32))[:, None]
    mag = jnp.exp(lam_re * dt)
    ab_re, ab_im = mag * jnp.cos(lam_im * dt), mag * jnp.sin(lam_im * dt)
    den = lam_re * lam_re + lam_im * lam_im
    er = ab_re - 1.0
    zr = (er * lam_re + ab_im * lam_im) / den
    zi = (ab_im * lam_re - er * lam_im) / den
    bb_re = zr[..., None] * b_re - zi[..., None] * b_im
    bb_im = zr[..., None] * b_im + zi[..., None] * b_re
    pw_re, pw_im = ab_re[None], ab_im[None]
    while pw_re.shape[0] < tc:
        top_re, top_im = pw_re[-1], pw_im[-1]
        pw_re, pw_im = (jnp.concatenate([pw_re, top_re * pw_re - top_im * pw_im]),
                        jnp.concatenate([pw_im, top_re * pw_im + top_im * pw_re]))
    pw_re = jnp.concatenate([jnp.ones_like(ab_re)[None], pw_re])
    pw_im = jnp.concatenate([jnp.zeros_like(ab_im)[None], pw_im])
    abr = pw_re[:tc, :, :, None] * bb_re - pw_im[:tc, :, :, None] * bb_im
    abi = pw_re[:tc, :, :, None] * bb_im + pw_im[:tc, :, :, None] * bb_re
    kern = (jnp.einsum('gjp,dgpi->dgji', c_re, abr, precision=HIGHEST)
            - jnp.einsum('gjp,dgpi->dgji', c_im, abi, precision=HIGHEST))
    gt = S5_GROUP_BLOCK
    nt = g // gt
    sw = 2 * p
    kc = kern[::-1].reshape(tc, nt, gt, C_GROUP, C_GROUP).transpose(1, 0, 2, 4, 3)
    kc = kc.reshape(nt, tc * gt * C_GROUP, C_GROUP)
    ab = jnp.concatenate([abr, abi], axis=2)[::-1]
    bc = ab.reshape(tc, nt, gt, sw, C_GROUP).transpose(1, 0, 2, 4, 3).reshape(nt, tc * gt * C_GROUP, sw)
    cr = c_re[None] * pw_re[1:, :, None, :] - c_im[None] * pw_im[1:, :, None, :]
    ci = -(c_re[None] * pw_im[1:, :, None, :] + c_im[None] * pw_re[1:, :, None, :])
    cc = jnp.concatenate([cr, ci], axis=3)
    cc = cc.reshape(tc, nt, gt, C_GROUP, sw).transpose(1, 2, 4, 0, 3).reshape(nt, gt * sw, tc * C_GROUP)
    r, i = pw_re[tc], pw_im[tc]
    rows = []
    for _ in range(max(n_log, 1)):
        rows += [jnp.concatenate([r, r], -1), jnp.concatenate([-i, i], -1)]
        r, i = r * r - i * i, 2.0 * r * i
    apow = jnp.stack(rows, axis=1)
    apow = apow.reshape(nt, gt, -1, sw).transpose(0, 2, 1, 3).reshape(nt, -1, gt * sw)
    return kc.astype(BF16), bc.astype(BF16), cc.astype(BF16), apow


def _s5(proj2d, ops, x0_re, x0_im, *, l):
    kc, bc, cc, apow = ops
    bsz = x0_re.shape[0]
    nt = C_GROUPS // S5_GROUP_BLOCK
    x0 = jnp.concatenate([x0_re, x0_im], axis=-1).reshape(bsz, nt, 1, -1).transpose(1, 0, 2, 3)
    bb = bsz if bsz * l <= 4096 else 1
    y, xf = _s5_chunks(proj2d, kc, bc, cc, apow, x0, l=l, bb=bb)
    xf = xf.transpose(1, 0, 2, 3).reshape(bsz, C_GROUPS, 2 * C_STATE)
    return y, xf[..., :C_STATE], xf[..., C_STATE:]


AB_SRC_GA = AB_QKV
AB_SRC_QKV = AB_SRC_GA + A_GATE_RANK
AB_SRC_TAIL = AB_SRC_QKV + B_QKV + B_VW
IN_AB = AB_SRC_TAIL + 2 * B_HEADS
IN_CD = CD_SMALL + 2 * D_HEADS
WPREP_AB_TILE = 512
WPREP_CD_TILE = 640


def _prep_w_ab_kernel(wt_ref, ga_ref, o_ref):
    j = pl.program_id(0)
    tile = o_ref.shape[1]
    n_small = A_GATE_RANK + 2 * B_HEADS

    @pl.when(j < AB_SMALL // tile)
    def _():
        o_ref[...] = wt_ref[...].T.astype(BF16)

    @pl.when(j == AB_SMALL // tile)
    def _():
        rows = jnp.concatenate([ga_ref[...], wt_ref[tile - 2 * B_HEADS:tile, :],
                                jnp.zeros((tile - n_small, wt_ref.shape[1]), F32)], axis=0)
        o_ref[...] = rows.T.astype(BF16)

    @pl.when(j > AB_SMALL // tile)
    def _():
        o_ref[...] = jnp.zeros(o_ref.shape, BF16)


def _prep_w_ab(w_t, n_out):
    n_in, d = w_t.shape
    tile = WPREP_AB_TILE
    assert AB_QKV % tile == 0 and AB_SMALL % tile == 0 and n_out % tile == 0 and n_in >= tile

    unit = A_GATE_RANK
    assert tile % unit == 0 and (AB_SRC_QKV - AB_QKV) % unit == 0 and (n_in - tile) % unit == 0

    def src_row(j):
        k = j * (tile // unit)
        k = jnp.where(j < AB_QKV // tile, k,
                      jnp.where(j < AB_SMALL // tile, k + (AB_SRC_QKV - AB_QKV) // unit, (n_in - tile) // unit))
        return unit * k

    return pl.pallas_call(
        _prep_w_ab_kernel,
        out_shape=jax.ShapeDtypeStruct((d, n_out), BF16),
        grid=(n_out // tile,),
        in_specs=[pl.BlockSpec((pl.Element(tile), pl.Element(d)), lambda j: (src_row(j), 0)),
                  pl.BlockSpec((pl.Element(A_GATE_RANK), pl.Element(d)), lambda j: (AB_SRC_GA, 0))],
        out_specs=pl.BlockSpec((d, tile), lambda j: (0, j)),
        compiler_params=pltpu.CompilerParams(
            dimension_semantics=("parallel",),
            vmem_limit_bytes=_vmem_limit(tile * d * 6, 2 * tile * d * 4)),
        name="prep_w_in_ab",
    )(w_t, w_t)


def _prep_w_cd_kernel(wt_ref, o_ref):
    tile = o_ref.shape[1]
    row = pl.program_id(0) * tile + lax.broadcasted_iota(jnp.int32, wt_ref.shape, 0)
    o_ref[...] = jnp.where(row < IN_CD, wt_ref[...], 0.0).T.astype(BF16)


def _prep_w_cd(w_t, n_out):
    n_in, d = w_t.shape
    tile = WPREP_CD_TILE
    assert n_out % tile == 0
    return pl.pallas_call(
        _prep_w_cd_kernel,
        out_shape=jax.ShapeDtypeStruct((d, n_out), BF16),
        grid=(n_out // tile,),
        in_specs=[pl.BlockSpec((tile, d), lambda j: (j, 0))],
        out_specs=pl.BlockSpec((d, tile), lambda j: (0, j)),
        compiler_params=pltpu.CompilerParams(
            dimension_semantics=("parallel",),
            vmem_limit_bytes=_vmem_limit(tile * d * 6, 2 * tile * d * 4)),
        name="prep_w_in_cd",
    )(w_t)


def _lane_row(vals, lane0):
    return jnp.zeros((1, LANES), F32).at[0, lane0:lane0 + vals.shape[0]].set(vals.astype(F32))


def _prepare_weights(norm_g, final_norm_g, w_in_ab, a_gate_w, a_gate_b, a_norm_g, b_conv_w, b_a_log,
                     b_dt_bias, b_norm_g, w_out_ab, w_in_cd, c_lam_re, c_lam_im, c_log_dt, c_b_re,
                     c_b_im, c_c_re, c_c_im, c_d, c_glu_w, c_glu_b, d_i_bias, d_f_bias, d_norm_g,
                     w_out_cd, n_log):
    assert w_in_ab.shape[1] == IN_AB and w_in_cd.shape[1] == IN_CD
    w_ab = w_in_ab.astype(F32).T
    w_cd = w_in_cd.astype(F32).T
    gate_w = jnp.zeros((LANES, A_KW), F32).at[AB_GA_LANE:AB_GA_LANE + A_GATE_RANK].set(
        a_gate_w.astype(F32)).astype(BF16)
    return dict(
        norm_g=norm_g.astype(F32), final_g=final_norm_g.astype(F32)[None, :],
        w_ab=w_ab, w_cd=w_cd, gate_w=gate_w, gate_b=a_gate_b.astype(F32)[None, :],
        a_norm_g=a_norm_g.astype(F32)[None, :], conv_w=b_conv_w.astype(F32),
        alog=_lane_row(b_a_log, AB_APRE_LANE), dtb=_lane_row(b_dt_bias, AB_APRE_LANE),
        b_norm_g=b_norm_g.astype(F32)[None, :],
        w_out_a=w_out_ab[:A_VW].astype(BF16), w_out_b=w_out_ab[A_VW:].astype(BF16),
        s5_ops=_s5_operators(c_lam_re.astype(F32), c_lam_im.astype(F32), c_log_dt, c_b_re.astype(F32),
                             c_b_im.astype(F32), c_c_re.astype(F32), c_c_im.astype(F32), n_log),
        c_d=c_d.astype(F32).reshape(1, C_W), glu_w=c_glu_w.astype(BF16),
        glu_b=c_glu_b.astype(F32)[None, :],
        ib=_lane_row(d_i_bias, CD_I_LANE), fb=_lane_row(d_f_bias, CD_F_LANE),
        d_norm_g=d_norm_g.astype(F32)[None, :],
        w_out_c=w_out_cd[:C_W].astype(BF16), w_out_d=w_out_cd[C_W:].astype(BF16),
    )


def _trunk(x, conv_prev, s_gla0, s_gdn0, s5_re0, s5_im0, mc0, mn0, mm0, w):
    bsz, l, d = x.shape
    c = min(CHUNK, l)
    tb = min(l, 8 * c)
    assert l % tb == 0 and l % S5_CHUNK == 0
    x2d = x.reshape(bsz * l, d)

    proj = _proj(_rmsnorm_cast(x2d, w['norm_g'][0:1]), w['w_ab'], n_aligned=AB_SMALL, shift_from=AB_QKV,
                 shift=AB_SRC_QKV - AB_QKV, head_row=AB_SRC_GA, head_rows=A_GATE_RANK, tail_rows=2 * B_HEADS)
    proj3 = proj.reshape(bsz, l, proj.shape[1])
    bb = MIXER_STREAMS if bsz % MIXER_STREAMS == 0 else 1
    tb_s = min(l, (8 // bb) * c)
    o_a, s_gla = _gla(proj3, w['gate_w'], w['gate_b'], w['a_norm_g'], s_gla0.astype(F32), c=c, tb=tb_s, bb=bb)
    o_b, s_gdn = _gdn(proj3, conv_prev.astype(F32), w['conv_w'], w['alog'], w['dtb'], w['b_norm_g'],
                      s_gdn0.astype(F32), c=c, tb=tb_s, bb=bb)
    conv_new = proj3[:, l - (B_CONV - 1):, AB_QKV:AB_QKV + B_QKV]
    h1, hn1 = _out_proj(o_a.reshape(bsz * l, A_VW), o_b.reshape(bsz * l, B_VW), w['w_out_a'], w['w_out_b'],
                        x2d, w['norm_g'][1:2])

    proj = _proj(hn1, w['w_cd'], n_aligned=CD_SMALL, shift_from=CD_SMALL, shift=0, head_row=0, head_rows=0,
                 tail_rows=2 * D_HEADS)
    proj3 = proj.reshape(bsz, l, proj.shape[1])
    y, s5_re, s5_im = _s5(proj, w['s5_ops'], s5_re0.astype(F32), s5_im0.astype(F32), l=l)
    o_d, mc, mn, mm = _mlstm(proj3, w['ib'], w['fb'], w['d_norm_g'], mc0.astype(F32),
                             mn0.astype(F32), mm0.astype(F32)[:, None, :], c=c, tb=tb_s, bb=bb)
    y_out = _glu_out_proj_norm(y, proj, o_d.reshape(bsz * l, D_VW), w['c_d'], w['glu_w'], w['glu_b'],
                               w['w_out_c'], w['w_out_d'], h1, w['final_g'])
    dt = x.dtype
    return (y_out.reshape(bsz, l, d).astype(dt), conv_new.astype(dt), s_gla.astype(dt), s_gdn.astype(dt),
            s5_re.astype(dt), s5_im.astype(dt), mc.astype(dt), mn.astype(dt), mm[:, 0, :].astype(dt))


def kernel(x_prompt, x_sample, cache_gdn_conv, state_gla, state_gdn, state_s5_re, state_s5_im,
           state_mlstm_c, state_mlstm_n, state_mlstm_m, norm_g, final_norm_g, w_in_ab, a_gate_w,
           a_gate_b, a_norm_g, b_conv_w, b_a_log, b_dt_bias, b_norm_g, w_out_ab, w_in_cd, c_lam_re,
           c_lam_im, c_log_dt, c_b_re, c_b_im, c_c_re, c_c_im, c_d, c_glu_w, c_glu_b, d_i_bias,
           d_f_bias, d_norm_g, w_out_cd):
    n_log = int(math.log2(max(x_prompt.shape[1], x_sample.shape[1]) // S5_CHUNK))
    w = _prepare_weights(norm_g, final_norm_g, w_in_ab, a_gate_w, a_gate_b, a_norm_g, b_conv_w, b_a_log,
                         b_dt_bias, b_norm_g, w_out_ab, w_in_cd, c_lam_re, c_lam_im, c_log_dt, c_b_re,
                         c_b_im, c_c_re, c_c_im, c_d, c_glu_w, c_glu_b, d_i_bias, d_f_bias, d_norm_g,
                         w_out_cd, n_log)
    nb = x_prompt.shape[0]
    zeros = lambda *shape: jnp.zeros(shape, F32)
    p_out = _trunk(x_prompt, zeros(nb, B_CONV - 1, B_QKV), zeros(nb, A_HEADS, A_DK, A_DV),
                   zeros(nb, B_HEADS, B_DK, B_DV), zeros(nb, C_GROUPS, C_STATE), zeros(nb, C_GROUPS, C_STATE),
                   zeros(nb, D_HEADS, D_DK, D_DV), zeros(nb, D_HEADS, D_DK), zeros(nb, D_HEADS), w)
    s_out = _trunk(x_sample, cache_gdn_conv, state_gla, state_gdn, state_s5_re, state_s5_im,
                   state_mlstm_c, state_mlstm_n, state_mlstm_m, w)
    return (p_out[0], s_out[0]) + tuple(p_out[1:]) + tuple(s_out[1:])
```

```python
import functools
import math

import jax
import jax.numpy as jnp
from jax import lax
from jax.experimental import pallas as pl
from jax.experimental.pallas import tpu as pltpu

F32 = jnp.float32
BF16 = jnp.bfloat16
HIGHEST = lax.Precision.HIGHEST

NORM_EPS = 1e-6
CHUNK = 64
A_HEADS, A_DK, A_DV, A_GATE_RANK, A_GATE_TAU = 4, 128, 256, 16, 16.0
B_HEADS, B_DK, B_DV, B_CONV = 8, 128, 128, 4
C_GROUP, C_GROUPS, C_STATE = 16, 64, 64
D_HEADS, D_DK, D_DV = 4, 128, 256
A_KW, A_VW = A_HEADS * A_DK, A_HEADS * A_DV
B_KW, B_VW = B_HEADS * B_DK, B_HEADS * B_DV
B_QKV = 2 * B_KW + B_VW
C_W = C_GROUPS * C_GROUP
D_KW, D_VW = D_HEADS * D_DK, D_HEADS * D_DV

LANES = 128
SUBLANES = 8
VMEM_BYTES_V7X = 64 * 1024 * 1024

AB_Q, AB_K, AB_V, AB_Z = 0, A_KW, 2 * A_KW, 2 * A_KW + A_VW
AB_QKV = AB_Z + A_VW
AB_ZB = AB_QKV + B_QKV
AB_SMALL = AB_ZB + B_VW
AB_GA_LANE, AB_BETA_LANE, AB_APRE_LANE = 0, A_GATE_RANK, A_GATE_RANK + B_HEADS
CD_U, CD_Z = 0, C_W
CD_Q = 2 * C_W
CD_K = CD_Q + D_KW
CD_V = CD_K + D_KW
CD_O = CD_V + D_VW
CD_ZD = CD_O + D_VW
CD_SMALL = CD_ZD + D_VW
CD_I_LANE, CD_F_LANE = 0, D_HEADS

PROJ_TN = 1280
S5_CHUNK = 16
S5_GROUP_BLOCK = 8
MIXER_STREAMS = 2
MLSTM_INTRA_CHUNKS = 4
GDN_WY_CHAINS = 32


def _round_up(x, m):
    return (x + m - 1) // m * m


def _vmem_limit(block_bytes, scratch_bytes=0):
    est = 2 * block_bytes + scratch_bytes
    return int(min(max(2 * est, 32 * 1024 * 1024), VMEM_BYTES_V7X - 8 * 1024 * 1024))


def _mm(a, b):
    return jnp.dot(a, b, preferred_element_type=F32)


def _dot(a, b):
    return _mm(a.astype(BF16), b.astype(BF16))


def _dot_nt(a, b):
    return lax.dot_general(a.astype(BF16), b.astype(BF16), (((1,), (1,)), ((), ())),
                           preferred_element_type=F32)


def _dot_tn(a, b):
    return lax.dot_general(a.astype(BF16), b.astype(BF16), (((0,), (0,)), ((), ())),
                           preferred_element_type=F32)


def _split2(x):
    hi = x.astype(BF16)
    return hi, (x - hi.astype(F32)).astype(BF16)


def _split3(x):
    hi = x.astype(BF16)
    r = x - hi.astype(F32)
    mid = r.astype(BF16)
    return hi, mid, (r - mid.astype(F32)).astype(BF16)


def _dot3(a, b):
    ah, al = _split2(a)
    bh, bl = _split2(b)
    return _mm(ah, bh) + _mm(ah, bl) + _mm(al, bh)


def _cumsum_rows(tri_bf16, x):
    hi, mid, lo = _split3(x)
    return _mm(tri_bf16, hi) + _mm(tri_bf16, mid) + _mm(tri_bf16, lo)


def _select_rows(sel_bf16, x):
    nt = lambda b: lax.dot_general(sel_bf16, b, (((1,), (1,)), ((), ())), preferred_element_type=F32)
    hi, mid, lo = _split3(x)
    return nt(hi) + nt(mid) + nt(lo)


def _lane_selector(lane0):
    r = lax.broadcasted_iota(jnp.int32, (SUBLANES, LANES), 0)
    l = lax.broadcasted_iota(jnp.int32, (SUBLANES, LANES), 1)
    return jnp.where(l == r + lane0, 1.0, 0.0).astype(BF16)


def _causal_masks(c):
    row = lax.broadcasted_iota(jnp.int32, (c, c), 0)
    col = lax.broadcasted_iota(jnp.int32, (c, c), 1)
    return row >= col, row > col


def _norm_matmul_kernel(x_ref, g_ref, w_ref, o_ref, xn_ref):
    @pl.when(pl.program_id(1) == 0)
    def _():
        x = x_ref[...]
        y = x * lax.rsqrt(jnp.mean(x * x, axis=-1, keepdims=True) + NORM_EPS) * g_ref[...]
        xn_ref[...] = y.astype(BF16)

    o_ref[...] = jnp.dot(xn_ref[...], w_ref[...], preferred_element_type=F32)


def _norm_matmul(x2d, g_row, w_bf16):
    m, d = x2d.shape
    n = w_bf16.shape[1]
    tm = min(m, 1024)
    tn = PROJ_TN
    assert m % tm == 0 and n % tn == 0
    blocks = tm * d * 4 + d * tn * 2 + tm * tn * 4
    return pl.pallas_call(
        _norm_matmul_kernel,
        out_shape=jax.ShapeDtypeStruct((m, n), F32),
        grid=(m // tm, n // tn),
        in_specs=[pl.BlockSpec((tm, d), lambda i, j: (i, 0)),
                  pl.BlockSpec((1, d), lambda i, j: (0, 0)),
                  pl.BlockSpec((d, tn), lambda i, j: (0, j))],
        out_specs=pl.BlockSpec((tm, tn), lambda i, j: (i, j)),
        scratch_shapes=[pltpu.VMEM((tm, d), BF16)],
        compiler_params=pltpu.CompilerParams(
            dimension_semantics=("parallel", "arbitrary"),
            vmem_limit_bytes=_vmem_limit(blocks, tm * d * 2)),
        name="norm_in_proj",
    )(x2d, g_row, w_bf16)


def _out_proj_kernel(a_ref, b_ref, wa_ref, wb_ref, h_ref, o_ref):
    out = (jnp.dot(a_ref[...], wa_ref[...], preferred_element_type=F32)
           + jnp.dot(b_ref[...], wb_ref[...], preferred_element_type=F32))
    o_ref[...] = h_ref[...] + out


def _glu_out_proj_norm_kernel(y_ref, u_ref, z_ref, od_ref, d_ref, gw_ref, gb_ref, wc_ref, wd_ref, h_ref, g_ref,
                              o_ref):
    y = jax.nn.gelu(y_ref[...] + d_ref[...] * u_ref[...])
    gate = jax.nn.sigmoid(jnp.dot(y.astype(BF16), gw_ref[...], preferred_element_type=F32) + gb_ref[...])
    o_c = (y * gate * jax.nn.silu(z_ref[...])).astype(BF16)
    out = (jnp.dot(o_c, wc_ref[...], preferred_element_type=F32)
           + jnp.dot(od_ref[...], wd_ref[...], preferred_element_type=F32))
    h = h_ref[...] + out
    o_ref[...] = h * lax.rsqrt(jnp.mean(h * h, axis=-1, keepdims=True) + NORM_EPS) * g_ref[...]


def _glu_out_proj_norm(y2d, proj2d, o_d, d_row, glu_w, glu_b_row, w_c, w_d, h2d, final_g_row):
    m, d = h2d.shape
    tm = min(m, 512)
    assert m % tm == 0
    row = lambda col: (lambda i: (i, col))
    const = lambda i: (0, 0)
    resident = pl.Buffered(1)
    blocks = 3 * tm * C_W * 4 + tm * D_VW * 2 + 2 * tm * d * 4
    weights = C_W * C_W * 2 + (C_W + D_VW) * d * 2
    return pl.pallas_call(
        _glu_out_proj_norm_kernel,
        out_shape=jax.ShapeDtypeStruct((m, d), F32),
        grid=(m // tm,),
        in_specs=[pl.BlockSpec((tm, C_W), row(0)),
                  pl.BlockSpec((tm, C_W), row(CD_U // C_W)),
                  pl.BlockSpec((tm, C_W), row(CD_Z // C_W)),
                  pl.BlockSpec((tm, D_VW), row(0)),
                  pl.BlockSpec((1, C_W), const),
                  pl.BlockSpec((C_W, C_W), const, pipeline_mode=resident),
                  pl.BlockSpec((1, C_W), const),
                  pl.BlockSpec((C_W, d), const, pipeline_mode=resident),
                  pl.BlockSpec((D_VW, d), const, pipeline_mode=resident),
                  pl.BlockSpec((tm, d), row(0)),
                  pl.BlockSpec((1, d), const)],
        out_specs=pl.BlockSpec((tm, d), row(0)),
        compiler_params=pltpu.CompilerParams(
            dimension_semantics=("parallel",),
            vmem_limit_bytes=_vmem_limit(blocks, weights + 3 * tm * C_W * 4)),
        name="glu_out_proj_norm",
    )(y2d, proj2d, proj2d, o_d, d_row, glu_w, glu_b_row, w_c, w_d, h2d, final_g_row)


def _out_proj(mix_a, mix_b, w_a, w_b, h2d):
    m, d = h2d.shape
    ka, kb = mix_a.shape[1], mix_b.shape[1]
    tm = min(m, 512)
    assert m % tm == 0
    blocks = tm * (ka + kb) * 2 + (ka + kb) * d * 2 + 2 * tm * d * 4
    return pl.pallas_call(
        _out_proj_kernel,
        out_shape=jax.ShapeDtypeStruct((m, d), F32),
        grid=(m // tm,),
        in_specs=[pl.BlockSpec((tm, ka), lambda i: (i, 0)),
                  pl.BlockSpec((tm, kb), lambda i: (i, 0)),
                  pl.BlockSpec((ka, d), lambda i: (0, 0)),
                  pl.BlockSpec((kb, d), lambda i: (0, 0)),
                  pl.BlockSpec((tm, d), lambda i: (i, 0))],
        out_specs=pl.BlockSpec((tm, d), lambda i: (i, 0)),
        compiler_params=pltpu.CompilerParams(
            dimension_semantics=("parallel",),
            vmem_limit_bytes=_vmem_limit(blocks)),
        name="out_proj",
    )(mix_a, mix_b, w_a, w_b, h2d)


def _gla_kernel(q_ref, k_ref, v_ref, z_ref, sm_ref, gw_ref, gb_ref, ng_ref, s0_ref,
                o_ref, sout_ref, st_ref, *, c, nchunks):
    t = pl.program_id(1)
    last_t = pl.num_programs(1) - 1
    bb = q_ref.shape[0]
    chains = [(bi, h) for bi in range(bb) for h in range(A_HEADS)]
    ksl = [slice(h * A_DK, (h + 1) * A_DK) for _, h in chains]
    vsl = [slice(h * A_DV, (h + 1) * A_DV) for _, h in chains]
    idx = range(len(chains))

    @pl.when(t == 0)
    def _():
        for bi, h in chains:
            st_ref[bi, h] = s0_ref[bi, h].T

    causal, _ = _causal_masks(c)
    tri = causal.astype(BF16)
    gw = gw_ref[...]
    gb = gb_ref[...]

    def body(n, carry):
        sl = pl.ds(pl.multiple_of(n * c, c), c)
        b_all = [_cumsum_rows(tri, jax.nn.log_sigmoid(_dot(sm_ref[bi, sl, :], gw) + gb) * (1.0 / A_GATE_TAU))
                 for bi in range(bb)]
        b = [b_all[chains[i][0]][:, ksl[i]] for i in idx]
        b_last = [b[i][c - 1:c, :] for i in idx]
        k = [k_ref[chains[i][0], sl, ksl[i]] for i in idx]
        v = [v_ref[chains[i][0], sl, vsl[i]].astype(BF16) for i in idx]
        q_dec = [(q_ref[chains[i][0], sl, ksl[i]] * (A_DK ** -0.5) * jnp.exp(b[i])).astype(BF16) for i in idx]
        k_dec = [(k[i] * jnp.exp(-b[i])).astype(BF16) for i in idx]
        k_w = [(k[i] * jnp.exp(b_last[i] - b[i])).astype(BF16) for i in idx]
        scores = [jnp.where(causal, _dot_nt(q_dec[i], k_dec[i]), 0.0).astype(BF16) for i in idx]
        s_t = [st_ref[bi, h] for bi, h in chains]
        outs = [_mm(scores[i], v[i]) + _dot_nt(q_dec[i], s_t[i]) for i in idx]
        for i, (bi, h) in enumerate(chains):
            st_ref[bi, h] = s_t[i] * jnp.exp(b_last[i]) + _dot_tn(v[i], k_w[i])
        for i, (bi, h) in enumerate(chains):
            o = outs[i]
            o = o * lax.rsqrt(jnp.mean(o * o, axis=-1, keepdims=True) + NORM_EPS) * ng_ref[:, vsl[i]]
            o_ref[bi, sl, vsl[i]] = (o * jax.nn.silu(z_ref[bi, sl, vsl[i]])).astype(o_ref.dtype)
        return carry

    lax.fori_loop(0, nchunks, body, 0)

    @pl.when(t == last_t)
    def _():
        for bi, h in chains:
            sout_ref[bi, h] = st_ref[bi, h].T


def _gla(proj3, gate_w_pad, gate_b_row, norm_g_row, s0, *, c, tb, bb):
    bsz, l, _ = proj3.shape
    nblk = l // tb
    assert bsz % bb == 0
    tok = lambda col: (lambda b, t: (b, t, col))
    in_specs = [
        pl.BlockSpec((bb, tb, A_KW), tok(AB_Q // A_KW)),
        pl.BlockSpec((bb, tb, A_KW), tok(AB_K // A_KW)),
        pl.BlockSpec((bb, tb, A_VW), tok(AB_V // A_VW)),
        pl.BlockSpec((bb, tb, A_VW), tok(AB_Z // A_VW)),
        pl.BlockSpec((bb, tb, LANES), tok(AB_SMALL // LANES)),
        pl.BlockSpec((LANES, A_KW), lambda b, t: (0, 0)),
        pl.BlockSpec((1, A_KW), lambda b, t: (0, 0)),
        pl.BlockSpec((1, A_VW), lambda b, t: (0, 0)),
        pl.BlockSpec((bb, A_HEADS, A_DK, A_DV), lambda b, t: (b, 0, 0, 0)),
    ]
    out_specs = [
        pl.BlockSpec((bb, tb, A_VW), lambda b, t: (b, t, 0)),
        pl.BlockSpec((bb, A_HEADS, A_DK, A_DV), lambda b, t: (b, 0, 0, 0)),
    ]
    state = bb * A_HEADS * A_DK * A_DV * 4
    blocks = bb * tb * (2 * A_KW + 2 * A_VW + LANES) * 4 + bb * tb * A_VW * 2 + 2 * state
    return pl.pallas_call(
        functools.partial(_gla_kernel, c=c, nchunks=tb // c),
        out_shape=[jax.ShapeDtypeStruct((bsz, l, A_VW), BF16),
                   jax.ShapeDtypeStruct((bsz, A_HEADS, A_DK, A_DV), F32)],
        grid=(bsz // bb, nblk),
        in_specs=in_specs,
        out_specs=out_specs,
        scratch_shapes=[pltpu.VMEM((bb, A_HEADS, A_DV, A_DK), F32)],
        compiler_params=pltpu.CompilerParams(
            dimension_semantics=("parallel", "arbitrary"),
            vmem_limit_bytes=_vmem_limit(blocks, state)),
        name="gla_mixer",
    )(proj3, proj3, proj3, proj3, proj3, gate_w_pad, gate_b_row, norm_g_row, s0)


def _gdn_kernel(x_ref, z_ref, sm_ref, w_ref, cp_ref, alog_ref, dtb_ref, ng_ref, s0_ref,
                o_ref, sout_ref, s_ref, tail_ref, u_ref, wm_ref, qg_ref, kg_ref, qk_ref, gl_ref,
                *, c, nchunks):
    t = pl.program_id(1)
    last_t = pl.num_programs(1) - 1
    tb = c * nchunks
    keep = SUBLANES - (B_CONV - 1)
    bb = x_ref.shape[0]

    @pl.when(t == 0)
    def _():
        s_ref[...] = s0_ref[...]
        tail_ref[:, 0:keep, :] = jnp.zeros((bb, keep, B_QKV), F32)
        tail_ref[:, keep:SUBLANES, :] = cp_ref[...]

    def conv_silu(bi, row0, first, cols):
        x = x_ref[bi, pl.ds(row0, c), cols]
        if first is None:
            prev = x_ref[bi, pl.ds(pl.multiple_of(row0 - SUBLANES, SUBLANES), SUBLANES), cols]
        else:
            before = pl.multiple_of(jnp.maximum(row0 - SUBLANES, 0), SUBLANES)
            prev = jnp.where(first, tail_ref[bi, :, cols], x_ref[bi, pl.ds(before, SUBLANES), cols])
        w = w_ref[:, cols]
        ext = jnp.concatenate([prev, x], axis=0)
        ext1 = pltpu.roll(ext, 1, 0)
        newer = ext * w[3:4, :] + ext1 * w[2:3, :]
        older = ext * w[1:2, :] + ext1 * w[0:1, :]
        conv = newer + pltpu.roll(older, 2, 0)
        return jax.nn.silu(conv[SUBLANES:SUBLANES + c, :])

    causal, strict = _causal_masks(c)
    tri = causal.astype(BF16)
    eye = jnp.where(causal & jnp.logical_not(strict), 1.0, 0.0).astype(F32)
    sel = _lane_selector(AB_APRE_LANE)
    neg_a_exp = -jnp.exp(alog_ref[...])
    dtb = dtb_ref[...]
    ng = ng_ref[...]
    n_double = int(math.log2(c)) - 1
    heads = range(B_HEADS)

    group = max(1, GDN_WY_CHAINS // (bb * B_HEADS))
    group = group if nchunks % group == 0 else 1

    def wy_factors(n, carry):
        pws, rhss, where = [], [], []
        for bi, ci in [(bi, ci) for ci in range(group) for bi in range(bb)]:
            row0 = pl.multiple_of((n * group + ci) * c, c)
            first = (n == 0) if ci == 0 else None
            sl = pl.ds(row0, c)
            sm = sm_ref[bi, sl, :]
            g_cum = _cumsum_rows(tri, neg_a_exp * jax.nn.softplus(sm + dtb))
            g_rows = _select_rows(sel, g_cum)
            beta_all = jax.nn.sigmoid(sm)
            gl_ref[bi, pl.ds(n * group + ci, 1), :] = g_cum[c - 1:c, :]
            for h in heads:
                hs = slice(h * B_DK, (h + 1) * B_DK)
                q = conv_silu(bi, row0, first, hs)
                k = conv_silu(bi, row0, first, slice(B_KW + h * B_DK, B_KW + (h + 1) * B_DK))
                v = conv_silu(bi, row0, first, slice(2 * B_KW + h * B_DV, 2 * B_KW + (h + 1) * B_DV))
                q = q * lax.rsqrt(jnp.sum(q * q, axis=-1, keepdims=True) + NORM_EPS) * (B_DK ** -0.5)
                k = k * lax.rsqrt(jnp.sum(k * k, axis=-1, keepdims=True) + NORM_EPS)
                g_col = g_cum[:, AB_APRE_LANE + h:AB_APRE_LANE + h + 1]
                beta = beta_all[:, AB_BETA_LANE + h:AB_BETA_LANE + h + 1]
                decay = jnp.exp(jnp.where(causal, g_col - g_rows[h:h + 1, :], -jnp.inf))
                e_g = jnp.exp(g_col)
                k_beta = k * beta
                pws.append(-jnp.where(strict, _dot_nt(k_beta, k) * decay, 0.0))
                rhss.append(jnp.concatenate([v * beta, k_beta * e_g], axis=1))
                where.append((bi, sl, hs))
                qk_ref[bi, sl, h * LANES:h * LANES + c] = jnp.where(causal, _dot_nt(q, k) * decay,
                                                                    0.0).astype(BF16)
                qg_ref[bi, sl, hs] = (q * e_g).astype(BF16)
                kg_ref[bi, sl, hs] = (k * jnp.exp(g_col[c - 1:c, :] - g_col)).astype(BF16)
        items = range(len(pws))
        neg_lower = [_split2(p) for p in pws]
        invs = [eye + p for p in pws]
        for _ in range(n_double):
            pws = [_dot(p, p) for p in pws]
            invs = [i + _dot(i, p) for i, p in zip(invs, pws)]
        invs = [i.astype(BF16) for i in invs]
        sol = [_mm(invs[i], rhss[i].astype(BF16)).astype(BF16) for i in items]
        resid = [rhss[i] - sol[i].astype(F32) + _mm(neg_lower[i][0], sol[i]) + _mm(neg_lower[i][1], sol[i])
                 for i in items]
        for i in items:
            bi, sl, hs = where[i]
            uw = sol[i].astype(F32) + _mm(invs[i], resid[i].astype(BF16))
            u_ref[bi, sl, hs] = uw[:, :B_DV]
            wm_ref[bi, sl, hs] = uw[:, B_DV:].astype(BF16)
        return carry

    lax.fori_loop(0, nchunks // group, wy_factors, 0)
    tail_ref[...] = x_ref[:, tb - SUBLANES:tb, :]

    chains = [(bi, h) for bi in range(bb) for h in heads]
    hsl = [slice(h * B_DK, (h + 1) * B_DK) for _, h in chains]

    def recurrence(n, carry):
        sl = pl.ds(pl.multiple_of(n * c, c), c)
        e_last = [jnp.exp(gl_ref[bi, pl.ds(n, 1), :]) for bi in range(bb)]
        s_old = [s_ref[bi, h] for bi, h in chains]
        s_bf = [s.astype(BF16) for s in s_old]
        v_new = [(u_ref[bi, sl, hsl[i]] - _mm(wm_ref[bi, sl, hsl[i]], s_bf[i])).astype(BF16)
                 for i, (bi, h) in enumerate(chains)]
        outs = [_mm(qg_ref[bi, sl, hsl[i]], s_bf[i]) + _mm(qk_ref[bi, sl, h * LANES:h * LANES + c], v_new[i])
                for i, (bi, h) in enumerate(chains)]
        for i, (bi, h) in enumerate(chains):
            s_ref[bi, h] = (e_last[bi][:, AB_APRE_LANE + h:AB_APRE_LANE + h + 1] * s_old[i]
                            + lax.dot_general(kg_ref[bi, sl, hsl[i]], v_new[i], (((0,), (0,)), ((), ())),
                                              preferred_element_type=F32))
        for i, (bi, h) in enumerate(chains):
            o = outs[i]
            o = o * lax.rsqrt(jnp.mean(o * o, axis=-1, keepdims=True) + NORM_EPS) * ng
            o_ref[bi, sl, hsl[i]] = (o * jax.nn.silu(z_ref[bi, sl, hsl[i]])).astype(o_ref.dtype)
        return carry

    lax.fori_loop(0, nchunks, recurrence, 0)

    @pl.when(t == last_t)
    def _():
        sout_ref[...] = s_ref[...]


def _gdn(proj3, conv_prev, conv_w, alog_row, dtb_row, norm_g_row, s0, *, c, tb, bb):
    bsz, l, _ = proj3.shape
    nblk = l // tb
    assert AB_QKV % B_QKV == 0 and AB_ZB % B_VW == 0 and bsz % bb == 0
    in_specs = [
        pl.BlockSpec((bb, tb, B_QKV), lambda b, t: (b, t, AB_QKV // B_QKV)),
        pl.BlockSpec((bb, tb, B_VW), lambda b, t: (b, t, AB_ZB // B_VW)),
        pl.BlockSpec((bb, tb, LANES), lambda b, t: (b, t, AB_SMALL // LANES)),
        pl.BlockSpec((B_CONV, B_QKV), lambda b, t: (0, 0)),
        pl.BlockSpec((bb, B_CONV - 1, B_QKV), lambda b, t: (b, 0, 0)),
        pl.BlockSpec((1, LANES), lambda b, t: (0, 0)),
        pl.BlockSpec((1, LANES), lambda b, t: (0, 0)),
        pl.BlockSpec((1, B_DV), lambda b, t: (0, 0)),
        pl.BlockSpec((bb, B_HEADS, B_DK, B_DV), lambda b, t: (b, 0, 0, 0)),
    ]
    out_specs = [
        pl.BlockSpec((bb, tb, B_VW), lambda b, t: (b, t, 0)),
        pl.BlockSpec((bb, B_HEADS, B_DK, B_DV), lambda b, t: (b, 0, 0, 0)),
    ]
    blocks = bb * (tb * (B_QKV + B_VW + LANES) * 4 + tb * B_VW * 2 + 2 * B_HEADS * B_DK * B_DV * 4)
    scratch = bb * (B_HEADS * B_DK * B_DV * 4 + SUBLANES * B_QKV * 4 + tb * B_VW * 4
                    + 4 * tb * B_KW * 2 + SUBLANES * LANES * 4)
    return pl.pallas_call(
        functools.partial(_gdn_kernel, c=c, nchunks=tb // c),
        out_shape=[jax.ShapeDtypeStruct((bsz, l, B_VW), BF16),
                   jax.ShapeDtypeStruct((bsz, B_HEADS, B_DK, B_DV), F32)],
        grid=(bsz // bb, nblk),
        in_specs=in_specs,
        out_specs=out_specs,
        scratch_shapes=[pltpu.VMEM((bb, B_HEADS, B_DK, B_DV), F32),
                        pltpu.VMEM((bb, SUBLANES, B_QKV), F32),
                        pltpu.VMEM((bb, tb, B_VW), F32),
                        pltpu.VMEM((bb, tb, B_KW), BF16),
                        pltpu.VMEM((bb, tb, B_KW), BF16),
                        pltpu.VMEM((bb, tb, B_KW), BF16),
                        pltpu.VMEM((bb, tb, B_HEADS * LANES), BF16),
                        pltpu.VMEM((bb, SUBLANES, LANES), F32)],
        compiler_params=pltpu.CompilerParams(
            dimension_semantics=("parallel", "arbitrary"),
            vmem_limit_bytes=_vmem_limit(blocks, scratch)),
        name="gdn_mixer",
    )(proj3, proj3, proj3, conv_w, conv_prev, alog_row, dtb_row, norm_g_row, s0)


def _mlstm_kernel(q_ref, k_ref, v_ref, og_ref, z_ref, sm_ref, ib_ref, fb_ref, ng_ref,
                  c0_ref, n0_ref, m0_ref, o_ref, cout_ref, nout_ref, mout_ref,
                  c_ref, n_ref, m_ref, hi_ref, mi_ref, ni_ref, bc_ref, kv_ref, ks_ref, mc_ref, bl_ref,
                  *, c, nchunks):
    t = pl.program_id(1)
    last_t = pl.num_programs(1) - 1
    bb = q_ref.shape[0]
    chains = [(bi, h) for bi in range(bb) for h in range(D_HEADS)]
    ksl = [slice(h * D_DK, (h + 1) * D_DK) for _, h in chains]
    vsl = [slice(h * D_DV, (h + 1) * D_DV) for _, h in chains]
    idx = range(len(chains))

    @pl.when(t == 0)
    def _():
        c_ref[...] = c0_ref[...]
        n_ref[...] = n0_ref[...]
        for bi, h in chains:
            m_ref[bi, h] = jnp.broadcast_to(m0_ref[bi, :, h:h + 1], (1, LANES))

    causal, _ = _causal_masks(c)
    tri = causal.astype(BF16)
    sel = _lane_selector(0)
    lane = lax.broadcasted_iota(jnp.int32, (c, LANES), 1)
    ib = ib_ref[...]
    fb = fb_ref[...]

    group = MLSTM_INTRA_CHUNKS if nchunks % MLSTM_INTRA_CHUNKS == 0 else 1

    def intra(n, carry):
        items = [(bi, ci, h) for ci in range(group) for bi, h in chains]
        sls = [pl.ds(pl.multiple_of((n * group + ci) * c, c), c) for ci in range(group)]
        sm = {(bi, ci): sm_ref[bi, sls[ci], :] for ci in range(group) for bi in range(bb)}
        i_full = {key: x + ib for key, x in sm.items()}
        b_full = {key: _cumsum_rows(tri, jax.nn.log_sigmoid(x + fb)) for key, x in sm.items()}
        rows = {key: _select_rows(sel, jnp.where(lane < CD_F_LANE, i_full[key], b_full[key]))
                for key in sm}
        b_col = [b_full[bi, ci][:, CD_F_LANE + h:CD_F_LANE + h + 1] for bi, ci, h in items]
        i_col = [i_full[bi, ci][:, CD_I_LANE + h:CD_I_LANE + h + 1] for bi, ci, h in items]
        logw = [jnp.where(causal, b_col[i] - rows[bi, ci][CD_F_LANE + h:CD_F_LANE + h + 1, :]
                          + rows[bi, ci][CD_I_LANE + h:CD_I_LANE + h + 1, :], -jnp.inf)
                for i, (bi, ci, h) in enumerate(items)]
        ids = range(len(items))
        m_intra = [jnp.max(logw[i], axis=-1, keepdims=True) for i in ids]
        ks_ = [slice(h * D_DK, (h + 1) * D_DK) for _, _, h in items]
        vs_ = [slice(h * D_DV, (h + 1) * D_DV) for _, _, h in items]
        q_bf = [(q_ref[bi, sls[ci], ks_[i]] * (D_DK ** -0.5)).astype(BF16) for i, (bi, ci, h) in enumerate(items)]
        k = [k_ref[bi, sls[ci], ks_[i]] for i, (bi, ci, h) in enumerate(items)]
        v = [v_ref[bi, sls[ci], vs_[i]].astype(BF16) for i, (bi, ci, h) in enumerate(items)]
        p = [jnp.exp(logw[i] - m_intra[i]) * _dot_nt(q_bf[i], k[i]) for i in ids]
        n_intra = [jnp.sum(p[i], axis=-1, keepdims=True) for i in ids]
        m_chunk = [m_intra[i][c - 1:c, :] for i in ids]
        k_w = [k[i] * jnp.exp(b_col[i][c - 1:c, :] - b_col[i] + i_col[i] - m_chunk[i]) for i in ids]
        rep = lambda x: jnp.broadcast_to(x, (x.shape[0], LANES))
        for i, (bi, ci, h) in enumerate(items):
            row = pl.ds(n * group + ci, 1)
            hi_ref[bi, sls[ci], vs_[i]] = _dot(p[i], v[i])
            kv_ref[bi, n * group + ci, h] = _dot_tn(k_w[i], v[i])
            ks_ref[bi, h, row, :] = jnp.sum(k_w[i], axis=0, keepdims=True)
            mi_ref[bi, h, sls[ci], :] = rep(m_intra[i])
            ni_ref[bi, h, sls[ci], :] = rep(n_intra[i])
            bc_ref[bi, h, sls[ci], :] = rep(b_col[i])
            mc_ref[bi, h, row, :] = rep(m_chunk[i])
            bl_ref[bi, h, row, :] = rep(b_col[i][c - 1:c, :])
        return carry

    lax.fori_loop(0, nchunks // group, intra, 0)

    twice = lambda x: jnp.concatenate([x, x], axis=1)

    def body(n, carry):
        sl = pl.ds(pl.multiple_of(n * c, c), c)
        row = pl.ds(n, 1)
        q = [q_ref[chains[i][0], sl, ksl[i]] * (D_DK ** -0.5) for i in idx]
        c_s = [c_ref[bi, h] for bi, h in chains]
        n_s = [n_ref[bi, h:h + 1, :] for bi, h in chains]
        m_s = [m_ref[bi, h] for bi, h in chains]
        qc = [_dot(q[i], c_s[i]) for i in idx]
        b_last = [bl_ref[bi, h, row, :] for bi, h in chains]
        m_chunk = [mc_ref[bi, h, row, :] for bi, h in chains]
        for i, (bi, h) in enumerate(chains):
            m_new = jnp.maximum(b_last[i] + m_s[i], m_chunk[i])
            w_old = jnp.exp(b_last[i] + m_s[i] - m_new)
            w_new = jnp.exp(m_chunk[i] - m_new)
            c_ref[bi, h] = twice(w_old) * c_s[i] + twice(w_new) * kv_ref[bi, n, h]
            n_ref[bi, h:h + 1, :] = w_old * n_s[i] + w_new * ks_ref[bi, h, row, :]
            m_ref[bi, h] = m_new
        for i, (bi, h) in enumerate(chains):
            m_intra = mi_ref[bi, h, sl, :]
            a = bc_ref[bi, h, sl, :] + m_s[i]
            m_t = jnp.maximum(a, m_intra)
            w_a = jnp.exp(a - m_t)
            w_i = jnp.exp(m_intra - m_t)
            num = twice(w_a) * qc[i] + twice(w_i) * hi_ref[bi, sl, vsl[i]]
            den = w_a * jnp.sum(q[i] * n_s[i], axis=-1, keepdims=True) + w_i * ni_ref[bi, h, sl, :]
            hh = num / twice(jnp.maximum(jnp.abs(den), jnp.exp(-m_t)))
            hd = jax.nn.sigmoid(og_ref[bi, sl, vsl[i]]) * hh
            oc = hd - jnp.mean(hd, axis=-1, keepdims=True)
            o = oc * lax.rsqrt(jnp.mean(oc * oc, axis=-1, keepdims=True) + NORM_EPS) * ng_ref[:, vsl[i]]
            o_ref[bi, sl, vsl[i]] = (o * jax.nn.silu(z_ref[bi, sl, vsl[i]])).astype(o_ref.dtype)
        return carry

    lax.fori_loop(0, nchunks, body, 0)

    @pl.when(t == last_t)
    def _():
        cout_ref[...] = c_ref[...]
        nout_ref[...] = n_ref[...]
        for bi, h in chains:
            mout_ref[bi, :, h:h + 1] = m_ref[bi, h][:, 0:1]


def _mlstm(proj3, ib_row, fb_row, norm_g_row, c0, n0, m0, *, c, tb, bb):
    bsz, l, _ = proj3.shape
    nblk = l // tb
    assert bsz % bb == 0
    tok = lambda col: (lambda b, t: (b, t, col))
    st4 = lambda b, t: (b, 0, 0, 0)
    st3 = lambda b, t: (b, 0, 0)
    in_specs = [
        pl.BlockSpec((bb, tb, D_KW), tok(CD_Q // D_KW)),
        pl.BlockSpec((bb, tb, D_KW), tok(CD_K // D_KW)),
        pl.BlockSpec((bb, tb, D_VW), tok(CD_V // D_VW)),
        pl.BlockSpec((bb, tb, D_VW), tok(CD_O // D_VW)),
        pl.BlockSpec((bb, tb, D_VW), tok(CD_ZD // D_VW)),
        pl.BlockSpec((bb, tb, LANES), tok(CD_SMALL // LANES)),
        pl.BlockSpec((1, LANES), lambda b, t: (0, 0)),
        pl.BlockSpec((1, LANES), lambda b, t: (0, 0)),
        pl.BlockSpec((1, D_VW), lambda b, t: (0, 0)),
        pl.BlockSpec((bb, D_HEADS, D_DK, D_DV), st4),
        pl.BlockSpec((bb, D_HEADS, D_DK), st3),
        pl.BlockSpec((bb, 1, D_HEADS), st3),
    ]
    out_specs = [
        pl.BlockSpec((bb, tb, D_VW), lambda b, t: (b, t, 0)),
        pl.BlockSpec((bb, D_HEADS, D_DK, D_DV), st4),
        pl.BlockSpec((bb, D_HEADS, D_DK), st3),
        pl.BlockSpec((bb, 1, D_HEADS), st3),
    ]
    state = bb * D_HEADS * D_DK * D_DV * 4
    blocks = bb * tb * (2 * D_KW + 3 * D_VW + LANES) * 4 + bb * tb * D_VW * 2 + 2 * state
    nchunks = tb // c
    return pl.pallas_call(
        functools.partial(_mlstm_kernel, c=c, nchunks=nchunks),
        out_shape=[jax.ShapeDtypeStruct((bsz, l, D_VW), BF16),
                   jax.ShapeDtypeStruct((bsz, D_HEADS, D_DK, D_DV), F32),
                   jax.ShapeDtypeStruct((bsz, D_HEADS, D_DK), F32),
                   jax.ShapeDtypeStruct((bsz, 1, D_HEADS), F32)],
        grid=(bsz // bb, nblk),
        in_specs=in_specs,
        out_specs=out_specs,
        scratch_shapes=[pltpu.VMEM((bb, D_HEADS, D_DK, D_DV), F32),
                        pltpu.VMEM((bb, D_HEADS, D_DK), F32),
                        pltpu.VMEM((bb, D_HEADS, 1, LANES), F32),
                        pltpu.VMEM((bb, tb, D_VW), F32),
                        pltpu.VMEM((bb, D_HEADS, tb, LANES), F32),
                        pltpu.VMEM((bb, D_HEADS, tb, LANES), F32),
                        pltpu.VMEM((bb, D_HEADS, tb, LANES), F32),
                        pltpu.VMEM((bb, nchunks, D_HEADS, D_DK, D_DV), F32),
                        pltpu.VMEM((bb, D_HEADS, _round_up(nchunks, SUBLANES), D_DK), F32),
                        pltpu.VMEM((bb, D_HEADS, _round_up(nchunks, SUBLANES), LANES), F32),
                        pltpu.VMEM((bb, D_HEADS, _round_up(nchunks, SUBLANES), LANES), F32)],
        compiler_params=pltpu.CompilerParams(
            dimension_semantics=("parallel", "arbitrary"),
            vmem_limit_bytes=_vmem_limit(blocks, state * (1 + nchunks) + bb * tb * (D_VW + 2 * LANES) * 4)),
        name="mlstm_mixer",
    )(proj3, proj3, proj3, proj3, proj3, proj3, ib_row, fb_row, norm_g_row, c0, n0, m0)


def _s5_expand_operators(kc_ref, bc_ref, cc_ref, bd_ref, bst_ref, cst_ref):
    tc, gt, sw, cg = S5_CHUNK, S5_GROUP_BLOCK, 2 * C_STATE, C_GROUP
    w_t = tc * LANES
    iota = lambda shape, d: lax.broadcasted_iota(jnp.int32, shape, d)
    row_g = (iota((w_t, LANES), 0) // cg) % gt
    tile16 = jnp.where(iota((cg, LANES), 1) % cg == iota((cg, LANES), 0), 1.0, 0.0).astype(BF16)
    bd = jnp.where(row_g == iota((w_t, LANES), 1) // cg, _mm(kc_ref[...], tile16), 0.0).astype(BF16)
    bd_ref[:, LANES:2 * LANES] = bd
    bd_ref[0:w_t - LANES, 0:LANES] = bd[LANES:, :]
    bd_ref[w_t - LANES:w_t, 0:LANES] = jnp.zeros((LANES, LANES), BF16)
    bc = bc_ref[...].astype(F32)
    for g in range(gt):
        bst_ref[:, g * sw:(g + 1) * sw] = jnp.where(row_g == g, bc, 0.0).astype(BF16)
    src, dst = iota((tc * cg, w_t), 0), iota((tc * cg, w_t), 1)
    spread = jnp.where((src // cg == dst // LANES) & (src % cg == dst % cg), 1.0, 0.0).astype(BF16)
    lane_g = (iota((sw, w_t), 1) // cg) % gt
    for g in range(gt):
        full = _mm(cc_ref[g * sw:(g + 1) * sw, :], spread)
        cst_ref[g * sw:(g + 1) * sw, :] = jnp.where(lane_g == g, full, 0.0).astype(BF16)


def _s5_kernel(u_ref, kc_ref, bc_ref, cc_ref, apow_ref, x0_ref, y_ref, xf_ref, bd_ref, bst_ref, cst_ref,
               *, nc, bb):
    m = nc * bb
    tc = S5_CHUNK
    sw = 2 * C_STATE

    @pl.when(pl.program_id(1) == 0)
    def _():
        _s5_expand_operators(kc_ref, bc_ref, cc_ref, bd_ref, bst_ref, cst_ref)

    row = lax.broadcasted_iota(jnp.int32, (m, sw), 0)
    n_idx = row & (nc - 1)
    n_log = int(math.log2(nc))

    def cmul(a1, a2, x):
        return a1 * x + a2 * pltpu.roll(x, C_STATE, 1)

    lhs = jnp.concatenate([u_ref[pl.ds(tau, m, stride=tc), :].astype(BF16) for tau in range(tc)], axis=1)
    e_all = _mm(lhs, bst_ref[...])
    groups = range(S5_GROUP_BLOCK)
    gsl = [slice(g * sw, (g + 1) * sw) for g in groups]
    x0_rows = []
    for g in groups:
        rows0 = jnp.zeros((m, sw), F32)
        for b in range(bb):
            rows0 = jnp.where(row == b * nc, x0_ref[b, :, gsl[g]], rows0)
        x0_rows.append(rows0)
    x = [e_all[:, gsl[g]] + cmul(apow_ref[0:1, gsl[g]], apow_ref[1:2, gsl[g]], x0_rows[g]) for g in groups]
    for j in range(n_log):
        sh = 1 << j
        shifted = [jnp.where(n_idx >= sh, pltpu.roll(x[g], sh, 0), 0.0) for g in groups]
        x = [x[g] + cmul(apow_ref[2 * j:2 * j + 1, gsl[g]], apow_ref[2 * j + 1:2 * j + 2, gsl[g]], shifted[g])
             for g in groups]
    x_start = [jnp.where(n_idx >= 1, pltpu.roll(x[g], 1, 0), x0_rows[g]).astype(BF16) for g in groups]
    for g in groups:
        for b in range(bb):
            xf_ref[b, :, gsl[g]] = x[g][b * nc + nc - 1:b * nc + nc, :]
    y_state = _mm(jnp.concatenate(x_start, axis=1), cst_ref[...])
    for tau in range(0, tc, 2):
        width = (tau + 2) * LANES
        start = (tc - 2 - tau) * LANES
        y = _mm(lhs[:, :width], bd_ref[start:start + width, :]) + y_state[:, tau * LANES:(tau + 2) * LANES]
        y_ref[pl.ds(tau, m, stride=tc), :] = y[:, :LANES]
        y_ref[pl.ds(tau + 1, m, stride=tc), :] = y[:, LANES:]


def _s5_chunks(proj2d, kc, bc, cc, apow, x0, *, l, bb):
    tc = S5_CHUNK
    nc = l // tc
    bsz = x0.shape[1]
    nt = C_GROUPS // S5_GROUP_BLOCK
    rows = bb * l
    sw = 2 * C_STATE
    sw_t = S5_GROUP_BLOCK * sw
    w_t = tc * LANES
    assert CD_U % LANES == 0 and bsz % bb == 0
    blocks = (2 * rows * LANES * 4 + 2 * w_t * LANES * 2 + sw_t * tc * C_GROUP * 2 + apow.shape[1] * sw_t * 4
              + 2 * bb * SUBLANES * sw_t * 4)
    scratch = w_t * 2 * LANES * 2 + 2 * w_t * sw_t * 2
    temps = bb * nc * (w_t * 2 + w_t * 4 + 3 * sw_t * 4) + 4 * sw * w_t * 4
    return pl.pallas_call(
        functools.partial(_s5_kernel, nc=nc, bb=bb),
        out_shape=[jax.ShapeDtypeStruct((bsz * l, C_W), F32),
                   jax.ShapeDtypeStruct((nt, bsz, 1, sw_t), F32)],
        grid=(nt, bsz // bb),
        in_specs=[pl.BlockSpec((rows, LANES), lambda i, j: (j, CD_U // LANES + i)),
                  pl.BlockSpec((None, w_t, C_GROUP), lambda i, j: (i, 0, 0)),
                  pl.BlockSpec((None, w_t, sw), lambda i, j: (i, 0, 0)),
                  pl.BlockSpec((None, sw_t, tc * C_GROUP), lambda i, j: (i, 0, 0)),
                  pl.BlockSpec((None, apow.shape[1], sw_t), lambda i, j: (i, 0, 0)),
                  pl.BlockSpec((None, bb, 1, sw_t), lambda i, j: (i, j, 0, 0))],
        out_specs=[pl.BlockSpec((rows, LANES), lambda i, j: (j, i)),
                   pl.BlockSpec((None, bb, 1, sw_t), lambda i, j: (i, j, 0, 0))],
        scratch_shapes=[pltpu.VMEM((w_t, 2 * LANES), BF16),
                        pltpu.VMEM((w_t, sw_t), BF16),
                        pltpu.VMEM((sw_t, w_t), BF16)],
        compiler_params=pltpu.CompilerParams(
            dimension_semantics=("parallel", "arbitrary"),
            vmem_limit_bytes=_vmem_limit(blocks, scratch + temps)),
        name="s5_chunks",
    )(proj2d, kc, bc, cc, apow, x0)


def _s5_operators(lam_re, lam_im, log_dt, b_re, b_im, c_re, c_im, n_log):
    g, p = lam_re.shape
    tc = S5_CHUNK
    dt = jnp.exp(log_dt.astype(F32))[:, None]
    mag = jnp.exp(lam_re * dt)
    ab_re, ab_im = mag * jnp.cos(lam_im * dt), mag * jnp.sin(lam_im * dt)
    den = lam_re * lam_re + lam_im * lam_im
    er = ab_re - 1.0
    zr = (er * lam_re + ab_im * lam_im) / den
    zi = (ab_im * lam_re - er * lam_im) / den
    bb_re = zr[..., None] * b_re - zi[..., None] * b_im
    bb_im = zr[..., None] * b_im + zi[..., None] * b_re
    pw_re, pw_im = ab_re[None], ab_im[None]
    while pw_re.shape[0] < tc:
        top_re, top_im = pw_re[-1], pw_im[-1]
        pw_re, pw_im = (jnp.concatenate([pw_re, top_re * pw_re - top_im * pw_im]),
                        jnp.concatenate([pw_im, top_re * pw_im + top_im * pw_re]))
    pw_re = jnp.concatenate([jnp.ones_like(ab_re)[None], pw_re])
    pw_im = jnp.concatenate([jnp.zeros_like(ab_im)[None], pw_im])
    abr = pw_re[:tc, :, :, None] * bb_re - pw_im[:tc, :, :, None] * bb_im
    abi = pw_re[:tc, :, :, None] * bb_im + pw_im[:tc, :, :, None] * bb_re
    kern = (jnp.einsum('gjp,dgpi->dgji', c_re, abr, precision=HIGHEST)
            - jnp.einsum('gjp,dgpi->dgji', c_im, abi, precision=HIGHEST))
    gt = S5_GROUP_BLOCK
    nt = g // gt
    sw = 2 * p
    kc = kern[::-1].reshape(tc, nt, gt, C_GROUP, C_GROUP).transpose(1, 0, 2, 4, 3)
    kc = kc.reshape(nt, tc * gt * C_GROUP, C_GROUP)
    ab = jnp.concatenate([abr, abi], axis=2)[::-1]
    bc = ab.reshape(tc, nt, gt, sw, C_GROUP).transpose(1, 0, 2, 4, 3).reshape(nt, tc * gt * C_GROUP, sw)
    cr = c_re[None] * pw_re[1:, :, None, :] - c_im[None] * pw_im[1:, :, None, :]
    ci = -(c_re[None] * pw_im[1:, :, None, :] + c_im[None] * pw_re[1:, :, None, :])
    cc = jnp.concatenate([cr, ci], axis=3)
    cc = cc.reshape(tc, nt, gt, C_GROUP, sw).transpose(1, 2, 4, 0, 3).reshape(nt, gt * sw, tc * C_GROUP)
    r, i = pw_re[tc], pw_im[tc]
    rows = []
    for _ in range(max(n_log, 1)):
        rows += [jnp.concatenate([r, r], -1), jnp.concatenate([-i, i], -1)]
        r, i = r * r - i * i, 2.0 * r * i
    apow = jnp.stack(rows, axis=1)
    apow = apow.reshape(nt, gt, -1, sw).transpose(0, 2, 1, 3).reshape(nt, -1, gt * sw)
    return kc.astype(BF16), bc.astype(BF16), cc.astype(BF16), apow


def _s5(proj2d, ops, x0_re, x0_im, *, l):
    kc, bc, cc, apow = ops
    bsz = x0_re.shape[0]
    nt = C_GROUPS // S5_GROUP_BLOCK
    x0 = jnp.concatenate([x0_re, x0_im], axis=-1).reshape(bsz, nt, 1, -1).transpose(1, 0, 2, 3)
    bb = bsz if bsz * l <= 4096 else 1
    y, xf = _s5_chunks(proj2d, kc, bc, cc, apow, x0, l=l, bb=bb)
    xf = xf.transpose(1, 0, 2, 3).reshape(bsz, C_GROUPS, 2 * C_STATE)
    return y, xf[..., :C_STATE], xf[..., C_STATE:]


AB_SRC_GA = AB_QKV
AB_SRC_QKV = AB_SRC_GA + A_GATE_RANK
AB_SRC_TAIL = AB_SRC_QKV + B_QKV + B_VW
IN_AB = AB_SRC_TAIL + 2 * B_HEADS
IN_CD = CD_SMALL + 2 * D_HEADS
WPREP_AB_TILE = 512
WPREP_CD_TILE = 640


def _prep_w_ab_kernel(wt_ref, ga_ref, o_ref):
    j = pl.program_id(0)
    tile = o_ref.shape[1]
    n_small = A_GATE_RANK + 2 * B_HEADS

    @pl.when(j < AB_SMALL // tile)
    def _():
        o_ref[...] = wt_ref[...].T.astype(BF16)

    @pl.when(j == AB_SMALL // tile)
    def _():
        rows = jnp.concatenate([ga_ref[...], wt_ref[tile - 2 * B_HEADS:tile, :],
                                jnp.zeros((tile - n_small, wt_ref.shape[1]), F32)], axis=0)
        o_ref[...] = rows.T.astype(BF16)

    @pl.when(j > AB_SMALL // tile)
    def _():
        o_ref[...] = jnp.zeros(o_ref.shape, BF16)


def _prep_w_ab(w_t, n_out):
    n_in, d = w_t.shape
    tile = WPREP_AB_TILE
    assert AB_QKV % tile == 0 and AB_SMALL % tile == 0 and n_out % tile == 0 and n_in >= tile

    unit = A_GATE_RANK
    assert tile % unit == 0 and (AB_SRC_QKV - AB_QKV) % unit == 0 and (n_in - tile) % unit == 0

    def src_row(j):
        k = j * (tile // unit)
        k = jnp.where(j < AB_QKV // tile, k,
                      jnp.where(j < AB_SMALL // tile, k + (AB_SRC_QKV - AB_QKV) // unit, (n_in - tile) // unit))
        return unit * k

    return pl.pallas_call(
        _prep_w_ab_kernel,
        out_shape=jax.ShapeDtypeStruct((d, n_out), BF16),
        grid=(n_out // tile,),
        in_specs=[pl.BlockSpec((pl.Element(tile), pl.Element(d)), lambda j: (src_row(j), 0)),
                  pl.BlockSpec((pl.Element(A_GATE_RANK), pl.Element(d)), lambda j: (AB_SRC_GA, 0))],
        out_specs=pl.BlockSpec((d, tile), lambda j: (0, j)),
        compiler_params=pltpu.CompilerParams(
            dimension_semantics=("parallel",),
            vmem_limit_bytes=_vmem_limit(tile * d * 6, 2 * tile * d * 4)),
        name="prep_w_in_ab",
    )(w_t, w_t)


def _prep_w_cd_kernel(wt_ref, o_ref):
    tile = o_ref.shape[1]
    row = pl.program_id(0) * tile + lax.broadcasted_iota(jnp.int32, wt_ref.shape, 0)
    o_ref[...] = jnp.where(row < IN_CD, wt_ref[...], 0.0).T.astype(BF16)


def _prep_w_cd(w_t, n_out):
    n_in, d = w_t.shape
    tile = WPREP_CD_TILE
    assert n_out % tile == 0
    return pl.pallas_call(
        _prep_w_cd_kernel,
        out_shape=jax.ShapeDtypeStruct((d, n_out), BF16),
        grid=(n_out // tile,),
        in_specs=[pl.BlockSpec((tile, d), lambda j: (j, 0))],
        out_specs=pl.BlockSpec((d, tile), lambda j: (0, j)),
        compiler_params=pltpu.CompilerParams(
            dimension_semantics=("parallel",),
            vmem_limit_bytes=_vmem_limit(tile * d * 6, 2 * tile * d * 4)),
        name="prep_w_in_cd",
    )(w_t)


def _lane_row(vals, lane0):
    return jnp.zeros((1, LANES), F32).at[0, lane0:lane0 + vals.shape[0]].set(vals.astype(F32))


def _prepare_weights(norm_g, final_norm_g, w_in_ab, a_gate_w, a_gate_b, a_norm_g, b_conv_w, b_a_log,
                     b_dt_bias, b_norm_g, w_out_ab, w_in_cd, c_lam_re, c_lam_im, c_log_dt, c_b_re,
                     c_b_im, c_c_re, c_c_im, c_d, c_glu_w, c_glu_b, d_i_bias, d_f_bias, d_norm_g,
                     w_out_cd, n_log):
    assert w_in_ab.shape[1] == IN_AB and w_in_cd.shape[1] == IN_CD
    w_ab = _prep_w_ab(w_in_ab.astype(F32).T, _round_up(AB_SMALL + LANES, PROJ_TN))
    w_cd = _prep_w_cd(w_in_cd.astype(F32).T, _round_up(CD_SMALL + LANES, PROJ_TN))
    gate_w = jnp.zeros((LANES, A_KW), F32).at[AB_GA_LANE:AB_GA_LANE + A_GATE_RANK].set(
        a_gate_w.astype(F32)).astype(BF16)
    return dict(
        norm_g=norm_g.astype(F32), final_g=final_norm_g.astype(F32)[None, :],
        w_ab=w_ab, w_cd=w_cd, gate_w=gate_w, gate_b=a_gate_b.astype(F32)[None, :],
        a_norm_g=a_norm_g.astype(F32)[None, :], conv_w=b_conv_w.astype(F32),
        alog=_lane_row(b_a_log, AB_APRE_LANE), dtb=_lane_row(b_dt_bias, AB_APRE_LANE),
        b_norm_g=b_norm_g.astype(F32)[None, :],
        w_out_a=w_out_ab[:A_VW].astype(BF16), w_out_b=w_out_ab[A_VW:].astype(BF16),
        s5_ops=_s5_operators(c_lam_re.astype(F32), c_lam_im.astype(F32), c_log_dt, c_b_re.astype(F32),
                             c_b_im.astype(F32), c_c_re.astype(F32), c_c_im.astype(F32), n_log),
        c_d=c_d.astype(F32).reshape(1, C_W), glu_w=c_glu_w.astype(BF16),
        glu_b=c_glu_b.astype(F32)[None, :],
        ib=_lane_row(d_i_bias, CD_I_LANE), fb=_lane_row(d_f_bias, CD_F_LANE),
        d_norm_g=d_norm_g.astype(F32)[None, :],
        w_out_c=w_out_cd[:C_W].astype(BF16), w_out_d=w_out_cd[C_W:].astype(BF16),
    )


def _trunk(x, conv_prev, s_gla0, s_gdn0, s5_re0, s5_im0, mc0, mn0, mm0, w):
    bsz, l, d = x.shape
    c = min(CHUNK, l)
    tb = min(l, 8 * c)
    assert l % tb == 0 and l % S5_CHUNK == 0
    x2d = x.reshape(bsz * l, d)

    proj = _norm_matmul(x2d, w['norm_g'][0:1], w['w_ab'])
    proj3 = proj.reshape(bsz, l, proj.shape[1])
    bb = MIXER_STREAMS if bsz % MIXER_STREAMS == 0 else 1
    tb_s = min(l, (8 // bb) * c)
    o_a, s_gla = _gla(proj3, w['gate_w'], w['gate_b'], w['a_norm_g'], s_gla0.astype(F32), c=c, tb=tb_s, bb=bb)
    o_b, s_gdn = _gdn(proj3, conv_prev.astype(F32), w['conv_w'], w['alog'], w['dtb'], w['b_norm_g'],
                      s_gdn0.astype(F32), c=c, tb=tb_s, bb=bb)
    conv_new = proj3[:, l - (B_CONV - 1):, AB_QKV:AB_QKV + B_QKV]
    h1 = _out_proj(o_a.reshape(bsz * l, A_VW), o_b.reshape(bsz * l, B_VW), w['w_out_a'], w['w_out_b'], x2d)

    proj = _norm_matmul(h1, w['norm_g'][1:2], w['w_cd'])
    proj3 = proj.reshape(bsz, l, proj.shape[1])
    y, s5_re, s5_im = _s5(proj, w['s5_ops'], s5_re0.astype(F32), s5_im0.astype(F32), l=l)
    o_d, mc, mn, mm = _mlstm(proj3, w['ib'], w['fb'], w['d_norm_g'], mc0.astype(F32),
                             mn0.astype(F32), mm0.astype(F32)[:, None, :], c=c, tb=tb_s, bb=bb)
    y_out = _glu_out_proj_norm(y, proj, o_d.reshape(bsz * l, D_VW), w['c_d'], w['glu_w'], w['glu_b'],
                               w['w_out_c'], w['w_out_d'], h1, w['final_g'])
    dt = x.dtype
    return (y_out.reshape(bsz, l, d).astype(dt), conv_new.astype(dt), s_gla.astype(dt), s_gdn.astype(dt),
            s5_re.astype(dt), s5_im.astype(dt), mc.astype(dt), mn.astype(dt), mm[:, 0, :].astype(dt))


def kernel(x_prompt, x_sample, cache_gdn_conv, state_gla, state_gdn, state_s5_re, state_s5_im,
           state_mlstm_c, state_mlstm_n, state_mlstm_m, norm_g, final_norm_g, w_in_ab, a_gate_w,
           a_gate_b, a_norm_g, b_conv_w, b_a_log, b_dt_bias, b_norm_g, w_out_ab, w_in_cd, c_lam_re,
           c_lam_im, c_log_dt, c_b_re, c_b_im, c_c_re, c_c_im, c_d, c_glu_w, c_glu_b, d_i_bias,
           d_f_bias, d_norm_g, w_out_cd):
    n_log = int(math.log2(max(x_prompt.shape[1], x_sample.shape[1]) // S5_CHUNK))
    w = _prepare_weights(norm_g, final_norm_g, w_in_ab, a_gate_w, a_gate_b, a_norm_g, b_conv_w, b_a_log,
                         b_dt_bias, b_norm_g, w_out_ab, w_in_cd, c_lam_re, c_lam_im, c_log_dt, c_b_re,
                         c_b_im, c_c_re, c_c_im, c_d, c_glu_w, c_glu_b, d_i_bias, d_f_bias, d_norm_g,
                         w_out_cd, n_log)
    nb = x_prompt.shape[0]
    zeros = lambda *shape: jnp.zeros(shape, F32)
    p_out = _trunk(x_prompt, zeros(nb, B_CONV - 1, B_QKV), zeros(nb, A_HEADS, A_DK, A_DV),
                   zeros(nb, B_HEADS, B_DK, B_DV), zeros(nb, C_GROUPS, C_STATE), zeros(nb, C_GROUPS, C_STATE),
                   zeros(nb, D_HEADS, D_DK, D_DV), zeros(nb, D_HEADS, D_DK), zeros(nb, D_HEADS), w)
    s_out = _trunk(x_sample, cache_gdn_conv, state_gla, state_gdn, state_s5_re, state_s5_im,
                   state_mlstm_c, state_mlstm_n, state_mlstm_m, w)
    return (p_out[0], s_out[0]) + tuple(p_out[1:]) + tuple(s_out[1:])
```

```python
import functools
import math

import jax
import jax.numpy as jnp
from jax import lax
from jax.experimental import pallas as pl
from jax.experimental.pallas import tpu as pltpu

F32 = jnp.float32
BF16 = jnp.bfloat16
HIGHEST = lax.Precision.HIGHEST

NORM_EPS = 1e-6
CHUNK = 64
A_HEADS, A_DK, A_DV, A_GATE_RANK, A_GATE_TAU = 4, 128, 256, 16, 16.0
B_HEADS, B_DK, B_DV, B_CONV = 8, 128, 128, 4
C_GROUP, C_GROUPS, C_STATE = 16, 64, 64
D_HEADS, D_DK, D_DV = 4, 128, 256
A_KW, A_VW = A_HEADS * A_DK, A_HEADS * A_DV
B_KW, B_VW = B_HEADS * B_DK, B_HEADS * B_DV
B_QKV = 2 * B_KW + B_VW
C_W = C_GROUPS * C_GROUP
D_KW, D_VW = D_HEADS * D_DK, D_HEADS * D_DV

LANES = 128
SUBLANES = 8
VMEM_BYTES_V7X = 64 * 1024 * 1024

AB_Q, AB_K, AB_V, AB_Z = 0, A_KW, 2 * A_KW, 2 * A_KW + A_VW
AB_QKV = AB_Z + A_VW
AB_ZB = AB_QKV + B_QKV
AB_SMALL = AB_ZB + B_VW
AB_GA_LANE, AB_BETA_LANE, AB_APRE_LANE = 0, A_GATE_RANK, A_GATE_RANK + B_HEADS
CD_U, CD_Z = 0, C_W
CD_Q = 2 * C_W
CD_K = CD_Q + D_KW
CD_V = CD_K + D_KW
CD_O = CD_V + D_VW
CD_ZD = CD_O + D_VW
CD_SMALL = CD_ZD + D_VW
CD_I_LANE, CD_F_LANE = 0, D_HEADS

PROJ_TN = 1280
S5_CHUNK = 16
S5_GROUP_BLOCK = 8
NORM_ROW_SPLITS = 2
GLU_ROW_SPLITS = 2
MIXER_STREAMS = 4
MLSTM_INTRA_CHUNKS = 4
GDN_WY_CHAINS = 32


def _round_up(x, m):
    return (x + m - 1) // m * m


def _vmem_limit(block_bytes, scratch_bytes=0):
    est = 2 * block_bytes + scratch_bytes
    return int(min(max(2 * est, 32 * 1024 * 1024), VMEM_BYTES_V7X - 8 * 1024 * 1024))


def _mm(a, b):
    return jnp.dot(a, b, preferred_element_type=F32)


def _dot(a, b):
    return _mm(a.astype(BF16), b.astype(BF16))


def _dot_nt(a, b):
    return lax.dot_general(a.astype(BF16), b.astype(BF16), (((1,), (1,)), ((), ())),
                           preferred_element_type=F32)


def _dot_tn(a, b):
    return lax.dot_general(a.astype(BF16), b.astype(BF16), (((0,), (0,)), ((), ())),
                           preferred_element_type=F32)


def _split2(x):
    hi = x.astype(BF16)
    return hi, (x - hi.astype(F32)).astype(BF16)


def _split3(x):
    hi = x.astype(BF16)
    r = x - hi.astype(F32)
    mid = r.astype(BF16)
    return hi, mid, (r - mid.astype(F32)).astype(BF16)


def _dot3(a, b):
    ah, al = _split2(a)
    bh, bl = _split2(b)
    return _mm(ah, bh) + _mm(ah, bl) + _mm(al, bh)


def _cumsum_rows(tri_bf16, x):
    hi, mid, lo = _split3(x)
    return _mm(tri_bf16, hi) + _mm(tri_bf16, mid) + _mm(tri_bf16, lo)


def _select_rows(sel_bf16, x):
    nt = lambda b: lax.dot_general(sel_bf16, b, (((1,), (1,)), ((), ())), preferred_element_type=F32)
    hi, mid, lo = _split3(x)
    return nt(hi) + nt(mid) + nt(lo)


def _lane_selector(lane0):
    r = lax.broadcasted_iota(jnp.int32, (SUBLANES, LANES), 0)
    l = lax.broadcasted_iota(jnp.int32, (SUBLANES, LANES), 1)
    return jnp.where(l == r + lane0, 1.0, 0.0).astype(BF16)


def _causal_masks(c):
    row = lax.broadcasted_iota(jnp.int32, (c, c), 0)
    col = lax.broadcasted_iota(jnp.int32, (c, c), 1)
    return row >= col, row > col


def _norm_matmul_kernel(x_ref, g_ref, w_ref, o_ref, xn_ref):
    j = pl.program_id(1)

    @pl.when(j == 0)
    def _():
        tm = x_ref.shape[0]
        sub = tm // NORM_ROW_SPLITS
        for i in range(NORM_ROW_SPLITS):
            s = slice(i * sub, (i + 1) * sub)
            x = x_ref[s, :]
            y = (x * lax.rsqrt(jnp.mean(x * x, axis=-1, keepdims=True) + NORM_EPS) * g_ref[...]).astype(BF16)
            xn_ref[s, :] = y
            o_ref[s, :] = jnp.dot(y, w_ref[...], preferred_element_type=F32)

    @pl.when(j != 0)
    def _():
        o_ref[...] = jnp.dot(xn_ref[...], w_ref[...], preferred_element_type=F32)


def _norm_matmul(x2d, g_row, w_bf16):
    m, d = x2d.shape
    n = w_bf16.shape[1]
    tm = min(m, 1024)
    tn = PROJ_TN
    assert m % tm == 0 and n % tn == 0
    blocks = tm * d * 4 + d * tn * 2 + tm * tn * 4
    return pl.pallas_call(
        _norm_matmul_kernel,
        out_shape=jax.ShapeDtypeStruct((m, n), F32),
        grid=(m // tm, n // tn),
        in_specs=[pl.BlockSpec((tm, d), lambda i, j: (i, 0)),
                  pl.BlockSpec((1, d), lambda i, j: (0, 0)),
                  pl.BlockSpec((d, tn), lambda i, j: (0, j))],
        out_specs=pl.BlockSpec((tm, tn), lambda i, j: (i, j)),
        scratch_shapes=[pltpu.VMEM((tm, d), BF16)],
        compiler_params=pltpu.CompilerParams(
            dimension_semantics=("parallel", "arbitrary"),
            vmem_limit_bytes=_vmem_limit(blocks, tm * d * 2)),
        name="norm_in_proj",
    )(x2d, g_row, w_bf16)


def _out_proj_kernel(a_ref, b_ref, wa_ref, wb_ref, h_ref, o_ref):
    out = (jnp.dot(a_ref[...], wa_ref[...], preferred_element_type=F32)
           + jnp.dot(b_ref[...], wb_ref[...], preferred_element_type=F32))
    o_ref[...] = h_ref[...] + out


def _glu_out_proj_norm_kernel(y_ref, u_ref, z_ref, od_ref, d_ref, gw_ref, gb_ref, wc_ref, wd_ref, h_ref, g_ref,
                              o_ref):
    tm = y_ref.shape[0]
    sub = tm // GLU_ROW_SPLITS
    halves = [slice(i * sub, (i + 1) * sub) for i in range(GLU_ROW_SPLITS)]
    y = [jax.nn.gelu(y_ref[s, :] + d_ref[...] * u_ref[s, :]) for s in halves]
    gate = [jax.nn.sigmoid(jnp.dot(v.astype(BF16), gw_ref[...], preferred_element_type=F32) + gb_ref[...])
            for v in y]
    o_c = [(y[i] * gate[i] * jax.nn.silu(z_ref[s, :])).astype(BF16) for i, s in enumerate(halves)]
    out = [jnp.dot(o_c[i], wc_ref[...], preferred_element_type=F32)
           + jnp.dot(od_ref[s, :], wd_ref[...], preferred_element_type=F32) for i, s in enumerate(halves)]
    for i, s in enumerate(halves):
        h = h_ref[s, :] + out[i]
        o_ref[s, :] = h * lax.rsqrt(jnp.mean(h * h, axis=-1, keepdims=True) + NORM_EPS) * g_ref[...]


def _glu_out_proj_norm(y2d, proj2d, o_d, d_row, glu_w, glu_b_row, w_c, w_d, h2d, final_g_row):
    m, d = h2d.shape
    tm = min(m, 512)
    assert m % tm == 0
    row = lambda col: (lambda i: (i, col))
    const = lambda i: (0, 0)
    resident = pl.Buffered(1)
    blocks = 3 * tm * C_W * 4 + tm * D_VW * 2 + 2 * tm * d * 4
    weights = C_W * C_W * 2 + (C_W + D_VW) * d * 2
    return pl.pallas_call(
        _glu_out_proj_norm_kernel,
        out_shape=jax.ShapeDtypeStruct((m, d), F32),
        grid=(m // tm,),
        in_specs=[pl.BlockSpec((tm, C_W), row(0)),
                  pl.BlockSpec((tm, C_W), row(CD_U // C_W)),
                  pl.BlockSpec((tm, C_W), row(CD_Z // C_W)),
                  pl.BlockSpec((tm, D_VW), row(0)),
                  pl.BlockSpec((1, C_W), const),
                  pl.BlockSpec((C_W, C_W), const, pipeline_mode=resident),
                  pl.BlockSpec((1, C_W), const),
                  pl.BlockSpec((C_W, d), const, pipeline_mode=resident),
                  pl.BlockSpec((D_VW, d), const, pipeline_mode=resident),
                  pl.BlockSpec((tm, d), row(0)),
                  pl.BlockSpec((1, d), const)],
        out_specs=pl.BlockSpec((tm, d), row(0)),
        compiler_params=pltpu.CompilerParams(
            dimension_semantics=("parallel",),
            vmem_limit_bytes=_vmem_limit(blocks, weights + 3 * tm * C_W * 4)),
        name="glu_out_proj_norm",
    )(y2d, proj2d, proj2d, o_d, d_row, glu_w, glu_b_row, w_c, w_d, h2d, final_g_row)


def _out_proj(mix_a, mix_b, w_a, w_b, h2d):
    m, d = h2d.shape
    ka, kb = mix_a.shape[1], mix_b.shape[1]
    tm = min(m, 512)
    assert m % tm == 0
    blocks = tm * (ka + kb) * 2 + (ka + kb) * d * 2 + 2 * tm * d * 4
    return pl.pallas_call(
        _out_proj_kernel,
        out_shape=jax.ShapeDtypeStruct((m, d), F32),
        grid=(m // tm,),
        in_specs=[pl.BlockSpec((tm, ka), lambda i: (i, 0)),
                  pl.BlockSpec((tm, kb), lambda i: (i, 0)),
                  pl.BlockSpec((ka, d), lambda i: (0, 0)),
                  pl.BlockSpec((kb, d), lambda i: (0, 0)),
                  pl.BlockSpec((tm, d), lambda i: (i, 0))],
        out_specs=pl.BlockSpec((tm, d), lambda i: (i, 0)),
        compiler_params=pltpu.CompilerParams(
            dimension_semantics=("parallel",),
            vmem_limit_bytes=_vmem_limit(blocks)),
        name="out_proj",
    )(mix_a, mix_b, w_a, w_b, h2d)


def _gla_kernel(q_ref, k_ref, v_ref, z_ref, sm_ref, gw_ref, gb_ref, ng_ref, s0_ref,
                o_ref, sout_ref, st_ref, *, c, nchunks):
    t = pl.program_id(1)
    last_t = pl.num_programs(1) - 1
    bb = q_ref.shape[0]
    chains = [(bi, h) for bi in range(bb) for h in range(A_HEADS)]
    ksl = [slice(h * A_DK, (h + 1) * A_DK) for _, h in chains]
    vsl = [slice(h * A_DV, (h + 1) * A_DV) for _, h in chains]
    idx = range(len(chains))

    @pl.when(t == 0)
    def _():
        for bi, h in chains:
            st_ref[bi, h] = s0_ref[bi, h].T

    causal, _ = _causal_masks(c)
    tri = causal.astype(BF16)
    gw = gw_ref[...]
    gb = gb_ref[...]

    def body(n, carry):
        sl = pl.ds(pl.multiple_of(n * c, c), c)
        b_all = [_cumsum_rows(tri, jax.nn.log_sigmoid(_dot(sm_ref[bi, sl, :], gw) + gb) * (1.0 / A_GATE_TAU))
                 for bi in range(bb)]
        b = [b_all[chains[i][0]][:, ksl[i]] for i in idx]
        b_last = [b[i][c - 1:c, :] for i in idx]
        k = [k_ref[chains[i][0], sl, ksl[i]] for i in idx]
        v = [v_ref[chains[i][0], sl, vsl[i]].astype(BF16) for i in idx]
        q_dec = [(q_ref[chains[i][0], sl, ksl[i]] * (A_DK ** -0.5) * jnp.exp(b[i])).astype(BF16) for i in idx]
        k_dec = [(k[i] * jnp.exp(-b[i])).astype(BF16) for i in idx]
        k_w = [(k[i] * jnp.exp(b_last[i] - b[i])).astype(BF16) for i in idx]
        scores = [jnp.where(causal, _dot_nt(q_dec[i], k_dec[i]), 0.0).astype(BF16) for i in idx]
        s_t = [st_ref[bi, h] for bi, h in chains]
        outs = [_mm(scores[i], v[i]) + _dot_nt(q_dec[i], s_t[i]) for i in idx]
        for i, (bi, h) in enumerate(chains):
            st_ref[bi, h] = s_t[i] * jnp.exp(b_last[i]) + _dot_tn(v[i], k_w[i])
        for i, (bi, h) in enumerate(chains):
            o = outs[i]
            o = o * lax.rsqrt(jnp.mean(o * o, axis=-1, keepdims=True) + NORM_EPS) * ng_ref[:, vsl[i]]
            o_ref[bi, sl, vsl[i]] = (o * jax.nn.silu(z_ref[bi, sl, vsl[i]])).astype(o_ref.dtype)
        return carry

    lax.fori_loop(0, nchunks, body, 0)

    @pl.when(t == last_t)
    def _():
        for bi, h in chains:
            sout_ref[bi, h] = st_ref[bi, h].T


def _gla(proj3, gate_w_pad, gate_b_row, norm_g_row, s0, *, c, tb, bb):
    bsz, l, _ = proj3.shape
    nblk = l // tb
    assert bsz % bb == 0
    tok = lambda col: (lambda b, t: (b, t, col))
    in_specs = [
        pl.BlockSpec((bb, tb, A_KW), tok(AB_Q // A_KW)),
        pl.BlockSpec((bb, tb, A_KW), tok(AB_K // A_KW)),
        pl.BlockSpec((bb, tb, A_VW), tok(AB_V // A_VW)),
        pl.BlockSpec((bb, tb, A_VW), tok(AB_Z // A_VW)),
        pl.BlockSpec((bb, tb, LANES), tok(AB_SMALL // LANES)),
        pl.BlockSpec((LANES, A_KW), lambda b, t: (0, 0)),
        pl.BlockSpec((1, A_KW), lambda b, t: (0, 0)),
        pl.BlockSpec((1, A_VW), lambda b, t: (0, 0)),
        pl.BlockSpec((bb, A_HEADS, A_DK, A_DV), lambda b, t: (b, 0, 0, 0)),
    ]
    out_specs = [
        pl.BlockSpec((bb, tb, A_VW), lambda b, t: (b, t, 0)),
        pl.BlockSpec((bb, A_HEADS, A_DK, A_DV), lambda b, t: (b, 0, 0, 0)),
    ]
    state = bb * A_HEADS * A_DK * A_DV * 4
    blocks = bb * tb * (2 * A_KW + 2 * A_VW + LANES) * 4 + bb * tb * A_VW * 2 + 2 * state
    return pl.pallas_call(
        functools.partial(_gla_kernel, c=c, nchunks=tb // c),
        out_shape=[jax.ShapeDtypeStruct((bsz, l, A_VW), BF16),
                   jax.ShapeDtypeStruct((bsz, A_HEADS, A_DK, A_DV), F32)],
        grid=(bsz // bb, nblk),
        in_specs=in_specs,
        out_specs=out_specs,
        scratch_shapes=[pltpu.VMEM((bb, A_HEADS, A_DV, A_DK), F32)],
        compiler_params=pltpu.CompilerParams(
            dimension_semantics=("parallel", "arbitrary"),
            vmem_limit_bytes=_vmem_limit(blocks, state)),
        name="gla_mixer",
    )(proj3, proj3, proj3, proj3, proj3, gate_w_pad, gate_b_row, norm_g_row, s0)


def _gdn_kernel(x_ref, z_ref, sm_ref, w_ref, cp_ref, alog_ref, dtb_ref, ng_ref, s0_ref,
                o_ref, sout_ref, s_ref, tail_ref, u_ref, wm_ref, qg_ref, kg_ref, qk_ref, gl_ref,
                *, c, nchunks):
    t = pl.program_id(1)
    last_t = pl.num_programs(1) - 1
    tb = c * nchunks
    keep = SUBLANES - (B_CONV - 1)
    bb = x_ref.shape[0]

    @pl.when(t == 0)
    def _():
        s_ref[...] = s0_ref[...]
        tail_ref[:, 0:keep, :] = jnp.zeros((bb, keep, B_QKV), F32)
        tail_ref[:, keep:SUBLANES, :] = cp_ref[...]

    def conv_silu(bi, row0, first, cols):
        x = x_ref[bi, pl.ds(row0, c), cols]
        if first is None:
            prev = x_ref[bi, pl.ds(pl.multiple_of(row0 - SUBLANES, SUBLANES), SUBLANES), cols]
        else:
            before = pl.multiple_of(jnp.maximum(row0 - SUBLANES, 0), SUBLANES)
            prev = jnp.where(first, tail_ref[bi, :, cols], x_ref[bi, pl.ds(before, SUBLANES), cols])
        w = w_ref[:, cols]
        ext = jnp.concatenate([prev, x], axis=0)
        ext1 = pltpu.roll(ext, 1, 0)
        newer = ext * w[3:4, :] + ext1 * w[2:3, :]
        older = ext * w[1:2, :] + ext1 * w[0:1, :]
        conv = newer + pltpu.roll(older, 2, 0)
        return jax.nn.silu(conv[SUBLANES:SUBLANES + c, :])

    causal, strict = _causal_masks(c)
    tri = causal.astype(BF16)
    eye = jnp.where(causal & jnp.logical_not(strict), 1.0, 0.0).astype(F32)
    sel = _lane_selector(AB_APRE_LANE)
    neg_a_exp = -jnp.exp(alog_ref[...])
    dtb = dtb_ref[...]
    ng = ng_ref[...]
    n_double = int(math.log2(c)) - 1
    heads = range(B_HEADS)

    group = max(1, GDN_WY_CHAINS // (bb * B_HEADS))
    group = group if nchunks % group == 0 else 1

    def wy_factors(n, carry):
        pws, rhss, where = [], [], []
        for bi, ci in [(bi, ci) for ci in range(group) for bi in range(bb)]:
            row0 = pl.multiple_of((n * group + ci) * c, c)
            first = (n == 0) if ci == 0 else None
            sl = pl.ds(row0, c)
            sm = sm_ref[bi, sl, :]
            g_cum = _cumsum_rows(tri, neg_a_exp * jax.nn.softplus(sm + dtb))
            g_rows = _select_rows(sel, g_cum)
            beta_all = jax.nn.sigmoid(sm)
            gl_ref[bi, pl.ds(n * group + ci, 1), :] = g_cum[c - 1:c, :]
            for h in heads:
                hs = slice(h * B_DK, (h + 1) * B_DK)
                q = conv_silu(bi, row0, first, hs)
                k = conv_silu(bi, row0, first, slice(B_KW + h * B_DK, B_KW + (h + 1) * B_DK))
                v = conv_silu(bi, row0, first, slice(2 * B_KW + h * B_DV, 2 * B_KW + (h + 1) * B_DV))
                q = q * lax.rsqrt(jnp.sum(q * q, axis=-1, keepdims=True) + NORM_EPS) * (B_DK ** -0.5)
                k = k * lax.rsqrt(jnp.sum(k * k, axis=-1, keepdims=True) + NORM_EPS)
                g_col = g_cum[:, AB_APRE_LANE + h:AB_APRE_LANE + h + 1]
                beta = beta_all[:, AB_BETA_LANE + h:AB_BETA_LANE + h + 1]
                decay = jnp.exp(jnp.where(causal, g_col - g_rows[h:h + 1, :], -jnp.inf))
                e_g = jnp.exp(g_col)
                k_beta = k * beta
                pws.append(-jnp.where(strict, _dot_nt(k_beta, k) * decay, 0.0))
                rhss.append(jnp.concatenate([v * beta, k_beta * e_g], axis=1))
                where.append((bi, sl, hs))
                qk_ref[bi, sl, h * LANES:h * LANES + c] = jnp.where(causal, _dot_nt(q, k) * decay,
                                                                    0.0).astype(BF16)
                qg_ref[bi, sl, hs] = (q * e_g).astype(BF16)
                kg_ref[bi, sl, hs] = (k * jnp.exp(g_col[c - 1:c, :] - g_col)).astype(BF16)
        items = range(len(pws))
        neg_lower = [_split2(p) for p in pws]
        invs = [eye + p for p in pws]
        for _ in range(n_double):
            pws = [_dot(p, p) for p in pws]
            invs = [i + _dot(i, p) for i, p in zip(invs, pws)]
        invs = [i.astype(BF16) for i in invs]
        sol = [_mm(invs[i], rhss[i].astype(BF16)).astype(BF16) for i in items]
        resid = [rhss[i] - sol[i].astype(F32) + _mm(neg_lower[i][0], sol[i]) + _mm(neg_lower[i][1], sol[i])
                 for i in items]
        for i in items:
            bi, sl, hs = where[i]
            uw = sol[i].astype(F32) + _mm(invs[i], resid[i].astype(BF16))
            u_ref[bi, sl, hs] = uw[:, :B_DV]
            wm_ref[bi, sl, hs] = uw[:, B_DV:].astype(BF16)
        return carry

    lax.fori_loop(0, nchunks // group, wy_factors, 0)
    tail_ref[...] = x_ref[:, tb - SUBLANES:tb, :]

    chains = [(bi, h) for bi in range(bb) for h in heads]
    hsl = [slice(h * B_DK, (h + 1) * B_DK) for _, h in chains]

    def recurrence(n, carry):
        sl = pl.ds(pl.multiple_of(n * c, c), c)
        e_last = [jnp.exp(gl_ref[bi, pl.ds(n, 1), :]) for bi in range(bb)]
        s_old = [s_ref[bi, h] for bi, h in chains]
        s_bf = [s.astype(BF16) for s in s_old]
        v_new = [(u_ref[bi, sl, hsl[i]] - _mm(wm_ref[bi, sl, hsl[i]], s_bf[i])).astype(BF16)
                 for i, (bi, h) in enumerate(chains)]
        outs = [_mm(qg_ref[bi, sl, hsl[i]], s_bf[i]) + _mm(qk_ref[bi, sl, h * LANES:h * LANES + c], v_new[i])
                for i, (bi, h) in enumerate(chains)]
        for i, (bi, h) in enumerate(chains):
            s_ref[bi, h] = (e_last[bi][:, AB_APRE_LANE + h:AB_APRE_LANE + h + 1] * s_old[i]
                            + lax.dot_general(kg_ref[bi, sl, hsl[i]], v_new[i], (((0,), (0,)), ((), ())),
                                              preferred_element_type=F32))
        for i, (bi, h) in enumerate(chains):
            o = outs[i]
            o = o * lax.rsqrt(jnp.mean(o * o, axis=-1, keepdims=True) + NORM_EPS) * ng
            o_ref[bi, sl, hsl[i]] = (o * jax.nn.silu(z_ref[bi, sl, hsl[i]])).astype(o_ref.dtype)
        return carry

    lax.fori_loop(0, nchunks, recurrence, 0)

    @pl.when(t == last_t)
    def _():
        sout_ref[...] = s_ref[...]


def _gdn(proj3, conv_prev, conv_w, alog_row, dtb_row, norm_g_row, s0, *, c, tb, bb):
    bsz, l, _ = proj3.shape
    nblk = l // tb
    assert AB_QKV % B_QKV == 0 and AB_ZB % B_VW == 0 and bsz % bb == 0
    in_specs = [
        pl.BlockSpec((bb, tb, B_QKV), lambda b, t: (b, t, AB_QKV // B_QKV)),
        pl.BlockSpec((bb, tb, B_VW), lambda b, t: (b, t, AB_ZB // B_VW)),
        pl.BlockSpec((bb, tb, LANES), lambda b, t: (b, t, AB_SMALL // LANES)),
        pl.BlockSpec((B_CONV, B_QKV), lambda b, t: (0, 0)),
        pl.BlockSpec((bb, B_CONV - 1, B_QKV), lambda b, t: (b, 0, 0)),
        pl.BlockSpec((1, LANES), lambda b, t: (0, 0)),
        pl.BlockSpec((1, LANES), lambda b, t: (0, 0)),
        pl.BlockSpec((1, B_DV), lambda b, t: (0, 0)),
        pl.BlockSpec((bb, B_HEADS, B_DK, B_DV), lambda b, t: (b, 0, 0, 0)),
    ]
    out_specs = [
        pl.BlockSpec((bb, tb, B_VW), lambda b, t: (b, t, 0)),
        pl.BlockSpec((bb, B_HEADS, B_DK, B_DV), lambda b, t: (b, 0, 0, 0)),
    ]
    blocks = bb * (tb * (B_QKV + B_VW + LANES) * 4 + tb * B_VW * 2 + 2 * B_HEADS * B_DK * B_DV * 4)
    scratch = bb * (B_HEADS * B_DK * B_DV * 4 + SUBLANES * B_QKV * 4 + tb * B_VW * 4
                    + 4 * tb * B_KW * 2 + SUBLANES * LANES * 4)
    return pl.pallas_call(
        functools.partial(_gdn_kernel, c=c, nchunks=tb // c),
        out_shape=[jax.ShapeDtypeStruct((bsz, l, B_VW), BF16),
                   jax.ShapeDtypeStruct((bsz, B_HEADS, B_DK, B_DV), F32)],
        grid=(bsz // bb, nblk),
        in_specs=in_specs,
        out_specs=out_specs,
        scratch_shapes=[pltpu.VMEM((bb, B_HEADS, B_DK, B_DV), F32),
                        pltpu.VMEM((bb, SUBLANES, B_QKV), F32),
                        pltpu.VMEM((bb, tb, B_VW), F32),
                        pltpu.VMEM((bb, tb, B_KW), BF16),
                        pltpu.VMEM((bb, tb, B_KW), BF16),
                        pltpu.VMEM((bb, tb, B_KW), BF16),
                        pltpu.VMEM((bb, tb, B_HEADS * LANES), BF16),
                        pltpu.VMEM((bb, SUBLANES, LANES), F32)],
        compiler_params=pltpu.CompilerParams(
            dimension_semantics=("parallel", "arbitrary"),
            vmem_limit_bytes=_vmem_limit(blocks, scratch)),
        name="gdn_mixer",
    )(proj3, proj3, proj3, conv_w, conv_prev, alog_row, dtb_row, norm_g_row, s0)


def _mlstm_kernel(q_ref, k_ref, v_ref, og_ref, z_ref, sm_ref, ib_ref, fb_ref, ng_ref,
                  c0_ref, n0_ref, m0_ref, o_ref, cout_ref, nout_ref, mout_ref,
                  c_ref, n_ref, m_ref, hi_ref, mi_ref, ni_ref, bc_ref, kv_ref, ks_ref, mc_ref, bl_ref,
                  *, c, nchunks):
    t = pl.program_id(1)
    last_t = pl.num_programs(1) - 1
    bb = q_ref.shape[0]
    chains = [(bi, h) for bi in range(bb) for h in range(D_HEADS)]
    ksl = [slice(h * D_DK, (h + 1) * D_DK) for _, h in chains]
    vsl = [slice(h * D_DV, (h + 1) * D_DV) for _, h in chains]
    idx = range(len(chains))

    @pl.when(t == 0)
    def _():
        c_ref[...] = c0_ref[...]
        n_ref[...] = n0_ref[...]
        for bi, h in chains:
            m_ref[bi, h] = jnp.broadcast_to(m0_ref[bi, :, h:h + 1], (1, LANES))

    causal, _ = _causal_masks(c)
    tri = causal.astype(BF16)
    sel = _lane_selector(0)
    lane = lax.broadcasted_iota(jnp.int32, (c, LANES), 1)
    ib = ib_ref[...]
    fb = fb_ref[...]

    group = MLSTM_INTRA_CHUNKS if nchunks % MLSTM_INTRA_CHUNKS == 0 else 1

    def intra(n, carry):
        items = [(bi, ci, h) for ci in range(group) for bi, h in chains]
        sls = [pl.ds(pl.multiple_of((n * group + ci) * c, c), c) for ci in range(group)]
        sm = {(bi, ci): sm_ref[bi, sls[ci], :] for ci in range(group) for bi in range(bb)}
        i_full = {key: x + ib for key, x in sm.items()}
        b_full = {key: _cumsum_rows(tri, jax.nn.log_sigmoid(x + fb)) for key, x in sm.items()}
        rows = {key: _select_rows(sel, jnp.where(lane < CD_F_LANE, i_full[key], b_full[key]))
                for key in sm}
        b_col = [b_full[bi, ci][:, CD_F_LANE + h:CD_F_LANE + h + 1] for bi, ci, h in items]
        i_col = [i_full[bi, ci][:, CD_I_LANE + h:CD_I_LANE + h + 1] for bi, ci, h in items]
        logw = [jnp.where(causal, b_col[i] - rows[bi, ci][CD_F_LANE + h:CD_F_LANE + h + 1, :]
                          + rows[bi, ci][CD_I_LANE + h:CD_I_LANE + h + 1, :], -jnp.inf)
                for i, (bi, ci, h) in enumerate(items)]
        ids = range(len(items))
        m_intra = [jnp.max(logw[i], axis=-1, keepdims=True) for i in ids]
        ks_ = [slice(h * D_DK, (h + 1) * D_DK) for _, _, h in items]
        vs_ = [slice(h * D_DV, (h + 1) * D_DV) for _, _, h in items]
        q_bf = [(q_ref[bi, sls[ci], ks_[i]] * (D_DK ** -0.5)).astype(BF16) for i, (bi, ci, h) in enumerate(items)]
        k = [k_ref[bi, sls[ci], ks_[i]] for i, (bi, ci, h) in enumerate(items)]
        v = [v_ref[bi, sls[ci], vs_[i]].astype(BF16) for i, (bi, ci, h) in enumerate(items)]
        p = [jnp.exp(logw[i] - m_intra[i]) * _dot_nt(q_bf[i], k[i]) for i in ids]
        n_intra = [jnp.sum(p[i], axis=-1, keepdims=True) for i in ids]
        m_chunk = [m_intra[i][c - 1:c, :] for i in ids]
        k_w = [k[i] * jnp.exp(b_col[i][c - 1:c, :] - b_col[i] + i_col[i] - m_chunk[i]) for i in ids]
        rep = lambda x: jnp.broadcast_to(x, (x.shape[0], LANES))
        for i, (bi, ci, h) in enumerate(items):
            row = pl.ds(n * group + ci, 1)
            hi_ref[bi, sls[ci], vs_[i]] = _dot(p[i], v[i])
            kv_ref[bi, n * group + ci, h] = _dot_tn(k_w[i], v[i])
            ks_ref[bi, h, row, :] = jnp.sum(k_w[i], axis=0, keepdims=True)
            mi_ref[bi, h, sls[ci], :] = rep(m_intra[i])
            ni_ref[bi, h, sls[ci], :] = rep(n_intra[i])
            bc_ref[bi, h, sls[ci], :] = rep(b_col[i])
            mc_ref[bi, h, row, :] = rep(m_chunk[i])
            bl_ref[bi, h, row, :] = rep(b_col[i][c - 1:c, :])
        return carry

    lax.fori_loop(0, nchunks // group, intra, 0)

    twice = lambda x: jnp.concatenate([x, x], axis=1)

    def body(n, carry):
        sl = pl.ds(pl.multiple_of(n * c, c), c)
        row = pl.ds(n, 1)
        q = [q_ref[chains[i][0], sl, ksl[i]] * (D_DK ** -0.5) for i in idx]
        c_s = [c_ref[bi, h] for bi, h in chains]
        n_s = [n_ref[bi, h:h + 1, :] for bi, h in chains]
        m_s = [m_ref[bi, h] for bi, h in chains]
        qc = [_dot(q[i], c_s[i]) for i in idx]
        b_last = [bl_ref[bi, h, row, :] for bi, h in chains]
        m_chunk = [mc_ref[bi, h, row, :] for bi, h in chains]
        for i, (bi, h) in enumerate(chains):
            m_new = jnp.maximum(b_last[i] + m_s[i], m_chunk[i])
            w_old = jnp.exp(b_last[i] + m_s[i] - m_new)
            w_new = jnp.exp(m_chunk[i] - m_new)
            c_ref[bi, h] = twice(w_old) * c_s[i] + twice(w_new) * kv_ref[bi, n, h]
            n_ref[bi, h:h + 1, :] = w_old * n_s[i] + w_new * ks_ref[bi, h, row, :]
            m_ref[bi, h] = m_new
        for i, (bi, h) in enumerate(chains):
            m_intra = mi_ref[bi, h, sl, :]
            a = bc_ref[bi, h, sl, :] + m_s[i]
            m_t = jnp.maximum(a, m_intra)
            w_a = jnp.exp(a - m_t)
            w_i = jnp.exp(m_intra - m_t)
            num = twice(w_a) * qc[i] + twice(w_i) * hi_ref[bi, sl, vsl[i]]
            den = w_a * jnp.sum(q[i] * n_s[i], axis=-1, keepdims=True) + w_i * ni_ref[bi, h, sl, :]
            hh = num / twice(jnp.maximum(jnp.abs(den), jnp.exp(-m_t)))
            hd = jax.nn.sigmoid(og_ref[bi, sl, vsl[i]]) * hh
            oc = hd - jnp.mean(hd, axis=-1, keepdims=True)
            o = oc * lax.rsqrt(jnp.mean(oc * oc, axis=-1, keepdims=True) + NORM_EPS) * ng_ref[:, vsl[i]]
            o_ref[bi, sl, vsl[i]] = (o * jax.nn.silu(z_ref[bi, sl, vsl[i]])).astype(o_ref.dtype)
        return carry

    lax.fori_loop(0, nchunks, body, 0)

    @pl.when(t == last_t)
    def _():
        cout_ref[...] = c_ref[...]
        nout_ref[...] = n_ref[...]
        for bi, h in chains:
            mout_ref[bi, :, h:h + 1] = m_ref[bi, h][:, 0:1]


def _mlstm(proj3, ib_row, fb_row, norm_g_row, c0, n0, m0, *, c, tb, bb):
    bsz, l, _ = proj3.shape
    nblk = l // tb
    assert bsz % bb == 0
    tok = lambda col: (lambda b, t: (b, t, col))
    st4 = lambda b, t: (b, 0, 0, 0)
    st3 = lambda b, t: (b, 0, 0)
    in_specs = [
        pl.BlockSpec((bb, tb, D_KW), tok(CD_Q // D_KW)),
        pl.BlockSpec((bb, tb, D_KW), tok(CD_K // D_KW)),
        pl.BlockSpec((bb, tb, D_VW), tok(CD_V // D_VW)),
        pl.BlockSpec((bb, tb, D_VW), tok(CD_O // D_VW)),
        pl.BlockSpec((bb, tb, D_VW), tok(CD_ZD // D_VW)),
        pl.BlockSpec((bb, tb, LANES), tok(CD_SMALL // LANES)),
        pl.BlockSpec((1, LANES), lambda b, t: (0, 0)),
        pl.BlockSpec((1, LANES), lambda b, t: (0, 0)),
        pl.BlockSpec((1, D_VW), lambda b, t: (0, 0)),
        pl.BlockSpec((bb, D_HEADS, D_DK, D_DV), st4),
        pl.BlockSpec((bb, D_HEADS, D_DK), st3),
        pl.BlockSpec((bb, 1, D_HEADS), st3),
    ]
    out_specs = [
        pl.BlockSpec((bb, tb, D_VW), lambda b, t: (b, t, 0)),
        pl.BlockSpec((bb, D_HEADS, D_DK, D_DV), st4),
        pl.BlockSpec((bb, D_HEADS, D_DK), st3),
        pl.BlockSpec((bb, 1, D_HEADS), st3),
    ]
    state = bb * D_HEADS * D_DK * D_DV * 4
    blocks = bb * tb * (2 * D_KW + 3 * D_VW + LANES) * 4 + bb * tb * D_VW * 2 + 2 * state
    nchunks = tb // c
    return pl.pallas_call(
        functools.partial(_mlstm_kernel, c=c, nchunks=nchunks),
        out_shape=[jax.ShapeDtypeStruct((bsz, l, D_VW), BF16),
                   jax.ShapeDtypeStruct((bsz, D_HEADS, D_DK, D_DV), F32),
                   jax.ShapeDtypeStruct((bsz, D_HEADS, D_DK), F32),
                   jax.ShapeDtypeStruct((bsz, 1, D_HEADS), F32)],
        grid=(bsz // bb, nblk),
        in_specs=in_specs,
        out_specs=out_specs,
        scratch_shapes=[pltpu.VMEM((bb, D_HEADS, D_DK, D_DV), F32),
                        pltpu.VMEM((bb, D_HEADS, D_DK), F32),
                        pltpu.VMEM((bb, D_HEADS, 1, LANES), F32),
                        pltpu.VMEM((bb, tb, D_VW), F32),
                        pltpu.VMEM((bb, D_HEADS, tb, LANES), F32),
                        pltpu.VMEM((bb, D_HEADS, tb, LANES), F32),
                        pltpu.VMEM((bb, D_HEADS, tb, LANES), F32),
                        pltpu.VMEM((bb, nchunks, D_HEADS, D_DK, D_DV), F32),
                        pltpu.VMEM((bb, D_HEADS, _round_up(nchunks, SUBLANES), D_DK), F32),
                        pltpu.VMEM((bb, D_HEADS, _round_up(nchunks, SUBLANES), LANES), F32),
                        pltpu.VMEM((bb, D_HEADS, _round_up(nchunks, SUBLANES), LANES), F32)],
        compiler_params=pltpu.CompilerParams(
            dimension_semantics=("parallel", "arbitrary"),
            vmem_limit_bytes=_vmem_limit(blocks, state * (1 + nchunks) + bb * tb * (D_VW + 2 * LANES) * 4)),
        name="mlstm_mixer",
    )(proj3, proj3, proj3, proj3, proj3, proj3, ib_row, fb_row, norm_g_row, c0, n0, m0)


def _s5_expand_operators(kc_ref, bc_ref, cc_ref, bd_ref, bst_ref, cst_ref):
    tc, gt, sw, cg = S5_CHUNK, S5_GROUP_BLOCK, 2 * C_STATE, C_GROUP
    w_t = tc * LANES
    iota = lambda shape, d: lax.broadcasted_iota(jnp.int32, shape, d)
    row_g = (iota((w_t, LANES), 0) // cg) % gt
    tile16 = jnp.where(iota((cg, LANES), 1) % cg == iota((cg, LANES), 0), 1.0, 0.0).astype(BF16)
    bd = jnp.where(row_g == iota((w_t, LANES), 1) // cg, _mm(kc_ref[...], tile16), 0.0).astype(BF16)
    bd_ref[:, LANES:2 * LANES] = bd
    bd_ref[0:w_t - LANES, 0:LANES] = bd[LANES:, :]
    bd_ref[w_t - LANES:w_t, 0:LANES] = jnp.zeros((LANES, LANES), BF16)
    bc = bc_ref[...].astype(F32)
    for g in range(gt):
        bst_ref[:, g * sw:(g + 1) * sw] = jnp.where(row_g == g, bc, 0.0).astype(BF16)
    src, dst = iota((tc * cg, w_t), 0), iota((tc * cg, w_t), 1)
    spread = jnp.where((src // cg == dst // LANES) & (src % cg == dst % cg), 1.0, 0.0).astype(BF16)
    lane_g = (iota((sw, w_t), 1) // cg) % gt
    for g in range(gt):
        full = _mm(cc_ref[g * sw:(g + 1) * sw, :], spread)
        cst_ref[g * sw:(g + 1) * sw, :] = jnp.where(lane_g == g, full, 0.0).astype(BF16)


def _s5_kernel(u_ref, kc_ref, bc_ref, cc_ref, apow_ref, x0_ref, y_ref, xf_ref, bd_ref, bst_ref, cst_ref,
               *, nc, bb):
    m = nc * bb
    tc = S5_CHUNK
    sw = 2 * C_STATE

    @pl.when(pl.program_id(1) == 0)
    def _():
        _s5_expand_operators(kc_ref, bc_ref, cc_ref, bd_ref, bst_ref, cst_ref)

    row = lax.broadcasted_iota(jnp.int32, (m, sw), 0)
    n_idx = row & (nc - 1)
    n_log = int(math.log2(nc))

    def cmul(a1, a2, x):
        return a1 * x + a2 * pltpu.roll(x, C_STATE, 1)

    lhs = jnp.concatenate([u_ref[pl.ds(tau, m, stride=tc), :].astype(BF16) for tau in range(tc)], axis=1)
    e_all = _mm(lhs, bst_ref[...])
    groups = range(S5_GROUP_BLOCK)
    gsl = [slice(g * sw, (g + 1) * sw) for g in groups]
    x0_rows = []
    for g in groups:
        rows0 = jnp.zeros((m, sw), F32)
        for b in range(bb):
            rows0 = jnp.where(row == b * nc, x0_ref[b, :, gsl[g]], rows0)
        x0_rows.append(rows0)
    x = [e_all[:, gsl[g]] + cmul(apow_ref[0:1, gsl[g]], apow_ref[1:2, gsl[g]], x0_rows[g]) for g in groups]
    for j in range(n_log):
        sh = 1 << j
        shifted = [jnp.where(n_idx >= sh, pltpu.roll(x[g], sh, 0), 0.0) for g in groups]
        x = [x[g] + cmul(apow_ref[2 * j:2 * j + 1, gsl[g]], apow_ref[2 * j + 1:2 * j + 2, gsl[g]], shifted[g])
             for g in groups]
    x_start = [jnp.where(n_idx >= 1, pltpu.roll(x[g], 1, 0), x0_rows[g]).astype(BF16) for g in groups]
    for g in groups:
        for b in range(bb):
            xf_ref[b, :, gsl[g]] = x[g][b * nc + nc - 1:b * nc + nc, :]
    y_state = _mm(jnp.concatenate(x_start, axis=1), cst_ref[...])
    for tau in range(0, tc, 2):
        width = (tau + 2) * LANES
        start = (tc - 2 - tau) * LANES
        y = _mm(lhs[:, :width], bd_ref[start:start + width, :]) + y_state[:, tau * LANES:(tau + 2) * LANES]
        y_ref[pl.ds(tau, m, stride=tc), :] = y[:, :LANES]
        y_ref[pl.ds(tau + 1, m, stride=tc), :] = y[:, LANES:]


def _s5_chunks(proj2d, kc, bc, cc, apow, x0, *, l, bb):
    tc = S5_CHUNK
    nc = l // tc
    bsz = x0.shape[1]
    nt = C_GROUPS // S5_GROUP_BLOCK
    rows = bb * l
    sw = 2 * C_STATE
    sw_t = S5_GROUP_BLOCK * sw
    w_t = tc * LANES
    assert CD_U % LANES == 0 and bsz % bb == 0
    blocks = (2 * rows * LANES * 4 + 2 * w_t * LANES * 2 + sw_t * tc * C_GROUP * 2 + apow.shape[1] * sw_t * 4
              + 2 * bb * SUBLANES * sw_t * 4)
    scratch = w_t * 2 * LANES * 2 + 2 * w_t * sw_t * 2
    temps = bb * nc * (w_t * 2 + w_t * 4 + 3 * sw_t * 4) + 4 * sw * w_t * 4
    return pl.pallas_call(
        functools.partial(_s5_kernel, nc=nc, bb=bb),
        out_shape=[jax.ShapeDtypeStruct((bsz * l, C_W), F32),
                   jax.ShapeDtypeStruct((nt, bsz, 1, sw_t), F32)],
        grid=(nt, bsz // bb),
        in_specs=[pl.BlockSpec((rows, LANES), lambda i, j: (j, CD_U // LANES + i)),
                  pl.BlockSpec((None, w_t, C_GROUP), lambda i, j: (i, 0, 0)),
                  pl.BlockSpec((None, w_t, sw), lambda i, j: (i, 0, 0)),
                  pl.BlockSpec((None, sw_t, tc * C_GROUP), lambda i, j: (i, 0, 0)),
                  pl.BlockSpec((None, apow.shape[1], sw_t), lambda i, j: (i, 0, 0)),
                  pl.BlockSpec((None, bb, 1, sw_t), lambda i, j: (i, j, 0, 0))],
        out_specs=[pl.BlockSpec((rows, LANES), lambda i, j: (j, i)),
                   pl.BlockSpec((None, bb, 1, sw_t), lambda i, j: (i, j, 0, 0))],
        scratch_shapes=[pltpu.VMEM((w_t, 2 * LANES), BF16),
                        pltpu.VMEM((w_t, sw_t), BF16),
                        pltpu.VMEM((sw_t, w_t), BF16)],
        compiler_params=pltpu.CompilerParams(
            dimension_semantics=("parallel", "arbitrary"),
            vmem_limit_bytes=_vmem_limit(blocks, scratch + temps)),
        name="s5_chunks",
    )(proj2d, kc, bc, cc, apow, x0)


def _s5_operators(lam_re, lam_im, log_dt, b_re, b_im, c_re, c_im, n_log):
    g, p = lam_re.shape
    tc = S5_CHUNK
    dt = jnp.exp(log_dt.astype(F32))[:, None]
    mag = jnp.exp(lam_re * dt)
    ab_re, ab_im = mag * jnp.cos(lam_im * dt), mag * jnp.sin(lam_im * dt)
    den = lam_re * lam_re + lam_im * lam_im
    er = ab_re - 1.0
    zr = (er * lam_re + ab_im * lam_im) / den
    zi = (ab_im * lam_re - er * lam_im) / den
    bb_re = zr[..., None] * b_re - zi[..., None] * b_im
    bb_im = zr[..., None] * b_im + zi[..., None] * b_re
    pw_re, pw_im = ab_re[None], ab_im[None]
    while pw_re.shape[0] < tc:
        top_re, top_im = pw_re[-1], pw_im[-1]
        pw_re, pw_im = (jnp.concatenate([pw_re, top_re * pw_re - top_im * pw_im]),
                        jnp.concatenate([pw_im, top_re * pw_im + top_im * pw_re]))
    pw_re = jnp.concatenate([jnp.ones_like(ab_re)[None], pw_re])
    pw_im = jnp.concatenate([jnp.zeros_like(ab_im)[None], pw_im])
    abr = pw_re[:tc, :, :, None] * bb_re - pw_im[:tc, :, :, None] * bb_im
    abi = pw_re[:tc, :, :, None] * bb_im + pw_im[:tc, :, :, None] * bb_re
    kern = (jnp.einsum('gjp,dgpi->dgji', c_re, abr, precision=HIGHEST)
            - jnp.einsum('gjp,dgpi->dgji', c_im, abi, precision=HIGHEST))
    gt = S5_GROUP_BLOCK
    nt = g // gt
    sw = 2 * p
    kc = kern[::-1].reshape(tc, nt, gt, C_GROUP, C_GROUP).transpose(1, 0, 2, 4, 3)
    kc = kc.reshape(nt, tc * gt * C_GROUP, C_GROUP)
    ab = jnp.concatenate([abr, abi], axis=2)[::-1]
    bc = ab.reshape(tc, nt, gt, sw, C_GROUP).transpose(1, 0, 2, 4, 3).reshape(nt, tc * gt * C_GROUP, sw)
    cr = c_re[None] * pw_re[1:, :, None, :] - c_im[None] * pw_im[1:, :, None, :]
    ci = -(c_re[None] * pw_im[1:, :, None, :] + c_im[None] * pw_re[1:, :, None, :])
    cc = jnp.concatenate([cr, ci], axis=3)
    cc = cc.reshape(tc, nt, gt, C_GROUP, sw).transpose(1, 2, 4, 0, 3).reshape(nt, gt * sw, tc * C_GROUP)
    r, i = pw_re[tc], pw_im[tc]
    rows = []
    for _ in range(max(n_log, 1)):
        rows += [jnp.concatenate([r, r], -1), jnp.concatenate([-i, i], -1)]
        r, i = r * r - i * i, 2.0 * r * i
    apow = jnp.stack(rows, axis=1)
    apow = apow.reshape(nt, gt, -1, sw).transpose(0, 2, 1, 3).reshape(nt, -1, gt * sw)
    return kc.astype(BF16), bc.astype(BF16), cc.astype(BF16), apow


def _s5(proj2d, ops, x0_re, x0_im, *, l):
    kc, bc, cc, apow = ops
    bsz = x0_re.shape[0]
    nt = C_GROUPS // S5_GROUP_BLOCK
    x0 = jnp.concatenate([x0_re, x0_im], axis=-1).reshape(bsz, nt, 1, -1).transpose(1, 0, 2, 3)
    bb = bsz if bsz * l <= 4096 else 1
    y, xf = _s5_chunks(proj2d, kc, bc, cc, apow, x0, l=l, bb=bb)
    xf = xf.transpose(1, 0, 2, 3).reshape(bsz, C_GROUPS, 2 * C_STATE)
    return y, xf[..., :C_STATE], xf[..., C_STATE:]


AB_SRC_GA = AB_QKV
AB_SRC_QKV = AB_SRC_GA + A_GATE_RANK
AB_SRC_TAIL = AB_SRC_QKV + B_QKV + B_VW
IN_AB = AB_SRC_TAIL + 2 * B_HEADS
IN_CD = CD_SMALL + 2 * D_HEADS
WPREP_AB_TILE = 512
WPREP_CD_TILE = 640


def _prep_w_ab_kernel(wt_ref, ga_ref, o_ref):
    j = pl.program_id(0)
    tile = o_ref.shape[1]
    n_small = A_GATE_RANK + 2 * B_HEADS

    @pl.when(j < AB_SMALL // tile)
    def _():
        o_ref[...] = wt_ref[...].T.astype(BF16)

    @pl.when(j == AB_SMALL // tile)
    def _():
        rows = jnp.concatenate([ga_ref[...], wt_ref[tile - 2 * B_HEADS:tile, :],
                                jnp.zeros((tile - n_small, wt_ref.shape[1]), F32)], axis=0)
        o_ref[...] = rows.T.astype(BF16)

    @pl.when(j > AB_SMALL // tile)
    def _():
        o_ref[...] = jnp.zeros(o_ref.shape, BF16)


def _prep_w_ab(w_t, n_out):
    n_in, d = w_t.shape
    tile = WPREP_AB_TILE
    assert AB_QKV % tile == 0 and AB_SMALL % tile == 0 and n_out % tile == 0 and n_in >= tile

    unit = A_GATE_RANK
    assert tile % unit == 0 and (AB_SRC_QKV - AB_QKV) % unit == 0 and (n_in - tile) % unit == 0

    def src_row(j):
        k = j * (tile // unit)
        k = jnp.where(j < AB_QKV // tile, k,
                      jnp.where(j < AB_SMALL // tile, k + (AB_SRC_QKV - AB_QKV) // unit, (n_in - tile) // unit))
        return unit * k

    return pl.pallas_call(
        _prep_w_ab_kernel,
        out_shape=jax.ShapeDtypeStruct((d, n_out), BF16),
        grid=(n_out // tile,),
        in_specs=[pl.BlockSpec((pl.Element(tile), pl.Element(d)), lambda j: (src_row(j), 0)),
                  pl.BlockSpec((pl.Element(A_GATE_RANK), pl.Element(d)), lambda j: (AB_SRC_GA, 0))],
        out_specs=pl.BlockSpec((d, tile), lambda j: (0, j)),
        compiler_params=pltpu.CompilerParams(
            dimension_semantics=("parallel",),
            vmem_limit_bytes=_vmem_limit(tile * d * 6, 2 * tile * d * 4)),
        name="prep_w_in_ab",
    )(w_t, w_t)


def _prep_w_cd_kernel(wt_ref, o_ref):
    tile = o_ref.shape[1]
    row = pl.program_id(0) * tile + lax.broadcasted_iota(jnp.int32, wt_ref.shape, 0)
    o_ref[...] = jnp.where(row < IN_CD, wt_ref[...], 0.0).T.astype(BF16)


def _prep_w_cd(w_t, n_out):
    n_in, d = w_t.shape
    tile = WPREP_CD_TILE
    assert n_out % tile == 0
    return pl.pallas_call(
        _prep_w_cd_kernel,
        out_shape=jax.ShapeDtypeStruct((d, n_out), BF16),
        grid=(n_out // tile,),
        in_specs=[pl.BlockSpec((tile, d), lambda j: (j, 0))],
        out_specs=pl.BlockSpec((d, tile), lambda j: (0, j)),
        compiler_params=pltpu.CompilerParams(
            dimension_semantics=("parallel",),
            vmem_limit_bytes=_vmem_limit(tile * d * 6, 2 * tile * d * 4)),
        name="prep_w_in_cd",
    )(w_t)


def _lane_row(vals, lane0):
    return jnp.zeros((1, LANES), F32).at[0, lane0:lane0 + vals.shape[0]].set(vals.astype(F32))


def _prepare_weights(norm_g, final_norm_g, w_in_ab, a_gate_w, a_gate_b, a_norm_g, b_conv_w, b_a_log,
                     b_dt_bias, b_norm_g, w_out_ab, w_in_cd, c_lam_re, c_lam_im, c_log_dt, c_b_re,
                     c_b_im, c_c_re, c_c_im, c_d, c_glu_w, c_glu_b, d_i_bias, d_f_bias, d_norm_g,
                     w_out_cd, n_log):
    assert w_in_ab.shape[1] == IN_AB and w_in_cd.shape[1] == IN_CD
    w_ab = _prep_w_ab(w_in_ab.astype(F32).T, _round_up(AB_SMALL + LANES, PROJ_TN))
    w_cd = _prep_w_cd(w_in_cd.astype(F32).T, _round_up(CD_SMALL + LANES, PROJ_TN))
    gate_w = jnp.zeros((LANES, A_KW), F32).at[AB_GA_LANE:AB_GA_LANE + A_GATE_RANK].set(
        a_gate_w.astype(F32)).astype(BF16)
    return dict(
        norm_g=norm_g.astype(F32), final_g=final_norm_g.astype(F32)[None, :],
        w_ab=w_ab, w_cd=w_cd, gate_w=gate_w, gate_b=a_gate_b.astype(F32)[None, :],
        a_norm_g=a_norm_g.astype(F32)[None, :], conv_w=b_conv_w.astype(F32),
        alog=_lane_row(b_a_log, AB_APRE_LANE), dtb=_lane_row(b_dt_bias, AB_APRE_LANE),
        b_norm_g=b_norm_g.astype(F32)[None, :],
        w_out_a=w_out_ab[:A_VW].astype(BF16), w_out_b=w_out_ab[A_VW:].astype(BF16),
        s5_ops=_s5_operators(c_lam_re.astype(F32), c_lam_im.astype(F32), c_log_dt, c_b_re.astype(F32),
                             c_b_im.astype(F32), c_c_re.astype(F32), c_c_im.astype(F32), n_log),
        c_d=c_d.astype(F32).reshape(1, C_W), glu_w=c_glu_w.astype(BF16),
        glu_b=c_glu_b.astype(F32)[None, :],
        ib=_lane_row(d_i_bias, CD_I_LANE), fb=_lane_row(d_f_bias, CD_F_LANE),
        d_norm_g=d_norm_g.astype(F32)[None, :],
        w_out_c=w_out_cd[:C_W].astype(BF16), w_out_d=w_out_cd[C_W:].astype(BF16),
    )


def _trunk(x, conv_prev, s_gla0, s_gdn0, s5_re0, s5_im0, mc0, mn0, mm0, w):
    bsz, l, d = x.shape
    c = min(CHUNK, l)
    tb = min(l, 8 * c)
    assert l % tb == 0 and l % S5_CHUNK == 0
    x2d = x.reshape(bsz * l, d)

    proj = _norm_matmul(x2d, w['norm_g'][0:1], w['w_ab'])
    proj3 = proj.reshape(bsz, l, proj.shape[1])
    bb = max(s for s in (1, 2, 4) if s <= MIXER_STREAMS and bsz % s == 0)
    tb_s = min(l, (8 // bb) * c)
    o_a, s_gla = _gla(proj3, w['gate_w'], w['gate_b'], w['a_norm_g'], s_gla0.astype(F32), c=c, tb=tb_s, bb=bb)
    o_b, s_gdn = _gdn(proj3, conv_prev.astype(F32), w['conv_w'], w['alog'], w['dtb'], w['b_norm_g'],
                      s_gdn0.astype(F32), c=c, tb=tb_s, bb=bb)
    conv_new = proj3[:, l - (B_CONV - 1):, AB_QKV:AB_QKV + B_QKV]
    h1 = _out_proj(o_a.reshape(bsz * l, A_VW), o_b.reshape(bsz * l, B_VW), w['w_out_a'], w['w_out_b'], x2d)

    proj = _norm_matmul(h1, w['norm_g'][1:2], w['w_cd'])
    proj3 = proj.reshape(bsz, l, proj.shape[1])
    y, s5_re, s5_im = _s5(proj, w['s5_ops'], s5_re0.astype(F32), s5_im0.astype(F32), l=l)
    o_d, mc, mn, mm = _mlstm(proj3, w['ib'], w['fb'], w['d_norm_g'], mc0.astype(F32),
                             mn0.astype(F32), mm0.astype(F32)[:, None, :], c=c, tb=tb_s, bb=bb)
    y_out = _glu_out_proj_norm(y, proj, o_d.reshape(bsz * l, D_VW), w['c_d'], w['glu_w'], w['glu_b'],
                               w['w_out_c'], w['w_out_d'], h1, w['final_g'])
    dt = x.dtype
    return (y_out.reshape(bsz, l, d).astype(dt), conv_new.astype(dt), s_gla.astype(dt), s_gdn.astype(dt),
            s5_re.astype(dt), s5_im.astype(dt), mc.astype(dt), mn.astype(dt), mm[:, 0, :].astype(dt))


def kernel(x_prompt, x_sample, cache_gdn_conv, state_gla, state_gdn, state_s5_re, state_s5_im,
           state_mlstm_c, state_mlstm_n, state_mlstm_m, norm_g, final_norm_g, w_in_ab, a_gate_w,
           a_gate_b, a_norm_g, b_conv_w, b_a_log, b_dt_bias, b_norm_g, w_out_ab, w_in_cd, c_lam_re,
           c_lam_im, c_log_dt, c_b_re, c_b_im, c_c_re, c_c_im, c_d, c_glu_w, c_glu_b, d_i_bias,
           d_f_bias, d_norm_g, w_out_cd):
    n_log = int(math.log2(max(x_prompt.shape[1], x_sample.shape[1]) // S5_CHUNK))
    w = _prepare_weights(norm_g, final_norm_g, w_in_ab, a_gate_w, a_gate_b, a_norm_g, b_conv_w, b_a_log,
                         b_dt_bias, b_norm_g, w_out_ab, w_in_cd, c_lam_re, c_lam_im, c_log_dt, c_b_re,
                         c_b_im, c_c_re, c_c_im, c_d, c_glu_w, c_glu_b, d_i_bias, d_f_bias, d_norm_g,
                         w_out_cd, n_log)
    nb = x_prompt.shape[0]
    zeros = lambda *shape: jnp.zeros(shape, F32)
    p_out = _trunk(x_prompt, zeros(nb, B_CONV - 1, B_QKV), zeros(nb, A_HEADS, A_DK, A_DV),
                   zeros(nb, B_HEADS, B_DK, B_DV), zeros(nb, C_GROUPS, C_STATE), zeros(nb, C_GROUPS, C_STATE),
                   zeros(nb, D_HEADS, D_DK, D_DV), zeros(nb, D_HEADS, D_DK), zeros(nb, D_HEADS), w)
    s_out = _trunk(x_sample, cache_gdn_conv, state_gla, state_gdn, state_s5_re, state_s5_im,
                   state_mlstm_c, state_mlstm_n, state_mlstm_m, w)
    return (p_out[0], s_out[0]) + tuple(p_out[1:]) + tuple(s_out[1:])
```

```python
import functools
import math

import jax
import jax.numpy as jnp
from jax import lax
from jax.experimental import pallas as pl
from jax.experimental.pallas import tpu as pltpu

F32 = jnp.float32
BF16 = jnp.bfloat16
HIGHEST = lax.Precision.HIGHEST

NORM_EPS = 1e-6
CHUNK = 64
A_HEADS, A_DK, A_DV, A_GATE_RANK, A_GATE_TAU = 4, 128, 256, 16, 16.0
B_HEADS, B_DK, B_DV, B_CONV = 8, 128, 128, 4
C_GROUP, C_GROUPS, C_STATE = 16, 64, 64
D_HEADS, D_DK, D_DV = 4, 128, 256
A_KW, A_VW = A_HEADS * A_DK, A_HEADS * A_DV
B_KW, B_VW = B_HEADS * B_DK, B_HEADS * B_DV
B_QKV = 2 * B_KW + B_VW
C_W = C_GROUPS * C_GROUP
D_KW, D_VW = D_HEADS * D_DK, D_HEADS * D_DV

LANES = 128
SUBLANES = 8
VMEM_BYTES_V7X = 64 * 1024 * 1024

AB_Q, AB_K, AB_V, AB_Z = 0, A_KW, 2 * A_KW, 2 * A_KW + A_VW
AB_QKV = AB_Z + A_VW
AB_ZB = AB_QKV + B_QKV
AB_SMALL = AB_ZB + B_VW
AB_GA_LANE, AB_BETA_LANE, AB_APRE_LANE = 0, A_GATE_RANK, A_GATE_RANK + B_HEADS
CD_U, CD_Z = 0, C_W
CD_Q = 2 * C_W
CD_K = CD_Q + D_KW
CD_V = CD_K + D_KW
CD_O = CD_V + D_VW
CD_ZD = CD_O + D_VW
CD_SMALL = CD_ZD + D_VW
CD_I_LANE, CD_F_LANE = 0, D_HEADS

PROJ_TN = 1280
S5_CHUNK = 16
S5_GROUP_BLOCK = 8
S5_MAX_ROWS = 8192
NORM_ROW_SPLITS = 2
GLU_ROW_SPLITS = 2
MIXER_STREAMS = 4
MLSTM_INTRA_CHUNKS = 4
GDN_WY_CHAINS = 32


def _round_up(x, m):
    return (x + m - 1) // m * m


def _vmem_limit(block_bytes, scratch_bytes=0):
    est = 2 * block_bytes + scratch_bytes
    return int(min(max(2 * est, 32 * 1024 * 1024), VMEM_BYTES_V7X - 8 * 1024 * 1024))


def _mm(a, b):
    return jnp.dot(a, b, preferred_element_type=F32)


def _dot(a, b):
    return _mm(a.astype(BF16), b.astype(BF16))


def _dot_nt(a, b):
    return lax.dot_general(a.astype(BF16), b.astype(BF16), (((1,), (1,)), ((), ())),
                           preferred_element_type=F32)


def _dot_tn(a, b):
    return lax.dot_general(a.astype(BF16), b.astype(BF16), (((0,), (0,)), ((), ())),
                           preferred_element_type=F32)


def _split2(x):
    hi = x.astype(BF16)
    return hi, (x - hi.astype(F32)).astype(BF16)


def _split3(x):
    hi = x.astype(BF16)
    r = x - hi.astype(F32)
    mid = r.astype(BF16)
    return hi, mid, (r - mid.astype(F32)).astype(BF16)


def _cumsum_rows(tri_bf16, x):
    hi, mid, lo = _split3(x)
    return _mm(tri_bf16, hi) + _mm(tri_bf16, mid) + _mm(tri_bf16, lo)


def _select_rows(sel_bf16, x):
    nt = lambda b: lax.dot_general(sel_bf16, b, (((1,), (1,)), ((), ())), preferred_element_type=F32)
    hi, mid, lo = _split3(x)
    return nt(hi) + nt(mid) + nt(lo)


def _lane_selector(lane0):
    r = lax.broadcasted_iota(jnp.int32, (SUBLANES, LANES), 0)
    l = lax.broadcasted_iota(jnp.int32, (SUBLANES, LANES), 1)
    return jnp.where(l == r + lane0, 1.0, 0.0).astype(BF16)


def _causal_masks(c):
    row = lax.broadcasted_iota(jnp.int32, (c, c), 0)
    col = lax.broadcasted_iota(jnp.int32, (c, c), 1)
    return row >= col, row > col


def _norm_matmul_kernel(x_ref, g_ref, w_ref, o_ref, xn_ref):
    j = pl.program_id(1)

    @pl.when(j == 0)
    def _():
        tm = x_ref.shape[0]
        sub = tm // NORM_ROW_SPLITS
        for i in range(NORM_ROW_SPLITS):
            s = slice(i * sub, (i + 1) * sub)
            x = x_ref[s, :]
            y = (x * lax.rsqrt(jnp.mean(x * x, axis=-1, keepdims=True) + NORM_EPS) * g_ref[...]).astype(BF16)
            xn_ref[s, :] = y
            o_ref[s, :] = jnp.dot(y, w_ref[...], preferred_element_type=F32)

    @pl.when(j != 0)
    def _():
        o_ref[...] = jnp.dot(xn_ref[...], w_ref[...], preferred_element_type=F32)


def _norm_matmul(x2d, g_row, w_bf16):
    m, d = x2d.shape
    n = w_bf16.shape[1]
    tm = min(m, 1024)
    tn = PROJ_TN
    assert m % tm == 0 and n % tn == 0
    blocks = tm * d * 4 + d * tn * 2 + tm * tn * 4
    return pl.pallas_call(
        _norm_matmul_kernel,
        out_shape=jax.ShapeDtypeStruct((m, n), F32),
        grid=(m // tm, n // tn),
        in_specs=[pl.BlockSpec((tm, d), lambda i, j: (i, 0)),
                  pl.BlockSpec((1, d), lambda i, j: (0, 0)),
                  pl.BlockSpec((d, tn), lambda i, j: (0, j))],
        out_specs=pl.BlockSpec((tm, tn), lambda i, j: (i, j)),
        scratch_shapes=[pltpu.VMEM((tm, d), BF16)],
        compiler_params=pltpu.CompilerParams(
            dimension_semantics=("parallel", "arbitrary"),
            vmem_limit_bytes=_vmem_limit(blocks, tm * d * 2)),
        name="norm_in_proj",
    )(x2d, g_row, w_bf16)


def _out_proj_kernel(a_ref, b_ref, wa_ref, wb_ref, h_ref, o_ref):
    out = (jnp.dot(a_ref[...], wa_ref[...], preferred_element_type=F32)
           + jnp.dot(b_ref[...], wb_ref[...], preferred_element_type=F32))
    o_ref[...] = h_ref[...] + out


def _glu_out_proj_norm_kernel(y_ref, u_ref, z_ref, od_ref, d_ref, gw_ref, gb_ref, wc_ref, wd_ref, h_ref, g_ref,
                              o_ref):
    tm = y_ref.shape[0]
    sub = tm // GLU_ROW_SPLITS
    halves = [slice(i * sub, (i + 1) * sub) for i in range(GLU_ROW_SPLITS)]
    y = [jax.nn.gelu(y_ref[s, :] + d_ref[...] * u_ref[s, :]) for s in halves]
    gate = [jax.nn.sigmoid(jnp.dot(v.astype(BF16), gw_ref[...], preferred_element_type=F32) + gb_ref[...])
            for v in y]
    o_c = [(y[i] * gate[i] * jax.nn.silu(z_ref[s, :])).astype(BF16) for i, s in enumerate(halves)]
    out = [jnp.dot(o_c[i], wc_ref[...], preferred_element_type=F32)
           + jnp.dot(od_ref[s, :], wd_ref[...], preferred_element_type=F32) for i, s in enumerate(halves)]
    for i, s in enumerate(halves):
        h = h_ref[s, :] + out[i]
        o_ref[s, :] = h * lax.rsqrt(jnp.mean(h * h, axis=-1, keepdims=True) + NORM_EPS) * g_ref[...]


def _glu_out_proj_norm(y2d, proj2d, o_d, d_row, glu_w, glu_b_row, w_c, w_d, h2d, final_g_row):
    m, d = h2d.shape
    tm = min(m, 512)
    assert m % tm == 0
    row = lambda col: (lambda i: (i, col))
    const = lambda i: (0, 0)
    resident = pl.Buffered(1)
    blocks = 3 * tm * C_W * 4 + tm * D_VW * 2 + 2 * tm * d * 4
    weights = C_W * C_W * 2 + (C_W + D_VW) * d * 2
    return pl.pallas_call(
        _glu_out_proj_norm_kernel,
        out_shape=jax.ShapeDtypeStruct((m, d), F32),
        grid=(m // tm,),
        in_specs=[pl.BlockSpec((tm, C_W), row(0)),
                  pl.BlockSpec((tm, C_W), row(CD_U // C_W)),
                  pl.BlockSpec((tm, C_W), row(CD_Z // C_W)),
                  pl.BlockSpec((tm, D_VW), row(0)),
                  pl.BlockSpec((1, C_W), const),
                  pl.BlockSpec((C_W, C_W), const, pipeline_mode=resident),
                  pl.BlockSpec((1, C_W), const),
                  pl.BlockSpec((C_W, d), const, pipeline_mode=resident),
                  pl.BlockSpec((D_VW, d), const, pipeline_mode=resident),
                  pl.BlockSpec((tm, d), row(0)),
                  pl.BlockSpec((1, d), const)],
        out_specs=pl.BlockSpec((tm, d), row(0)),
        compiler_params=pltpu.CompilerParams(
            dimension_semantics=("parallel",),
            vmem_limit_bytes=_vmem_limit(blocks, weights + 3 * tm * C_W * 4)),
        name="glu_out_proj_norm",
    )(y2d, proj2d, proj2d, o_d, d_row, glu_w, glu_b_row, w_c, w_d, h2d, final_g_row)


def _out_proj(mix_a, mix_b, w_a, w_b, h2d):
    m, d = h2d.shape
    ka, kb = mix_a.shape[1], mix_b.shape[1]
    tm = min(m, 512)
    assert m % tm == 0
    blocks = tm * (ka + kb) * 2 + (ka + kb) * d * 2 + 2 * tm * d * 4
    return pl.pallas_call(
        _out_proj_kernel,
        out_shape=jax.ShapeDtypeStruct((m, d), F32),
        grid=(m // tm,),
        in_specs=[pl.BlockSpec((tm, ka), lambda i: (i, 0)),
                  pl.BlockSpec((tm, kb), lambda i: (i, 0)),
                  pl.BlockSpec((ka, d), lambda i: (0, 0)),
                  pl.BlockSpec((kb, d), lambda i: (0, 0)),
                  pl.BlockSpec((tm, d), lambda i: (i, 0))],
        out_specs=pl.BlockSpec((tm, d), lambda i: (i, 0)),
        compiler_params=pltpu.CompilerParams(
            dimension_semantics=("parallel",),
            vmem_limit_bytes=_vmem_limit(blocks)),
        name="out_proj",
    )(mix_a, mix_b, w_a, w_b, h2d)


def _gla_kernel(q_ref, k_ref, v_ref, z_ref, sm_ref, gw_ref, gb_ref, ng_ref, s0_ref,
                o_ref, sout_ref, st_ref, *, c, nchunks):
    t = pl.program_id(1)
    last_t = pl.num_programs(1) - 1
    bb = q_ref.shape[0]
    chains = [(bi, h) for bi in range(bb) for h in range(A_HEADS)]
    ksl = [slice(h * A_DK, (h + 1) * A_DK) for _, h in chains]
    vsl = [slice(h * A_DV, (h + 1) * A_DV) for _, h in chains]
    idx = range(len(chains))

    @pl.when(t == 0)
    def _():
        for bi, h in chains:
            st_ref[bi, h] = s0_ref[bi, h].T

    causal, _ = _causal_masks(c)
    tri = causal.astype(BF16)
    gw = gw_ref[...]
    gb = gb_ref[...]

    def body(n, carry):
        sl = pl.ds(pl.multiple_of(n * c, c), c)
        b_all = [_cumsum_rows(tri, jax.nn.log_sigmoid(_dot(sm_ref[bi, sl, :], gw) + gb) * (1.0 / A_GATE_TAU))
                 for bi in range(bb)]
        b = [b_all[chains[i][0]][:, ksl[i]] for i in idx]
        b_last = [b[i][c - 1:c, :] for i in idx]
        k = [k_ref[chains[i][0], sl, ksl[i]] for i in idx]
        v = [v_ref[chains[i][0], sl, vsl[i]].astype(BF16) for i in idx]
        q_dec = [(q_ref[chains[i][0], sl, ksl[i]] * (A_DK ** -0.5) * jnp.exp(b[i])).astype(BF16) for i in idx]
        k_dec = [(k[i] * jnp.exp(-b[i])).astype(BF16) for i in idx]
        k_w = [(k[i] * jnp.exp(b_last[i] - b[i])).astype(BF16) for i in idx]
        scores = [jnp.where(causal, _dot_nt(q_dec[i], k_dec[i]), 0.0).astype(BF16) for i in idx]
        s_t = [st_ref[bi, h] for bi, h in chains]
        outs = [_mm(scores[i], v[i]) + _dot_nt(q_dec[i], s_t[i]) for i in idx]
        for i, (bi, h) in enumerate(chains):
            st_ref[bi, h] = s_t[i] * jnp.exp(b_last[i]) + _dot_tn(v[i], k_w[i])
        for i, (bi, h) in enumerate(chains):
            o = outs[i]
            o = o * lax.rsqrt(jnp.mean(o * o, axis=-1, keepdims=True) + NORM_EPS) * ng_ref[:, vsl[i]]
            o_ref[bi, sl, vsl[i]] = (o * jax.nn.silu(z_ref[bi, sl, vsl[i]])).astype(o_ref.dtype)
        return carry

    lax.fori_loop(0, nchunks, body, 0)

    @pl.when(t == last_t)
    def _():
        for bi, h in chains:
            sout_ref[bi, h] = st_ref[bi, h].T


def _gla(proj3, gate_w_pad, gate_b_row, norm_g_row, s0, *, c, tb, bb):
    bsz, l, _ = proj3.shape
    nblk = l // tb
    assert bsz % bb == 0
    tok = lambda col: (lambda b, t: (b, t, col))
    in_specs = [
        pl.BlockSpec((bb, tb, A_KW), tok(AB_Q // A_KW)),
        pl.BlockSpec((bb, tb, A_KW), tok(AB_K // A_KW)),
        pl.BlockSpec((bb, tb, A_VW), tok(AB_V // A_VW)),
        pl.BlockSpec((bb, tb, A_VW), tok(AB_Z // A_VW)),
        pl.BlockSpec((bb, tb, LANES), tok(AB_SMALL // LANES)),
        pl.BlockSpec((LANES, A_KW), lambda b, t: (0, 0)),
        pl.BlockSpec((1, A_KW), lambda b, t: (0, 0)),
        pl.BlockSpec((1, A_VW), lambda b, t: (0, 0)),
        pl.BlockSpec((bb, A_HEADS, A_DK, A_DV), lambda b, t: (b, 0, 0, 0)),
    ]
    out_specs = [
        pl.BlockSpec((bb, tb, A_VW), lambda b, t: (b, t, 0)),
        pl.BlockSpec((bb, A_HEADS, A_DK, A_DV), lambda b, t: (b, 0, 0, 0)),
    ]
    state = bb * A_HEADS * A_DK * A_DV * 4
    blocks = bb * tb * (2 * A_KW + 2 * A_VW + LANES) * 4 + bb * tb * A_VW * 2 + 2 * state
    return pl.pallas_call(
        functools.partial(_gla_kernel, c=c, nchunks=tb // c),
        out_shape=[jax.ShapeDtypeStruct((bsz, l, A_VW), BF16),
                   jax.ShapeDtypeStruct((bsz, A_HEADS, A_DK, A_DV), F32)],
        grid=(bsz // bb, nblk),
        in_specs=in_specs,
        out_specs=out_specs,
        scratch_shapes=[pltpu.VMEM((bb, A_HEADS, A_DV, A_DK), F32)],
        compiler_params=pltpu.CompilerParams(
            dimension_semantics=("parallel", "arbitrary"),
            vmem_limit_bytes=_vmem_limit(blocks, state)),
        name="gla_mixer",
    )(proj3, proj3, proj3, proj3, proj3, gate_w_pad, gate_b_row, norm_g_row, s0)


def _gdn_kernel(x_ref, z_ref, sm_ref, w_ref, cp_ref, alog_ref, dtb_ref, ng_ref, s0_ref,
                o_ref, sout_ref, s_ref, tail_ref, u_ref, wm_ref, qg_ref, kg_ref, qk_ref, gl_ref,
                *, c, nchunks):
    t = pl.program_id(1)
    last_t = pl.num_programs(1) - 1
    tb = c * nchunks
    keep = SUBLANES - (B_CONV - 1)
    bb = x_ref.shape[0]

    @pl.when(t == 0)
    def _():
        s_ref[...] = s0_ref[...]
        tail_ref[:, 0:keep, :] = jnp.zeros((bb, keep, B_QKV), F32)
        tail_ref[:, keep:SUBLANES, :] = cp_ref[...]

    def conv_silu(bi, row0, first, cols):
        x = x_ref[bi, pl.ds(row0, c), cols]
        if first is None:
            prev = x_ref[bi, pl.ds(pl.multiple_of(row0 - SUBLANES, SUBLANES), SUBLANES), cols]
        else:
            before = pl.multiple_of(jnp.maximum(row0 - SUBLANES, 0), SUBLANES)
            prev = jnp.where(first, tail_ref[bi, :, cols], x_ref[bi, pl.ds(before, SUBLANES), cols])
        w = w_ref[:, cols]
        ext = jnp.concatenate([prev, x], axis=0)
        ext1 = pltpu.roll(ext, 1, 0)
        newer = ext * w[3:4, :] + ext1 * w[2:3, :]
        older = ext * w[1:2, :] + ext1 * w[0:1, :]
        conv = newer + pltpu.roll(older, 2, 0)
        return jax.nn.silu(conv[SUBLANES:SUBLANES + c, :])

    causal, strict = _causal_masks(c)
    tri = causal.astype(BF16)
    eye = jnp.where(causal & jnp.logical_not(strict), 1.0, 0.0).astype(F32)
    sel = _lane_selector(AB_APRE_LANE)
    neg_a_exp = -jnp.exp(alog_ref[...])
    dtb = dtb_ref[...]
    ng = ng_ref[...]
    n_double = int(math.log2(c)) - 1
    heads = range(B_HEADS)

    group = max(1, GDN_WY_CHAINS // (bb * B_HEADS))
    group = group if nchunks % group == 0 else 1

    def wy_factors(n, carry):
        pws, rhss, where = [], [], []
        for bi, ci in [(bi, ci) for ci in range(group) for bi in range(bb)]:
            row0 = pl.multiple_of((n * group + ci) * c, c)
            first = (n == 0) if ci == 0 else None
            sl = pl.ds(row0, c)
            sm = sm_ref[bi, sl, :]
            g_cum = _cumsum_rows(tri, neg_a_exp * jax.nn.softplus(sm + dtb))
            g_rows = _select_rows(sel, g_cum)
            beta_all = jax.nn.sigmoid(sm)
            gl_ref[bi, pl.ds(n * group + ci, 1), :] = g_cum[c - 1:c, :]
            for h in heads:
                hs = slice(h * B_DK, (h + 1) * B_DK)
                q = conv_silu(bi, row0, first, hs)
                k = conv_silu(bi, row0, first, slice(B_KW + h * B_DK, B_KW + (h + 1) * B_DK))
                v = conv_silu(bi, row0, first, slice(2 * B_KW + h * B_DV, 2 * B_KW + (h + 1) * B_DV))
                q = q * lax.rsqrt(jnp.sum(q * q, axis=-1, keepdims=True) + NORM_EPS) * (B_DK ** -0.5)
                k = k * lax.rsqrt(jnp.sum(k * k, axis=-1, keepdims=True) + NORM_EPS)
                g_col = g_cum[:, AB_APRE_LANE + h:AB_APRE_LANE + h + 1]
                beta = beta_all[:, AB_BETA_LANE + h:AB_BETA_LANE + h + 1]
                decay = jnp.exp(jnp.where(causal, g_col - g_rows[h:h + 1, :], -jnp.inf))
                e_g = jnp.exp(g_col)
                k_beta = k * beta
                pws.append(-jnp.where(strict, _dot_nt(k_beta, k) * decay, 0.0))
                rhss.append(jnp.concatenate([v * beta, k_beta * e_g], axis=1))
                where.append((bi, sl, hs))
                qk_ref[bi, sl, h * LANES:h * LANES + c] = jnp.where(causal, _dot_nt(q, k) * decay,
                                                                    0.0).astype(BF16)
                qg_ref[bi, sl, hs] = (q * e_g).astype(BF16)
                kg_ref[bi, sl, hs] = (k * jnp.exp(g_col[c - 1:c, :] - g_col)).astype(BF16)
        items = range(len(pws))
        neg_lower = [_split2(p) for p in pws]
        invs = [eye + p for p in pws]
        for _ in range(n_double):
            pws = [_dot(p, p) for p in pws]
            invs = [i + _dot(i, p) for i, p in zip(invs, pws)]
        invs = [i.astype(BF16) for i in invs]
        sol = [_mm(invs[i], rhss[i].astype(BF16)).astype(BF16) for i in items]
        resid = [rhss[i] - sol[i].astype(F32) + _mm(neg_lower[i][0], sol[i]) + _mm(neg_lower[i][1], sol[i])
                 for i in items]
        for i in items:
            bi, sl, hs = where[i]
            uw = sol[i].astype(F32) + _mm(invs[i], resid[i].astype(BF16))
            u_ref[bi, sl, hs] = uw[:, :B_DV]
            wm_ref[bi, sl, hs] = uw[:, B_DV:].astype(BF16)
        return carry

    lax.fori_loop(0, nchunks // group, wy_factors, 0)
    tail_ref[...] = x_ref[:, tb - SUBLANES:tb, :]

    chains = [(bi, h) for bi in range(bb) for h in heads]
    hsl = [slice(h * B_DK, (h + 1) * B_DK) for _, h in chains]

    def recurrence(n, carry):
        sl = pl.ds(pl.multiple_of(n * c, c), c)
        e_last = [jnp.exp(gl_ref[bi, pl.ds(n, 1), :]) for bi in range(bb)]
        s_old = [s_ref[bi, h] for bi, h in chains]
        s_bf = [s.astype(BF16) for s in s_old]
        v_new = [(u_ref[bi, sl, hsl[i]] - _mm(wm_ref[bi, sl, hsl[i]], s_bf[i])).astype(BF16)
                 for i, (bi, h) in enumerate(chains)]
        outs = [_mm(qg_ref[bi, sl, hsl[i]], s_bf[i]) + _mm(qk_ref[bi, sl, h * LANES:h * LANES + c], v_new[i])
                for i, (bi, h) in enumerate(chains)]
        for i, (bi, h) in enumerate(chains):
            s_ref[bi, h] = (e_last[bi][:, AB_APRE_LANE + h:AB_APRE_LANE + h + 1] * s_old[i]
                            + lax.dot_general(kg_ref[bi, sl, hsl[i]], v_new[i], (((0,), (0,)), ((), ())),
                                              preferred_element_type=F32))
        for i, (bi, h) in enumerate(chains):
            o = outs[i]
            o = o * lax.rsqrt(jnp.mean(o * o, axis=-1, keepdims=True) + NORM_EPS) * ng
            o_ref[bi, sl, hsl[i]] = (o * jax.nn.silu(z_ref[bi, sl, hsl[i]])).astype(o_ref.dtype)
        return carry

    lax.fori_loop(0, nchunks, recurrence, 0)

    @pl.when(t == last_t)
    def _():
        sout_ref[...] = s_ref[...]


def _gdn(proj3, conv_prev, conv_w, alog_row, dtb_row, norm_g_row, s0, *, c, tb, bb):
    bsz, l, _ = proj3.shape
    nblk = l // tb
    assert AB_QKV % B_QKV == 0 and AB_ZB % B_VW == 0 and bsz % bb == 0
    in_specs = [
        pl.BlockSpec((bb, tb, B_QKV), lambda b, t: (b, t, AB_QKV // B_QKV)),
        pl.BlockSpec((bb, tb, B_VW), lambda b, t: (b, t, AB_ZB // B_VW)),
        pl.BlockSpec((bb, tb, LANES), lambda b, t: (b, t, AB_SMALL // LANES)),
        pl.BlockSpec((B_CONV, B_QKV), lambda b, t: (0, 0)),
        pl.BlockSpec((bb, B_CONV - 1, B_QKV), lambda b, t: (b, 0, 0)),
        pl.BlockSpec((1, LANES), lambda b, t: (0, 0)),
        pl.BlockSpec((1, LANES), lambda b, t: (0, 0)),
        pl.BlockSpec((1, B_DV), lambda b, t: (0, 0)),
        pl.BlockSpec((bb, B_HEADS, B_DK, B_DV), lambda b, t: (b, 0, 0, 0)),
    ]
    out_specs = [
        pl.BlockSpec((bb, tb, B_VW), lambda b, t: (b, t, 0)),
        pl.BlockSpec((bb, B_HEADS, B_DK, B_DV), lambda b, t: (b, 0, 0, 0)),
    ]
    blocks = bb * (tb * (B_QKV + B_VW + LANES) * 4 + tb * B_VW * 2 + 2 * B_HEADS * B_DK * B_DV * 4)
    scratch = bb * (B_HEADS * B_DK * B_DV * 4 + SUBLANES * B_QKV * 4 + tb * B_VW * 4
                    + 4 * tb * B_KW * 2 + SUBLANES * LANES * 4)
    return pl.pallas_call(
        functools.partial(_gdn_kernel, c=c, nchunks=tb // c),
        out_shape=[jax.ShapeDtypeStruct((bsz, l, B_VW), BF16),
                   jax.ShapeDtypeStruct((bsz, B_HEADS, B_DK, B_DV), F32)],
        grid=(bsz // bb, nblk),
        in_specs=in_specs,
        out_specs=out_specs,
        scratch_shapes=[pltpu.VMEM((bb, B_HEADS, B_DK, B_DV), F32),
                        pltpu.VMEM((bb, SUBLANES, B_QKV), F32),
                        pltpu.VMEM((bb, tb, B_VW), F32),
                        pltpu.VMEM((bb, tb, B_KW), BF16),
                        pltpu.VMEM((bb, tb, B_KW), BF16),
                        pltpu.VMEM((bb, tb, B_KW), BF16),
                        pltpu.VMEM((bb, tb, B_HEADS * LANES), BF16),
                        pltpu.VMEM((bb, SUBLANES, LANES), F32)],
        compiler_params=pltpu.CompilerParams(
            dimension_semantics=("parallel", "arbitrary"),
            vmem_limit_bytes=_vmem_limit(blocks, scratch)),
        name="gdn_mixer",
    )(proj3, proj3, proj3, conv_w, conv_prev, alog_row, dtb_row, norm_g_row, s0)


def _mlstm_kernel(q_ref, k_ref, v_ref, og_ref, z_ref, sm_ref, ib_ref, fb_ref, ng_ref,
                  c0_ref, n0_ref, m0_ref, o_ref, cout_ref, nout_ref, mout_ref,
                  c_ref, n_ref, m_ref, hi_ref, mi_ref, ni_ref, bc_ref, kv_ref, ks_ref, mc_ref, bl_ref,
                  *, c, nchunks):
    t = pl.program_id(1)
    last_t = pl.num_programs(1) - 1
    bb = q_ref.shape[0]
    chains = [(bi, h) for bi in range(bb) for h in range(D_HEADS)]
    ksl = [slice(h * D_DK, (h + 1) * D_DK) for _, h in chains]
    vsl = [slice(h * D_DV, (h + 1) * D_DV) for _, h in chains]
    idx = range(len(chains))

    @pl.when(t == 0)
    def _():
        c_ref[...] = c0_ref[...]
        n_ref[...] = n0_ref[...]
        for bi, h in chains:
            m_ref[bi, h] = jnp.broadcast_to(m0_ref[bi, :, h:h + 1], (1, LANES))

    causal, _ = _causal_masks(c)
    tri = causal.astype(BF16)
    sel = _lane_selector(0)
    lane = lax.broadcasted_iota(jnp.int32, (c, LANES), 1)
    ib = ib_ref[...]
    fb = fb_ref[...]

    group = MLSTM_INTRA_CHUNKS if nchunks % MLSTM_INTRA_CHUNKS == 0 else 1

    def intra(n, carry):
        items = [(bi, ci, h) for ci in range(group) for bi, h in chains]
        sls = [pl.ds(pl.multiple_of((n * group + ci) * c, c), c) for ci in range(group)]
        sm = {(bi, ci): sm_ref[bi, sls[ci], :] for ci in range(group) for bi in range(bb)}
        i_full = {key: x + ib for key, x in sm.items()}
        b_full = {key: _cumsum_rows(tri, jax.nn.log_sigmoid(x + fb)) for key, x in sm.items()}
        rows = {key: _select_rows(sel, jnp.where(lane < CD_F_LANE, i_full[key], b_full[key]))
                for key in sm}
        b_col = [b_full[bi, ci][:, CD_F_LANE + h:CD_F_LANE + h + 1] for bi, ci, h in items]
        i_col = [i_full[bi, ci][:, CD_I_LANE + h:CD_I_LANE + h + 1] for bi, ci, h in items]
        logw = [jnp.where(causal, b_col[i] - rows[bi, ci][CD_F_LANE + h:CD_F_LANE + h + 1, :]
                          + rows[bi, ci][CD_I_LANE + h:CD_I_LANE + h + 1, :], -jnp.inf)
                for i, (bi, ci, h) in enumerate(items)]
        ids = range(len(items))
        m_intra = [jnp.max(logw[i], axis=-1, keepdims=True) for i in ids]
        ks_ = [slice(h * D_DK, (h + 1) * D_DK) for _, _, h in items]
        vs_ = [slice(h * D_DV, (h + 1) * D_DV) for _, _, h in items]
        q_bf = [(q_ref[bi, sls[ci], ks_[i]] * (D_DK ** -0.5)).astype(BF16) for i, (bi, ci, h) in enumerate(items)]
        k = [k_ref[bi, sls[ci], ks_[i]] for i, (bi, ci, h) in enumerate(items)]
        v = [v_ref[bi, sls[ci], vs_[i]].astype(BF16) for i, (bi, ci, h) in enumerate(items)]
        p = [jnp.exp(logw[i] - m_intra[i]) * _dot_nt(q_bf[i], k[i]) for i in ids]
        n_intra = [jnp.sum(p[i], axis=-1, keepdims=True) for i in ids]
        m_chunk = [m_intra[i][c - 1:c, :] for i in ids]
        k_w = [k[i] * jnp.exp(b_col[i][c - 1:c, :] - b_col[i] + i_col[i] - m_chunk[i]) for i in ids]
        rep = lambda x: jnp.broadcast_to(x, (x.shape[0], LANES))
        for i, (bi, ci, h) in enumerate(items):
            row = pl.ds(n * group + ci, 1)
            hi_ref[bi, sls[ci], vs_[i]] = _dot(p[i], v[i])
            kv_ref[bi, n * group + ci, h] = _dot_tn(k_w[i], v[i])
            ks_ref[bi, h, row, :] = jnp.sum(k_w[i], axis=0, keepdims=True)
            mi_ref[bi, h, sls[ci], :] = rep(m_intra[i])
            ni_ref[bi, h, sls[ci], :] = rep(n_intra[i])
            bc_ref[bi, h, sls[ci], :] = rep(b_col[i])
            mc_ref[bi, h, row, :] = rep(m_chunk[i])
            bl_ref[bi, h, row, :] = rep(b_col[i][c - 1:c, :])
        return carry

    lax.fori_loop(0, nchunks // group, intra, 0)

    twice = lambda x: jnp.concatenate([x, x], axis=1)

    def body(n, carry):
        sl = pl.ds(pl.multiple_of(n * c, c), c)
        row = pl.ds(n, 1)
        q = [q_ref[chains[i][0], sl, ksl[i]] * (D_DK ** -0.5) for i in idx]
        c_s = [c_ref[bi, h] for bi, h in chains]
        n_s = [n_ref[bi, h:h + 1, :] for bi, h in chains]
        m_s = [m_ref[bi, h] for bi, h in chains]
        qc = [_dot(q[i], c_s[i]) for i in idx]
        b_last = [bl_ref[bi, h, row, :] for bi, h in chains]
        m_chunk = [mc_ref[bi, h, row, :] for bi, h in chains]
        for i, (bi, h) in enumerate(chains):
            m_new = jnp.maximum(b_last[i] + m_s[i], m_chunk[i])
            w_old = jnp.exp(b_last[i] + m_s[i] - m_new)
            w_new = jnp.exp(m_chunk[i] - m_new)
            c_ref[bi, h] = twice(w_old) * c_s[i] + twice(w_new) * kv_ref[bi, n, h]
            n_ref[bi, h:h + 1, :] = w_old * n_s[i] + w_new * ks_ref[bi, h, row, :]
            m_ref[bi, h] = m_new
        for i, (bi, h) in enumerate(chains):
            m_intra = mi_ref[bi, h, sl, :]
            a = bc_ref[bi, h, sl, :] + m_s[i]
            m_t = jnp.maximum(a, m_intra)
            w_a = jnp.exp(a - m_t)
            w_i = jnp.exp(m_intra - m_t)
            num = twice(w_a) * qc[i] + twice(w_i) * hi_ref[bi, sl, vsl[i]]
            den = w_a * jnp.sum(q[i] * n_s[i], axis=-1, keepdims=True) + w_i * ni_ref[bi, h, sl, :]
            hh = num / twice(jnp.maximum(jnp.abs(den), jnp.exp(-m_t)))
            hd = jax.nn.sigmoid(og_ref[bi, sl, vsl[i]]) * hh
            oc = hd - jnp.mean(hd, axis=-1, keepdims=True)
            o = oc * lax.rsqrt(jnp.mean(oc * oc, axis=-1, keepdims=True) + NORM_EPS) * ng_ref[:, vsl[i]]
            o_ref[bi, sl, vsl[i]] = (o * jax.nn.silu(z_ref[bi, sl, vsl[i]])).astype(o_ref.dtype)
        return carry

    lax.fori_loop(0, nchunks, body, 0)

    @pl.when(t == last_t)
    def _():
        cout_ref[...] = c_ref[...]
        nout_ref[...] = n_ref[...]
        for bi, h in chains:
            mout_ref[bi, :, h:h + 1] = m_ref[bi, h][:, 0:1]


def _mlstm(proj3, ib_row, fb_row, norm_g_row, c0, n0, m0, *, c, tb, bb):
    bsz, l, _ = proj3.shape
    nblk = l // tb
    assert bsz % bb == 0
    tok = lambda col: (lambda b, t: (b, t, col))
    st4 = lambda b, t: (b, 0, 0, 0)
    st3 = lambda b, t: (b, 0, 0)
    in_specs = [
        pl.BlockSpec((bb, tb, D_KW), tok(CD_Q // D_KW)),
        pl.BlockSpec((bb, tb, D_KW), tok(CD_K // D_KW)),
        pl.BlockSpec((bb, tb, D_VW), tok(CD_V // D_VW)),
        pl.BlockSpec((bb, tb, D_VW), tok(CD_O // D_VW)),
        pl.BlockSpec((bb, tb, D_VW), tok(CD_ZD // D_VW)),
        pl.BlockSpec((bb, tb, LANES), tok(CD_SMALL // LANES)),
        pl.BlockSpec((1, LANES), lambda b, t: (0, 0)),
        pl.BlockSpec((1, LANES), lambda b, t: (0, 0)),
        pl.BlockSpec((1, D_VW), lambda b, t: (0, 0)),
        pl.BlockSpec((bb, D_HEADS, D_DK, D_DV), st4),
        pl.BlockSpec((bb, D_HEADS, D_DK), st3),
        pl.BlockSpec((bb, 1, D_HEADS), st3),
    ]
    out_specs = [
        pl.BlockSpec((bb, tb, D_VW), lambda b, t: (b, t, 0)),
        pl.BlockSpec((bb, D_HEADS, D_DK, D_DV), st4),
        pl.BlockSpec((bb, D_HEADS, D_DK), st3),
        pl.BlockSpec((bb, 1, D_HEADS), st3),
    ]
    state = bb * D_HEADS * D_DK * D_DV * 4
    blocks = bb * tb * (2 * D_KW + 3 * D_VW + LANES) * 4 + bb * tb * D_VW * 2 + 2 * state
    nchunks = tb // c
    return pl.pallas_call(
        functools.partial(_mlstm_kernel, c=c, nchunks=nchunks),
        out_shape=[jax.ShapeDtypeStruct((bsz, l, D_VW), BF16),
                   jax.ShapeDtypeStruct((bsz, D_HEADS, D_DK, D_DV), F32),
                   jax.ShapeDtypeStruct((bsz, D_HEADS, D_DK), F32),
                   jax.ShapeDtypeStruct((bsz, 1, D_HEADS), F32)],
        grid=(bsz // bb, nblk),
        in_specs=in_specs,
        out_specs=out_specs,
        scratch_shapes=[pltpu.VMEM((bb, D_HEADS, D_DK, D_DV), F32),
                        pltpu.VMEM((bb, D_HEADS, D_DK), F32),
                        pltpu.VMEM((bb, D_HEADS, 1, LANES), F32),
                        pltpu.VMEM((bb, tb, D_VW), F32),
                        pltpu.VMEM((bb, D_HEADS, tb, LANES), F32),
                        pltpu.VMEM((bb, D_HEADS, tb, LANES), F32),
                        pltpu.VMEM((bb, D_HEADS, tb, LANES), F32),
                        pltpu.VMEM((bb, nchunks, D_HEADS, D_DK, D_DV), F32),
                        pltpu.VMEM((bb, D_HEADS, _round_up(nchunks, SUBLANES), D_DK), F32),
                        pltpu.VMEM((bb, D_HEADS, _round_up(nchunks, SUBLANES), LANES), F32),
                        pltpu.VMEM((bb, D_HEADS, _round_up(nchunks, SUBLANES), LANES), F32)],
        compiler_params=pltpu.CompilerParams(
            dimension_semantics=("parallel", "arbitrary"),
            vmem_limit_bytes=_vmem_limit(blocks, state * (1 + nchunks) + bb * tb * (D_VW + 2 * LANES) * 4)),
        name="mlstm_mixer",
    )(proj3, proj3, proj3, proj3, proj3, proj3, ib_row, fb_row, norm_g_row, c0, n0, m0)


def _s5_expand_operators(kc_ref, bc_ref, cc_ref, bd_ref, bst_ref, cst_ref):
    tc, gt, sw, cg = S5_CHUNK, S5_GROUP_BLOCK, 2 * C_STATE, C_GROUP
    w_t = tc * LANES
    iota = lambda shape, d: lax.broadcasted_iota(jnp.int32, shape, d)
    row_g = (iota((w_t, LANES), 0) // cg) % gt
    tile16 = jnp.where(iota((cg, LANES), 1) % cg == iota((cg, LANES), 0), 1.0, 0.0).astype(BF16)
    bd = jnp.where(row_g == iota((w_t, LANES), 1) // cg, _mm(kc_ref[...], tile16), 0.0).astype(BF16)
    bd_ref[:, LANES:2 * LANES] = bd
    bd_ref[0:w_t - LANES, 0:LANES] = bd[LANES:, :]
    bd_ref[w_t - LANES:w_t, 0:LANES] = jnp.zeros((LANES, LANES), BF16)
    bc = bc_ref[...].astype(F32)
    for g in range(gt):
        bst_ref[:, g * sw:(g + 1) * sw] = jnp.where(row_g == g, bc, 0.0).astype(BF16)
    src, dst = iota((tc * cg, w_t), 0), iota((tc * cg, w_t), 1)
    spread = jnp.where((src // cg == dst // LANES) & (src % cg == dst % cg), 1.0, 0.0).astype(BF16)
    lane_g = (iota((sw, w_t), 1) // cg) % gt
    for g in range(gt):
        full = _mm(cc_ref[g * sw:(g + 1) * sw, :], spread)
        cst_ref[g * sw:(g + 1) * sw, :] = jnp.where(lane_g == g, full, 0.0).astype(BF16)


def _s5_kernel(u_ref, kc_ref, bc_ref, cc_ref, apow_ref, x0_ref, y_ref, xf_ref, bd_ref, bst_ref, cst_ref,
               *, nc, bb):
    m = nc * bb
    tc = S5_CHUNK
    sw = 2 * C_STATE

    @pl.when(pl.program_id(1) == 0)
    def _():
        _s5_expand_operators(kc_ref, bc_ref, cc_ref, bd_ref, bst_ref, cst_ref)

    row = lax.broadcasted_iota(jnp.int32, (m, sw), 0)
    n_idx = row & (nc - 1)
    n_log = int(math.log2(nc))

    def cmul(a1, a2, x):
        return a1 * x + a2 * pltpu.roll(x, C_STATE, 1)

    lhs = jnp.concatenate([u_ref[pl.ds(tau, m, stride=tc), :].astype(BF16) for tau in range(tc)], axis=1)
    e_all = _mm(lhs, bst_ref[...])
    groups = range(S5_GROUP_BLOCK)
    gsl = [slice(g * sw, (g + 1) * sw) for g in groups]
    x0_rows = []
    for g in groups:
        rows0 = jnp.zeros((m, sw), F32)
        for b in range(bb):
            rows0 = jnp.where(row == b * nc, x0_ref[b, :, gsl[g]], rows0)
        x0_rows.append(rows0)
    x = [e_all[:, gsl[g]] + cmul(apow_ref[0:1, gsl[g]], apow_ref[1:2, gsl[g]], x0_rows[g]) for g in groups]
    for j in range(n_log):
        sh = 1 << j
        shifted = [jnp.where(n_idx >= sh, pltpu.roll(x[g], sh, 0), 0.0) for g in groups]
        x = [x[g] + cmul(apow_ref[2 * j:2 * j + 1, gsl[g]], apow_ref[2 * j + 1:2 * j + 2, gsl[g]], shifted[g])
             for g in groups]
    x_start = [jnp.where(n_idx >= 1, pltpu.roll(x[g], 1, 0), x0_rows[g]).astype(BF16) for g in groups]
    for g in groups:
        for b in range(bb):
            xf_ref[b, :, gsl[g]] = x[g][b * nc + nc - 1:b * nc + nc, :]
    y_state = _mm(jnp.concatenate(x_start, axis=1), cst_ref[...])
    for tau in range(0, tc, 2):
        width = (tau + 2) * LANES
        start = (tc - 2 - tau) * LANES
        y = _mm(lhs[:, :width], bd_ref[start:start + width, :]) + y_state[:, tau * LANES:(tau + 2) * LANES]
        y_ref[pl.ds(tau, m, stride=tc), :] = y[:, :LANES]
        y_ref[pl.ds(tau + 1, m, stride=tc), :] = y[:, LANES:]


def _s5_chunks(proj2d, kc, bc, cc, apow, x0, *, l, bb):
    tc = S5_CHUNK
    nc = l // tc
    bsz = x0.shape[1]
    nt = C_GROUPS // S5_GROUP_BLOCK
    rows = bb * l
    sw = 2 * C_STATE
    sw_t = S5_GROUP_BLOCK * sw
    w_t = tc * LANES
    assert CD_U % LANES == 0 and bsz % bb == 0
    blocks = (2 * rows * LANES * 4 + 2 * w_t * LANES * 2 + sw_t * tc * C_GROUP * 2 + apow.shape[1] * sw_t * 4
              + 2 * bb * SUBLANES * sw_t * 4)
    scratch = w_t * 2 * LANES * 2 + 2 * w_t * sw_t * 2
    temps = bb * nc * (w_t * 2 + w_t * 4 + 3 * sw_t * 4) + 4 * sw * w_t * 4
    return pl.pallas_call(
        functools.partial(_s5_kernel, nc=nc, bb=bb),
        out_shape=[jax.ShapeDtypeStruct((bsz * l, C_W), F32),
                   jax.ShapeDtypeStruct((nt, bsz, 1, sw_t), F32)],
        grid=(nt, bsz // bb),
        in_specs=[pl.BlockSpec((rows, LANES), lambda i, j: (j, CD_U // LANES + i)),
                  pl.BlockSpec((None, w_t, C_GROUP), lambda i, j: (i, 0, 0)),
                  pl.BlockSpec((None, w_t, sw), lambda i, j: (i, 0, 0)),
                  pl.BlockSpec((None, sw_t, tc * C_GROUP), lambda i, j: (i, 0, 0)),
                  pl.BlockSpec((None, apow.shape[1], sw_t), lambda i, j: (i, 0, 0)),
                  pl.BlockSpec((None, bb, 1, sw_t), lambda i, j: (i, j, 0, 0))],
        out_specs=[pl.BlockSpec((rows, LANES), lambda i, j: (j, i)),
                   pl.BlockSpec((None, bb, 1, sw_t), lambda i, j: (i, j, 0, 0))],
        scratch_shapes=[pltpu.VMEM((w_t, 2 * LANES), BF16),
                        pltpu.VMEM((w_t, sw_t), BF16),
                        pltpu.VMEM((sw_t, w_t), BF16)],
        compiler_params=pltpu.CompilerParams(
            dimension_semantics=("parallel", "arbitrary"),
            vmem_limit_bytes=_vmem_limit(blocks, scratch + temps)),
        name="s5_chunks",
    )(proj2d, kc, bc, cc, apow, x0)


def _s5_operators(lam_re, lam_im, log_dt, b_re, b_im, c_re, c_im, n_log):
    g, p = lam_re.shape
    tc = S5_CHUNK
    dt = jnp.exp(log_dt.astype(F32))[:, None]
    mag = jnp.exp(lam_re * dt)
    ab_re, ab_im = mag * jnp.cos(lam_im * dt), mag * jnp.sin(lam_im * dt)
    den = lam_re * lam_re + lam_im * lam_im
    er = ab_re - 1.0
    zr = (er * lam_re + ab_im * lam_im) / den
    zi = (ab_im * lam_re - er * lam_im) / den
    bb_re = zr[..., None] * b_re - zi[..., None] * b_im
    bb_im = zr[..., None] * b_im + zi[..., None] * b_re
    pw_re, pw_im = ab_re[None], ab_im[None]
    while pw_re.shape[0] < tc:
        top_re, top_im = pw_re[-1], pw_im[-1]
        pw_re, pw_im = (jnp.concatenate([pw_re, top_re * pw_re - top_im * pw_im]),
                        jnp.concatenate([pw_im, top_re * pw_im + top_im * pw_re]))
    pw_re = jnp.concatenate([jnp.ones_like(ab_re)[None], pw_re])
    pw_im = jnp.concatenate([jnp.zeros_like(ab_im)[None], pw_im])
    abr = pw_re[:tc, :, :, None] * bb_re - pw_im[:tc, :, :, None] * bb_im
    abi = pw_re[:tc, :, :, None] * bb_im + pw_im[:tc, :, :, None] * bb_re
    kern = (jnp.einsum('gjp,dgpi->dgji', c_re, abr, precision=HIGHEST)
            - jnp.einsum('gjp,dgpi->dgji', c_im, abi, precision=HIGHEST))
    gt = S5_GROUP_BLOCK
    nt = g // gt
    sw = 2 * p
    kc = kern[::-1].reshape(tc, nt, gt, C_GROUP, C_GROUP).transpose(1, 0, 2, 4, 3)
    kc = kc.reshape(nt, tc * gt * C_GROUP, C_GROUP)
    ab = jnp.concatenate([abr, abi], axis=2)[::-1]
    bc = ab.reshape(tc, nt, gt, sw, C_GROUP).transpose(1, 0, 2, 4, 3).reshape(nt, tc * gt * C_GROUP, sw)
    cr = c_re[None] * pw_re[1:, :, None, :] - c_im[None] * pw_im[1:, :, None, :]
    ci = -(c_re[None] * pw_im[1:, :, None, :] + c_im[None] * pw_re[1:, :, None, :])
    cc = jnp.concatenate([cr, ci], axis=3)
    cc = cc.reshape(tc, nt, gt, C_GROUP, sw).transpose(1, 2, 4, 0, 3).reshape(nt, gt * sw, tc * C_GROUP)
    r, i = pw_re[tc], pw_im[tc]
    rows = []
    for _ in range(max(n_log, 1)):
        rows += [jnp.concatenate([r, r], -1), jnp.concatenate([-i, i], -1)]
        r, i = r * r - i * i, 2.0 * r * i
    apow = jnp.stack(rows, axis=1)
    apow = apow.reshape(nt, gt, -1, sw).transpose(0, 2, 1, 3).reshape(nt, -1, gt * sw)
    return kc.astype(BF16), bc.astype(BF16), cc.astype(BF16), apow


def _s5(proj2d, ops, x0_re, x0_im, *, l):
    kc, bc, cc, apow = ops
    bsz = x0_re.shape[0]
    nt = C_GROUPS // S5_GROUP_BLOCK
    x0 = jnp.concatenate([x0_re, x0_im], axis=-1).reshape(bsz, nt, 1, -1).transpose(1, 0, 2, 3)
    bb = max(s for s in range(1, bsz + 1) if bsz % s == 0 and s * l <= S5_MAX_ROWS)
    y, xf = _s5_chunks(proj2d, kc, bc, cc, apow, x0, l=l, bb=bb)
    xf = xf.transpose(1, 0, 2, 3).reshape(bsz, C_GROUPS, 2 * C_STATE)
    return y, xf[..., :C_STATE], xf[..., C_STATE:]


AB_SRC_GA = AB_QKV
AB_SRC_QKV = AB_SRC_GA + A_GATE_RANK
AB_SRC_TAIL = AB_SRC_QKV + B_QKV + B_VW
IN_AB = AB_SRC_TAIL + 2 * B_HEADS
IN_CD = CD_SMALL + 2 * D_HEADS
WPREP_AB_TILE = 512
WPREP_CD_TILE = 640


def _prep_w_ab_kernel(wt_ref, ga_ref, o_ref):
    j = pl.program_id(0)
    tile = o_ref.shape[1]
    n_small = A_GATE_RANK + 2 * B_HEADS

    @pl.when(j < AB_SMALL // tile)
    def _():
        o_ref[...] = wt_ref[...].T.astype(BF16)

    @pl.when(j == AB_SMALL // tile)
    def _():
        rows = jnp.concatenate([ga_ref[...], wt_ref[tile - 2 * B_HEADS:tile, :],
                                jnp.zeros((tile - n_small, wt_ref.shape[1]), F32)], axis=0)
        o_ref[...] = rows.T.astype(BF16)

    @pl.when(j > AB_SMALL // tile)
    def _():
        o_ref[...] = jnp.zeros(o_ref.shape, BF16)


def _prep_w_ab(w_t, n_out):
    n_in, d = w_t.shape
    tile = WPREP_AB_TILE
    assert AB_QKV % tile == 0 and AB_SMALL % tile == 0 and n_out % tile == 0 and n_in >= tile

    unit = A_GATE_RANK
    assert tile % unit == 0 and (AB_SRC_QKV - AB_QKV) % unit == 0 and (n_in - tile) % unit == 0

    def src_row(j):
        k = j * (tile // unit)
        k = jnp.where(j < AB_QKV // tile, k,
                      jnp.where(j < AB_SMALL // tile, k + (AB_SRC_QKV - AB_QKV) // unit, (n_in - tile) // unit))
        return unit * k

    return pl.pallas_call(
        _prep_w_ab_kernel,
        out_shape=jax.ShapeDtypeStruct((d, n_out), BF16),
        grid=(n_out // tile,),
        in_specs=[pl.BlockSpec((pl.Element(tile), pl.Element(d)), lambda j: (src_row(j), 0)),
                  pl.BlockSpec((pl.Element(A_GATE_RANK), pl.Element(d)), lambda j: (AB_SRC_GA, 0))],
        out_specs=pl.BlockSpec((d, tile), lambda j: (0, j)),
        compiler_params=pltpu.CompilerParams(
            dimension_semantics=("parallel",),
            vmem_limit_bytes=_vmem_limit(tile * d * 6, 2 * tile * d * 4)),
        name="prep_w_in_ab",
    )(w_t, w_t)


def _prep_w_cd_kernel(wt_ref, o_ref):
    tile = o_ref.shape[1]
    row = pl.program_id(0) * tile + lax.broadcasted_iota(jnp.int32, wt_ref.shape, 0)
    o_ref[...] = jnp.where(row < IN_CD, wt_ref[...], 0.0).T.astype(BF16)


def _prep_w_cd(w_t, n_out):
    n_in, d = w_t.shape
    tile = WPREP_CD_TILE
    assert n_out % tile == 0
    return pl.pallas_call(
        _prep_w_cd_kernel,
        out_shape=jax.ShapeDtypeStruct((d, n_out), BF16),
        grid=(n_out // tile,),
        in_specs=[pl.BlockSpec((tile, d), lambda j: (j, 0))],
        out_specs=pl.BlockSpec((d, tile), lambda j: (0, j)),
        compiler_params=pltpu.CompilerParams(
            dimension_semantics=("parallel",),
            vmem_limit_bytes=_vmem_limit(tile * d * 6, 2 * tile * d * 4)),
        name="prep_w_in_cd",
    )(w_t)


def _lane_row(vals, lane0):
    return jnp.zeros((1, LANES), F32).at[0, lane0:lane0 + vals.shape[0]].set(vals.astype(F32))


def _prepare_weights(norm_g, final_norm_g, w_in_ab, a_gate_w, a_gate_b, a_norm_g, b_conv_w, b_a_log,
                     b_dt_bias, b_norm_g, w_out_ab, w_in_cd, c_lam_re, c_lam_im, c_log_dt, c_b_re,
                     c_b_im, c_c_re, c_c_im, c_d, c_glu_w, c_glu_b, d_i_bias, d_f_bias, d_norm_g,
                     w_out_cd, n_log):
    assert w_in_ab.shape[1] == IN_AB and w_in_cd.shape[1] == IN_CD
    w_ab = _prep_w_ab(w_in_ab.astype(F32).T, _round_up(AB_SMALL + LANES, PROJ_TN))
    w_cd = _prep_w_cd(w_in_cd.astype(F32).T, _round_up(CD_SMALL + LANES, PROJ_TN))
    gate_w = jnp.zeros((LANES, A_KW), F32).at[AB_GA_LANE:AB_GA_LANE + A_GATE_RANK].set(
        a_gate_w.astype(F32)).astype(BF16)
    return dict(
        norm_g=norm_g.astype(F32), final_g=final_norm_g.astype(F32)[None, :],
        w_ab=w_ab, w_cd=w_cd, gate_w=gate_w, gate_b=a_gate_b.astype(F32)[None, :],
        a_norm_g=a_norm_g.astype(F32)[None, :], conv_w=b_conv_w.astype(F32),
        alog=_lane_row(b_a_log, AB_APRE_LANE), dtb=_lane_row(b_dt_bias, AB_APRE_LANE),
        b_norm_g=b_norm_g.astype(F32)[None, :],
        w_out_a=w_out_ab[:A_VW].astype(BF16), w_out_b=w_out_ab[A_VW:].astype(BF16),
        s5_ops=_s5_operators(c_lam_re.astype(F32), c_lam_im.astype(F32), c_log_dt, c_b_re.astype(F32),
                             c_b_im.astype(F32), c_c_re.astype(F32), c_c_im.astype(F32), n_log),
        c_d=c_d.astype(F32).reshape(1, C_W), glu_w=c_glu_w.astype(BF16),
        glu_b=c_glu_b.astype(F32)[None, :],
        ib=_lane_row(d_i_bias, CD_I_LANE), fb=_lane_row(d_f_bias, CD_F_LANE),
        d_norm_g=d_norm_g.astype(F32)[None, :],
        w_out_c=w_out_cd[:C_W].astype(BF16), w_out_d=w_out_cd[C_W:].astype(BF16),
    )


def _trunk(x, conv_prev, s_gla0, s_gdn0, s5_re0, s5_im0, mc0, mn0, mm0, w):
    bsz, l, d = x.shape
    c = min(CHUNK, l)
    bb = max(s for s in (1, 2, 4) if s <= MIXER_STREAMS and bsz % s == 0)
    tb_s = min(l, (8 // bb) * c)
    n_s5 = l // S5_CHUNK
    assert l % tb_s == 0 and l % S5_CHUNK == 0 and n_s5 & (n_s5 - 1) == 0
    x2d = x.reshape(bsz * l, d)

    proj = _norm_matmul(x2d, w['norm_g'][0:1], w['w_ab'])
    proj3 = proj.reshape(bsz, l, proj.shape[1])
    o_a, s_gla = _gla(proj3, w['gate_w'], w['gate_b'], w['a_norm_g'], s_gla0.astype(F32), c=c, tb=tb_s, bb=bb)
    o_b, s_gdn = _gdn(proj3, conv_prev.astype(F32), w['conv_w'], w['alog'], w['dtb'], w['b_norm_g'],
                      s_gdn0.astype(F32), c=c, tb=tb_s, bb=bb)
    conv_new = proj3[:, l - (B_CONV - 1):, AB_QKV:AB_QKV + B_QKV]
    h1 = _out_proj(o_a.reshape(bsz * l, A_VW), o_b.reshape(bsz * l, B_VW), w['w_out_a'], w['w_out_b'], x2d)

    proj = _norm_matmul(h1, w['norm_g'][1:2], w['w_cd'])
    proj3 = proj.reshape(bsz, l, proj.shape[1])
    y, s5_re, s5_im = _s5(proj, w['s5_ops'], s5_re0.astype(F32), s5_im0.astype(F32), l=l)
    o_d, mc, mn, mm = _mlstm(proj3, w['ib'], w['fb'], w['d_norm_g'], mc0.astype(F32),
                             mn0.astype(F32), mm0.astype(F32)[:, None, :], c=c, tb=tb_s, bb=bb)
    y_out = _glu_out_proj_norm(y, proj, o_d.reshape(bsz * l, D_VW), w['c_d'], w['glu_w'], w['glu_b'],
                               w['w_out_c'], w['w_out_d'], h1, w['final_g'])
    dt = x.dtype
    return (y_out.reshape(bsz, l, d).astype(dt), conv_new.astype(dt), s_gla.astype(dt), s_gdn.astype(dt),
            s5_re.astype(dt), s5_im.astype(dt), mc.astype(dt), mn.astype(dt), mm[:, 0, :].astype(dt))


def kernel(x_prompt, x_sample, cache_gdn_conv, state_gla, state_gdn, state_s5_re, state_s5_im,
           state_mlstm_c, state_mlstm_n, state_mlstm_m, norm_g, final_norm_g, w_in_ab, a_gate_w,
           a_gate_b, a_norm_g, b_conv_w, b_a_log, b_dt_bias, b_norm_g, w_out_ab, w_in_cd, c_lam_re,
           c_lam_im, c_log_dt, c_b_re, c_b_im, c_c_re, c_c_im, c_d, c_glu_w, c_glu_b, d_i_bias,
           d_f_bias, d_norm_g, w_out_cd):
    n_log = int(math.log2(max(x_prompt.shape[1], x_sample.shape[1]) // S5_CHUNK))
    w = _prepare_weights(norm_g, final_norm_g, w_in_ab, a_gate_w, a_gate_b, a_norm_g, b_conv_w, b_a_log,
                         b_dt_bias, b_norm_g, w_out_ab, w_in_cd, c_lam_re, c_lam_im, c_log_dt, c_b_re,
                         c_b_im, c_c_re, c_c_im, c_d, c_glu_w, c_glu_b, d_i_bias, d_f_bias, d_norm_g,
                         w_out_cd, n_log)
    nb = x_prompt.shape[0]
    zeros = lambda *shape: jnp.zeros(shape, F32)
    p_out = _trunk(x_prompt, zeros(nb, B_CONV - 1, B_QKV), zeros(nb, A_HEADS, A_DK, A_DV),
                   zeros(nb, B_HEADS, B_DK, B_DV), zeros(nb, C_GROUPS, C_STATE), zeros(nb, C_GROUPS, C_STATE),
                   zeros(nb, D_HEADS, D_DK, D_DV), zeros(nb, D_HEADS, D_DK), zeros(nb, D_HEADS), w)
    s_out = _trunk(x_sample, cache_gdn_conv, state_gla, state_gdn, state_s5_re, state_s5_im,
                   state_mlstm_c, state_mlstm_n, state_mlstm_m, w)
    return (p_out[0], s_out[0]) + tuple(p_out[1:]) + tuple(s_out[1:])
```

```python
import functools
import math

import jax
import jax.numpy as jnp
from jax import lax
from jax.experimental import pallas as pl
from jax.experimental.pallas import tpu as pltpu

F32 = jnp.float32
BF16 = jnp.bfloat16
HIGHEST = lax.Precision.HIGHEST

NORM_EPS = 1e-6
CHUNK = 64
A_HEADS, A_DK, A_DV, A_GATE_RANK, A_GATE_TAU = 4, 128, 256, 16, 16.0
B_HEADS, B_DK, B_DV, B_CONV = 8, 128, 128, 4
C_GROUP, C_GROUPS, C_STATE = 16, 64, 64
D_HEADS, D_DK, D_DV = 4, 128, 256
A_KW, A_VW = A_HEADS * A_DK, A_HEADS * A_DV
B_KW, B_VW = B_HEADS * B_DK, B_HEADS * B_DV
B_QKV = 2 * B_KW + B_VW
C_W = C_GROUPS * C_GROUP
D_KW, D_VW = D_HEADS * D_DK, D_HEADS * D_DV

LANES = 128
SUBLANES = 8
VMEM_BYTES_V7X = 64 * 1024 * 1024

AB_Q, AB_K, AB_V, AB_Z = 0, A_KW, 2 * A_KW, 2 * A_KW + A_VW
AB_QKV = AB_Z + A_VW
AB_ZB = AB_QKV + B_QKV
AB_SMALL = AB_ZB + B_VW
AB_GA_LANE, AB_BETA_LANE, AB_APRE_LANE = 0, A_GATE_RANK, A_GATE_RANK + B_HEADS
CD_U, CD_Z = 0, C_W
CD_Q = 2 * C_W
CD_K = CD_Q + D_KW
CD_V = CD_K + D_KW
CD_O = CD_V + D_VW
CD_ZD = CD_O + D_VW
CD_SMALL = CD_ZD + D_VW
CD_I_LANE, CD_F_LANE = 0, D_HEADS

PROJ_TN = 1280
S5_CHUNK = 16
S5_GROUP_BLOCK = 8
S5_MAX_ROWS = 8192
NORM_ROW_SPLITS = 4
GLU_ROW_SPLITS = 2
MIXER_STREAMS = 4
MLSTM_INTRA_CHUNKS = 4
GDN_WY_CHAINS = 32


def _round_up(x, m):
    return (x + m - 1) // m * m


def _vmem_limit(block_bytes, scratch_bytes=0):
    est = 2 * block_bytes + scratch_bytes
    return int(min(max(2 * est, 32 * 1024 * 1024), VMEM_BYTES_V7X - 8 * 1024 * 1024))


def _mm(a, b):
    return jnp.dot(a, b, preferred_element_type=F32)


def _dot(a, b):
    return _mm(a.astype(BF16), b.astype(BF16))


def _dot_nt(a, b):
    return lax.dot_general(a.astype(BF16), b.astype(BF16), (((1,), (1,)), ((), ())),
                           preferred_element_type=F32)


def _dot_tn(a, b):
    return lax.dot_general(a.astype(BF16), b.astype(BF16), (((0,), (0,)), ((), ())),
                           preferred_element_type=F32)


def _split2(x):
    hi = x.astype(BF16)
    return hi, (x - hi.astype(F32)).astype(BF16)


def _split3(x):
    hi = x.astype(BF16)
    r = x - hi.astype(F32)
    mid = r.astype(BF16)
    return hi, mid, (r - mid.astype(F32)).astype(BF16)


def _cumsum_rows(tri_bf16, x):
    hi, mid, lo = _split3(x)
    return _mm(tri_bf16, hi) + _mm(tri_bf16, mid) + _mm(tri_bf16, lo)


def _select_rows(sel_bf16, x):
    nt = lambda b: lax.dot_general(sel_bf16, b, (((1,), (1,)), ((), ())), preferred_element_type=F32)
    hi, mid, lo = _split3(x)
    return nt(hi) + nt(mid) + nt(lo)


def _lane_selector(lane0):
    r = lax.broadcasted_iota(jnp.int32, (SUBLANES, LANES), 0)
    l = lax.broadcasted_iota(jnp.int32, (SUBLANES, LANES), 1)
    return jnp.where(l == r + lane0, 1.0, 0.0).astype(BF16)


def _causal_masks(c):
    row = lax.broadcasted_iota(jnp.int32, (c, c), 0)
    col = lax.broadcasted_iota(jnp.int32, (c, c), 1)
    return row >= col, row > col


def _norm_matmul_kernel(x_ref, g_ref, w_ref, o_ref, xn_ref):
    j = pl.program_id(1)

    @pl.when(j == 0)
    def _():
        tm = x_ref.shape[0]
        sub = tm // NORM_ROW_SPLITS
        for i in range(NORM_ROW_SPLITS):
            s = slice(i * sub, (i + 1) * sub)
            x = x_ref[s, :]
            y = (x * lax.rsqrt(jnp.mean(x * x, axis=-1, keepdims=True) + NORM_EPS) * g_ref[...]).astype(BF16)
            xn_ref[s, :] = y
            o_ref[s, :] = jnp.dot(y, w_ref[...], preferred_element_type=F32)

    @pl.when(j != 0)
    def _():
        o_ref[...] = jnp.dot(xn_ref[...], w_ref[...], preferred_element_type=F32)


def _norm_matmul(x2d, g_row, w_bf16):
    m, d = x2d.shape
    n = w_bf16.shape[1]
    tm = min(m, 1024)
    tn = PROJ_TN
    assert m % tm == 0 and n % tn == 0
    blocks = tm * d * 4 + d * tn * 2 + tm * tn * 4
    return pl.pallas_call(
        _norm_matmul_kernel,
        out_shape=jax.ShapeDtypeStruct((m, n), F32),
        grid=(m // tm, n // tn),
        in_specs=[pl.BlockSpec((tm, d), lambda i, j: (i, 0)),
                  pl.BlockSpec((1, d), lambda i, j: (0, 0)),
                  pl.BlockSpec((d, tn), lambda i, j: (0, j))],
        out_specs=pl.BlockSpec((tm, tn), lambda i, j: (i, j)),
        scratch_shapes=[pltpu.VMEM((tm, d), BF16)],
        compiler_params=pltpu.CompilerParams(
            dimension_semantics=("parallel", "arbitrary"),
            vmem_limit_bytes=_vmem_limit(blocks, tm * d * 2)),
        name="norm_in_proj",
    )(x2d, g_row, w_bf16)


def _out_proj_kernel(a_ref, b_ref, wa_ref, wb_ref, h_ref, o_ref):
    out = (jnp.dot(a_ref[...], wa_ref[...], preferred_element_type=F32)
           + jnp.dot(b_ref[...], wb_ref[...], preferred_element_type=F32))
    o_ref[...] = h_ref[...] + out


def _glu_out_proj_norm_kernel(y_ref, u_ref, z_ref, od_ref, d_ref, gw_ref, gb_ref, wc_ref, wd_ref, h_ref, g_ref,
                              o_ref):
    tm = y_ref.shape[0]
    sub = tm // GLU_ROW_SPLITS
    halves = [slice(i * sub, (i + 1) * sub) for i in range(GLU_ROW_SPLITS)]
    y = [jax.nn.gelu(y_ref[s, :] + d_ref[...] * u_ref[s, :]) for s in halves]
    gate = [jax.nn.sigmoid(jnp.dot(v.astype(BF16), gw_ref[...], preferred_element_type=F32) + gb_ref[...])
            for v in y]
    o_c = [(y[i] * gate[i] * jax.nn.silu(z_ref[s, :])).astype(BF16) for i, s in enumerate(halves)]
    out = [jnp.dot(o_c[i], wc_ref[...], preferred_element_type=F32)
           + jnp.dot(od_ref[s, :], wd_ref[...], preferred_element_type=F32) for i, s in enumerate(halves)]
    for i, s in enumerate(halves):
        h = h_ref[s, :] + out[i]
        o_ref[s, :] = h * lax.rsqrt(jnp.mean(h * h, axis=-1, keepdims=True) + NORM_EPS) * g_ref[...]


def _glu_out_proj_norm(y2d, proj2d, o_d, d_row, glu_w, glu_b_row, w_out, h2d, final_g_row):
    m, d = h2d.shape
    tm = min(m, 512)
    assert m % tm == 0 and C_W == D_VW and w_out.shape[0] == C_W + D_VW
    row = lambda col: (lambda i: (i, col))
    const = lambda i: (0, 0)
    resident = pl.Buffered(1)
    blocks = 3 * tm * C_W * 4 + tm * D_VW * 2 + 2 * tm * d * 4
    weights = C_W * C_W * 2 + (C_W + D_VW) * d * 2
    return pl.pallas_call(
        _glu_out_proj_norm_kernel,
        out_shape=jax.ShapeDtypeStruct((m, d), F32),
        grid=(m // tm,),
        in_specs=[pl.BlockSpec((tm, C_W), row(0)),
                  pl.BlockSpec((tm, C_W), row(CD_U // C_W)),
                  pl.BlockSpec((tm, C_W), row(CD_Z // C_W)),
                  pl.BlockSpec((tm, D_VW), row(0)),
                  pl.BlockSpec((1, C_W), const),
                  pl.BlockSpec((C_W, C_W), const, pipeline_mode=resident),
                  pl.BlockSpec((1, C_W), const),
                  pl.BlockSpec((C_W, d), const, pipeline_mode=resident),
                  pl.BlockSpec((D_VW, d), lambda i: (1, 0), pipeline_mode=resident),
                  pl.BlockSpec((tm, d), row(0)),
                  pl.BlockSpec((1, d), const)],
        out_specs=pl.BlockSpec((tm, d), row(0)),
        compiler_params=pltpu.CompilerParams(
            dimension_semantics=("parallel",),
            vmem_limit_bytes=_vmem_limit(blocks, weights + 3 * tm * C_W * 4)),
        name="glu_out_proj_norm",
    )(y2d, proj2d, proj2d, o_d, d_row, glu_w, glu_b_row, w_out, w_out, h2d, final_g_row)


def _out_proj(mix_a, mix_b, w_out, h2d):
    m, d = h2d.shape
    ka, kb = mix_a.shape[1], mix_b.shape[1]
    tm = min(m, 512)
    assert m % tm == 0 and ka == kb and w_out.shape[0] == ka + kb
    blocks = tm * (ka + kb) * 2 + (ka + kb) * d * 2 + 2 * tm * d * 4
    return pl.pallas_call(
        _out_proj_kernel,
        out_shape=jax.ShapeDtypeStruct((m, d), F32),
        grid=(m // tm,),
        in_specs=[pl.BlockSpec((tm, ka), lambda i: (i, 0)),
                  pl.BlockSpec((tm, kb), lambda i: (i, 0)),
                  pl.BlockSpec((ka, d), lambda i: (0, 0)),
                  pl.BlockSpec((kb, d), lambda i: (1, 0)),
                  pl.BlockSpec((tm, d), lambda i: (i, 0))],
        out_specs=pl.BlockSpec((tm, d), lambda i: (i, 0)),
        compiler_params=pltpu.CompilerParams(
            dimension_semantics=("parallel",),
            vmem_limit_bytes=_vmem_limit(blocks)),
        name="out_proj",
    )(mix_a, mix_b, w_out, w_out, h2d)


def _gla_kernel(q_ref, k_ref, v_ref, z_ref, sm_ref, gw_ref, gb_ref, ng_ref, s0_ref,
                o_ref, sout_ref, st_ref, *, c, nchunks):
    t = pl.program_id(1)
    last_t = pl.num_programs(1) - 1
    bb = q_ref.shape[0]
    chains = [(bi, h) for bi in range(bb) for h in range(A_HEADS)]
    ksl = [slice(h * A_DK, (h + 1) * A_DK) for _, h in chains]
    vsl = [slice(h * A_DV, (h + 1) * A_DV) for _, h in chains]
    idx = range(len(chains))

    @pl.when(t == 0)
    def _():
        for bi, h in chains:
            st_ref[bi, h] = s0_ref[bi, h].T

    causal, _ = _causal_masks(c)
    tri = causal.astype(BF16)
    gw = gw_ref[...]
    gb = gb_ref[...]

    def body(n, carry):
        sl = pl.ds(pl.multiple_of(n * c, c), c)
        b_all = [_cumsum_rows(tri, jax.nn.log_sigmoid(_dot(sm_ref[bi, sl, :], gw) + gb) * (1.0 / A_GATE_TAU))
                 for bi in range(bb)]
        b = [b_all[chains[i][0]][:, ksl[i]] for i in idx]
        b_last = [b[i][c - 1:c, :] for i in idx]
        k = [k_ref[chains[i][0], sl, ksl[i]] for i in idx]
        v = [v_ref[chains[i][0], sl, vsl[i]].astype(BF16) for i in idx]
        q_dec = [(q_ref[chains[i][0], sl, ksl[i]] * (A_DK ** -0.5) * jnp.exp(b[i])).astype(BF16) for i in idx]
        k_dec = [(k[i] * jnp.exp(-b[i])).astype(BF16) for i in idx]
        k_w = [(k[i] * jnp.exp(b_last[i] - b[i])).astype(BF16) for i in idx]
        scores = [jnp.where(causal, _dot_nt(q_dec[i], k_dec[i]), 0.0).astype(BF16) for i in idx]
        s_t = [st_ref[bi, h] for bi, h in chains]
        outs = [_mm(scores[i], v[i]) + _dot_nt(q_dec[i], s_t[i]) for i in idx]
        for i, (bi, h) in enumerate(chains):
            st_ref[bi, h] = s_t[i] * jnp.exp(b_last[i]) + _dot_tn(v[i], k_w[i])
        for i, (bi, h) in enumerate(chains):
            o = outs[i]
            o = o * lax.rsqrt(jnp.mean(o * o, axis=-1, keepdims=True) + NORM_EPS) * ng_ref[:, vsl[i]]
            o_ref[bi, sl, vsl[i]] = (o * jax.nn.silu(z_ref[bi, sl, vsl[i]])).astype(o_ref.dtype)
        return carry

    lax.fori_loop(0, nchunks, body, 0)

    @pl.when(t == last_t)
    def _():
        for bi, h in chains:
            sout_ref[bi, h] = st_ref[bi, h].T


def _gla(proj3, gate_w_pad, gate_b_row, norm_g_row, s0, *, c, tb, bb):
    bsz, l, _ = proj3.shape
    nblk = l // tb
    assert bsz % bb == 0
    tok = lambda col: (lambda b, t: (b, t, col))
    in_specs = [
        pl.BlockSpec((bb, tb, A_KW), tok(AB_Q // A_KW)),
        pl.BlockSpec((bb, tb, A_KW), tok(AB_K // A_KW)),
        pl.BlockSpec((bb, tb, A_VW), tok(AB_V // A_VW)),
        pl.BlockSpec((bb, tb, A_VW), tok(AB_Z // A_VW)),
        pl.BlockSpec((bb, tb, LANES), tok(AB_SMALL // LANES)),
        pl.BlockSpec((LANES, A_KW), lambda b, t: (0, 0)),
        pl.BlockSpec((1, A_KW), lambda b, t: (0, 0)),
        pl.BlockSpec((1, A_VW), lambda b, t: (0, 0)),
        pl.BlockSpec((bb, A_HEADS, A_DK, A_DV), lambda b, t: (b, 0, 0, 0)),
    ]
    out_specs = [
        pl.BlockSpec((bb, tb, A_VW), lambda b, t: (b, t, 0)),
        pl.BlockSpec((bb, A_HEADS, A_DK, A_DV), lambda b, t: (b, 0, 0, 0)),
    ]
    state = bb * A_HEADS * A_DK * A_DV * 4
    blocks = bb * tb * (2 * A_KW + 2 * A_VW + LANES) * 4 + bb * tb * A_VW * 2 + 2 * state
    return pl.pallas_call(
        functools.partial(_gla_kernel, c=c, nchunks=tb // c),
        out_shape=[jax.ShapeDtypeStruct((bsz, l, A_VW), BF16),
                   jax.ShapeDtypeStruct((bsz, A_HEADS, A_DK, A_DV), F32)],
        grid=(bsz // bb, nblk),
        in_specs=in_specs,
        out_specs=out_specs,
        scratch_shapes=[pltpu.VMEM((bb, A_HEADS, A_DV, A_DK), F32)],
        compiler_params=pltpu.CompilerParams(
            dimension_semantics=("parallel", "arbitrary"),
            vmem_limit_bytes=_vmem_limit(blocks, state)),
        name="gla_mixer",
    )(proj3, proj3, proj3, proj3, proj3, gate_w_pad, gate_b_row, norm_g_row, s0)


def _gdn_kernel(x_ref, z_ref, sm_ref, w_ref, cp_ref, alog_ref, dtb_ref, ng_ref, s0_ref,
                o_ref, sout_ref, s_ref, tail_ref, u_ref, wm_ref, qg_ref, kg_ref, qk_ref, gl_ref,
                *, c, nchunks):
    t = pl.program_id(1)
    last_t = pl.num_programs(1) - 1
    tb = c * nchunks
    keep = SUBLANES - (B_CONV - 1)
    bb = x_ref.shape[0]

    @pl.when(t == 0)
    def _():
        s_ref[...] = s0_ref[...]
        tail_ref[:, 0:keep, :] = jnp.zeros((bb, keep, B_QKV), F32)
        tail_ref[:, keep:SUBLANES, :] = cp_ref[...]

    def conv_silu(bi, row0, first, cols):
        x = x_ref[bi, pl.ds(row0, c), cols]
        if first is None:
            prev = x_ref[bi, pl.ds(pl.multiple_of(row0 - SUBLANES, SUBLANES), SUBLANES), cols]
        else:
            before = pl.multiple_of(jnp.maximum(row0 - SUBLANES, 0), SUBLANES)
            prev = jnp.where(first, tail_ref[bi, :, cols], x_ref[bi, pl.ds(before, SUBLANES), cols])
        w = w_ref[:, cols]
        ext = jnp.concatenate([prev, x], axis=0)
        ext1 = pltpu.roll(ext, 1, 0)
        newer = ext * w[3:4, :] + ext1 * w[2:3, :]
        older = ext * w[1:2, :] + ext1 * w[0:1, :]
        conv = newer + pltpu.roll(older, 2, 0)
        return jax.nn.silu(conv[SUBLANES:SUBLANES + c, :])

    causal, strict = _causal_masks(c)
    tri = causal.astype(BF16)
    eye = jnp.where(causal & jnp.logical_not(strict), 1.0, 0.0).astype(F32)
    sel = _lane_selector(AB_APRE_LANE)
    neg_a_exp = -jnp.exp(alog_ref[...])
    dtb = dtb_ref[...]
    ng = ng_ref[...]
    n_double = int(math.log2(c)) - 1
    heads = range(B_HEADS)

    group = max(1, GDN_WY_CHAINS // (bb * B_HEADS))
    group = group if nchunks % group == 0 else 1

    def wy_factors(n, carry):
        pws, rhss, where = [], [], []
        for bi, ci in [(bi, ci) for ci in range(group) for bi in range(bb)]:
            row0 = pl.multiple_of((n * group + ci) * c, c)
            first = (n == 0) if ci == 0 else None
            sl = pl.ds(row0, c)
            sm = sm_ref[bi, sl, :]
            g_cum = _cumsum_rows(tri, neg_a_exp * jax.nn.softplus(sm + dtb))
            g_rows = _select_rows(sel, g_cum)
            beta_all = jax.nn.sigmoid(sm)
            gl_ref[bi, pl.ds(n * group + ci, 1), :] = g_cum[c - 1:c, :]
            for h in heads:
                hs = slice(h * B_DK, (h + 1) * B_DK)
                q = conv_silu(bi, row0, first, hs)
                k = conv_silu(bi, row0, first, slice(B_KW + h * B_DK, B_KW + (h + 1) * B_DK))
                v = conv_silu(bi, row0, first, slice(2 * B_KW + h * B_DV, 2 * B_KW + (h + 1) * B_DV))
                q = q * lax.rsqrt(jnp.sum(q * q, axis=-1, keepdims=True) + NORM_EPS) * (B_DK ** -0.5)
                k = k * lax.rsqrt(jnp.sum(k * k, axis=-1, keepdims=True) + NORM_EPS)
                g_col = g_cum[:, AB_APRE_LANE + h:AB_APRE_LANE + h + 1]
                beta = beta_all[:, AB_BETA_LANE + h:AB_BETA_LANE + h + 1]
                decay = jnp.exp(jnp.where(causal, g_col - g_rows[h:h + 1, :], -jnp.inf))
                e_g = jnp.exp(g_col)
                k_beta = k * beta
                pws.append(-jnp.where(strict, _dot_nt(k_beta, k) * decay, 0.0))
                rhss.append(jnp.concatenate([v * beta, k_beta * e_g], axis=1))
                where.append((bi, sl, hs))
                qk_ref[bi, sl, h * LANES:h * LANES + c] = jnp.where(causal, _dot_nt(q, k) * decay,
                                                                    0.0).astype(BF16)
                qg_ref[bi, sl, hs] = (q * e_g).astype(BF16)
                kg_ref[bi, sl, hs] = (k * jnp.exp(g_col[c - 1:c, :] - g_col)).astype(BF16)
        items = range(len(pws))
        neg_lower = [_split2(p) for p in pws]
        invs = [eye + p for p in pws]
        for _ in range(n_double):
            pws = [_dot(p, p) for p in pws]
            invs = [i + _dot(i, p) for i, p in zip(invs, pws)]
        invs = [i.astype(BF16) for i in invs]
        sol = [_mm(invs[i], rhss[i].astype(BF16)).astype(BF16) for i in items]
        resid = [rhss[i] - sol[i].astype(F32) + _mm(neg_lower[i][0], sol[i]) + _mm(neg_lower[i][1], sol[i])
                 for i in items]
        for i in items:
            bi, sl, hs = where[i]
            uw = sol[i].astype(F32) + _mm(invs[i], resid[i].astype(BF16))
            u_ref[bi, sl, hs] = uw[:, :B_DV]
            wm_ref[bi, sl, hs] = uw[:, B_DV:].astype(BF16)
        return carry

    lax.fori_loop(0, nchunks // group, wy_factors, 0)
    tail_ref[...] = x_ref[:, tb - SUBLANES:tb, :]

    chains = [(bi, h) for bi in range(bb) for h in heads]
    hsl = [slice(h * B_DK, (h + 1) * B_DK) for _, h in chains]

    def recurrence(n, carry):
        sl = pl.ds(pl.multiple_of(n * c, c), c)
        e_last = [jnp.exp(gl_ref[bi, pl.ds(n, 1), :]) for bi in range(bb)]
        s_old = [s_ref[bi, h] for bi, h in chains]
        s_bf = [s.astype(BF16) for s in s_old]
        v_new = [(u_ref[bi, sl, hsl[i]] - _mm(wm_ref[bi, sl, hsl[i]], s_bf[i])).astype(BF16)
                 for i, (bi, h) in enumerate(chains)]
        outs = [_mm(qg_ref[bi, sl, hsl[i]], s_bf[i]) + _mm(qk_ref[bi, sl, h * LANES:h * LANES + c], v_new[i])
                for i, (bi, h) in enumerate(chains)]
        for i, (bi, h) in enumerate(chains):
            s_ref[bi, h] = (e_last[bi][:, AB_APRE_LANE + h:AB_APRE_LANE + h + 1] * s_old[i]
                            + lax.dot_general(kg_ref[bi, sl, hsl[i]], v_new[i], (((0,), (0,)), ((), ())),
                                              preferred_element_type=F32))
        for i, (bi, h) in enumerate(chains):
            o = outs[i]
            o = o * lax.rsqrt(jnp.mean(o * o, axis=-1, keepdims=True) + NORM_EPS) * ng
            o_ref[bi, sl, hsl[i]] = (o * jax.nn.silu(z_ref[bi, sl, hsl[i]])).astype(o_ref.dtype)
        return carry

    lax.fori_loop(0, nchunks, recurrence, 0)

    @pl.when(t == last_t)
    def _():
        sout_ref[...] = s_ref[...]


def _gdn(proj3, conv_prev, conv_w, alog_row, dtb_row, norm_g_row, s0, *, c, tb, bb):
    bsz, l, _ = proj3.shape
    nblk = l // tb
    assert AB_QKV % B_QKV == 0 and AB_ZB % B_VW == 0 and bsz % bb == 0
    in_specs = [
        pl.BlockSpec((bb, tb, B_QKV), lambda b, t: (b, t, AB_QKV // B_QKV)),
        pl.BlockSpec((bb, tb, B_VW), lambda b, t: (b, t, AB_ZB // B_VW)),
        pl.BlockSpec((bb, tb, LANES), lambda b, t: (b, t, AB_SMALL // LANES)),
        pl.BlockSpec((B_CONV, B_QKV), lambda b, t: (0, 0)),
        pl.BlockSpec((bb, B_CONV - 1, B_QKV), lambda b, t: (b, 0, 0)),
        pl.BlockSpec((1, LANES), lambda b, t: (0, 0)),
        pl.BlockSpec((1, LANES), lambda b, t: (0, 0)),
        pl.BlockSpec((1, B_DV), lambda b, t: (0, 0)),
        pl.BlockSpec((bb, B_HEADS, B_DK, B_DV), lambda b, t: (b, 0, 0, 0)),
    ]
    out_specs = [
        pl.BlockSpec((bb, tb, B_VW), lambda b, t: (b, t, 0)),
        pl.BlockSpec((bb, B_HEADS, B_DK, B_DV), lambda b, t: (b, 0, 0, 0)),
    ]
    blocks = bb * (tb * (B_QKV + B_VW + LANES) * 4 + tb * B_VW * 2 + 2 * B_HEADS * B_DK * B_DV * 4)
    scratch = bb * (B_HEADS * B_DK * B_DV * 4 + SUBLANES * B_QKV * 4 + tb * B_VW * 4
                    + 4 * tb * B_KW * 2 + SUBLANES * LANES * 4)
    return pl.pallas_call(
        functools.partial(_gdn_kernel, c=c, nchunks=tb // c),
        out_shape=[jax.ShapeDtypeStruct((bsz, l, B_VW), BF16),
                   jax.ShapeDtypeStruct((bsz, B_HEADS, B_DK, B_DV), F32)],
        grid=(bsz // bb, nblk),
        in_specs=in_specs,
        out_specs=out_specs,
        scratch_shapes=[pltpu.VMEM((bb, B_HEADS, B_DK, B_DV), F32),
                        pltpu.VMEM((bb, SUBLANES, B_QKV), F32),
                        pltpu.VMEM((bb, tb, B_VW), F32),
                        pltpu.VMEM((bb, tb, B_KW), BF16),
                        pltpu.VMEM((bb, tb, B_KW), BF16),
                        pltpu.VMEM((bb, tb, B_KW), BF16),
                        pltpu.VMEM((bb, tb, B_HEADS * LANES), BF16),
                        pltpu.VMEM((bb, SUBLANES, LANES), F32)],
        compiler_params=pltpu.CompilerParams(
            dimension_semantics=("parallel", "arbitrary"),
            vmem_limit_bytes=_vmem_limit(blocks, scratch)),
        name="gdn_mixer",
    )(proj3, proj3, proj3, conv_w, conv_prev, alog_row, dtb_row, norm_g_row, s0)


def _mlstm_kernel(q_ref, k_ref, v_ref, og_ref, z_ref, sm_ref, ib_ref, fb_ref, ng_ref,
                  c0_ref, n0_ref, m0_ref, o_ref, cout_ref, nout_ref, mout_ref,
                  c_ref, n_ref, m_ref, hi_ref, mi_ref, ni_ref, bc_ref, kv_ref, ks_ref, mc_ref, bl_ref,
                  *, c, nchunks):
    t = pl.program_id(1)
    last_t = pl.num_programs(1) - 1
    bb = q_ref.shape[0]
    chains = [(bi, h) for bi in range(bb) for h in range(D_HEADS)]
    ksl = [slice(h * D_DK, (h + 1) * D_DK) for _, h in chains]
    vsl = [slice(h * D_DV, (h + 1) * D_DV) for _, h in chains]
    idx = range(len(chains))

    @pl.when(t == 0)
    def _():
        c_ref[...] = c0_ref[...]
        n_ref[...] = n0_ref[...]
        for bi, h in chains:
            m_ref[bi, h] = jnp.broadcast_to(m0_ref[bi, :, h:h + 1], (1, LANES))

    causal, _ = _causal_masks(c)
    tri = causal.astype(BF16)
    sel = _lane_selector(0)
    lane = lax.broadcasted_iota(jnp.int32, (c, LANES), 1)
    ib = ib_ref[...]
    fb = fb_ref[...]

    group = MLSTM_INTRA_CHUNKS if nchunks % MLSTM_INTRA_CHUNKS == 0 else 1

    def intra(n, carry):
        items = [(bi, ci, h) for ci in range(group) for bi, h in chains]
        sls = [pl.ds(pl.multiple_of((n * group + ci) * c, c), c) for ci in range(group)]
        sm = {(bi, ci): sm_ref[bi, sls[ci], :] for ci in range(group) for bi in range(bb)}
        i_full = {key: x + ib for key, x in sm.items()}
        b_full = {key: _cumsum_rows(tri, jax.nn.log_sigmoid(x + fb)) for key, x in sm.items()}
        rows = {key: _select_rows(sel, jnp.where(lane < CD_F_LANE, i_full[key], b_full[key]))
                for key in sm}
        b_col = [b_full[bi, ci][:, CD_F_LANE + h:CD_F_LANE + h + 1] for bi, ci, h in items]
        i_col = [i_full[bi, ci][:, CD_I_LANE + h:CD_I_LANE + h + 1] for bi, ci, h in items]
        logw = [jnp.where(causal, b_col[i] - rows[bi, ci][CD_F_LANE + h:CD_F_LANE + h + 1, :]
                          + rows[bi, ci][CD_I_LANE + h:CD_I_LANE + h + 1, :], -jnp.inf)
                for i, (bi, ci, h) in enumerate(items)]
        ids = range(len(items))
        m_intra = [jnp.max(logw[i], axis=-1, keepdims=True) for i in ids]
        ks_ = [slice(h * D_DK, (h + 1) * D_DK) for _, _, h in items]
        vs_ = [slice(h * D_DV, (h + 1) * D_DV) for _, _, h in items]
        q_bf = [(q_ref[bi, sls[ci], ks_[i]] * (D_DK ** -0.5)).astype(BF16) for i, (bi, ci, h) in enumerate(items)]
        k = [k_ref[bi, sls[ci], ks_[i]] for i, (bi, ci, h) in enumerate(items)]
        v = [v_ref[bi, sls[ci], vs_[i]].astype(BF16) for i, (bi, ci, h) in enumerate(items)]
        p = [jnp.exp(logw[i] - m_intra[i]) * _dot_nt(q_bf[i], k[i]) for i in ids]
        n_intra = [jnp.sum(p[i], axis=-1, keepdims=True) for i in ids]
        m_chunk = [m_intra[i][c - 1:c, :] for i in ids]
        k_w = [k[i] * jnp.exp(b_col[i][c - 1:c, :] - b_col[i] + i_col[i] - m_chunk[i]) for i in ids]
        rep = lambda x: jnp.broadcast_to(x, (x.shape[0], LANES))
        for i, (bi, ci, h) in enumerate(items):
            row = pl.ds(n * group + ci, 1)
            hi_ref[bi, sls[ci], vs_[i]] = _dot(p[i], v[i])
            kv_ref[bi, n * group + ci, h] = _dot_tn(k_w[i], v[i])
            ks_ref[bi, h, row, :] = jnp.sum(k_w[i], axis=0, keepdims=True)
            mi_ref[bi, h, sls[ci], :] = rep(m_intra[i])
            ni_ref[bi, h, sls[ci], :] = rep(n_intra[i])
            bc_ref[bi, h, sls[ci], :] = rep(b_col[i])
            mc_ref[bi, h, row, :] = rep(m_chunk[i])
            bl_ref[bi, h, row, :] = rep(b_col[i][c - 1:c, :])
        return carry

    lax.fori_loop(0, nchunks // group, intra, 0)

    twice = lambda x: jnp.concatenate([x, x], axis=1)

    def body(n, carry):
        sl = pl.ds(pl.multiple_of(n * c, c), c)
        row = pl.ds(n, 1)
        q = [q_ref[chains[i][0], sl, ksl[i]] * (D_DK ** -0.5) for i in idx]
        c_s = [c_ref[bi, h] for bi, h in chains]
        n_s = [n_ref[bi, h:h + 1, :] for bi, h in chains]
        m_s = [m_ref[bi, h] for bi, h in chains]
        qc = [_dot(q[i], c_s[i]) for i in idx]
        b_last = [bl_ref[bi, h, row, :] for bi, h in chains]
        m_chunk = [mc_ref[bi, h, row, :] for bi, h in chains]
        for i, (bi, h) in enumerate(chains):
            m_new = jnp.maximum(b_last[i] + m_s[i], m_chunk[i])
            w_old = jnp.exp(b_last[i] + m_s[i] - m_new)
            w_new = jnp.exp(m_chunk[i] - m_new)
            c_ref[bi, h] = twice(w_old) * c_s[i] + twice(w_new) * kv_ref[bi, n, h]
            n_ref[bi, h:h + 1, :] = w_old * n_s[i] + w_new * ks_ref[bi, h, row, :]
            m_ref[bi, h] = m_new
        for i, (bi, h) in enumerate(chains):
            m_intra = mi_ref[bi, h, sl, :]
            a = bc_ref[bi, h, sl, :] + m_s[i]
            m_t = jnp.maximum(a, m_intra)
            w_a = jnp.exp(a - m_t)
            w_i = jnp.exp(m_intra - m_t)
            num = twice(w_a) * qc[i] + twice(w_i) * hi_ref[bi, sl, vsl[i]]
            den = w_a * jnp.sum(q[i] * n_s[i], axis=-1, keepdims=True) + w_i * ni_ref[bi, h, sl, :]
            hh = num / twice(jnp.maximum(jnp.abs(den), jnp.exp(-m_t)))
            hd = jax.nn.sigmoid(og_ref[bi, sl, vsl[i]]) * hh
            oc = hd - jnp.mean(hd, axis=-1, keepdims=True)
            o = oc * lax.rsqrt(jnp.mean(oc * oc, axis=-1, keepdims=True) + NORM_EPS) * ng_ref[:, vsl[i]]
            o_ref[bi, sl, vsl[i]] = (o * jax.nn.silu(z_ref[bi, sl, vsl[i]])).astype(o_ref.dtype)
        return carry

    lax.fori_loop(0, nchunks, body, 0)

    @pl.when(t == last_t)
    def _():
        cout_ref[...] = c_ref[...]
        nout_ref[...] = n_ref[...]
        for bi, h in chains:
            mout_ref[bi, :, h:h + 1] = m_ref[bi, h][:, 0:1]


def _mlstm(proj3, ib_row, fb_row, norm_g_row, c0, n0, m0, *, c, tb, bb):
    bsz, l, _ = proj3.shape
    nblk = l // tb
    assert bsz % bb == 0
    tok = lambda col: (lambda b, t: (b, t, col))
    st4 = lambda b, t: (b, 0, 0, 0)
    st3 = lambda b, t: (b, 0, 0)
    in_specs = [
        pl.BlockSpec((bb, tb, D_KW), tok(CD_Q // D_KW)),
        pl.BlockSpec((bb, tb, D_KW), tok(CD_K // D_KW)),
        pl.BlockSpec((bb, tb, D_VW), tok(CD_V // D_VW)),
        pl.BlockSpec((bb, tb, D_VW), tok(CD_O // D_VW)),
        pl.BlockSpec((bb, tb, D_VW), tok(CD_ZD // D_VW)),
        pl.BlockSpec((bb, tb, LANES), tok(CD_SMALL // LANES)),
        pl.BlockSpec((1, LANES), lambda b, t: (0, 0)),
        pl.BlockSpec((1, LANES), lambda b, t: (0, 0)),
        pl.BlockSpec((1, D_VW), lambda b, t: (0, 0)),
        pl.BlockSpec((bb, D_HEADS, D_DK, D_DV), st4),
        pl.BlockSpec((bb, D_HEADS, D_DK), st3),
        pl.BlockSpec((bb, 1, D_HEADS), st3),
    ]
    out_specs = [
        pl.BlockSpec((bb, tb, D_VW), lambda b, t: (b, t, 0)),
        pl.BlockSpec((bb, D_HEADS, D_DK, D_DV), st4),
        pl.BlockSpec((bb, D_HEADS, D_DK), st3),
        pl.BlockSpec((bb, 1, D_HEADS), st3),
    ]
    state = bb * D_HEADS * D_DK * D_DV * 4
    blocks = bb * tb * (2 * D_KW + 3 * D_VW + LANES) * 4 + bb * tb * D_VW * 2 + 2 * state
    nchunks = tb // c
    return pl.pallas_call(
        functools.partial(_mlstm_kernel, c=c, nchunks=nchunks),
        out_shape=[jax.ShapeDtypeStruct((bsz, l, D_VW), BF16),
                   jax.ShapeDtypeStruct((bsz, D_HEADS, D_DK, D_DV), F32),
                   jax.ShapeDtypeStruct((bsz, D_HEADS, D_DK), F32),
                   jax.ShapeDtypeStruct((bsz, 1, D_HEADS), F32)],
        grid=(bsz // bb, nblk),
        in_specs=in_specs,
        out_specs=out_specs,
        scratch_shapes=[pltpu.VMEM((bb, D_HEADS, D_DK, D_DV), F32),
                        pltpu.VMEM((bb, D_HEADS, D_DK), F32),
                        pltpu.VMEM((bb, D_HEADS, 1, LANES), F32),
                        pltpu.VMEM((bb, tb, D_VW), F32),
                        pltpu.VMEM((bb, D_HEADS, tb, LANES), F32),
                        pltpu.VMEM((bb, D_HEADS, tb, LANES), F32),
                        pltpu.VMEM((bb, D_HEADS, tb, LANES), F32),
                        pltpu.VMEM((bb, nchunks, D_HEADS, D_DK, D_DV), F32),
                        pltpu.VMEM((bb, D_HEADS, _round_up(nchunks, SUBLANES), D_DK), F32),
                        pltpu.VMEM((bb, D_HEADS, _round_up(nchunks, SUBLANES), LANES), F32),
                        pltpu.VMEM((bb, D_HEADS, _round_up(nchunks, SUBLANES), LANES), F32)],
        compiler_params=pltpu.CompilerParams(
            dimension_semantics=("parallel", "arbitrary"),
            vmem_limit_bytes=_vmem_limit(blocks, state * (1 + nchunks) + bb * tb * (D_VW + 2 * LANES) * 4)),
        name="mlstm_mixer",
    )(proj3, proj3, proj3, proj3, proj3, proj3, ib_row, fb_row, norm_g_row, c0, n0, m0)


def _s5_expand_operators(kc_ref, bc_ref, cc_ref, bd_ref, bst_ref, cst_ref):
    tc, gt, sw, cg = S5_CHUNK, S5_GROUP_BLOCK, 2 * C_STATE, C_GROUP
    w_t = tc * LANES
    iota = lambda shape, d: lax.broadcasted_iota(jnp.int32, shape, d)
    row_g = (iota((w_t, LANES), 0) // cg) % gt
    tile16 = jnp.where(iota((cg, LANES), 1) % cg == iota((cg, LANES), 0), 1.0, 0.0).astype(BF16)
    bd = jnp.where(row_g == iota((w_t, LANES), 1) // cg, _mm(kc_ref[...], tile16), 0.0).astype(BF16)
    bd_ref[:, LANES:2 * LANES] = bd
    bd_ref[0:w_t - LANES, 0:LANES] = bd[LANES:, :]
    bd_ref[w_t - LANES:w_t, 0:LANES] = jnp.zeros((LANES, LANES), BF16)
    bc = bc_ref[...].astype(F32)
    for g in range(gt):
        bst_ref[:, g * sw:(g + 1) * sw] = jnp.where(row_g == g, bc, 0.0).astype(BF16)
    src, dst = iota((tc * cg, w_t), 0), iota((tc * cg, w_t), 1)
    spread = jnp.where((src // cg == dst // LANES) & (src % cg == dst % cg), 1.0, 0.0).astype(BF16)
    lane_g = (iota((sw, w_t), 1) // cg) % gt
    for g in range(gt):
        full = _mm(cc_ref[g * sw:(g + 1) * sw, :], spread)
        cst_ref[g * sw:(g + 1) * sw, :] = jnp.where(lane_g == g, full, 0.0).astype(BF16)


def _s5_kernel(u_ref, kc_ref, bc_ref, cc_ref, apow_ref, x0_ref, y_ref, xf_ref, bd_ref, bst_ref, cst_ref,
               *, nc, bb):
    m = nc * bb
    tc = S5_CHUNK
    sw = 2 * C_STATE

    @pl.when(pl.program_id(1) == 0)
    def _():
        _s5_expand_operators(kc_ref, bc_ref, cc_ref, bd_ref, bst_ref, cst_ref)

    row = lax.broadcasted_iota(jnp.int32, (m, sw), 0)
    n_idx = row & (nc - 1)
    n_log = int(math.log2(nc))

    def cmul(a1, a2, x):
        return a1 * x + a2 * pltpu.roll(x, C_STATE, 1)

    lhs = jnp.concatenate([u_ref[pl.ds(tau, m, stride=tc), :].astype(BF16) for tau in range(tc)], axis=1)
    e_all = _mm(lhs, bst_ref[...])
    groups = range(S5_GROUP_BLOCK)
    gsl = [slice(g * sw, (g + 1) * sw) for g in groups]
    x0_rows = []
    for g in groups:
        rows0 = jnp.zeros((m, sw), F32)
        for b in range(bb):
            rows0 = jnp.where(row == b * nc, x0_ref[b, :, gsl[g]], rows0)
        x0_rows.append(rows0)
    x = [e_all[:, gsl[g]] + cmul(apow_ref[0:1, gsl[g]], apow_ref[1:2, gsl[g]], x0_rows[g]) for g in groups]
    for j in range(n_log):
        sh = 1 << j
        shifted = [jnp.where(n_idx >= sh, pltpu.roll(x[g], sh, 0), 0.0) for g in groups]
        x = [x[g] + cmul(apow_ref[2 * j:2 * j + 1, gsl[g]], apow_ref[2 * j + 1:2 * j + 2, gsl[g]], shifted[g])
             for g in groups]
    x_start = [jnp.where(n_idx >= 1, pltpu.roll(x[g], 1, 0), x0_rows[g]).astype(BF16) for g in groups]
    for g in groups:
        for b in range(bb):
            xf_ref[b, :, gsl[g]] = x[g][b * nc + nc - 1:b * nc + nc, :]
    y_state = _mm(jnp.concatenate(x_start, axis=1), cst_ref[...])
    for tau in range(0, tc, 2):
        width = (tau + 2) * LANES
        start = (tc - 2 - tau) * LANES
        y = _mm(lhs[:, :width], bd_ref[start:start + width, :]) + y_state[:, tau * LANES:(tau + 2) * LANES]
        y_ref[pl.ds(tau, m, stride=tc), :] = y[:, :LANES]
        y_ref[pl.ds(tau + 1, m, stride=tc), :] = y[:, LANES:]


def _s5_chunks(proj2d, kc, bc, cc, apow, x0, *, l, bb):
    tc = S5_CHUNK
    nc = l // tc
    bsz = x0.shape[1]
    nt = C_GROUPS // S5_GROUP_BLOCK
    rows = bb * l
    sw = 2 * C_STATE
    sw_t = S5_GROUP_BLOCK * sw
    w_t = tc * LANES
    assert CD_U % LANES == 0 and bsz % bb == 0
    blocks = (2 * rows * LANES * 4 + 2 * w_t * LANES * 2 + sw_t * tc * C_GROUP * 2 + apow.shape[1] * sw_t * 4
              + 2 * bb * SUBLANES * sw_t * 4)
    scratch = w_t * 2 * LANES * 2 + 2 * w_t * sw_t * 2
    temps = bb * nc * (w_t * 2 + w_t * 4 + 3 * sw_t * 4) + 4 * sw * w_t * 4
    return pl.pallas_call(
        functools.partial(_s5_kernel, nc=nc, bb=bb),
        out_shape=[jax.ShapeDtypeStruct((bsz * l, C_W), F32),
                   jax.ShapeDtypeStruct((nt, bsz, 1, sw_t), F32)],
        grid=(nt, bsz // bb),
        in_specs=[pl.BlockSpec((rows, LANES), lambda i, j: (j, CD_U // LANES + i)),
                  pl.BlockSpec((None, w_t, C_GROUP), lambda i, j: (i, 0, 0)),
                  pl.BlockSpec((None, w_t, sw), lambda i, j: (i, 0, 0)),
                  pl.BlockSpec((None, sw_t, tc * C_GROUP), lambda i, j: (i, 0, 0)),
                  pl.BlockSpec((None, apow.shape[1], sw_t), lambda i, j: (i, 0, 0)),
                  pl.BlockSpec((None, bb, 1, sw_t), lambda i, j: (i, j, 0, 0))],
        out_specs=[pl.BlockSpec((rows, LANES), lambda i, j: (j, i)),
                   pl.BlockSpec((None, bb, 1, sw_t), lambda i, j: (i, j, 0, 0))],
        scratch_shapes=[pltpu.VMEM((w_t, 2 * LANES), BF16),
                        pltpu.VMEM((w_t, sw_t), BF16),
                        pltpu.VMEM((sw_t, w_t), BF16)],
        compiler_params=pltpu.CompilerParams(
            dimension_semantics=("parallel", "arbitrary"),
            vmem_limit_bytes=_vmem_limit(blocks, scratch + temps)),
        name="s5_chunks",
    )(proj2d, kc, bc, cc, apow, x0)


def _s5_operators(lam_re, lam_im, log_dt, b_re, b_im, c_re, c_im, n_log):
    g, p = lam_re.shape
    tc = S5_CHUNK
    dt = jnp.exp(log_dt.astype(F32))[:, None]
    mag = jnp.exp(lam_re * dt)
    ab_re, ab_im = mag * jnp.cos(lam_im * dt), mag * jnp.sin(lam_im * dt)
    den = lam_re * lam_re + lam_im * lam_im
    er = ab_re - 1.0
    zr = (er * lam_re + ab_im * lam_im) / den
    zi = (ab_im * lam_re - er * lam_im) / den
    bb_re = zr[..., None] * b_re - zi[..., None] * b_im
    bb_im = zr[..., None] * b_im + zi[..., None] * b_re
    pw_re, pw_im = ab_re[None], ab_im[None]
    while pw_re.shape[0] < tc:
        top_re, top_im = pw_re[-1], pw_im[-1]
        pw_re, pw_im = (jnp.concatenate([pw_re, top_re * pw_re - top_im * pw_im]),
                        jnp.concatenate([pw_im, top_re * pw_im + top_im * pw_re]))
    pw_re = jnp.concatenate([jnp.ones_like(ab_re)[None], pw_re])
    pw_im = jnp.concatenate([jnp.zeros_like(ab_im)[None], pw_im])
    abr = pw_re[:tc, :, :, None] * bb_re - pw_im[:tc, :, :, None] * bb_im
    abi = pw_re[:tc, :, :, None] * bb_im + pw_im[:tc, :, :, None] * bb_re
    kern = (jnp.einsum('gjp,dgpi->dgji', c_re, abr, precision=HIGHEST)
            - jnp.einsum('gjp,dgpi->dgji', c_im, abi, precision=HIGHEST))
    gt = S5_GROUP_BLOCK
    nt = g // gt
    sw = 2 * p
    kc = kern[::-1].reshape(tc, nt, gt, C_GROUP, C_GROUP).transpose(1, 0, 2, 4, 3)
    kc = kc.reshape(nt, tc * gt * C_GROUP, C_GROUP)
    ab = jnp.concatenate([abr, abi], axis=2)[::-1]
    bc = ab.reshape(tc, nt, gt, sw, C_GROUP).transpose(1, 0, 2, 4, 3).reshape(nt, tc * gt * C_GROUP, sw)
    cr = c_re[None] * pw_re[1:, :, None, :] - c_im[None] * pw_im[1:, :, None, :]
    ci = -(c_re[None] * pw_im[1:, :, None, :] + c_im[None] * pw_re[1:, :, None, :])
    cc = jnp.concatenate([cr, ci], axis=3)
    cc = cc.reshape(tc, nt, gt, C_GROUP, sw).transpose(1, 2, 4, 0, 3).reshape(nt, gt * sw, tc * C_GROUP)
    r, i = pw_re[tc], pw_im[tc]
    rows = []
    for _ in range(max(n_log, 1)):
        rows += [jnp.concatenate([r, r], -1), jnp.concatenate([-i, i], -1)]
        r, i = r * r - i * i, 2.0 * r * i
    apow = jnp.stack(rows, axis=1)
    apow = apow.reshape(nt, gt, -1, sw).transpose(0, 2, 1, 3).reshape(nt, -1, gt * sw)
    return kc.astype(BF16), bc.astype(BF16), cc.astype(BF16), apow


def _s5(proj2d, ops, x0_re, x0_im, *, l):
    kc, bc, cc, apow = ops
    bsz = x0_re.shape[0]
    nt = C_GROUPS // S5_GROUP_BLOCK
    x0 = jnp.concatenate([x0_re, x0_im], axis=-1).reshape(bsz, nt, 1, -1).transpose(1, 0, 2, 3)
    bb = max(s for s in range(1, bsz + 1) if bsz % s == 0 and s * l <= S5_MAX_ROWS)
    y, xf = _s5_chunks(proj2d, kc, bc, cc, apow, x0, l=l, bb=bb)
    xf = xf.transpose(1, 0, 2, 3).reshape(bsz, C_GROUPS, 2 * C_STATE)
    return y, xf[..., :C_STATE], xf[..., C_STATE:]


AB_SRC_GA = AB_QKV
AB_SRC_QKV = AB_SRC_GA + A_GATE_RANK
AB_SRC_TAIL = AB_SRC_QKV + B_QKV + B_VW
IN_AB = AB_SRC_TAIL + 2 * B_HEADS
IN_CD = CD_SMALL + 2 * D_HEADS
WPREP_AB_TILE = 512
WPREP_CD_TILE = 640


def _prep_w_ab_kernel(wt_ref, ga_ref, o_ref):
    j = pl.program_id(0)
    tile = o_ref.shape[1]
    n_small = A_GATE_RANK + 2 * B_HEADS

    @pl.when(j < AB_SMALL // tile)
    def _():
        o_ref[...] = wt_ref[...].T.astype(BF16)

    @pl.when(j == AB_SMALL // tile)
    def _():
        rows = jnp.concatenate([ga_ref[...], wt_ref[tile - 2 * B_HEADS:tile, :],
                                jnp.zeros((tile - n_small, wt_ref.shape[1]), F32)], axis=0)
        o_ref[...] = rows.T.astype(BF16)

    @pl.when(j > AB_SMALL // tile)
    def _():
        o_ref[...] = jnp.zeros(o_ref.shape, BF16)


def _prep_w_ab(w_t, n_out):
    n_in, d = w_t.shape
    tile = WPREP_AB_TILE
    assert AB_QKV % tile == 0 and AB_SMALL % tile == 0 and n_out % tile == 0 and n_in >= tile

    unit = A_GATE_RANK
    assert tile % unit == 0 and (AB_SRC_QKV - AB_QKV) % unit == 0 and (n_in - tile) % unit == 0

    def src_row(j):
        k = j * (tile // unit)
        k = jnp.where(j < AB_QKV // tile, k,
                      jnp.where(j < AB_SMALL // tile, k + (AB_SRC_QKV - AB_QKV) // unit, (n_in - tile) // unit))
        return unit * k

    return pl.pallas_call(
        _prep_w_ab_kernel,
        out_shape=jax.ShapeDtypeStruct((d, n_out), BF16),
        grid=(n_out // tile,),
        in_specs=[pl.BlockSpec((pl.Element(tile), pl.Element(d)), lambda j: (src_row(j), 0)),
                  pl.BlockSpec((pl.Element(A_GATE_RANK), pl.Element(d)), lambda j: (AB_SRC_GA, 0))],
        out_specs=pl.BlockSpec((d, tile), lambda j: (0, j)),
        compiler_params=pltpu.CompilerParams(
            dimension_semantics=("parallel",),
            vmem_limit_bytes=_vmem_limit(tile * d * 6, 2 * tile * d * 4)),
        name="prep_w_in_ab",
    )(w_t, w_t)


def _prep_w_cd_kernel(wt_ref, o_ref):
    tile = o_ref.shape[1]
    row = pl.program_id(0) * tile + lax.broadcasted_iota(jnp.int32, wt_ref.shape, 0)
    o_ref[...] = jnp.where(row < IN_CD, wt_ref[...], 0.0).T.astype(BF16)


def _prep_w_cd(w_t, n_out):
    n_in, d = w_t.shape
    tile = WPREP_CD_TILE
    assert n_out % tile == 0
    return pl.pallas_call(
        _prep_w_cd_kernel,
        out_shape=jax.ShapeDtypeStruct((d, n_out), BF16),
        grid=(n_out // tile,),
        in_specs=[pl.BlockSpec((tile, d), lambda j: (j, 0))],
        out_specs=pl.BlockSpec((d, tile), lambda j: (0, j)),
        compiler_params=pltpu.CompilerParams(
            dimension_semantics=("parallel",),
            vmem_limit_bytes=_vmem_limit(tile * d * 6, 2 * tile * d * 4)),
        name="prep_w_in_cd",
    )(w_t)


def _lane_row(vals, lane0):
    return jnp.zeros((1, LANES), F32).at[0, lane0:lane0 + vals.shape[0]].set(vals.astype(F32))


def _prepare_weights(norm_g, final_norm_g, w_in_ab, a_gate_w, a_gate_b, a_norm_g, b_conv_w, b_a_log,
                     b_dt_bias, b_norm_g, w_out_ab, w_in_cd, c_lam_re, c_lam_im, c_log_dt, c_b_re,
                     c_b_im, c_c_re, c_c_im, c_d, c_glu_w, c_glu_b, d_i_bias, d_f_bias, d_norm_g,
                     w_out_cd, n_log):
    assert w_in_ab.shape[1] == IN_AB and w_in_cd.shape[1] == IN_CD
    w_ab = _prep_w_ab(w_in_ab.astype(F32).T, _round_up(AB_SMALL + LANES, PROJ_TN))
    w_cd = _prep_w_cd(w_in_cd.astype(F32).T, _round_up(CD_SMALL + LANES, PROJ_TN))
    gate_w = jnp.zeros((LANES, A_KW), F32).at[AB_GA_LANE:AB_GA_LANE + A_GATE_RANK].set(
        a_gate_w.astype(F32)).astype(BF16)
    return dict(
        norm_g=norm_g.astype(F32), final_g=final_norm_g.astype(F32)[None, :],
        w_ab=w_ab, w_cd=w_cd, gate_w=gate_w, gate_b=a_gate_b.astype(F32)[None, :],
        a_norm_g=a_norm_g.astype(F32)[None, :], conv_w=b_conv_w.astype(F32),
        alog=_lane_row(b_a_log, AB_APRE_LANE), dtb=_lane_row(b_dt_bias, AB_APRE_LANE),
        b_norm_g=b_norm_g.astype(F32)[None, :],
        w_out_ab=w_out_ab.astype(BF16),
        s5_ops=_s5_operators(c_lam_re.astype(F32), c_lam_im.astype(F32), c_log_dt, c_b_re.astype(F32),
                             c_b_im.astype(F32), c_c_re.astype(F32), c_c_im.astype(F32), n_log),
        c_d=c_d.astype(F32).reshape(1, C_W), glu_w=c_glu_w.astype(BF16),
        glu_b=c_glu_b.astype(F32)[None, :],
        ib=_lane_row(d_i_bias, CD_I_LANE), fb=_lane_row(d_f_bias, CD_F_LANE),
        d_norm_g=d_norm_g.astype(F32)[None, :],
        w_out_cd=w_out_cd.astype(BF16),
    )


def _trunk(x, conv_prev, s_gla0, s_gdn0, s5_re0, s5_im0, mc0, mn0, mm0, w):
    bsz, l, d = x.shape
    c = min(CHUNK, l)
    bb = max(s for s in (1, 2, 4) if s <= MIXER_STREAMS and bsz % s == 0)
    tb_s = min(l, (8 // bb) * c)
    n_s5 = l // S5_CHUNK
    assert l % tb_s == 0 and l % S5_CHUNK == 0 and n_s5 & (n_s5 - 1) == 0
    x2d = x.reshape(bsz * l, d)

    proj = _norm_matmul(x2d, w['norm_g'][0:1], w['w_ab'])
    proj3 = proj.reshape(bsz, l, proj.shape[1])
    o_a, s_gla = _gla(proj3, w['gate_w'], w['gate_b'], w['a_norm_g'], s_gla0.astype(F32), c=c, tb=tb_s, bb=bb)
    o_b, s_gdn = _gdn(proj3, conv_prev.astype(F32), w['conv_w'], w['alog'], w['dtb'], w['b_norm_g'],
                      s_gdn0.astype(F32), c=c, tb=tb_s, bb=bb)
    conv_new = proj3[:, l - (B_CONV - 1):, AB_QKV:AB_QKV + B_QKV]
    h1 = _out_proj(o_a.reshape(bsz * l, A_VW), o_b.reshape(bsz * l, B_VW), w['w_out_ab'], x2d)

    proj = _norm_matmul(h1, w['norm_g'][1:2], w['w_cd'])
    proj3 = proj.reshape(bsz, l, proj.shape[1])
    y, s5_re, s5_im = _s5(proj, w['s5_ops'], s5_re0.astype(F32), s5_im0.astype(F32), l=l)
    o_d, mc, mn, mm = _mlstm(proj3, w['ib'], w['fb'], w['d_norm_g'], mc0.astype(F32),
                             mn0.astype(F32), mm0.astype(F32)[:, None, :], c=c, tb=tb_s, bb=bb)
    y_out = _glu_out_proj_norm(y, proj, o_d.reshape(bsz * l, D_VW), w['c_d'], w['glu_w'], w['glu_b'],
                               w['w_out_cd'], h1, w['final_g'])
    dt = x.dtype
    return (y_out.reshape(bsz, l, d).astype(dt), conv_new.astype(dt), s_gla.astype(dt), s_gdn.astype(dt),
            s5_re.astype(dt), s5_im.astype(dt), mc.astype(dt), mn.astype(dt), mm[:, 0, :].astype(dt))


def kernel(x_prompt, x_sample, cache_gdn_conv, state_gla, state_gdn, state_s5_re, state_s5_im,
           state_mlstm_c, state_mlstm_n, state_mlstm_m, norm_g, final_norm_g, w_in_ab, a_gate_w,
           a_gate_b, a_norm_g, b_conv_w, b_a_log, b_dt_bias, b_norm_g, w_out_ab, w_in_cd, c_lam_re,
           c_lam_im, c_log_dt, c_b_re, c_b_im, c_c_re, c_c_im, c_d, c_glu_w, c_glu_b, d_i_bias,
           d_f_bias, d_norm_g, w_out_cd):
    n_log = int(math.log2(max(x_prompt.shape[1], x_sample.shape[1]) // S5_CHUNK))
    w = _prepare_weights(norm_g, final_norm_g, w_in_ab, a_gate_w, a_gate_b, a_norm_g, b_conv_w, b_a_log,
                         b_dt_bias, b_norm_g, w_out_ab, w_in_cd, c_lam_re, c_lam_im, c_log_dt, c_b_re,
                         c_b_im, c_c_re, c_c_im, c_d, c_glu_w, c_glu_b, d_i_bias, d_f_bias, d_norm_g,
                         w_out_cd, n_log)
    nb = x_prompt.shape[0]
    zeros = lambda *shape: jnp.zeros(shape, F32)
    p_out = _trunk(x_prompt, zeros(nb, B_CONV - 1, B_QKV), zeros(nb, A_HEADS, A_DK, A_DV),
                   zeros(nb, B_HEADS, B_DK, B_DV), zeros(nb, C_GROUPS, C_STATE), zeros(nb, C_GROUPS, C_STATE),
                   zeros(nb, D_HEADS, D_DK, D_DV), zeros(nb, D_HEADS, D_DK), zeros(nb, D_HEADS), w)
    s_out = _trunk(x_sample, cache_gdn_conv, state_gla, state_gdn, state_s5_re, state_s5_im,
                   state_mlstm_c, state_mlstm_n, state_mlstm_m, w)
    return (p_out[0], s_out[0]) + tuple(p_out[1:]) + tuple(s_out[1:])
```

```python
import functools
import math

import jax
import jax.numpy as jnp
from jax import lax
from jax.experimental import pallas as pl
from jax.experimental.pallas import tpu as pltpu

F32 = jnp.float32
BF16 = jnp.bfloat16
HIGHEST = lax.Precision.HIGHEST

NORM_EPS = 1e-6
CHUNK = 64
A_HEADS, A_DK, A_DV, A_GATE_RANK, A_GATE_TAU = 4, 128, 256, 16, 16.0
B_HEADS, B_DK, B_DV, B_CONV = 8, 128, 128, 4
C_GROUP, C_GROUPS, C_STATE = 16, 64, 64
D_HEADS, D_DK, D_DV = 4, 128, 256
A_KW, A_VW = A_HEADS * A_DK, A_HEADS * A_DV
B_KW, B_VW = B_HEADS * B_DK, B_HEADS * B_DV
B_QKV = 2 * B_KW + B_VW
C_W = C_GROUPS * C_GROUP
D_KW, D_VW = D_HEADS * D_DK, D_HEADS * D_DV

LANES = 128
SUBLANES = 8
VMEM_BYTES_V7X = 64 * 1024 * 1024

AB_Q, AB_K, AB_V, AB_Z = 0, A_KW, 2 * A_KW, 2 * A_KW + A_VW
AB_QKV = AB_Z + A_VW
AB_ZB = AB_QKV + B_QKV
AB_SMALL = AB_ZB + B_VW
AB_GA_LANE, AB_BETA_LANE, AB_APRE_LANE = 0, A_GATE_RANK, A_GATE_RANK + B_HEADS
CD_U, CD_Z = 0, C_W
CD_Q = 2 * C_W
CD_K = CD_Q + D_KW
CD_V = CD_K + D_KW
CD_O = CD_V + D_VW
CD_ZD = CD_O + D_VW
CD_SMALL = CD_ZD + D_VW
CD_I_LANE, CD_F_LANE = 0, D_HEADS

PROJ_TN = 1280
S5_CHUNK = 16
S5_GROUP_BLOCK = 8
S5_MAX_ROWS = 8192
NORM_ROW_SPLITS = 4
GLU_ROW_SPLITS = 2
MIXER_STREAMS = 4
MLSTM_INTRA_CHUNKS = 4
GDN_WY_CHAINS = 32


def _round_up(x, m):
    return (x + m - 1) // m * m


def _vmem_limit(block_bytes, scratch_bytes=0):
    est = 2 * block_bytes + scratch_bytes
    return int(min(max(2 * est, 32 * 1024 * 1024), VMEM_BYTES_V7X - 8 * 1024 * 1024))


def _mm(a, b):
    return jnp.dot(a, b, preferred_element_type=F32)


def _dot(a, b):
    return _mm(a.astype(BF16), b.astype(BF16))


def _dot_nt(a, b):
    return lax.dot_general(a.astype(BF16), b.astype(BF16), (((1,), (1,)), ((), ())),
                           preferred_element_type=F32)


def _dot_tn(a, b):
    return lax.dot_general(a.astype(BF16), b.astype(BF16), (((0,), (0,)), ((), ())),
                           preferred_element_type=F32)


def _split2(x):
    hi = x.astype(BF16)
    return hi, (x - hi.astype(F32)).astype(BF16)


def _split3(x):
    hi = x.astype(BF16)
    r = x - hi.astype(F32)
    mid = r.astype(BF16)
    return hi, mid, (r - mid.astype(F32)).astype(BF16)


def _cumsum_rows(tri_bf16, x):
    hi, mid, lo = _split3(x)
    return _mm(tri_bf16, hi) + _mm(tri_bf16, mid) + _mm(tri_bf16, lo)


def _select_rows(sel_bf16, x):
    nt = lambda b: lax.dot_general(sel_bf16, b, (((1,), (1,)), ((), ())), preferred_element_type=F32)
    hi, mid, lo = _split3(x)
    return nt(hi) + nt(mid) + nt(lo)


def _lane_selector(lane0):
    r = lax.broadcasted_iota(jnp.int32, (SUBLANES, LANES), 0)
    l = lax.broadcasted_iota(jnp.int32, (SUBLANES, LANES), 1)
    return jnp.where(l == r + lane0, 1.0, 0.0).astype(BF16)


def _causal_masks(c):
    row = lax.broadcasted_iota(jnp.int32, (c, c), 0)
    col = lax.broadcasted_iota(jnp.int32, (c, c), 1)
    return row >= col, row > col


def _norm_matmul_kernel(x_ref, g_ref, w_ref, o_ref, xn_ref):
    j = pl.program_id(1)

    @pl.when(j == 0)
    def _():
        tm = x_ref.shape[0]
        sub = tm // NORM_ROW_SPLITS
        for i in range(NORM_ROW_SPLITS):
            s = slice(i * sub, (i + 1) * sub)
            x = x_ref[s, :]
            y = (x * lax.rsqrt(jnp.mean(x * x, axis=-1, keepdims=True) + NORM_EPS) * g_ref[...]).astype(BF16)
            xn_ref[s, :] = y
            o_ref[s, :] = jnp.dot(y, w_ref[...], preferred_element_type=F32)

    @pl.when(j != 0)
    def _():
        o_ref[...] = jnp.dot(xn_ref[...], w_ref[...], preferred_element_type=F32)


def _norm_matmul(x2d, g_row, w_bf16):
    m, d = x2d.shape
    n = w_bf16.shape[1]
    tm = min(m, 1024)
    tn = PROJ_TN
    assert m % tm == 0 and n % tn == 0
    blocks = tm * d * 4 + d * tn * 2 + tm * tn * 4
    return pl.pallas_call(
        _norm_matmul_kernel,
        out_shape=jax.ShapeDtypeStruct((m, n), F32),
        grid=(m // tm, n // tn),
        in_specs=[pl.BlockSpec((tm, d), lambda i, j: (i, 0)),
                  pl.BlockSpec((1, d), lambda i, j: (0, 0)),
                  pl.BlockSpec((d, tn), lambda i, j: (0, j))],
        out_specs=pl.BlockSpec((tm, tn), lambda i, j: (i, j)),
        scratch_shapes=[pltpu.VMEM((tm, d), BF16)],
        compiler_params=pltpu.CompilerParams(
            dimension_semantics=("parallel", "arbitrary"),
            vmem_limit_bytes=_vmem_limit(blocks, tm * d * 2)),
        name="norm_in_proj",
    )(x2d, g_row, w_bf16)


def _out_proj_kernel(a_ref, b_ref, wa_ref, wb_ref, h_ref, o_ref):
    out = (jnp.dot(a_ref[...], wa_ref[...], preferred_element_type=F32)
           + jnp.dot(b_ref[...], wb_ref[...], preferred_element_type=F32))
    o_ref[...] = h_ref[...] + out


def _glu_out_proj_norm_kernel(y_ref, u_ref, z_ref, od_ref, d_ref, gw_ref, gb_ref, wc_ref, wd_ref, h_ref, g_ref,
                              o_ref):
    tm = y_ref.shape[0]
    sub = tm // GLU_ROW_SPLITS
    halves = [slice(i * sub, (i + 1) * sub) for i in range(GLU_ROW_SPLITS)]
    y = [jax.nn.gelu(y_ref[s, :] + d_ref[...] * u_ref[s, :]) for s in halves]
    gate = [jax.nn.sigmoid(jnp.dot(v.astype(BF16), gw_ref[...], preferred_element_type=F32) + gb_ref[...])
            for v in y]
    o_c = [(y[i] * gate[i] * jax.nn.silu(z_ref[s, :])).astype(BF16) for i, s in enumerate(halves)]
    out = [jnp.dot(o_c[i], wc_ref[...], preferred_element_type=F32)
           + jnp.dot(od_ref[s, :], wd_ref[...], preferred_element_type=F32) for i, s in enumerate(halves)]
    for i, s in enumerate(halves):
        h = h_ref[s, :] + out[i]
        o_ref[s, :] = h * lax.rsqrt(jnp.mean(h * h, axis=-1, keepdims=True) + NORM_EPS) * g_ref[...]


def _glu_out_proj_norm(y2d, proj2d, o_d, d_row, glu_w, glu_b_row, w_out, h2d, final_g_row):
    m, d = h2d.shape
    tm = min(m, 512)
    assert m % tm == 0 and C_W == D_VW and w_out.shape[0] == C_W + D_VW
    row = lambda col: (lambda i: (i, col))
    const = lambda i: (0, 0)
    resident = pl.Buffered(1)
    blocks = 3 * tm * C_W * 4 + tm * D_VW * 2 + 2 * tm * d * 4
    weights = C_W * C_W * 2 + (C_W + D_VW) * d * 2
    return pl.pallas_call(
        _glu_out_proj_norm_kernel,
        out_shape=jax.ShapeDtypeStruct((m, d), F32),
        grid=(m // tm,),
        in_specs=[pl.BlockSpec((tm, C_W), row(0)),
                  pl.BlockSpec((tm, C_W), row(CD_U // C_W)),
                  pl.BlockSpec((tm, C_W), row(CD_Z // C_W)),
                  pl.BlockSpec((tm, D_VW), row(0)),
                  pl.BlockSpec((1, C_W), const),
                  pl.BlockSpec((C_W, C_W), const, pipeline_mode=resident),
                  pl.BlockSpec((1, C_W), const),
                  pl.BlockSpec((C_W, d), const, pipeline_mode=resident),
                  pl.BlockSpec((D_VW, d), lambda i: (1, 0), pipeline_mode=resident),
                  pl.BlockSpec((tm, d), row(0)),
                  pl.BlockSpec((1, d), const)],
        out_specs=pl.BlockSpec((tm, d), row(0)),
        compiler_params=pltpu.CompilerParams(
            dimension_semantics=("parallel",),
            vmem_limit_bytes=_vmem_limit(blocks, weights + 3 * tm * C_W * 4)),
        name="glu_out_proj_norm",
    )(y2d, proj2d, proj2d, o_d, d_row, glu_w, glu_b_row, w_out, w_out, h2d, final_g_row)


def _out_proj(mix_a, mix_b, w_out, h2d):
    m, d = h2d.shape
    ka, kb = mix_a.shape[1], mix_b.shape[1]
    tm = min(m, 512)
    assert m % tm == 0 and ka == kb and w_out.shape[0] == ka + kb
    blocks = tm * (ka + kb) * 2 + (ka + kb) * d * 2 + 2 * tm * d * 4
    return pl.pallas_call(
        _out_proj_kernel,
        out_shape=jax.ShapeDtypeStruct((m, d), F32),
        grid=(m // tm,),
        in_specs=[pl.BlockSpec((tm, ka), lambda i: (i, 0)),
                  pl.BlockSpec((tm, kb), lambda i: (i, 0)),
                  pl.BlockSpec((ka, d), lambda i: (0, 0)),
                  pl.BlockSpec((kb, d), lambda i: (1, 0)),
                  pl.BlockSpec((tm, d), lambda i: (i, 0))],
        out_specs=pl.BlockSpec((tm, d), lambda i: (i, 0)),
        compiler_params=pltpu.CompilerParams(
            dimension_semantics=("parallel",),
            vmem_limit_bytes=_vmem_limit(blocks)),
        name="out_proj",
    )(mix_a, mix_b, w_out, w_out, h2d)


def _gla_kernel(q_ref, k_ref, v_ref, z_ref, sm_ref, gw_ref, gb_ref, ng_ref, s0_ref,
                o_ref, sout_ref, st_ref, *, c, nchunks):
    t = pl.program_id(1)
    last_t = pl.num_programs(1) - 1
    bb = q_ref.shape[0]
    chains = [(bi, h) for bi in range(bb) for h in range(A_HEADS)]
    ksl = [slice(h * A_DK, (h + 1) * A_DK) for _, h in chains]
    vsl = [slice(h * A_DV, (h + 1) * A_DV) for _, h in chains]
    idx = range(len(chains))

    @pl.when(t == 0)
    def _():
        for bi, h in chains:
            st_ref[bi, h] = s0_ref[bi, h].T

    causal, _ = _causal_masks(c)
    tri = causal.astype(BF16)
    gw = gw_ref[...]
    gb = gb_ref[...]

    def body(n, carry):
        sl = pl.ds(pl.multiple_of(n * c, c), c)
        b_all = [_cumsum_rows(tri, jax.nn.log_sigmoid(_dot(sm_ref[bi, sl, :], gw) + gb) * (1.0 / A_GATE_TAU))
                 for bi in range(bb)]
        b = [b_all[chains[i][0]][:, ksl[i]] for i in idx]
        b_last = [b[i][c - 1:c, :] for i in idx]
        k = [k_ref[chains[i][0], sl, ksl[i]] for i in idx]
        v = [v_ref[chains[i][0], sl, vsl[i]].astype(BF16) for i in idx]
        q_dec = [(q_ref[chains[i][0], sl, ksl[i]] * (A_DK ** -0.5) * jnp.exp(b[i])).astype(BF16) for i in idx]
        k_dec = [(k[i] * jnp.exp(-b[i])).astype(BF16) for i in idx]
        k_w = [(k[i] * jnp.exp(b_last[i] - b[i])).astype(BF16) for i in idx]
        scores = [jnp.where(causal, _dot_nt(q_dec[i], k_dec[i]), 0.0).astype(BF16) for i in idx]
        s_t = [st_ref[bi, h] for bi, h in chains]
        outs = [_mm(scores[i], v[i]) + _dot_nt(q_dec[i], s_t[i]) for i in idx]
        for i, (bi, h) in enumerate(chains):
            st_ref[bi, h] = s_t[i] * jnp.exp(b_last[i]) + _dot_tn(v[i], k_w[i])
        for i, (bi, h) in enumerate(chains):
            o = outs[i]
            o = o * lax.rsqrt(jnp.mean(o * o, axis=-1, keepdims=True) + NORM_EPS) * ng_ref[:, vsl[i]]
            o_ref[bi, sl, vsl[i]] = (o * jax.nn.silu(z_ref[bi, sl, vsl[i]])).astype(o_ref.dtype)
        return carry

    lax.fori_loop(0, nchunks, body, 0)

    @pl.when(t == last_t)
    def _():
        for bi, h in chains:
            sout_ref[bi, h] = st_ref[bi, h].T


def _gla(proj3, gate_w_pad, gate_b_row, norm_g_row, s0, *, c, tb, bb):
    bsz, l, _ = proj3.shape
    nblk = l // tb
    assert bsz % bb == 0
    tok = lambda col: (lambda b, t: (b, t, col))
    in_specs = [
        pl.BlockSpec((bb, tb, A_KW), tok(AB_Q // A_KW)),
        pl.BlockSpec((bb, tb, A_KW), tok(AB_K // A_KW)),
        pl.BlockSpec((bb, tb, A_VW), tok(AB_V // A_VW)),
        pl.BlockSpec((bb, tb, A_VW), tok(AB_Z // A_VW)),
        pl.BlockSpec((bb, tb, LANES), tok(AB_SMALL // LANES)),
        pl.BlockSpec((LANES, A_KW), lambda b, t: (0, 0)),
        pl.BlockSpec((1, A_KW), lambda b, t: (0, 0)),
        pl.BlockSpec((1, A_VW), lambda b, t: (0, 0)),
        pl.BlockSpec((bb, A_HEADS, A_DK, A_DV), lambda b, t: (b, 0, 0, 0)),
    ]
    out_specs = [
        pl.BlockSpec((bb, tb, A_VW), lambda b, t: (b, t, 0)),
        pl.BlockSpec((bb, A_HEADS, A_DK, A_DV), lambda b, t: (b, 0, 0, 0)),
    ]
    state = bb * A_HEADS * A_DK * A_DV * 4
    blocks = bb * tb * (2 * A_KW + 2 * A_VW + LANES) * 4 + bb * tb * A_VW * 2 + 2 * state
    return pl.pallas_call(
        functools.partial(_gla_kernel, c=c, nchunks=tb // c),
        out_shape=[jax.ShapeDtypeStruct((bsz, l, A_VW), BF16),
                   jax.ShapeDtypeStruct((bsz, A_HEADS, A_DK, A_DV), F32)],
        grid=(bsz // bb, nblk),
        in_specs=in_specs,
        out_specs=out_specs,
        scratch_shapes=[pltpu.VMEM((bb, A_HEADS, A_DV, A_DK), F32)],
        compiler_params=pltpu.CompilerParams(
            dimension_semantics=("parallel", "arbitrary"),
            vmem_limit_bytes=_vmem_limit(blocks, state)),
        name="gla_mixer",
    )(proj3, proj3, proj3, proj3, proj3, gate_w_pad, gate_b_row, norm_g_row, s0)


def _gdn_kernel(x_ref, z_ref, sm_ref, w_ref, cp_ref, alog_ref, dtb_ref, ng_ref, s0_ref,
                o_ref, sout_ref, s_ref, tail_ref, u_ref, wm_ref, qg_ref, kg_ref, qk_ref, gl_ref,
                *, c, nchunks):
    t = pl.program_id(1)
    last_t = pl.num_programs(1) - 1
    tb = c * nchunks
    keep = SUBLANES - (B_CONV - 1)
    bb = x_ref.shape[0]

    @pl.when(t == 0)
    def _():
        s_ref[...] = s0_ref[...]
        tail_ref[:, 0:keep, :] = jnp.zeros((bb, keep, B_QKV), F32)
        tail_ref[:, keep:SUBLANES, :] = cp_ref[...]

    def conv_silu(bi, row0, first, cols):
        x = x_ref[bi, pl.ds(row0, c), cols]
        if first is None:
            prev = x_ref[bi, pl.ds(pl.multiple_of(row0 - SUBLANES, SUBLANES), SUBLANES), cols]
        else:
            before = pl.multiple_of(jnp.maximum(row0 - SUBLANES, 0), SUBLANES)
            prev = jnp.where(first, tail_ref[bi, :, cols], x_ref[bi, pl.ds(before, SUBLANES), cols])
        w = w_ref[:, cols]
        ext = jnp.concatenate([prev, x], axis=0)
        ext1 = pltpu.roll(ext, 1, 0)
        newer = ext * w[3:4, :] + ext1 * w[2:3, :]
        older = ext * w[1:2, :] + ext1 * w[0:1, :]
        conv = newer + pltpu.roll(older, 2, 0)
        return jax.nn.silu(conv[SUBLANES:SUBLANES + c, :])

    causal, strict = _causal_masks(c)
    tri = causal.astype(BF16)
    eye = jnp.where(causal & jnp.logical_not(strict), 1.0, 0.0).astype(F32)
    sel = _lane_selector(AB_APRE_LANE)
    neg_a_exp = -jnp.exp(alog_ref[...])
    dtb = dtb_ref[...]
    ng = ng_ref[...]
    n_double = int(math.log2(c)) - 1
    heads = range(B_HEADS)

    group = max(1, GDN_WY_CHAINS // (bb * B_HEADS))
    group = group if nchunks % group == 0 else 1

    def wy_factors(n, carry):
        pws, rhss, where = [], [], []
        for bi, ci in [(bi, ci) for ci in range(group) for bi in range(bb)]:
            row0 = pl.multiple_of((n * group + ci) * c, c)
            first = (n == 0) if ci == 0 else None
            sl = pl.ds(row0, c)
            sm = sm_ref[bi, sl, :]
            g_cum = _cumsum_rows(tri, neg_a_exp * jax.nn.softplus(sm + dtb))
            g_rows = _select_rows(sel, g_cum)
            beta_all = jax.nn.sigmoid(sm)
            gl_ref[bi, pl.ds(n * group + ci, 1), :] = g_cum[c - 1:c, :]
            for h in heads:
                hs = slice(h * B_DK, (h + 1) * B_DK)
                q = conv_silu(bi, row0, first, hs)
                k = conv_silu(bi, row0, first, slice(B_KW + h * B_DK, B_KW + (h + 1) * B_DK))
                v = conv_silu(bi, row0, first, slice(2 * B_KW + h * B_DV, 2 * B_KW + (h + 1) * B_DV))
                q = q * lax.rsqrt(jnp.sum(q * q, axis=-1, keepdims=True) + NORM_EPS) * (B_DK ** -0.5)
                k = k * lax.rsqrt(jnp.sum(k * k, axis=-1, keepdims=True) + NORM_EPS)
                g_col = g_cum[:, AB_APRE_LANE + h:AB_APRE_LANE + h + 1]
                beta = beta_all[:, AB_BETA_LANE + h:AB_BETA_LANE + h + 1]
                decay = jnp.exp(jnp.where(causal, g_col - g_rows[h:h + 1, :], -jnp.inf))
                e_g = jnp.exp(g_col)
                k_beta = k * beta
                pws.append(-jnp.where(strict, _dot_nt(k_beta, k) * decay, 0.0))
                rhss.append(jnp.concatenate([v * beta, k_beta * e_g], axis=1))
                where.append((bi, sl, hs))
                qk_ref[bi, sl, h * LANES:h * LANES + c] = jnp.where(causal, _dot_nt(q, k) * decay,
                                                                    0.0).astype(BF16)
                qg_ref[bi, sl, hs] = (q * e_g).astype(BF16)
                kg_ref[bi, sl, hs] = (k * jnp.exp(g_col[c - 1:c, :] - g_col)).astype(BF16)
        items = range(len(pws))
        neg_lower = [_split2(p) for p in pws]
        invs = [eye + p for p in pws]
        for _ in range(n_double):
            pws = [_dot(p, p) for p in pws]
            invs = [i + _dot(i, p) for i, p in zip(invs, pws)]
        invs = [i.astype(BF16) for i in invs]
        sol = [_mm(invs[i], rhss[i].astype(BF16)).astype(BF16) for i in items]
        resid = [rhss[i] - sol[i].astype(F32) + _mm(neg_lower[i][0], sol[i]) + _mm(neg_lower[i][1], sol[i])
                 for i in items]
        for i in items:
            bi, sl, hs = where[i]
            uw = sol[i].astype(F32) + _mm(invs[i], resid[i].astype(BF16))
            u_ref[bi, sl, hs] = uw[:, :B_DV]
            wm_ref[bi, sl, hs] = uw[:, B_DV:].astype(BF16)
        return carry

    lax.fori_loop(0, nchunks // group, wy_factors, 0)
    tail_ref[...] = x_ref[:, tb - SUBLANES:tb, :]

    chains = [(bi, h) for bi in range(bb) for h in heads]
    hsl = [slice(h * B_DK, (h + 1) * B_DK) for _, h in chains]

    def recurrence(n, carry):
        sl = pl.ds(pl.multiple_of(n * c, c), c)
        e_last = [jnp.exp(gl_ref[bi, pl.ds(n, 1), :]) for bi in range(bb)]
        s_old = [s_ref[bi, h] for bi, h in chains]
        s_bf = [s.astype(BF16) for s in s_old]
        v_new = [(u_ref[bi, sl, hsl[i]] - _mm(wm_ref[bi, sl, hsl[i]], s_bf[i])).astype(BF16)
                 for i, (bi, h) in enumerate(chains)]
        outs = [_mm(qg_ref[bi, sl, hsl[i]], s_bf[i]) + _mm(qk_ref[bi, sl, h * LANES:h * LANES + c], v_new[i])
                for i, (bi, h) in enumerate(chains)]
        for i, (bi, h) in enumerate(chains):
            s_ref[bi, h] = (e_last[bi][:, AB_APRE_LANE + h:AB_APRE_LANE + h + 1] * s_old[i]
                            + lax.dot_general(kg_ref[bi, sl, hsl[i]], v_new[i], (((0,), (0,)), ((), ())),
                                              preferred_element_type=F32))
        for i, (bi, h) in enumerate(chains):
            o = outs[i]
            o = o * lax.rsqrt(jnp.mean(o * o, axis=-1, keepdims=True) + NORM_EPS) * ng
            o_ref[bi, sl, hsl[i]] = (o * jax.nn.silu(z_ref[bi, sl, hsl[i]])).astype(o_ref.dtype)
        return carry

    lax.fori_loop(0, nchunks, recurrence, 0)

    @pl.when(t == last_t)
    def _():
        sout_ref[...] = s_ref[...]


def _gdn(proj3, conv_prev, conv_w, alog_row, dtb_row, norm_g_row, s0, *, c, tb, bb):
    bsz, l, _ = proj3.shape
    nblk = l // tb
    assert AB_QKV % B_QKV == 0 and AB_ZB % B_VW == 0 and bsz % bb == 0
    in_specs = [
        pl.BlockSpec((bb, tb, B_QKV), lambda b, t: (b, t, AB_QKV // B_QKV)),
        pl.BlockSpec((bb, tb, B_VW), lambda b, t: (b, t, AB_ZB // B_VW)),
        pl.BlockSpec((bb, tb, LANES), lambda b, t: (b, t, AB_SMALL // LANES)),
        pl.BlockSpec((B_CONV, B_QKV), lambda b, t: (0, 0)),
        pl.BlockSpec((bb, B_CONV - 1, B_QKV), lambda b, t: (b, 0, 0)),
        pl.BlockSpec((1, LANES), lambda b, t: (0, 0)),
        pl.BlockSpec((1, LANES), lambda b, t: (0, 0)),
        pl.BlockSpec((1, B_DV), lambda b, t: (0, 0)),
        pl.BlockSpec((bb, B_HEADS, B_DK, B_DV), lambda b, t: (b, 0, 0, 0)),
    ]
    out_specs = [
        pl.BlockSpec((bb, tb, B_VW), lambda b, t: (b, t, 0)),
        pl.BlockSpec((bb, B_HEADS, B_DK, B_DV), lambda b, t: (b, 0, 0, 0)),
    ]
    blocks = bb * (tb * (B_QKV + B_VW + LANES) * 4 + tb * B_VW * 2 + 2 * B_HEADS * B_DK * B_DV * 4)
    scratch = bb * (B_HEADS * B_DK * B_DV * 4 + SUBLANES * B_QKV * 4 + tb * B_VW * 4
                    + 4 * tb * B_KW * 2 + SUBLANES * LANES * 4)
    return pl.pallas_call(
        functools.partial(_gdn_kernel, c=c, nchunks=tb // c),
        out_shape=[jax.ShapeDtypeStruct((bsz, l, B_VW), BF16),
                   jax.ShapeDtypeStruct((bsz, B_HEADS, B_DK, B_DV), F32)],
        grid=(bsz // bb, nblk),
        in_specs=in_specs,
        out_specs=out_specs,
        scratch_shapes=[pltpu.VMEM((bb, B_HEADS, B_DK, B_DV), F32),
                        pltpu.VMEM((bb, SUBLANES, B_QKV), F32),
                        pltpu.VMEM((bb, tb, B_VW), F32),
                        pltpu.VMEM((bb, tb, B_KW), BF16),
                        pltpu.VMEM((bb, tb, B_KW), BF16),
                        pltpu.VMEM((bb, tb, B_KW), BF16),
                        pltpu.VMEM((bb, tb, B_HEADS * LANES), BF16),
                        pltpu.VMEM((bb, SUBLANES, LANES), F32)],
        compiler_params=pltpu.CompilerParams(
            dimension_semantics=("parallel", "arbitrary"),
            vmem_limit_bytes=_vmem_limit(blocks, scratch)),
        name="gdn_mixer",
    )(proj3, proj3, proj3, conv_w, conv_prev, alog_row, dtb_row, norm_g_row, s0)


def _mlstm_kernel(q_ref, k_ref, v_ref, og_ref, z_ref, sm_ref, ib_ref, fb_ref, ng_ref,
                  c0_ref, n0_ref, m0_ref, o_ref, cout_ref, nout_ref, mout_ref,
                  c_ref, n_ref, m_ref, hi_ref, mi_ref, ni_ref, bc_ref, kv_ref, ks_ref, mc_ref, bl_ref,
                  *, c, nchunks):
    t = pl.program_id(1)
    last_t = pl.num_programs(1) - 1
    bb = q_ref.shape[0]
    chains = [(bi, h) for bi in range(bb) for h in range(D_HEADS)]
    ksl = [slice(h * D_DK, (h + 1) * D_DK) for _, h in chains]
    vsl = [slice(h * D_DV, (h + 1) * D_DV) for _, h in chains]
    idx = range(len(chains))

    @pl.when(t == 0)
    def _():
        c_ref[...] = c0_ref[...]
        n_ref[...] = n0_ref[...]
        for bi, h in chains:
            m_ref[bi, h] = jnp.broadcast_to(m0_ref[bi, :, h:h + 1], (1, LANES))

    causal, _ = _causal_masks(c)
    tri = causal.astype(BF16)
    sel = _lane_selector(0)
    lane = lax.broadcasted_iota(jnp.int32, (c, LANES), 1)
    ib = ib_ref[...]
    fb = fb_ref[...]

    group = MLSTM_INTRA_CHUNKS if nchunks % MLSTM_INTRA_CHUNKS == 0 else 1

    def intra(n, carry):
        items = [(bi, ci, h) for ci in range(group) for bi, h in chains]
        sls = [pl.ds(pl.multiple_of((n * group + ci) * c, c), c) for ci in range(group)]
        sm = {(bi, ci): sm_ref[bi, sls[ci], :] for ci in range(group) for bi in range(bb)}
        i_full = {key: x + ib for key, x in sm.items()}
        b_full = {key: _cumsum_rows(tri, jax.nn.log_sigmoid(x + fb)) for key, x in sm.items()}
        rows = {key: _select_rows(sel, jnp.where(lane < CD_F_LANE, i_full[key], b_full[key]))
                for key in sm}
        b_col = [b_full[bi, ci][:, CD_F_LANE + h:CD_F_LANE + h + 1] for bi, ci, h in items]
        i_col = [i_full[bi, ci][:, CD_I_LANE + h:CD_I_LANE + h + 1] for bi, ci, h in items]
        logw = [jnp.where(causal, b_col[i] - rows[bi, ci][CD_F_LANE + h:CD_F_LANE + h + 1, :]
                          + rows[bi, ci][CD_I_LANE + h:CD_I_LANE + h + 1, :], -jnp.inf)
                for i, (bi, ci, h) in enumerate(items)]
        ids = range(len(items))
        m_intra = [jnp.max(logw[i], axis=-1, keepdims=True) for i in ids]
        ks_ = [slice(h * D_DK, (h + 1) * D_DK) for _, _, h in items]
        vs_ = [slice(h * D_DV, (h + 1) * D_DV) for _, _, h in items]
        q_bf = [(q_ref[bi, sls[ci], ks_[i]] * (D_DK ** -0.5)).astype(BF16) for i, (bi, ci, h) in enumerate(items)]
        k = [k_ref[bi, sls[ci], ks_[i]] for i, (bi, ci, h) in enumerate(items)]
        v = [v_ref[bi, sls[ci], vs_[i]].astype(BF16) for i, (bi, ci, h) in enumerate(items)]
        p = [jnp.exp(logw[i] - m_intra[i]) * _dot_nt(q_bf[i], k[i]) for i in ids]
        n_intra = [jnp.sum(p[i], axis=-1, keepdims=True) for i in ids]
        m_chunk = [m_intra[i][c - 1:c, :] for i in ids]
        k_w = [k[i] * jnp.exp(b_col[i][c - 1:c, :] - b_col[i] + i_col[i] - m_chunk[i]) for i in ids]
        rep = lambda x: jnp.broadcast_to(x, (x.shape[0], LANES))
        for i, (bi, ci, h) in enumerate(items):
            row = pl.ds(n * group + ci, 1)
            hi_ref[bi, sls[ci], vs_[i]] = _dot(p[i], v[i])
            kv_ref[bi, n * group + ci, h] = _dot_tn(k_w[i], v[i])
            ks_ref[bi, h, row, :] = jnp.sum(k_w[i], axis=0, keepdims=True)
            mi_ref[bi, h, sls[ci], :] = rep(m_intra[i])
            ni_ref[bi, h, sls[ci], :] = rep(n_intra[i])
            bc_ref[bi, h, sls[ci], :] = rep(b_col[i])
            mc_ref[bi, h, row, :] = rep(m_chunk[i])
            bl_ref[bi, h, row, :] = rep(b_col[i][c - 1:c, :])
        return carry

    lax.fori_loop(0, nchunks // group, intra, 0)

    twice = lambda x: jnp.concatenate([x, x], axis=1)

    def body(n, carry):
        sl = pl.ds(pl.multiple_of(n * c, c), c)
        row = pl.ds(n, 1)
        q = [q_ref[chains[i][0], sl, ksl[i]] * (D_DK ** -0.5) for i in idx]
        c_s = [c_ref[bi, h] for bi, h in chains]
        n_s = [n_ref[bi, h:h + 1, :] for bi, h in chains]
        m_s = [m_ref[bi, h] for bi, h in chains]
        qc = [_dot(q[i], c_s[i]) for i in idx]
        b_last = [bl_ref[bi, h, row, :] for bi, h in chains]
        m_chunk = [mc_ref[bi, h, row, :] for bi, h in chains]
        for i, (bi, h) in enumerate(chains):
            m_new = jnp.maximum(b_last[i] + m_s[i], m_chunk[i])
            w_old = jnp.exp(b_last[i] + m_s[i] - m_new)
            w_new = jnp.exp(m_chunk[i] - m_new)
            c_ref[bi, h] = twice(w_old) * c_s[i] + twice(w_new) * kv_ref[bi, n, h]
            n_ref[bi, h:h + 1, :] = w_old * n_s[i] + w_new * ks_ref[bi, h, row, :]
            m_ref[bi, h] = m_new
        for i, (bi, h) in enumerate(chains):
            m_intra = mi_ref[bi, h, sl, :]
            a = bc_ref[bi, h, sl, :] + m_s[i]
            m_t = jnp.maximum(a, m_intra)
            w_a = jnp.exp(a - m_t)
            w_i = jnp.exp(m_intra - m_t)
            num = twice(w_a) * qc[i] + twice(w_i) * hi_ref[bi, sl, vsl[i]]
            den = w_a * jnp.sum(q[i] * n_s[i], axis=-1, keepdims=True) + w_i * ni_ref[bi, h, sl, :]
            hh = num / twice(jnp.maximum(jnp.abs(den), jnp.exp(-m_t)))
            hd = jax.nn.sigmoid(og_ref[bi, sl, vsl[i]]) * hh
            oc = hd - jnp.mean(hd, axis=-1, keepdims=True)
            o = oc * lax.rsqrt(jnp.mean(oc * oc, axis=-1, keepdims=True) + NORM_EPS) * ng_ref[:, vsl[i]]
            o_ref[bi, sl, vsl[i]] = (o * jax.nn.silu(z_ref[bi, sl, vsl[i]])).astype(o_ref.dtype)
        return carry

    lax.fori_loop(0, nchunks, body, 0)

    @pl.when(t == last_t)
    def _():
        cout_ref[...] = c_ref[...]
        nout_ref[...] = n_ref[...]
        for bi, h in chains:
            mout_ref[bi, :, h:h + 1] = m_ref[bi, h][:, 0:1]


def _mlstm(proj3, ib_row, fb_row, norm_g_row, c0, n0, m0, *, c, tb, bb):
    bsz, l, _ = proj3.shape
    nblk = l // tb
    assert bsz % bb == 0
    tok = lambda col: (lambda b, t: (b, t, col))
    st4 = lambda b, t: (b, 0, 0, 0)
    st3 = lambda b, t: (b, 0, 0)
    in_specs = [
        pl.BlockSpec((bb, tb, D_KW), tok(CD_Q // D_KW)),
        pl.BlockSpec((bb, tb, D_KW), tok(CD_K // D_KW)),
        pl.BlockSpec((bb, tb, D_VW), tok(CD_V // D_VW)),
        pl.BlockSpec((bb, tb, D_VW), tok(CD_O // D_VW)),
        pl.BlockSpec((bb, tb, D_VW), tok(CD_ZD // D_VW)),
        pl.BlockSpec((bb, tb, LANES), tok(CD_SMALL // LANES)),
        pl.BlockSpec((1, LANES), lambda b, t: (0, 0)),
        pl.BlockSpec((1, LANES), lambda b, t: (0, 0)),
        pl.BlockSpec((1, D_VW), lambda b, t: (0, 0)),
        pl.BlockSpec((bb, D_HEADS, D_DK, D_DV), st4),
        pl.BlockSpec((bb, D_HEADS, D_DK), st3),
        pl.BlockSpec((bb, 1, D_HEADS), st3),
    ]
    out_specs = [
        pl.BlockSpec((bb, tb, D_VW), lambda b, t: (b, t, 0)),
        pl.BlockSpec((bb, D_HEADS, D_DK, D_DV), st4),
        pl.BlockSpec((bb, D_HEADS, D_DK), st3),
        pl.BlockSpec((bb, 1, D_HEADS), st3),
    ]
    state = bb * D_HEADS * D_DK * D_DV * 4
    blocks = bb * tb * (2 * D_KW + 3 * D_VW + LANES) * 4 + bb * tb * D_VW * 2 + 2 * state
    nchunks = tb // c
    return pl.pallas_call(
        functools.partial(_mlstm_kernel, c=c, nchunks=nchunks),
        out_shape=[jax.ShapeDtypeStruct((bsz, l, D_VW), BF16),
                   jax.ShapeDtypeStruct((bsz, D_HEADS, D_DK, D_DV), F32),
                   jax.ShapeDtypeStruct((bsz, D_HEADS, D_DK), F32),
                   jax.ShapeDtypeStruct((bsz, 1, D_HEADS), F32)],
        grid=(bsz // bb, nblk),
        in_specs=in_specs,
        out_specs=out_specs,
        scratch_shapes=[pltpu.VMEM((bb, D_HEADS, D_DK, D_DV), F32),
                        pltpu.VMEM((bb, D_HEADS, D_DK), F32),
                        pltpu.VMEM((bb, D_HEADS, 1, LANES), F32),
                        pltpu.VMEM((bb, tb, D_VW), F32),
                        pltpu.VMEM((bb, D_HEADS, tb, LANES), F32),
                        pltpu.VMEM((bb, D_HEADS, tb, LANES), F32),
                        pltpu.VMEM((bb, D_HEADS, tb, LANES), F32),
                        pltpu.VMEM((bb, nchunks, D_HEADS, D_DK, D_DV), F32),
                        pltpu.VMEM((bb, D_HEADS, _round_up(nchunks, SUBLANES), D_DK), F32),
                        pltpu.VMEM((bb, D_HEADS, _round_up(nchunks, SUBLANES), LANES), F32),
                        pltpu.VMEM((bb, D_HEADS, _round_up(nchunks, SUBLANES), LANES), F32)],
        compiler_params=pltpu.CompilerParams(
            dimension_semantics=("parallel", "arbitrary"),
            vmem_limit_bytes=_vmem_limit(blocks, state * (1 + nchunks) + bb * tb * (D_VW + 2 * LANES) * 4)),
        name="mlstm_mixer",
    )(proj3, proj3, proj3, proj3, proj3, proj3, ib_row, fb_row, norm_g_row, c0, n0, m0)


def _s5_expand_operators(kc_ref, bc_ref, cc_ref, bd_ref, bst_ref, cst_ref):
    tc, gt, sw, cg = S5_CHUNK, S5_GROUP_BLOCK, 2 * C_STATE, C_GROUP
    w_t = tc * LANES
    iota = lambda shape, d: lax.broadcasted_iota(jnp.int32, shape, d)
    row_g = (iota((w_t, LANES), 0) // cg) % gt
    tile16 = jnp.where(iota((cg, LANES), 1) % cg == iota((cg, LANES), 0), 1.0, 0.0).astype(BF16)
    bd = jnp.where(row_g == iota((w_t, LANES), 1) // cg, _mm(kc_ref[...], tile16), 0.0).astype(BF16)
    bd_ref[:, LANES:2 * LANES] = bd
    bd_ref[0:w_t - LANES, 0:LANES] = bd[LANES:, :]
    bd_ref[w_t - LANES:w_t, 0:LANES] = jnp.zeros((LANES, LANES), BF16)
    bc = bc_ref[...].astype(F32)
    for g in range(gt):
        bst_ref[:, g * sw:(g + 1) * sw] = jnp.where(row_g == g, bc, 0.0).astype(BF16)
    src, dst = iota((tc * cg, w_t), 0), iota((tc * cg, w_t), 1)
    spread = jnp.where((src // cg == dst // LANES) & (src % cg == dst % cg), 1.0, 0.0).astype(BF16)
    lane_g = (iota((sw, w_t), 1) // cg) % gt
    for g in range(gt):
        full = _mm(cc_ref[g * sw:(g + 1) * sw, :], spread)
        cst_ref[g * sw:(g + 1) * sw, :] = jnp.where(lane_g == g, full, 0.0).astype(BF16)


def _s5_kernel(u_ref, kc_ref, bc_ref, cc_ref, apow_ref, x0_ref, y_ref, xf_ref, bd_ref, bst_ref, cst_ref,
               *, nc, bb):
    m = nc * bb
    tc = S5_CHUNK
    sw = 2 * C_STATE

    @pl.when(pl.program_id(1) == 0)
    def _():
        _s5_expand_operators(kc_ref, bc_ref, cc_ref, bd_ref, bst_ref, cst_ref)

    row = lax.broadcasted_iota(jnp.int32, (m, sw), 0)
    n_idx = row & (nc - 1)
    n_log = int(math.log2(nc))

    def cmul(a1, a2, x):
        return a1 * x + a2 * pltpu.roll(x, C_STATE, 1)

    lhs = jnp.concatenate([u_ref[pl.ds(tau, m, stride=tc), :].astype(BF16) for tau in range(tc)], axis=1)
    e_all = _mm(lhs, bst_ref[...])
    groups = range(S5_GROUP_BLOCK)
    gsl = [slice(g * sw, (g + 1) * sw) for g in groups]
    x0_rows = []
    for g in groups:
        rows0 = jnp.zeros((m, sw), F32)
        for b in range(bb):
            rows0 = jnp.where(row == b * nc, x0_ref[b, :, gsl[g]], rows0)
        x0_rows.append(rows0)
    x = [e_all[:, gsl[g]] + cmul(apow_ref[0:1, gsl[g]], apow_ref[1:2, gsl[g]], x0_rows[g]) for g in groups]
    for j in range(n_log):
        sh = 1 << j
        shifted = [jnp.where(n_idx >= sh, pltpu.roll(x[g], sh, 0), 0.0) for g in groups]
        x = [x[g] + cmul(apow_ref[2 * j:2 * j + 1, gsl[g]], apow_ref[2 * j + 1:2 * j + 2, gsl[g]], shifted[g])
             for g in groups]
    x_start = [jnp.where(n_idx >= 1, pltpu.roll(x[g], 1, 0), x0_rows[g]).astype(BF16) for g in groups]
    for g in groups:
        for b in range(bb):
            xf_ref[b, :, gsl[g]] = x[g][b * nc + nc - 1:b * nc + nc, :]
    y_state = _mm(jnp.concatenate(x_start, axis=1), cst_ref[...])
    for tau in range(0, tc, 2):
        width = (tau + 2) * LANES
        start = (tc - 2 - tau) * LANES
        y = _mm(lhs[:, :width], bd_ref[start:start + width, :]) + y_state[:, tau * LANES:(tau + 2) * LANES]
        y_ref[pl.ds(tau, m, stride=tc), :] = y[:, :LANES]
        y_ref[pl.ds(tau + 1, m, stride=tc), :] = y[:, LANES:]


def _s5_chunks(proj2d, kc, bc, cc, apow, x0, *, l, bb):
    tc = S5_CHUNK
    nc = l // tc
    bsz = x0.shape[1]
    nt = C_GROUPS // S5_GROUP_BLOCK
    rows = bb * l
    sw = 2 * C_STATE
    sw_t = S5_GROUP_BLOCK * sw
    w_t = tc * LANES
    assert CD_U % LANES == 0 and bsz % bb == 0
    blocks = (2 * rows * LANES * 4 + 2 * w_t * LANES * 2 + sw_t * tc * C_GROUP * 2 + apow.shape[1] * sw_t * 4
              + 2 * bb * SUBLANES * sw_t * 4)
    scratch = w_t * 2 * LANES * 2 + 2 * w_t * sw_t * 2
    temps = bb * nc * (w_t * 2 + w_t * 4 + 3 * sw_t * 4) + 4 * sw * w_t * 4
    return pl.pallas_call(
        functools.partial(_s5_kernel, nc=nc, bb=bb),
        out_shape=[jax.ShapeDtypeStruct((bsz * l, C_W), F32),
                   jax.ShapeDtypeStruct((nt, bsz, 1, sw_t), F32)],
        grid=(nt, bsz // bb),
        in_specs=[pl.BlockSpec((rows, LANES), lambda i, j: (j, CD_U // LANES + i)),
                  pl.BlockSpec((None, w_t, C_GROUP), lambda i, j: (i, 0, 0)),
                  pl.BlockSpec((None, w_t, sw), lambda i, j: (i, 0, 0)),
                  pl.BlockSpec((None, sw_t, tc * C_GROUP), lambda i, j: (i, 0, 0)),
                  pl.BlockSpec((None, apow.shape[1], sw_t), lambda i, j: (i, 0, 0)),
                  pl.BlockSpec((None, bb, 1, sw_t), lambda i, j: (i, j, 0, 0))],
        out_specs=[pl.BlockSpec((rows, LANES), lambda i, j: (j, i)),
                   pl.BlockSpec((None, bb, 1, sw_t), lambda i, j: (i, j, 0, 0))],
        scratch_shapes=[pltpu.VMEM((w_t, 2 * LANES), BF16),
                        pltpu.VMEM((w_t, sw_t), BF16),
                        pltpu.VMEM((sw_t, w_t), BF16)],
        compiler_params=pltpu.CompilerParams(
            dimension_semantics=("parallel", "arbitrary"),
            vmem_limit_bytes=_vmem_limit(blocks, scratch + temps)),
        name="s5_chunks",
    )(proj2d, kc, bc, cc, apow, x0)


def _s5_operators(lam_re, lam_im, log_dt, b_re, b_im, c_re, c_im, n_log):
    g, p = lam_re.shape
    tc = S5_CHUNK
    dt = jnp.exp(log_dt.astype(F32))[:, None]
    mag = jnp.exp(lam_re * dt)
    ab_re, ab_im = mag * jnp.cos(lam_im * dt), mag * jnp.sin(lam_im * dt)
    den = lam_re * lam_re + lam_im * lam_im
    er = ab_re - 1.0
    zr = (er * lam_re + ab_im * lam_im) / den
    zi = (ab_im * lam_re - er * lam_im) / den
    b_re_t, b_im_t = b_re.swapaxes(1, 2), b_im.swapaxes(1, 2)
    bb_re = zr[:, None, :] * b_re_t - zi[:, None, :] * b_im_t
    bb_im = zr[:, None, :] * b_im_t + zi[:, None, :] * b_re_t
    pw_re, pw_im = ab_re[None], ab_im[None]
    while pw_re.shape[0] < tc:
        top_re, top_im = pw_re[-1], pw_im[-1]
        pw_re, pw_im = (jnp.concatenate([pw_re, top_re * pw_re - top_im * pw_im]),
                        jnp.concatenate([pw_im, top_re * pw_im + top_im * pw_re]))
    pw_re = jnp.concatenate([jnp.ones_like(ab_re)[None], pw_re])
    pw_im = jnp.concatenate([jnp.zeros_like(ab_im)[None], pw_im])
    rv_re, rv_im = pw_re[tc - 1::-1, :, None, :], pw_im[tc - 1::-1, :, None, :]
    abr = rv_re * bb_re - rv_im * bb_im
    abi = rv_re * bb_im + rv_im * bb_re
    kern = (jnp.einsum('gjp,tgip->tgij', c_re, abr, precision=HIGHEST)
            - jnp.einsum('gjp,tgip->tgij', c_im, abi, precision=HIGHEST))
    gt = S5_GROUP_BLOCK
    nt = g // gt
    sw = 2 * p
    kc = kern.reshape(tc, nt, gt * C_GROUP, C_GROUP).swapaxes(0, 1).reshape(nt, tc * gt * C_GROUP, C_GROUP)
    bc = jnp.concatenate([abr, abi], axis=3)
    bc = bc.reshape(tc, nt, gt * C_GROUP, sw).swapaxes(0, 1).reshape(nt, tc * gt * C_GROUP, sw)
    c_re_t, c_im_t = c_re.swapaxes(1, 2)[:, :, None, :], c_im.swapaxes(1, 2)[:, :, None, :]
    up_re = pw_re[1:].transpose(1, 2, 0)[..., None]
    up_im = pw_im[1:].transpose(1, 2, 0)[..., None]
    cr = c_re_t * up_re - c_im_t * up_im
    ci = -(c_re_t * up_im + c_im_t * up_re)
    cc = jnp.concatenate([cr, ci], axis=1).reshape(nt, gt * sw, tc * C_GROUP)
    r, i = pw_re[tc], pw_im[tc]
    rows = []
    for _ in range(max(n_log, 1)):
        rows += [jnp.concatenate([r, r], -1), jnp.concatenate([-i, i], -1)]
        r, i = r * r - i * i, 2.0 * r * i
    apow = jnp.stack(rows, axis=1)
    apow = apow.reshape(nt, gt, -1, sw).transpose(0, 2, 1, 3).reshape(nt, -1, gt * sw)
    return kc.astype(BF16), bc.astype(BF16), cc.astype(BF16), apow


def _s5(proj2d, ops, x0_re, x0_im, *, l):
    kc, bc, cc, apow = ops
    bsz = x0_re.shape[0]
    nt = C_GROUPS // S5_GROUP_BLOCK
    x0 = jnp.concatenate([x0_re, x0_im], axis=-1).reshape(bsz, nt, 1, -1).transpose(1, 0, 2, 3)
    bb = max(s for s in range(1, bsz + 1) if bsz % s == 0 and s * l <= S5_MAX_ROWS)
    y, xf = _s5_chunks(proj2d, kc, bc, cc, apow, x0, l=l, bb=bb)
    xf = xf.transpose(1, 0, 2, 3).reshape(bsz, C_GROUPS, 2 * C_STATE)
    return y, xf[..., :C_STATE], xf[..., C_STATE:]


AB_SRC_GA = AB_QKV
AB_SRC_QKV = AB_SRC_GA + A_GATE_RANK
AB_SRC_TAIL = AB_SRC_QKV + B_QKV + B_VW
IN_AB = AB_SRC_TAIL + 2 * B_HEADS
IN_CD = CD_SMALL + 2 * D_HEADS
WPREP_AB_TILE = 512
WPREP_CD_TILE = 640


def _prep_w_ab_kernel(wt_ref, ga_ref, o_ref):
    j = pl.program_id(0)
    tile = o_ref.shape[1]
    n_small = A_GATE_RANK + 2 * B_HEADS

    @pl.when(j < AB_SMALL // tile)
    def _():
        o_ref[...] = wt_ref[...].T.astype(BF16)

    @pl.when(j == AB_SMALL // tile)
    def _():
        rows = jnp.concatenate([ga_ref[...], wt_ref[tile - 2 * B_HEADS:tile, :],
                                jnp.zeros((tile - n_small, wt_ref.shape[1]), F32)], axis=0)
        o_ref[...] = rows.T.astype(BF16)

    @pl.when(j > AB_SMALL // tile)
    def _():
        o_ref[...] = jnp.zeros(o_ref.shape, BF16)


def _prep_w_ab(w_t, n_out):
    n_in, d = w_t.shape
    tile = WPREP_AB_TILE
    assert AB_QKV % tile == 0 and AB_SMALL % tile == 0 and n_out % tile == 0 and n_in >= tile

    unit = A_GATE_RANK
    assert tile % unit == 0 and (AB_SRC_QKV - AB_QKV) % unit == 0 and (n_in - tile) % unit == 0

    def src_row(j):
        k = j * (tile // unit)
        k = jnp.where(j < AB_QKV // tile, k,
                      jnp.where(j < AB_SMALL // tile, k + (AB_SRC_QKV - AB_QKV) // unit, (n_in - tile) // unit))
        return unit * k

    return pl.pallas_call(
        _prep_w_ab_kernel,
        out_shape=jax.ShapeDtypeStruct((d, n_out), BF16),
        grid=(n_out // tile,),
        in_specs=[pl.BlockSpec((pl.Element(tile), pl.Element(d)), lambda j: (src_row(j), 0)),
                  pl.BlockSpec((pl.Element(A_GATE_RANK), pl.Element(d)), lambda j: (AB_SRC_GA, 0))],
        out_specs=pl.BlockSpec((d, tile), lambda j: (0, j)),
        compiler_params=pltpu.CompilerParams(
            dimension_semantics=("parallel",),
            vmem_limit_bytes=_vmem_limit(tile * d * 6, 2 * tile * d * 4)),
        name="prep_w_in_ab",
    )(w_t, w_t)


def _prep_w_cd_kernel(wt_ref, o_ref):
    tile = o_ref.shape[1]
    row = pl.program_id(0) * tile + lax.broadcasted_iota(jnp.int32, wt_ref.shape, 0)
    o_ref[...] = jnp.where(row < IN_CD, wt_ref[...], 0.0).T.astype(BF16)


def _prep_w_cd(w_t, n_out):
    n_in, d = w_t.shape
    tile = WPREP_CD_TILE
    assert n_out % tile == 0
    return pl.pallas_call(
        _prep_w_cd_kernel,
        out_shape=jax.ShapeDtypeStruct((d, n_out), BF16),
        grid=(n_out // tile,),
        in_specs=[pl.BlockSpec((tile, d), lambda j: (j, 0))],
        out_specs=pl.BlockSpec((d, tile), lambda j: (0, j)),
        compiler_params=pltpu.CompilerParams(
            dimension_semantics=("parallel",),
            vmem_limit_bytes=_vmem_limit(tile * d * 6, 2 * tile * d * 4)),
        name="prep_w_in_cd",
    )(w_t)


def _lane_row(vals, lane0):
    return jnp.zeros((1, LANES), F32).at[0, lane0:lane0 + vals.shape[0]].set(vals.astype(F32))


def _prepare_weights(norm_g, final_norm_g, w_in_ab, a_gate_w, a_gate_b, a_norm_g, b_conv_w, b_a_log,
                     b_dt_bias, b_norm_g, w_out_ab, w_in_cd, c_lam_re, c_lam_im, c_log_dt, c_b_re,
                     c_b_im, c_c_re, c_c_im, c_d, c_glu_w, c_glu_b, d_i_bias, d_f_bias, d_norm_g,
                     w_out_cd, n_log):
    assert w_in_ab.shape[1] == IN_AB and w_in_cd.shape[1] == IN_CD
    w_ab = _prep_w_ab(w_in_ab.astype(F32).T, _round_up(AB_SMALL + LANES, PROJ_TN))
    w_cd = _prep_w_cd(w_in_cd.astype(F32).T, _round_up(CD_SMALL + LANES, PROJ_TN))
    gate_w = jnp.zeros((LANES, A_KW), F32).at[AB_GA_LANE:AB_GA_LANE + A_GATE_RANK].set(
        a_gate_w.astype(F32)).astype(BF16)
    return dict(
        norm_g=norm_g.astype(F32), final_g=final_norm_g.astype(F32)[None, :],
        w_ab=w_ab, w_cd=w_cd, gate_w=gate_w, gate_b=a_gate_b.astype(F32)[None, :],
        a_norm_g=a_norm_g.astype(F32)[None, :], conv_w=b_conv_w.astype(F32),
        alog=_lane_row(b_a_log, AB_APRE_LANE), dtb=_lane_row(b_dt_bias, AB_APRE_LANE),
        b_norm_g=b_norm_g.astype(F32)[None, :],
        w_out_ab=w_out_ab.astype(BF16),
        s5_ops=_s5_operators(c_lam_re.astype(F32), c_lam_im.astype(F32), c_log_dt, c_b_re.astype(F32),
                             c_b_im.astype(F32), c_c_re.astype(F32), c_c_im.astype(F32), n_log),
        c_d=c_d.astype(F32).reshape(1, C_W), glu_w=c_glu_w.astype(BF16),
        glu_b=c_glu_b.astype(F32)[None, :],
        ib=_lane_row(d_i_bias, CD_I_LANE), fb=_lane_row(d_f_bias, CD_F_LANE),
        d_norm_g=d_norm_g.astype(F32)[None, :],
        w_out_cd=w_out_cd.astype(BF16),
    )


def _trunk(x, conv_prev, s_gla0, s_gdn0, s5_re0, s5_im0, mc0, mn0, mm0, w):
    bsz, l, d = x.shape
    c = min(CHUNK, l)
    bb = max(s for s in (1, 2, 4) if s <= MIXER_STREAMS and bsz % s == 0)
    tb_s = min(l, (8 // bb) * c)
    n_s5 = l // S5_CHUNK
    assert l % tb_s == 0 and l % S5_CHUNK == 0 and n_s5 & (n_s5 - 1) == 0
    x2d = x.reshape(bsz * l, d)

    proj = _norm_matmul(x2d, w['norm_g'][0:1], w['w_ab'])
    proj3 = proj.reshape(bsz, l, proj.shape[1])
    o_a, s_gla = _gla(proj3, w['gate_w'], w['gate_b'], w['a_norm_g'], s_gla0.astype(F32), c=c, tb=tb_s, bb=bb)
    o_b, s_gdn = _gdn(proj3, conv_prev.astype(F32), w['conv_w'], w['alog'], w['dtb'], w['b_norm_g'],
                      s_gdn0.astype(F32), c=c, tb=tb_s, bb=bb)
    conv_new = proj3[:, l - (B_CONV - 1):, AB_QKV:AB_QKV + B_QKV]
    h1 = _out_proj(o_a.reshape(bsz * l, A_VW), o_b.reshape(bsz * l, B_VW), w['w_out_ab'], x2d)

    proj = _norm_matmul(h1, w['norm_g'][1:2], w['w_cd'])
    proj3 = proj.reshape(bsz, l, proj.shape[1])
    y, s5_re, s5_im = _s5(proj, w['s5_ops'], s5_re0.astype(F32), s5_im0.astype(F32), l=l)
    o_d, mc, mn, mm = _mlstm(proj3, w['ib'], w['fb'], w['d_norm_g'], mc0.astype(F32),
                             mn0.astype(F32), mm0.astype(F32)[:, None, :], c=c, tb=tb_s, bb=bb)
    y_out = _glu_out_proj_norm(y, proj, o_d.reshape(bsz * l, D_VW), w['c_d'], w['glu_w'], w['glu_b'],
                               w['w_out_cd'], h1, w['final_g'])
    dt = x.dtype
    return (y_out.reshape(bsz, l, d).astype(dt), conv_new.astype(dt), s_gla.astype(dt), s_gdn.astype(dt),
            s5_re.astype(dt), s5_im.astype(dt), mc.astype(dt), mn.astype(dt), mm[:, 0, :].astype(dt))


def kernel(x_prompt, x_sample, cache_gdn_conv, state_gla, state_gdn, state_s5_re, state_s5_im,
           state_mlstm_c, state_mlstm_n, state_mlstm_m, norm_g, final_norm_g, w_in_ab, a_gate_w,
           a_gate_b, a_norm_g, b_conv_w, b_a_log, b_dt_bias, b_norm_g, w_out_ab, w_in_cd, c_lam_re,
           c_lam_im, c_log_dt, c_b_re, c_b_im, c_c_re, c_c_im, c_d, c_glu_w, c_glu_b, d_i_bias,
           d_f_bias, d_norm_g, w_out_cd):
    n_log = int(math.log2(max(x_prompt.shape[1], x_sample.shape[1]) // S5_CHUNK))
    w = _prepare_weights(norm_g, final_norm_g, w_in_ab, a_gate_w, a_gate_b, a_norm_g, b_conv_w, b_a_log,
                         b_dt_bias, b_norm_g, w_out_ab, w_in_cd, c_lam_re, c_lam_im, c_log_dt, c_b_re,
                         c_b_im, c_c_re, c_c_im, c_d, c_glu_w, c_glu_b, d_i_bias, d_f_bias, d_norm_g,
                         w_out_cd, n_log)
    nb = x_prompt.shape[0]
    zeros = lambda *shape: jnp.zeros(shape, F32)
    p_out = _trunk(x_prompt, zeros(nb, B_CONV - 1, B_QKV), zeros(nb, A_HEADS, A_DK, A_DV),
                   zeros(nb, B_HEADS, B_DK, B_DV), zeros(nb, C_GROUPS, C_STATE), zeros(nb, C_GROUPS, C_STATE),
                   zeros(nb, D_HEADS, D_DK, D_DV), zeros(nb, D_HEADS, D_DK), zeros(nb, D_HEADS), w)
    s_out = _trunk(x_sample, cache_gdn_conv, state_gla, state_gdn, state_s5_re, state_s5_im,
                   state_mlstm_c, state_mlstm_n, state_mlstm_m, w)
    return (p_out[0], s_out[0]) + tuple(p_out[1:]) + tuple(s_out[1:])
```

```python
import functools
import math

import jax
import jax.numpy as jnp
from jax import lax
from jax.experimental import pallas as pl
from jax.experimental.pallas import tpu as pltpu

F32 = jnp.float32
BF16 = jnp.bfloat16
HIGHEST = lax.Precision.HIGHEST

NORM_EPS = 1e-6
CHUNK = 64
A_HEADS, A_DK, A_DV, A_GATE_RANK, A_GATE_TAU = 4, 128, 256, 16, 16.0
B_HEADS, B_DK, B_DV, B_CONV = 8, 128, 128, 4
C_GROUP, C_GROUPS, C_STATE = 16, 64, 64
D_HEADS, D_DK, D_DV = 4, 128, 256
A_KW, A_VW = A_HEADS * A_DK, A_HEADS * A_DV
B_KW, B_VW = B_HEADS * B_DK, B_HEADS * B_DV
B_QKV = 2 * B_KW + B_VW
C_W = C_GROUPS * C_GROUP
D_KW, D_VW = D_HEADS * D_DK, D_HEADS * D_DV

LANES = 128
SUBLANES = 8
VMEM_BYTES_V7X = 64 * 1024 * 1024

AB_Q, AB_K, AB_V, AB_Z = 0, A_KW, 2 * A_KW, 2 * A_KW + A_VW
AB_QKV = AB_Z + A_VW
AB_ZB = AB_QKV + B_QKV
AB_SMALL = AB_ZB + B_VW
AB_GA_LANE, AB_BETA_LANE, AB_APRE_LANE = 0, A_GATE_RANK, A_GATE_RANK + B_HEADS
CD_U, CD_Z = 0, C_W
CD_Q = 2 * C_W
CD_K = CD_Q + D_KW
CD_V = CD_K + D_KW
CD_O = CD_V + D_VW
CD_ZD = CD_O + D_VW
CD_SMALL = CD_ZD + D_VW
CD_I_LANE, CD_F_LANE = 0, D_HEADS

PROJ_TN = 1280
S5_CHUNK = 16
S5_GROUP_BLOCK = 8
S5_MAX_ROWS = 8192
NORM_ROW_SPLITS = 4
GLU_ROW_SPLITS = 2
MIXER_STREAMS = 4
MLSTM_INTRA_CHUNKS = 4
GDN_WY_CHAINS = 32


def _round_up(x, m):
    return (x + m - 1) // m * m


def _vmem_limit(block_bytes, scratch_bytes=0):
    est = 2 * block_bytes + scratch_bytes
    return int(min(max(2 * est, 32 * 1024 * 1024), VMEM_BYTES_V7X - 8 * 1024 * 1024))


def _mm(a, b):
    return jnp.dot(a, b, preferred_element_type=F32)


def _dot(a, b):
    return _mm(a.astype(BF16), b.astype(BF16))


def _dot_nt(a, b):
    return lax.dot_general(a.astype(BF16), b.astype(BF16), (((1,), (1,)), ((), ())),
                           preferred_element_type=F32)


def _dot_tn(a, b):
    return lax.dot_general(a.astype(BF16), b.astype(BF16), (((0,), (0,)), ((), ())),
                           preferred_element_type=F32)


def _split2(x):
    hi = x.astype(BF16)
    return hi, (x - hi.astype(F32)).astype(BF16)


def _split3(x):
    hi = x.astype(BF16)
    r = x - hi.astype(F32)
    mid = r.astype(BF16)
    return hi, mid, (r - mid.astype(F32)).astype(BF16)


def _cumsum_rows(tri_bf16, x):
    hi, mid, lo = _split3(x)
    return _mm(tri_bf16, hi) + _mm(tri_bf16, mid) + _mm(tri_bf16, lo)


def _select_rows(sel_bf16, x):
    nt = lambda b: lax.dot_general(sel_bf16, b, (((1,), (1,)), ((), ())), preferred_element_type=F32)
    hi, mid, lo = _split3(x)
    return nt(hi) + nt(mid) + nt(lo)


def _lane_selector(lane0):
    r = lax.broadcasted_iota(jnp.int32, (SUBLANES, LANES), 0)
    l = lax.broadcasted_iota(jnp.int32, (SUBLANES, LANES), 1)
    return jnp.where(l == r + lane0, 1.0, 0.0).astype(BF16)


def _causal_masks(c):
    row = lax.broadcasted_iota(jnp.int32, (c, c), 0)
    col = lax.broadcasted_iota(jnp.int32, (c, c), 1)
    return row >= col, row > col


def _norm_matmul_kernel(x_ref, g_ref, w_ref, o_ref, xn_ref):
    j = pl.program_id(1)

    @pl.when(j == 0)
    def _():
        tm = x_ref.shape[0]
        sub = tm // NORM_ROW_SPLITS
        for i in range(NORM_ROW_SPLITS):
            s = slice(i * sub, (i + 1) * sub)
            x = x_ref[s, :]
            y = (x * lax.rsqrt(jnp.mean(x * x, axis=-1, keepdims=True) + NORM_EPS) * g_ref[...]).astype(BF16)
            xn_ref[s, :] = y
            o_ref[s, :] = jnp.dot(y, w_ref[...], preferred_element_type=F32)

    @pl.when(j != 0)
    def _():
        o_ref[...] = jnp.dot(xn_ref[...], w_ref[...], preferred_element_type=F32)


def _norm_matmul(x2d, g_row, w_bf16):
    m, d = x2d.shape
    n = w_bf16.shape[1]
    tm = min(m, 1024)
    tn = PROJ_TN
    assert m % tm == 0 and n % tn == 0
    blocks = tm * d * 4 + d * tn * 2 + tm * tn * 4
    return pl.pallas_call(
        _norm_matmul_kernel,
        out_shape=jax.ShapeDtypeStruct((m, n), F32),
        grid=(m // tm, n // tn),
        in_specs=[pl.BlockSpec((tm, d), lambda i, j: (i, 0)),
                  pl.BlockSpec((1, d), lambda i, j: (0, 0)),
                  pl.BlockSpec((d, tn), lambda i, j: (0, j))],
        out_specs=pl.BlockSpec((tm, tn), lambda i, j: (i, j)),
        scratch_shapes=[pltpu.VMEM((tm, d), BF16)],
        compiler_params=pltpu.CompilerParams(
            dimension_semantics=("parallel", "arbitrary"),
            vmem_limit_bytes=_vmem_limit(blocks, tm * d * 2)),
        name="norm_in_proj",
    )(x2d, g_row, w_bf16)


def _out_proj_kernel(a_ref, b_ref, wa_ref, wb_ref, h_ref, o_ref):
    out = (jnp.dot(a_ref[...], wa_ref[...], preferred_element_type=F32)
           + jnp.dot(b_ref[...], wb_ref[...], preferred_element_type=F32))
    o_ref[...] = h_ref[...] + out


def _glu_out_proj_norm_kernel(y_ref, u_ref, z_ref, od_ref, d_ref, gw_ref, gb_ref, wc_ref, wd_ref, h_ref, g_ref,
                              o_ref):
    tm = y_ref.shape[0]
    sub = tm // GLU_ROW_SPLITS
    halves = [slice(i * sub, (i + 1) * sub) for i in range(GLU_ROW_SPLITS)]
    y = [jax.nn.gelu(y_ref[s, :] + d_ref[...] * u_ref[s, :]) for s in halves]
    gate = [jax.nn.sigmoid(jnp.dot(v.astype(BF16), gw_ref[...], preferred_element_type=F32) + gb_ref[...])
            for v in y]
    o_c = [(y[i] * gate[i] * jax.nn.silu(z_ref[s, :])).astype(BF16) for i, s in enumerate(halves)]
    out = [jnp.dot(o_c[i], wc_ref[...], preferred_element_type=F32)
           + jnp.dot(od_ref[s, :], wd_ref[...], preferred_element_type=F32) for i, s in enumerate(halves)]
    for i, s in enumerate(halves):
        h = h_ref[s, :] + out[i]
        o_ref[s, :] = h * lax.rsqrt(jnp.mean(h * h, axis=-1, keepdims=True) + NORM_EPS) * g_ref[...]


def _glu_out_proj_norm(y2d, proj2d, o_d, d_row, glu_w, glu_b_row, w_out, h2d, final_g_row):
    m, d = h2d.shape
    tm = min(m, 512)
    assert m % tm == 0 and C_W == D_VW and w_out.shape[0] == C_W + D_VW
    row = lambda col: (lambda i: (i, col))
    const = lambda i: (0, 0)
    resident = pl.Buffered(1)
    blocks = 3 * tm * C_W * 4 + tm * D_VW * 2 + 2 * tm * d * 4
    weights = C_W * C_W * 2 + (C_W + D_VW) * d * 2
    return pl.pallas_call(
        _glu_out_proj_norm_kernel,
        out_shape=jax.ShapeDtypeStruct((m, d), F32),
        grid=(m // tm,),
        in_specs=[pl.BlockSpec((tm, C_W), row(0)),
                  pl.BlockSpec((tm, C_W), row(CD_U // C_W)),
                  pl.BlockSpec((tm, C_W), row(CD_Z // C_W)),
                  pl.BlockSpec((tm, D_VW), row(0)),
                  pl.BlockSpec((1, C_W), const),
                  pl.BlockSpec((C_W, C_W), const, pipeline_mode=resident),
                  pl.BlockSpec((1, C_W), const),
                  pl.BlockSpec((C_W, d), const, pipeline_mode=resident),
                  pl.BlockSpec((D_VW, d), lambda i: (1, 0), pipeline_mode=resident),
                  pl.BlockSpec((tm, d), row(0)),
                  pl.BlockSpec((1, d), const)],
        out_specs=pl.BlockSpec((tm, d), row(0)),
        compiler_params=pltpu.CompilerParams(
            dimension_semantics=("parallel",),
            vmem_limit_bytes=_vmem_limit(blocks, weights + 3 * tm * C_W * 4)),
        name="glu_out_proj_norm",
    )(y2d, proj2d, proj2d, o_d, d_row, glu_w, glu_b_row, w_out, w_out, h2d, final_g_row)


def _out_proj(mix_a, mix_b, w_out, h2d):
    m, d = h2d.shape
    ka, kb = mix_a.shape[1], mix_b.shape[1]
    tm = min(m, 512)
    assert m % tm == 0 and ka == kb and w_out.shape[0] == ka + kb
    blocks = tm * (ka + kb) * 2 + (ka + kb) * d * 2 + 2 * tm * d * 4
    return pl.pallas_call(
        _out_proj_kernel,
        out_shape=jax.ShapeDtypeStruct((m, d), F32),
        grid=(m // tm,),
        in_specs=[pl.BlockSpec((tm, ka), lambda i: (i, 0)),
                  pl.BlockSpec((tm, kb), lambda i: (i, 0)),
                  pl.BlockSpec((ka, d), lambda i: (0, 0)),
                  pl.BlockSpec((kb, d), lambda i: (1, 0)),
                  pl.BlockSpec((tm, d), lambda i: (i, 0))],
        out_specs=pl.BlockSpec((tm, d), lambda i: (i, 0)),
        compiler_params=pltpu.CompilerParams(
            dimension_semantics=("parallel",),
            vmem_limit_bytes=_vmem_limit(blocks)),
        name="out_proj",
    )(mix_a, mix_b, w_out, w_out, h2d)


def _gla_kernel(q_ref, k_ref, v_ref, z_ref, sm_ref, gw_ref, gb_ref, ng_ref, s0_ref,
                o_ref, sout_ref, st_ref, *, c, nchunks):
    t = pl.program_id(1)
    last_t = pl.num_programs(1) - 1
    bb = q_ref.shape[0]
    chains = [(bi, h) for bi in range(bb) for h in range(A_HEADS)]
    ksl = [slice(h * A_DK, (h + 1) * A_DK) for _, h in chains]
    vsl = [slice(h * A_DV, (h + 1) * A_DV) for _, h in chains]
    idx = range(len(chains))

    @pl.when(t == 0)
    def _():
        for bi, h in chains:
            st_ref[bi, h] = s0_ref[bi, h].T

    causal, _ = _causal_masks(c)
    tri = causal.astype(BF16)
    gw = gw_ref[...]
    gb = gb_ref[...]

    def body(n, carry):
        sl = pl.ds(pl.multiple_of(n * c, c), c)
        b_all = [_cumsum_rows(tri, jax.nn.log_sigmoid(_dot(sm_ref[bi, sl, :], gw) + gb) * (1.0 / A_GATE_TAU))
                 for bi in range(bb)]
        b = [b_all[chains[i][0]][:, ksl[i]] for i in idx]
        b_last = [b[i][c - 1:c, :] for i in idx]
        k = [k_ref[chains[i][0], sl, ksl[i]] for i in idx]
        v = [v_ref[chains[i][0], sl, vsl[i]].astype(BF16) for i in idx]
        q_dec = [(q_ref[chains[i][0], sl, ksl[i]] * (A_DK ** -0.5) * jnp.exp(b[i])).astype(BF16) for i in idx]
        k_dec = [(k[i] * jnp.exp(-b[i])).astype(BF16) for i in idx]
        k_w = [(k[i] * jnp.exp(b_last[i] - b[i])).astype(BF16) for i in idx]
        scores = [jnp.where(causal, _dot_nt(q_dec[i], k_dec[i]), 0.0).astype(BF16) for i in idx]
        s_t = [st_ref[bi, h] for bi, h in chains]
        outs = [_mm(scores[i], v[i]) + _dot_nt(q_dec[i], s_t[i]) for i in idx]
        for i, (bi, h) in enumerate(chains):
            st_ref[bi, h] = s_t[i] * jnp.exp(b_last[i]) + _dot_tn(v[i], k_w[i])
        for i, (bi, h) in enumerate(chains):
            o = outs[i]
            o = o * lax.rsqrt(jnp.mean(o * o, axis=-1, keepdims=True) + NORM_EPS) * ng_ref[:, vsl[i]]
            o_ref[bi, sl, vsl[i]] = (o * jax.nn.silu(z_ref[bi, sl, vsl[i]])).astype(o_ref.dtype)
        return carry

    lax.fori_loop(0, nchunks, body, 0)

    @pl.when(t == last_t)
    def _():
        for bi, h in chains:
            sout_ref[bi, h] = st_ref[bi, h].T


def _gla(proj3, gate_w_pad, gate_b_row, norm_g_row, s0, *, c, tb, bb):
    bsz, l, _ = proj3.shape
    nblk = l // tb
    assert bsz % bb == 0
    tok = lambda col: (lambda b, t: (b, t, col))
    in_specs = [
        pl.BlockSpec((bb, tb, A_KW), tok(AB_Q // A_KW)),
        pl.BlockSpec((bb, tb, A_KW), tok(AB_K // A_KW)),
        pl.BlockSpec((bb, tb, A_VW), tok(AB_V // A_VW)),
        pl.BlockSpec((bb, tb, A_VW), tok(AB_Z // A_VW)),
        pl.BlockSpec((bb, tb, LANES), tok(AB_SMALL // LANES)),
        pl.BlockSpec((LANES, A_KW), lambda b, t: (0, 0)),
        pl.BlockSpec((1, A_KW), lambda b, t: (0, 0)),
        pl.BlockSpec((1, A_VW), lambda b, t: (0, 0)),
        pl.BlockSpec((bb, A_HEADS, A_DK, A_DV), lambda b, t: (b, 0, 0, 0)),
    ]
    out_specs = [
        pl.BlockSpec((bb, tb, A_VW), lambda b, t: (b, t, 0)),
        pl.BlockSpec((bb, A_HEADS, A_DK, A_DV), lambda b, t: (b, 0, 0, 0)),
    ]
    state = bb * A_HEADS * A_DK * A_DV * 4
    blocks = bb * tb * (2 * A_KW + 2 * A_VW + LANES) * 4 + bb * tb * A_VW * 2 + 2 * state
    return pl.pallas_call(
        functools.partial(_gla_kernel, c=c, nchunks=tb // c),
        out_shape=[jax.ShapeDtypeStruct((bsz, l, A_VW), BF16),
                   jax.ShapeDtypeStruct((bsz, A_HEADS, A_DK, A_DV), F32)],
        grid=(bsz // bb, nblk),
        in_specs=in_specs,
        out_specs=out_specs,
        scratch_shapes=[pltpu.VMEM((bb, A_HEADS, A_DV, A_DK), F32)],
        compiler_params=pltpu.CompilerParams(
            dimension_semantics=("parallel", "arbitrary"),
            vmem_limit_bytes=_vmem_limit(blocks, state)),
        name="gla_mixer",
    )(proj3, proj3, proj3, proj3, proj3, gate_w_pad, gate_b_row, norm_g_row, s0)


def _gdn_kernel(x_ref, z_ref, sm_ref, w_ref, cp_ref, alog_ref, dtb_ref, ng_ref, s0_ref,
                o_ref, sout_ref, s_ref, tail_ref, u_ref, wm_ref, qg_ref, kg_ref, qk_ref, gl_ref,
                *, c, nchunks):
    t = pl.program_id(1)
    last_t = pl.num_programs(1) - 1
    tb = c * nchunks
    keep = SUBLANES - (B_CONV - 1)
    bb = x_ref.shape[0]

    @pl.when(t == 0)
    def _():
        s_ref[...] = s0_ref[...]
        tail_ref[:, 0:keep, :] = jnp.zeros((bb, keep, B_QKV), F32)
        tail_ref[:, keep:SUBLANES, :] = cp_ref[...]

    def conv_silu(bi, row0, first, cols):
        x = x_ref[bi, pl.ds(row0, c), cols]
        if first is None:
            prev = x_ref[bi, pl.ds(pl.multiple_of(row0 - SUBLANES, SUBLANES), SUBLANES), cols]
        else:
            before = pl.multiple_of(jnp.maximum(row0 - SUBLANES, 0), SUBLANES)
            prev = jnp.where(first, tail_ref[bi, :, cols], x_ref[bi, pl.ds(before, SUBLANES), cols])
        w = w_ref[:, cols]
        ext = jnp.concatenate([prev, x], axis=0)
        ext1 = pltpu.roll(ext, 1, 0)
        newer = ext * w[3:4, :] + ext1 * w[2:3, :]
        older = ext * w[1:2, :] + ext1 * w[0:1, :]
        conv = newer + pltpu.roll(older, 2, 0)
        return jax.nn.silu(conv[SUBLANES:SUBLANES + c, :])

    causal, strict = _causal_masks(c)
    tri = causal.astype(BF16)
    eye = jnp.where(causal & jnp.logical_not(strict), 1.0, 0.0).astype(F32)
    sel = _lane_selector(AB_APRE_LANE)
    neg_a_exp = -jnp.exp(alog_ref[...])
    dtb = dtb_ref[...]
    ng = ng_ref[...]
    n_double = int(math.log2(c)) - 1
    heads = range(B_HEADS)

    group = max(1, GDN_WY_CHAINS // (bb * B_HEADS))
    group = group if nchunks % group == 0 else 1

    def wy_factors(n, carry):
        pws, rhss, where = [], [], []
        for bi, ci in [(bi, ci) for ci in range(group) for bi in range(bb)]:
            row0 = pl.multiple_of((n * group + ci) * c, c)
            first = (n == 0) if ci == 0 else None
            sl = pl.ds(row0, c)
            sm = sm_ref[bi, sl, :]
            g_cum = _cumsum_rows(tri, neg_a_exp * jax.nn.softplus(sm + dtb))
            g_rows = _select_rows(sel, g_cum)
            beta_all = jax.nn.sigmoid(sm)
            gl_ref[bi, pl.ds(n * group + ci, 1), :] = g_cum[c - 1:c, :]
            for h in heads:
                hs = slice(h * B_DK, (h + 1) * B_DK)
                q = conv_silu(bi, row0, first, hs)
                k = conv_silu(bi, row0, first, slice(B_KW + h * B_DK, B_KW + (h + 1) * B_DK))
                v = conv_silu(bi, row0, first, slice(2 * B_KW + h * B_DV, 2 * B_KW + (h + 1) * B_DV))
                q = q * lax.rsqrt(jnp.sum(q * q, axis=-1, keepdims=True) + NORM_EPS) * (B_DK ** -0.5)
                k = k * lax.rsqrt(jnp.sum(k * k, axis=-1, keepdims=True) + NORM_EPS)
                g_col = g_cum[:, AB_APRE_LANE + h:AB_APRE_LANE + h + 1]
                beta = beta_all[:, AB_BETA_LANE + h:AB_BETA_LANE + h + 1]
                decay = jnp.exp(jnp.where(causal, g_col - g_rows[h:h + 1, :], -jnp.inf))
                e_g = jnp.exp(g_col)
                k_beta = k * beta
                pws.append(-jnp.where(strict, _dot_nt(k_beta, k) * decay, 0.0))
                rhss.append(jnp.concatenate([v * beta, k_beta * e_g], axis=1))
                where.append((bi, sl, hs))
                qk_ref[bi, sl, h * LANES:h * LANES + c] = jnp.where(causal, _dot_nt(q, k) * decay,
                                                                    0.0).astype(BF16)
                qg_ref[bi, sl, hs] = (q * e_g).astype(BF16)
                kg_ref[bi, sl, hs] = (k * jnp.exp(g_col[c - 1:c, :] - g_col)).astype(BF16)
        items = range(len(pws))
        neg_lower = [_split2(p) for p in pws]
        invs = [eye + p for p in pws]
        for _ in range(n_double):
            pws = [_dot(p, p) for p in pws]
            invs = [i + _dot(i, p) for i, p in zip(invs, pws)]
        invs = [i.astype(BF16) for i in invs]
        sol = [_mm(invs[i], rhss[i].astype(BF16)).astype(BF16) for i in items]
        resid = [rhss[i] - sol[i].astype(F32) + _mm(neg_lower[i][0], sol[i]) + _mm(neg_lower[i][1], sol[i])
                 for i in items]
        for i in items:
            bi, sl, hs = where[i]
            uw = sol[i].astype(F32) + _mm(invs[i], resid[i].astype(BF16))
            u_ref[bi, sl, hs] = uw[:, :B_DV]
            wm_ref[bi, sl, hs] = uw[:, B_DV:].astype(BF16)
        return carry

    lax.fori_loop(0, nchunks // group, wy_factors, 0)
    tail_ref[...] = x_ref[:, tb - SUBLANES:tb, :]

    chains = [(bi, h) for bi in range(bb) for h in heads]
    hsl = [slice(h * B_DK, (h + 1) * B_DK) for _, h in chains]

    def recurrence(n, carry):
        sl = pl.ds(pl.multiple_of(n * c, c), c)
        e_last = [jnp.exp(gl_ref[bi, pl.ds(n, 1), :]) for bi in range(bb)]
        s_old = [s_ref[bi, h] for bi, h in chains]
        s_bf = [s.astype(BF16) for s in s_old]
        v_new = [(u_ref[bi, sl, hsl[i]] - _mm(wm_ref[bi, sl, hsl[i]], s_bf[i])).astype(BF16)
                 for i, (bi, h) in enumerate(chains)]
        outs = [_mm(qg_ref[bi, sl, hsl[i]], s_bf[i]) + _mm(qk_ref[bi, sl, h * LANES:h * LANES + c], v_new[i])
                for i, (bi, h) in enumerate(chains)]
        for i, (bi, h) in enumerate(chains):
            s_ref[bi, h] = (e_last[bi][:, AB_APRE_LANE + h:AB_APRE_LANE + h + 1] * s_old[i]
                            + lax.dot_general(kg_ref[bi, sl, hsl[i]], v_new[i], (((0,), (0,)), ((), ())),
                                              preferred_element_type=F32))
        for i, (bi, h) in enumerate(chains):
            o = outs[i]
            o = o * lax.rsqrt(jnp.mean(o * o, axis=-1, keepdims=True) + NORM_EPS) * ng
            o_ref[bi, sl, hsl[i]] = (o * jax.nn.silu(z_ref[bi, sl, hsl[i]])).astype(o_ref.dtype)
        return carry

    lax.fori_loop(0, nchunks, recurrence, 0)

    @pl.when(t == last_t)
    def _():
        sout_ref[...] = s_ref[...]


def _gdn(proj3, conv_prev, conv_w, alog_row, dtb_row, norm_g_row, s0, *, c, tb, bb):
    bsz, l, _ = proj3.shape
    nblk = l // tb
    assert AB_QKV % B_QKV == 0 and AB_ZB % B_VW == 0 and bsz % bb == 0
    in_specs = [
        pl.BlockSpec((bb, tb, B_QKV), lambda b, t: (b, t, AB_QKV // B_QKV)),
        pl.BlockSpec((bb, tb, B_VW), lambda b, t: (b, t, AB_ZB // B_VW)),
        pl.BlockSpec((bb, tb, LANES), lambda b, t: (b, t, AB_SMALL // LANES)),
        pl.BlockSpec((B_CONV, B_QKV), lambda b, t: (0, 0)),
        pl.BlockSpec((bb, B_CONV - 1, B_QKV), lambda b, t: (b, 0, 0)),
        pl.BlockSpec((1, LANES), lambda b, t: (0, 0)),
        pl.BlockSpec((1, LANES), lambda b, t: (0, 0)),
        pl.BlockSpec((1, B_DV), lambda b, t: (0, 0)),
        pl.BlockSpec((bb, B_HEADS, B_DK, B_DV), lambda b, t: (b, 0, 0, 0)),
    ]
    out_specs = [
        pl.BlockSpec((bb, tb, B_VW), lambda b, t: (b, t, 0)),
        pl.BlockSpec((bb, B_HEADS, B_DK, B_DV), lambda b, t: (b, 0, 0, 0)),
    ]
    blocks = bb * (tb * (B_QKV + B_VW + LANES) * 4 + tb * B_VW * 2 + 2 * B_HEADS * B_DK * B_DV * 4)
    scratch = bb * (B_HEADS * B_DK * B_DV * 4 + SUBLANES * B_QKV * 4 + tb * B_VW * 4
                    + 4 * tb * B_KW * 2 + SUBLANES * LANES * 4)
    return pl.pallas_call(
        functools.partial(_gdn_kernel, c=c, nchunks=tb // c),
        out_shape=[jax.ShapeDtypeStruct((bsz, l, B_VW), BF16),
                   jax.ShapeDtypeStruct((bsz, B_HEADS, B_DK, B_DV), F32)],
        grid=(bsz // bb, nblk),
        in_specs=in_specs,
        out_specs=out_specs,
        scratch_shapes=[pltpu.VMEM((bb, B_HEADS, B_DK, B_DV), F32),
                        pltpu.VMEM((bb, SUBLANES, B_QKV), F32),
                        pltpu.VMEM((bb, tb, B_VW), F32),
                        pltpu.VMEM((bb, tb, B_KW), BF16),
                        pltpu.VMEM((bb, tb, B_KW), BF16),
                        pltpu.VMEM((bb, tb, B_KW), BF16),
                        pltpu.VMEM((bb, tb, B_HEADS * LANES), BF16),
                        pltpu.VMEM((bb, SUBLANES, LANES), F32)],
        compiler_params=pltpu.CompilerParams(
            dimension_semantics=("parallel", "arbitrary"),
            vmem_limit_bytes=_vmem_limit(blocks, scratch)),
        name="gdn_mixer",
    )(proj3, proj3, proj3, conv_w, conv_prev, alog_row, dtb_row, norm_g_row, s0)


def _mlstm_kernel(q_ref, k_ref, v_ref, og_ref, z_ref, sm_ref, ib_ref, fb_ref, ng_ref,
                  c0_ref, n0_ref, m0_ref, o_ref, cout_ref, nout_ref, mout_ref,
                  c_ref, n_ref, m_ref, hi_ref, mi_ref, ni_ref, bc_ref, kv_ref, ks_ref, mc_ref, bl_ref,
                  *, c, nchunks):
    t = pl.program_id(1)
    last_t = pl.num_programs(1) - 1
    bb = q_ref.shape[0]
    chains = [(bi, h) for bi in range(bb) for h in range(D_HEADS)]
    ksl = [slice(h * D_DK, (h + 1) * D_DK) for _, h in chains]
    vsl = [slice(h * D_DV, (h + 1) * D_DV) for _, h in chains]
    idx = range(len(chains))

    @pl.when(t == 0)
    def _():
        c_ref[...] = c0_ref[...]
        n_ref[...] = n0_ref[...]
        for bi, h in chains:
            m_ref[bi, h] = jnp.broadcast_to(m0_ref[bi, :, h:h + 1], (1, LANES))

    causal, _ = _causal_masks(c)
    tri = causal.astype(BF16)
    sel = _lane_selector(0)
    lane = lax.broadcasted_iota(jnp.int32, (c, LANES), 1)
    ib = ib_ref[...]
    fb = fb_ref[...]

    group = MLSTM_INTRA_CHUNKS if nchunks % MLSTM_INTRA_CHUNKS == 0 else 1

    def intra(n, carry):
        items = [(bi, ci, h) for ci in range(group) for bi, h in chains]
        sls = [pl.ds(pl.multiple_of((n * group + ci) * c, c), c) for ci in range(group)]
        sm = {(bi, ci): sm_ref[bi, sls[ci], :] for ci in range(group) for bi in range(bb)}
        i_full = {key: x + ib for key, x in sm.items()}
        b_full = {key: _cumsum_rows(tri, jax.nn.log_sigmoid(x + fb)) for key, x in sm.items()}
        rows = {key: _select_rows(sel, jnp.where(lane < CD_F_LANE, i_full[key], b_full[key]))
                for key in sm}
        b_col = [b_full[bi, ci][:, CD_F_LANE + h:CD_F_LANE + h + 1] for bi, ci, h in items]
        i_col = [i_full[bi, ci][:, CD_I_LANE + h:CD_I_LANE + h + 1] for bi, ci, h in items]
        logw = [jnp.where(causal, b_col[i] - rows[bi, ci][CD_F_LANE + h:CD_F_LANE + h + 1, :]
                          + rows[bi, ci][CD_I_LANE + h:CD_I_LANE + h + 1, :], -jnp.inf)
                for i, (bi, ci, h) in enumerate(items)]
        ids = range(len(items))
        m_intra = [jnp.max(logw[i], axis=-1, keepdims=True) for i in ids]
        ks_ = [slice(h * D_DK, (h + 1) * D_DK) for _, _, h in items]
        vs_ = [slice(h * D_DV, (h + 1) * D_DV) for _, _, h in items]
        q_bf = [(q_ref[bi, sls[ci], ks_[i]] * (D_DK ** -0.5)).astype(BF16) for i, (bi, ci, h) in enumerate(items)]
        k = [k_ref[bi, sls[ci], ks_[i]] for i, (bi, ci, h) in enumerate(items)]
        v = [v_ref[bi, sls[ci], vs_[i]].astype(BF16) for i, (bi, ci, h) in enumerate(items)]
        p = [jnp.exp(logw[i] - m_intra[i]) * _dot_nt(q_bf[i], k[i]) for i in ids]
        n_intra = [jnp.sum(p[i], axis=-1, keepdims=True) for i in ids]
        m_chunk = [m_intra[i][c - 1:c, :] for i in ids]
        k_w = [k[i] * jnp.exp(b_col[i][c - 1:c, :] - b_col[i] + i_col[i] - m_chunk[i]) for i in ids]
        rep = lambda x: jnp.broadcast_to(x, (x.shape[0], LANES))
        for i, (bi, ci, h) in enumerate(items):
            row = pl.ds(n * group + ci, 1)
            hi_ref[bi, sls[ci], vs_[i]] = _dot(p[i], v[i])
            kv_ref[bi, n * group + ci, h] = _dot_tn(k_w[i], v[i])
            ks_ref[bi, h, row, :] = jnp.sum(k_w[i], axis=0, keepdims=True)
            mi_ref[bi, h, sls[ci], :] = rep(m_intra[i])
            ni_ref[bi, h, sls[ci], :] = rep(n_intra[i])
            bc_ref[bi, h, sls[ci], :] = rep(b_col[i])
            mc_ref[bi, h, row, :] = rep(m_chunk[i])
            bl_ref[bi, h, row, :] = rep(b_col[i][c - 1:c, :])
        return carry

    lax.fori_loop(0, nchunks // group, intra, 0)

    twice = lambda x: jnp.concatenate([x, x], axis=1)

    def body(n, carry):
        sl = pl.ds(pl.multiple_of(n * c, c), c)
        row = pl.ds(n, 1)
        q = [q_ref[chains[i][0], sl, ksl[i]] * (D_DK ** -0.5) for i in idx]
        c_s = [c_ref[bi, h] for bi, h in chains]
        n_s = [n_ref[bi, h:h + 1, :] for bi, h in chains]
        m_s = [m_ref[bi, h] for bi, h in chains]
        qc = [_dot(q[i], c_s[i]) for i in idx]
        b_last = [bl_ref[bi, h, row, :] for bi, h in chains]
        m_chunk = [mc_ref[bi, h, row, :] for bi, h in chains]
        for i, (bi, h) in enumerate(chains):
            m_new = jnp.maximum(b_last[i] + m_s[i], m_chunk[i])
            w_old = jnp.exp(b_last[i] + m_s[i] - m_new)
            w_new = jnp.exp(m_chunk[i] - m_new)
            c_ref[bi, h] = twice(w_old) * c_s[i] + twice(w_new) * kv_ref[bi, n, h]
            n_ref[bi, h:h + 1, :] = w_old * n_s[i] + w_new * ks_ref[bi, h, row, :]
            m_ref[bi, h] = m_new
        for i, (bi, h) in enumerate(chains):
            m_intra = mi_ref[bi, h, sl, :]
            a = bc_ref[bi, h, sl, :] + m_s[i]
            m_t = jnp.maximum(a, m_intra)
            w_a = jnp.exp(a - m_t)
            w_i = jnp.exp(m_intra - m_t)
            num = twice(w_a) * qc[i] + twice(w_i) * hi_ref[bi, sl, vsl[i]]
            den = w_a * jnp.sum(q[i] * n_s[i], axis=-1, keepdims=True) + w_i * ni_ref[bi, h, sl, :]
            hh = num / twice(jnp.maximum(jnp.abs(den), jnp.exp(-m_t)))
            hd = jax.nn.sigmoid(og_ref[bi, sl, vsl[i]]) * hh
            oc = hd - jnp.mean(hd, axis=-1, keepdims=True)
            o = oc * lax.rsqrt(jnp.mean(oc * oc, axis=-1, keepdims=True) + NORM_EPS) * ng_ref[:, vsl[i]]
            o_ref[bi, sl, vsl[i]] = (o * jax.nn.silu(z_ref[bi, sl, vsl[i]])).astype(o_ref.dtype)
        return carry

    lax.fori_loop(0, nchunks, body, 0)

    @pl.when(t == last_t)
    def _():
        cout_ref[...] = c_ref[...]
        nout_ref[...] = n_ref[...]
        for bi, h in chains:
            mout_ref[bi, :, h:h + 1] = m_ref[bi, h][:, 0:1]


def _mlstm(proj3, ib_row, fb_row, norm_g_row, c0, n0, m0, *, c, tb, bb):
    bsz, l, _ = proj3.shape
    nblk = l // tb
    assert bsz % bb == 0
    tok = lambda col: (lambda b, t: (b, t, col))
    st4 = lambda b, t: (b, 0, 0, 0)
    st3 = lambda b, t: (b, 0, 0)
    in_specs = [
        pl.BlockSpec((bb, tb, D_KW), tok(CD_Q // D_KW)),
        pl.BlockSpec((bb, tb, D_KW), tok(CD_K // D_KW)),
        pl.BlockSpec((bb, tb, D_VW), tok(CD_V // D_VW)),
        pl.BlockSpec((bb, tb, D_VW), tok(CD_O // D_VW)),
        pl.BlockSpec((bb, tb, D_VW), tok(CD_ZD // D_VW)),
        pl.BlockSpec((bb, tb, LANES), tok(CD_SMALL // LANES)),
        pl.BlockSpec((1, LANES), lambda b, t: (0, 0)),
        pl.BlockSpec((1, LANES), lambda b, t: (0, 0)),
        pl.BlockSpec((1, D_VW), lambda b, t: (0, 0)),
        pl.BlockSpec((bb, D_HEADS, D_DK, D_DV), st4),
        pl.BlockSpec((bb, D_HEADS, D_DK), st3),
        pl.BlockSpec((bb, 1, D_HEADS), st3),
    ]
    out_specs = [
        pl.BlockSpec((bb, tb, D_VW), lambda b, t: (b, t, 0)),
        pl.BlockSpec((bb, D_HEADS, D_DK, D_DV), st4),
        pl.BlockSpec((bb, D_HEADS, D_DK), st3),
        pl.BlockSpec((bb, 1, D_HEADS), st3),
    ]
    state = bb * D_HEADS * D_DK * D_DV * 4
    blocks = bb * tb * (2 * D_KW + 3 * D_VW + LANES) * 4 + bb * tb * D_VW * 2 + 2 * state
    nchunks = tb // c
    return pl.pallas_call(
        functools.partial(_mlstm_kernel, c=c, nchunks=nchunks),
        out_shape=[jax.ShapeDtypeStruct((bsz, l, D_VW), BF16),
                   jax.ShapeDtypeStruct((bsz, D_HEADS, D_DK, D_DV), F32),
                   jax.ShapeDtypeStruct((bsz, D_HEADS, D_DK), F32),
                   jax.ShapeDtypeStruct((bsz, 1, D_HEADS), F32)],
        grid=(bsz // bb, nblk),
        in_specs=in_specs,
        out_specs=out_specs,
        scratch_shapes=[pltpu.VMEM((bb, D_HEADS, D_DK, D_DV), F32),
                        pltpu.VMEM((bb, D_HEADS, D_DK), F32),
                        pltpu.VMEM((bb, D_HEADS, 1, LANES), F32),
                        pltpu.VMEM((bb, tb, D_VW), F32),
                        pltpu.VMEM((bb, D_HEADS, tb, LANES), F32),
                        pltpu.VMEM((bb, D_HEADS, tb, LANES), F32),
                        pltpu.VMEM((bb, D_HEADS, tb, LANES), F32),
                        pltpu.VMEM((bb, nchunks, D_HEADS, D_DK, D_DV), F32),
                        pltpu.VMEM((bb, D_HEADS, _round_up(nchunks, SUBLANES), D_DK), F32),
                        pltpu.VMEM((bb, D_HEADS, _round_up(nchunks, SUBLANES), LANES), F32),
                        pltpu.VMEM((bb, D_HEADS, _round_up(nchunks, SUBLANES), LANES), F32)],
        compiler_params=pltpu.CompilerParams(
            dimension_semantics=("parallel", "arbitrary"),
            vmem_limit_bytes=_vmem_limit(blocks, state * (1 + nchunks) + bb * tb * (D_VW + 2 * LANES) * 4)),
        name="mlstm_mixer",
    )(proj3, proj3, proj3, proj3, proj3, proj3, ib_row, fb_row, norm_g_row, c0, n0, m0)


def _s5_expand_operators(kc_ref, bc_ref, cc_ref, bd_ref, bst_ref, cst_ref):
    tc, gt, sw, cg = S5_CHUNK, S5_GROUP_BLOCK, 2 * C_STATE, C_GROUP
    w_t = tc * LANES
    iota = lambda shape, d: lax.broadcasted_iota(jnp.int32, shape, d)
    row_g = (iota((w_t, LANES), 0) // cg) % gt
    tile16 = jnp.where(iota((cg, LANES), 1) % cg == iota((cg, LANES), 0), 1.0, 0.0).astype(BF16)
    bd = jnp.where(row_g == iota((w_t, LANES), 1) // cg, _mm(kc_ref[...].reshape(w_t, cg), tile16),
                   0.0).astype(BF16)
    bd_ref[:, LANES:2 * LANES] = bd
    bd_ref[0:w_t - LANES, 0:LANES] = bd[LANES:, :]
    bd_ref[w_t - LANES:w_t, 0:LANES] = jnp.zeros((LANES, LANES), BF16)
    bc = bc_ref[...].reshape(w_t, sw).astype(F32)
    for g in range(gt):
        bst_ref[:, g * sw:(g + 1) * sw] = jnp.where(row_g == g, bc, 0.0).astype(BF16)
    src, dst = iota((tc * cg, w_t), 0), iota((tc * cg, w_t), 1)
    spread = jnp.where((src // cg == dst // LANES) & (src % cg == dst % cg), 1.0, 0.0).astype(BF16)
    lane_g = (iota((sw, w_t), 1) // cg) % gt
    for g in range(gt):
        full = _mm(cc_ref[g * sw:(g + 1) * sw, :], spread)
        cst_ref[g * sw:(g + 1) * sw, :] = jnp.where(lane_g == g, full, 0.0).astype(BF16)


def _s5_kernel(u_ref, kc_ref, bc_ref, cc_ref, apow_ref, x0_ref, y_ref, xf_ref, bd_ref, bst_ref, cst_ref,
               *, nc, bb):
    m = nc * bb
    tc = S5_CHUNK
    sw = 2 * C_STATE

    @pl.when(pl.program_id(1) == 0)
    def _():
        _s5_expand_operators(kc_ref, bc_ref, cc_ref, bd_ref, bst_ref, cst_ref)

    row = lax.broadcasted_iota(jnp.int32, (m, sw), 0)
    n_idx = row & (nc - 1)
    n_log = int(math.log2(nc))

    def cmul(a1, a2, x):
        return a1 * x + a2 * pltpu.roll(x, C_STATE, 1)

    lhs = jnp.concatenate([u_ref[pl.ds(tau, m, stride=tc), :].astype(BF16) for tau in range(tc)], axis=1)
    e_all = _mm(lhs, bst_ref[...])
    groups = range(S5_GROUP_BLOCK)
    gsl = [slice(g * sw, (g + 1) * sw) for g in groups]
    x0_rows = []
    for g in groups:
        rows0 = jnp.zeros((m, sw), F32)
        for b in range(bb):
            rows0 = jnp.where(row == b * nc, x0_ref[b, :, gsl[g]], rows0)
        x0_rows.append(rows0)
    x = [e_all[:, gsl[g]] + cmul(apow_ref[0:1, gsl[g]], apow_ref[1:2, gsl[g]], x0_rows[g]) for g in groups]
    for j in range(n_log):
        sh = 1 << j
        shifted = [jnp.where(n_idx >= sh, pltpu.roll(x[g], sh, 0), 0.0) for g in groups]
        x = [x[g] + cmul(apow_ref[2 * j:2 * j + 1, gsl[g]], apow_ref[2 * j + 1:2 * j + 2, gsl[g]], shifted[g])
             for g in groups]
    x_start = [jnp.where(n_idx >= 1, pltpu.roll(x[g], 1, 0), x0_rows[g]).astype(BF16) for g in groups]
    for g in groups:
        for b in range(bb):
            xf_ref[b, :, gsl[g]] = x[g][b * nc + nc - 1:b * nc + nc, :]
    y_state = _mm(jnp.concatenate(x_start, axis=1), cst_ref[...])
    for tau in range(0, tc, 2):
        width = (tau + 2) * LANES
        start = (tc - 2 - tau) * LANES
        y = _mm(lhs[:, :width], bd_ref[start:start + width, :]) + y_state[:, tau * LANES:(tau + 2) * LANES]
        y_ref[pl.ds(tau, m, stride=tc), :] = y[:, :LANES]
        y_ref[pl.ds(tau + 1, m, stride=tc), :] = y[:, LANES:]


def _s5_chunks(proj2d, kc, bc, cc, apow, x0, *, l, bb):
    tc = S5_CHUNK
    nc = l // tc
    bsz = x0.shape[1]
    nt = C_GROUPS // S5_GROUP_BLOCK
    rows = bb * l
    sw = 2 * C_STATE
    sw_t = S5_GROUP_BLOCK * sw
    w_t = tc * LANES
    assert CD_U % LANES == 0 and bsz % bb == 0 and S5_GROUP_BLOCK * C_GROUP == LANES
    blocks = (2 * rows * LANES * 4 + 2 * w_t * LANES * 2 + sw_t * tc * C_GROUP * 2 + apow.shape[1] * sw_t * 4
              + 2 * bb * SUBLANES * sw_t * 4)
    scratch = w_t * 2 * LANES * 2 + 2 * w_t * sw_t * 2
    temps = bb * nc * (w_t * 2 + w_t * 4 + 3 * sw_t * 4) + 4 * sw * w_t * 4
    return pl.pallas_call(
        functools.partial(_s5_kernel, nc=nc, bb=bb),
        out_shape=[jax.ShapeDtypeStruct((bsz * l, C_W), F32),
                   jax.ShapeDtypeStruct((nt, bsz, 1, sw_t), F32)],
        grid=(nt, bsz // bb),
        in_specs=[pl.BlockSpec((rows, LANES), lambda i, j: (j, CD_U // LANES + i)),
                  pl.BlockSpec((tc, None, LANES, C_GROUP), lambda i, j: (0, i, 0, 0)),
                  pl.BlockSpec((tc, None, LANES, sw), lambda i, j: (0, i, 0, 0)),
                  pl.BlockSpec((None, sw_t, tc * C_GROUP), lambda i, j: (i, 0, 0)),
                  pl.BlockSpec((None, apow.shape[1], sw_t), lambda i, j: (i, 0, 0)),
                  pl.BlockSpec((None, bb, 1, sw_t), lambda i, j: (i, j, 0, 0))],
        out_specs=[pl.BlockSpec((rows, LANES), lambda i, j: (j, i)),
                   pl.BlockSpec((None, bb, 1, sw_t), lambda i, j: (i, j, 0, 0))],
        scratch_shapes=[pltpu.VMEM((w_t, 2 * LANES), BF16),
                        pltpu.VMEM((w_t, sw_t), BF16),
                        pltpu.VMEM((sw_t, w_t), BF16)],
        compiler_params=pltpu.CompilerParams(
            dimension_semantics=("parallel", "arbitrary"),
            vmem_limit_bytes=_vmem_limit(blocks, scratch + temps)),
        name="s5_chunks",
    )(proj2d, kc, bc, cc, apow, x0)


def _s5_operators(lam_re, lam_im, log_dt, b_re, b_im, c_re, c_im, n_log):
    g, p = lam_re.shape
    tc = S5_CHUNK
    dt = jnp.exp(log_dt.astype(F32))[:, None]
    mag = jnp.exp(lam_re * dt)
    ab_re, ab_im = mag * jnp.cos(lam_im * dt), mag * jnp.sin(lam_im * dt)
    den = lam_re * lam_re + lam_im * lam_im
    er = ab_re - 1.0
    zr = (er * lam_re + ab_im * lam_im) / den
    zi = (ab_im * lam_re - er * lam_im) / den
    b_re_t, b_im_t = b_re.swapaxes(1, 2), b_im.swapaxes(1, 2)
    bb_re = zr[:, None, :] * b_re_t - zi[:, None, :] * b_im_t
    bb_im = zr[:, None, :] * b_im_t + zi[:, None, :] * b_re_t
    pw_re, pw_im = ab_re[None], ab_im[None]
    while pw_re.shape[0] < tc:
        top_re, top_im = pw_re[-1], pw_im[-1]
        pw_re, pw_im = (jnp.concatenate([pw_re, top_re * pw_re - top_im * pw_im]),
                        jnp.concatenate([pw_im, top_re * pw_im + top_im * pw_re]))
    pw_re = jnp.concatenate([jnp.ones_like(ab_re)[None], pw_re])
    pw_im = jnp.concatenate([jnp.zeros_like(ab_im)[None], pw_im])
    rv_re, rv_im = pw_re[tc - 1::-1, :, None, :], pw_im[tc - 1::-1, :, None, :]
    abr = rv_re * bb_re - rv_im * bb_im
    abi = rv_re * bb_im + rv_im * bb_re
    kern = (jnp.einsum('gjp,tgip->tgij', c_re, abr, precision=HIGHEST)
            - jnp.einsum('gjp,tgip->tgij', c_im, abi, precision=HIGHEST))
    gt = S5_GROUP_BLOCK
    nt = g // gt
    sw = 2 * p
    kc = kern.reshape(tc, nt, gt * C_GROUP, C_GROUP)
    bc = jnp.concatenate([abr, abi], axis=3).reshape(tc, nt, gt * C_GROUP, sw)
    c_re_t, c_im_t = c_re.swapaxes(1, 2)[:, :, None, :], c_im.swapaxes(1, 2)[:, :, None, :]
    up_re = pw_re[1:].transpose(1, 2, 0)[..., None]
    up_im = pw_im[1:].transpose(1, 2, 0)[..., None]
    cr = c_re_t * up_re - c_im_t * up_im
    ci = -(c_re_t * up_im + c_im_t * up_re)
    cc = jnp.concatenate([cr, ci], axis=1).reshape(nt, gt * sw, tc * C_GROUP)
    r, i = pw_re[tc], pw_im[tc]
    rows = []
    for _ in range(max(n_log, 1)):
        rows += [jnp.concatenate([r, r], -1), jnp.concatenate([-i, i], -1)]
        r, i = r * r - i * i, 2.0 * r * i
    apow = jnp.stack(rows, axis=1)
    apow = apow.reshape(nt, gt, -1, sw).transpose(0, 2, 1, 3).reshape(nt, -1, gt * sw)
    return kc.astype(BF16), bc.astype(BF16), cc.astype(BF16), apow


def _s5(proj2d, ops, x0_re, x0_im, *, l):
    kc, bc, cc, apow = ops
    bsz = x0_re.shape[0]
    nt = C_GROUPS // S5_GROUP_BLOCK
    x0 = jnp.concatenate([x0_re, x0_im], axis=-1).reshape(bsz, nt, 1, -1).transpose(1, 0, 2, 3)
    bb = max(s for s in range(1, bsz + 1) if bsz % s == 0 and s * l <= S5_MAX_ROWS)
    y, xf = _s5_chunks(proj2d, kc, bc, cc, apow, x0, l=l, bb=bb)
    xf = xf.transpose(1, 0, 2, 3).reshape(bsz, C_GROUPS, 2 * C_STATE)
    return y, xf[..., :C_STATE], xf[..., C_STATE:]


AB_SRC_GA = AB_QKV
AB_SRC_QKV = AB_SRC_GA + A_GATE_RANK
AB_SRC_TAIL = AB_SRC_QKV + B_QKV + B_VW
IN_AB = AB_SRC_TAIL + 2 * B_HEADS
IN_CD = CD_SMALL + 2 * D_HEADS
WPREP_AB_TILE = 512
WPREP_CD_TILE = 640


def _prep_w_ab_kernel(wt_ref, ga_ref, o_ref):
    j = pl.program_id(0)
    tile = o_ref.shape[1]
    n_small = A_GATE_RANK + 2 * B_HEADS

    @pl.when(j < AB_SMALL // tile)
    def _():
        o_ref[...] = wt_ref[...].T.astype(BF16)

    @pl.when(j == AB_SMALL // tile)
    def _():
        rows = jnp.concatenate([ga_ref[...], wt_ref[tile - 2 * B_HEADS:tile, :],
                                jnp.zeros((tile - n_small, wt_ref.shape[1]), F32)], axis=0)
        o_ref[...] = rows.T.astype(BF16)

    @pl.when(j > AB_SMALL // tile)
    def _():
        o_ref[...] = jnp.zeros(o_ref.shape, BF16)


def _prep_w_ab(w_t, n_out):
    n_in, d = w_t.shape
    tile = WPREP_AB_TILE
    assert AB_QKV % tile == 0 and AB_SMALL % tile == 0 and n_out % tile == 0 and n_in >= tile

    unit = A_GATE_RANK
    assert tile % unit == 0 and (AB_SRC_QKV - AB_QKV) % unit == 0 and (n_in - tile) % unit == 0

    def src_row(j):
        k = j * (tile // unit)
        k = jnp.where(j < AB_QKV // tile, k,
                      jnp.where(j < AB_SMALL // tile, k + (AB_SRC_QKV - AB_QKV) // unit, (n_in - tile) // unit))
        return unit * k

    return pl.pallas_call(
        _prep_w_ab_kernel,
        out_shape=jax.ShapeDtypeStruct((d, n_out), BF16),
        grid=(n_out // tile,),
        in_specs=[pl.BlockSpec((pl.Element(tile), pl.Element(d)), lambda j: (src_row(j), 0)),
                  pl.BlockSpec((pl.Element(A_GATE_RANK), pl.Element(d)), lambda j: (AB_SRC_GA, 0))],
        out_specs=pl.BlockSpec((d, tile), lambda j: (0, j)),
        compiler_params=pltpu.CompilerParams(
            dimension_semantics=("parallel",),
            vmem_limit_bytes=_vmem_limit(tile * d * 6, 2 * tile * d * 4)),
        name="prep_w_in_ab",
    )(w_t, w_t)


def _prep_w_cd_kernel(wt_ref, o_ref):
    tile = o_ref.shape[1]
    row = pl.program_id(0) * tile + lax.broadcasted_iota(jnp.int32, wt_ref.shape, 0)
    o_ref[...] = jnp.where(row < IN_CD, wt_ref[...], 0.0).T.astype(BF16)


def _prep_w_cd(w_t, n_out):
    n_in, d = w_t.shape
    tile = WPREP_CD_TILE
    assert n_out % tile == 0
    return pl.pallas_call(
        _prep_w_cd_kernel,
        out_shape=jax.ShapeDtypeStruct((d, n_out), BF16),
        grid=(n_out // tile,),
        in_specs=[pl.BlockSpec((tile, d), lambda j: (j, 0))],
        out_specs=pl.BlockSpec((d, tile), lambda j: (0, j)),
        compiler_params=pltpu.CompilerParams(
            dimension_semantics=("parallel",),
            vmem_limit_bytes=_vmem_limit(tile * d * 6, 2 * tile * d * 4)),
        name="prep_w_in_cd",
    )(w_t)


def _lane_row(vals, lane0):
    return jnp.zeros((1, LANES), F32).at[0, lane0:lane0 + vals.shape[0]].set(vals.astype(F32))


def _prepare_weights(norm_g, final_norm_g, w_in_ab, a_gate_w, a_gate_b, a_norm_g, b_conv_w, b_a_log,
                     b_dt_bias, b_norm_g, w_out_ab, w_in_cd, c_lam_re, c_lam_im, c_log_dt, c_b_re,
                     c_b_im, c_c_re, c_c_im, c_d, c_glu_w, c_glu_b, d_i_bias, d_f_bias, d_norm_g,
                     w_out_cd, n_log):
    assert w_in_ab.shape[1] == IN_AB and w_in_cd.shape[1] == IN_CD
    w_ab = _prep_w_ab(w_in_ab.astype(F32).T, _round_up(AB_SMALL + LANES, PROJ_TN))
    w_cd = _prep_w_cd(w_in_cd.astype(F32).T, _round_up(CD_SMALL + LANES, PROJ_TN))
    gate_w = jnp.zeros((LANES, A_KW), F32).at[AB_GA_LANE:AB_GA_LANE + A_GATE_RANK].set(
        a_gate_w.astype(F32)).astype(BF16)
    return dict(
        norm_g=norm_g.astype(F32), final_g=final_norm_g.astype(F32)[None, :],
        w_ab=w_ab, w_cd=w_cd, gate_w=gate_w, gate_b=a_gate_b.astype(F32)[None, :],
        a_norm_g=a_norm_g.astype(F32)[None, :], conv_w=b_conv_w.astype(F32),
        alog=_lane_row(b_a_log, AB_APRE_LANE), dtb=_lane_row(b_dt_bias, AB_APRE_LANE),
        b_norm_g=b_norm_g.astype(F32)[None, :],
        w_out_ab=w_out_ab.astype(BF16),
        s5_ops=_s5_operators(c_lam_re.astype(F32), c_lam_im.astype(F32), c_log_dt, c_b_re.astype(F32),
                             c_b_im.astype(F32), c_c_re.astype(F32), c_c_im.astype(F32), n_log),
        c_d=c_d.astype(F32).reshape(1, C_W), glu_w=c_glu_w.astype(BF16),
        glu_b=c_glu_b.astype(F32)[None, :],
        ib=_lane_row(d_i_bias, CD_I_LANE), fb=_lane_row(d_f_bias, CD_F_LANE),
        d_norm_g=d_norm_g.astype(F32)[None, :],
        w_out_cd=w_out_cd.astype(BF16),
    )


def _trunk(x, conv_prev, s_gla0, s_gdn0, s5_re0, s5_im0, mc0, mn0, mm0, w):
    bsz, l, d = x.shape
    c = min(CHUNK, l)
    bb = max(s for s in (1, 2, 4) if s <= MIXER_STREAMS and bsz % s == 0)
    tb_s = min(l, (8 // bb) * c)
    n_s5 = l // S5_CHUNK
    assert l % tb_s == 0 and l % S5_CHUNK == 0 and n_s5 & (n_s5 - 1) == 0
    x2d = x.reshape(bsz * l, d)

    proj = _norm_matmul(x2d, w['norm_g'][0:1], w['w_ab'])
    proj3 = proj.reshape(bsz, l, proj.shape[1])
    o_a, s_gla = _gla(proj3, w['gate_w'], w['gate_b'], w['a_norm_g'], s_gla0.astype(F32), c=c, tb=tb_s, bb=bb)
    o_b, s_gdn = _gdn(proj3, conv_prev.astype(F32), w['conv_w'], w['alog'], w['dtb'], w['b_norm_g'],
                      s_gdn0.astype(F32), c=c, tb=tb_s, bb=bb)
    conv_new = proj3[:, l - (B_CONV - 1):, AB_QKV:AB_QKV + B_QKV]
    h1 = _out_proj(o_a.reshape(bsz * l, A_VW), o_b.reshape(bsz * l, B_VW), w['w_out_ab'], x2d)

    proj = _norm_matmul(h1, w['norm_g'][1:2], w['w_cd'])
    proj3 = proj.reshape(bsz, l, proj.shape[1])
    y, s5_re, s5_im = _s5(proj, w['s5_ops'], s5_re0.astype(F32), s5_im0.astype(F32), l=l)
    o_d, mc, mn, mm = _mlstm(proj3, w['ib'], w['fb'], w['d_norm_g'], mc0.astype(F32),
                             mn0.astype(F32), mm0.astype(F32)[:, None, :], c=c, tb=tb_s, bb=bb)
    y_out = _glu_out_proj_norm(y, proj, o_d.reshape(bsz * l, D_VW), w['c_d'], w['glu_w'], w['glu_b'],
                               w['w_out_cd'], h1, w['final_g'])
    dt = x.dtype
    return (y_out.reshape(bsz, l, d).astype(dt), conv_new.astype(dt), s_gla.astype(dt), s_gdn.astype(dt),
            s5_re.astype(dt), s5_im.astype(dt), mc.astype(dt), mn.astype(dt), mm[:, 0, :].astype(dt))


def kernel(x_prompt, x_sample, cache_gdn_conv, state_gla, state_gdn, state_s5_re, state_s5_im,
           state_mlstm_c, state_mlstm_n, state_mlstm_m, norm_g, final_norm_g, w_in_ab, a_gate_w,
           a_gate_b, a_norm_g, b_conv_w, b_a_log, b_dt_bias, b_norm_g, w_out_ab, w_in_cd, c_lam_re,
           c_lam_im, c_log_dt, c_b_re, c_b_im, c_c_re, c_c_im, c_d, c_glu_w, c_glu_b, d_i_bias,
           d_f_bias, d_norm_g, w_out_cd):
    n_log = int(math.log2(max(x_prompt.shape[1], x_sample.shape[1]) // S5_CHUNK))
    w = _prepare_weights(norm_g, final_norm_g, w_in_ab, a_gate_w, a_gate_b, a_norm_g, b_conv_w, b_a_log,
                         b_dt_bias, b_norm_g, w_out_ab, w_in_cd, c_lam_re, c_lam_im, c_log_dt, c_b_re,
                         c_b_im, c_c_re, c_c_im, c_d, c_glu_w, c_glu_b, d_i_bias, d_f_bias, d_norm_g,
                         w_out_cd, n_log)
    nb = x_prompt.shape[0]
    zeros = lambda *shape: jnp.zeros(shape, F32)
    p_out = _trunk(x_prompt, zeros(nb, B_CONV - 1, B_QKV), zeros(nb, A_HEADS, A_DK, A_DV),
                   zeros(nb, B_HEADS, B_DK, B_DV), zeros(nb, C_GROUPS, C_STATE), zeros(nb, C_GROUPS, C_STATE),
                   zeros(nb, D_HEADS, D_DK, D_DV), zeros(nb, D_HEADS, D_DK), zeros(nb, D_HEADS), w)
    s_out = _trunk(x_sample, cache_gdn_conv, state_gla, state_gdn, state_s5_re, state_s5_im,
                   state_mlstm_c, state_mlstm_n, state_mlstm_m, w)
    return (p_out[0], s_out[0]) + tuple(p_out[1:]) + tuple(s_out[1:])
```

```python
import functools
import math

import jax
import jax.numpy as jnp
from jax import lax
from jax.experimental import pallas as pl
from jax.experimental.pallas import tpu as pltpu

F32 = jnp.float32
BF16 = jnp.bfloat16
HIGHEST = lax.Precision.HIGHEST

NORM_EPS = 1e-6
CHUNK = 64
A_HEADS, A_DK, A_DV, A_GATE_RANK, A_GATE_TAU = 4, 128, 256, 16, 16.0
B_HEADS, B_DK, B_DV, B_CONV = 8, 128, 128, 4
C_GROUP, C_GROUPS, C_STATE = 16, 64, 64
D_HEADS, D_DK, D_DV = 4, 128, 256
A_KW, A_VW = A_HEADS * A_DK, A_HEADS * A_DV
B_KW, B_VW = B_HEADS * B_DK, B_HEADS * B_DV
B_QKV = 2 * B_KW + B_VW
C_W = C_GROUPS * C_GROUP
D_KW, D_VW = D_HEADS * D_DK, D_HEADS * D_DV

LANES = 128
SUBLANES = 8
VMEM_BYTES_V7X = 64 * 1024 * 1024

AB_Q, AB_K, AB_V, AB_Z = 0, A_KW, 2 * A_KW, 2 * A_KW + A_VW
AB_QKV = AB_Z + A_VW
AB_ZB = AB_QKV + B_QKV
AB_SMALL = AB_ZB + B_VW
AB_GA_LANE, AB_BETA_LANE, AB_APRE_LANE = 0, A_GATE_RANK, A_GATE_RANK + B_HEADS
CD_U, CD_Z = 0, C_W
CD_Q = 2 * C_W
CD_K = CD_Q + D_KW
CD_V = CD_K + D_KW
CD_O = CD_V + D_VW
CD_ZD = CD_O + D_VW
CD_SMALL = CD_ZD + D_VW
CD_I_LANE, CD_F_LANE = 0, D_HEADS

PROJ_TN = 1280
S5_CHUNK = 16
S5_GROUP_BLOCK = 8
S5_MAX_ROWS = 8192
NORM_ROW_SPLITS = 4
GLU_ROW_SPLITS = 2
MIXER_STREAMS = 4
MLSTM_INTRA_CHUNKS = 4
GDN_WY_CHAINS = 32


def _round_up(x, m):
    return (x + m - 1) // m * m


def _vmem_limit(block_bytes, scratch_bytes=0):
    est = 2 * block_bytes + scratch_bytes
    return int(min(max(2 * est, 32 * 1024 * 1024), VMEM_BYTES_V7X - 8 * 1024 * 1024))


def _mm(a, b):
    return jnp.dot(a, b, preferred_element_type=F32)


def _dot(a, b):
    return _mm(a.astype(BF16), b.astype(BF16))


def _dot_nt(a, b):
    return lax.dot_general(a.astype(BF16), b.astype(BF16), (((1,), (1,)), ((), ())),
                           preferred_element_type=F32)


def _dot_tn(a, b):
    return lax.dot_general(a.astype(BF16), b.astype(BF16), (((0,), (0,)), ((), ())),
                           preferred_element_type=F32)


def _split2(x):
    hi = x.astype(BF16)
    return hi, (x - hi.astype(F32)).astype(BF16)


def _split3(x):
    hi = x.astype(BF16)
    r = x - hi.astype(F32)
    mid = r.astype(BF16)
    return hi, mid, (r - mid.astype(F32)).astype(BF16)


def _cumsum_rows(tri_bf16, x):
    hi, mid, lo = _split3(x)
    return _mm(tri_bf16, hi) + _mm(tri_bf16, mid) + _mm(tri_bf16, lo)


def _select_rows(sel_bf16, x):
    nt = lambda b: lax.dot_general(sel_bf16, b, (((1,), (1,)), ((), ())), preferred_element_type=F32)
    hi, mid, lo = _split3(x)
    return nt(hi) + nt(mid) + nt(lo)


def _lane_selector(lane0):
    r = lax.broadcasted_iota(jnp.int32, (SUBLANES, LANES), 0)
    l = lax.broadcasted_iota(jnp.int32, (SUBLANES, LANES), 1)
    return jnp.where(l == r + lane0, 1.0, 0.0).astype(BF16)


def _causal_masks(c):
    row = lax.broadcasted_iota(jnp.int32, (c, c), 0)
    col = lax.broadcasted_iota(jnp.int32, (c, c), 1)
    return row >= col, row > col


def _norm_matmul_kernel(x_ref, g_ref, w_ref, o_ref, xn_ref):
    j = pl.program_id(1)

    @pl.when(j == 0)
    def _():
        tm = x_ref.shape[0]
        sub = tm // NORM_ROW_SPLITS
        for i in range(NORM_ROW_SPLITS):
            s = slice(i * sub, (i + 1) * sub)
            x = x_ref[s, :]
            y = (x * lax.rsqrt(jnp.mean(x * x, axis=-1, keepdims=True) + NORM_EPS) * g_ref[...]).astype(BF16)
            xn_ref[s, :] = y
            o_ref[s, :] = jnp.dot(y, w_ref[...], preferred_element_type=F32)

    @pl.when(j != 0)
    def _():
        o_ref[...] = jnp.dot(xn_ref[...], w_ref[...], preferred_element_type=F32)


def _norm_matmul(x2d, g_row, w_bf16):
    m, d = x2d.shape
    n = w_bf16.shape[1]
    tm = min(m, 1024)
    tn = PROJ_TN
    assert m % tm == 0 and n % tn == 0
    blocks = tm * d * 4 + d * tn * 2 + tm * tn * 4
    return pl.pallas_call(
        _norm_matmul_kernel,
        out_shape=jax.ShapeDtypeStruct((m, n), F32),
        grid=(m // tm, n // tn),
        in_specs=[pl.BlockSpec((tm, d), lambda i, j: (i, 0)),
                  pl.BlockSpec((1, d), lambda i, j: (0, 0)),
                  pl.BlockSpec((d, tn), lambda i, j: (0, j))],
        out_specs=pl.BlockSpec((tm, tn), lambda i, j: (i, j)),
        scratch_shapes=[pltpu.VMEM((tm, d), BF16)],
        compiler_params=pltpu.CompilerParams(
            dimension_semantics=("parallel", "arbitrary"),
            vmem_limit_bytes=_vmem_limit(blocks, tm * d * 2)),
        name="norm_in_proj",
    )(x2d, g_row, w_bf16)


def _out_proj_kernel(a_ref, b_ref, wa_ref, wb_ref, h_ref, o_ref):
    out = (jnp.dot(a_ref[...], wa_ref[...], preferred_element_type=F32)
           + jnp.dot(b_ref[...], wb_ref[...], preferred_element_type=F32))
    o_ref[...] = h_ref[...] + out


def _glu_out_proj_norm_kernel(y_ref, u_ref, z_ref, od_ref, d_ref, gw_ref, gb_ref, wc_ref, wd_ref, h_ref, g_ref,
                              o_ref):
    tm = y_ref.shape[0]
    sub = tm // GLU_ROW_SPLITS
    halves = [slice(i * sub, (i + 1) * sub) for i in range(GLU_ROW_SPLITS)]
    y = [jax.nn.gelu(y_ref[s, :] + d_ref[...] * u_ref[s, :]) for s in halves]
    gate = [jax.nn.sigmoid(jnp.dot(v.astype(BF16), gw_ref[...], preferred_element_type=F32) + gb_ref[...])
            for v in y]
    o_c = [(y[i] * gate[i] * jax.nn.silu(z_ref[s, :])).astype(BF16) for i, s in enumerate(halves)]
    out = [jnp.dot(o_c[i], wc_ref[...], preferred_element_type=F32)
           + jnp.dot(od_ref[s, :], wd_ref[...], preferred_element_type=F32) for i, s in enumerate(halves)]
    for i, s in enumerate(halves):
        h = h_ref[s, :] + out[i]
        o_ref[s, :] = h * lax.rsqrt(jnp.mean(h * h, axis=-1, keepdims=True) + NORM_EPS) * g_ref[...]


def _glu_out_proj_norm(y2d, proj2d, o_d, d_row, glu_w, glu_b_row, w_out, h2d, final_g_row):
    m, d = h2d.shape
    tm = min(m, 512)
    assert m % tm == 0 and C_W == D_VW and w_out.shape[0] == C_W + D_VW
    row = lambda col: (lambda i: (i, col))
    const = lambda i: (0, 0)
    resident = pl.Buffered(1)
    blocks = 3 * tm * C_W * 4 + tm * D_VW * 2 + 2 * tm * d * 4
    weights = C_W * C_W * 2 + (C_W + D_VW) * d * 2
    return pl.pallas_call(
        _glu_out_proj_norm_kernel,
        out_shape=jax.ShapeDtypeStruct((m, d), F32),
        grid=(m // tm,),
        in_specs=[pl.BlockSpec((tm, C_W), row(0)),
                  pl.BlockSpec((tm, C_W), row(CD_U // C_W)),
                  pl.BlockSpec((tm, C_W), row(CD_Z // C_W)),
                  pl.BlockSpec((tm, D_VW), row(0)),
                  pl.BlockSpec((1, C_W), const),
                  pl.BlockSpec((C_W, C_W), const, pipeline_mode=resident),
                  pl.BlockSpec((1, C_W), const),
                  pl.BlockSpec((C_W, d), const, pipeline_mode=resident),
                  pl.BlockSpec((D_VW, d), lambda i: (1, 0), pipeline_mode=resident),
                  pl.BlockSpec((tm, d), row(0)),
                  pl.BlockSpec((1, d), const)],
        out_specs=pl.BlockSpec((tm, d), row(0)),
        compiler_params=pltpu.CompilerParams(
            dimension_semantics=("parallel",),
            vmem_limit_bytes=_vmem_limit(blocks, weights + 3 * tm * C_W * 4)),
        name="glu_out_proj_norm",
    )(y2d, proj2d, proj2d, o_d, d_row, glu_w, glu_b_row, w_out, w_out, h2d, final_g_row)


def _out_proj(mix_a, mix_b, w_out, h2d):
    m, d = h2d.shape
    ka, kb = mix_a.shape[1], mix_b.shape[1]
    tm = min(m, 512)
    assert m % tm == 0 and ka == kb and w_out.shape[0] == ka + kb
    blocks = tm * (ka + kb) * 2 + (ka + kb) * d * 2 + 2 * tm * d * 4
    return pl.pallas_call(
        _out_proj_kernel,
        out_shape=jax.ShapeDtypeStruct((m, d), F32),
        grid=(m // tm,),
        in_specs=[pl.BlockSpec((tm, ka), lambda i: (i, 0)),
                  pl.BlockSpec((tm, kb), lambda i: (i, 0)),
                  pl.BlockSpec((ka, d), lambda i: (0, 0)),
                  pl.BlockSpec((kb, d), lambda i: (1, 0)),
                  pl.BlockSpec((tm, d), lambda i: (i, 0))],
        out_specs=pl.BlockSpec((tm, d), lambda i: (i, 0)),
        compiler_params=pltpu.CompilerParams(
            dimension_semantics=("parallel",),
            vmem_limit_bytes=_vmem_limit(blocks)),
        name="out_proj",
    )(mix_a, mix_b, w_out, w_out, h2d)


def _gla_kernel(q_ref, k_ref, v_ref, z_ref, sm_ref, gw_ref, gb_ref, ng_ref, s0_ref,
                o_ref, sout_ref, st_ref, *, c, nchunks):
    t = pl.program_id(1)
    last_t = pl.num_programs(1) - 1
    bb = q_ref.shape[0]
    chains = [(bi, h) for bi in range(bb) for h in range(A_HEADS)]
    ksl = [slice(h * A_DK, (h + 1) * A_DK) for _, h in chains]
    vsl = [slice(h * A_DV, (h + 1) * A_DV) for _, h in chains]
    idx = range(len(chains))

    @pl.when(t == 0)
    def _():
        for bi, h in chains:
            st_ref[bi, h] = s0_ref[bi, h].T

    causal, _ = _causal_masks(c)
    tri = causal.astype(BF16)
    gw = gw_ref[...]
    gb = gb_ref[...]

    def body(n, carry):
        sl = pl.ds(pl.multiple_of(n * c, c), c)
        b_all = [_cumsum_rows(tri, jax.nn.log_sigmoid(_dot(sm_ref[bi, sl, :], gw) + gb) * (1.0 / A_GATE_TAU))
                 for bi in range(bb)]
        b = [b_all[chains[i][0]][:, ksl[i]] for i in idx]
        b_last = [b[i][c - 1:c, :] for i in idx]
        k = [k_ref[chains[i][0], sl, ksl[i]] for i in idx]
        v = [v_ref[chains[i][0], sl, vsl[i]].astype(BF16) for i in idx]
        q_dec = [(q_ref[chains[i][0], sl, ksl[i]] * (A_DK ** -0.5) * jnp.exp(b[i])).astype(BF16) for i in idx]
        k_dec = [(k[i] * jnp.exp(-b[i])).astype(BF16) for i in idx]
        k_w = [(k[i] * jnp.exp(b_last[i] - b[i])).astype(BF16) for i in idx]
        scores = [jnp.where(causal, _dot_nt(q_dec[i], k_dec[i]), 0.0).astype(BF16) for i in idx]
        s_t = [st_ref[bi, h] for bi, h in chains]
        outs = [_mm(scores[i], v[i]) + _dot_nt(q_dec[i], s_t[i]) for i in idx]
        for i, (bi, h) in enumerate(chains):
            st_ref[bi, h] = s_t[i] * jnp.exp(b_last[i]) + _dot_tn(v[i], k_w[i])
        for i, (bi, h) in enumerate(chains):
            o = outs[i]
            o = o * lax.rsqrt(jnp.mean(o * o, axis=-1, keepdims=True) + NORM_EPS) * ng_ref[:, vsl[i]]
            o_ref[bi, sl, vsl[i]] = (o * jax.nn.silu(z_ref[bi, sl, vsl[i]])).astype(o_ref.dtype)
        return carry

    lax.fori_loop(0, nchunks, body, 0)

    @pl.when(t == last_t)
    def _():
        for bi, h in chains:
            sout_ref[bi, h] = st_ref[bi, h].T


def _gla(proj3, gate_w_pad, gate_b_row, norm_g_row, s0, *, c, tb, bb):
    bsz, l, _ = proj3.shape
    nblk = l // tb
    assert bsz % bb == 0
    tok = lambda col: (lambda b, t: (b, t, col))
    in_specs = [
        pl.BlockSpec((bb, tb, A_KW), tok(AB_Q // A_KW)),
        pl.BlockSpec((bb, tb, A_KW), tok(AB_K // A_KW)),
        pl.BlockSpec((bb, tb, A_VW), tok(AB_V // A_VW)),
        pl.BlockSpec((bb, tb, A_VW), tok(AB_Z // A_VW)),
        pl.BlockSpec((bb, tb, LANES), tok(AB_SMALL // LANES)),
        pl.BlockSpec((LANES, A_KW), lambda b, t: (0, 0)),
        pl.BlockSpec((1, A_KW), lambda b, t: (0, 0)),
        pl.BlockSpec((1, A_VW), lambda b, t: (0, 0)),
        pl.BlockSpec((bb, A_HEADS, A_DK, A_DV), lambda b, t: (b, 0, 0, 0)),
    ]
    out_specs = [
        pl.BlockSpec((bb, tb, A_VW), lambda b, t: (b, t, 0)),
        pl.BlockSpec((bb, A_HEADS, A_DK, A_DV), lambda b, t: (b, 0, 0, 0)),
    ]
    state = bb * A_HEADS * A_DK * A_DV * 4
    blocks = bb * tb * (2 * A_KW + 2 * A_VW + LANES) * 4 + bb * tb * A_VW * 2 + 2 * state
    return pl.pallas_call(
        functools.partial(_gla_kernel, c=c, nchunks=tb // c),
        out_shape=[jax.ShapeDtypeStruct((bsz, l, A_VW), BF16),
                   jax.ShapeDtypeStruct((bsz, A_HEADS, A_DK, A_DV), F32)],
        grid=(bsz // bb, nblk),
        in_specs=in_specs,
        out_specs=out_specs,
        scratch_shapes=[pltpu.VMEM((bb, A_HEADS, A_DV, A_DK), F32)],
        compiler_params=pltpu.CompilerParams(
            dimension_semantics=("parallel", "arbitrary"),
            vmem_limit_bytes=_vmem_limit(blocks, state)),
        name="gla_mixer",
    )(proj3, proj3, proj3, proj3, proj3, gate_w_pad, gate_b_row, norm_g_row, s0)


def _gdn_kernel(x_ref, z_ref, sm_ref, w_ref, cp_ref, alog_ref, dtb_ref, ng_ref, s0_ref,
                o_ref, sout_ref, s_ref, tail_ref, u_ref, wm_ref, qg_ref, kg_ref, qk_ref, gl_ref,
                *, c, nchunks):
    t = pl.program_id(1)
    last_t = pl.num_programs(1) - 1
    tb = c * nchunks
    keep = SUBLANES - (B_CONV - 1)
    bb = x_ref.shape[0]

    @pl.when(t == 0)
    def _():
        s_ref[...] = s0_ref[...]
        tail_ref[:, 0:keep, :] = jnp.zeros((bb, keep, B_QKV), F32)
        tail_ref[:, keep:SUBLANES, :] = cp_ref[...]

    def conv_silu(bi, row0, first, cols):
        x = x_ref[bi, pl.ds(row0, c), cols]
        if first is None:
            prev = x_ref[bi, pl.ds(pl.multiple_of(row0 - SUBLANES, SUBLANES), SUBLANES), cols]
        else:
            before = pl.multiple_of(jnp.maximum(row0 - SUBLANES, 0), SUBLANES)
            prev = jnp.where(first, tail_ref[bi, :, cols], x_ref[bi, pl.ds(before, SUBLANES), cols])
        w = w_ref[:, cols]
        ext = jnp.concatenate([prev, x], axis=0)
        ext1 = pltpu.roll(ext, 1, 0)
        newer = ext * w[3:4, :] + ext1 * w[2:3, :]
        older = ext * w[1:2, :] + ext1 * w[0:1, :]
        conv = newer + pltpu.roll(older, 2, 0)
        return jax.nn.silu(conv[SUBLANES:SUBLANES + c, :])

    causal, strict = _causal_masks(c)
    tri = causal.astype(BF16)
    eye = jnp.where(causal & jnp.logical_not(strict), 1.0, 0.0).astype(F32)
    sel = _lane_selector(AB_APRE_LANE)
    neg_a_exp = -jnp.exp(alog_ref[...])
    dtb = dtb_ref[...]
    ng = ng_ref[...]
    n_double = int(math.log2(c)) - 1
    heads = range(B_HEADS)

    group = max(1, GDN_WY_CHAINS // (bb * B_HEADS))
    group = group if nchunks % group == 0 else 1

    def wy_factors(n, carry):
        pws, rhss, where = [], [], []
        for bi, ci in [(bi, ci) for ci in range(group) for bi in range(bb)]:
            row0 = pl.multiple_of((n * group + ci) * c, c)
            first = (n == 0) if ci == 0 else None
            sl = pl.ds(row0, c)
            sm = sm_ref[bi, sl, :]
            g_cum = _cumsum_rows(tri, neg_a_exp * jax.nn.softplus(sm + dtb))
            g_rows = _select_rows(sel, g_cum)
            beta_all = jax.nn.sigmoid(sm)
            gl_ref[bi, pl.ds(n * group + ci, 1), :] = g_cum[c - 1:c, :]
            for h in heads:
                hs = slice(h * B_DK, (h + 1) * B_DK)
                q = conv_silu(bi, row0, first, hs)
                k = conv_silu(bi, row0, first, slice(B_KW + h * B_DK, B_KW + (h + 1) * B_DK))
                v = conv_silu(bi, row0, first, slice(2 * B_KW + h * B_DV, 2 * B_KW + (h + 1) * B_DV))
                q = q * lax.rsqrt(jnp.sum(q * q, axis=-1, keepdims=True) + NORM_EPS) * (B_DK ** -0.5)
                k = k * lax.rsqrt(jnp.sum(k * k, axis=-1, keepdims=True) + NORM_EPS)
                g_col = g_cum[:, AB_APRE_LANE + h:AB_APRE_LANE + h + 1]
                beta = beta_all[:, AB_BETA_LANE + h:AB_BETA_LANE + h + 1]
                decay = jnp.exp(jnp.where(causal, g_col - g_rows[h:h + 1, :], -jnp.inf))
                e_g = jnp.exp(g_col)
                k_beta = k * beta
                pws.append(-jnp.where(strict, _dot_nt(k_beta, k) * decay, 0.0))
                rhss.append(jnp.concatenate([v * beta, k_beta * e_g], axis=1))
                where.append((bi, sl, hs))
                qk_ref[bi, sl, h * LANES:h * LANES + c] = jnp.where(causal, _dot_nt(q, k) * decay,
                                                                    0.0).astype(BF16)
                qg_ref[bi, sl, hs] = (q * e_g).astype(BF16)
                kg_ref[bi, sl, hs] = (k * jnp.exp(g_col[c - 1:c, :] - g_col)).astype(BF16)
        items = range(len(pws))
        neg_lower = [_split2(p) for p in pws]
        invs = [eye + p for p in pws]
        for _ in range(n_double):
            pws = [_dot(p, p) for p in pws]
            invs = [i + _dot(i, p) for i, p in zip(invs, pws)]
        invs = [i.astype(BF16) for i in invs]
        sol = [_mm(invs[i], rhss[i].astype(BF16)).astype(BF16) for i in items]
        resid = [rhss[i] - sol[i].astype(F32) + _mm(neg_lower[i][0], sol[i]) + _mm(neg_lower[i][1], sol[i])
                 for i in items]
        for i in items:
            bi, sl, hs = where[i]
            uw = sol[i].astype(F32) + _mm(invs[i], resid[i].astype(BF16))
            u_ref[bi, sl, hs] = uw[:, :B_DV]
            wm_ref[bi, sl, hs] = uw[:, B_DV:].astype(BF16)
        return carry

    lax.fori_loop(0, nchunks // group, wy_factors, 0)
    tail_ref[...] = x_ref[:, tb - SUBLANES:tb, :]

    chains = [(bi, h) for bi in range(bb) for h in heads]
    hsl = [slice(h * B_DK, (h + 1) * B_DK) for _, h in chains]

    def recurrence(n, carry):
        sl = pl.ds(pl.multiple_of(n * c, c), c)
        e_last = [jnp.exp(gl_ref[bi, pl.ds(n, 1), :]) for bi in range(bb)]
        s_old = [s_ref[bi, h] for bi, h in chains]
        s_bf = [s.astype(BF16) for s in s_old]
        v_new = [(u_ref[bi, sl, hsl[i]] - _mm(wm_ref[bi, sl, hsl[i]], s_bf[i])).astype(BF16)
                 for i, (bi, h) in enumerate(chains)]
        outs = [_mm(qg_ref[bi, sl, hsl[i]], s_bf[i]) + _mm(qk_ref[bi, sl, h * LANES:h * LANES + c], v_new[i])
                for i, (bi, h) in enumerate(chains)]
        for i, (bi, h) in enumerate(chains):
            s_ref[bi, h] = (e_last[bi][:, AB_APRE_LANE + h:AB_APRE_LANE + h + 1] * s_old[i]
                            + lax.dot_general(kg_ref[bi, sl, hsl[i]], v_new[i], (((0,), (0,)), ((), ())),
                                              preferred_element_type=F32))
        for i, (bi, h) in enumerate(chains):
            o = outs[i]
            o = o * lax.rsqrt(jnp.mean(o * o, axis=-1, keepdims=True) + NORM_EPS) * ng
            o_ref[bi, sl, hsl[i]] = (o * jax.nn.silu(z_ref[bi, sl, hsl[i]])).astype(o_ref.dtype)
        return carry

    lax.fori_loop(0, nchunks, recurrence, 0)

    @pl.when(t == last_t)
    def _():
        sout_ref[...] = s_ref[...]


def _gdn(proj3, conv_prev, conv_w, alog_row, dtb_row, norm_g_row, s0, *, c, tb, bb):
    bsz, l, _ = proj3.shape
    nblk = l // tb
    assert AB_QKV % B_QKV == 0 and AB_ZB % B_VW == 0 and bsz % bb == 0
    in_specs = [
        pl.BlockSpec((bb, tb, B_QKV), lambda b, t: (b, t, AB_QKV // B_QKV)),
        pl.BlockSpec((bb, tb, B_VW), lambda b, t: (b, t, AB_ZB // B_VW)),
        pl.BlockSpec((bb, tb, LANES), lambda b, t: (b, t, AB_SMALL // LANES)),
        pl.BlockSpec((B_CONV, B_QKV), lambda b, t: (0, 0)),
        pl.BlockSpec((bb, B_CONV - 1, B_QKV), lambda b, t: (b, 0, 0)),
        pl.BlockSpec((1, LANES), lambda b, t: (0, 0)),
        pl.BlockSpec((1, LANES), lambda b, t: (0, 0)),
        pl.BlockSpec((1, B_DV), lambda b, t: (0, 0)),
        pl.BlockSpec((bb, B_HEADS, B_DK, B_DV), lambda b, t: (b, 0, 0, 0)),
    ]
    out_specs = [
        pl.BlockSpec((bb, tb, B_VW), lambda b, t: (b, t, 0)),
        pl.BlockSpec((bb, B_HEADS, B_DK, B_DV), lambda b, t: (b, 0, 0, 0)),
    ]
    blocks = bb * (tb * (B_QKV + B_VW + LANES) * 4 + tb * B_VW * 2 + 2 * B_HEADS * B_DK * B_DV * 4)
    scratch = bb * (B_HEADS * B_DK * B_DV * 4 + SUBLANES * B_QKV * 4 + tb * B_VW * 4
                    + 4 * tb * B_KW * 2 + SUBLANES * LANES * 4)
    return pl.pallas_call(
        functools.partial(_gdn_kernel, c=c, nchunks=tb // c),
        out_shape=[jax.ShapeDtypeStruct((bsz, l, B_VW), BF16),
                   jax.ShapeDtypeStruct((bsz, B_HEADS, B_DK, B_DV), F32)],
        grid=(bsz // bb, nblk),
        in_specs=in_specs,
        out_specs=out_specs,
        scratch_shapes=[pltpu.VMEM((bb, B_HEADS, B_DK, B_DV), F32),
                        pltpu.VMEM((bb, SUBLANES, B_QKV), F32),
                        pltpu.VMEM((bb, tb, B_VW), F32),
                        pltpu.VMEM((bb, tb, B_KW), BF16),
                        pltpu.VMEM((bb, tb, B_KW), BF16),
                        pltpu.VMEM((bb, tb, B_KW), BF16),
                        pltpu.VMEM((bb, tb, B_HEADS * LANES), BF16),
                        pltpu.VMEM((bb, SUBLANES, LANES), F32)],
        compiler_params=pltpu.CompilerParams(
            dimension_semantics=("parallel", "arbitrary"),
            vmem_limit_bytes=_vmem_limit(blocks, scratch)),
        name="gdn_mixer",
    )(proj3, proj3, proj3, conv_w, conv_prev, alog_row, dtb_row, norm_g_row, s0)


def _mlstm_kernel(q_ref, k_ref, v_ref, og_ref, z_ref, sm_ref, ib_ref, fb_ref, ng_ref,
                  c0_ref, n0_ref, m0_ref, o_ref, cout_ref, nout_ref, mout_ref,
                  c_ref, n_ref, m_ref, hi_ref, mi_ref, ni_ref, bc_ref, kv_ref, ks_ref, mc_ref, bl_ref,
                  *, c, nchunks):
    t = pl.program_id(1)
    last_t = pl.num_programs(1) - 1
    bb = q_ref.shape[0]
    chains = [(bi, h) for bi in range(bb) for h in range(D_HEADS)]
    ksl = [slice(h * D_DK, (h + 1) * D_DK) for _, h in chains]
    vsl = [slice(h * D_DV, (h + 1) * D_DV) for _, h in chains]
    idx = range(len(chains))

    @pl.when(t == 0)
    def _():
        c_ref[...] = c0_ref[...]
        n_ref[...] = n0_ref[...]
        for bi, h in chains:
            m_ref[bi, h] = jnp.broadcast_to(m0_ref[bi, :, h:h + 1], (1, LANES))

    causal, _ = _causal_masks(c)
    tri = causal.astype(BF16)
    sel = _lane_selector(0)
    lane = lax.broadcasted_iota(jnp.int32, (c, LANES), 1)
    ib = ib_ref[...]
    fb = fb_ref[...]

    group = MLSTM_INTRA_CHUNKS if nchunks % MLSTM_INTRA_CHUNKS == 0 else 1

    def intra(n, carry):
        items = [(bi, ci, h) for ci in range(group) for bi, h in chains]
        sls = [pl.ds(pl.multiple_of((n * group + ci) * c, c), c) for ci in range(group)]
        sm = {(bi, ci): sm_ref[bi, sls[ci], :] for ci in range(group) for bi in range(bb)}
        i_full = {key: x + ib for key, x in sm.items()}
        b_full = {key: _cumsum_rows(tri, jax.nn.log_sigmoid(x + fb)) for key, x in sm.items()}
        rows = {key: _select_rows(sel, jnp.where(lane < CD_F_LANE, i_full[key], b_full[key]))
                for key in sm}
        b_col = [b_full[bi, ci][:, CD_F_LANE + h:CD_F_LANE + h + 1] for bi, ci, h in items]
        i_col = [i_full[bi, ci][:, CD_I_LANE + h:CD_I_LANE + h + 1] for bi, ci, h in items]
        logw = [jnp.where(causal, b_col[i] - rows[bi, ci][CD_F_LANE + h:CD_F_LANE + h + 1, :]
                          + rows[bi, ci][CD_I_LANE + h:CD_I_LANE + h + 1, :], -jnp.inf)
                for i, (bi, ci, h) in enumerate(items)]
        ids = range(len(items))
        m_intra = [jnp.max(logw[i], axis=-1, keepdims=True) for i in ids]
        ks_ = [slice(h * D_DK, (h + 1) * D_DK) for _, _, h in items]
        vs_ = [slice(h * D_DV, (h + 1) * D_DV) for _, _, h in items]
        q_bf = [(q_ref[bi, sls[ci], ks_[i]] * (D_DK ** -0.5)).astype(BF16) for i, (bi, ci, h) in enumerate(items)]
        k = [k_ref[bi, sls[ci], ks_[i]] for i, (bi, ci, h) in enumerate(items)]
        v = [v_ref[bi, sls[ci], vs_[i]].astype(BF16) for i, (bi, ci, h) in enumerate(items)]
        p = [jnp.exp(logw[i] - m_intra[i]) * _dot_nt(q_bf[i], k[i]) for i in ids]
        n_intra = [jnp.sum(p[i], axis=-1, keepdims=True) for i in ids]
        m_chunk = [m_intra[i][c - 1:c, :] for i in ids]
        k_w = [k[i] * jnp.exp(b_col[i][c - 1:c, :] - b_col[i] + i_col[i] - m_chunk[i]) for i in ids]
        rep = lambda x: jnp.broadcast_to(x, (x.shape[0], LANES))
        for i, (bi, ci, h) in enumerate(items):
            row = pl.ds(n * group + ci, 1)
            hi_ref[bi, sls[ci], vs_[i]] = _dot(p[i], v[i])
            kv_ref[bi, n * group + ci, h] = _dot_tn(k_w[i], v[i])
            ks_ref[bi, h, row, :] = jnp.sum(k_w[i], axis=0, keepdims=True)
            mi_ref[bi, h, sls[ci], :] = rep(m_intra[i])
            ni_ref[bi, h, sls[ci], :] = rep(n_intra[i])
            bc_ref[bi, h, sls[ci], :] = rep(b_col[i])
            mc_ref[bi, h, row, :] = rep(m_chunk[i])
            bl_ref[bi, h, row, :] = rep(b_col[i][c - 1:c, :])
        return carry

    lax.fori_loop(0, nchunks // group, intra, 0)

    twice = lambda x: jnp.concatenate([x, x], axis=1)

    def body(n, carry):
        sl = pl.ds(pl.multiple_of(n * c, c), c)
        row = pl.ds(n, 1)
        q = [q_ref[chains[i][0], sl, ksl[i]] * (D_DK ** -0.5) for i in idx]
        c_s = [c_ref[bi, h] for bi, h in chains]
        n_s = [n_ref[bi, h:h + 1, :] for bi, h in chains]
        m_s = [m_ref[bi, h] for bi, h in chains]
        qc = [_dot(q[i], c_s[i]) for i in idx]
        b_last = [bl_ref[bi, h, row, :] for bi, h in chains]
        m_chunk = [mc_ref[bi, h, row, :] for bi, h in chains]
        for i, (bi, h) in enumerate(chains):
            m_new = jnp.maximum(b_last[i] + m_s[i], m_chunk[i])
            w_old = jnp.exp(b_last[i] + m_s[i] - m_new)
            w_new = jnp.exp(m_chunk[i] - m_new)
            c_ref[bi, h] = twice(w_old) * c_s[i] + twice(w_new) * kv_ref[bi, n, h]
            n_ref[bi, h:h + 1, :] = w_old * n_s[i] + w_new * ks_ref[bi, h, row, :]
            m_ref[bi, h] = m_new
        for i, (bi, h) in enumerate(chains):
            m_intra = mi_ref[bi, h, sl, :]
            a = bc_ref[bi, h, sl, :] + m_s[i]
            m_t = jnp.maximum(a, m_intra)
            w_a = jnp.exp(a - m_t)
            w_i = jnp.exp(m_intra - m_t)
            num = twice(w_a) * qc[i] + twice(w_i) * hi_ref[bi, sl, vsl[i]]
            den = w_a * jnp.sum(q[i] * n_s[i], axis=-1, keepdims=True) + w_i * ni_ref[bi, h, sl, :]
            hh = num / twice(jnp.maximum(jnp.abs(den), jnp.exp(-m_t)))
            hd = jax.nn.sigmoid(og_ref[bi, sl, vsl[i]]) * hh
            oc = hd - jnp.mean(hd, axis=-1, keepdims=True)
            o = oc * lax.rsqrt(jnp.mean(oc * oc, axis=-1, keepdims=True) + NORM_EPS) * ng_ref[:, vsl[i]]
            o_ref[bi, sl, vsl[i]] = (o * jax.nn.silu(z_ref[bi, sl, vsl[i]])).astype(o_ref.dtype)
        return carry

    lax.fori_loop(0, nchunks, body, 0)

    @pl.when(t == last_t)
    def _():
        cout_ref[...] = c_ref[...]
        nout_ref[...] = n_ref[...]
        for bi, h in chains:
            mout_ref[bi, :, h:h + 1] = m_ref[bi, h][:, 0:1]


def _mlstm(proj3, ib_row, fb_row, norm_g_row, c0, n0, m0, *, c, tb, bb):
    bsz, l, _ = proj3.shape
    nblk = l // tb
    assert bsz % bb == 0
    tok = lambda col: (lambda b, t: (b, t, col))
    st4 = lambda b, t: (b, 0, 0, 0)
    st3 = lambda b, t: (b, 0, 0)
    in_specs = [
        pl.BlockSpec((bb, tb, D_KW), tok(CD_Q // D_KW)),
        pl.BlockSpec((bb, tb, D_KW), tok(CD_K // D_KW)),
        pl.BlockSpec((bb, tb, D_VW), tok(CD_V // D_VW)),
        pl.BlockSpec((bb, tb, D_VW), tok(CD_O // D_VW)),
        pl.BlockSpec((bb, tb, D_VW), tok(CD_ZD // D_VW)),
        pl.BlockSpec((bb, tb, LANES), tok(CD_SMALL // LANES)),
        pl.BlockSpec((1, LANES), lambda b, t: (0, 0)),
        pl.BlockSpec((1, LANES), lambda b, t: (0, 0)),
        pl.BlockSpec((1, D_VW), lambda b, t: (0, 0)),
        pl.BlockSpec((bb, D_HEADS, D_DK, D_DV), st4),
        pl.BlockSpec((bb, D_HEADS, D_DK), st3),
        pl.BlockSpec((bb, 1, D_HEADS), st3),
    ]
    out_specs = [
        pl.BlockSpec((bb, tb, D_VW), lambda b, t: (b, t, 0)),
        pl.BlockSpec((bb, D_HEADS, D_DK, D_DV), st4),
        pl.BlockSpec((bb, D_HEADS, D_DK), st3),
        pl.BlockSpec((bb, 1, D_HEADS), st3),
    ]
    state = bb * D_HEADS * D_DK * D_DV * 4
    blocks = bb * tb * (2 * D_KW + 3 * D_VW + LANES) * 4 + bb * tb * D_VW * 2 + 2 * state
    nchunks = tb // c
    return pl.pallas_call(
        functools.partial(_mlstm_kernel, c=c, nchunks=nchunks),
        out_shape=[jax.ShapeDtypeStruct((bsz, l, D_VW), BF16),
                   jax.ShapeDtypeStruct((bsz, D_HEADS, D_DK, D_DV), F32),
                   jax.ShapeDtypeStruct((bsz, D_HEADS, D_DK), F32),
                   jax.ShapeDtypeStruct((bsz, 1, D_HEADS), F32)],
        grid=(bsz // bb, nblk),
        in_specs=in_specs,
        out_specs=out_specs,
        scratch_shapes=[pltpu.VMEM((bb, D_HEADS, D_DK, D_DV), F32),
                        pltpu.VMEM((bb, D_HEADS, D_DK), F32),
                        pltpu.VMEM((bb, D_HEADS, 1, LANES), F32),
                        pltpu.VMEM((bb, tb, D_VW), F32),
                        pltpu.VMEM((bb, D_HEADS, tb, LANES), F32),
                        pltpu.VMEM((bb, D_HEADS, tb, LANES), F32),
                        pltpu.VMEM((bb, D_HEADS, tb, LANES), F32),
                        pltpu.VMEM((bb, nchunks, D_HEADS, D_DK, D_DV), F32),
                        pltpu.VMEM((bb, D_HEADS, _round_up(nchunks, SUBLANES), D_DK), F32),
                        pltpu.VMEM((bb, D_HEADS, _round_up(nchunks, SUBLANES), LANES), F32),
                        pltpu.VMEM((bb, D_HEADS, _round_up(nchunks, SUBLANES), LANES), F32)],
        compiler_params=pltpu.CompilerParams(
            dimension_semantics=("parallel", "arbitrary"),
            vmem_limit_bytes=_vmem_limit(blocks, state * (1 + nchunks) + bb * tb * (D_VW + 2 * LANES) * 4)),
        name="mlstm_mixer",
    )(proj3, proj3, proj3, proj3, proj3, proj3, ib_row, fb_row, norm_g_row, c0, n0, m0)


def _s5_expand_operators(kc_ref, bc_ref, cc_ref, bd_ref, bst_ref, cst_ref):
    tc, gt, sw, cg = S5_CHUNK, S5_GROUP_BLOCK, 2 * C_STATE, C_GROUP
    w_t = tc * LANES
    iota = lambda shape, d: lax.broadcasted_iota(jnp.int32, shape, d)
    row_g = (iota((w_t, LANES), 0) // cg) % gt
    tile16 = jnp.where(iota((cg, LANES), 1) % cg == iota((cg, LANES), 0), 1.0, 0.0).astype(BF16)
    bd = jnp.where(row_g == iota((w_t, LANES), 1) // cg, _mm(kc_ref[...].reshape(w_t, cg), tile16),
                   0.0).astype(BF16)
    bd_ref[:, LANES:2 * LANES] = bd
    bd_ref[0:w_t - LANES, 0:LANES] = bd[LANES:, :]
    bd_ref[w_t - LANES:w_t, 0:LANES] = jnp.zeros((LANES, LANES), BF16)
    bc = bc_ref[...].reshape(w_t, sw).astype(F32)
    for g in range(gt):
        bst_ref[:, g * sw:(g + 1) * sw] = jnp.where(row_g == g, bc, 0.0).astype(BF16)
    src, dst = iota((tc * cg, w_t), 0), iota((tc * cg, w_t), 1)
    spread = jnp.where((src // cg == dst // LANES) & (src % cg == dst % cg), 1.0, 0.0).astype(BF16)
    lane_g = (iota((sw, w_t), 1) // cg) % gt
    for g in range(gt):
        full = _mm(cc_ref[g * sw:(g + 1) * sw, :], spread)
        cst_ref[g * sw:(g + 1) * sw, :] = jnp.where(lane_g == g, full, 0.0).astype(BF16)


def _s5_kernel(u_ref, kc_ref, bc_ref, cc_ref, apow_ref, x0_ref, y_ref, xf_ref, bd_ref, bst_ref, cst_ref,
               *, nc, bb):
    m = nc * bb
    tc = S5_CHUNK
    sw = 2 * C_STATE

    @pl.when(pl.program_id(1) == 0)
    def _():
        _s5_expand_operators(kc_ref, bc_ref, cc_ref, bd_ref, bst_ref, cst_ref)

    row = lax.broadcasted_iota(jnp.int32, (m, sw), 0)
    n_idx = row & (nc - 1)
    n_log = int(math.log2(nc))

    def cmul(a1, a2, x):
        return a1 * x + a2 * pltpu.roll(x, C_STATE, 1)

    lhs = jnp.concatenate([u_ref[pl.ds(tau, m, stride=tc), :].astype(BF16) for tau in range(tc)], axis=1)
    e_all = _mm(lhs, bst_ref[...])
    groups = range(S5_GROUP_BLOCK)
    gsl = [slice(g * sw, (g + 1) * sw) for g in groups]
    x0_rows = []
    for g in groups:
        rows0 = jnp.zeros((m, sw), F32)
        for b in range(bb):
            rows0 = jnp.where(row == b * nc, x0_ref[b, :, gsl[g]], rows0)
        x0_rows.append(rows0)
    x = [e_all[:, gsl[g]] + cmul(apow_ref[0:1, gsl[g]], apow_ref[1:2, gsl[g]], x0_rows[g]) for g in groups]
    for j in range(n_log):
        sh = 1 << j
        shifted = [jnp.where(n_idx >= sh, pltpu.roll(x[g], sh, 0), 0.0) for g in groups]
        x = [x[g] + cmul(apow_ref[2 * j:2 * j + 1, gsl[g]], apow_ref[2 * j + 1:2 * j + 2, gsl[g]], shifted[g])
             for g in groups]
    x_start = [jnp.where(n_idx >= 1, pltpu.roll(x[g], 1, 0), x0_rows[g]).astype(BF16) for g in groups]
    for g in groups:
        for b in range(bb):
            xf_ref[b, :, gsl[g]] = x[g][b * nc + nc - 1:b * nc + nc, :]
    y_state = _mm(jnp.concatenate(x_start, axis=1), cst_ref[...])
    for tau in range(0, tc, 2):
        width = (tau + 2) * LANES
        start = (tc - 2 - tau) * LANES
        y = _mm(lhs[:, :width], bd_ref[start:start + width, :]) + y_state[:, tau * LANES:(tau + 2) * LANES]
        y_ref[pl.ds(tau, m, stride=tc), :] = y[:, :LANES]
        y_ref[pl.ds(tau + 1, m, stride=tc), :] = y[:, LANES:]


def _s5_chunks(proj2d, kc, bc, cc, apow, x0, *, l, bb):
    tc = S5_CHUNK
    nc = l // tc
    bsz = x0.shape[1]
    nt = C_GROUPS // S5_GROUP_BLOCK
    rows = bb * l
    sw = 2 * C_STATE
    sw_t = S5_GROUP_BLOCK * sw
    w_t = tc * LANES
    assert CD_U % LANES == 0 and bsz % bb == 0 and S5_GROUP_BLOCK * C_GROUP == LANES
    blocks = (2 * rows * LANES * 4 + 2 * w_t * LANES * 2 + sw_t * tc * C_GROUP * 2 + apow.shape[1] * sw_t * 4
              + 2 * bb * SUBLANES * sw_t * 4)
    scratch = w_t * 2 * LANES * 2 + 2 * w_t * sw_t * 2
    temps = bb * nc * (w_t * 2 + w_t * 4 + 3 * sw_t * 4) + 4 * sw * w_t * 4
    return pl.pallas_call(
        functools.partial(_s5_kernel, nc=nc, bb=bb),
        out_shape=[jax.ShapeDtypeStruct((bsz * l, C_W), F32),
                   jax.ShapeDtypeStruct((nt, bsz, 1, sw_t), F32)],
        grid=(nt, bsz // bb),
        in_specs=[pl.BlockSpec((rows, LANES), lambda i, j: (j, CD_U // LANES + i)),
                  pl.BlockSpec((tc, None, LANES, C_GROUP), lambda i, j: (0, i, 0, 0)),
                  pl.BlockSpec((tc, None, LANES, sw), lambda i, j: (0, i, 0, 0)),
                  pl.BlockSpec((None, sw_t, tc * C_GROUP), lambda i, j: (i, 0, 0)),
                  pl.BlockSpec((None, apow.shape[1], sw_t), lambda i, j: (i, 0, 0)),
                  pl.BlockSpec((None, bb, 1, sw_t), lambda i, j: (i, j, 0, 0))],
        out_specs=[pl.BlockSpec((rows, LANES), lambda i, j: (j, i)),
                   pl.BlockSpec((None, bb, 1, sw_t), lambda i, j: (i, j, 0, 0))],
        scratch_shapes=[pltpu.VMEM((w_t, 2 * LANES), BF16),
                        pltpu.VMEM((w_t, sw_t), BF16),
                        pltpu.VMEM((sw_t, w_t), BF16)],
        compiler_params=pltpu.CompilerParams(
            dimension_semantics=("parallel", "arbitrary"),
            vmem_limit_bytes=_vmem_limit(blocks, scratch + temps)),
        name="s5_chunks",
    )(proj2d, kc, bc, cc, apow, x0)


def _s5_operators(lam_re, lam_im, log_dt, b_re, b_im, c_re, c_im, n_log):
    g, p = lam_re.shape
    tc = S5_CHUNK
    dt = jnp.exp(log_dt.astype(F32))[:, None]
    mag = jnp.exp(lam_re * dt)
    ab_re, ab_im = mag * jnp.cos(lam_im * dt), mag * jnp.sin(lam_im * dt)
    den = lam_re * lam_re + lam_im * lam_im
    er = ab_re - 1.0
    zr = (er * lam_re + ab_im * lam_im) / den
    zi = (ab_im * lam_re - er * lam_im) / den
    b_re_t, b_im_t = b_re.swapaxes(1, 2), b_im.swapaxes(1, 2)
    bb_re = zr[:, None, :] * b_re_t - zi[:, None, :] * b_im_t
    bb_im = zr[:, None, :] * b_im_t + zi[:, None, :] * b_re_t
    pw_re, pw_im = ab_re[None], ab_im[None]
    while pw_re.shape[0] < tc:
        top_re, top_im = pw_re[-1], pw_im[-1]
        pw_re, pw_im = (jnp.concatenate([pw_re, top_re * pw_re - top_im * pw_im]),
                        jnp.concatenate([pw_im, top_re * pw_im + top_im * pw_re]))
    pw_re = jnp.concatenate([jnp.ones_like(ab_re)[None], pw_re])
    pw_im = jnp.concatenate([jnp.zeros_like(ab_im)[None], pw_im])
    rv_re, rv_im = pw_re[tc - 1::-1, :, None, :], pw_im[tc - 1::-1, :, None, :]
    abr = rv_re * bb_re - rv_im * bb_im
    abi = rv_re * bb_im + rv_im * bb_re
    ab = jnp.concatenate([abr, abi], axis=3)
    kern = jnp.einsum('gjq,tgiq->tgij', jnp.concatenate([c_re, -c_im], axis=2), ab, precision=HIGHEST)
    gt = S5_GROUP_BLOCK
    nt = g // gt
    sw = 2 * p
    kc = kern.reshape(tc, nt, gt * C_GROUP, C_GROUP)
    bc = ab.reshape(tc, nt, gt * C_GROUP, sw)
    c_re_t, c_im_t = c_re.swapaxes(1, 2)[:, :, None, :], c_im.swapaxes(1, 2)[:, :, None, :]
    up_re = pw_re[1:].transpose(1, 2, 0)[..., None]
    up_im = pw_im[1:].transpose(1, 2, 0)[..., None]
    cr = c_re_t * up_re - c_im_t * up_im
    ci = -(c_re_t * up_im + c_im_t * up_re)
    cc = jnp.concatenate([cr, ci], axis=1).reshape(nt, gt * sw, tc * C_GROUP)
    r, i = pw_re[tc], pw_im[tc]
    rows = []
    for _ in range(max(n_log, 1)):
        rows += [jnp.concatenate([r, r], -1), jnp.concatenate([-i, i], -1)]
        r, i = r * r - i * i, 2.0 * r * i
    apow = jnp.stack(rows, axis=1)
    apow = apow.reshape(nt, gt, -1, sw).transpose(0, 2, 1, 3).reshape(nt, -1, gt * sw)
    return kc.astype(BF16), bc.astype(BF16), cc.astype(BF16), apow


def _s5(proj2d, ops, x0_re, x0_im, *, l):
    kc, bc, cc, apow = ops
    bsz = x0_re.shape[0]
    nt = C_GROUPS // S5_GROUP_BLOCK
    x0 = jnp.concatenate([x0_re, x0_im], axis=-1).reshape(bsz, nt, 1, -1).transpose(1, 0, 2, 3)
    bb = max(s for s in range(1, bsz + 1) if bsz % s == 0 and s * l <= S5_MAX_ROWS)
    y, xf = _s5_chunks(proj2d, kc, bc, cc, apow, x0, l=l, bb=bb)
    xf = xf.transpose(1, 0, 2, 3).reshape(bsz, C_GROUPS, 2 * C_STATE)
    return y, xf[..., :C_STATE], xf[..., C_STATE:]


AB_SRC_GA = AB_QKV
AB_SRC_QKV = AB_SRC_GA + A_GATE_RANK
AB_SRC_TAIL = AB_SRC_QKV + B_QKV + B_VW
IN_AB = AB_SRC_TAIL + 2 * B_HEADS
IN_CD = CD_SMALL + 2 * D_HEADS
WPREP_AB_TILE = 512
WPREP_CD_TILE = 640


def _prep_w_ab_kernel(wt_ref, ga_ref, o_ref):
    j = pl.program_id(0)
    tile = o_ref.shape[1]
    n_small = A_GATE_RANK + 2 * B_HEADS

    @pl.when(j < AB_SMALL // tile)
    def _():
        o_ref[...] = wt_ref[...].T.astype(BF16)

    @pl.when(j == AB_SMALL // tile)
    def _():
        rows = jnp.concatenate([ga_ref[...], wt_ref[tile - 2 * B_HEADS:tile, :],
                                jnp.zeros((tile - n_small, wt_ref.shape[1]), F32)], axis=0)
        o_ref[...] = rows.T.astype(BF16)

    @pl.when(j > AB_SMALL // tile)
    def _():
        o_ref[...] = jnp.zeros(o_ref.shape, BF16)


def _prep_w_ab(w_t, n_out):
    n_in, d = w_t.shape
    tile = WPREP_AB_TILE
    assert AB_QKV % tile == 0 and AB_SMALL % tile == 0 and n_out % tile == 0 and n_in >= tile

    unit = A_GATE_RANK
    assert tile % unit == 0 and (AB_SRC_QKV - AB_QKV) % unit == 0 and (n_in - tile) % unit == 0

    def src_row(j):
        k = j * (tile // unit)
        k = jnp.where(j < AB_QKV // tile, k,
                      jnp.where(j < AB_SMALL // tile, k + (AB_SRC_QKV - AB_QKV) // unit, (n_in - tile) // unit))
        return unit * k

    return pl.pallas_call(
        _prep_w_ab_kernel,
        out_shape=jax.ShapeDtypeStruct((d, n_out), BF16),
        grid=(n_out // tile,),
        in_specs=[pl.BlockSpec((pl.Element(tile), pl.Element(d)), lambda j: (src_row(j), 0)),
                  pl.BlockSpec((pl.Element(A_GATE_RANK), pl.Element(d)), lambda j: (AB_SRC_GA, 0))],
        out_specs=pl.BlockSpec((d, tile), lambda j: (0, j)),
        compiler_params=pltpu.CompilerParams(
            dimension_semantics=("parallel",),
            vmem_limit_bytes=_vmem_limit(tile * d * 6, 2 * tile * d * 4)),
        name="prep_w_in_ab",
    )(w_t, w_t)


def _prep_w_cd_kernel(wt_ref, o_ref):
    tile = o_ref.shape[1]
    row = pl.program_id(0) * tile + lax.broadcasted_iota(jnp.int32, wt_ref.shape, 0)
    o_ref[...] = jnp.where(row < IN_CD, wt_ref[...], 0.0).T.astype(BF16)


def _prep_w_cd(w_t, n_out):
    n_in, d = w_t.shape
    tile = WPREP_CD_TILE
    assert n_out % tile == 0
    return pl.pallas_call(
        _prep_w_cd_kernel,
        out_shape=jax.ShapeDtypeStruct((d, n_out), BF16),
        grid=(n_out // tile,),
        in_specs=[pl.BlockSpec((tile, d), lambda j: (j, 0))],
        out_specs=pl.BlockSpec((d, tile), lambda j: (0, j)),
        compiler_params=pltpu.CompilerParams(
            dimension_semantics=("parallel",),
            vmem_limit_bytes=_vmem_limit(tile * d * 6, 2 * tile * d * 4)),
        name="prep_w_in_cd",
    )(w_t)


def _lane_row(vals, lane0):
    return jnp.zeros((1, LANES), F32).at[0, lane0:lane0 + vals.shape[0]].set(vals.astype(F32))


def _prepare_weights(norm_g, final_norm_g, w_in_ab, a_gate_w, a_gate_b, a_norm_g, b_conv_w, b_a_log,
                     b_dt_bias, b_norm_g, w_out_ab, w_in_cd, c_lam_re, c_lam_im, c_log_dt, c_b_re,
                     c_b_im, c_c_re, c_c_im, c_d, c_glu_w, c_glu_b, d_i_bias, d_f_bias, d_norm_g,
                     w_out_cd, n_log):
    assert w_in_ab.shape[1] == IN_AB and w_in_cd.shape[1] == IN_CD
    w_ab = _prep_w_ab(w_in_ab.astype(F32).T, _round_up(AB_SMALL + LANES, PROJ_TN))
    w_cd = _prep_w_cd(w_in_cd.astype(F32).T, _round_up(CD_SMALL + LANES, PROJ_TN))
    gate_w = jnp.zeros((LANES, A_KW), F32).at[AB_GA_LANE:AB_GA_LANE + A_GATE_RANK].set(
        a_gate_w.astype(F32)).astype(BF16)
    return dict(
        norm_g=norm_g.astype(F32), final_g=final_norm_g.astype(F32)[None, :],
        w_ab=w_ab, w_cd=w_cd, gate_w=gate_w, gate_b=a_gate_b.astype(F32)[None, :],
        a_norm_g=a_norm_g.astype(F32)[None, :], conv_w=b_conv_w.astype(F32),
        alog=_lane_row(b_a_log, AB_APRE_LANE), dtb=_lane_row(b_dt_bias, AB_APRE_LANE),
        b_norm_g=b_norm_g.astype(F32)[None, :],
        w_out_ab=w_out_ab.astype(BF16),
        s5_ops=_s5_operators(c_lam_re.astype(F32), c_lam_im.astype(F32), c_log_dt, c_b_re.astype(F32),
                             c_b_im.astype(F32), c_c_re.astype(F32), c_c_im.astype(F32), n_log),
        c_d=c_d.astype(F32).reshape(1, C_W), glu_w=c_glu_w.astype(BF16),
        glu_b=c_glu_b.astype(F32)[None, :],
        ib=_lane_row(d_i_bias, CD_I_LANE), fb=_lane_row(d_f_bias, CD_F_LANE),
        d_norm_g=d_norm_g.astype(F32)[None, :],
        w_out_cd=w_out_cd.astype(BF16),
    )


def _trunk(x, conv_prev, s_gla0, s_gdn0, s5_re0, s5_im0, mc0, mn0, mm0, w):
    bsz, l, d = x.shape
    c = min(CHUNK, l)
    bb = max(s for s in (1, 2, 4) if s <= MIXER_STREAMS and bsz % s == 0)
    tb_s = min(l, (8 // bb) * c)
    n_s5 = l // S5_CHUNK
    assert l % tb_s == 0 and l % S5_CHUNK == 0 and n_s5 & (n_s5 - 1) == 0
    x2d = x.reshape(bsz * l, d)

    proj = _norm_matmul(x2d, w['norm_g'][0:1], w['w_ab'])
    proj3 = proj.reshape(bsz, l, proj.shape[1])
    o_a, s_gla = _gla(proj3, w['gate_w'], w['gate_b'], w['a_norm_g'], s_gla0.astype(F32), c=c, tb=tb_s, bb=bb)
    o_b, s_gdn = _gdn(proj3, conv_prev.astype(F32), w['conv_w'], w['alog'], w['dtb'], w['b_norm_g'],
                      s_gdn0.astype(F32), c=c, tb=tb_s, bb=bb)
    conv_new = proj3[:, l - (B_CONV - 1):, AB_QKV:AB_QKV + B_QKV]
    h1 = _out_proj(o_a.reshape(bsz * l, A_VW), o_b.reshape(bsz * l, B_VW), w['w_out_ab'], x2d)

    proj = _norm_matmul(h1, w['norm_g'][1:2], w['w_cd'])
    proj3 = proj.reshape(bsz, l, proj.shape[1])
    y, s5_re, s5_im = _s5(proj, w['s5_ops'], s5_re0.astype(F32), s5_im0.astype(F32), l=l)
    o_d, mc, mn, mm = _mlstm(proj3, w['ib'], w['fb'], w['d_norm_g'], mc0.astype(F32),
                             mn0.astype(F32), mm0.astype(F32)[:, None, :], c=c, tb=tb_s, bb=bb)
    y_out = _glu_out_proj_norm(y, proj, o_d.reshape(bsz * l, D_VW), w['c_d'], w['glu_w'], w['glu_b'],
                               w['w_out_cd'], h1, w['final_g'])
    dt = x.dtype
    return (y_out.reshape(bsz, l, d).astype(dt), conv_new.astype(dt), s_gla.astype(dt), s_gdn.astype(dt),
            s5_re.astype(dt), s5_im.astype(dt), mc.astype(dt), mn.astype(dt), mm[:, 0, :].astype(dt))


def kernel(x_prompt, x_sample, cache_gdn_conv, state_gla, state_gdn, state_s5_re, state_s5_im,
           state_mlstm_c, state_mlstm_n, state_mlstm_m, norm_g, final_norm_g, w_in_ab, a_gate_w,
           a_gate_b, a_norm_g, b_conv_w, b_a_log, b_dt_bias, b_norm_g, w_out_ab, w_in_cd, c_lam_re,
           c_lam_im, c_log_dt, c_b_re, c_b_im, c_c_re, c_c_im, c_d, c_glu_w, c_glu_b, d_i_bias,
           d_f_bias, d_norm_g, w_out_cd):
    n_log = int(math.log2(max(x_prompt.shape[1], x_sample.shape[1]) // S5_CHUNK))
    w = _prepare_weights(norm_g, final_norm_g, w_in_ab, a_gate_w, a_gate_b, a_norm_g, b_conv_w, b_a_log,
                         b_dt_bias, b_norm_g, w_out_ab, w_in_cd, c_lam_re, c_lam_im, c_log_dt, c_b_re,
                         c_b_im, c_c_re, c_c_im, c_d, c_glu_w, c_glu_b, d_i_bias, d_f_bias, d_norm_g,
                         w_out_cd, n_log)
    nb = x_prompt.shape[0]
    zeros = lambda *shape: jnp.zeros(shape, F32)
    p_out = _trunk(x_prompt, zeros(nb, B_CONV - 1, B_QKV), zeros(nb, A_HEADS, A_DK, A_DV),
                   zeros(nb, B_HEADS, B_DK, B_DV), zeros(nb, C_GROUPS, C_STATE), zeros(nb, C_GROUPS, C_STATE),
                   zeros(nb, D_HEADS, D_DK, D_DV), zeros(nb, D_HEADS, D_DK), zeros(nb, D_HEADS), w)
    s_out = _trunk(x_sample, cache_gdn_conv, state_gla, state_gdn, state_s5_re, state_s5_im,
                   state_mlstm_c, state_mlstm_n, state_mlstm_m, w)
    return (p_out[0], s_out[0]) + tuple(p_out[1:]) + tuple(s_out[1:])
```

```python
import functools
import math

import jax
import jax.numpy as jnp
from jax import lax
from jax.experimental import pallas as pl
from jax.experimental.pallas import tpu as pltpu

F32 = jnp.float32
BF16 = jnp.bfloat16
HIGHEST = lax.Precision.HIGHEST

NORM_EPS = 1e-6
CHUNK = 64
A_HEADS, A_DK, A_DV, A_GATE_RANK, A_GATE_TAU = 4, 128, 256, 16, 16.0
B_HEADS, B_DK, B_DV, B_CONV = 8, 128, 128, 4
C_GROUP, C_GROUPS, C_STATE = 16, 64, 64
D_HEADS, D_DK, D_DV = 4, 128, 256
A_KW, A_VW = A_HEADS * A_DK, A_HEADS * A_DV
B_KW, B_VW = B_HEADS * B_DK, B_HEADS * B_DV
B_QKV = 2 * B_KW + B_VW
C_W = C_GROUPS * C_GROUP
D_KW, D_VW = D_HEADS * D_DK, D_HEADS * D_DV

LANES = 128
SUBLANES = 8
VMEM_BYTES_V7X = 64 * 1024 * 1024

AB_Q, AB_K, AB_V, AB_Z = 0, A_KW, 2 * A_KW, 2 * A_KW + A_VW
AB_QKV = AB_Z + A_VW
AB_ZB = AB_QKV + B_QKV
AB_SMALL = AB_ZB + B_VW
AB_GA_LANE, AB_BETA_LANE, AB_APRE_LANE = 0, A_GATE_RANK, A_GATE_RANK + B_HEADS
CD_U, CD_Z = 0, C_W
CD_Q = 2 * C_W
CD_K = CD_Q + D_KW
CD_V = CD_K + D_KW
CD_O = CD_V + D_VW
CD_ZD = CD_O + D_VW
CD_SMALL = CD_ZD + D_VW
CD_I_LANE, CD_F_LANE = 0, D_HEADS

PROJ_TN = 1280
S5_CHUNK = 16
S5_GROUP_BLOCK = 8
S5_MAX_ROWS = 8192
NORM_ROW_SPLITS = 4
GLU_ROW_SPLITS = 2
MIXER_STREAMS = 4
MLSTM_INTRA_CHUNKS = 4
GDN_WY_CHAINS = 32


def _round_up(x, m):
    return (x + m - 1) // m * m


def _vmem_limit(block_bytes, scratch_bytes=0):
    est = 2 * block_bytes + scratch_bytes
    return int(min(max(2 * est, 32 * 1024 * 1024), VMEM_BYTES_V7X - 8 * 1024 * 1024))


def _mm(a, b):
    return jnp.dot(a, b, preferred_element_type=F32)


def _dot(a, b):
    return _mm(a.astype(BF16), b.astype(BF16))


def _dot_nt(a, b):
    return lax.dot_general(a.astype(BF16), b.astype(BF16), (((1,), (1,)), ((), ())),
                           preferred_element_type=F32)


def _dot_tn(a, b):
    return lax.dot_general(a.astype(BF16), b.astype(BF16), (((0,), (0,)), ((), ())),
                           preferred_element_type=F32)


def _split2(x):
    hi = x.astype(BF16)
    return hi, (x - hi.astype(F32)).astype(BF16)


def _split3(x):
    hi = x.astype(BF16)
    r = x - hi.astype(F32)
    mid = r.astype(BF16)
    return hi, mid, (r - mid.astype(F32)).astype(BF16)


def _cumsum_rows(tri_bf16, x):
    hi, mid, lo = _split3(x)
    return _mm(tri_bf16, hi) + _mm(tri_bf16, mid) + _mm(tri_bf16, lo)


def _select_rows(sel_bf16, x):
    nt = lambda b: lax.dot_general(sel_bf16, b, (((1,), (1,)), ((), ())), preferred_element_type=F32)
    hi, mid, lo = _split3(x)
    return nt(hi) + nt(mid) + nt(lo)


def _lane_selector(lane0):
    r = lax.broadcasted_iota(jnp.int32, (SUBLANES, LANES), 0)
    l = lax.broadcasted_iota(jnp.int32, (SUBLANES, LANES), 1)
    return jnp.where(l == r + lane0, 1.0, 0.0).astype(BF16)


def _causal_masks(c):
    row = lax.broadcasted_iota(jnp.int32, (c, c), 0)
    col = lax.broadcasted_iota(jnp.int32, (c, c), 1)
    return row >= col, row > col


def _norm_matmul_kernel(x_ref, g_ref, w_ref, o_ref, xn_ref):
    j = pl.program_id(1)

    @pl.when(j == 0)
    def _():
        tm = x_ref.shape[0]
        sub = tm // NORM_ROW_SPLITS
        for i in range(NORM_ROW_SPLITS):
            s = slice(i * sub, (i + 1) * sub)
            x = x_ref[s, :]
            y = (x * lax.rsqrt(jnp.mean(x * x, axis=-1, keepdims=True) + NORM_EPS) * g_ref[...]).astype(BF16)
            xn_ref[s, :] = y
            o_ref[s, :] = jnp.dot(y, w_ref[...], preferred_element_type=F32)

    @pl.when(j != 0)
    def _():
        o_ref[...] = jnp.dot(xn_ref[...], w_ref[...], preferred_element_type=F32)


def _norm_matmul(x2d, g_row, w_bf16):
    m, d = x2d.shape
    n = w_bf16.shape[1]
    tm = min(m, 1024)
    tn = PROJ_TN
    assert m % tm == 0 and n % tn == 0
    blocks = tm * d * 4 + d * tn * 2 + tm * tn * 4
    return pl.pallas_call(
        _norm_matmul_kernel,
        out_shape=jax.ShapeDtypeStruct((m, n), F32),
        grid=(m // tm, n // tn),
        in_specs=[pl.BlockSpec((tm, d), lambda i, j: (i, 0)),
                  pl.BlockSpec((1, d), lambda i, j: (0, 0)),
                  pl.BlockSpec((d, tn), lambda i, j: (0, j))],
        out_specs=pl.BlockSpec((tm, tn), lambda i, j: (i, j)),
        scratch_shapes=[pltpu.VMEM((tm, d), BF16)],
        compiler_params=pltpu.CompilerParams(
            dimension_semantics=("parallel", "arbitrary"),
            vmem_limit_bytes=_vmem_limit(blocks, tm * d * 2)),
        name="norm_in_proj",
    )(x2d, g_row, w_bf16)


def _out_proj_kernel(a_ref, b_ref, wa_ref, wb_ref, h_ref, o_ref):
    out = (jnp.dot(a_ref[...], wa_ref[...], preferred_element_type=F32)
           + jnp.dot(b_ref[...], wb_ref[...], preferred_element_type=F32))
    o_ref[...] = h_ref[...] + out


def _glu_out_proj_norm_kernel(y_ref, u_ref, z_ref, od_ref, d_ref, gw_ref, gb_ref, wc_ref, wd_ref, h_ref, g_ref,
                              o_ref):
    tm = y_ref.shape[0]
    sub = tm // GLU_ROW_SPLITS
    halves = [slice(i * sub, (i + 1) * sub) for i in range(GLU_ROW_SPLITS)]
    y = [jax.nn.gelu(y_ref[s, :] + d_ref[...] * u_ref[s, :]) for s in halves]
    gate = [jax.nn.sigmoid(jnp.dot(v.astype(BF16), gw_ref[...], preferred_element_type=F32) + gb_ref[...])
            for v in y]
    o_c = [(y[i] * gate[i] * jax.nn.silu(z_ref[s, :])).astype(BF16) for i, s in enumerate(halves)]
    out = [jnp.dot(o_c[i], wc_ref[...], preferred_element_type=F32)
           + jnp.dot(od_ref[s, :], wd_ref[...], preferred_element_type=F32) for i, s in enumerate(halves)]
    for i, s in enumerate(halves):
        h = h_ref[s, :] + out[i]
        o_ref[s, :] = h * lax.rsqrt(jnp.mean(h * h, axis=-1, keepdims=True) + NORM_EPS) * g_ref[...]


def _glu_out_proj_norm(y2d, proj2d, o_d, d_row, glu_w, glu_b_row, w_out, h2d, final_g_row):
    m, d = h2d.shape
    tm = min(m, 512)
    assert m % tm == 0 and C_W == D_VW and w_out.shape[0] == C_W + D_VW
    row = lambda col: (lambda i: (i, col))
    const = lambda i: (0, 0)
    resident = pl.Buffered(1)
    blocks = 3 * tm * C_W * 4 + tm * D_VW * 2 + 2 * tm * d * 4
    weights = C_W * C_W * 2 + (C_W + D_VW) * d * 2
    return pl.pallas_call(
        _glu_out_proj_norm_kernel,
        out_shape=jax.ShapeDtypeStruct((m, d), F32),
        grid=(m // tm,),
        in_specs=[pl.BlockSpec((tm, C_W), row(0)),
                  pl.BlockSpec((tm, C_W), row(CD_U // C_W)),
                  pl.BlockSpec((tm, C_W), row(CD_Z // C_W)),
                  pl.BlockSpec((tm, D_VW), row(0)),
                  pl.BlockSpec((1, C_W), const),
                  pl.BlockSpec((C_W, C_W), const, pipeline_mode=resident),
                  pl.BlockSpec((1, C_W), const),
                  pl.BlockSpec((C_W, d), const, pipeline_mode=resident),
                  pl.BlockSpec((D_VW, d), lambda i: (1, 0), pipeline_mode=resident),
                  pl.BlockSpec((tm, d), row(0)),
                  pl.BlockSpec((1, d), const)],
        out_specs=pl.BlockSpec((tm, d), row(0)),
        compiler_params=pltpu.CompilerParams(
            dimension_semantics=("parallel",),
            vmem_limit_bytes=_vmem_limit(blocks, weights + 3 * tm * C_W * 4)),
        name="glu_out_proj_norm",
    )(y2d, proj2d, proj2d, o_d, d_row, glu_w, glu_b_row, w_out, w_out, h2d, final_g_row)


def _out_proj(mix_a, mix_b, w_out, h2d):
    m, d = h2d.shape
    ka, kb = mix_a.shape[1], mix_b.shape[1]
    tm = min(m, 512)
    assert m % tm == 0 and ka == kb and w_out.shape[0] == ka + kb
    blocks = tm * (ka + kb) * 2 + (ka + kb) * d * 2 + 2 * tm * d * 4
    return pl.pallas_call(
        _out_proj_kernel,
        out_shape=jax.ShapeDtypeStruct((m, d), F32),
        grid=(m // tm,),
        in_specs=[pl.BlockSpec((tm, ka), lambda i: (i, 0)),
                  pl.BlockSpec((tm, kb), lambda i: (i, 0)),
                  pl.BlockSpec((ka, d), lambda i: (0, 0)),
                  pl.BlockSpec((kb, d), lambda i: (1, 0)),
                  pl.BlockSpec((tm, d), lambda i: (i, 0))],
        out_specs=pl.BlockSpec((tm, d), lambda i: (i, 0)),
        compiler_params=pltpu.CompilerParams(
            dimension_semantics=("parallel",),
            vmem_limit_bytes=_vmem_limit(blocks)),
        name="out_proj",
    )(mix_a, mix_b, w_out, w_out, h2d)


def _gla_kernel(q_ref, k_ref, v_ref, z_ref, sm_ref, gw_ref, gb_ref, ng_ref, s0_ref,
                o_ref, sout_ref, st_ref, *, c, nchunks):
    t = pl.program_id(1)
    last_t = pl.num_programs(1) - 1
    bb = q_ref.shape[0]
    chains = [(bi, h) for bi in range(bb) for h in range(A_HEADS)]
    ksl = [slice(h * A_DK, (h + 1) * A_DK) for _, h in chains]
    vsl = [slice(h * A_DV, (h + 1) * A_DV) for _, h in chains]
    idx = range(len(chains))

    @pl.when(t == 0)
    def _():
        for bi, h in chains:
            st_ref[bi, h] = s0_ref[bi, h].T

    causal, _ = _causal_masks(c)
    tri = causal.astype(BF16)
    gw = gw_ref[...]
    gb = gb_ref[...]

    def body(n, carry):
        sl = pl.ds(pl.multiple_of(n * c, c), c)
        b_all = [_cumsum_rows(tri, jax.nn.log_sigmoid(_dot(sm_ref[bi, sl, :], gw) + gb) * (1.0 / A_GATE_TAU))
                 for bi in range(bb)]
        b = [b_all[chains[i][0]][:, ksl[i]] for i in idx]
        b_last = [b[i][c - 1:c, :] for i in idx]
        k = [k_ref[chains[i][0], sl, ksl[i]] for i in idx]
        v = [v_ref[chains[i][0], sl, vsl[i]].astype(BF16) for i in idx]
        q_dec = [(q_ref[chains[i][0], sl, ksl[i]] * (A_DK ** -0.5) * jnp.exp(b[i])).astype(BF16) for i in idx]
        k_dec = [(k[i] * jnp.exp(-b[i])).astype(BF16) for i in idx]
        k_w = [(k[i] * jnp.exp(b_last[i] - b[i])).astype(BF16) for i in idx]
        scores = [jnp.where(causal, _dot_nt(q_dec[i], k_dec[i]), 0.0).astype(BF16) for i in idx]
        s_t = [st_ref[bi, h] for bi, h in chains]
        outs = [_mm(scores[i], v[i]) + _dot_nt(q_dec[i], s_t[i]) for i in idx]
        for i, (bi, h) in enumerate(chains):
            st_ref[bi, h] = s_t[i] * jnp.exp(b_last[i]) + _dot_tn(v[i], k_w[i])
        for i, (bi, h) in enumerate(chains):
            o = outs[i]
            o = o * lax.rsqrt(jnp.mean(o * o, axis=-1, keepdims=True) + NORM_EPS) * ng_ref[:, vsl[i]]
            o_ref[bi, sl, vsl[i]] = (o * jax.nn.silu(z_ref[bi, sl, vsl[i]])).astype(o_ref.dtype)
        return carry

    lax.fori_loop(0, nchunks, body, 0)

    @pl.when(t == last_t)
    def _():
        for bi, h in chains:
            sout_ref[bi, h] = st_ref[bi, h].T


def _gla(proj3, gate_w_pad, gate_b_row, norm_g_row, s0, *, c, tb, bb):
    bsz, l, _ = proj3.shape
    nblk = l // tb
    assert bsz % bb == 0
    tok = lambda col: (lambda b, t: (b, t, col))
    in_specs = [
        pl.BlockSpec((bb, tb, A_KW), tok(AB_Q // A_KW)),
        pl.BlockSpec((bb, tb, A_KW), tok(AB_K // A_KW)),
        pl.BlockSpec((bb, tb, A_VW), tok(AB_V // A_VW)),
        pl.BlockSpec((bb, tb, A_VW), tok(AB_Z // A_VW)),
        pl.BlockSpec((bb, tb, LANES), tok(AB_SMALL // LANES)),
        pl.BlockSpec((LANES, A_KW), lambda b, t: (0, 0)),
        pl.BlockSpec((1, A_KW), lambda b, t: (0, 0)),
        pl.BlockSpec((1, A_VW), lambda b, t: (0, 0)),
        pl.BlockSpec((bb, A_HEADS, A_DK, A_DV), lambda b, t: (b, 0, 0, 0)),
    ]
    out_specs = [
        pl.BlockSpec((bb, tb, A_VW), lambda b, t: (b, t, 0)),
        pl.BlockSpec((bb, A_HEADS, A_DK, A_DV), lambda b, t: (b, 0, 0, 0)),
    ]
    state = bb * A_HEADS * A_DK * A_DV * 4
    blocks = bb * tb * (2 * A_KW + 2 * A_VW + LANES) * 4 + bb * tb * A_VW * 2 + 2 * state
    return pl.pallas_call(
        functools.partial(_gla_kernel, c=c, nchunks=tb // c),
        out_shape=[jax.ShapeDtypeStruct((bsz, l, A_VW), BF16),
                   jax.ShapeDtypeStruct((bsz, A_HEADS, A_DK, A_DV), F32)],
        grid=(bsz // bb, nblk),
        in_specs=in_specs,
        out_specs=out_specs,
        scratch_shapes=[pltpu.VMEM((bb, A_HEADS, A_DV, A_DK), F32)],
        compiler_params=pltpu.CompilerParams(
            dimension_semantics=("parallel", "arbitrary"),
            vmem_limit_bytes=_vmem_limit(blocks, state)),
        name="gla_mixer",
    )(proj3, proj3, proj3, proj3, proj3, gate_w_pad, gate_b_row, norm_g_row, s0)


def _gdn_kernel(x_ref, z_ref, sm_ref, w_ref, cp_ref, alog_ref, dtb_ref, ng_ref, s0_ref,
                o_ref, sout_ref, s_ref, tail_ref, u_ref, wm_ref, qg_ref, kg_ref, qk_ref, gl_ref,
                *, c, nchunks):
    t = pl.program_id(1)
    last_t = pl.num_programs(1) - 1
    tb = c * nchunks
    keep = SUBLANES - (B_CONV - 1)
    bb = x_ref.shape[0]

    @pl.when(t == 0)
    def _():
        s_ref[...] = s0_ref[...]
        tail_ref[:, 0:keep, :] = jnp.zeros((bb, keep, B_QKV), F32)
        tail_ref[:, keep:SUBLANES, :] = cp_ref[...]

    def conv_silu(bi, row0, first, cols):
        x = x_ref[bi, pl.ds(row0, c), cols]
        if first is None:
            prev = x_ref[bi, pl.ds(pl.multiple_of(row0 - SUBLANES, SUBLANES), SUBLANES), cols]
        else:
            before = pl.multiple_of(jnp.maximum(row0 - SUBLANES, 0), SUBLANES)
            prev = jnp.where(first, tail_ref[bi, :, cols], x_ref[bi, pl.ds(before, SUBLANES), cols])
        w = w_ref[:, cols]
        ext = jnp.concatenate([prev, x], axis=0)
        ext1 = pltpu.roll(ext, 1, 0)
        newer = ext * w[3:4, :] + ext1 * w[2:3, :]
        older = ext * w[1:2, :] + ext1 * w[0:1, :]
        conv = newer + pltpu.roll(older, 2, 0)
        return jax.nn.silu(conv[SUBLANES:SUBLANES + c, :])

    causal, strict = _causal_masks(c)
    tri = causal.astype(BF16)
    eye = jnp.where(causal & jnp.logical_not(strict), 1.0, 0.0).astype(F32)
    sel = _lane_selector(AB_APRE_LANE)
    neg_a_exp = -jnp.exp(alog_ref[...])
    dtb = dtb_ref[...]
    ng = ng_ref[...]
    n_double = int(math.log2(c)) - 1
    heads = range(B_HEADS)

    group = max(1, GDN_WY_CHAINS // (bb * B_HEADS))
    group = group if nchunks % group == 0 else 1

    def wy_factors(n, carry):
        pws, rhss, where = [], [], []
        for bi, ci in [(bi, ci) for ci in range(group) for bi in range(bb)]:
            row0 = pl.multiple_of((n * group + ci) * c, c)
            first = (n == 0) if ci == 0 else None
            sl = pl.ds(row0, c)
            sm = sm_ref[bi, sl, :]
            g_cum = _cumsum_rows(tri, neg_a_exp * jax.nn.softplus(sm + dtb))
            g_rows = _select_rows(sel, g_cum)
            beta_all = jax.nn.sigmoid(sm)
            gl_ref[bi, pl.ds(n * group + ci, 1), :] = g_cum[c - 1:c, :]
            for h in heads:
                hs = slice(h * B_DK, (h + 1) * B_DK)
                q = conv_silu(bi, row0, first, hs)
                k = conv_silu(bi, row0, first, slice(B_KW + h * B_DK, B_KW + (h + 1) * B_DK))
                v = conv_silu(bi, row0, first, slice(2 * B_KW + h * B_DV, 2 * B_KW + (h + 1) * B_DV))
                q = q * lax.rsqrt(jnp.sum(q * q, axis=-1, keepdims=True) + NORM_EPS) * (B_DK ** -0.5)
                k = k * lax.rsqrt(jnp.sum(k * k, axis=-1, keepdims=True) + NORM_EPS)
                g_col = g_cum[:, AB_APRE_LANE + h:AB_APRE_LANE + h + 1]
                beta = beta_all[:, AB_BETA_LANE + h:AB_BETA_LANE + h + 1]
                decay = jnp.exp(jnp.where(causal, g_col - g_rows[h:h + 1, :], -jnp.inf))
                e_g = jnp.exp(g_col)
                k_beta = k * beta
                pws.append(-jnp.where(strict, _dot_nt(k_beta, k) * decay, 0.0))
                rhss.append(jnp.concatenate([v * beta, k_beta * e_g], axis=1))
                where.append((bi, sl, hs))
                qk_ref[bi, sl, h * LANES:h * LANES + c] = jnp.where(causal, _dot_nt(q, k) * decay,
                                                                    0.0).astype(BF16)
                qg_ref[bi, sl, hs] = (q * e_g).astype(BF16)
                kg_ref[bi, sl, hs] = (k * jnp.exp(g_col[c - 1:c, :] - g_col)).astype(BF16)
        items = range(len(pws))
        neg_lower = [_split2(p) for p in pws]
        invs = [eye + p for p in pws]
        for _ in range(n_double):
            pws = [_dot(p, p) for p in pws]
            invs = [i + _dot(i, p) for i, p in zip(invs, pws)]
        invs = [i.astype(BF16) for i in invs]
        sol = [_mm(invs[i], rhss[i].astype(BF16)).astype(BF16) for i in items]
        resid = [rhss[i] - sol[i].astype(F32) + _mm(neg_lower[i][0], sol[i]) + _mm(neg_lower[i][1], sol[i])
                 for i in items]
        for i in items:
            bi, sl, hs = where[i]
            uw = sol[i].astype(F32) + _mm(invs[i], resid[i].astype(BF16))
            u_ref[bi, sl, hs] = uw[:, :B_DV]
            wm_ref[bi, sl, hs] = uw[:, B_DV:].astype(BF16)
        return carry

    lax.fori_loop(0, nchunks // group, wy_factors, 0)
    tail_ref[...] = x_ref[:, tb - SUBLANES:tb, :]

    chains = [(bi, h) for bi in range(bb) for h in heads]
    hsl = [slice(h * B_DK, (h + 1) * B_DK) for _, h in chains]

    def recurrence(n, carry):
        sl = pl.ds(pl.multiple_of(n * c, c), c)
        e_last = [jnp.exp(gl_ref[bi, pl.ds(n, 1), :]) for bi in range(bb)]
        s_old = [s_ref[bi, h] for bi, h in chains]
        s_bf = [s.astype(BF16) for s in s_old]
        v_new = [(u_ref[bi, sl, hsl[i]] - _mm(wm_ref[bi, sl, hsl[i]], s_bf[i])).astype(BF16)
                 for i, (bi, h) in enumerate(chains)]
        outs = [_mm(qg_ref[bi, sl, hsl[i]], s_bf[i]) + _mm(qk_ref[bi, sl, h * LANES:h * LANES + c], v_new[i])
                for i, (bi, h) in enumerate(chains)]
        for i, (bi, h) in enumerate(chains):
            s_ref[bi, h] = (e_last[bi][:, AB_APRE_LANE + h:AB_APRE_LANE + h + 1] * s_old[i]
                            + lax.dot_general(kg_ref[bi, sl, hsl[i]], v_new[i], (((0,), (0,)), ((), ())),
                                              preferred_element_type=F32))
        for i, (bi, h) in enumerate(chains):
            o = outs[i]
            o = o * lax.rsqrt(jnp.mean(o * o, axis=-1, keepdims=True) + NORM_EPS) * ng
            o_ref[bi, sl, hsl[i]] = (o * jax.nn.silu(z_ref[bi, sl, hsl[i]])).astype(o_ref.dtype)
        return carry

    lax.fori_loop(0, nchunks, recurrence, 0)

    @pl.when(t == last_t)
    def _():
        sout_ref[...] = s_ref[...]


def _gdn(proj3, conv_prev, conv_w, alog_row, dtb_row, norm_g_row, s0, *, c, tb, bb):
    bsz, l, _ = proj3.shape
    nblk = l // tb
    assert AB_QKV % B_QKV == 0 and AB_ZB % B_VW == 0 and bsz % bb == 0
    in_specs = [
        pl.BlockSpec((bb, tb, B_QKV), lambda b, t: (b, t, AB_QKV // B_QKV)),
        pl.BlockSpec((bb, tb, B_VW), lambda b, t: (b, t, AB_ZB // B_VW)),
        pl.BlockSpec((bb, tb, LANES), lambda b, t: (b, t, AB_SMALL // LANES)),
        pl.BlockSpec((B_CONV, B_QKV), lambda b, t: (0, 0)),
        pl.BlockSpec((bb, B_CONV - 1, B_QKV), lambda b, t: (b, 0, 0)),
        pl.BlockSpec((1, LANES), lambda b, t: (0, 0)),
        pl.BlockSpec((1, LANES), lambda b, t: (0, 0)),
        pl.BlockSpec((1, B_DV), lambda b, t: (0, 0)),
        pl.BlockSpec((bb, B_HEADS, B_DK, B_DV), lambda b, t: (b, 0, 0, 0)),
    ]
    out_specs = [
        pl.BlockSpec((bb, tb, B_VW), lambda b, t: (b, t, 0)),
        pl.BlockSpec((bb, B_HEADS, B_DK, B_DV), lambda b, t: (b, 0, 0, 0)),
    ]
    blocks = bb * (tb * (B_QKV + B_VW + LANES) * 4 + tb * B_VW * 2 + 2 * B_HEADS * B_DK * B_DV * 4)
    scratch = bb * (B_HEADS * B_DK * B_DV * 4 + SUBLANES * B_QKV * 4 + tb * B_VW * 4
                    + 4 * tb * B_KW * 2 + SUBLANES * LANES * 4)
    return pl.pallas_call(
        functools.partial(_gdn_kernel, c=c, nchunks=tb // c),
        out_shape=[jax.ShapeDtypeStruct((bsz, l, B_VW), BF16),
                   jax.ShapeDtypeStruct((bsz, B_HEADS, B_DK, B_DV), F32)],
        grid=(bsz // bb, nblk),
        in_specs=in_specs,
        out_specs=out_specs,
        scratch_shapes=[pltpu.VMEM((bb, B_HEADS, B_DK, B_DV), F32),
                        pltpu.VMEM((bb, SUBLANES, B_QKV), F32),
                        pltpu.VMEM((bb, tb, B_VW), F32),
                        pltpu.VMEM((bb, tb, B_KW), BF16),
                        pltpu.VMEM((bb, tb, B_KW), BF16),
                        pltpu.VMEM((bb, tb, B_KW), BF16),
                        pltpu.VMEM((bb, tb, B_HEADS * LANES), BF16),
                        pltpu.VMEM((bb, SUBLANES, LANES), F32)],
        compiler_params=pltpu.CompilerParams(
            dimension_semantics=("parallel", "arbitrary"),
            vmem_limit_bytes=_vmem_limit(blocks, scratch)),
        name="gdn_mixer",
    )(proj3, proj3, proj3, conv_w, conv_prev, alog_row, dtb_row, norm_g_row, s0)


def _mlstm_kernel(q_ref, k_ref, v_ref, og_ref, z_ref, sm_ref, ib_ref, fb_ref, ng_ref,
                  c0_ref, n0_ref, m0_ref, o_ref, cout_ref, nout_ref, mout_ref,
                  c_ref, n_ref, m_ref, hi_ref, mi_ref, ni_ref, bc_ref, kv_ref, ks_ref, mc_ref, bl_ref,
                  *, c, nchunks):
    t = pl.program_id(1)
    last_t = pl.num_programs(1) - 1
    bb = q_ref.shape[0]
    chains = [(bi, h) for bi in range(bb) for h in range(D_HEADS)]
    ksl = [slice(h * D_DK, (h + 1) * D_DK) for _, h in chains]
    vsl = [slice(h * D_DV, (h + 1) * D_DV) for _, h in chains]
    idx = range(len(chains))

    @pl.when(t == 0)
    def _():
        c_ref[...] = c0_ref[...]
        n_ref[...] = n0_ref[...]
        for bi, h in chains:
            m_ref[bi, h] = jnp.broadcast_to(m0_ref[bi, :, h:h + 1], (1, LANES))

    causal, _ = _causal_masks(c)
    tri = causal.astype(BF16)
    sel = _lane_selector(0)
    lane = lax.broadcasted_iota(jnp.int32, (c, LANES), 1)
    ib = ib_ref[...]
    fb = fb_ref[...]

    group = MLSTM_INTRA_CHUNKS if nchunks % MLSTM_INTRA_CHUNKS == 0 else 1

    def intra(n, carry):
        items = [(bi, ci, h) for ci in range(group) for bi, h in chains]
        sls = [pl.ds(pl.multiple_of((n * group + ci) * c, c), c) for ci in range(group)]
        sm = {(bi, ci): sm_ref[bi, sls[ci], :] for ci in range(group) for bi in range(bb)}
        i_full = {key: x + ib for key, x in sm.items()}
        b_full = {key: _cumsum_rows(tri, jax.nn.log_sigmoid(x + fb)) for key, x in sm.items()}
        rows = {key: _select_rows(sel, jnp.where(lane < CD_F_LANE, i_full[key], b_full[key]))
                for key in sm}
        b_col = [b_full[bi, ci][:, CD_F_LANE + h:CD_F_LANE + h + 1] for bi, ci, h in items]
        i_col = [i_full[bi, ci][:, CD_I_LANE + h:CD_I_LANE + h + 1] for bi, ci, h in items]
        logw = [jnp.where(causal, b_col[i] - rows[bi, ci][CD_F_LANE + h:CD_F_LANE + h + 1, :]
                          + rows[bi, ci][CD_I_LANE + h:CD_I_LANE + h + 1, :], -jnp.inf)
                for i, (bi, ci, h) in enumerate(items)]
        ids = range(len(items))
        m_intra = [jnp.max(logw[i], axis=-1, keepdims=True) for i in ids]
        ks_ = [slice(h * D_DK, (h + 1) * D_DK) for _, _, h in items]
        vs_ = [slice(h * D_DV, (h + 1) * D_DV) for _, _, h in items]
        q_bf = [(q_ref[bi, sls[ci], ks_[i]] * (D_DK ** -0.5)).astype(BF16) for i, (bi, ci, h) in enumerate(items)]
        k = [k_ref[bi, sls[ci], ks_[i]] for i, (bi, ci, h) in enumerate(items)]
        v = [v_ref[bi, sls[ci], vs_[i]].astype(BF16) for i, (bi, ci, h) in enumerate(items)]
        p = [jnp.exp(logw[i] - m_intra[i]) * _dot_nt(q_bf[i], k[i]) for i in ids]
        n_intra = [jnp.sum(p[i], axis=-1, keepdims=True) for i in ids]
        m_chunk = [m_intra[i][c - 1:c, :] for i in ids]
        k_w = [k[i] * jnp.exp(b_col[i][c - 1:c, :] - b_col[i] + i_col[i] - m_chunk[i]) for i in ids]
        rep = lambda x: jnp.broadcast_to(x, (x.shape[0], LANES))
        for i, (bi, ci, h) in enumerate(items):
            row = pl.ds(n * group + ci, 1)
            hi_ref[bi, sls[ci], vs_[i]] = _dot(p[i], v[i])
            kv_ref[bi, n * group + ci, h] = _dot_tn(k_w[i], v[i])
            ks_ref[bi, h, row, :] = jnp.sum(k_w[i], axis=0, keepdims=True)
            mi_ref[bi, h, sls[ci], :] = rep(m_intra[i])
            ni_ref[bi, h, sls[ci], :] = rep(n_intra[i])
            bc_ref[bi, h, sls[ci], :] = rep(b_col[i])
            mc_ref[bi, h, row, :] = rep(m_chunk[i])
            bl_ref[bi, h, row, :] = rep(b_col[i][c - 1:c, :])
        return carry

    lax.fori_loop(0, nchunks // group, intra, 0)

    twice = lambda x: jnp.concatenate([x, x], axis=1)

    def body(n, carry):
        sl = pl.ds(pl.multiple_of(n * c, c), c)
        row = pl.ds(n, 1)
        q = [q_ref[chains[i][0], sl, ksl[i]] * (D_DK ** -0.5) for i in idx]
        c_s = [c_ref[bi, h] for bi, h in chains]
        n_s = [n_ref[bi, h:h + 1, :] for bi, h in chains]
        m_s = [m_ref[bi, h] for bi, h in chains]
        qc = [_dot(q[i], c_s[i]) for i in idx]
        b_last = [bl_ref[bi, h, row, :] for bi, h in chains]
        m_chunk = [mc_ref[bi, h, row, :] for bi, h in chains]
        for i, (bi, h) in enumerate(chains):
            m_new = jnp.maximum(b_last[i] + m_s[i], m_chunk[i])
            w_old = jnp.exp(b_last[i] + m_s[i] - m_new)
            w_new = jnp.exp(m_chunk[i] - m_new)
            c_ref[bi, h] = twice(w_old) * c_s[i] + twice(w_new) * kv_ref[bi, n, h]
            n_ref[bi, h:h + 1, :] = w_old * n_s[i] + w_new * ks_ref[bi, h, row, :]
            m_ref[bi, h] = m_new
        for i, (bi, h) in enumerate(chains):
            m_intra = mi_ref[bi, h, sl, :]
            a = bc_ref[bi, h, sl, :] + m_s[i]
            m_t = jnp.maximum(a, m_intra)
            w_a = jnp.exp(a - m_t)
            w_i = jnp.exp(m_intra - m_t)
            num = twice(w_a) * qc[i] + twice(w_i) * hi_ref[bi, sl, vsl[i]]
            den = w_a * jnp.sum(q[i] * n_s[i], axis=-1, keepdims=True) + w_i * ni_ref[bi, h, sl, :]
            hh = num / twice(jnp.maximum(jnp.abs(den), jnp.exp(-m_t)))
            hd = jax.nn.sigmoid(og_ref[bi, sl, vsl[i]]) * hh
            oc = hd - jnp.mean(hd, axis=-1, keepdims=True)
            o = oc * lax.rsqrt(jnp.mean(oc * oc, axis=-1, keepdims=True) + NORM_EPS) * ng_ref[:, vsl[i]]
            o_ref[bi, sl, vsl[i]] = (o * jax.nn.silu(z_ref[bi, sl, vsl[i]])).astype(o_ref.dtype)
        return carry

    lax.fori_loop(0, nchunks, body, 0)

    @pl.when(t == last_t)
    def _():
        cout_ref[...] = c_ref[...]
        nout_ref[...] = n_ref[...]
        for bi, h in chains:
            mout_ref[bi, :, h:h + 1] = m_ref[bi, h][:, 0:1]


def _mlstm(proj3, ib_row, fb_row, norm_g_row, c0, n0, m0, *, c, tb, bb):
    bsz, l, _ = proj3.shape
    nblk = l // tb
    assert bsz % bb == 0
    tok = lambda col: (lambda b, t: (b, t, col))
    st4 = lambda b, t: (b, 0, 0, 0)
    st3 = lambda b, t: (b, 0, 0)
    in_specs = [
        pl.BlockSpec((bb, tb, D_KW), tok(CD_Q // D_KW)),
        pl.BlockSpec((bb, tb, D_KW), tok(CD_K // D_KW)),
        pl.BlockSpec((bb, tb, D_VW), tok(CD_V // D_VW)),
        pl.BlockSpec((bb, tb, D_VW), tok(CD_O // D_VW)),
        pl.BlockSpec((bb, tb, D_VW), tok(CD_ZD // D_VW)),
        pl.BlockSpec((bb, tb, LANES), tok(CD_SMALL // LANES)),
        pl.BlockSpec((1, LANES), lambda b, t: (0, 0)),
        pl.BlockSpec((1, LANES), lambda b, t: (0, 0)),
        pl.BlockSpec((1, D_VW), lambda b, t: (0, 0)),
        pl.BlockSpec((bb, D_HEADS, D_DK, D_DV), st4),
        pl.BlockSpec((bb, D_HEADS, D_DK), st3),
        pl.BlockSpec((bb, 1, D_HEADS), st3),
    ]
    out_specs = [
        pl.BlockSpec((bb, tb, D_VW), lambda b, t: (b, t, 0)),
        pl.BlockSpec((bb, D_HEADS, D_DK, D_DV), st4),
        pl.BlockSpec((bb, D_HEADS, D_DK), st3),
        pl.BlockSpec((bb, 1, D_HEADS), st3),
    ]
    state = bb * D_HEADS * D_DK * D_DV * 4
    blocks = bb * tb * (2 * D_KW + 3 * D_VW + LANES) * 4 + bb * tb * D_VW * 2 + 2 * state
    nchunks = tb // c
    return pl.pallas_call(
        functools.partial(_mlstm_kernel, c=c, nchunks=nchunks),
        out_shape=[jax.ShapeDtypeStruct((bsz, l, D_VW), BF16),
                   jax.ShapeDtypeStruct((bsz, D_HEADS, D_DK, D_DV), F32),
                   jax.ShapeDtypeStruct((bsz, D_HEADS, D_DK), F32),
                   jax.ShapeDtypeStruct((bsz, 1, D_HEADS), F32)],
        grid=(bsz // bb, nblk),
        in_specs=in_specs,
        out_specs=out_specs,
        scratch_shapes=[pltpu.VMEM((bb, D_HEADS, D_DK, D_DV), F32),
                        pltpu.VMEM((bb, D_HEADS, D_DK), F32),
                        pltpu.VMEM((bb, D_HEADS, 1, LANES), F32),
                        pltpu.VMEM((bb, tb, D_VW), F32),
                        pltpu.VMEM((bb, D_HEADS, tb, LANES), F32),
                        pltpu.VMEM((bb, D_HEADS, tb, LANES), F32),
                        pltpu.VMEM((bb, D_HEADS, tb, LANES), F32),
                        pltpu.VMEM((bb, nchunks, D_HEADS, D_DK, D_DV), F32),
                        pltpu.VMEM((bb, D_HEADS, _round_up(nchunks, SUBLANES), D_DK), F32),
                        pltpu.VMEM((bb, D_HEADS, _round_up(nchunks, SUBLANES), LANES), F32),
                        pltpu.VMEM((bb, D_HEADS, _round_up(nchunks, SUBLANES), LANES), F32)],
        compiler_params=pltpu.CompilerParams(
            dimension_semantics=("parallel", "arbitrary"),
            vmem_limit_bytes=_vmem_limit(blocks, state * (1 + nchunks) + bb * tb * (D_VW + 2 * LANES) * 4)),
        name="mlstm_mixer",
    )(proj3, proj3, proj3, proj3, proj3, proj3, ib_row, fb_row, norm_g_row, c0, n0, m0)


def _s5_expand_operators(kc_ref, bc_ref, cc_ref, bd_ref, bst_ref, cst_ref):
    tc, gt, sw, cg = S5_CHUNK, S5_GROUP_BLOCK, 2 * C_STATE, C_GROUP
    w_t = tc * LANES
    iota = lambda shape, d: lax.broadcasted_iota(jnp.int32, shape, d)
    row_g = (iota((w_t, LANES), 0) // cg) % gt
    tile16 = jnp.where(iota((cg, LANES), 1) % cg == iota((cg, LANES), 0), 1.0, 0.0).astype(BF16)
    bd = jnp.where(row_g == iota((w_t, LANES), 1) // cg, _mm(kc_ref[...].reshape(w_t, cg), tile16),
                   0.0).astype(BF16)
    bd_ref[:, LANES:2 * LANES] = bd
    bd_ref[0:w_t - LANES, 0:LANES] = bd[LANES:, :]
    bd_ref[w_t - LANES:w_t, 0:LANES] = jnp.zeros((LANES, LANES), BF16)
    bc = bc_ref[...].reshape(w_t, sw).astype(F32)
    for g in range(gt):
        bst_ref[:, g * sw:(g + 1) * sw] = jnp.where(row_g == g, bc, 0.0).astype(BF16)
    src, dst = iota((tc * cg, w_t), 0), iota((tc * cg, w_t), 1)
    spread = jnp.where((src // cg == dst // LANES) & (src % cg == dst % cg), 1.0, 0.0).astype(BF16)
    lane_g = (iota((sw, w_t), 1) // cg) % gt
    for g in range(gt):
        full = _mm(cc_ref[g * sw:(g + 1) * sw, :], spread)
        cst_ref[g * sw:(g + 1) * sw, :] = jnp.where(lane_g == g, full, 0.0).astype(BF16)


def _s5_expand(kc, bc, cc):
    tc = S5_CHUNK
    nt = kc.shape[1]
    sw = 2 * C_STATE
    sw_t = S5_GROUP_BLOCK * sw
    w_t = tc * LANES
    assert S5_GROUP_BLOCK * C_GROUP == LANES
    blocks = 2 * w_t * LANES * 2 + sw_t * tc * C_GROUP * 2 + w_t * 2 * LANES * 2 + 2 * w_t * sw_t * 2
    temps = 4 * w_t * LANES * 4 + 4 * sw * w_t * 4
    return pl.pallas_call(
        _s5_expand_operators,
        out_shape=[jax.ShapeDtypeStruct((nt, w_t, 2 * LANES), BF16),
                   jax.ShapeDtypeStruct((nt, w_t, sw_t), BF16),
                   jax.ShapeDtypeStruct((nt, sw_t, w_t), BF16)],
        grid=(nt,),
        in_specs=[pl.BlockSpec((tc, None, LANES, C_GROUP), lambda i: (0, i, 0, 0)),
                  pl.BlockSpec((tc, None, LANES, sw), lambda i: (0, i, 0, 0)),
                  pl.BlockSpec((None, sw_t, tc * C_GROUP), lambda i: (i, 0, 0))],
        out_specs=[pl.BlockSpec((None, w_t, 2 * LANES), lambda i: (i, 0, 0)),
                   pl.BlockSpec((None, w_t, sw_t), lambda i: (i, 0, 0)),
                   pl.BlockSpec((None, sw_t, w_t), lambda i: (i, 0, 0))],
        compiler_params=pltpu.CompilerParams(
            dimension_semantics=("parallel",), vmem_limit_bytes=_vmem_limit(blocks, temps)),
        name="s5_expand",
    )(kc, bc, cc)


def _s5_kernel(u_ref, bd_ref, bst_ref, cst_ref, apow_ref, x0_ref, y_ref, xf_ref, *, nc, bb):
    m = nc * bb
    tc = S5_CHUNK
    sw = 2 * C_STATE
    row =lax.broadcasted_iota(jnp.int32, (m, sw), 0)
    n_idx = row & (nc - 1)
    n_log = int(math.log2(nc))

    def cmul(a1, a2, x):
        return a1 * x + a2 * pltpu.roll(x, C_STATE, 1)

    lhs = jnp.concatenate([u_ref[pl.ds(tau, m, stride=tc), :].astype(BF16) for tau in range(tc)], axis=1)
    e_all = _mm(lhs, bst_ref[...])
    groups = range(S5_GROUP_BLOCK)
    gsl = [slice(g * sw, (g + 1) * sw) for g in groups]
    x0_rows = []
    for g in groups:
        rows0 = jnp.zeros((m, sw), F32)
        for b in range(bb):
            rows0 = jnp.where(row == b * nc, x0_ref[b, :, gsl[g]], rows0)
        x0_rows.append(rows0)
    x = [e_all[:, gsl[g]] + cmul(apow_ref[0:1, gsl[g]], apow_ref[1:2, gsl[g]], x0_rows[g]) for g in groups]
    for j in range(n_log):
        sh = 1 << j
        shifted = [jnp.where(n_idx >= sh, pltpu.roll(x[g], sh, 0), 0.0) for g in groups]
        x = [x[g] + cmul(apow_ref[2 * j:2 * j + 1, gsl[g]], apow_ref[2 * j + 1:2 * j + 2, gsl[g]], shifted[g])
             for g in groups]
    x_start = [jnp.where(n_idx >= 1, pltpu.roll(x[g], 1, 0), x0_rows[g]).astype(BF16) for g in groups]
    for g in groups:
        for b in range(bb):
            xf_ref[b, :, gsl[g]] = x[g][b * nc + nc - 1:b * nc + nc, :]
    y_state = _mm(jnp.concatenate(x_start, axis=1), cst_ref[...])
    for tau in range(0, tc, 2):
        width = (tau + 2) * LANES
        start = (tc - 2 - tau) * LANES
        y = _mm(lhs[:, :width], bd_ref[start:start + width, :]) + y_state[:, tau * LANES:(tau + 2) * LANES]
        y_ref[pl.ds(tau, m, stride=tc), :] = y[:, :LANES]
        y_ref[pl.ds(tau + 1, m, stride=tc), :] = y[:, LANES:]


def _s5_chunks(proj2d, bd, bst, cst, apow, x0, *, l, bb):
    tc = S5_CHUNK
    nc = l // tc
    bsz = x0.shape[1]
    nt = C_GROUPS // S5_GROUP_BLOCK
    rows = bb * l
    sw = 2 * C_STATE
    sw_t = S5_GROUP_BLOCK * sw
    w_t = tc * LANES
    assert CD_U % LANES == 0 and bsz % bb == 0
    blocks = (2 * rows * LANES * 4 + w_t * 2 * LANES * 2 + 2 * w_t * sw_t * 2 + apow.shape[1] * sw_t * 4
              + 2 * bb * SUBLANES * sw_t * 4)
    temps = bb * nc * (w_t * 2 + w_t * 4 + 3 * sw_t * 4)
    return pl.pallas_call(
        functools.partial(_s5_kernel, nc=nc, bb=bb),
        out_shape=[jax.ShapeDtypeStruct((bsz * l, C_W), F32),
                   jax.ShapeDtypeStruct((nt, bsz, 1, sw_t), F32)],
        grid=(nt, bsz // bb),
        in_specs=[pl.BlockSpec((rows, LANES), lambda i, j: (j, CD_U // LANES + i)),
                  pl.BlockSpec((None, w_t, 2 * LANES), lambda i, j: (i, 0, 0)),
                  pl.BlockSpec((None, w_t, sw_t), lambda i, j: (i, 0, 0)),
                  pl.BlockSpec((None, sw_t, w_t), lambda i, j: (i, 0, 0)),
                  pl.BlockSpec((None, apow.shape[1], sw_t), lambda i, j: (i, 0, 0)),
                  pl.BlockSpec((None, bb, 1, sw_t), lambda i, j: (i, j, 0, 0))],
        out_specs=[pl.BlockSpec((rows, LANES), lambda i, j: (j, i)),
                   pl.BlockSpec((None, bb, 1, sw_t), lambda i, j: (i, j, 0, 0))],
        compiler_params=pltpu.CompilerParams(
            dimension_semantics=("parallel", "parallel"),
            vmem_limit_bytes=_vmem_limit(blocks, temps)),
        name="s5_chunks",
    )(proj2d, bd, bst, cst, apow, x0)


def _s5_operators(lam_re, lam_im, log_dt, b_re, b_im, c_re, c_im, n_log):
    g, p = lam_re.shape
    tc = S5_CHUNK
    dt = jnp.exp(log_dt.astype(F32))[:, None]
    mag = jnp.exp(lam_re * dt)
    ab_re, ab_im = mag * jnp.cos(lam_im * dt), mag * jnp.sin(lam_im * dt)
    den = lam_re * lam_re + lam_im * lam_im
    er = ab_re - 1.0
    zr = (er * lam_re + ab_im * lam_im) / den
    zi = (ab_im * lam_re - er * lam_im) / den
    b_re_t, b_im_t = b_re.swapaxes(1, 2), b_im.swapaxes(1, 2)
    bb_re = zr[:, None, :] * b_re_t - zi[:, None, :] * b_im_t
    bb_im = zr[:, None, :] * b_im_t + zi[:, None, :] * b_re_t
    pw_re, pw_im = ab_re[None], ab_im[None]
    while pw_re.shape[0] < tc:
        top_re, top_im = pw_re[-1], pw_im[-1]
        pw_re, pw_im = (jnp.concatenate([pw_re, top_re * pw_re - top_im * pw_im]),
                        jnp.concatenate([pw_im, top_re * pw_im + top_im * pw_re]))
    pw_re = jnp.concatenate([jnp.ones_like(ab_re)[None], pw_re])
    pw_im = jnp.concatenate([jnp.zeros_like(ab_im)[None], pw_im])
    rv_re, rv_im = pw_re[tc - 1::-1, :, None, :], pw_im[tc - 1::-1, :, None, :]
    abr = rv_re * bb_re - rv_im * bb_im
    abi = rv_re * bb_im + rv_im * bb_re
    kern = (jnp.einsum('gjp,tgip->tgij', c_re, abr, precision=HIGHEST)
            - jnp.einsum('gjp,tgip->tgij', c_im, abi, precision=HIGHEST))
    gt = S5_GROUP_BLOCK
    nt = g // gt
    sw = 2 * p
    kc = kern.reshape(tc, nt, gt * C_GROUP, C_GROUP)
    bc = jnp.concatenate([abr, abi], axis=3).reshape(tc, nt, gt * C_GROUP, sw)
    c_re_t, c_im_t = c_re.swapaxes(1, 2)[:, :, None, :], c_im.swapaxes(1, 2)[:, :, None, :]
    up_re = pw_re[1:].transpose(1, 2, 0)[..., None]
    up_im = pw_im[1:].transpose(1, 2, 0)[..., None]
    cr = c_re_t * up_re - c_im_t * up_im
    ci = -(c_re_t * up_im + c_im_t * up_re)
    cc = jnp.concatenate([cr, ci], axis=1).reshape(nt, gt * sw, tc * C_GROUP)
    r, i = pw_re[tc], pw_im[tc]
    rows = []
    for _ in range(max(n_log, 1)):
        rows += [jnp.concatenate([r, r], -1), jnp.concatenate([-i, i], -1)]
        r, i = r * r - i * i, 2.0 * r * i
    apow = jnp.stack(rows, axis=1)
    apow = apow.reshape(nt, gt, -1, sw).transpose(0, 2, 1, 3).reshape(nt, -1, gt * sw)
    return tuple(_s5_expand(kc.astype(BF16), bc.astype(BF16), cc.astype(BF16))) + (apow,)


def _s5(proj2d, ops, x0_re, x0_im, *, l):
    bd, bst, cst, apow = ops
    bsz = x0_re.shape[0]
    nt = C_GROUPS // S5_GROUP_BLOCK
    x0 = jnp.concatenate([x0_re, x0_im], axis=-1).reshape(bsz, nt, 1, -1).transpose(1, 0, 2, 3)
    bb = max(s for s in range(1, bsz + 1) if bsz % s == 0 and s * l <= S5_MAX_ROWS)
    y, xf = _s5_chunks(proj2d, bd, bst, cst, apow, x0, l=l, bb=bb)
    xf = xf.transpose(1, 0, 2, 3).reshape(bsz, C_GROUPS, 2 * C_STATE)
    return y, xf[..., :C_STATE], xf[..., C_STATE:]


AB_SRC_GA = AB_QKV
AB_SRC_QKV = AB_SRC_GA + A_GATE_RANK
AB_SRC_TAIL = AB_SRC_QKV + B_QKV + B_VW
IN_AB = AB_SRC_TAIL + 2 * B_HEADS
IN_CD = CD_SMALL + 2 * D_HEADS
WPREP_AB_TILE = 512
WPREP_CD_TILE = 640


def _prep_w_ab_kernel(wt_ref, ga_ref, o_ref):
    j = pl.program_id(0)
    tile = o_ref.shape[1]
    n_small = A_GATE_RANK + 2 * B_HEADS

    @pl.when(j < AB_SMALL // tile)
    def _():
        o_ref[...] = wt_ref[...].T.astype(BF16)

    @pl.when(j == AB_SMALL // tile)
    def _():
        rows = jnp.concatenate([ga_ref[...], wt_ref[tile - 2 * B_HEADS:tile, :],
                                jnp.zeros((tile - n_small, wt_ref.shape[1]), F32)], axis=0)
        o_ref[...] = rows.T.astype(BF16)

    @pl.when(j > AB_SMALL // tile)
    def _():
        o_ref[...] = jnp.zeros(o_ref.shape, BF16)


def _prep_w_ab(w_t, n_out):
    n_in, d = w_t.shape
    tile = WPREP_AB_TILE
    assert AB_QKV % tile == 0 and AB_SMALL % tile == 0 and n_out % tile == 0 and n_in >= tile

    unit = A_GATE_RANK
    assert tile % unit == 0 and (AB_SRC_QKV - AB_QKV) % unit == 0 and (n_in - tile) % unit == 0

    def src_row(j):
        k = j * (tile // unit)
        k = jnp.where(j < AB_QKV // tile, k,
                      jnp.where(j < AB_SMALL // tile, k + (AB_SRC_QKV - AB_QKV) // unit, (n_in - tile) // unit))
        return unit * k

    return pl.pallas_call(
        _prep_w_ab_kernel,
        out_shape=jax.ShapeDtypeStruct((d, n_out), BF16),
        grid=(n_out // tile,),
        in_specs=[pl.BlockSpec((pl.Element(tile), pl.Element(d)), lambda j: (src_row(j), 0)),
                  pl.BlockSpec((pl.Element(A_GATE_RANK), pl.Element(d)), lambda j: (AB_SRC_GA, 0))],
        out_specs=pl.BlockSpec((d, tile), lambda j: (0, j)),
        compiler_params=pltpu.CompilerParams(
            dimension_semantics=("parallel",),
            vmem_limit_bytes=_vmem_limit(tile * d * 6, 2 * tile * d * 4)),
        name="prep_w_in_ab",
    )(w_t, w_t)


def _prep_w_cd_kernel(wt_ref, o_ref):
    tile = o_ref.shape[1]
    row = pl.program_id(0) * tile + lax.broadcasted_iota(jnp.int32, wt_ref.shape, 0)
    o_ref[...] = jnp.where(row < IN_CD, wt_ref[...], 0.0).T.astype(BF16)


def _prep_w_cd(w_t, n_out):
    n_in, d = w_t.shape
    tile = WPREP_CD_TILE
    assert n_out % tile == 0
    return pl.pallas_call(
        _prep_w_cd_kernel,
        out_shape=jax.ShapeDtypeStruct((d, n_out), BF16),
        grid=(n_out // tile,),
        in_specs=[pl.BlockSpec((tile, d), lambda j: (j, 0))],
        out_specs=pl.BlockSpec((d, tile), lambda j: (0, j)),
        compiler_params=pltpu.CompilerParams(
            dimension_semantics=("parallel",),
            vmem_limit_bytes=_vmem_limit(tile * d * 6, 2 * tile * d * 4)),
        name="prep_w_in_cd",
    )(w_t)


def _lane_row(vals, lane0):
    return jnp.zeros((1, LANES), F32).at[0, lane0:lane0 + vals.shape[0]].set(vals.astype(F32))


def _prepare_weights(norm_g, final_norm_g, w_in_ab, a_gate_w, a_gate_b, a_norm_g, b_conv_w, b_a_log,
                     b_dt_bias, b_norm_g, w_out_ab, w_in_cd, c_lam_re, c_lam_im, c_log_dt, c_b_re,
                     c_b_im, c_c_re, c_c_im, c_d, c_glu_w, c_glu_b, d_i_bias, d_f_bias, d_norm_g,
                     w_out_cd, n_log):
    assert w_in_ab.shape[1] == IN_AB and w_in_cd.shape[1] == IN_CD
    w_ab = _prep_w_ab(w_in_ab.astype(F32).T, _round_up(AB_SMALL + LANES, PROJ_TN))
    w_cd = _prep_w_cd(w_in_cd.astype(F32).T, _round_up(CD_SMALL + LANES, PROJ_TN))
    gate_w = jnp.zeros((LANES, A_KW), F32).at[AB_GA_LANE:AB_GA_LANE + A_GATE_RANK].set(
        a_gate_w.astype(F32)).astype(BF16)
    return dict(
        norm_g=norm_g.astype(F32), final_g=final_norm_g.astype(F32)[None, :],
        w_ab=w_ab, w_cd=w_cd, gate_w=gate_w, gate_b=a_gate_b.astype(F32)[None, :],
        a_norm_g=a_norm_g.astype(F32)[None, :], conv_w=b_conv_w.astype(F32),
        alog=_lane_row(b_a_log, AB_APRE_LANE), dtb=_lane_row(b_dt_bias, AB_APRE_LANE),
        b_norm_g=b_norm_g.astype(F32)[None, :],
        w_out_ab=w_out_ab.astype(BF16),
        s5_ops=_s5_operators(c_lam_re.astype(F32), c_lam_im.astype(F32), c_log_dt, c_b_re.astype(F32),
                             c_b_im.astype(F32), c_c_re.astype(F32), c_c_im.astype(F32), n_log),
        c_d=c_d.astype(F32).reshape(1, C_W), glu_w=c_glu_w.astype(BF16),
        glu_b=c_glu_b.astype(F32)[None, :],
        ib=_lane_row(d_i_bias, CD_I_LANE), fb=_lane_row(d_f_bias, CD_F_LANE),
        d_norm_g=d_norm_g.astype(F32)[None, :],
        w_out_cd=w_out_cd.astype(BF16),
    )


def _trunk(x, conv_prev, s_gla0, s_gdn0, s5_re0, s5_im0, mc0, mn0, mm0, w):
    bsz, l, d = x.shape
    c = min(CHUNK, l)
    bb = max(s for s in (1, 2, 4) if s <= MIXER_STREAMS and bsz % s == 0)
    tb_s = min(l, (8 // bb) * c)
    n_s5 = l // S5_CHUNK
    assert l % tb_s == 0 and l % S5_CHUNK == 0 and n_s5 & (n_s5 - 1) == 0
    x2d = x.reshape(bsz * l, d)

    proj = _norm_matmul(x2d, w['norm_g'][0:1], w['w_ab'])
    proj3 = proj.reshape(bsz, l, proj.shape[1])
    o_a, s_gla = _gla(proj3, w['gate_w'], w['gate_b'], w['a_norm_g'], s_gla0.astype(F32), c=c, tb=tb_s, bb=bb)
    o_b, s_gdn = _gdn(proj3, conv_prev.astype(F32), w['conv_w'], w['alog'], w['dtb'], w['b_norm_g'],
                      s_gdn0.astype(F32), c=c, tb=tb_s, bb=bb)
    conv_new = proj3[:, l - (B_CONV - 1):, AB_QKV:AB_QKV + B_QKV]
    h1 = _out_proj(o_a.reshape(bsz * l, A_VW), o_b.reshape(bsz * l, B_VW), w['w_out_ab'], x2d)

    proj = _norm_matmul(h1, w['norm_g'][1:2], w['w_cd'])
    proj3 = proj.reshape(bsz, l, proj.shape[1])
    y, s5_re, s5_im = _s5(proj, w['s5_ops'], s5_re0.astype(F32), s5_im0.astype(F32), l=l)
    o_d, mc, mn, mm = _mlstm(proj3, w['ib'], w['fb'], w['d_norm_g'], mc0.astype(F32),
                             mn0.astype(F32), mm0.astype(F32)[:, None, :], c=c, tb=tb_s, bb=bb)
    y_out = _glu_out_proj_norm(y, proj, o_d.reshape(bsz * l, D_VW), w['c_d'], w['glu_w'], w['glu_b'],
                               w['w_out_cd'], h1, w['final_g'])
    dt = x.dtype
    return (y_out.reshape(bsz, l, d).astype(dt), conv_new.astype(dt), s_gla.astype(dt), s_gdn.astype(dt),
            s5_re.astype(dt), s5_im.astype(dt), mc.astype(dt), mn.astype(dt), mm[:, 0, :].astype(dt))


def kernel(x_prompt, x_sample, cache_gdn_conv, state_gla, state_gdn, state_s5_re, state_s5_im,
           state_mlstm_c, state_mlstm_n, state_mlstm_m, norm_g, final_norm_g, w_in_ab, a_gate_w,
           a_gate_b, a_norm_g, b_conv_w, b_a_log, b_dt_bias, b_norm_g, w_out_ab, w_in_cd, c_lam_re,
           c_lam_im, c_log_dt, c_b_re, c_b_im, c_c_re, c_c_im, c_d, c_glu_w, c_glu_b, d_i_bias,
           d_f_bias, d_norm_g, w_out_cd):
    n_log = int(math.log2(max(x_prompt.shape[1], x_sample.shape[1]) // S5_CHUNK))
    w = _prepare_weights(norm_g, final_norm_g, w_in_ab, a_gate_w, a_gate_b, a_norm_g, b_conv_w, b_a_log,
                         b_dt_bias, b_norm_g, w_out_ab, w_in_cd, c_lam_re, c_lam_im, c_log_dt, c_b_re,
                         c_b_im, c_c_re, c_c_im, c_d, c_glu_w, c_glu_b, d_i_bias, d_f_bias, d_norm_g,
                         w_out_cd, n_log)
    nb = x_prompt.shape[0]
    zeros = lambda *shape: jnp.zeros(shape, F32)
    p_out = _trunk(x_prompt, zeros(nb, B_CONV - 1, B_QKV), zeros(nb, A_HEADS, A_DK, A_DV),
                   zeros(nb, B_HEADS, B_DK, B_DV), zeros(nb, C_GROUPS, C_STATE), zeros(nb, C_GROUPS, C_STATE),
                   zeros(nb, D_HEADS, D_DK, D_DV), zeros(nb, D_HEADS, D_DK), zeros(nb, D_HEADS), w)
    s_out = _trunk(x_sample, cache_gdn_conv, state_gla, state_gdn, state_s5_re, state_s5_im,
                   state_mlstm_c, state_mlstm_n, state_mlstm_m, w)
    return (p_out[0], s_out[0]) + tuple(p_out[1:]) + tuple(s_out[1:])
```
